```python
import numpy as np
import jax
import jax.numpy as jnp
from jax import lax

D_MODEL = 1024
BATCH = 8
SEQ = 2048
DEPTH = 4

N_EVEN = (DEPTH + 1) // 2
N_ODD = DEPTH // 2
ALPHA = (2 * DEPTH) ** 0.25
BETA = (8 * DEPTH) ** -0.25
LN_EPS = 1e-5
CONV_DIM = D_MODEL // 2
CONV_WIDTH = 3
SSM_D_INNER = D_MODEL
SSM_HEAD_DIM = 64
SSM_HEADS = SSM_D_INNER // SSM_HEAD_DIM
SSM_GROUPS = 4
SSM_STATE = 64
SSM_CONV = 4
SSM_CONV_DIM = SSM_D_INNER + 2 * SSM_GROUPS * SSM_STATE
SSD_CHUNK = 128
MLSTM_HEADS = 4
MLSTM_HEAD_DIM = D_MODEL // 8
MLSTM_W = MLSTM_HEADS * MLSTM_HEAD_DIM
MLSTM_CHUNK = 64
FOX_HEADS = 8
FOX_HEAD_DIM = D_MODEL // 16
FOX_W = FOX_HEADS * FOX_HEAD_DIM
Q_BLOCK = 128
AB_SIZES = (CONV_DIM, CONV_DIM, CONV_DIM, SSM_D_INNER, SSM_CONV_DIM, SSM_HEADS)
AB_IN = sum(AB_SIZES)
AB_MIX = CONV_DIM + SSM_D_INNER
CD_SIZES = (MLSTM_W, MLSTM_W, MLSTM_W, MLSTM_HEADS, MLSTM_HEADS, MLSTM_W,
            FOX_W, FOX_W, FOX_W, FOX_HEADS)
CD_IN = sum(CD_SIZES)
CD_MIX = MLSTM_W + FOX_W
MOE_GROUPS = 4
EXPERTS_PER_GROUP = 8
N_EXPERTS = MOE_GROUPS * EXPERTS_PER_GROUP
TOP_K = 2
D_EXPERT = D_MODEL // 2
MOE_BLOCK = 128

kernel_name = 'hybrid_conv_ssd_mlstm_fox_hiermoe_deepnorm'


def split_cols(a, sizes):
    cuts = [int(c) for c in np.cumsum(sizes)[:-1]]
    return jnp.split(a, cuts, axis=-1)


def layer_norm(x, g, b):
    xf = x.astype(jnp.float32)
    mu = jnp.mean(xf, axis=-1, keepdims=True)
    var = jnp.mean(jnp.square(xf - mu), axis=-1, keepdims=True)
    return ((xf - mu) * lax.rsqrt(var + LN_EPS) * g + b).astype(x.dtype)


def gated_rms_norm(y, z, w):
    u = y.astype(jnp.float32) * jax.nn.silu(z.astype(jnp.float32))
    return (u * lax.rsqrt(jnp.mean(u * u, axis=-1, keepdims=True) + LN_EPS) * w).astype(z.dtype)


def causal_dwconv(x, w):
    k_width, ch = w.shape
    return lax.conv_general_dilated(
        x, w[:, None, :].astype(x.dtype), window_strides=(1,),
        padding=[(k_width - 1, 0)], dimension_numbers=('NWC', 'WIO', 'NWC'),
        feature_group_count=ch)


def ssd_chunked(xs, dt, a_neg, b_mat, c_mat):
    bsz, s_len, n_heads, p_dim = xs.shape
    n_groups, n_state = b_mat.shape[2], b_mat.shape[3]
    r = n_heads // n_groups
    l_c = SSD_CHUNK
    nc = s_len // l_c
    f32 = jnp.float32
    xdt = (xs.astype(f32) * dt[..., None]).reshape(bsz, nc, l_c, n_groups, r, p_dim)
    a = jnp.transpose((dt * a_neg).reshape(bsz, nc, l_c, n_groups, r), (0, 1, 3, 4, 2))
    a_cs = jnp.cumsum(a, axis=-1)
    bc = b_mat.astype(f32).reshape(bsz, nc, l_c, n_groups, n_state)
    cc = c_mat.astype(f32).reshape(bsz, nc, l_c, n_groups, n_state)
    causal = jnp.tril(jnp.ones((l_c, l_c), bool))
    seg = jnp.exp(jnp.where(causal, a_cs[..., :, None] - a_cs[..., None, :], -jnp.inf))
    cb = jnp.einsum('bclgn,bcsgn->bcgls', cc, bc)
    y_diag = jnp.einsum('bcgls,bcgrls,bcsgrp->bclgrp', cb, seg, xdt)
    decay_to_end = jnp.exp(a_cs[..., -1:] - a_cs)
    chunk_states = jnp.einsum('bclgn,bcgrl,bclgrp->bcgrpn', bc, decay_to_end, xdt)
    chunk_decay = jnp.exp(a_cs[..., -1])

    def step(h, inp):
        s_c, d_c = inp
        return h * d_c[..., None, None] + s_c, h

    h0 = jnp.zeros((bsz, n_groups, r, p_dim, n_state), f32)
    _, h_prev = lax.scan(step, h0, (jnp.moveaxis(chunk_states, 1, 0), jnp.moveaxis(chunk_decay, 1, 0)))
    h_prev = jnp.moveaxis(h_prev, 0, 1)
    y_off = jnp.einsum('bclgn,bcgrpn,bcgrl->bclgrp', cc, h_prev, jnp.exp(a_cs))
    return (y_diag + y_off).reshape(bsz, s_len, n_heads, p_dim)


def mlstm_chunkwise(q, k, v, i_pre, f_pre):
    bsz, s_len, n_heads, dk = q.shape
    dv = v.shape[-1]
    l_c = MLSTM_CHUNK
    nc = s_len // l_c
    f32 = jnp.float32

    def to_chunks(a):
        a = a.astype(f32).reshape((bsz, nc, l_c) + a.shape[2:])
        return jnp.moveaxis(a, (1, 3), (0, 2))

    qc, kc, vc = to_chunks(q), to_chunks(k) * (dk ** -0.5), to_chunks(v)
    ic = to_chunks(i_pre)
    fc = to_chunks(jax.nn.log_sigmoid(f_pre.astype(f32)))
    causal = jnp.tril(jnp.ones((l_c, l_c), bool))

    def step(carry, inp):
        c_mat, n_vec, m_prev = carry
        q_b, k_b, v_b, i_b, f_b = inp
        b_cum = jnp.cumsum(f_b, axis=-1)
        d_mat = jnp.where(causal, b_cum[..., :, None] - b_cum[..., None, :] + i_b[..., None, :], -jnp.inf)
        inter = b_cum + m_prev[..., None]
        m_t = jnp.maximum(jnp.max(d_mat, axis=-1), inter)
        w_qk = jnp.einsum('bhld,bhsd->bhls', q_b, k_b) * jnp.exp(d_mat - m_t[..., None])
        s_inter = jnp.exp(inter - m_t)
        num = jnp.einsum('bhls,bhsv->bhlv', w_qk, v_b) + s_inter[..., None] * jnp.einsum('bhld,bhdv->bhlv', q_b, c_mat)
        den = jnp.sum(w_qk, axis=-1) + s_inter * jnp.einsum('bhld,bhd->bhl', q_b, n_vec)
        h = num / jnp.maximum(jnp.abs(den), jnp.exp(-m_t))[..., None]
        b_last = b_cum[..., -1]
        g_log = b_last[..., None] - b_cum + i_b
        m_new = jnp.maximum(b_last + m_prev, jnp.max(g_log, axis=-1))
        w_k = jnp.exp(g_log - m_new[..., None])
        decay = jnp.exp(b_last + m_prev - m_new)
        c_new = decay[..., None, None] * c_mat + jnp.einsum('bhs,bhsd,bhsv->bhdv', w_k, k_b, v_b)
        n_new = decay[..., None] * n_vec + jnp.einsum('bhs,bhsd->bhd', w_k, k_b)
        return (c_new, n_new, m_new), h

    init = (jnp.zeros((bsz, n_heads, dk, dv), f32), jnp.zeros((bsz, n_heads, dk), f32),
            jnp.zeros((bsz, n_heads), f32))
    _, h = lax.scan(step, init, (qc, kc, vc, ic, fc))
    return jnp.transpose(h, (1, 0, 3, 2, 4)).reshape(bsz, s_len, n_heads, dv)


def forgetting_attention(q, k, v, f_pre):
    bsz, s_len, n_heads, d = q.shape
    scale = d ** -0.5
    cum = jnp.transpose(jnp.cumsum(jax.nn.log_sigmoid(f_pre.astype(jnp.float32)), axis=1), (0, 2, 1))
    qh, kh, vh = (jnp.transpose(a, (0, 2, 1, 3)) for a in (q, k, v))
    outs = []
    for blk in range(s_len // Q_BLOCK):
        q0, q1 = blk * Q_BLOCK, (blk + 1) * Q_BLOCK
        logits = (jnp.einsum('bhqd,bhkd->bhqk', qh[:, :, q0:q1], kh[:, :, :q1]).astype(jnp.float32) * scale
                  + cum[:, :, q0:q1, None] - cum[:, :, None, :q1])
        mask = jnp.arange(q0, q1)[:, None] >= jnp.arange(q1)[None, :]
        p = jax.nn.softmax(jnp.where(mask, logits, -jnp.inf), axis=-1)
        outs.append(jnp.einsum('bhqk,bhkd->bhqd', p.astype(vh.dtype), vh[:, :, :q1]))
    return jnp.transpose(jnp.concatenate(outs, axis=2), (0, 2, 1, 3))


def conv_ssd_mixer(x, w_in, conv_a, conv_ssm_w, conv_ssm_b, dt_bias, a_log, d_skip, norm_w, w_out):
    bsz, s_len, _ = x.shape
    proj = jnp.dot(x, w_in)
    b_gate, c_gate, x_a, z, xbc, dt_raw = split_cols(proj, AB_SIZES)
    y_a = b_gate * causal_dwconv(c_gate * x_a, conv_a)
    xbc = jax.nn.silu(causal_dwconv(xbc, conv_ssm_w) + conv_ssm_b)
    xs, b_mat, c_mat = split_cols(xbc, (SSM_D_INNER, SSM_GROUPS * SSM_STATE, SSM_GROUPS * SSM_STATE))
    xs = xs.reshape(bsz, s_len, SSM_HEADS, SSM_HEAD_DIM)
    b_mat = b_mat.reshape(bsz, s_len, SSM_GROUPS, SSM_STATE)
    c_mat = c_mat.reshape(bsz, s_len, SSM_GROUPS, SSM_STATE)
    dt = jax.nn.softplus(dt_raw.astype(jnp.float32) + dt_bias)
    a_neg = -jnp.exp(a_log.astype(jnp.float32))
    y = ssd_chunked(xs, dt, a_neg, b_mat, c_mat) + d_skip[:, None] * xs.astype(jnp.float32)
    y_b = gated_rms_norm(y.reshape(bsz, s_len, SSM_D_INNER), z, norm_w)
    return jnp.dot(jnp.concatenate([y_a, y_b.astype(y_a.dtype)], axis=-1), w_out)


def mlstm_fox_mixer(x, w_in, i_bias, f_bias, hnorm_w, fox_f_bias, w_out):
    bsz, s_len, _ = x.shape
    proj = jnp.dot(x, w_in)
    q, k, v, i_pre, f_pre, o_pre, fq, fk, fv, ff = split_cols(proj, CD_SIZES)
    mh = lambda a: a.reshape(bsz, s_len, MLSTM_HEADS, MLSTM_HEAD_DIM)
    h = mlstm_chunkwise(mh(q), mh(k), mh(v), i_pre.astype(jnp.float32) + i_bias,
                        f_pre.astype(jnp.float32) + f_bias)
    mu = jnp.mean(h, axis=-1, keepdims=True)
    var = jnp.mean(jnp.square(h - mu), axis=-1, keepdims=True)
    h = (h - mu) * lax.rsqrt(var + LN_EPS) * hnorm_w.reshape(MLSTM_HEADS, MLSTM_HEAD_DIM)
    y_c = (jax.nn.sigmoid(o_pre.astype(jnp.float32)) * h.reshape(bsz, s_len, MLSTM_W)).astype(x.dtype)
    fh = lambda a: a.reshape(bsz, s_len, FOX_HEADS, FOX_HEAD_DIM)
    y_d = forgetting_attention(fh(fq), fh(fk), fh(fv), ff.astype(jnp.float32) + fox_f_bias)
    y_d = y_d.reshape(bsz, s_len, FOX_W).astype(x.dtype)
    return jnp.dot(jnp.concatenate([y_c, y_d], axis=-1), w_out)


def grouped_expert_ffn(xt, expert_id, w_gate, w_up, w_down):
    n_rows = expert_id.shape[0]
    d = xt.shape[1]
    n_exp = w_gate.shape[0]
    n_blocks = -(-n_rows // MOE_BLOCK) + n_exp
    order = jnp.argsort(expert_id)
    sorted_e = expert_id[order]
    counts = jnp.bincount(expert_id, length=n_exp)
    padded = (counts + MOE_BLOCK - 1) // MOE_BLOCK * MOE_BLOCK
    pad_end = jnp.cumsum(padded)
    pad_start = pad_end - padded
    start = jnp.cumsum(counts) - counts
    dest = pad_start[sorted_e] + jnp.arange(n_rows) - start[sorted_e]
    buf = jnp.zeros((n_blocks * MOE_BLOCK, d), xt.dtype).at[dest].set(xt[order // TOP_K])
    block_expert = jnp.minimum(
        jnp.searchsorted(pad_end, jnp.arange(n_blocks) * MOE_BLOCK, side='right'), n_exp - 1)

    def expert_block(args):
        xb, e = args
        hidden = jax.nn.silu(jnp.dot(xb, w_gate[e])) * jnp.dot(xb, w_up[e])
        return jnp.dot(hidden, w_down[e])

    out = lax.map(expert_block, (buf.reshape(n_blocks, MOE_BLOCK, d), block_expert))
    out_sorted = out.reshape(-1, d)[dest]
    return jnp.zeros((n_rows, d), out.dtype).at[order].set(out_sorted)


def hierarchical_moe(x, rg_w, rg_b, re_w, re_b, w_gate, w_up, w_down):
    bsz, s_len, d = x.shape
    xt = x.reshape(-1, d)
    n_tok = xt.shape[0]
    g_logits = jnp.dot(xt, rg_w).astype(jnp.float32) + rg_b
    g_prob = jax.nn.softmax(g_logits, axis=-1)
    g_idx = jnp.argmax(g_logits, axis=-1)
    p_group = jnp.max(g_prob, axis=-1, keepdims=True)
    e_logits_all = jnp.einsum('td,gde->tge', xt, re_w).astype(jnp.float32) + re_b
    e_logits = jnp.einsum('tge,tg->te', e_logits_all, jax.nn.one_hot(g_idx, MOE_GROUPS, dtype=jnp.float32))
    top_vals, top_loc = lax.top_k(e_logits, TOP_K)
    gate = p_group * jax.nn.softmax(top_vals, axis=-1)
    expert_id = (g_idx[:, None] * EXPERTS_PER_GROUP + top_loc).reshape(-1).astype(jnp.int32)
    rows = grouped_expert_ffn(xt, expert_id, w_gate, w_up, w_down)
    y = jnp.einsum('tkd,tk->td', rows.reshape(n_tok, TOP_K, d), gate.astype(rows.dtype))
    return y.reshape(bsz, s_len, d)


def setup_inputs(seed: int = 0) -> dict:
    key = jax.random.key(seed)
    ks = iter(jax.random.split(key, 48))
    f32 = jnp.float32

    def nrm(shape, scale):
        return jax.random.normal(next(ks), shape, f32) * scale

    dt0 = jnp.exp(jax.random.uniform(next(ks), (N_EVEN, SSM_HEADS), f32)
                  * (np.log(0.1) - np.log(0.001)) + np.log(0.001))
    return {
        'x': nrm((BATCH, SEQ, D_MODEL), 1.0),
        'ab_w_in': nrm((N_EVEN, D_MODEL, AB_IN), D_MODEL ** -0.5),
        'ab_conv_a': nrm((N_EVEN, CONV_WIDTH, CONV_DIM), CONV_WIDTH ** -0.5),
        'ab_conv_ssm_w': nrm((N_EVEN, SSM_CONV, SSM_CONV_DIM), SSM_CONV ** -0.5),
        'ab_conv_ssm_b': nrm((N_EVEN, SSM_CONV_DIM), 0.02),
        'ab_dt_bias': dt0 + jnp.log(-jnp.expm1(-dt0)),
        'ab_a_log': jnp.log(jax.random.uniform(next(ks), (N_EVEN, SSM_HEADS), f32, 1.0, 16.0)),
        'ab_d_skip': 1.0 + nrm((N_EVEN, SSM_HEADS), 0.1),
        'ab_norm_w': 1.0 + nrm((N_EVEN, SSM_D_INNER), 0.02),
        'ab_w_out': nrm((N_EVEN, AB_MIX, D_MODEL), AB_MIX ** -0.5 * BETA),
        'cd_w_in': nrm((N_ODD, D_MODEL, CD_IN), D_MODEL ** -0.5),
        'cd_i_bias': nrm((N_ODD, MLSTM_HEADS), 0.5),
        'cd_f_bias': 3.0 + nrm((N_ODD, MLSTM_HEADS), 0.5),
        'cd_hnorm_w': 1.0 + nrm((N_ODD, MLSTM_W), 0.02),
        'cd_fox_f_bias': 3.0 + nrm((N_ODD, FOX_HEADS), 0.5),
        'cd_w_out': nrm((N_ODD, CD_MIX, D_MODEL), CD_MIX ** -0.5 * BETA),
        'ln1_g': 1.0 + nrm((DEPTH, D_MODEL), 0.02),
        'ln1_b': nrm((DEPTH, D_MODEL), 0.02),
        'ln2_g': 1.0 + nrm((DEPTH, D_MODEL), 0.02),
        'ln2_b': nrm((DEPTH, D_MODEL), 0.02),
        'moe_rg_w': nrm((DEPTH, D_MODEL, MOE_GROUPS), D_MODEL ** -0.5),
        'moe_rg_b': nrm((DEPTH, MOE_GROUPS), 0.01),
        'moe_re_w': nrm((DEPTH, MOE_GROUPS, D_MODEL, EXPERTS_PER_GROUP), D_MODEL ** -0.5),
        'moe_re_b': nrm((DEPTH, MOE_GROUPS, EXPERTS_PER_GROUP), 0.01),
        'moe_w_gate': nrm((DEPTH, N_EXPERTS, D_MODEL, D_EXPERT), D_MODEL ** -0.5),
        'moe_w_up': nrm((DEPTH, N_EXPERTS, D_MODEL, D_EXPERT), D_MODEL ** -0.5),
        'moe_w_down': nrm((DEPTH, N_EXPERTS, D_EXPERT, D_MODEL), D_EXPERT ** -0.5 * BETA),
    }


def reference(x, ab_w_in, ab_conv_a, ab_conv_ssm_w, ab_conv_ssm_b, ab_dt_bias, ab_a_log, ab_d_skip,
              ab_norm_w, ab_w_out, cd_w_in, cd_i_bias, cd_f_bias, cd_hnorm_w, cd_fox_f_bias, cd_w_out,
              ln1_g, ln1_b, ln2_g, ln2_b, moe_rg_w, moe_rg_b, moe_re_w, moe_re_b,
              moe_w_gate, moe_w_up, moe_w_down):
    h = x
    for layer in range(DEPTH):
        j = layer // 2
        if layer % 2 == 0:
            mix = conv_ssd_mixer(h, ab_w_in[j], ab_conv_a[j], ab_conv_ssm_w[j], ab_conv_ssm_b[j],
                                 ab_dt_bias[j], ab_a_log[j], ab_d_skip[j], ab_norm_w[j], ab_w_out[j])
        else:
            mix = mlstm_fox_mixer(h, cd_w_in[j], cd_i_bias[j], cd_f_bias[j], cd_hnorm_w[j],
                                  cd_fox_f_bias[j], cd_w_out[j])
        h = layer_norm(ALPHA * h + mix, ln1_g[layer], ln1_b[layer])
        ffn = hierarchical_moe(h, moe_rg_w[layer], moe_rg_b[layer], moe_re_w[layer], moe_re_b[layer],
                               moe_w_gate[layer], moe_w_up[layer], moe_w_down[layer])
        h = layer_norm(ALPHA * h + ffn, ln2_g[layer], ln2_b[layer])
    return h
```

```python
import functools

import numpy as np
import jax
import jax.numpy as jnp
from jax import lax
from jax.experimental import pallas as pl
from jax.experimental.pallas import tpu as pltpu

F32 = jnp.float32
MXU_DTYPE = jnp.bfloat16
HIGHEST = lax.Precision.HIGHEST

D_MODEL = 1024
DEPTH = 4
ALPHA = (2 * DEPTH) ** 0.25
LN_EPS = 1e-5
CONV_DIM = D_MODEL // 2
CONV_WIDTH = 3
SSM_D_INNER = D_MODEL
SSM_HEAD_DIM = 64
SSM_HEADS = SSM_D_INNER // SSM_HEAD_DIM
SSM_GROUPS = 4
SSM_STATE = 64
SSM_CONV = 4
SSM_BC = SSM_GROUPS * SSM_STATE
SSM_CONV_DIM = SSM_D_INNER + 2 * SSM_BC
MLSTM_HEADS = 4
MLSTM_HEAD_DIM = D_MODEL // 8
MLSTM_W = MLSTM_HEADS * MLSTM_HEAD_DIM
FOX_HEADS = 8
FOX_HEAD_DIM = D_MODEL // 16
FOX_W = FOX_HEADS * FOX_HEAD_DIM
MOE_GROUPS = 4
EXPERTS_PER_GROUP = 8
N_EXPERTS = MOE_GROUPS * EXPERTS_PER_GROUP
TOP_K = 2
D_EXPERT = D_MODEL // 2

LANES = 128
SUBLANES = 8
VMEM_LIMIT_BYTES = 56 * 1024 * 1024

CHUNK = 128
TM_FRONT = 256
TM_OUT = 512
TM_ROUTER = 512
TQ_FOX = 256
MOE_ROWS = 256
TM_COMBINE = 256

AB_PROJ = 4224
CD_MAIN = 7 * 512
CD_PROJ = CD_MAIN + LANES
G_I, G_F, G_FOX, G_BCUM = 0, 4, 8, 16


def _params(*sem):
    return pltpu.CompilerParams(dimension_semantics=sem, vmem_limit_bytes=VMEM_LIMIT_BYTES)


def _softplus(x):
    return jnp.maximum(x, 0.0) + jnp.log(1.0 + jnp.exp(-jnp.abs(x)))


def _sigmoid(x):
    return 1.0 / (1.0 + jnp.exp(-x))


def _layer_norm_rows(v, g, b):
    mu = jnp.mean(v, axis=-1, keepdims=True)
    c = v - mu
    var = jnp.mean(c * c, axis=-1, keepdims=True)
    return c * lax.rsqrt(var + LN_EPS) * g + b


def _tril(n, block):
    i = np.arange(n)
    m = (i[:, None] >= i[None, :]) & (i[:, None] // block == i[None, :] // block)
    return jnp.asarray(m.astype(np.float32))


def _full(shape):
    return pl.BlockSpec(shape, lambda *_: (0,) * len(shape))


def _even_front_kernel(h_ref, w_ref, ca_ref, cw_ref, cb_ref, dtb_ref, aneg_ref, tril_ref,
                       ya_ref, z_ref, xbc_ref, dt_ref, acs_ref, acst_ref,
                       ua_ext, xbc_ext):
    tm = h_ref.shape[0]
    proj = jnp.dot(h_ref[...].astype(MXU_DTYPE), w_ref[...], preferred_element_type=F32)
    b_gate = proj[:, 0:CONV_DIM]
    c_gate = proj[:, CONV_DIM:2 * CONV_DIM]
    x_a = proj[:, 2 * CONV_DIM:3 * CONV_DIM]
    z0 = 3 * CONV_DIM
    z_ref[...] = proj[:, z0:z0 + SSM_D_INNER]
    x0 = z0 + SSM_D_INNER

    @pl.when(pl.program_id(1) == 0)
    def _():
        ua_ext[0:SUBLANES, :] = jnp.zeros((SUBLANES, CONV_DIM), F32)
        xbc_ext[0:SUBLANES, :] = jnp.zeros((SUBLANES, SSM_CONV_DIM), F32)

    ua_ext[SUBLANES:SUBLANES + tm, :] = c_gate * x_a
    conv = ca_ref[0:1, :] * ua_ext[pl.ds(SUBLANES - 2, tm), :]
    for k in range(1, CONV_WIDTH):
        conv = conv + ca_ref[k:k + 1, :] * ua_ext[pl.ds(SUBLANES - (CONV_WIDTH - 1) + k, tm), :]
    ya_ref[...] = b_gate * conv
    ua_ext[0:SUBLANES, :] = ua_ext[tm:tm + SUBLANES, :]

    xbc_ext[SUBLANES:SUBLANES + tm, :] = proj[:, x0:x0 + SSM_CONV_DIM]
    conv = cb_ref[...] + cw_ref[0:1, :] * xbc_ext[pl.ds(SUBLANES - (SSM_CONV - 1), tm), :]
    for k in range(1, SSM_CONV):
        conv = conv + cw_ref[k:k + 1, :] * xbc_ext[pl.ds(SUBLANES - (SSM_CONV - 1) + k, tm), :]
    xbc_ref[...] = conv * _sigmoid(conv)
    xbc_ext[0:SUBLANES, :] = xbc_ext[tm:tm + SUBLANES, :]

    d0 = x0 + SSM_CONV_DIM
    dt = _softplus(proj[:, d0:d0 + LANES] + dtb_ref[...])
    a = dt * aneg_ref[...]
    acs = jnp.dot(tril_ref[...], a, precision=HIGHEST, preferred_element_type=F32)
    dt_ref[...] = dt
    acs_ref[...] = acs
    acst_ref[...] = acs.T


def _even_front(h, w_in, conv_a, conv_w, conv_b, dt_bias_row, aneg_row, bsz, seq):
    tok = bsz * seq
    tm = TM_FRONT
    ns = seq // tm
    row = lambda b, s: (b * ns + s, 0)
    out_shapes = (
        jax.ShapeDtypeStruct((tok, CONV_DIM), F32),
        jax.ShapeDtypeStruct((tok, SSM_D_INNER), F32),
        jax.ShapeDtypeStruct((tok, SSM_CONV_DIM), F32),
        jax.ShapeDtypeStruct((tok, LANES), F32),
        jax.ShapeDtypeStruct((tok, LANES), F32),
        jax.ShapeDtypeStruct((LANES, tok), F32),
    )
    return pl.pallas_call(
        _even_front_kernel,
        out_shape=out_shapes,
        grid=(bsz, ns),
        in_specs=[
            pl.BlockSpec((tm, D_MODEL), row),
            _full((D_MODEL, AB_PROJ)),
            _full((CONV_WIDTH, CONV_DIM)),
            _full((SSM_CONV, SSM_CONV_DIM)),
            _full((1, SSM_CONV_DIM)),
            _full((1, LANES)),
            _full((1, LANES)),
            _full((tm, tm)),
        ],
        out_specs=(
            pl.BlockSpec((tm, CONV_DIM), row),
            pl.BlockSpec((tm, SSM_D_INNER), row),
            pl.BlockSpec((tm, SSM_CONV_DIM), row),
            pl.BlockSpec((tm, LANES), row),
            pl.BlockSpec((tm, LANES), row),
            pl.BlockSpec((LANES, tm), lambda b, s: (0, b * ns + s)),
        ),
        scratch_shapes=[
            pltpu.VMEM((tm + SUBLANES, CONV_DIM), F32),
            pltpu.VMEM((tm + SUBLANES, SSM_CONV_DIM), F32),
        ],
        compiler_params=_params("arbitrary", "arbitrary"),
        name="even_front",
    )(h, w_in, conv_a, conv_w, conv_b, dt_bias_row, aneg_row, _tril(tm, CHUNK))


def _bcast_heads(arr, n_heads, width):
    per = LANES // width
    length = arr.shape[0]
    lane = lax.broadcasted_iota(jnp.int32, (length, LANES), 1)
    outs = []
    for j in range(n_heads // per):
        v = jnp.broadcast_to(arr[:, j * per:j * per + 1], (length, LANES))
        for r in range(1, per):
            v = jnp.where(lane >= r * width, jnp.broadcast_to(arr[:, j * per + r:j * per + r + 1], (length, LANES)), v)
        outs.append(v)
    return jnp.concatenate(outs, axis=1)


def _ssd_kernel(xbc_ref, dt_ref, acs_ref, acst_ref, z_ref, dskip_ref, nw_ref, y_ref, state):
    L = CHUNK
    P = SSM_HEAD_DIM
    R = SSM_HEADS // SSM_GROUPS
    GW = R * P

    @pl.when(pl.program_id(1) == 0)
    def _():
        state[...] = jnp.zeros(state.shape, F32)

    xs = xbc_ref[:, 0:SSM_D_INNER]
    bm = xbc_ref[:, SSM_D_INNER:SSM_D_INNER + SSM_BC]
    cm = xbc_ref[:, SSM_D_INNER + SSM_BC:SSM_CONV_DIM]
    dt = dt_ref[...]
    acs = acs_ref[...]
    acst = acst_ref[...]
    a_last = acs[L - 1:L, :]

    dtx = _bcast_heads(dt, SSM_HEADS, P)
    decx = _bcast_heads(jnp.exp(a_last - acs), SSM_HEADS, P)
    expx = _bcast_heads(jnp.exp(acs), SSM_HEADS, P)
    xdt = xs * dtx
    xdec = (xdt * decx).astype(MXU_DTYPE)
    xdt_m = xdt.astype(MXU_DTYPE)
    chunk_decay = jnp.exp(jnp.broadcast_to(a_last, (SUBLANES, LANES)))
    cdx = _bcast_heads(chunk_decay, SSM_HEADS, P)[0:1, :]

    bm_t = bm.T.astype(MXU_DTYPE)
    cm_m = cm.astype(MXU_DTYPE)
    bm_m = bm.astype(MXU_DTYPE)
    row = lax.broadcasted_iota(jnp.int32, (L, L), 0)
    col = lax.broadcasted_iota(jnp.int32, (L, L), 1)
    causal = row >= col
    lane_g = lax.broadcasted_iota(jnp.int32, (L, GW), 1)

    ys = []
    for g in range(SSM_GROUPS):
        n0 = g * SSM_STATE
        c_g = cm_m[:, n0:n0 + SSM_STATE]
        cb = lax.dot_general(c_g, bm_m[:, n0:n0 + SSM_STATE], (((1,), (1,)), ((), ())),
                             preferred_element_type=F32)
        ms = []
        for r in range(R):
            hd = g * R + r
            seg = jnp.exp(jnp.where(causal, acs[:, hd:hd + 1] - acst[hd:hd + 1, :], -jnp.inf))
            ms.append((cb * seg).astype(MXU_DTYPE))
        big = jnp.dot(jnp.concatenate(ms, axis=0), xdt_m[:, g * GW:(g + 1) * GW],
                      preferred_element_type=F32)
        y_diag = big[0:L, :]
        for r in range(1, R):
            y_diag = jnp.where(lane_g >= r * P, big[r * L:(r + 1) * L, :], y_diag)
        st = state[g]
        y_off = jnp.dot(c_g, st.astype(MXU_DTYPE), preferred_element_type=F32)
        new = jnp.dot(bm_t[n0:n0 + SSM_STATE, :], xdec[:, g * GW:(g + 1) * GW],
                      preferred_element_type=F32)
        state[g] = st * cdx[:, g * GW:(g + 1) * GW] + new
        ys.append(y_diag + y_off * expx[:, g * GW:(g + 1) * GW])
    y = jnp.concatenate(ys, axis=1) + dskip_ref[...] * xs
    z = z_ref[...]
    u = y * (z * _sigmoid(z))
    y_ref[...] = u * lax.rsqrt(jnp.mean(u * u, axis=-1, keepdims=True) + LN_EPS) * nw_ref[...]


def _ssd(xbc, dt, acs, acst, z, dskip_row, normw_row, bsz, seq):
    tok = bsz * seq
    nc = seq // CHUNK
    row = lambda b, c: (b * nc + c, 0)
    return pl.pallas_call(
        _ssd_kernel,
        out_shape=jax.ShapeDtypeStruct((tok, SSM_D_INNER), F32),
        grid=(bsz, nc),
        in_specs=[
            pl.BlockSpec((CHUNK, SSM_CONV_DIM), row),
            pl.BlockSpec((CHUNK, LANES), row),
            pl.BlockSpec((CHUNK, LANES), row),
            pl.BlockSpec((LANES, CHUNK), lambda b, c: (0, b * nc + c)),
            pl.BlockSpec((CHUNK, SSM_D_INNER), row),
            _full((1, SSM_D_INNER)),
            _full((1, SSM_D_INNER)),
        ],
        out_specs=pl.BlockSpec((CHUNK, SSM_D_INNER), row),
        scratch_shapes=[pltpu.VMEM((SSM_GROUPS, SSM_STATE, SSM_D_INNER // SSM_GROUPS), F32)],
        compiler_params=_params("arbitrary", "arbitrary"),
        name="ssd_scan",
    )(xbc, dt, acs, acst, z, dskip_row, normw_row)


def _outproj_ln_kernel(*refs, widths):
    n = len(widths)
    parts = refs[:n]
    h_ref, w_ref, g_ref, b_ref, o_ref = refs[n:]
    acc = ALPHA * h_ref[...]
    off = 0
    for p, wd in zip(parts, widths):
        acc = acc + jnp.dot(p[...].astype(MXU_DTYPE), w_ref[off:off + wd, :], preferred_element_type=F32)
        off += wd
    o_ref[...] = _layer_norm_rows(acc, g_ref[...], b_ref[...])


def _outproj_ln(parts, h, w_out, g_row, b_row):
    tok = h.shape[0]
    tm = TM_OUT
    widths = tuple(p.shape[1] for p in parts)
    row = lambda i: (i, 0)
    return pl.pallas_call(
        functools.partial(_outproj_ln_kernel, widths=widths),
        out_shape=jax.ShapeDtypeStruct((tok, D_MODEL), F32),
        grid=(tok // tm,),
        in_specs=[pl.BlockSpec((tm, wd), row) for wd in widths] + [
            pl.BlockSpec((tm, D_MODEL), row),
            _full((sum(widths), D_MODEL)),
            _full((1, D_MODEL)),
            _full((1, D_MODEL)),
        ],
        out_specs=pl.BlockSpec((tm, D_MODEL), row),
        compiler_params=_params("arbitrary"),
        name="outproj_ln",
    )(*parts, h, w_out, g_row, b_row)


def _odd_front_kernel(h_ref, w_ref, gb_ref, tril_l_ref, tril_g_ref,
                      q_ref, k_ref, v_ref, o_ref, fq_ref, fk_ref, fv_ref, g_ref, gt_ref, carry):
    tm = h_ref.shape[0]
    proj = jnp.dot(h_ref[...].astype(MXU_DTYPE), w_ref[...], preferred_element_type=F32)
    q_ref[...] = proj[:, 0:512]
    k_ref[...] = proj[:, 512:1024] * (MLSTM_HEAD_DIM ** -0.5)
    v_ref[...] = proj[:, 1024:1536]
    o_ref[...] = _sigmoid(proj[:, 1536:2048])
    fq_ref[...] = proj[:, 2048:2560] * (FOX_HEAD_DIM ** -0.5)
    fk_ref[...] = proj[:, 2560:3072]
    fv_ref[...] = proj[:, 3072:3584]

    @pl.when(pl.program_id(1) == 0)
    def _():
        carry[...] = jnp.zeros(carry.shape, F32)

    raw = proj[:, CD_MAIN:CD_MAIN + LANES] + gb_ref[...]
    lane = lax.broadcasted_iota(jnp.int32, (tm, LANES), 1)
    g = jnp.where(lane < G_F, raw, -_softplus(-raw))
    local = jnp.dot(tril_l_ref[...], g, precision=HIGHEST, preferred_element_type=F32)
    glob = jnp.dot(tril_g_ref[...], g, precision=HIGHEST, preferred_element_type=F32) + carry[0:1, :]
    carry[...] = jnp.broadcast_to(glob[tm - 1:tm, :], carry.shape)
    out = jnp.where((lane >= G_FOX) & (lane < G_BCUM), glob, jnp.where(lane >= G_BCUM, local, g))
    g_ref[...] = out
    gt_ref[...] = out.T


def _odd_front(h, w_in, gate_bias_row, bsz, seq):
    tok = bsz * seq
    tm = TM_FRONT
    ns = seq // tm
    row = lambda b, s: (b * ns + s, 0)
    wide = jax.ShapeDtypeStruct((tok, 512), F32)
    return pl.pallas_call(
        _odd_front_kernel,
        out_shape=(wide,) * 7 + (jax.ShapeDtypeStruct((tok, LANES), F32),
                                 jax.ShapeDtypeStruct((LANES, tok), F32)),
        grid=(bsz, ns),
        in_specs=[
            pl.BlockSpec((tm, D_MODEL), row),
            _full((D_MODEL, CD_PROJ)),
            _full((1, LANES)),
            _full((tm, tm)),
            _full((tm, tm)),
        ],
        out_specs=(pl.BlockSpec((tm, 512), row),) * 7 + (
            pl.BlockSpec((tm, LANES), row),
            pl.BlockSpec((LANES, tm), lambda b, s: (0, b * ns + s)),
        ),
        scratch_shapes=[pltpu.VMEM((SUBLANES, LANES), F32)],
        compiler_params=_params("arbitrary", "arbitrary"),
        name="odd_front",
    )(h, w_in, gate_bias_row, _tril(tm, CHUNK), _tril(tm, tm))


def _mlstm_kernel(q_ref, k_ref, v_ref, o_ref, g_ref, gt_ref, nw_ref, y_ref, c_state, m_state):
    L = CHUNK
    DH = MLSTM_HEAD_DIM

    @pl.when(pl.program_id(1) == 0)
    def _():
        c_state[...] = jnp.zeros(c_state.shape, F32)
        m_state[...] = jnp.zeros(m_state.shape, F32)

    gates = g_ref[...]
    gates_t = gt_ref[...]
    row = lax.broadcasted_iota(jnp.int32, (L, L), 0)
    col = lax.broadcasted_iota(jnp.int32, (L, L), 1)
    causal = row >= col
    ones_col = jnp.where(lax.broadcasted_iota(jnp.int32, (L, DH), 1) == 0, 1.0, 0.0)

    for hd in range(MLSTM_HEADS):
        sl = slice(hd * DH, (hd + 1) * DH)
        q = q_ref[:, sl].astype(MXU_DTYPE)
        k = k_ref[:, sl]
        v_ext = jnp.concatenate([v_ref[:, sl], ones_col], axis=1).astype(MXU_DTYPE)
        b_col = gates[:, G_BCUM + hd:G_BCUM + hd + 1]
        i_col = gates[:, G_I + hd:G_I + hd + 1]
        b_row = gates_t[G_BCUM + hd:G_BCUM + hd + 1, :]
        i_row = gates_t[G_I + hd:G_I + hd + 1, :]
        m_prev = m_state[hd:hd + 1, 0:1]
        c_ext = c_state[hd]

        d_mat = jnp.where(causal, b_col - b_row + i_row, -jnp.inf)
        inter = b_col + m_prev
        m_t = jnp.maximum(jnp.max(d_mat, axis=-1, keepdims=True), inter)
        s_qk = lax.dot_general(q, k.astype(MXU_DTYPE), (((1,), (1,)), ((), ())), preferred_element_type=F32)
        w_qk = s_qk * jnp.exp(d_mat - m_t)
        s_inter = jnp.exp(inter - m_t)
        num_ext = (jnp.dot(w_qk.astype(MXU_DTYPE), v_ext, preferred_element_type=F32)
                   + s_inter * jnp.dot(q, c_ext.astype(MXU_DTYPE), preferred_element_type=F32))
        den = num_ext[:, DH:DH + 1]
        hval = num_ext[:, 0:DH] / jnp.maximum(jnp.abs(den), jnp.exp(-m_t))

        b_last = b_col[L - 1:L, :]
        g_log = b_last - b_col + i_col
        m_new = jnp.maximum(b_last + m_prev, jnp.max(g_log, axis=0, keepdims=True))
        w_k = jnp.exp(g_log - m_new)
        decay = jnp.exp(b_last + m_prev - m_new)
        kw_t = (k * w_k).T.astype(MXU_DTYPE)
        c_state[hd] = decay * c_ext + jnp.dot(kw_t, v_ext, preferred_element_type=F32)
        m_state[hd:hd + 1, :] = jnp.broadcast_to(m_new, (1, LANES))

        mu = jnp.mean(hval, axis=-1, keepdims=True)
        cen = hval - mu
        var = jnp.mean(cen * cen, axis=-1, keepdims=True)
        y_ref[:, sl] = o_ref[:, sl] * (cen * lax.rsqrt(var + LN_EPS) * nw_ref[:, sl])


def _mlstm(q, k, v, o, gates, gates_t, hnorm_row, bsz, seq):
    tok = bsz * seq
    nc = seq // CHUNK
    row = lambda b, c: (b * nc + c, 0)
    wide = pl.BlockSpec((CHUNK, MLSTM_W), row)
    return pl.pallas_call(
        _mlstm_kernel,
        out_shape=jax.ShapeDtypeStruct((tok, MLSTM_W), F32),
        grid=(bsz, nc),
        in_specs=[wide, wide, wide, wide,
                  pl.BlockSpec((CHUNK, LANES), row),
                  pl.BlockSpec((LANES, CHUNK), lambda b, c: (0, b * nc + c)),
                  _full((1, MLSTM_W))],
        out_specs=wide,
        scratch_shapes=[pltpu.VMEM((MLSTM_HEADS, MLSTM_HEAD_DIM, 2 * MLSTM_HEAD_DIM), F32),
                        pltpu.VMEM((SUBLANES, LANES), F32)],
        compiler_params=_params("arbitrary", "arbitrary"),
        name="mlstm_scan",
    )(q, k, v, o, gates, gates_t, hnorm_row)


def _fox_kernel(q_ref, k_ref, v_ref, g_ref, gt_ref, y_ref):
    tq = q_ref.shape[0]
    qi = pl.program_id(1)
    dh = FOX_HEAD_DIM
    lane = lax.broadcasted_iota(jnp.int32, (tq, LANES), 1)
    low = lane < dh
    row = lax.broadcasted_iota(jnp.int32, (tq, tq), 0)
    col = lax.broadcasted_iota(jnp.int32, (tq, tq), 1)
    causal = row >= col

    for pair in range(FOX_HEADS // 2):
        ps = slice(pair * LANES, (pair + 1) * LANES)
        q_pair = q_ref[:, ps]
        outs = []
        for half in range(2):
            hd = 2 * pair + half
            keep = low if half == 0 else jnp.logical_not(low)
            qh = jnp.where(keep, q_pair, 0.0).astype(MXU_DTYPE)
            cq = g_ref[:, G_FOX + hd:G_FOX + hd + 1]

            def step(j, carry, masked, qh=qh, cq=cq, hd=hd, ps=ps):
                m, l, acc = carry
                k0 = pl.multiple_of(j * tq, tq)
                kj = k_ref[pl.ds(k0, tq), ps].astype(MXU_DTYPE)
                vj = v_ref[pl.ds(k0, tq), ps].astype(MXU_DTYPE)
                ck = gt_ref[G_FOX + hd:G_FOX + hd + 1, pl.ds(k0, tq)]
                s = lax.dot_general(qh, kj, (((1,), (1,)), ((), ())), preferred_element_type=F32) + cq - ck
                if masked:
                    s = jnp.where(causal, s, -jnp.inf)
                m_new = jnp.maximum(m, jnp.max(s, axis=-1, keepdims=True))
                p = jnp.exp(s - m_new)
                alpha = jnp.exp(m - m_new)
                l = alpha * l + jnp.sum(p, axis=-1, keepdims=True)
                acc = alpha * acc + jnp.dot(p.astype(MXU_DTYPE), vj, preferred_element_type=F32)
                return m_new, l, acc

            init = (jnp.full((tq, 1), -jnp.inf, F32), jnp.zeros((tq, 1), F32), jnp.zeros((tq, LANES), F32))
            carry = lax.fori_loop(0, qi, functools.partial(step, masked=False), init)
            m, l, acc = step(qi, carry, masked=True)
            outs.append(acc / l)
        y_ref[:, ps] = jnp.where(low, outs[0], outs[1])


def _fox(fq, fk, fv, gates, gates_t, bsz, seq):
    tok = bsz * seq
    tq = TQ_FOX
    nq = seq // tq
    return pl.pallas_call(
        _fox_kernel,
        out_shape=jax.ShapeDtypeStruct((tok, FOX_W), F32),
        grid=(bsz, nq),
        in_specs=[
            pl.BlockSpec((tq, FOX_W), lambda b, i: (b * nq + i, 0)),
            pl.BlockSpec((seq, FOX_W), lambda b, i: (b, 0)),
            pl.BlockSpec((seq, FOX_W), lambda b, i: (b, 0)),
            pl.BlockSpec((tq, LANES), lambda b, i: (b * nq + i, 0)),
            pl.BlockSpec((LANES, seq), lambda b, i: (0, b)),
        ],
        out_specs=pl.BlockSpec((tq, FOX_W), lambda b, i: (b * nq + i, 0)),
        compiler_params=_params("arbitrary", "arbitrary"),
        name="fox_attention",
    )(fq, fk, fv, gates, gates_t)


def _router_kernel(h_ref, w_ref, b_ref, id_ref, gate_ref):
    tm = h_ref.shape[0]
    logits = jnp.dot(h_ref[...], w_ref[...], precision=HIGHEST, preferred_element_type=F32) + b_ref[...]
    lane = lax.broadcasted_iota(jnp.int32, (tm, LANES), 1).astype(F32)
    neg = -jnp.inf
    first = lambda hit: jnp.min(jnp.where(hit, lane, float(LANES)), axis=-1, keepdims=True)
    gl = jnp.where(lane < MOE_GROUPS, logits, neg)
    g_max = jnp.max(gl, axis=-1, keepdims=True)
    g_idx = first(gl == g_max)
    p_group = 1.0 / jnp.sum(jnp.exp(gl - g_max), axis=-1, keepdims=True)
    e_lo = MOE_GROUPS + g_idx * EXPERTS_PER_GROUP
    el = jnp.where((lane >= e_lo) & (lane < e_lo + EXPERTS_PER_GROUP), logits, neg)
    v1 = jnp.max(el, axis=-1, keepdims=True)
    i1 = first(el == v1)
    el2 = jnp.where(lane == i1, neg, el)
    v2 = jnp.max(el2, axis=-1, keepdims=True)
    i2 = first(el2 == v2)
    t = jnp.exp(v2 - v1)
    w1 = 1.0 / (1.0 + t)
    ids = jnp.where(lane == 0.0, i1 - MOE_GROUPS, jnp.where(lane == 1.0, i2 - MOE_GROUPS, 0.0))
    id_ref[...] = ids.astype(jnp.int32)
    gate_ref[...] = jnp.where(lane == 0.0, p_group * w1, jnp.where(lane == 1.0, p_group * (t * w1), 0.0))


def _router(h, w_router, b_router):
    tok = h.shape[0]
    tm = TM_ROUTER
    row = lambda i: (i, 0)
    return pl.pallas_call(
        _router_kernel,
        out_shape=(jax.ShapeDtypeStruct((tok, LANES), jnp.int32), jax.ShapeDtypeStruct((tok, LANES), F32)),
        grid=(tok // tm,),
        in_specs=[pl.BlockSpec((tm, D_MODEL), row), _full((D_MODEL, LANES)), _full((1, LANES))],
        out_specs=(pl.BlockSpec((tm, LANES), row), pl.BlockSpec((tm, LANES), row)),
        compiler_params=_params("arbitrary"),
        name="moe_router",
    )(h, w_router, b_router)


def _gather_rows(idx_ref, base, n_rows, src_hbm, dst, sem):
    def copy(r, tok):
        return pltpu.make_async_copy(src_hbm.at[pl.ds(tok, 1)], dst.at[pl.ds(r, 1)], sem)

    def issue(r, c):
        copy(r, idx_ref[base + r]).start()
        return c

    def drain(r, c):
        copy(r, 0).wait()
        return c

    lax.fori_loop(0, n_rows, issue, 0)
    lax.fori_loop(0, n_rows, drain, 0)


def _expert_kernel(be_ref, tok_ref, nused_ref, x_hbm, wg_ref, wu_ref, wd_ref, out_ref, xbuf, sem):
    i = pl.program_id(0)
    rows = xbuf.shape[0]

    @pl.when(i < nused_ref[0])
    def _():
        _gather_rows(tok_ref, i * rows, rows, x_hbm, xbuf, sem)
        x = xbuf[...].astype(MXU_DTYPE)
        gate = jnp.dot(x, wg_ref[0].astype(MXU_DTYPE), preferred_element_type=F32)
        up = jnp.dot(x, wu_ref[0].astype(MXU_DTYPE), preferred_element_type=F32)
        hidden = (gate * _sigmoid(gate) * up).astype(MXU_DTYPE)
        out_ref[...] = jnp.dot(hidden, wd_ref[0].astype(MXU_DTYPE), preferred_element_type=F32)

    @pl.when(i >= nused_ref[0])
    def _():
        out_ref[...] = jnp.zeros(out_ref.shape, F32)


def _expert_ffn(h, block_expert, slot_token, n_used, w_gate, w_up, w_down):
    n_blocks = block_expert.shape[0]
    rows = MOE_ROWS
    wsel = lambda i, be, tok, nu: (be[i], 0, 0)
    grid_spec = pltpu.PrefetchScalarGridSpec(
        num_scalar_prefetch=3,
        grid=(n_blocks,),
        in_specs=[
            pl.BlockSpec(memory_space=pl.ANY),
            pl.BlockSpec((1, D_MODEL, D_EXPERT), wsel),
            pl.BlockSpec((1, D_MODEL, D_EXPERT), wsel),
            pl.BlockSpec((1, D_EXPERT, D_MODEL), wsel),
        ],
        out_specs=pl.BlockSpec((rows, D_MODEL), lambda i, be, tok, nu: (i, 0)),
        scratch_shapes=[pltpu.VMEM((rows, D_MODEL), F32), pltpu.SemaphoreType.DMA(())],
    )
    return pl.pallas_call(
        _expert_kernel,
        out_shape=jax.ShapeDtypeStruct((n_blocks * rows, D_MODEL), F32),
        grid_spec=grid_spec,
        compiler_params=_params("arbitrary"),
        name="moe_experts",
    )(block_expert, slot_token, n_used, h, w_gate, w_up, w_down)


def _combine_ln_kernel(d0_ref, d1_ref, rows_hbm, h_ref, gate_ref, g_ref, b_ref, o_ref, buf0, buf1, sem):
    tm = h_ref.shape[0]
    base = pl.program_id(0) * tm
    _gather_rows(d0_ref, base, tm, rows_hbm, buf0, sem)
    _gather_rows(d1_ref, base, tm, rows_hbm, buf1, sem)
    gate = gate_ref[...]
    acc = ALPHA * h_ref[...] + gate[:, 0:1] * buf0[...] + gate[:, 1:2] * buf1[...]
    o_ref[...] = _layer_norm_rows(acc, g_ref[...], b_ref[...])


def _combine_ln(dest0, dest1, rows, h, gates, g_row, b_row):
    tok = h.shape[0]
    tm = TM_COMBINE
    row = lambda i, d0, d1: (i, 0)
    const = lambda i, d0, d1: (0, 0)
    grid_spec = pltpu.PrefetchScalarGridSpec(
        num_scalar_prefetch=2,
        grid=(tok // tm,),
        in_specs=[
            pl.BlockSpec(memory_space=pl.ANY),
            pl.BlockSpec((tm, D_MODEL), row),
            pl.BlockSpec((tm, LANES), row),
            pl.BlockSpec((1, D_MODEL), const),
            pl.BlockSpec((1, D_MODEL), const),
        ],
        out_specs=pl.BlockSpec((tm, D_MODEL), row),
        scratch_shapes=[pltpu.VMEM((tm, D_MODEL), F32), pltpu.VMEM((tm, D_MODEL), F32),
                        pltpu.SemaphoreType.DMA(())],
    )
    return pl.pallas_call(
        _combine_ln_kernel,
        out_shape=jax.ShapeDtypeStruct((tok, D_MODEL), F32),
        grid_spec=grid_spec,
        compiler_params=_params("arbitrary"),
        name="moe_combine_ln",
    )(dest0, dest1, rows, h, gates, g_row, b_row)


def _dispatch_indices(expert_id):
    n_tok = expert_id.shape[0]
    n_rows = n_tok * TOP_K
    n_blocks = n_rows // MOE_ROWS + N_EXPERTS
    flat = expert_id.reshape(-1)
    onehot = (flat[:, None] == jnp.arange(N_EXPERTS, dtype=jnp.int32)[None, :]).astype(jnp.int32)
    csum = jnp.cumsum(onehot, axis=0)
    rank = jnp.sum(onehot * csum, axis=1) - 1
    counts = csum[-1]
    padded = (counts + MOE_ROWS - 1) // MOE_ROWS * MOE_ROWS
    pad_end = jnp.cumsum(padded)
    pad_start = pad_end - padded
    dest = pad_start[flat] + rank
    slot_token = jnp.zeros((n_blocks * MOE_ROWS,), jnp.int32).at[dest].set(
        jnp.arange(n_rows, dtype=jnp.int32) // TOP_K)
    block_expert = jnp.minimum(
        jnp.searchsorted(pad_end, jnp.arange(n_blocks, dtype=jnp.int32) * MOE_ROWS, side='right'),
        N_EXPERTS - 1).astype(jnp.int32)
    n_used = (pad_end[-1:] // MOE_ROWS).astype(jnp.int32)
    return dest.reshape(n_tok, TOP_K), slot_token, block_expert, n_used


def _moe_ln(h, w_router, b_router, w_gate, w_up, w_down, g_row, b_row):
    ids, gates = _router(h, w_router, b_router)
    dest, slot_token, block_expert, n_used = _dispatch_indices(ids[:, :TOP_K])
    rows = _expert_ffn(h, block_expert, slot_token, n_used, w_gate, w_up, w_down)
    return _combine_ln(dest[:, 0], dest[:, 1], rows, h, gates, g_row, b_row)


def _pad_lanes(v):
    return jnp.pad(v, (0, LANES - v.shape[0])).reshape(1, LANES)


def _even_mixer(h, bsz, seq, w_in, conv_a, conv_w, conv_b, dt_bias, a_log, d_skip, norm_w, w_out, g_row, b_row):
    w = jnp.pad(w_in, ((0, 0), (0, AB_PROJ - w_in.shape[1]))).astype(MXU_DTYPE)
    y_a, z, xbc, dt, acs, acst = _even_front(
        h, w, conv_a, conv_w, conv_b.reshape(1, -1), _pad_lanes(dt_bias), _pad_lanes(-jnp.exp(a_log)), bsz, seq)
    dskip_row = jnp.repeat(d_skip, SSM_HEAD_DIM).reshape(1, -1)
    y_b = _ssd(xbc, dt, acs, acst, z, dskip_row, norm_w.reshape(1, -1), bsz, seq)
    return _outproj_ln([y_a, y_b], h, w_out.astype(MXU_DTYPE), g_row, b_row)


def _odd_mixer(h, bsz, seq, w_in, i_bias, f_bias, hnorm_w, fox_f_bias, w_out, g_row, b_row):
    c = np.cumsum((0, MLSTM_W, MLSTM_W, MLSTM_W, MLSTM_HEADS, MLSTM_HEADS, MLSTM_W, FOX_W, FOX_W, FOX_W, FOX_HEADS))
    part = lambda j: w_in[:, c[j]:c[j + 1]]
    q, k, v, i_pre, f_pre, o_pre, fq, fk, fv, ff = (part(j) for j in range(10))
    gate_cols = jnp.concatenate([i_pre, f_pre, ff, f_pre], axis=1)
    gate_cols = jnp.pad(gate_cols, ((0, 0), (0, LANES - gate_cols.shape[1])))
    w = jnp.concatenate([q, k, v, o_pre, fq, fk, fv, gate_cols], axis=1).astype(MXU_DTYPE)
    gate_bias = _pad_lanes(jnp.concatenate([i_bias, f_bias, fox_f_bias, f_bias]))
    q, k, v, o, fq, fk, fv, gates, gates_t = _odd_front(h, w, gate_bias, bsz, seq)
    y_c = _mlstm(q, k, v, o, gates, gates_t, hnorm_w.reshape(1, -1), bsz, seq)
    y_d = _fox(fq, fk, fv, gates, gates_t, bsz, seq)
    return _outproj_ln([y_c, y_d], h, w_out.astype(MXU_DTYPE), g_row, b_row)


def kernel(x, ab_w_in, ab_conv_a, ab_conv_ssm_w, ab_conv_ssm_b, ab_dt_bias, ab_a_log, ab_d_skip, ab_norm_w, ab_w_out, cd_w_in, cd_i_bias, cd_f_bias, cd_hnorm_w, cd_fox_f_bias, cd_w_out, ln1_g, ln1_b, ln2_g, ln2_b, moe_rg_w, moe_rg_b, moe_re_w, moe_re_b, moe_w_gate, moe_w_up, moe_w_down):
    bsz, seq, d = x.shape
    h = x.reshape(bsz * seq, d)
    for layer in range(DEPTH):
        j = layer // 2
        g1, b1 = ln1_g[layer].reshape(1, -1), ln1_b[layer].reshape(1, -1)
        if layer % 2 == 0:
            h = _even_mixer(h, bsz, seq, ab_w_in[j], ab_conv_a[j], ab_conv_ssm_w[j], ab_conv_ssm_b[j],
                            ab_dt_bias[j], ab_a_log[j], ab_d_skip[j], ab_norm_w[j], ab_w_out[j], g1, b1)
        else:
            h = _odd_mixer(h, bsz, seq, cd_w_in[j], cd_i_bias[j], cd_f_bias[j], cd_hnorm_w[j],
                           cd_fox_f_bias[j], cd_w_out[j], g1, b1)
        re_w = jnp.transpose(moe_re_w[layer], (1, 0, 2)).reshape(d, N_EXPERTS)
        w_router = jnp.pad(jnp.concatenate([moe_rg_w[layer], re_w], axis=1),
                           ((0, 0), (0, LANES - MOE_GROUPS - N_EXPERTS)))
        b_router = _pad_lanes(jnp.concatenate([moe_rg_b[layer], moe_re_b[layer].reshape(-1)]))
        h = _moe_ln(h, w_router, b_router, moe_w_gate[layer], moe_w_up[layer], moe_w_down[layer],
                    ln2_g[layer].reshape(1, -1), ln2_b[layer].reshape(1, -1))
    return h.reshape(bsz, seq, d)
```

```python
import functools

import numpy as np
import jax
import jax.numpy as jnp
from jax import lax
from jax.experimental import pallas as pl
from jax.experimental.pallas import tpu as pltpu

F32 = jnp.float32
MXU_DTYPE = jnp.bfloat16
HIGHEST = lax.Precision.HIGHEST

D_MODEL = 1024
DEPTH = 4
ALPHA = (2 * DEPTH) ** 0.25
LN_EPS = 1e-5
CONV_DIM = D_MODEL // 2
CONV_WIDTH = 3
SSM_D_INNER = D_MODEL
SSM_HEAD_DIM = 64
SSM_HEADS = SSM_D_INNER // SSM_HEAD_DIM
SSM_GROUPS = 4
SSM_STATE = 64
SSM_CONV = 4
SSM_BC = SSM_GROUPS * SSM_STATE
SSM_CONV_DIM = SSM_D_INNER + 2 * SSM_BC
MLSTM_HEADS = 4
MLSTM_HEAD_DIM = D_MODEL // 8
MLSTM_W = MLSTM_HEADS * MLSTM_HEAD_DIM
FOX_HEADS = 8
FOX_HEAD_DIM = D_MODEL // 16
FOX_W = FOX_HEADS * FOX_HEAD_DIM
MOE_GROUPS = 4
EXPERTS_PER_GROUP = 8
N_EXPERTS = MOE_GROUPS * EXPERTS_PER_GROUP
TOP_K = 2
D_EXPERT = D_MODEL // 2

LANES = 128
SUBLANES = 8
VMEM_LIMIT_BYTES = 56 * 1024 * 1024

CHUNK = 128
TM_FRONT = 256
TM_OUT = 512
TM_ROUTER = 512
TQ_FOX = 256
MOE_ROWS = 256
TM_COMBINE = 256
TM_DISPATCH = 512

AB_PROJ = 4224
CD_MAIN = 7 * 512
CD_PROJ = CD_MAIN + LANES
G_I, G_F, G_FOX, G_BCUM = 0, 4, 8, 16


def _params(*sem):
    return pltpu.CompilerParams(dimension_semantics=sem, vmem_limit_bytes=VMEM_LIMIT_BYTES)


def _softplus(x):
    return jnp.maximum(x, 0.0) + jnp.log(1.0 + jnp.exp(-jnp.abs(x)))


def _sigmoid(x):
    return 1.0 / (1.0 + jnp.exp(-x))


def _layer_norm_rows(v, g, b):
    mu = jnp.mean(v, axis=-1, keepdims=True)
    c = v - mu
    var = jnp.mean(c * c, axis=-1, keepdims=True)
    return c * lax.rsqrt(var + LN_EPS) * g + b


def _tril(n, block):
    i = np.arange(n)
    m = (i[:, None] >= i[None, :]) & (i[:, None] // block == i[None, :] // block)
    return jnp.asarray(m.astype(np.float32))


def _full(shape):
    return pl.BlockSpec(shape, lambda *_: (0,) * len(shape))


def _even_front_kernel(h_ref, w_ref, ca_ref, cw_ref, cb_ref, dtb_ref, aneg_ref, tril_ref,
                       ya_ref, z_ref, xbc_ref, dt_ref, acs_ref, acst_ref,
                       ua_ext, xbc_ext):
    tm = h_ref.shape[0]
    proj = jnp.dot(h_ref[...].astype(MXU_DTYPE), w_ref[...], preferred_element_type=F32)
    b_gate = proj[:, 0:CONV_DIM]
    c_gate = proj[:, CONV_DIM:2 * CONV_DIM]
    x_a = proj[:, 2 * CONV_DIM:3 * CONV_DIM]
    z0 = 3 * CONV_DIM
    z_ref[...] = proj[:, z0:z0 + SSM_D_INNER]
    x0 = z0 + SSM_D_INNER

    @pl.when(pl.program_id(1) == 0)
    def _():
        ua_ext[0:SUBLANES, :] = jnp.zeros((SUBLANES, CONV_DIM), F32)
        xbc_ext[0:SUBLANES, :] = jnp.zeros((SUBLANES, SSM_CONV_DIM), F32)

    ua_ext[SUBLANES:SUBLANES + tm, :] = c_gate * x_a
    conv = ca_ref[0:1, :] * ua_ext[pl.ds(SUBLANES - 2, tm), :]
    for k in range(1, CONV_WIDTH):
        conv = conv + ca_ref[k:k + 1, :] * ua_ext[pl.ds(SUBLANES - (CONV_WIDTH - 1) + k, tm), :]
    ya_ref[...] = b_gate * conv
    ua_ext[0:SUBLANES, :] = ua_ext[tm:tm + SUBLANES, :]

    xbc_ext[SUBLANES:SUBLANES + tm, :] = proj[:, x0:x0 + SSM_CONV_DIM]
    conv = cb_ref[...] + cw_ref[0:1, :] * xbc_ext[pl.ds(SUBLANES - (SSM_CONV - 1), tm), :]
    for k in range(1, SSM_CONV):
        conv = conv + cw_ref[k:k + 1, :] * xbc_ext[pl.ds(SUBLANES - (SSM_CONV - 1) + k, tm), :]
    xbc_ref[...] = conv * _sigmoid(conv)
    xbc_ext[0:SUBLANES, :] = xbc_ext[tm:tm + SUBLANES, :]

    d0 = x0 + SSM_CONV_DIM
    dt = _softplus(proj[:, d0:d0 + LANES] + dtb_ref[...])
    a = dt * aneg_ref[...]
    acs = jnp.dot(tril_ref[...], a, precision=HIGHEST, preferred_element_type=F32)
    dt_ref[...] = dt
    acs_ref[...] = acs
    acst_ref[...] = acs.T


def _even_front(h, w_in, conv_a, conv_w, conv_b, dt_bias_row, aneg_row, bsz, seq):
    tok = bsz * seq
    tm = TM_FRONT
    ns = seq // tm
    row = lambda b, s: (b * ns + s, 0)
    out_shapes = (
        jax.ShapeDtypeStruct((tok, CONV_DIM), F32),
        jax.ShapeDtypeStruct((tok, SSM_D_INNER), F32),
        jax.ShapeDtypeStruct((tok, SSM_CONV_DIM), F32),
        jax.ShapeDtypeStruct((tok, LANES), F32),
        jax.ShapeDtypeStruct((tok, LANES), F32),
        jax.ShapeDtypeStruct((LANES, tok), F32),
    )
    return pl.pallas_call(
        _even_front_kernel,
        out_shape=out_shapes,
        grid=(bsz, ns),
        in_specs=[
            pl.BlockSpec((tm, D_MODEL), row),
            _full((D_MODEL, AB_PROJ)),
            _full((CONV_WIDTH, CONV_DIM)),
            _full((SSM_CONV, SSM_CONV_DIM)),
            _full((1, SSM_CONV_DIM)),
            _full((1, LANES)),
            _full((1, LANES)),
            _full((tm, tm)),
        ],
        out_specs=(
            pl.BlockSpec((tm, CONV_DIM), row),
            pl.BlockSpec((tm, SSM_D_INNER), row),
            pl.BlockSpec((tm, SSM_CONV_DIM), row),
            pl.BlockSpec((tm, LANES), row),
            pl.BlockSpec((tm, LANES), row),
            pl.BlockSpec((LANES, tm), lambda b, s: (0, b * ns + s)),
        ),
        scratch_shapes=[
            pltpu.VMEM((tm + SUBLANES, CONV_DIM), F32),
            pltpu.VMEM((tm + SUBLANES, SSM_CONV_DIM), F32),
        ],
        compiler_params=_params("arbitrary", "arbitrary"),
        name="even_front",
    )(h, w_in, conv_a, conv_w, conv_b, dt_bias_row, aneg_row, _tril(tm, CHUNK))


def _bcast_heads(arr, n_heads, width):
    per = LANES // width
    length = arr.shape[0]
    lane = lax.broadcasted_iota(jnp.int32, (length, LANES), 1)
    outs = []
    for j in range(n_heads // per):
        v = jnp.broadcast_to(arr[:, j * per:j * per + 1], (length, LANES))
        for r in range(1, per):
            v = jnp.where(lane >= r * width, jnp.broadcast_to(arr[:, j * per + r:j * per + r + 1], (length, LANES)), v)
        outs.append(v)
    return jnp.concatenate(outs, axis=1)


def _ssd_kernel(xbc_ref, dt_ref, acs_ref, acst_ref, z_ref, dskip_ref, nw_ref, y_ref, state):
    L = CHUNK
    P = SSM_HEAD_DIM
    R = SSM_HEADS // SSM_GROUPS
    GW = R * P

    @pl.when(pl.program_id(1) == 0)
    def _():
        state[...] = jnp.zeros(state.shape, F32)

    xs = xbc_ref[:, 0:SSM_D_INNER]
    bm = xbc_ref[:, SSM_D_INNER:SSM_D_INNER + SSM_BC]
    cm = xbc_ref[:, SSM_D_INNER + SSM_BC:SSM_CONV_DIM]
    dt = dt_ref[...]
    acs = acs_ref[...]
    acst = acst_ref[...]
    a_last = acs[L - 1:L, :]

    dtx = _bcast_heads(dt, SSM_HEADS, P)
    decx = _bcast_heads(jnp.exp(a_last - acs), SSM_HEADS, P)
    expx = _bcast_heads(jnp.exp(acs), SSM_HEADS, P)
    xdt = xs * dtx
    xdec = (xdt * decx).astype(MXU_DTYPE)
    xdt_m = xdt.astype(MXU_DTYPE)
    chunk_decay = jnp.exp(jnp.broadcast_to(a_last, (SUBLANES, LANES)))
    cdx = _bcast_heads(chunk_decay, SSM_HEADS, P)[0:1, :]

    bm_t = bm.T.astype(MXU_DTYPE)
    cm_m = cm.astype(MXU_DTYPE)
    bm_m = bm.astype(MXU_DTYPE)
    row = lax.broadcasted_iota(jnp.int32, (L, L), 0)
    col = lax.broadcasted_iota(jnp.int32, (L, L), 1)
    causal = row >= col
    lane_g = lax.broadcasted_iota(jnp.int32, (L, GW), 1)

    ys = []
    for g in range(SSM_GROUPS):
        n0 = g * SSM_STATE
        c_g = cm_m[:, n0:n0 + SSM_STATE]
        cb = lax.dot_general(c_g, bm_m[:, n0:n0 + SSM_STATE], (((1,), (1,)), ((), ())),
                             preferred_element_type=F32)
        ms = []
        for r in range(R):
            hd = g * R + r
            seg = jnp.exp(jnp.where(causal, acs[:, hd:hd + 1] - acst[hd:hd + 1, :], -jnp.inf))
            ms.append((cb * seg).astype(MXU_DTYPE))
        big = jnp.dot(jnp.concatenate(ms, axis=0), xdt_m[:, g * GW:(g + 1) * GW],
                      preferred_element_type=F32)
        y_diag = big[0:L, :]
        for r in range(1, R):
            y_diag = jnp.where(lane_g >= r * P, big[r * L:(r + 1) * L, :], y_diag)
        st = state[g]
        y_off = jnp.dot(c_g, st.astype(MXU_DTYPE), preferred_element_type=F32)
        new = jnp.dot(bm_t[n0:n0 + SSM_STATE, :], xdec[:, g * GW:(g + 1) * GW],
                      preferred_element_type=F32)
        state[g] = st * cdx[:, g * GW:(g + 1) * GW] + new
        ys.append(y_diag + y_off * expx[:, g * GW:(g + 1) * GW])
    y = jnp.concatenate(ys, axis=1) + dskip_ref[...] * xs
    z = z_ref[...]
    u = y * (z * _sigmoid(z))
    y_ref[...] = u * lax.rsqrt(jnp.mean(u * u, axis=-1, keepdims=True) + LN_EPS) * nw_ref[...]


def _ssd(xbc, dt, acs, acst, z, dskip_row, normw_row, bsz, seq):
    tok = bsz * seq
    nc = seq // CHUNK
    row = lambda b, c: (b * nc + c, 0)
    return pl.pallas_call(
        _ssd_kernel,
        out_shape=jax.ShapeDtypeStruct((tok, SSM_D_INNER), F32),
        grid=(bsz, nc),
        in_specs=[
            pl.BlockSpec((CHUNK, SSM_CONV_DIM), row),
            pl.BlockSpec((CHUNK, LANES), row),
            pl.BlockSpec((CHUNK, LANES), row),
            pl.BlockSpec((LANES, CHUNK), lambda b, c: (0, b * nc + c)),
            pl.BlockSpec((CHUNK, SSM_D_INNER), row),
            _full((1, SSM_D_INNER)),
            _full((1, SSM_D_INNER)),
        ],
        out_specs=pl.BlockSpec((CHUNK, SSM_D_INNER), row),
        scratch_shapes=[pltpu.VMEM((SSM_GROUPS, SSM_STATE, SSM_D_INNER // SSM_GROUPS), F32)],
        compiler_params=_params("arbitrary", "arbitrary"),
        name="ssd_scan",
    )(xbc, dt, acs, acst, z, dskip_row, normw_row)


def _outproj_ln_kernel(*refs, widths):
    n = len(widths)
    parts = refs[:n]
    h_ref, w_ref, g_ref, b_ref, o_ref = refs[n:]
    acc = ALPHA * h_ref[...]
    off = 0
    for p, wd in zip(parts, widths):
        acc = acc + jnp.dot(p[...].astype(MXU_DTYPE), w_ref[off:off + wd, :], preferred_element_type=F32)
        off += wd
    o_ref[...] = _layer_norm_rows(acc, g_ref[...], b_ref[...])


def _outproj_ln(parts, h, w_out, g_row, b_row):
    tok = h.shape[0]
    tm = TM_OUT
    widths = tuple(p.shape[1] for p in parts)
    row = lambda i: (i, 0)
    return pl.pallas_call(
        functools.partial(_outproj_ln_kernel, widths=widths),
        out_shape=jax.ShapeDtypeStruct((tok, D_MODEL), F32),
        grid=(tok // tm,),
        in_specs=[pl.BlockSpec((tm, wd), row) for wd in widths] + [
            pl.BlockSpec((tm, D_MODEL), row),
            _full((sum(widths), D_MODEL)),
            _full((1, D_MODEL)),
            _full((1, D_MODEL)),
        ],
        out_specs=pl.BlockSpec((tm, D_MODEL), row),
        compiler_params=_params("arbitrary"),
        name="outproj_ln",
    )(*parts, h, w_out, g_row, b_row)


def _odd_front_kernel(h_ref, w_ref, gb_ref, tril_l_ref, tril_g_ref,
                      q_ref, k_ref, v_ref, o_ref, fq_ref, fk_ref, fv_ref, g_ref, gt_ref, carry):
    tm = h_ref.shape[0]
    proj = jnp.dot(h_ref[...].astype(MXU_DTYPE), w_ref[...], preferred_element_type=F32)
    q_ref[...] = proj[:, 0:512]
    k_ref[...] = proj[:, 512:1024] * (MLSTM_HEAD_DIM ** -0.5)
    v_ref[...] = proj[:, 1024:1536]
    o_ref[...] = _sigmoid(proj[:, 1536:2048])
    fq_ref[...] = proj[:, 2048:2560] * (FOX_HEAD_DIM ** -0.5)
    fk_ref[...] = proj[:, 2560:3072]
    fv_ref[...] = proj[:, 3072:3584]

    @pl.when(pl.program_id(1) == 0)
    def _():
        carry[...] = jnp.zeros(carry.shape, F32)

    raw = proj[:, CD_MAIN:CD_MAIN + LANES] + gb_ref[...]
    lane = lax.broadcasted_iota(jnp.int32, (tm, LANES), 1)
    g = jnp.where(lane < G_F, raw, -_softplus(-raw))
    local = jnp.dot(tril_l_ref[...], g, precision=HIGHEST, preferred_element_type=F32)
    glob = jnp.dot(tril_g_ref[...], g, precision=HIGHEST, preferred_element_type=F32) + carry[0:1, :]
    carry[...] = jnp.broadcast_to(glob[tm - 1:tm, :], carry.shape)
    out = jnp.where((lane >= G_FOX) & (lane < G_BCUM), glob, jnp.where(lane >= G_BCUM, local, g))
    g_ref[...] = out
    gt_ref[...] = out.T


def _odd_front(h, w_in, gate_bias_row, bsz, seq):
    tok = bsz * seq
    tm = TM_FRONT
    ns = seq // tm
    row = lambda b, s: (b * ns + s, 0)
    wide = jax.ShapeDtypeStruct((tok, 512), F32)
    return pl.pallas_call(
        _odd_front_kernel,
        out_shape=(wide,) * 7 + (jax.ShapeDtypeStruct((tok, LANES), F32),
                                 jax.ShapeDtypeStruct((LANES, tok), F32)),
        grid=(bsz, ns),
        in_specs=[
            pl.BlockSpec((tm, D_MODEL), row),
            _full((D_MODEL, CD_PROJ)),
            _full((1, LANES)),
            _full((tm, tm)),
            _full((tm, tm)),
        ],
        out_specs=(pl.BlockSpec((tm, 512), row),) * 7 + (
            pl.BlockSpec((tm, LANES), row),
            pl.BlockSpec((LANES, tm), lambda b, s: (0, b * ns + s)),
        ),
        scratch_shapes=[pltpu.VMEM((SUBLANES, LANES), F32)],
        compiler_params=_params("arbitrary", "arbitrary"),
        name="odd_front",
    )(h, w_in, gate_bias_row, _tril(tm, CHUNK), _tril(tm, tm))


def _mlstm_kernel(q_ref, k_ref, v_ref, o_ref, g_ref, gt_ref, nw_ref, y_ref, c_state, m_state):
    L = CHUNK
    DH = MLSTM_HEAD_DIM

    @pl.when(pl.program_id(1) == 0)
    def _():
        c_state[...] = jnp.zeros(c_state.shape, F32)
        m_state[...] = jnp.zeros(m_state.shape, F32)

    gates = g_ref[...]
    gates_t = gt_ref[...]
    row = lax.broadcasted_iota(jnp.int32, (L, L), 0)
    col = lax.broadcasted_iota(jnp.int32, (L, L), 1)
    causal = row >= col
    ones_col = jnp.where(lax.broadcasted_iota(jnp.int32, (L, DH), 1) == 0, 1.0, 0.0)

    for hd in range(MLSTM_HEADS):
        sl = slice(hd * DH, (hd + 1) * DH)
        q = q_ref[:, sl].astype(MXU_DTYPE)
        k = k_ref[:, sl]
        v_ext = jnp.concatenate([v_ref[:, sl], ones_col], axis=1).astype(MXU_DTYPE)
        b_col = gates[:, G_BCUM + hd:G_BCUM + hd + 1]
        i_col = gates[:, G_I + hd:G_I + hd + 1]
        b_row = gates_t[G_BCUM + hd:G_BCUM + hd + 1, :]
        i_row = gates_t[G_I + hd:G_I + hd + 1, :]
        m_prev = m_state[hd:hd + 1, 0:1]
        c_ext = c_state[hd]

        d_mat = jnp.where(causal, b_col - b_row + i_row, -jnp.inf)
        inter = b_col + m_prev
        m_t = jnp.maximum(jnp.max(d_mat, axis=-1, keepdims=True), inter)
        s_qk = lax.dot_general(q, k.astype(MXU_DTYPE), (((1,), (1,)), ((), ())), preferred_element_type=F32)
        w_qk = s_qk * jnp.exp(d_mat - m_t)
        s_inter = jnp.exp(inter - m_t)
        num_ext = (jnp.dot(w_qk.astype(MXU_DTYPE), v_ext, preferred_element_type=F32)
                   + s_inter * jnp.dot(q, c_ext.astype(MXU_DTYPE), preferred_element_type=F32))
        den = num_ext[:, DH:DH + 1]
        hval = num_ext[:, 0:DH] / jnp.maximum(jnp.abs(den), jnp.exp(-m_t))

        b_last = b_col[L - 1:L, :]
        g_log = b_last - b_col + i_col
        m_new = jnp.maximum(b_last + m_prev, jnp.max(g_log, axis=0, keepdims=True))
        w_k = jnp.exp(g_log - m_new)
        decay = jnp.exp(b_last + m_prev - m_new)
        kw_t = (k * w_k).T.astype(MXU_DTYPE)
        c_state[hd] = decay * c_ext + jnp.dot(kw_t, v_ext, preferred_element_type=F32)
        m_state[hd:hd + 1, :] = jnp.broadcast_to(m_new, (1, LANES))

        mu = jnp.mean(hval, axis=-1, keepdims=True)
        cen = hval - mu
        var = jnp.mean(cen * cen, axis=-1, keepdims=True)
        y_ref[:, sl] = o_ref[:, sl] * (cen * lax.rsqrt(var + LN_EPS) * nw_ref[:, sl])


def _mlstm(q, k, v, o, gates, gates_t, hnorm_row, bsz, seq):
    tok = bsz * seq
    nc = seq // CHUNK
    row = lambda b, c: (b * nc + c, 0)
    wide = pl.BlockSpec((CHUNK, MLSTM_W), row)
    return pl.pallas_call(
        _mlstm_kernel,
        out_shape=jax.ShapeDtypeStruct((tok, MLSTM_W), F32),
        grid=(bsz, nc),
        in_specs=[wide, wide, wide, wide,
                  pl.BlockSpec((CHUNK, LANES), row),
                  pl.BlockSpec((LANES, CHUNK), lambda b, c: (0, b * nc + c)),
                  _full((1, MLSTM_W))],
        out_specs=wide,
        scratch_shapes=[pltpu.VMEM((MLSTM_HEADS, MLSTM_HEAD_DIM, 2 * MLSTM_HEAD_DIM), F32),
                        pltpu.VMEM((SUBLANES, LANES), F32)],
        compiler_params=_params("arbitrary", "arbitrary"),
        name="mlstm_scan",
    )(q, k, v, o, gates, gates_t, hnorm_row)


def _fox_kernel(q_ref, k_ref, v_ref, g_ref, gt_ref, y_ref):
    tq = q_ref.shape[0]
    qi = pl.program_id(1)
    dh = FOX_HEAD_DIM
    lane = lax.broadcasted_iota(jnp.int32, (tq, LANES), 1)
    low = lane < dh
    row = lax.broadcasted_iota(jnp.int32, (tq, tq), 0)
    col = lax.broadcasted_iota(jnp.int32, (tq, tq), 1)
    causal = row >= col

    for pair in range(FOX_HEADS // 2):
        ps = slice(pair * LANES, (pair + 1) * LANES)
        q_pair = q_ref[:, ps]
        outs = []
        for half in range(2):
            hd = 2 * pair + half
            keep = low if half == 0 else jnp.logical_not(low)
            qh = jnp.where(keep, q_pair, 0.0).astype(MXU_DTYPE)
            cq = g_ref[:, G_FOX + hd:G_FOX + hd + 1]

            def step(j, carry, masked, qh=qh, cq=cq, hd=hd, ps=ps):
                m, l, acc = carry
                k0 = pl.multiple_of(j * tq, tq)
                kj = k_ref[pl.ds(k0, tq), ps].astype(MXU_DTYPE)
                vj = v_ref[pl.ds(k0, tq), ps].astype(MXU_DTYPE)
                ck = gt_ref[G_FOX + hd:G_FOX + hd + 1, pl.ds(k0, tq)]
                s = lax.dot_general(qh, kj, (((1,), (1,)), ((), ())), preferred_element_type=F32) + cq - ck
                if masked:
                    s = jnp.where(causal, s, -jnp.inf)
                m_new = jnp.maximum(m, jnp.max(s, axis=-1, keepdims=True))
                p = jnp.exp(s - m_new)
                alpha = jnp.exp(m - m_new)
                l = alpha * l + jnp.sum(p, axis=-1, keepdims=True)
                acc = alpha * acc + jnp.dot(p.astype(MXU_DTYPE), vj, preferred_element_type=F32)
                return m_new, l, acc

            init = (jnp.full((tq, 1), -jnp.inf, F32), jnp.zeros((tq, 1), F32), jnp.zeros((tq, LANES), F32))
            carry = lax.fori_loop(0, qi, functools.partial(step, masked=False), init)
            m, l, acc = step(qi, carry, masked=True)
            outs.append(acc / l)
        y_ref[:, ps] = jnp.where(low, outs[0], outs[1])


def _fox(fq, fk, fv, gates, gates_t, bsz, seq):
    tok = bsz * seq
    tq = TQ_FOX
    nq = seq // tq
    return pl.pallas_call(
        _fox_kernel,
        out_shape=jax.ShapeDtypeStruct((tok, FOX_W), F32),
        grid=(bsz, nq),
        in_specs=[
            pl.BlockSpec((tq, FOX_W), lambda b, i: (b * nq + i, 0)),
            pl.BlockSpec((seq, FOX_W), lambda b, i: (b, 0)),
            pl.BlockSpec((seq, FOX_W), lambda b, i: (b, 0)),
            pl.BlockSpec((tq, LANES), lambda b, i: (b * nq + i, 0)),
            pl.BlockSpec((LANES, seq), lambda b, i: (0, b)),
        ],
        out_specs=pl.BlockSpec((tq, FOX_W), lambda b, i: (b * nq + i, 0)),
        compiler_params=_params("arbitrary", "arbitrary"),
        name="fox_attention",
    )(fq, fk, fv, gates, gates_t)


def _router_kernel(h_ref, w_ref, b_ref, stril_ref, id_ref, gate_ref, cnt_ref, carry):
    tm = h_ref.shape[0]

    @pl.when(pl.program_id(0) == 0)
    def _():
        carry[...] = jnp.zeros(carry.shape, F32)

    logits = jnp.dot(h_ref[...], w_ref[...], precision=HIGHEST, preferred_element_type=F32) + b_ref[...]
    lane = lax.broadcasted_iota(jnp.int32, (tm, LANES), 1).astype(F32)
    neg = -jnp.inf
    first = lambda hit: jnp.min(jnp.where(hit, lane, float(LANES)), axis=-1, keepdims=True)
    gl = jnp.where(lane < MOE_GROUPS, logits, neg)
    g_max = jnp.max(gl, axis=-1, keepdims=True)
    g_idx = first(gl == g_max)
    p_group = 1.0 / jnp.sum(jnp.exp(gl - g_max), axis=-1, keepdims=True)
    e_lo = MOE_GROUPS + g_idx * EXPERTS_PER_GROUP
    el = jnp.where((lane >= e_lo) & (lane < e_lo + EXPERTS_PER_GROUP), logits, neg)
    v1 = jnp.max(el, axis=-1, keepdims=True)
    i1 = first(el == v1)
    el2 = jnp.where(lane == i1, neg, el)
    v2 = jnp.max(el2, axis=-1, keepdims=True)
    i2 = first(el2 == v2)
    t = jnp.exp(v2 - v1)
    w1 = 1.0 / (1.0 + t)
    gate_ref[...] = jnp.where(lane == 0.0, p_group * w1, jnp.where(lane == 1.0, p_group * (t * w1), 0.0))

    hit1 = lane == i1
    hit2 = lane == i2
    sent = jnp.where(hit1 | hit2, 1.0, 0.0)
    before = jnp.dot(stril_ref[...], sent.astype(jnp.bfloat16), preferred_element_type=F32) + carry[0:1, :]
    r1 = jnp.sum(jnp.where(hit1, before, 0.0), axis=-1, keepdims=True)
    r2 = jnp.sum(jnp.where(hit2, before, 0.0), axis=-1, keepdims=True)
    total = before[tm - 1:tm, :] + sent[tm - 1:tm, :]
    carry[...] = jnp.broadcast_to(total, carry.shape)
    cnt_ref[...] = jnp.broadcast_to(total, cnt_ref.shape).astype(jnp.int32)
    ids = jnp.where(lane == 0.0, i1 - MOE_GROUPS, jnp.where(lane == 1.0, i2 - MOE_GROUPS,
                    jnp.where(lane == 2.0, r1, jnp.where(lane == 3.0, r2, 0.0))))
    id_ref[...] = ids.astype(jnp.int32)


def _router(h, w_router, b_router):
    tok = h.shape[0]
    tm = TM_ROUTER
    row = lambda i: (i, 0)
    idx = np.arange(tm)
    stril = jnp.asarray((idx[:, None] > idx[None, :]).astype(np.float32), jnp.bfloat16)
    return pl.pallas_call(
        _router_kernel,
        out_shape=(jax.ShapeDtypeStruct((tok, LANES), jnp.int32), jax.ShapeDtypeStruct((tok, LANES), F32),
                   jax.ShapeDtypeStruct((SUBLANES, LANES), jnp.int32)),
        grid=(tok // tm,),
        in_specs=[pl.BlockSpec((tm, D_MODEL), row), _full((D_MODEL, LANES)), _full((1, LANES)), _full((tm, tm))],
        out_specs=(pl.BlockSpec((tm, LANES), row), pl.BlockSpec((tm, LANES), row), _full((SUBLANES, LANES))),
        scratch_shapes=[pltpu.VMEM((SUBLANES, LANES), F32)],
        compiler_params=_params("arbitrary"),
        name="moe_router",
    )(h, w_router, b_router, stril)


def _pad_pieces(n):
    return tuple(1 << b for b in range((SUBLANES - 1).bit_length(), (n - 1).bit_length()))


def _dispatch_kernel(e0_ref, e1_ref, r0_ref, r1_ref, start_ref, cnt_ref, nused_ref, h_ref, xs_hbm, zeros, sem,
                     pad_sem):
    tm = h_ref.shape[0]
    i = pl.program_id(0)
    base = i * tm

    def row_copy(r, slot):
        return pltpu.make_async_copy(h_ref.at[pl.ds(r, 1)], xs_hbm.at[pl.ds(slot, 1)], sem)

    def issue(r, c):
        t = base + r
        row_copy(r, start_ref[e0_ref[t]] + r0_ref[t]).start()
        row_copy(r, start_ref[e1_ref[t]] + r1_ref[t]).start()
        return c

    lax.fori_loop(0, tm, issue, 0, unroll=8)

    def pad_copies(e, fn):
        cnt = cnt_ref[e]
        n_pad = (MOE_ROWS - cnt % MOE_ROWS) % MOE_ROWS
        first = start_ref[e] + cnt
        head = n_pad % SUBLANES
        for j in range(SUBLANES - 1):
            @pl.when(j < head)
            def _(j=j):
                fn(pltpu.make_async_copy(zeros.at[pl.ds(0, 1)], xs_hbm.at[pl.ds(first + j, 1)], pad_sem))
        for piece in _pad_pieces(MOE_ROWS):
            @pl.when((n_pad & piece) != 0)
            def _(piece=piece):
                off = pl.multiple_of(first + head + (n_pad & (piece - SUBLANES)), SUBLANES)
                fn(pltpu.make_async_copy(zeros.at[pl.ds(0, piece)], xs_hbm.at[pl.ds(off, piece)], pad_sem))

    def tail_copies(blk, fn):
        for part in range(MOE_ROWS // zeros.shape[0]):
            off = pl.multiple_of(blk * MOE_ROWS + part * zeros.shape[0], SUBLANES)
            fn(pltpu.make_async_copy(zeros, xs_hbm.at[pl.ds(off, zeros.shape[0])], pad_sem))

    @pl.when(i == 0)
    def _():
        zeros[...] = jnp.zeros(zeros.shape, F32)
        n_blocks = xs_hbm.shape[0] // MOE_ROWS

        def start(e, c):
            pad_copies(e, lambda cp: cp.start())
            return c

        def wait(e, c):
            pad_copies(e, lambda cp: cp.wait())
            return c

        def tail_start(blk, c):
            tail_copies(blk, lambda cp: cp.start())
            return c

        def tail_wait(blk, c):
            tail_copies(blk, lambda cp: cp.wait())
            return c

        lax.fori_loop(0, N_EXPERTS, start, 0)
        lax.fori_loop(nused_ref[0], n_blocks, tail_start, 0)
        lax.fori_loop(0, N_EXPERTS, wait, 0)
        lax.fori_loop(nused_ref[0], n_blocks, tail_wait, 0)

    for _ in range(TOP_K):
        pltpu.make_async_copy(h_ref, xs_hbm.at[pl.ds(0, tm)], sem).wait()


def _dispatch(h, e0, e1, r0, r1, pad_start, counts, n_used, n_slots):
    tok = h.shape[0]
    tm = TM_DISPATCH
    grid_spec = pltpu.PrefetchScalarGridSpec(
        num_scalar_prefetch=7,
        grid=(tok // tm,),
        in_specs=[pl.BlockSpec((tm, D_MODEL), lambda i, *_: (i, 0))],
        out_specs=pl.BlockSpec(memory_space=pl.ANY),
        scratch_shapes=[pltpu.VMEM((MOE_ROWS // 2, D_MODEL), F32),
                        pltpu.SemaphoreType.DMA(()), pltpu.SemaphoreType.DMA(())],
    )
    return pl.pallas_call(
        _dispatch_kernel,
        out_shape=jax.ShapeDtypeStruct((n_slots, D_MODEL), F32),
        grid_spec=grid_spec,
        compiler_params=_params("arbitrary"),
        name="moe_dispatch",
    )(e0, e1, r0, r1, pad_start, counts, n_used, h)


def _expert_kernel(be_ref, nused_ref, x_ref, wg_ref, wu_ref, wd_ref, out_ref, wg_s, wu_s, wd_s):
    i = pl.program_id(0)

    @pl.when(i < nused_ref[0])
    def _():
        @pl.when((i == 0) | (be_ref[i] != be_ref[jnp.maximum(i - 1, 0)]))
        def _():
            wg_s[...] = wg_ref[0].astype(MXU_DTYPE)
            wu_s[...] = wu_ref[0].astype(MXU_DTYPE)
            wd_s[...] = wd_ref[0].astype(MXU_DTYPE)

        x = x_ref[...].astype(MXU_DTYPE)
        gate = jnp.dot(x, wg_s[...], preferred_element_type=F32)
        up = jnp.dot(x, wu_s[...], preferred_element_type=F32)
        hidden = (gate * _sigmoid(gate) * up).astype(MXU_DTYPE)
        out_ref[...] = jnp.dot(hidden, wd_s[...], preferred_element_type=F32)

    @pl.when(i >= nused_ref[0])
    def _():
        out_ref[...] = jnp.zeros(out_ref.shape, F32)


def _expert_ffn(xs, block_expert, n_used, w_gate, w_up, w_down, layer):
    n_blocks = block_expert.shape[0]
    rows = MOE_ROWS
    blk = lambda i, be, nu: (jnp.minimum(i, nu[0] - 1), 0)
    wsel = lambda i, be, nu: (layer * N_EXPERTS + be[i], 0, 0)
    grid_spec = pltpu.PrefetchScalarGridSpec(
        num_scalar_prefetch=2,
        grid=(n_blocks,),
        in_specs=[
            pl.BlockSpec((rows, D_MODEL), blk),
            pl.BlockSpec((1, D_MODEL, D_EXPERT), wsel),
            pl.BlockSpec((1, D_MODEL, D_EXPERT), wsel),
            pl.BlockSpec((1, D_EXPERT, D_MODEL), wsel),
        ],
        out_specs=pl.BlockSpec((rows, D_MODEL), lambda i, be, nu: (i, 0)),
        scratch_shapes=[pltpu.VMEM((D_MODEL, D_EXPERT), MXU_DTYPE), pltpu.VMEM((D_MODEL, D_EXPERT), MXU_DTYPE),
                        pltpu.VMEM((D_EXPERT, D_MODEL), MXU_DTYPE)],
    )
    return pl.pallas_call(
        _expert_kernel,
        out_shape=jax.ShapeDtypeStruct((n_blocks * rows, D_MODEL), F32),
        grid_spec=grid_spec,
        compiler_params=_params("arbitrary"),
        name="moe_experts",
    )(block_expert, n_used, xs, w_gate, w_up, w_down)


def _combine_ln_kernel(e0_ref, e1_ref, r0_ref, r1_ref, start_ref, rows_hbm, h_ref, gate_ref, g_ref, b_ref, o_ref,
                       buf, sem):
    tm = h_ref.shape[0]
    i = pl.program_id(0)

    def gather(tile, s):
        base = tile * tm

        def issue(r, c):
            t = base + r
            pltpu.make_async_copy(rows_hbm.at[pl.ds(start_ref[e0_ref[t]] + r0_ref[t], 1)],
                                  buf.at[s, 0, pl.ds(r, 1)], sem.at[s]).start()
            pltpu.make_async_copy(rows_hbm.at[pl.ds(start_ref[e1_ref[t]] + r1_ref[t], 1)],
                                  buf.at[s, 1, pl.ds(r, 1)], sem.at[s]).start()
            return c

        lax.fori_loop(0, tm, issue, 0, unroll=8)

    cur = i % 2

    @pl.when(i == 0)
    def _():
        gather(0, 0)

    @pl.when(i + 1 < pl.num_programs(0))
    def _():
        gather(i + 1, 1 - cur)

    for k in range(TOP_K):
        pltpu.make_async_copy(rows_hbm.at[pl.ds(0, tm)], buf.at[cur, k], sem.at[cur]).wait()
    gate = gate_ref[...]
    acc = ALPHA * h_ref[...] + gate[:, 0:1] * buf[cur, 0] + gate[:, 1:2] * buf[cur, 1]
    o_ref[...] = _layer_norm_rows(acc, g_ref[...], b_ref[...])


def _combine_ln(e0, e1, r0, r1, pad_start, rows, h, gates, g_row, b_row):
    tok = h.shape[0]
    tm = TM_COMBINE
    row = lambda i, *_: (i, 0)
    const = lambda i, *_: (0, 0)
    grid_spec = pltpu.PrefetchScalarGridSpec(
        num_scalar_prefetch=5,
        grid=(tok // tm,),
        in_specs=[
            pl.BlockSpec(memory_space=pl.ANY),
            pl.BlockSpec((tm, D_MODEL), row),
            pl.BlockSpec((tm, LANES), row),
            pl.BlockSpec((1, D_MODEL), const),
            pl.BlockSpec((1, D_MODEL), const),
        ],
        out_specs=pl.BlockSpec((tm, D_MODEL), row),
        scratch_shapes=[pltpu.VMEM((2, TOP_K, tm, D_MODEL), F32), pltpu.SemaphoreType.DMA((2,))],
    )
    return pl.pallas_call(
        _combine_ln_kernel,
        out_shape=jax.ShapeDtypeStruct((tok, D_MODEL), F32),
        grid_spec=grid_spec,
        compiler_params=_params("arbitrary"),
        name="moe_combine_ln",
    )(e0, e1, r0, r1, pad_start, rows, h, gates, g_row, b_row)


def _slot_layout(counts, n_blocks):
    padded = (counts + MOE_ROWS - 1) // MOE_ROWS * MOE_ROWS
    pad_end = jnp.cumsum(padded)
    n_used = pad_end[-1:] // MOE_ROWS
    blocks = jnp.arange(n_blocks, dtype=jnp.int32)
    first_row = jnp.minimum(blocks, n_used - 1) * MOE_ROWS
    block_expert = jnp.sum(first_row[:, None] >= pad_end[None, :], axis=1)
    return (pad_end - padded).astype(jnp.int32), block_expert.astype(jnp.int32), n_used.astype(jnp.int32)


def _moe_ln(h, w_router, b_router, w_gate, w_up, w_down, layer, g_row, b_row):
    n_blocks = h.shape[0] * TOP_K // MOE_ROWS + N_EXPERTS
    ids, gates, counts = _router(h, w_router, b_router)
    e0, e1, r0, r1 = (ids[:, c] for c in range(4))
    counts = counts[0, MOE_GROUPS:MOE_GROUPS + N_EXPERTS]
    pad_start, block_expert, n_used = _slot_layout(counts, n_blocks)
    xs = _dispatch(h, e0, e1, r0, r1, pad_start, counts, n_used, n_blocks * MOE_ROWS)
    rows = _expert_ffn(xs, block_expert, n_used, w_gate, w_up, w_down, layer)
    return _combine_ln(e0, e1, r0, r1, pad_start, rows, h, gates, g_row, b_row)


def _pad_lanes(v):
    return jnp.pad(v, (0, LANES - v.shape[0])).reshape(1, LANES)


def _even_mixer(h, bsz, seq, w_in, conv_a, conv_w, conv_b, dt_bias, a_log, d_skip, norm_w, w_out, g_row, b_row):
    w = jnp.pad(w_in, ((0, 0), (0, AB_PROJ - w_in.shape[1]))).astype(MXU_DTYPE)
    y_a, z, xbc, dt, acs, acst = _even_front(
        h, w, conv_a, conv_w, conv_b.reshape(1, -1), _pad_lanes(dt_bias), _pad_lanes(-jnp.exp(a_log)), bsz, seq)
    dskip_row = jnp.repeat(d_skip, SSM_HEAD_DIM).reshape(1, -1)
    y_b = _ssd(xbc, dt, acs, acst, z, dskip_row, norm_w.reshape(1, -1), bsz, seq)
    return _outproj_ln([y_a, y_b], h, w_out.astype(MXU_DTYPE), g_row, b_row)


def _odd_mixer(h, bsz, seq, w_in, i_bias, f_bias, hnorm_w, fox_f_bias, w_out, g_row, b_row):
    c = np.cumsum((0, MLSTM_W, MLSTM_W, MLSTM_W, MLSTM_HEADS, MLSTM_HEADS, MLSTM_W, FOX_W, FOX_W, FOX_W, FOX_HEADS))
    part = lambda j: w_in[:, c[j]:c[j + 1]]
    q, k, v, i_pre, f_pre, o_pre, fq, fk, fv, ff = (part(j) for j in range(10))
    gate_cols = jnp.concatenate([i_pre, f_pre, ff, f_pre], axis=1)
    gate_cols = jnp.pad(gate_cols, ((0, 0), (0, LANES - gate_cols.shape[1])))
    w = jnp.concatenate([q, k, v, o_pre, fq, fk, fv, gate_cols], axis=1).astype(MXU_DTYPE)
    gate_bias = _pad_lanes(jnp.concatenate([i_bias, f_bias, fox_f_bias, f_bias]))
    q, k, v, o, fq, fk, fv, gates, gates_t = _odd_front(h, w, gate_bias, bsz, seq)
    y_c = _mlstm(q, k, v, o, gates, gates_t, hnorm_w.reshape(1, -1), bsz, seq)
    y_d = _fox(fq, fk, fv, gates, gates_t, bsz, seq)
    return _outproj_ln([y_c, y_d], h, w_out.astype(MXU_DTYPE), g_row, b_row)


def kernel(x, ab_w_in, ab_conv_a, ab_conv_ssm_w, ab_conv_ssm_b, ab_dt_bias, ab_a_log, ab_d_skip, ab_norm_w, ab_w_out, cd_w_in, cd_i_bias, cd_f_bias, cd_hnorm_w, cd_fox_f_bias, cd_w_out, ln1_g, ln1_b, ln2_g, ln2_b, moe_rg_w, moe_rg_b, moe_re_w, moe_re_b, moe_w_gate, moe_w_up, moe_w_down):
    bsz, seq, d = x.shape
    h = x.reshape(bsz * seq, d)
    stack = lambda w: w.reshape((w.shape[0] * w.shape[1],) + w.shape[2:])
    w_gate, w_up, w_down = stack(moe_w_gate), stack(moe_w_up), stack(moe_w_down)
    for layer in range(DEPTH):
        j = layer // 2
        g1, b1 = ln1_g[layer].reshape(1, -1), ln1_b[layer].reshape(1, -1)
        if layer % 2 == 0:
            h = _even_mixer(h, bsz, seq, ab_w_in[j], ab_conv_a[j], ab_conv_ssm_w[j], ab_conv_ssm_b[j],
                            ab_dt_bias[j], ab_a_log[j], ab_d_skip[j], ab_norm_w[j], ab_w_out[j], g1, b1)
        else:
            h = _odd_mixer(h, bsz, seq, cd_w_in[j], cd_i_bias[j], cd_f_bias[j], cd_hnorm_w[j],
                           cd_fox_f_bias[j], cd_w_out[j], g1, b1)
        re_w = jnp.transpose(moe_re_w[layer], (1, 0, 2)).reshape(d, N_EXPERTS)
        w_router = jnp.pad(jnp.concatenate([moe_rg_w[layer], re_w], axis=1),
                           ((0, 0), (0, LANES - MOE_GROUPS - N_EXPERTS)))
        b_router = _pad_lanes(jnp.concatenate([moe_rg_b[layer], moe_re_b[layer].reshape(-1)]))
        h = _moe_ln(h, w_router, b_router, w_gate, w_up, w_down, layer,
                    ln2_g[layer].reshape(1, -1), ln2_b[layer].reshape(1, -1))
    return h.reshape(bsz, seq, d)
```

```python
import functools

import numpy as np
import jax
import jax.numpy as jnp
from jax import lax
from jax.experimental import pallas as pl
from jax.experimental.pallas import tpu as pltpu

F32 = jnp.float32
MXU_DTYPE = jnp.bfloat16
HIGHEST = lax.Precision.HIGHEST

D_MODEL = 1024
DEPTH = 4
ALPHA = (2 * DEPTH) ** 0.25
LN_EPS = 1e-5
CONV_DIM = D_MODEL // 2
CONV_WIDTH = 3
SSM_D_INNER = D_MODEL
SSM_HEAD_DIM = 64
SSM_HEADS = SSM_D_INNER // SSM_HEAD_DIM
SSM_GROUPS = 4
SSM_STATE = 64
SSM_CONV = 4
SSM_BC = SSM_GROUPS * SSM_STATE
SSM_CONV_DIM = SSM_D_INNER + 2 * SSM_BC
MLSTM_HEADS = 4
MLSTM_HEAD_DIM = D_MODEL // 8
MLSTM_W = MLSTM_HEADS * MLSTM_HEAD_DIM
FOX_HEADS = 8
FOX_HEAD_DIM = D_MODEL // 16
FOX_W = FOX_HEADS * FOX_HEAD_DIM
MOE_GROUPS = 4
EXPERTS_PER_GROUP = 8
N_EXPERTS = MOE_GROUPS * EXPERTS_PER_GROUP
TOP_K = 2
D_EXPERT = D_MODEL // 2

LANES = 128
SUBLANES = 8
VMEM_LIMIT_BYTES = 56 * 1024 * 1024

CHUNK = 128
TM_FRONT = 512
TM_OUT = 1024
TM_ROUTER = 512
TQ_FOX = 256
MOE_ROWS = 512
TM_COMBINE = 256
TM_DISPATCH = 512

AB_PROJ = 4224
FOX_AUG = FOX_HEADS * LANES
CD_FK = 5 * 512
CD_FV = CD_FK + FOX_AUG
CD_GATES = CD_FV + FOX_AUG
CD_PROJ = CD_GATES + LANES
LOG2E = 1.4426950408889634
TK_FOX = 128
FOX_ACC_ROWS = FOX_HEAD_DIM + SUBLANES
G_I, G_F, G_FOX, G_BCUM = 0, 4, 8, 16


def _params(*sem):
    return pltpu.CompilerParams(dimension_semantics=sem, vmem_limit_bytes=VMEM_LIMIT_BYTES)


def _softplus(x):
    return jnp.maximum(x, 0.0) + jnp.log(1.0 + jnp.exp(-jnp.abs(x)))


def _sigmoid(x):
    return 1.0 / (1.0 + jnp.exp(-x))


def _layer_norm_rows(v, g, b):
    mu = jnp.mean(v, axis=-1, keepdims=True)
    c = v - mu
    var = jnp.mean(c * c, axis=-1, keepdims=True)
    return c * lax.rsqrt(var + LN_EPS) * g + b


def _tril(n, block):
    i = np.arange(n)
    m = (i[:, None] >= i[None, :]) & (i[:, None] // block == i[None, :] // block)
    return jnp.asarray(m.astype(np.float32), MXU_DTYPE)


def _full(shape):
    return pl.BlockSpec(shape, lambda *_: (0,) * len(shape), pipeline_mode=pl.Buffered(1))


def _split3(x):
    narrow = lambda v: v.astype(MXU_DTYPE).astype(F32)
    x1 = narrow(x)
    x2 = narrow(x - x1)
    return x1, x2, narrow(x - x1 - x2)


def _cumsum_rows(tril, x):
    parts = jnp.dot(tril, jnp.concatenate(_split3(x), axis=1).astype(MXU_DTYPE), preferred_element_type=F32)
    return parts[:, 0:LANES] + parts[:, LANES:2 * LANES] + parts[:, 2 * LANES:3 * LANES]


def _even_front_kernel(h_ref, w_ref, ca_ref, cw_ref, cb_ref, dtb_ref, aneg_ref, tril_ref,
                       ya_ref, z_ref, xbc_ref, dt_ref, acs_ref, acst_ref,
                       ua_ext, xbc_ext):
    tm = h_ref.shape[0]
    proj = jnp.dot(h_ref[...].astype(MXU_DTYPE), w_ref[...], preferred_element_type=F32)
    b_gate = proj[:, 0:CONV_DIM]
    c_gate = proj[:, CONV_DIM:2 * CONV_DIM]
    x_a = proj[:, 2 * CONV_DIM:3 * CONV_DIM]
    z0 = 3 * CONV_DIM
    z_ref[...] = proj[:, z0:z0 + SSM_D_INNER]
    x0 = z0 + SSM_D_INNER

    @pl.when(pl.program_id(1) == 0)
    def _():
        ua_ext[0:SUBLANES, :] = jnp.zeros((SUBLANES, CONV_DIM), F32)
        xbc_ext[0:SUBLANES, :] = jnp.zeros((SUBLANES, SSM_CONV_DIM), F32)

    ua_ext[SUBLANES:SUBLANES + tm, :] = c_gate * x_a
    conv = ca_ref[0:1, :] * ua_ext[pl.ds(SUBLANES - 2, tm), :]
    for k in range(1, CONV_WIDTH):
        conv = conv + ca_ref[k:k + 1, :] * ua_ext[pl.ds(SUBLANES - (CONV_WIDTH - 1) + k, tm), :]
    ya_ref[...] = b_gate * conv
    ua_ext[0:SUBLANES, :] = ua_ext[tm:tm + SUBLANES, :]

    xbc_ext[SUBLANES:SUBLANES + tm, :] = proj[:, x0:x0 + SSM_CONV_DIM]
    conv = cb_ref[...] + cw_ref[0:1, :] * xbc_ext[pl.ds(SUBLANES - (SSM_CONV - 1), tm), :]
    for k in range(1, SSM_CONV):
        conv = conv + cw_ref[k:k + 1, :] * xbc_ext[pl.ds(SUBLANES - (SSM_CONV - 1) + k, tm), :]
    xbc_ref[...] = conv * _sigmoid(conv)
    xbc_ext[0:SUBLANES, :] = xbc_ext[tm:tm + SUBLANES, :]

    d0 = x0 + SSM_CONV_DIM
    dt = _softplus(proj[:, d0:d0 + LANES] + dtb_ref[...])
    a = dt * aneg_ref[...]
    acs = _cumsum_rows(tril_ref[...], a)
    dt_ref[...] = dt
    acs_ref[...] = acs
    acst_ref[...] = acs.T


def _even_front(h, w_in, conv_a, conv_w, conv_b, dt_bias_row, aneg_row, bsz, seq):
    tok = bsz * seq
    tm = TM_FRONT
    ns = seq // tm
    row = lambda b, s: (b * ns + s, 0)
    out_shapes = (
        jax.ShapeDtypeStruct((tok, CONV_DIM), F32),
        jax.ShapeDtypeStruct((tok, SSM_D_INNER), F32),
        jax.ShapeDtypeStruct((tok, SSM_CONV_DIM), F32),
        jax.ShapeDtypeStruct((tok, LANES), F32),
        jax.ShapeDtypeStruct((tok, LANES), F32),
        jax.ShapeDtypeStruct((LANES, tok), F32),
    )
    return pl.pallas_call(
        _even_front_kernel,
        out_shape=out_shapes,
        grid=(bsz, ns),
        in_specs=[
            pl.BlockSpec((tm, D_MODEL), row),
            _full((D_MODEL, AB_PROJ)),
            _full((CONV_WIDTH, CONV_DIM)),
            _full((SSM_CONV, SSM_CONV_DIM)),
            _full((1, SSM_CONV_DIM)),
            _full((1, LANES)),
            _full((1, LANES)),
            _full((tm, tm)),
        ],
        out_specs=(
            pl.BlockSpec((tm, CONV_DIM), row),
            pl.BlockSpec((tm, SSM_D_INNER), row),
            pl.BlockSpec((tm, SSM_CONV_DIM), row),
            pl.BlockSpec((tm, LANES), row),
            pl.BlockSpec((tm, LANES), row),
            pl.BlockSpec((LANES, tm), lambda b, s: (0, b * ns + s)),
        ),
        scratch_shapes=[
            pltpu.VMEM((tm + SUBLANES, CONV_DIM), F32),
            pltpu.VMEM((tm + SUBLANES, SSM_CONV_DIM), F32),
        ],
        compiler_params=_params("arbitrary", "arbitrary"),
        name="even_front",
    )(h, w_in, conv_a, conv_w, conv_b, dt_bias_row, aneg_row, _tril(tm, CHUNK))


def _bcast_heads(arr, n_heads, width):
    per = LANES // width
    length = arr.shape[0]
    lane = lax.broadcasted_iota(jnp.int32, (length, LANES), 1)
    outs = []
    for j in range(n_heads // per):
        v = jnp.broadcast_to(arr[:, j * per:j * per + 1], (length, LANES))
        for r in range(1, per):
            v = jnp.where(lane >= r * width, jnp.broadcast_to(arr[:, j * per + r:j * per + r + 1], (length, LANES)), v)
        outs.append(v)
    return jnp.concatenate(outs, axis=1)


def _ssd_kernel(xbc_ref, dt_ref, acs_ref, acst_ref, z_ref, dskip_ref, nw_ref, y_ref, state):
    L = CHUNK
    P = SSM_HEAD_DIM
    R = SSM_HEADS // SSM_GROUPS
    GW = R * P

    @pl.when(pl.program_id(1) == 0)
    def _():
        state[...] = jnp.zeros(state.shape, F32)

    xs = xbc_ref[:, 0:SSM_D_INNER]
    bm = xbc_ref[:, SSM_D_INNER:SSM_D_INNER + SSM_BC]
    cm = xbc_ref[:, SSM_D_INNER + SSM_BC:SSM_CONV_DIM]
    dt = dt_ref[...]
    acs = acs_ref[...]
    acst = acst_ref[...]
    a_last = acs[L - 1:L, :]

    dtx = _bcast_heads(dt, SSM_HEADS, P)
    decx = _bcast_heads(jnp.exp(a_last - acs), SSM_HEADS, P)
    expx = _bcast_heads(jnp.exp(acs), SSM_HEADS, P)
    xdt = xs * dtx
    xdec = (xdt * decx).astype(MXU_DTYPE)
    xdt_m = xdt.astype(MXU_DTYPE)
    chunk_decay = jnp.exp(jnp.broadcast_to(a_last, (SUBLANES, LANES)))
    cdx = _bcast_heads(chunk_decay, SSM_HEADS, P)[0:1, :]

    bm_t = bm.T.astype(MXU_DTYPE)
    cm_m = cm.astype(MXU_DTYPE)
    bm_m = bm.astype(MXU_DTYPE)
    row = lax.broadcasted_iota(jnp.int32, (L, L), 0)
    col = lax.broadcasted_iota(jnp.int32, (L, L), 1)
    causal = row >= col
    lane_g = lax.broadcasted_iota(jnp.int32, (L, GW), 1)

    ys = []
    for g in range(SSM_GROUPS):
        n0 = g * SSM_STATE
        c_g = cm_m[:, n0:n0 + SSM_STATE]
        cb = lax.dot_general(c_g, bm_m[:, n0:n0 + SSM_STATE], (((1,), (1,)), ((), ())),
                             preferred_element_type=F32)
        ms = []
        for r in range(R):
            hd = g * R + r
            seg = jnp.exp(jnp.where(causal, acs[:, hd:hd + 1] - acst[hd:hd + 1, :], -jnp.inf))
            ms.append((cb * seg).astype(MXU_DTYPE))
        big = jnp.dot(jnp.concatenate(ms, axis=0), xdt_m[:, g * GW:(g + 1) * GW],
                      preferred_element_type=F32)
        y_diag = big[0:L, :]
        for r in range(1, R):
            y_diag = jnp.where(lane_g >= r * P, big[r * L:(r + 1) * L, :], y_diag)
        st = state[g]
        y_off = jnp.dot(c_g, st.astype(MXU_DTYPE), preferred_element_type=F32)
        new = jnp.dot(bm_t[n0:n0 + SSM_STATE, :], xdec[:, g * GW:(g + 1) * GW],
                      preferred_element_type=F32)
        state[g] = st * cdx[:, g * GW:(g + 1) * GW] + new
        ys.append(y_diag + y_off * expx[:, g * GW:(g + 1) * GW])
    y = jnp.concatenate(ys, axis=1) + dskip_ref[...] * xs
    z = z_ref[...]
    u = y * (z * _sigmoid(z))
    y_ref[...] = u * lax.rsqrt(jnp.mean(u * u, axis=-1, keepdims=True) + LN_EPS) * nw_ref[...]


def _ssd(xbc, dt, acs, acst, z, dskip_row, normw_row, bsz, seq):
    tok = bsz * seq
    nc = seq // CHUNK
    row = lambda b, c: (b * nc + c, 0)
    return pl.pallas_call(
        _ssd_kernel,
        out_shape=jax.ShapeDtypeStruct((tok, SSM_D_INNER), F32),
        grid=(bsz, nc),
        in_specs=[
            pl.BlockSpec((CHUNK, SSM_CONV_DIM), row),
            pl.BlockSpec((CHUNK, LANES), row),
            pl.BlockSpec((CHUNK, LANES), row),
            pl.BlockSpec((LANES, CHUNK), lambda b, c: (0, b * nc + c)),
            pl.BlockSpec((CHUNK, SSM_D_INNER), row),
            _full((1, SSM_D_INNER)),
            _full((1, SSM_D_INNER)),
        ],
        out_specs=pl.BlockSpec((CHUNK, SSM_D_INNER), row),
        scratch_shapes=[pltpu.VMEM((SSM_GROUPS, SSM_STATE, SSM_D_INNER // SSM_GROUPS), F32)],
        compiler_params=_params("arbitrary", "arbitrary"),
        name="ssd_scan",
    )(xbc, dt, acs, acst, z, dskip_row, normw_row)


def _outproj_ln_kernel(*refs, widths):
    n = len(widths)
    parts = refs[:n]
    h_ref, w_ref, g_ref, b_ref, o_ref = refs[n:]
    acc = ALPHA * h_ref[...]
    off = 0
    for p, wd in zip(parts, widths):
        acc = acc + jnp.dot(p[...].astype(MXU_DTYPE), w_ref[off:off + wd, :], preferred_element_type=F32)
        off += wd
    o_ref[...] = _layer_norm_rows(acc, g_ref[...], b_ref[...])


def _outproj_ln(parts, h, w_out, g_row, b_row):
    tok = h.shape[0]
    tm = TM_OUT
    widths = tuple(p.shape[1] for p in parts)
    row = lambda i: (i, 0)
    return pl.pallas_call(
        functools.partial(_outproj_ln_kernel, widths=widths),
        out_shape=jax.ShapeDtypeStruct((tok, D_MODEL), F32),
        grid=(tok // tm,),
        in_specs=[pl.BlockSpec((tm, wd), row) for wd in widths] + [
            pl.BlockSpec((tm, D_MODEL), row),
            _full((sum(widths), D_MODEL)),
            _full((1, D_MODEL)),
            _full((1, D_MODEL)),
        ],
        out_specs=pl.BlockSpec((tm, D_MODEL), row),
        compiler_params=_params("arbitrary"),
        name="outproj_ln",
    )(*parts, h, w_out, g_row, b_row)


def _odd_front_kernel(h_ref, w_ref, gb_ref, tril_ref,
                      q_ref, k_ref, v_ref, o_ref, fq_ref, fk_ref, fvt_ref, g_ref, gt_ref, carry):
    tm = h_ref.shape[0]
    proj = jnp.dot(h_ref[...].astype(MXU_DTYPE), w_ref[...], preferred_element_type=F32)
    q_ref[...] = proj[:, 0:512]
    k_ref[...] = proj[:, 512:1024] * (MLSTM_HEAD_DIM ** -0.5)
    v_ref[...] = proj[:, 1024:1536]
    o_ref[...] = _sigmoid(proj[:, 1536:2048])
    fq_ref[...] = proj[:, 2048:2560] * (FOX_HEAD_DIM ** -0.5 * LOG2E)

    @pl.when(pl.program_id(1) == 0)
    def _():
        carry[...] = jnp.zeros(carry.shape, F32)

    raw = proj[:, CD_GATES:CD_GATES + LANES] + gb_ref[...]
    lane = lax.broadcasted_iota(jnp.int32, (tm, LANES), 1)
    g = jnp.where(lane < G_F, raw, -_softplus(-raw))
    prev = carry[0:1, :]
    glob = _cumsum_rows(tril_ref[...], g) + prev
    before = []
    for c in range(tm // CHUNK):
        before.append(jnp.broadcast_to(prev, (CHUNK, LANES)))
        prev = glob[(c + 1) * CHUNK - 1:(c + 1) * CHUNK, :]
    carry[...] = jnp.broadcast_to(prev, carry.shape)
    local = glob - jnp.concatenate(before, axis=0)
    out = jnp.where((lane >= G_FOX) & (lane < G_BCUM), glob, jnp.where(lane >= G_BCUM, local, g))
    g_ref[...] = out
    gt_ref[...] = out.T

    is_bias = (lane >= FOX_HEAD_DIM) & (lane < FOX_HEAD_DIM + 3)
    for hd in range(FOX_HEADS):
        c1, c2, c3 = _split3(out[:, G_FOX + hd:G_FOX + hd + 1] * (-LOG2E))
        bias = jnp.where(lane == FOX_HEAD_DIM, c1, jnp.where(lane == FOX_HEAD_DIM + 1, c2, c3))
        k_h = proj[:, CD_FK + hd * LANES:CD_FK + (hd + 1) * LANES]
        fk_ref[:, hd * LANES:(hd + 1) * LANES] = jnp.where(is_bias, bias, k_h).astype(MXU_DTYPE)
    ones_lane = jnp.where(lax.broadcasted_iota(jnp.int32, (tm, FOX_AUG), 1) % LANES == FOX_HEAD_DIM, 1.0, 0.0)
    fvt_ref[...] = (proj[:, CD_FV:CD_FV + FOX_AUG] + ones_lane).T.astype(MXU_DTYPE)


def _odd_front(h, w_in, gate_bias_row, bsz, seq):
    tok = bsz * seq
    tm = TM_FRONT
    ns = seq // tm
    row = lambda b, s: (b * ns + s, 0)
    col = lambda b, s: (0, b * ns + s)
    wide = jax.ShapeDtypeStruct((tok, 512), F32)
    return pl.pallas_call(
        _odd_front_kernel,
        out_shape=(wide,) * 5 + (jax.ShapeDtypeStruct((tok, FOX_AUG), MXU_DTYPE),
                                 jax.ShapeDtypeStruct((FOX_AUG, tok), MXU_DTYPE),
                                 jax.ShapeDtypeStruct((tok, LANES), F32),
                                 jax.ShapeDtypeStruct((LANES, tok), F32)),
        grid=(bsz, ns),
        in_specs=[
            pl.BlockSpec((tm, D_MODEL), row),
            _full((D_MODEL, CD_PROJ)),
            _full((1, LANES)),
            _full((tm, tm)),
        ],
        out_specs=(pl.BlockSpec((tm, 512), row),) * 5 + (
            pl.BlockSpec((tm, FOX_AUG), row),
            pl.BlockSpec((FOX_AUG, tm), col),
            pl.BlockSpec((tm, LANES), row),
            pl.BlockSpec((LANES, tm), col),
        ),
        scratch_shapes=[pltpu.VMEM((SUBLANES, LANES), F32)],
        compiler_params=_params("arbitrary", "arbitrary"),
        name="odd_front",
    )(h, w_in, gate_bias_row, _tril(tm, tm))


def _mlstm_kernel(q_ref, k_ref, v_ref, o_ref, g_ref, gt_ref, nw_ref, y_ref, c_state, m_state):
    L = CHUNK
    DH = MLSTM_HEAD_DIM

    @pl.when(pl.program_id(1) == 0)
    def _():
        c_state[...] = jnp.zeros(c_state.shape, F32)
        m_state[...] = jnp.zeros(m_state.shape, F32)

    gates = g_ref[...]
    gates_t = gt_ref[...]
    row = lax.broadcasted_iota(jnp.int32, (L, L), 0)
    col = lax.broadcasted_iota(jnp.int32, (L, L), 1)
    causal = row >= col
    ones_col = jnp.where(lax.broadcasted_iota(jnp.int32, (L, DH), 1) == 0, 1.0, 0.0)

    for hd in range(MLSTM_HEADS):
        sl = slice(hd * DH, (hd + 1) * DH)
        q = q_ref[:, sl].astype(MXU_DTYPE)
        k = k_ref[:, sl]
        v_ext = jnp.concatenate([v_ref[:, sl], ones_col], axis=1).astype(MXU_DTYPE)
        b_col = gates[:, G_BCUM + hd:G_BCUM + hd + 1]
        i_col = gates[:, G_I + hd:G_I + hd + 1]
        b_row = gates_t[G_BCUM + hd:G_BCUM + hd + 1, :]
        i_row = gates_t[G_I + hd:G_I + hd + 1, :]
        m_prev = m_state[hd:hd + 1, 0:1]
        c_ext = c_state[hd]

        d_mat = jnp.where(causal, b_col - b_row + i_row, -jnp.inf)
        inter = b_col + m_prev
        m_t = jnp.maximum(jnp.max(d_mat, axis=-1, keepdims=True), inter)
        s_qk = lax.dot_general(q, k.astype(MXU_DTYPE), (((1,), (1,)), ((), ())), preferred_element_type=F32)
        w_qk = s_qk * jnp.exp(d_mat - m_t)
        s_inter = jnp.exp(inter - m_t)
        num_ext = (jnp.dot(w_qk.astype(MXU_DTYPE), v_ext, preferred_element_type=F32)
                   + s_inter * jnp.dot(q, c_ext.astype(MXU_DTYPE), preferred_element_type=F32))
        den = num_ext[:, DH:DH + 1]
        hval = num_ext[:, 0:DH] / jnp.maximum(jnp.abs(den), jnp.exp(-m_t))

        b_last = b_col[L - 1:L, :]
        g_log = b_last - b_col + i_col
        m_new = jnp.maximum(b_last + m_prev, jnp.max(g_log, axis=0, keepdims=True))
        w_k = jnp.exp(g_log - m_new)
        decay = jnp.exp(b_last + m_prev - m_new)
        kw_t = (k * w_k).T.astype(MXU_DTYPE)
        c_state[hd] = decay * c_ext + jnp.dot(kw_t, v_ext, preferred_element_type=F32)
        m_state[hd:hd + 1, :] = jnp.broadcast_to(m_new, (1, LANES))

        mu = jnp.mean(hval, axis=-1, keepdims=True)
        cen = hval - mu
        var = jnp.mean(cen * cen, axis=-1, keepdims=True)
        y_ref[:, sl] = o_ref[:, sl] * (cen * lax.rsqrt(var + LN_EPS) * nw_ref[:, sl])


def _mlstm(q, k, v, o, gates, gates_t, hnorm_row, bsz, seq):
    tok = bsz * seq
    nc = seq // CHUNK
    row = lambda b, c: (b * nc + c, 0)
    wide = pl.BlockSpec((CHUNK, MLSTM_W), row)
    return pl.pallas_call(
        _mlstm_kernel,
        out_shape=jax.ShapeDtypeStruct((tok, MLSTM_W), F32),
        grid=(bsz, nc),
        in_specs=[wide, wide, wide, wide,
                  pl.BlockSpec((CHUNK, LANES), row),
                  pl.BlockSpec((LANES, CHUNK), lambda b, c: (0, b * nc + c)),
                  _full((1, MLSTM_W))],
        out_specs=wide,
        scratch_shapes=[pltpu.VMEM((MLSTM_HEADS, MLSTM_HEAD_DIM, 2 * MLSTM_HEAD_DIM), F32),
                        pltpu.VMEM((SUBLANES, LANES), F32)],
        compiler_params=_params("arbitrary", "arbitrary"),
        name="mlstm_scan",
    )(q, k, v, o, gates, gates_t, hnorm_row)


def _fox_kernel(q_ref, k_ref, vt_ref, gt_ref, y_ref, *scratch):
    acc_refs = scratch[:FOX_HEADS]
    qa_ref = scratch[FOX_HEADS]
    tq = q_ref.shape[0]
    tk = TK_FOX
    dh = FOX_HEAD_DIM
    qi = pl.program_id(1)
    q_t = q_ref[...].T
    bias_rows = jnp.where(lax.broadcasted_iota(jnp.int32, (LANES - dh, tq), 0) < 3, 1.0, 0.0)
    for hd in range(FOX_HEADS):
        qa_ref[hd] = jnp.concatenate([q_t[hd * dh:(hd + 1) * dh, :], bias_rows], axis=0).astype(MXU_DTYPE)
        acc_refs[hd][...] = jnp.zeros(acc_refs[hd].shape, F32)
    cq = gt_ref[G_FOX:G_FOX + FOX_HEADS, :] * LOG2E
    key_pos = lax.broadcasted_iota(jnp.int32, (tk, tq), 0)
    qry_pos = lax.broadcasted_iota(jnp.int32, (tk, tq), 1) + qi * tq
    n_full = qi * (tq // tk)

    def block(j, ms, masked):
        k0 = pl.multiple_of(j * tk, tk)
        out = []
        for hd in range(FOX_HEADS):
            hs = slice(hd * LANES, (hd + 1) * LANES)
            s = jnp.dot(k_ref[pl.ds(k0, tk), hs], qa_ref[hd], preferred_element_type=F32)
            if masked:
                s = jnp.where(key_pos + k0 <= qry_pos, s, -jnp.inf)
            cq_h = cq[hd:hd + 1, :]
            m_new = jnp.maximum(ms[hd], jnp.max(s, axis=0, keepdims=True) + cq_h)
            p = jnp.exp2(s - (m_new - cq_h))
            pv = jnp.dot(vt_ref[hs, pl.ds(k0, tk)], p.astype(MXU_DTYPE), preferred_element_type=F32)
            acc_refs[hd][...] = jnp.exp2(ms[hd] - m_new) * acc_refs[hd][...] + pv[0:FOX_ACC_ROWS, :]
            out.append(m_new)
        return tuple(out)

    ms = tuple(jnp.full((1, tq), -jnp.inf, F32) for _ in range(FOX_HEADS))
    ms = lax.fori_loop(0, n_full, functools.partial(block, masked=False), ms)
    for d in range(tq // tk):
        ms = block(n_full + d, ms, masked=True)
    outs = []
    for hd in range(FOX_HEADS):
        acc = acc_refs[hd][...]
        outs.append(acc[0:dh, :] / acc[dh:dh + 1, :])
    y_ref[...] = jnp.concatenate(outs, axis=0).T


def _fox(fq, fk_aug, fvt_aug, gates_t, bsz, seq):
    tok = bsz * seq
    tq = TQ_FOX
    nq = seq // tq
    return pl.pallas_call(
        _fox_kernel,
        out_shape=jax.ShapeDtypeStruct((tok, FOX_W), F32),
        grid=(bsz, nq),
        in_specs=[
            pl.BlockSpec((tq, FOX_W), lambda b, i: (b * nq + i, 0)),
            pl.BlockSpec((seq, FOX_AUG), lambda b, i: (b, 0)),
            pl.BlockSpec((FOX_AUG, seq), lambda b, i: (0, b)),
            pl.BlockSpec((LANES, tq), lambda b, i: (0, b * nq + i)),
        ],
        out_specs=pl.BlockSpec((tq, FOX_W), lambda b, i: (b * nq + i, 0)),
        scratch_shapes=[pltpu.VMEM((FOX_ACC_ROWS, tq), F32)] * FOX_HEADS
                       + [pltpu.VMEM((FOX_HEADS, LANES, tq), MXU_DTYPE)],
        compiler_params=_params("arbitrary", "arbitrary"),
        name="fox_attention",
    )(fq, fk_aug, fvt_aug, gates_t)


def _router_kernel(h_ref, w_ref, b_ref, stril_ref, id_ref, gate_ref, cnt_ref, carry):
    tm = h_ref.shape[0]

    @pl.when(pl.program_id(0) == 0)
    def _():
        carry[...] = jnp.zeros(carry.shape, F32)

    logits = jnp.dot(h_ref[...], w_ref[...], precision=HIGHEST, preferred_element_type=F32) + b_ref[...]
    lane = lax.broadcasted_iota(jnp.int32, (tm, LANES), 1).astype(F32)
    neg = -jnp.inf
    first = lambda hit: jnp.min(jnp.where(hit, lane, float(LANES)), axis=-1, keepdims=True)
    gl = jnp.where(lane < MOE_GROUPS, logits, neg)
    g_max = jnp.max(gl, axis=-1, keepdims=True)
    g_idx = first(gl == g_max)
    p_group = 1.0 / jnp.sum(jnp.exp(gl - g_max), axis=-1, keepdims=True)
    e_lo = MOE_GROUPS + g_idx * EXPERTS_PER_GROUP
    el = jnp.where((lane >= e_lo) & (lane < e_lo + EXPERTS_PER_GROUP), logits, neg)
    v1 = jnp.max(el, axis=-1, keepdims=True)
    i1 = first(el == v1)
    el2 = jnp.where(lane == i1, neg, el)
    v2 = jnp.max(el2, axis=-1, keepdims=True)
    i2 = first(el2 == v2)
    t = jnp.exp(v2 - v1)
    w1 = 1.0 / (1.0 + t)
    gate_ref[...] = jnp.where(lane == 0.0, p_group * w1, jnp.where(lane == 1.0, p_group * (t * w1), 0.0))

    hit1 = lane == i1
    hit2 = lane == i2
    sent = jnp.where(hit1 | hit2, 1.0, 0.0)
    before = jnp.dot(stril_ref[...], sent.astype(jnp.bfloat16), preferred_element_type=F32) + carry[0:1, :]
    r1 = jnp.sum(jnp.where(hit1, before, 0.0), axis=-1, keepdims=True)
    r2 = jnp.sum(jnp.where(hit2, before, 0.0), axis=-1, keepdims=True)
    total = before[tm - 1:tm, :] + sent[tm - 1:tm, :]
    carry[...] = jnp.broadcast_to(total, carry.shape)
    cnt_ref[...] = jnp.broadcast_to(total, cnt_ref.shape).astype(jnp.int32)
    ids = jnp.where(lane == 0.0, i1 - MOE_GROUPS, jnp.where(lane == 1.0, i2 - MOE_GROUPS,
                    jnp.where(lane == 2.0, r1, jnp.where(lane == 3.0, r2, 0.0))))
    id_ref[...] = ids.astype(jnp.int32)


def _router(h, w_router, b_router):
    tok = h.shape[0]
    tm = TM_ROUTER
    row = lambda i: (i, 0)
    idx = np.arange(tm)
    stril = jnp.asarray((idx[:, None] > idx[None, :]).astype(np.float32), jnp.bfloat16)
    return pl.pallas_call(
        _router_kernel,
        out_shape=(jax.ShapeDtypeStruct((tok, LANES), jnp.int32), jax.ShapeDtypeStruct((tok, LANES), F32),
                   jax.ShapeDtypeStruct((SUBLANES, LANES), jnp.int32)),
        grid=(tok // tm,),
        in_specs=[pl.BlockSpec((tm, D_MODEL), row), _full((D_MODEL, LANES)), _full((1, LANES)), _full((tm, tm))],
        out_specs=(pl.BlockSpec((tm, LANES), row), pl.BlockSpec((tm, LANES), row), _full((SUBLANES, LANES))),
        scratch_shapes=[pltpu.VMEM((SUBLANES, LANES), F32)],
        compiler_params=_params("arbitrary"),
        name="moe_router",
    )(h, w_router, b_router, stril)


def _pad_pieces(n):
    return tuple(1 << b for b in range((SUBLANES - 1).bit_length(), (n - 1).bit_length()))


def _dispatch_kernel(e0_ref, e1_ref, r0_ref, r1_ref, start_ref, cnt_ref, nused_ref, h_ref, xs_hbm, zeros, sem,
                     pad_sem):
    tm = h_ref.shape[0]
    i = pl.program_id(0)
    base = i * tm

    def row_copy(r, slot):
        return pltpu.make_async_copy(h_ref.at[pl.ds(r, 1)], xs_hbm.at[pl.ds(slot, 1)], sem)

    def issue(r, c):
        t = base + r
        row_copy(r, start_ref[e0_ref[t]] + r0_ref[t]).start()
        row_copy(r, start_ref[e1_ref[t]] + r1_ref[t]).start()
        return c

    lax.fori_loop(0, tm, issue, 0, unroll=8)

    def pad_copies(e, fn):
        cnt = cnt_ref[e]
        n_pad = (MOE_ROWS - cnt % MOE_ROWS) % MOE_ROWS
        first = start_ref[e] + cnt
        head = n_pad % SUBLANES
        for j in range(SUBLANES - 1):
            @pl.when(j < head)
            def _(j=j):
                fn(pltpu.make_async_copy(zeros.at[pl.ds(0, 1)], xs_hbm.at[pl.ds(first + j, 1)], pad_sem))
        for piece in _pad_pieces(MOE_ROWS):
            @pl.when((n_pad & piece) != 0)
            def _(piece=piece):
                off = pl.multiple_of(first + head + (n_pad & (piece - SUBLANES)), SUBLANES)
                fn(pltpu.make_async_copy(zeros.at[pl.ds(0, piece)], xs_hbm.at[pl.ds(off, piece)], pad_sem))

    def tail_copies(blk, fn):
        for part in range(MOE_ROWS // zeros.shape[0]):
            off = pl.multiple_of(blk * MOE_ROWS + part * zeros.shape[0], SUBLANES)
            fn(pltpu.make_async_copy(zeros, xs_hbm.at[pl.ds(off, zeros.shape[0])], pad_sem))

    @pl.when(i == 0)
    def _():
        zeros[...] = jnp.zeros(zeros.shape, F32)
        n_blocks = xs_hbm.shape[0] // MOE_ROWS

        def start(e, c):
            pad_copies(e, lambda cp: cp.start())
            return c

        def wait(e, c):
            pad_copies(e, lambda cp: cp.wait())
            return c

        def tail_start(blk, c):
            tail_copies(blk, lambda cp: cp.start())
            return c

        def tail_wait(blk, c):
            tail_copies(blk, lambda cp: cp.wait())
            return c

        lax.fori_loop(0, N_EXPERTS, start, 0)
        lax.fori_loop(nused_ref[0], n_blocks, tail_start, 0)
        lax.fori_loop(0, N_EXPERTS, wait, 0)
        lax.fori_loop(nused_ref[0], n_blocks, tail_wait, 0)

    for _ in range(TOP_K):
        pltpu.make_async_copy(h_ref, xs_hbm.at[pl.ds(0, tm)], sem).wait()


def _dispatch(h, e0, e1, r0, r1, pad_start, counts, n_used, n_slots):
    tok = h.shape[0]
    tm = TM_DISPATCH
    grid_spec = pltpu.PrefetchScalarGridSpec(
        num_scalar_prefetch=7,
        grid=(tok // tm,),
        in_specs=[pl.BlockSpec((tm, D_MODEL), lambda i, *_: (i, 0))],
        out_specs=pl.BlockSpec(memory_space=pl.ANY),
        scratch_shapes=[pltpu.VMEM((MOE_ROWS // 2, D_MODEL), F32),
                        pltpu.SemaphoreType.DMA(()), pltpu.SemaphoreType.DMA(())],
    )
    return pl.pallas_call(
        _dispatch_kernel,
        out_shape=jax.ShapeDtypeStruct((n_slots, D_MODEL), F32),
        grid_spec=grid_spec,
        compiler_params=_params("arbitrary"),
        name="moe_dispatch",
    )(e0, e1, r0, r1, pad_start, counts, n_used, h)


def _expert_kernel(be_ref, nused_ref, x_ref, wg_ref, wu_ref, wd_ref, out_ref, wg_s, wu_s, wd_s):
    i = pl.program_id(0)

    @pl.when(i < nused_ref[0])
    def _():
        @pl.when((i == 0) | (be_ref[i] != be_ref[jnp.maximum(i - 1, 0)]))
        def _():
            wg_s[...] = wg_ref[0].astype(MXU_DTYPE)
            wu_s[...] = wu_ref[0].astype(MXU_DTYPE)
            wd_s[...] = wd_ref[0].astype(MXU_DTYPE)

        x = x_ref[...].astype(MXU_DTYPE)
        gate = jnp.dot(x, wg_s[...], preferred_element_type=F32)
        up = jnp.dot(x, wu_s[...], preferred_element_type=F32)
        hidden = (gate * _sigmoid(gate) * up).astype(MXU_DTYPE)
        out_ref[...] = jnp.dot(hidden, wd_s[...], preferred_element_type=F32)

    @pl.when(i >= nused_ref[0])
    def _():
        out_ref[...] = jnp.zeros(out_ref.shape, F32)


def _expert_ffn(xs, block_expert, n_used, w_gate, w_up, w_down, layer):
    n_blocks = block_expert.shape[0]
    rows = MOE_ROWS
    blk = lambda i, be, nu: (jnp.maximum(jnp.minimum(i, nu[0] - 1), 0), 0)
    wsel = lambda i, be, nu: (layer * N_EXPERTS + be[i], 0, 0)
    grid_spec = pltpu.PrefetchScalarGridSpec(
        num_scalar_prefetch=2,
        grid=(n_blocks,),
        in_specs=[
            pl.BlockSpec((rows, D_MODEL), blk),
            pl.BlockSpec((1, D_MODEL, D_EXPERT), wsel),
            pl.BlockSpec((1, D_MODEL, D_EXPERT), wsel),
            pl.BlockSpec((1, D_EXPERT, D_MODEL), wsel),
        ],
        out_specs=pl.BlockSpec((rows, D_MODEL), lambda i, be, nu: (i, 0)),
        scratch_shapes=[pltpu.VMEM((D_MODEL, D_EXPERT), MXU_DTYPE), pltpu.VMEM((D_MODEL, D_EXPERT), MXU_DTYPE),
                        pltpu.VMEM((D_EXPERT, D_MODEL), MXU_DTYPE)],
    )
    return pl.pallas_call(
        _expert_kernel,
        out_shape=jax.ShapeDtypeStruct((n_blocks * rows, D_MODEL), F32),
        grid_spec=grid_spec,
        compiler_params=_params("arbitrary"),
        name="moe_experts",
    )(block_expert, n_used, xs, w_gate, w_up, w_down)


def _combine_ln_kernel(e0_ref, e1_ref, r0_ref, r1_ref, start_ref, rows_hbm, h_ref, gate_ref, g_ref, b_ref, o_ref,
                       buf, sem):
    tm = h_ref.shape[0]
    i = pl.program_id(0)

    def gather(tile, s):
        base = tile * tm

        def issue(r, c):
            t = base + r
            pltpu.make_async_copy(rows_hbm.at[pl.ds(start_ref[e0_ref[t]] + r0_ref[t], 1)],
                                  buf.at[s, 0, pl.ds(r, 1)], sem.at[s]).start()
            pltpu.make_async_copy(rows_hbm.at[pl.ds(start_ref[e1_ref[t]] + r1_ref[t], 1)],
                                  buf.at[s, 1, pl.ds(r, 1)], sem.at[s]).start()
            return c

        lax.fori_loop(0, tm, issue, 0, unroll=8)

    cur = i % 2

    @pl.when(i == 0)
    def _():
        gather(0, 0)

    @pl.when(i + 1 < pl.num_programs(0))
    def _():
        gather(i + 1, 1 - cur)

    for k in range(TOP_K):
        pltpu.make_async_copy(rows_hbm.at[pl.ds(0, tm)], buf.at[cur, k], sem.at[cur]).wait()
    gate = gate_ref[...]
    acc = ALPHA * h_ref[...] + gate[:, 0:1] * buf[cur, 0] + gate[:, 1:2] * buf[cur, 1]
    o_ref[...] = _layer_norm_rows(acc, g_ref[...], b_ref[...])


def _combine_ln(e0, e1, r0, r1, pad_start, rows, h, gates, g_row, b_row):
    tok = h.shape[0]
    tm = TM_COMBINE
    row = lambda i, *_: (i, 0)
    const = lambda i, *_: (0, 0)
    grid_spec = pltpu.PrefetchScalarGridSpec(
        num_scalar_prefetch=5,
        grid=(tok // tm,),
        in_specs=[
            pl.BlockSpec(memory_space=pl.ANY),
            pl.BlockSpec((tm, D_MODEL), row),
            pl.BlockSpec((tm, LANES), row),
            pl.BlockSpec((1, D_MODEL), const),
            pl.BlockSpec((1, D_MODEL), const),
        ],
        out_specs=pl.BlockSpec((tm, D_MODEL), row),
        scratch_shapes=[pltpu.VMEM((2, TOP_K, tm, D_MODEL), F32), pltpu.SemaphoreType.DMA((2,))],
    )
    return pl.pallas_call(
        _combine_ln_kernel,
        out_shape=jax.ShapeDtypeStruct((tok, D_MODEL), F32),
        grid_spec=grid_spec,
        compiler_params=_params("arbitrary"),
        name="moe_combine_ln",
    )(e0, e1, r0, r1, pad_start, rows, h, gates, g_row, b_row)


def _slot_layout(counts, n_blocks):
    padded = (counts + MOE_ROWS - 1) // MOE_ROWS * MOE_ROWS
    pad_end = jnp.cumsum(padded)
    n_used = pad_end[-1:] // MOE_ROWS
    blocks = jnp.arange(n_blocks, dtype=jnp.int32)
    first_row = jnp.minimum(blocks, n_used - 1) * MOE_ROWS
    block_expert = jnp.sum(first_row[:, None] >= pad_end[None, :], axis=1)
    return (pad_end - padded).astype(jnp.int32), block_expert.astype(jnp.int32), n_used.astype(jnp.int32)


def _moe_ln(h, w_router, b_router, w_gate, w_up, w_down, layer, g_row, b_row):
    n_blocks = h.shape[0] * TOP_K // MOE_ROWS + N_EXPERTS
    ids, gates, counts = _router(h, w_router, b_router)
    e0, e1, r0, r1 = (ids[:, c] for c in range(4))
    counts = counts[0, MOE_GROUPS:MOE_GROUPS + N_EXPERTS]
    pad_start, block_expert, n_used = _slot_layout(counts, n_blocks)
    xs = _dispatch(h, e0, e1, r0, r1, pad_start, counts, n_used, n_blocks * MOE_ROWS)
    rows = _expert_ffn(xs, block_expert, n_used, w_gate, w_up, w_down, layer)
    return _combine_ln(e0, e1, r0, r1, pad_start, rows, h, gates, g_row, b_row)


def _pad_lanes(v):
    return jnp.pad(v, (0, LANES - v.shape[0])).reshape(1, LANES)


def _even_mixer(h, bsz, seq, w_in, conv_a, conv_w, conv_b, dt_bias, a_log, d_skip, norm_w, w_out, g_row, b_row):
    w = jnp.pad(w_in, ((0, 0), (0, AB_PROJ - w_in.shape[1]))).astype(MXU_DTYPE)
    y_a, z, xbc, dt, acs, acst = _even_front(
        h, w, conv_a, conv_w, conv_b.reshape(1, -1), _pad_lanes(dt_bias), _pad_lanes(-jnp.exp(a_log)), bsz, seq)
    dskip_row = jnp.repeat(d_skip, SSM_HEAD_DIM).reshape(1, -1)
    y_b = _ssd(xbc, dt, acs, acst, z, dskip_row, norm_w.reshape(1, -1), bsz, seq)
    return _outproj_ln([y_a, y_b], h, w_out.astype(MXU_DTYPE), g_row, b_row)


def _odd_mixer(h, bsz, seq, w_in, i_bias, f_bias, hnorm_w, fox_f_bias, w_out, g_row, b_row):
    c = np.cumsum((0, MLSTM_W, MLSTM_W, MLSTM_W, MLSTM_HEADS, MLSTM_HEADS, MLSTM_W, FOX_W, FOX_W, FOX_W, FOX_HEADS))
    part = lambda j: w_in[:, c[j]:c[j + 1]]
    q, k, v, i_pre, f_pre, o_pre, fq, fk, fv, ff = (part(j) for j in range(10))
    gate_cols = jnp.concatenate([i_pre, f_pre, ff, f_pre], axis=1)
    gate_cols = jnp.pad(gate_cols, ((0, 0), (0, LANES - gate_cols.shape[1])))
    spread = lambda m: jnp.pad(m.reshape(-1, FOX_HEADS, FOX_HEAD_DIM),
                               ((0, 0), (0, 0), (0, LANES - FOX_HEAD_DIM))).reshape(-1, FOX_AUG)
    w = jnp.concatenate([q, k, v, o_pre, fq, spread(fk), spread(fv), gate_cols], axis=1).astype(MXU_DTYPE)
    gate_bias = _pad_lanes(jnp.concatenate([i_bias, f_bias, fox_f_bias, f_bias]))
    q, k, v, o, fq, fk_aug, fvt_aug, gates, gates_t = _odd_front(h, w, gate_bias, bsz, seq)
    y_c = _mlstm(q, k, v, o, gates, gates_t, hnorm_w.reshape(1, -1), bsz, seq)
    y_d = _fox(fq, fk_aug, fvt_aug, gates_t, bsz, seq)
    return _outproj_ln([y_c, y_d], h, w_out.astype(MXU_DTYPE), g_row, b_row)


def kernel(x, ab_w_in, ab_conv_a, ab_conv_ssm_w, ab_conv_ssm_b, ab_dt_bias, ab_a_log, ab_d_skip, ab_norm_w, ab_w_out, cd_w_in, cd_i_bias, cd_f_bias, cd_hnorm_w, cd_fox_f_bias, cd_w_out, ln1_g, ln1_b, ln2_g, ln2_b, moe_rg_w, moe_rg_b, moe_re_w, moe_re_b, moe_w_gate, moe_w_up, moe_w_down):
    bsz, seq, d = x.shape
    h = x.reshape(bsz * seq, d)
    stack = lambda w: w.reshape((w.shape[0] * w.shape[1],) + w.shape[2:])
    w_gate, w_up, w_down = stack(moe_w_gate), stack(moe_w_up), stack(moe_w_down)
    for layer in range(DEPTH):
        j = layer // 2
        g1, b1 = ln1_g[layer].reshape(1, -1), ln1_b[layer].reshape(1, -1)
        if layer % 2 == 0:
            h = _even_mixer(h, bsz, seq, ab_w_in[j], ab_conv_a[j], ab_conv_ssm_w[j], ab_conv_ssm_b[j],
                            ab_dt_bias[j], ab_a_log[j], ab_d_skip[j], ab_norm_w[j], ab_w_out[j], g1, b1)
        else:
            h = _odd_mixer(h, bsz, seq, cd_w_in[j], cd_i_bias[j], cd_f_bias[j], cd_hnorm_w[j],
                           cd_fox_f_bias[j], cd_w_out[j], g1, b1)
        re_w = jnp.transpose(moe_re_w[layer], (1, 0, 2)).reshape(d, N_EXPERTS)
        w_router = jnp.pad(jnp.concatenate([moe_rg_w[layer], re_w], axis=1),
                           ((0, 0), (0, LANES - MOE_GROUPS - N_EXPERTS)))
        b_router = _pad_lanes(jnp.concatenate([moe_rg_b[layer], moe_re_b[layer].reshape(-1)]))
        h = _moe_ln(h, w_router, b_router, w_gate, w_up, w_down, layer,
                    ln2_g[layer].reshape(1, -1), ln2_b[layer].reshape(1, -1))
    return h.reshape(bsz, seq, d)
```

```python
import functools

import numpy as np
import jax
import jax.numpy as jnp
from jax import lax
from jax.experimental import pallas as pl
from jax.experimental.pallas import tpu as pltpu

F32 = jnp.float32
MXU_DTYPE = jnp.bfloat16
HIGHEST = lax.Precision.HIGHEST

D_MODEL = 1024
DEPTH = 4
ALPHA = (2 * DEPTH) ** 0.25
LN_EPS = 1e-5
CONV_DIM = D_MODEL // 2
CONV_WIDTH = 3
SSM_D_INNER = D_MODEL
SSM_HEAD_DIM = 64
SSM_HEADS = SSM_D_INNER // SSM_HEAD_DIM
SSM_GROUPS = 4
SSM_STATE = 64
SSM_CONV = 4
SSM_BC = SSM_GROUPS * SSM_STATE
SSM_CONV_DIM = SSM_D_INNER + 2 * SSM_BC
MLSTM_HEADS = 4
MLSTM_HEAD_DIM = D_MODEL // 8
MLSTM_W = MLSTM_HEADS * MLSTM_HEAD_DIM
FOX_HEADS = 8
FOX_HEAD_DIM = D_MODEL // 16
FOX_W = FOX_HEADS * FOX_HEAD_DIM
MOE_GROUPS = 4
EXPERTS_PER_GROUP = 8
N_EXPERTS = MOE_GROUPS * EXPERTS_PER_GROUP
TOP_K = 2
D_EXPERT = D_MODEL // 2

LANES = 128
SUBLANES = 8
VMEM_LIMIT_BYTES = 56 * 1024 * 1024

CHUNK = 128
TM_FRONT = 512
TM_OUT = 1024
TM_ROUTER = 512
TQ_FOX = 256
MOE_ROWS = 512
TM_COMBINE = 256
TM_DISPATCH = 512

AB_PROJ = 4224
FOX_AUG = FOX_HEADS * LANES
CD_FK = 5 * 512
CD_FV = CD_FK + FOX_AUG
CD_GATES = CD_FV + FOX_AUG
CD_PROJ = CD_GATES + LANES
LOG2E = 1.4426950408889634
TK_FOX = 128
FOX_ACC_ROWS = FOX_HEAD_DIM + SUBLANES
G_I, G_F, G_FOX, G_BCUM = 0, 4, 8, 16


def _params(*sem):
    return pltpu.CompilerParams(dimension_semantics=sem, vmem_limit_bytes=VMEM_LIMIT_BYTES)


def _softplus(x):
    return jnp.maximum(x, 0.0) + jnp.log(1.0 + jnp.exp(-jnp.abs(x)))


def _sigmoid(x):
    return 1.0 / (1.0 + jnp.exp(-x))


def _layer_norm_rows(v, g, b):
    mu = jnp.mean(v, axis=-1, keepdims=True)
    c = v - mu
    var = jnp.mean(c * c, axis=-1, keepdims=True)
    return c * lax.rsqrt(var + LN_EPS) * g + b


def _tril(n, block):
    i = np.arange(n)
    m = (i[:, None] >= i[None, :]) & (i[:, None] // block == i[None, :] // block)
    return jnp.asarray(m.astype(np.float32), MXU_DTYPE)


def _full(shape):
    return pl.BlockSpec(shape, lambda *_: (0,) * len(shape), pipeline_mode=pl.Buffered(1))


ROW_TILE = (D_MODEL // LANES, LANES)


def _tiles_to_rows(ref):
    return jnp.concatenate([ref[:, s, :] for s in range(ROW_TILE[0])], axis=1)


def _rows_to_tiles(ref, val):
    for s in range(ROW_TILE[0]):
        ref[:, s, :] = val[:, s * LANES:(s + 1) * LANES]


def _split3(x):
    narrow = lambda v: v.astype(MXU_DTYPE).astype(F32)
    x1 = narrow(x)
    x2 = narrow(x - x1)
    return x1, x2, narrow(x - x1 - x2)


def _cumsum_rows(tril, x):
    parts = jnp.dot(tril, jnp.concatenate(_split3(x), axis=1).astype(MXU_DTYPE), preferred_element_type=F32)
    return parts[:, 0:LANES] + parts[:, LANES:2 * LANES] + parts[:, 2 * LANES:3 * LANES]


def _even_front_kernel(h_ref, w_ref, ca_ref, cw_ref, cb_ref, dtb_ref, aneg_ref, tril_ref,
                       ya_ref, z_ref, xbc_ref, dt_ref, acs_ref, acst_ref,
                       ua_ext, xbc_ext):
    tm = h_ref.shape[0]
    proj = jnp.dot(h_ref[...].astype(MXU_DTYPE), w_ref[...], preferred_element_type=F32)
    b_gate = proj[:, 0:CONV_DIM]
    c_gate = proj[:, CONV_DIM:2 * CONV_DIM]
    x_a = proj[:, 2 * CONV_DIM:3 * CONV_DIM]
    z0 = 3 * CONV_DIM
    z_ref[...] = proj[:, z0:z0 + SSM_D_INNER]
    x0 = z0 + SSM_D_INNER

    @pl.when(pl.program_id(1) == 0)
    def _():
        ua_ext[0:SUBLANES, :] = jnp.zeros((SUBLANES, CONV_DIM), F32)
        xbc_ext[0:SUBLANES, :] = jnp.zeros((SUBLANES, SSM_CONV_DIM), F32)

    ua_ext[SUBLANES:SUBLANES + tm, :] = c_gate * x_a
    conv = ca_ref[0:1, :] * ua_ext[pl.ds(SUBLANES - 2, tm), :]
    for k in range(1, CONV_WIDTH):
        conv = conv + ca_ref[k:k + 1, :] * ua_ext[pl.ds(SUBLANES - (CONV_WIDTH - 1) + k, tm), :]
    ya_ref[...] = b_gate * conv
    ua_ext[0:SUBLANES, :] = ua_ext[tm:tm + SUBLANES, :]

    xbc_ext[SUBLANES:SUBLANES + tm, :] = proj[:, x0:x0 + SSM_CONV_DIM]
    conv = cb_ref[...] + cw_ref[0:1, :] * xbc_ext[pl.ds(SUBLANES - (SSM_CONV - 1), tm), :]
    for k in range(1, SSM_CONV):
        conv = conv + cw_ref[k:k + 1, :] * xbc_ext[pl.ds(SUBLANES - (SSM_CONV - 1) + k, tm), :]
    xbc_ref[...] = conv * _sigmoid(conv)
    xbc_ext[0:SUBLANES, :] = xbc_ext[tm:tm + SUBLANES, :]

    d0 = x0 + SSM_CONV_DIM
    dt = _softplus(proj[:, d0:d0 + LANES] + dtb_ref[...])
    a = dt * aneg_ref[...]
    acs = _cumsum_rows(tril_ref[...], a)
    dt_ref[...] = dt
    acs_ref[...] = acs
    acst_ref[...] = acs.T


def _even_front(h, w_in, conv_a, conv_w, conv_b, dt_bias_row, aneg_row, bsz, seq):
    tok = bsz * seq
    tm = TM_FRONT
    ns = seq // tm
    row = lambda b, s: (b * ns + s, 0)
    out_shapes = (
        jax.ShapeDtypeStruct((tok, CONV_DIM), F32),
        jax.ShapeDtypeStruct((tok, SSM_D_INNER), F32),
        jax.ShapeDtypeStruct((tok, SSM_CONV_DIM), F32),
        jax.ShapeDtypeStruct((tok, LANES), F32),
        jax.ShapeDtypeStruct((tok, LANES), F32),
        jax.ShapeDtypeStruct((LANES, tok), F32),
    )
    return pl.pallas_call(
        _even_front_kernel,
        out_shape=out_shapes,
        grid=(bsz, ns),
        in_specs=[
            pl.BlockSpec((tm, D_MODEL), row),
            _full((D_MODEL, AB_PROJ)),
            _full((CONV_WIDTH, CONV_DIM)),
            _full((SSM_CONV, SSM_CONV_DIM)),
            _full((1, SSM_CONV_DIM)),
            _full((1, LANES)),
            _full((1, LANES)),
            _full((tm, tm)),
        ],
        out_specs=(
            pl.BlockSpec((tm, CONV_DIM), row),
            pl.BlockSpec((tm, SSM_D_INNER), row),
            pl.BlockSpec((tm, SSM_CONV_DIM), row),
            pl.BlockSpec((tm, LANES), row),
            pl.BlockSpec((tm, LANES), row),
            pl.BlockSpec((LANES, tm), lambda b, s: (0, b * ns + s)),
        ),
        scratch_shapes=[
            pltpu.VMEM((tm + SUBLANES, CONV_DIM), F32),
            pltpu.VMEM((tm + SUBLANES, SSM_CONV_DIM), F32),
        ],
        compiler_params=_params("arbitrary", "arbitrary"),
        name="even_front",
    )(h, w_in, conv_a, conv_w, conv_b, dt_bias_row, aneg_row, _tril(tm, CHUNK))


def _bcast_heads(arr, n_heads, width):
    per = LANES // width
    length = arr.shape[0]
    lane = lax.broadcasted_iota(jnp.int32, (length, LANES), 1)
    outs = []
    for j in range(n_heads // per):
        v = jnp.broadcast_to(arr[:, j * per:j * per + 1], (length, LANES))
        for r in range(1, per):
            v = jnp.where(lane >= r * width, jnp.broadcast_to(arr[:, j * per + r:j * per + r + 1], (length, LANES)), v)
        outs.append(v)
    return jnp.concatenate(outs, axis=1)


def _ssd_kernel(xbc_ref, dt_ref, acs_ref, acst_ref, z_ref, dskip_ref, nw_ref, y_ref, state):
    L = CHUNK
    P = SSM_HEAD_DIM
    R = SSM_HEADS // SSM_GROUPS
    GW = R * P

    @pl.when(pl.program_id(1) == 0)
    def _():
        state[...] = jnp.zeros(state.shape, F32)

    xs = xbc_ref[:, 0:SSM_D_INNER]
    bm = xbc_ref[:, SSM_D_INNER:SSM_D_INNER + SSM_BC]
    cm = xbc_ref[:, SSM_D_INNER + SSM_BC:SSM_CONV_DIM]
    dt = dt_ref[...]
    acs = acs_ref[...]
    acst = acst_ref[...]
    a_last = acs[L - 1:L, :]

    dtx = _bcast_heads(dt, SSM_HEADS, P)
    decx = _bcast_heads(jnp.exp(a_last - acs), SSM_HEADS, P)
    expx = _bcast_heads(jnp.exp(acs), SSM_HEADS, P)
    xdt = xs * dtx
    xdec = (xdt * decx).astype(MXU_DTYPE)
    xdt_m = xdt.astype(MXU_DTYPE)
    chunk_decay = jnp.exp(jnp.broadcast_to(a_last, (SUBLANES, LANES)))
    cdx = _bcast_heads(chunk_decay, SSM_HEADS, P)[0:1, :]

    bm_t = bm.T.astype(MXU_DTYPE)
    cm_m = cm.astype(MXU_DTYPE)
    bm_m = bm.astype(MXU_DTYPE)
    row = lax.broadcasted_iota(jnp.int32, (L, L), 0)
    col = lax.broadcasted_iota(jnp.int32, (L, L), 1)
    causal = row >= col
    lane_g = lax.broadcasted_iota(jnp.int32, (L, GW), 1)

    ys = []
    for g in range(SSM_GROUPS):
        n0 = g * SSM_STATE
        c_g = cm_m[:, n0:n0 + SSM_STATE]
        cb = lax.dot_general(c_g, bm_m[:, n0:n0 + SSM_STATE], (((1,), (1,)), ((), ())),
                             preferred_element_type=F32)
        ms = []
        for r in range(R):
            hd = g * R + r
            seg = jnp.exp(jnp.where(causal, acs[:, hd:hd + 1] - acst[hd:hd + 1, :], -jnp.inf))
            ms.append((cb * seg).astype(MXU_DTYPE))
        big = jnp.dot(jnp.concatenate(ms, axis=0), xdt_m[:, g * GW:(g + 1) * GW],
                      preferred_element_type=F32)
        y_diag = big[0:L, :]
        for r in range(1, R):
            y_diag = jnp.where(lane_g >= r * P, big[r * L:(r + 1) * L, :], y_diag)
        st = state[g]
        y_off = jnp.dot(c_g, st.astype(MXU_DTYPE), preferred_element_type=F32)
        new = jnp.dot(bm_t[n0:n0 + SSM_STATE, :], xdec[:, g * GW:(g + 1) * GW],
                      preferred_element_type=F32)
        state[g] = st * cdx[:, g * GW:(g + 1) * GW] + new
        ys.append(y_diag + y_off * expx[:, g * GW:(g + 1) * GW])
    y = jnp.concatenate(ys, axis=1) + dskip_ref[...] * xs
    z = z_ref[...]
    u = y * (z * _sigmoid(z))
    y_ref[...] = u * lax.rsqrt(jnp.mean(u * u, axis=-1, keepdims=True) + LN_EPS) * nw_ref[...]


def _ssd(xbc, dt, acs, acst, z, dskip_row, normw_row, bsz, seq):
    tok = bsz * seq
    nc = seq // CHUNK
    row = lambda b, c: (b * nc + c, 0)
    return pl.pallas_call(
        _ssd_kernel,
        out_shape=jax.ShapeDtypeStruct((tok, SSM_D_INNER), F32),
        grid=(bsz, nc),
        in_specs=[
            pl.BlockSpec((CHUNK, SSM_CONV_DIM), row),
            pl.BlockSpec((CHUNK, LANES), row),
            pl.BlockSpec((CHUNK, LANES), row),
            pl.BlockSpec((LANES, CHUNK), lambda b, c: (0, b * nc + c)),
            pl.BlockSpec((CHUNK, SSM_D_INNER), row),
            _full((1, SSM_D_INNER)),
            _full((1, SSM_D_INNER)),
        ],
        out_specs=pl.BlockSpec((CHUNK, SSM_D_INNER), row),
        scratch_shapes=[pltpu.VMEM((SSM_GROUPS, SSM_STATE, SSM_D_INNER // SSM_GROUPS), F32)],
        compiler_params=_params("arbitrary", "arbitrary"),
        name="ssd_scan",
    )(xbc, dt, acs, acst, z, dskip_row, normw_row)


def _outproj_ln_kernel(*refs, widths):
    n = len(widths)
    parts = refs[:n]
    h_ref, w_ref, g_ref, b_ref, o_ref, ot_ref = refs[n:]
    acc = ALPHA * h_ref[...]
    off = 0
    for p, wd in zip(parts, widths):
        acc = acc + jnp.dot(p[...].astype(MXU_DTYPE), w_ref[off:off + wd, :], preferred_element_type=F32)
        off += wd
    out = _layer_norm_rows(acc, g_ref[...], b_ref[...])
    o_ref[...] = out
    _rows_to_tiles(ot_ref, out)


def _outproj_ln(parts, h, w_out, g_row, b_row):
    tok = h.shape[0]
    tm = TM_OUT
    widths = tuple(p.shape[1] for p in parts)
    row = lambda i: (i, 0)
    return pl.pallas_call(
        functools.partial(_outproj_ln_kernel, widths=widths),
        out_shape=(jax.ShapeDtypeStruct((tok, D_MODEL), F32), jax.ShapeDtypeStruct((tok,) + ROW_TILE, F32)),
        grid=(tok // tm,),
        in_specs=[pl.BlockSpec((tm, wd), row) for wd in widths] + [
            pl.BlockSpec((tm, D_MODEL), row),
            _full((sum(widths), D_MODEL)),
            _full((1, D_MODEL)),
            _full((1, D_MODEL)),
        ],
        out_specs=(pl.BlockSpec((tm, D_MODEL), row), pl.BlockSpec((tm,) + ROW_TILE, lambda i: (i, 0, 0))),
        compiler_params=_params("arbitrary"),
        name="outproj_ln",
    )(*parts, h, w_out, g_row, b_row)


def _odd_front_kernel(h_ref, w_ref, gb_ref, tril_ref,
                      q_ref, k_ref, v_ref, o_ref, fq_ref, fk_ref, fvt_ref, g_ref, gt_ref, carry):
    tm = h_ref.shape[0]
    proj = jnp.dot(h_ref[...].astype(MXU_DTYPE), w_ref[...], preferred_element_type=F32)
    q_ref[...] = proj[:, 0:512]
    k_ref[...] = proj[:, 512:1024] * (MLSTM_HEAD_DIM ** -0.5)
    v_ref[...] = proj[:, 1024:1536]
    o_ref[...] = _sigmoid(proj[:, 1536:2048])
    fq_ref[...] = proj[:, 2048:2560] * (FOX_HEAD_DIM ** -0.5 * LOG2E)

    @pl.when(pl.program_id(1) == 0)
    def _():
        carry[...] = jnp.zeros(carry.shape, F32)

    raw = proj[:, CD_GATES:CD_GATES + LANES] + gb_ref[...]
    lane = lax.broadcasted_iota(jnp.int32, (tm, LANES), 1)
    g = jnp.where(lane < G_F, raw, -_softplus(-raw))
    prev = carry[0:1, :]
    glob = _cumsum_rows(tril_ref[...], g) + prev
    before = []
    for c in range(tm // CHUNK):
        before.append(jnp.broadcast_to(prev, (CHUNK, LANES)))
        prev = glob[(c + 1) * CHUNK - 1:(c + 1) * CHUNK, :]
    carry[...] = jnp.broadcast_to(prev, carry.shape)
    local = glob - jnp.concatenate(before, axis=0)
    out = jnp.where((lane >= G_FOX) & (lane < G_BCUM), glob, jnp.where(lane >= G_BCUM, local, g))
    g_ref[...] = out
    gt_ref[...] = out.T

    is_bias = (lane >= FOX_HEAD_DIM) & (lane < FOX_HEAD_DIM + 3)
    for hd in range(FOX_HEADS):
        c1, c2, c3 = _split3(out[:, G_FOX + hd:G_FOX + hd + 1] * (-LOG2E))
        bias = jnp.where(lane == FOX_HEAD_DIM, c1, jnp.where(lane == FOX_HEAD_DIM + 1, c2, c3))
        k_h = proj[:, CD_FK + hd * LANES:CD_FK + (hd + 1) * LANES]
        fk_ref[:, hd * LANES:(hd + 1) * LANES] = jnp.where(is_bias, bias, k_h).astype(MXU_DTYPE)
    ones_lane = jnp.where(lax.broadcasted_iota(jnp.int32, (tm, FOX_AUG), 1) % LANES == FOX_HEAD_DIM, 1.0, 0.0)
    fvt_ref[...] = (proj[:, CD_FV:CD_FV + FOX_AUG] + ones_lane).T.astype(MXU_DTYPE)


def _odd_front(h, w_in, gate_bias_row, bsz, seq):
    tok = bsz * seq
    tm = TM_FRONT
    ns = seq // tm
    row = lambda b, s: (b * ns + s, 0)
    col = lambda b, s: (0, b * ns + s)
    wide = jax.ShapeDtypeStruct((tok, 512), F32)
    return pl.pallas_call(
        _odd_front_kernel,
        out_shape=(wide,) * 5 + (jax.ShapeDtypeStruct((tok, FOX_AUG), MXU_DTYPE),
                                 jax.ShapeDtypeStruct((FOX_AUG, tok), MXU_DTYPE),
                                 jax.ShapeDtypeStruct((tok, LANES), F32),
                                 jax.ShapeDtypeStruct((LANES, tok), F32)),
        grid=(bsz, ns),
        in_specs=[
            pl.BlockSpec((tm, D_MODEL), row),
            _full((D_MODEL, CD_PROJ)),
            _full((1, LANES)),
            _full((tm, tm)),
        ],
        out_specs=(pl.BlockSpec((tm, 512), row),) * 5 + (
            pl.BlockSpec((tm, FOX_AUG), row),
            pl.BlockSpec((FOX_AUG, tm), col),
            pl.BlockSpec((tm, LANES), row),
            pl.BlockSpec((LANES, tm), col),
        ),
        scratch_shapes=[pltpu.VMEM((SUBLANES, LANES), F32)],
        compiler_params=_params("arbitrary", "arbitrary"),
        name="odd_front",
    )(h, w_in, gate_bias_row, _tril(tm, tm))


def _mlstm_kernel(q_ref, k_ref, v_ref, o_ref, g_ref, gt_ref, nw_ref, y_ref, c_state, m_state):
    L = CHUNK
    DH = MLSTM_HEAD_DIM

    @pl.when(pl.program_id(1) == 0)
    def _():
        c_state[...] = jnp.zeros(c_state.shape, F32)
        m_state[...] = jnp.zeros(m_state.shape, F32)

    gates = g_ref[...]
    gates_t = gt_ref[...]
    row = lax.broadcasted_iota(jnp.int32, (L, L), 0)
    col = lax.broadcasted_iota(jnp.int32, (L, L), 1)
    causal = row >= col
    ones_col = jnp.where(lax.broadcasted_iota(jnp.int32, (L, DH), 1) == 0, 1.0, 0.0)

    for hd in range(MLSTM_HEADS):
        sl = slice(hd * DH, (hd + 1) * DH)
        q = q_ref[:, sl].astype(MXU_DTYPE)
        k = k_ref[:, sl]
        v_ext = jnp.concatenate([v_ref[:, sl], ones_col], axis=1).astype(MXU_DTYPE)
        b_col = gates[:, G_BCUM + hd:G_BCUM + hd + 1]
        i_col = gates[:, G_I + hd:G_I + hd + 1]
        b_row = gates_t[G_BCUM + hd:G_BCUM + hd + 1, :]
        i_row = gates_t[G_I + hd:G_I + hd + 1, :]
        m_prev = m_state[hd:hd + 1, 0:1]
        c_ext = c_state[hd]

        d_mat = jnp.where(causal, b_col - b_row + i_row, -jnp.inf)
        inter = b_col + m_prev
        m_t = jnp.maximum(jnp.max(d_mat, axis=-1, keepdims=True), inter)
        s_qk = lax.dot_general(q, k.astype(MXU_DTYPE), (((1,), (1,)), ((), ())), preferred_element_type=F32)
        w_qk = s_qk * jnp.exp(d_mat - m_t)
        s_inter = jnp.exp(inter - m_t)
        num_ext = (jnp.dot(w_qk.astype(MXU_DTYPE), v_ext, preferred_element_type=F32)
                   + s_inter * jnp.dot(q, c_ext.astype(MXU_DTYPE), preferred_element_type=F32))
        den = num_ext[:, DH:DH + 1]
        hval = num_ext[:, 0:DH] / jnp.maximum(jnp.abs(den), jnp.exp(-m_t))

        b_last = b_col[L - 1:L, :]
        g_log = b_last - b_col + i_col
        m_new = jnp.maximum(b_last + m_prev, jnp.max(g_log, axis=0, keepdims=True))
        w_k = jnp.exp(g_log - m_new)
        decay = jnp.exp(b_last + m_prev - m_new)
        kw_t = (k * w_k).T.astype(MXU_DTYPE)
        c_state[hd] = decay * c_ext + jnp.dot(kw_t, v_ext, preferred_element_type=F32)
        m_state[hd:hd + 1, :] = jnp.broadcast_to(m_new, (1, LANES))

        mu = jnp.mean(hval, axis=-1, keepdims=True)
        cen = hval - mu
        var = jnp.mean(cen * cen, axis=-1, keepdims=True)
        y_ref[:, sl] = o_ref[:, sl] * (cen * lax.rsqrt(var + LN_EPS) * nw_ref[:, sl])


def _mlstm(q, k, v, o, gates, gates_t, hnorm_row, bsz, seq):
    tok = bsz * seq
    nc = seq // CHUNK
    row = lambda b, c: (b * nc + c, 0)
    wide = pl.BlockSpec((CHUNK, MLSTM_W), row)
    return pl.pallas_call(
        _mlstm_kernel,
        out_shape=jax.ShapeDtypeStruct((tok, MLSTM_W), F32),
        grid=(bsz, nc),
        in_specs=[wide, wide, wide, wide,
                  pl.BlockSpec((CHUNK, LANES), row),
                  pl.BlockSpec((LANES, CHUNK), lambda b, c: (0, b * nc + c)),
                  _full((1, MLSTM_W))],
        out_specs=wide,
        scratch_shapes=[pltpu.VMEM((MLSTM_HEADS, MLSTM_HEAD_DIM, 2 * MLSTM_HEAD_DIM), F32),
                        pltpu.VMEM((SUBLANES, LANES), F32)],
        compiler_params=_params("arbitrary", "arbitrary"),
        name="mlstm_scan",
    )(q, k, v, o, gates, gates_t, hnorm_row)


def _fox_kernel(q_ref, k_ref, vt_ref, gt_ref, y_ref, *scratch):
    acc_refs = scratch[:FOX_HEADS]
    qa_ref = scratch[FOX_HEADS]
    tq = q_ref.shape[0]
    tk = TK_FOX
    dh = FOX_HEAD_DIM
    qi = pl.program_id(1)
    q_t = q_ref[...].T
    bias_rows = jnp.where(lax.broadcasted_iota(jnp.int32, (LANES - dh, tq), 0) < 3, 1.0, 0.0)
    for hd in range(FOX_HEADS):
        qa_ref[hd] = jnp.concatenate([q_t[hd * dh:(hd + 1) * dh, :], bias_rows], axis=0).astype(MXU_DTYPE)
        acc_refs[hd][...] = jnp.zeros(acc_refs[hd].shape, F32)
    cq = gt_ref[G_FOX:G_FOX + FOX_HEADS, :] * LOG2E
    key_pos = lax.broadcasted_iota(jnp.int32, (tk, tq), 0)
    qry_pos = lax.broadcasted_iota(jnp.int32, (tk, tq), 1) + qi * tq
    n_full = qi * (tq // tk)

    def block(j, ms, masked):
        k0 = pl.multiple_of(j * tk, tk)
        out = []
        for hd in range(FOX_HEADS):
            hs = slice(hd * LANES, (hd + 1) * LANES)
            s = jnp.dot(k_ref[pl.ds(k0, tk), hs], qa_ref[hd], preferred_element_type=F32)
            if masked:
                s = jnp.where(key_pos + k0 <= qry_pos, s, -jnp.inf)
            cq_h = cq[hd:hd + 1, :]
            m_new = jnp.maximum(ms[hd], jnp.max(s, axis=0, keepdims=True) + cq_h)
            p = jnp.exp2(s - (m_new - cq_h))
            pv = jnp.dot(vt_ref[hs, pl.ds(k0, tk)], p.astype(MXU_DTYPE), preferred_element_type=F32)
            acc_refs[hd][...] = jnp.exp2(ms[hd] - m_new) * acc_refs[hd][...] + pv[0:FOX_ACC_ROWS, :]
            out.append(m_new)
        return tuple(out)

    ms = tuple(jnp.full((1, tq), -jnp.inf, F32) for _ in range(FOX_HEADS))
    ms = lax.fori_loop(0, n_full, functools.partial(block, masked=False), ms)
    for d in range(tq // tk):
        ms = block(n_full + d, ms, masked=True)
    outs = []
    for hd in range(FOX_HEADS):
        acc = acc_refs[hd][...]
        outs.append(acc[0:dh, :] / acc[dh:dh + 1, :])
    y_ref[...] = jnp.concatenate(outs, axis=0).T


def _fox(fq, fk_aug, fvt_aug, gates_t, bsz, seq):
    tok = bsz * seq
    tq = TQ_FOX
    nq = seq // tq
    return pl.pallas_call(
        _fox_kernel,
        out_shape=jax.ShapeDtypeStruct((tok, FOX_W), F32),
        grid=(bsz, nq),
        in_specs=[
            pl.BlockSpec((tq, FOX_W), lambda b, i: (b * nq + i, 0)),
            pl.BlockSpec((seq, FOX_AUG), lambda b, i: (b, 0)),
            pl.BlockSpec((FOX_AUG, seq), lambda b, i: (0, b)),
            pl.BlockSpec((LANES, tq), lambda b, i: (0, b * nq + i)),
        ],
        out_specs=pl.BlockSpec((tq, FOX_W), lambda b, i: (b * nq + i, 0)),
        scratch_shapes=[pltpu.VMEM((FOX_ACC_ROWS, tq), F32)] * FOX_HEADS
                       + [pltpu.VMEM((FOX_HEADS, LANES, tq), MXU_DTYPE)],
        compiler_params=_params("arbitrary", "arbitrary"),
        name="fox_attention",
    )(fq, fk_aug, fvt_aug, gates_t)


def _router_kernel(h_ref, w_ref, b_ref, stril_ref, id_ref, gate_ref, cnt_ref, carry):
    tm = h_ref.shape[0]

    @pl.when(pl.program_id(0) == 0)
    def _():
        carry[...] = jnp.zeros(carry.shape, F32)

    h = h_ref[...]
    h_hi = h.astype(MXU_DTYPE)
    h_lo = (h - h_hi.astype(F32)).astype(MXU_DTYPE)
    both = jnp.dot(h_hi, w_ref[...], preferred_element_type=F32)
    logits = (both[:, 0:LANES] + both[:, LANES:2 * LANES]
              + jnp.dot(h_lo, w_ref[:, 0:LANES], preferred_element_type=F32) + b_ref[...])
    lane = lax.broadcasted_iota(jnp.int32, (tm, LANES), 1).astype(F32)
    neg = -jnp.inf
    first = lambda hit: jnp.min(jnp.where(hit, lane, float(LANES)), axis=-1, keepdims=True)
    gl = jnp.where(lane < MOE_GROUPS, logits, neg)
    g_max = jnp.max(gl, axis=-1, keepdims=True)
    g_idx = first(gl == g_max)
    p_group = 1.0 / jnp.sum(jnp.exp(gl - g_max), axis=-1, keepdims=True)
    e_lo = MOE_GROUPS + g_idx * EXPERTS_PER_GROUP
    el = jnp.where((lane >= e_lo) & (lane < e_lo + EXPERTS_PER_GROUP), logits, neg)
    v1 = jnp.max(el, axis=-1, keepdims=True)
    i1 = first(el == v1)
    el2 = jnp.where(lane == i1, neg, el)
    v2 = jnp.max(el2, axis=-1, keepdims=True)
    i2 = first(el2 == v2)
    t = jnp.exp(v2 - v1)
    w1 = 1.0 / (1.0 + t)
    gate_ref[...] = jnp.where(lane == 0.0, p_group * w1, jnp.where(lane == 1.0, p_group * (t * w1), 0.0))

    hit1 = lane == i1
    hit2 = lane == i2
    sent = jnp.where(hit1 | hit2, 1.0, 0.0)
    before = jnp.dot(stril_ref[...], sent.astype(jnp.bfloat16), preferred_element_type=F32) + carry[0:1, :]
    r1 = jnp.sum(jnp.where(hit1, before, 0.0), axis=-1, keepdims=True)
    r2 = jnp.sum(jnp.where(hit2, before, 0.0), axis=-1, keepdims=True)
    total = before[tm - 1:tm, :] + sent[tm - 1:tm, :]
    carry[...] = jnp.broadcast_to(total, carry.shape)
    cnt_ref[...] = jnp.broadcast_to(total, cnt_ref.shape).astype(jnp.int32)
    ids = jnp.where(lane == 0.0, i1 - MOE_GROUPS, jnp.where(lane == 1.0, i2 - MOE_GROUPS,
                    jnp.where(lane == 2.0, r1, jnp.where(lane == 3.0, r2, 0.0))))
    id_ref[...] = ids.T[0:SUBLANES, :].astype(jnp.int32)


def _router(h, w_router, b_router):
    tok = h.shape[0]
    tm = TM_ROUTER
    row = lambda i: (i, 0)
    idx = np.arange(tm)
    stril = jnp.asarray((idx[:, None] > idx[None, :]).astype(np.float32), jnp.bfloat16)
    w_hi = w_router.astype(MXU_DTYPE)
    w_hi_lo = (w_hi, (w_router - w_hi.astype(F32)).astype(MXU_DTYPE))
    return pl.pallas_call(
        _router_kernel,
        out_shape=(jax.ShapeDtypeStruct((SUBLANES, tok), jnp.int32), jax.ShapeDtypeStruct((tok, LANES), F32),
                   jax.ShapeDtypeStruct((SUBLANES, LANES), jnp.int32)),
        grid=(tok // tm,),
        in_specs=[pl.BlockSpec((tm, D_MODEL), row), _full((D_MODEL, 2 * LANES)), _full((1, LANES)),
                  _full((tm, tm))],
        out_specs=(pl.BlockSpec((SUBLANES, tm), lambda i: (0, i)), pl.BlockSpec((tm, LANES), row),
                   _full((SUBLANES, LANES))),
        scratch_shapes=[pltpu.VMEM((SUBLANES, LANES), F32)],
        compiler_params=_params("arbitrary"),
        name="moe_router",
    )(h, jnp.concatenate([w_hi, w_hi_lo[1]], axis=1), b_router, stril)


def _token_bits(n_tok):
    return (n_tok - 1).bit_length()


def _slot_index_kernel(d0_ref, d1_ref, slot_ref):
    n_tok = d0_ref.shape[0]
    bits = _token_bits(n_tok)

    def init(s, c):
        slot_ref[s] = (TOP_K * n_tok + s % MOE_ROWS) << bits
        return c

    def put(t, c):
        slot_ref[d0_ref[t]] = t | (t << bits)
        slot_ref[d1_ref[t]] = t | ((n_tok + t) << bits)
        return c

    lax.fori_loop(0, slot_ref.shape[0], init, 0, unroll=8)
    lax.fori_loop(0, n_tok, put, 0, unroll=8)


def _slot_index(dest0, dest1, n_slots):
    smem = pl.BlockSpec(memory_space=pltpu.SMEM)
    return pl.pallas_call(
        _slot_index_kernel,
        out_shape=jax.ShapeDtypeStruct((n_slots,), jnp.int32),
        in_specs=[smem, smem],
        out_specs=smem,
        name="moe_slot_index",
    )(dest0, dest1)


def _expert_kernel(be_ref, nused_ref, slot_ref, h_hbm, wg_ref, wu_ref, wd_ref, y_hbm,
                   xbuf, obuf, wg_s, wu_s, wd_s, gsem, ssem):
    i = pl.program_id(0)
    n_used = nused_ref[0]
    n_blocks = pl.num_programs(0) - 1
    cur = i % 2
    n_tok = h_hbm.shape[0]
    bits = _token_bits(n_tok)

    def gather(block, b):
        base = jnp.minimum(block, n_blocks - 1) * MOE_ROWS
        for r in range(MOE_ROWS):
            tok = slot_ref[base + r] & (2 ** bits - 1)
            pltpu.make_async_copy(h_hbm.at[tok], xbuf.at[b, r // SUBLANES, :, r % SUBLANES, :],
                                  gsem.at[b]).start(priority=r % 2)

    def scatter(block, b):
        base = block * MOE_ROWS
        for r in range(MOE_ROWS):
            row = slot_ref[base + r] >> bits
            pltpu.make_async_copy(obuf.at[b, pl.ds(r, 1)], y_hbm.at[pl.ds(row, 1)],
                                  ssem.at[b]).start(priority=r % 2)

    def wait_gather(b):
        pltpu.make_async_copy(xbuf.at[1 - b], xbuf.at[b], gsem.at[b]).wait()

    def wait_scatter(b):
        pltpu.make_async_copy(obuf.at[b], y_hbm.at[pl.ds(0, MOE_ROWS)], ssem.at[b]).wait()

    def ffn(b):
        x = jnp.concatenate([xbuf[b, :, s, :, :].reshape(MOE_ROWS, LANES) for s in range(ROW_TILE[0])], axis=1)
        x = x.astype(MXU_DTYPE)
        gate = jnp.dot(x, wg_s[...], preferred_element_type=F32)
        up = jnp.dot(x, wu_s[...], preferred_element_type=F32)
        hidden = (gate * _sigmoid(gate) * up).astype(MXU_DTYPE)
        obuf[b] = jnp.dot(hidden, wd_s[...], preferred_element_type=F32)

    @pl.when(i == 0)
    def _():
        obuf[1] = jnp.zeros(obuf.shape[1:], F32)
        init = pltpu.make_async_copy(obuf.at[1], y_hbm.at[pl.ds(TOP_K * n_tok, MOE_ROWS)], ssem.at[1])
        init.start()
        init.wait()
        gather(0, 0)

    @pl.when(i < n_used)
    def _():
        @pl.when((i == 0) | (be_ref[i] != be_ref[jnp.maximum(i - 1, 0)]))
        def _():
            wg_s[...] = wg_ref[0].astype(MXU_DTYPE)
            wu_s[...] = wu_ref[0].astype(MXU_DTYPE)
            wd_s[...] = wd_ref[0].astype(MXU_DTYPE)

        wait_gather(cur)

        @pl.when(i >= 2)
        def _():
            wait_scatter(cur)

    @pl.when(i == 0)
    def _():
        gather(1, 1)
        ffn(0)

    for parity in range(2):
        @pl.when((i >= 1) & (i < n_used) & (cur == parity))
        def _(parity=parity):
            gather(i + 1, 1 - parity)
            scatter(i - 1, 1 - parity)
            ffn(parity)

    @pl.when(i == n_used)
    def _():
        wait_gather(cur)

        @pl.when(i >= 2)
        def _():
            wait_scatter(cur)

        for parity in range(2):
            @pl.when(cur == parity)
            def _(parity=parity):
                scatter(i - 1, 1 - parity)
        wait_scatter(1 - cur)


def _expert_ffn(h_tiles, slot_index, block_expert, n_used, w_gate, w_up, w_down, layer):
    tok = h_tiles.shape[0]
    n_blocks = block_expert.shape[0]
    wsel = lambda i, be, nu, sl: (layer * N_EXPERTS + be[jnp.minimum(i, n_blocks - 1)], 0, 0)
    grid_spec = pltpu.PrefetchScalarGridSpec(
        num_scalar_prefetch=3,
        grid=(n_blocks + 1,),
        in_specs=[
            pl.BlockSpec(memory_space=pl.ANY),
            pl.BlockSpec((1, D_MODEL, D_EXPERT), wsel),
            pl.BlockSpec((1, D_MODEL, D_EXPERT), wsel),
            pl.BlockSpec((1, D_EXPERT, D_MODEL), wsel),
        ],
        out_specs=pl.BlockSpec(memory_space=pl.ANY),
        scratch_shapes=[pltpu.VMEM((2, MOE_ROWS // SUBLANES, ROW_TILE[0], SUBLANES, LANES), F32),
                        pltpu.VMEM((2, MOE_ROWS, D_MODEL), F32),
                        pltpu.VMEM((D_MODEL, D_EXPERT), MXU_DTYPE), pltpu.VMEM((D_MODEL, D_EXPERT), MXU_DTYPE),
                        pltpu.VMEM((D_EXPERT, D_MODEL), MXU_DTYPE),
                        pltpu.SemaphoreType.DMA((2,)), pltpu.SemaphoreType.DMA((2,))],
    )
    return pl.pallas_call(
        _expert_kernel,
        out_shape=jax.ShapeDtypeStruct((TOP_K * tok + MOE_ROWS, D_MODEL), F32),
        grid_spec=grid_spec,
        compiler_params=_params("arbitrary"),
        name="moe_experts",
    )(block_expert, n_used, slot_index, h_tiles, w_gate, w_up, w_down)


def _combine_ln_kernel(h_ref, y0_ref, y1_ref, gate_ref, g_ref, b_ref, o_ref):
    gate = gate_ref[...]
    acc = ALPHA * h_ref[...] + gate[:, 0:1] * y0_ref[...] + gate[:, 1:2] * y1_ref[...]
    o_ref[...] = _layer_norm_rows(acc, g_ref[...], b_ref[...])


def _combine_ln(y, h, gates, g_row, b_row):
    tok = h.shape[0]
    tm = TM_COMBINE
    nt = tok // tm
    row = lambda i: (i, 0)
    return pl.pallas_call(
        _combine_ln_kernel,
        out_shape=jax.ShapeDtypeStruct((tok, D_MODEL), F32),
        grid=(nt,),
        in_specs=[
            pl.BlockSpec((tm, D_MODEL), row),
            pl.BlockSpec((tm, D_MODEL), row),
            pl.BlockSpec((tm, D_MODEL), lambda i: (nt + i, 0)),
            pl.BlockSpec((tm, LANES), row),
            _full((1, D_MODEL)),
            _full((1, D_MODEL)),
        ],
        out_specs=pl.BlockSpec((tm, D_MODEL), row),
        compiler_params=_params("arbitrary"),
        name="moe_combine_ln",
    )(h, y, y, gates, g_row, b_row)


def _slot_layout(counts, n_blocks):
    padded = (counts + MOE_ROWS - 1) // MOE_ROWS * MOE_ROWS
    pad_end = jnp.cumsum(padded)
    n_used = pad_end[-1:] // MOE_ROWS
    blocks = jnp.arange(n_blocks, dtype=jnp.int32)
    first_row = jnp.minimum(blocks, n_used - 1) * MOE_ROWS
    block_expert = jnp.sum(first_row[:, None] >= pad_end[None, :], axis=1)
    return (pad_end - padded).astype(jnp.int32), block_expert.astype(jnp.int32), n_used.astype(jnp.int32)


def _moe_ln(h, h_tiles, w_router, b_router, w_gate, w_up, w_down, layer, g_row, b_row):
    n_blocks = h.shape[0] * TOP_K // MOE_ROWS + N_EXPERTS
    ids, gates, counts = _router(h, w_router, b_router)
    counts = counts[0, MOE_GROUPS:MOE_GROUPS + N_EXPERTS]
    pad_start, block_expert, n_used = _slot_layout(counts, n_blocks)
    slot_index = _slot_index(pad_start[ids[0]] + ids[2], pad_start[ids[1]] + ids[3], n_blocks * MOE_ROWS)
    y = _expert_ffn(h_tiles, slot_index, block_expert, n_used, w_gate, w_up, w_down, layer)
    return _combine_ln(y, h, gates, g_row, b_row)


def _pad_lanes(v):
    return jnp.pad(v, (0, LANES - v.shape[0])).reshape(1, LANES)


def _even_mixer(h, bsz, seq, w_in, conv_a, conv_w, conv_b, dt_bias, a_log, d_skip, norm_w, w_out, g_row, b_row):
    w = jnp.pad(w_in, ((0, 0), (0, AB_PROJ - w_in.shape[1]))).astype(MXU_DTYPE)
    y_a, z, xbc, dt, acs, acst = _even_front(
        h, w, conv_a, conv_w, conv_b.reshape(1, -1), _pad_lanes(dt_bias), _pad_lanes(-jnp.exp(a_log)), bsz, seq)
    dskip_row = jnp.repeat(d_skip, SSM_HEAD_DIM).reshape(1, -1)
    y_b = _ssd(xbc, dt, acs, acst, z, dskip_row, norm_w.reshape(1, -1), bsz, seq)
    return _outproj_ln([y_a, y_b], h, w_out.astype(MXU_DTYPE), g_row, b_row)


def _odd_mixer(h, bsz, seq, w_in, i_bias, f_bias, hnorm_w, fox_f_bias, w_out, g_row, b_row):
    c = np.cumsum((0, MLSTM_W, MLSTM_W, MLSTM_W, MLSTM_HEADS, MLSTM_HEADS, MLSTM_W, FOX_W, FOX_W, FOX_W, FOX_HEADS))
    part = lambda j: w_in[:, c[j]:c[j + 1]]
    q, k, v, i_pre, f_pre, o_pre, fq, fk, fv, ff = (part(j) for j in range(10))
    gate_cols = jnp.concatenate([i_pre, f_pre, ff, f_pre], axis=1)
    gate_cols = jnp.pad(gate_cols, ((0, 0), (0, LANES - gate_cols.shape[1])))
    spread = lambda m: jnp.pad(m.reshape(-1, FOX_HEADS, FOX_HEAD_DIM),
                               ((0, 0), (0, 0), (0, LANES - FOX_HEAD_DIM))).reshape(-1, FOX_AUG)
    w = jnp.concatenate([q, k, v, o_pre, fq, spread(fk), spread(fv), gate_cols], axis=1).astype(MXU_DTYPE)
    gate_bias = _pad_lanes(jnp.concatenate([i_bias, f_bias, fox_f_bias, f_bias]))
    q, k, v, o, fq, fk_aug, fvt_aug, gates, gates_t = _odd_front(h, w, gate_bias, bsz, seq)
    y_c = _mlstm(q, k, v, o, gates, gates_t, hnorm_w.reshape(1, -1), bsz, seq)
    y_d = _fox(fq, fk_aug, fvt_aug, gates_t, bsz, seq)
    return _outproj_ln([y_c, y_d], h, w_out.astype(MXU_DTYPE), g_row, b_row)


def kernel(x, ab_w_in, ab_conv_a, ab_conv_ssm_w, ab_conv_ssm_b, ab_dt_bias, ab_a_log, ab_d_skip, ab_norm_w, ab_w_out, cd_w_in, cd_i_bias, cd_f_bias, cd_hnorm_w, cd_fox_f_bias, cd_w_out, ln1_g, ln1_b, ln2_g, ln2_b, moe_rg_w, moe_rg_b, moe_re_w, moe_re_b, moe_w_gate, moe_w_up, moe_w_down):
    bsz, seq, d = x.shape
    h = x.reshape(bsz * seq, d)
    stack = lambda w: w.reshape((w.shape[0] * w.shape[1],) + w.shape[2:])
    w_gate, w_up, w_down = stack(moe_w_gate), stack(moe_w_up), stack(moe_w_down)
    for layer in range(DEPTH):
        j = layer // 2
        g1, b1 = ln1_g[layer].reshape(1, -1), ln1_b[layer].reshape(1, -1)
        if layer % 2 == 0:
            h, h_tiles = _even_mixer(h, bsz, seq, ab_w_in[j], ab_conv_a[j], ab_conv_ssm_w[j], ab_conv_ssm_b[j],
                                     ab_dt_bias[j], ab_a_log[j], ab_d_skip[j], ab_norm_w[j], ab_w_out[j], g1, b1)
        else:
            h, h_tiles = _odd_mixer(h, bsz, seq, cd_w_in[j], cd_i_bias[j], cd_f_bias[j], cd_hnorm_w[j],
                                    cd_fox_f_bias[j], cd_w_out[j], g1, b1)
        re_w = jnp.transpose(moe_re_w[layer], (1, 0, 2)).reshape(d, N_EXPERTS)
        w_router = jnp.pad(jnp.concatenate([moe_rg_w[layer], re_w], axis=1),
                           ((0, 0), (0, LANES - MOE_GROUPS - N_EXPERTS)))
        b_router = _pad_lanes(jnp.concatenate([moe_rg_b[layer], moe_re_b[layer].reshape(-1)]))
        h = _moe_ln(h, h_tiles, w_router, b_router, w_gate, w_up, w_down, layer,
                    ln2_g[layer].reshape(1, -1), ln2_b[layer].reshape(1, -1))
    return h.reshape(bsz, seq, d)
```

```python
import functools

import numpy as np
import jax
import jax.numpy as jnp
from jax import lax
from jax.experimental import pallas as pl
from jax.experimental.pallas import tpu as pltpu

F32 = jnp.float32
MXU_DTYPE = jnp.bfloat16
HIGHEST = lax.Precision.HIGHEST

D_MODEL = 1024
DEPTH = 4
ALPHA = (2 * DEPTH) ** 0.25
LN_EPS = 1e-5
CONV_DIM = D_MODEL // 2
CONV_WIDTH = 3
SSM_D_INNER = D_MODEL
SSM_HEAD_DIM = 64
SSM_HEADS = SSM_D_INNER // SSM_HEAD_DIM
SSM_GROUPS = 4
SSM_STATE = 64
SSM_CONV = 4
SSM_BC = SSM_GROUPS * SSM_STATE
SSM_CONV_DIM = SSM_D_INNER + 2 * SSM_BC
MLSTM_HEADS = 4
MLSTM_HEAD_DIM = D_MODEL // 8
MLSTM_W = MLSTM_HEADS * MLSTM_HEAD_DIM
FOX_HEADS = 8
FOX_HEAD_DIM = D_MODEL // 16
FOX_W = FOX_HEADS * FOX_HEAD_DIM
MOE_GROUPS = 4
EXPERTS_PER_GROUP = 8
N_EXPERTS = MOE_GROUPS * EXPERTS_PER_GROUP
TOP_K = 2
D_EXPERT = D_MODEL // 2

LANES = 128
SUBLANES = 8
VMEM_LIMIT_BYTES = 56 * 1024 * 1024

CHUNK = 128
TM_FRONT = 512
TM_OUT = 1024
TM_ROUTER = 512
TQ_FOX = 256
MOE_ROWS = 512
TM_COMBINE = 256
TM_DISPATCH = 512

AB_PROJ = 4224
FOX_AUG = FOX_HEADS * LANES
CD_FK = 5 * 512
CD_FV = CD_FK + FOX_AUG
CD_GATES = CD_FV + FOX_AUG
CD_PROJ = CD_GATES + LANES
LOG2E = 1.4426950408889634
TK_FOX = 128
FOX_ACC_ROWS = FOX_HEAD_DIM + SUBLANES
G_I, G_F, G_FOX, G_BCUM = 0, 4, 8, 16


def _params(*sem):
    return pltpu.CompilerParams(dimension_semantics=sem, vmem_limit_bytes=VMEM_LIMIT_BYTES)


def _softplus(x):
    return jnp.maximum(x, 0.0) + jnp.log(1.0 + jnp.exp(-jnp.abs(x)))


def _sigmoid(x):
    return 1.0 / (1.0 + jnp.exp(-x))


def _layer_norm_rows(v, g, b):
    mu = jnp.mean(v, axis=-1, keepdims=True)
    c = v - mu
    var = jnp.mean(c * c, axis=-1, keepdims=True)
    return c * lax.rsqrt(var + LN_EPS) * g + b


def _tril(n, block):
    i = np.arange(n)
    m = (i[:, None] >= i[None, :]) & (i[:, None] // block == i[None, :] // block)
    return jnp.asarray(m.astype(np.float32), MXU_DTYPE)


def _full(shape):
    return pl.BlockSpec(shape, lambda *_: (0,) * len(shape), pipeline_mode=pl.Buffered(1))


ROW_TILE = (D_MODEL // LANES, LANES)


def _tiles_to_rows(ref):
    n = ref.shape[0] // ROW_TILE[0]
    return jnp.concatenate([ref[pl.ds(s, n, stride=ROW_TILE[0]), :] for s in range(ROW_TILE[0])], axis=1)


def _rows_to_tiles(ref, val):
    n = val.shape[0]
    for s in range(ROW_TILE[0]):
        ref[pl.ds(s, n, stride=ROW_TILE[0]), :] = val[:, s * LANES:(s + 1) * LANES]


def _split3(x):
    narrow = lambda v: v.astype(MXU_DTYPE).astype(F32)
    x1 = narrow(x)
    x2 = narrow(x - x1)
    return x1, x2, narrow(x - x1 - x2)


def _cumsum_rows(tril, x):
    parts = jnp.dot(tril, jnp.concatenate(_split3(x), axis=1).astype(MXU_DTYPE), preferred_element_type=F32)
    return parts[:, 0:LANES] + parts[:, LANES:2 * LANES] + parts[:, 2 * LANES:3 * LANES]


def _even_front_kernel(h_ref, w_ref, ca_ref, cw_ref, cb_ref, dtb_ref, aneg_ref, tril_ref,
                       ya_ref, z_ref, xbc_ref, dt_ref, acs_ref, acst_ref,
                       ua_ext, xbc_ext):
    tm = h_ref.shape[0]
    proj = jnp.dot(h_ref[...].astype(MXU_DTYPE), w_ref[...], preferred_element_type=F32)
    b_gate = proj[:, 0:CONV_DIM]
    c_gate = proj[:, CONV_DIM:2 * CONV_DIM]
    x_a = proj[:, 2 * CONV_DIM:3 * CONV_DIM]
    z0 = 3 * CONV_DIM
    z_ref[...] = proj[:, z0:z0 + SSM_D_INNER]
    x0 = z0 + SSM_D_INNER

    @pl.when(pl.program_id(1) == 0)
    def _():
        ua_ext[0:SUBLANES, :] = jnp.zeros((SUBLANES, CONV_DIM), F32)
        xbc_ext[0:SUBLANES, :] = jnp.zeros((SUBLANES, SSM_CONV_DIM), F32)

    ua_ext[SUBLANES:SUBLANES + tm, :] = c_gate * x_a
    conv = ca_ref[0:1, :] * ua_ext[pl.ds(SUBLANES - 2, tm), :]
    for k in range(1, CONV_WIDTH):
        conv = conv + ca_ref[k:k + 1, :] * ua_ext[pl.ds(SUBLANES - (CONV_WIDTH - 1) + k, tm), :]
    ya_ref[...] = b_gate * conv
    ua_ext[0:SUBLANES, :] = ua_ext[tm:tm + SUBLANES, :]

    xbc_ext[SUBLANES:SUBLANES + tm, :] = proj[:, x0:x0 + SSM_CONV_DIM]
    conv = cb_ref[...] + cw_ref[0:1, :] * xbc_ext[pl.ds(SUBLANES - (SSM_CONV - 1), tm), :]
    for k in range(1, SSM_CONV):
        conv = conv + cw_ref[k:k + 1, :] * xbc_ext[pl.ds(SUBLANES - (SSM_CONV - 1) + k, tm), :]
    xbc_ref[...] = conv * _sigmoid(conv)
    xbc_ext[0:SUBLANES, :] = xbc_ext[tm:tm + SUBLANES, :]

    d0 = x0 + SSM_CONV_DIM
    dt = _softplus(proj[:, d0:d0 + LANES] + dtb_ref[...])
    a = dt * aneg_ref[...]
    acs = _cumsum_rows(tril_ref[...], a)
    dt_ref[...] = dt
    acs_ref[...] = acs
    acst_ref[...] = acs.T


def _even_front(h, w_in, conv_a, conv_w, conv_b, dt_bias_row, aneg_row, bsz, seq):
    tok = bsz * seq
    tm = TM_FRONT
    ns = seq // tm
    row = lambda b, s: (b * ns + s, 0)
    out_shapes = (
        jax.ShapeDtypeStruct((tok, CONV_DIM), F32),
        jax.ShapeDtypeStruct((tok, SSM_D_INNER), F32),
        jax.ShapeDtypeStruct((tok, SSM_CONV_DIM), F32),
        jax.ShapeDtypeStruct((tok, LANES), F32),
        jax.ShapeDtypeStruct((tok, LANES), F32),
        jax.ShapeDtypeStruct((LANES, tok), F32),
    )
    return pl.pallas_call(
        _even_front_kernel,
        out_shape=out_shapes,
        grid=(bsz, ns),
        in_specs=[
            pl.BlockSpec((tm, D_MODEL), row),
            _full((D_MODEL, AB_PROJ)),
            _full((CONV_WIDTH, CONV_DIM)),
            _full((SSM_CONV, SSM_CONV_DIM)),
            _full((1, SSM_CONV_DIM)),
            _full((1, LANES)),
            _full((1, LANES)),
            _full((tm, tm)),
        ],
        out_specs=(
            pl.BlockSpec((tm, CONV_DIM), row),
            pl.BlockSpec((tm, SSM_D_INNER), row),
            pl.BlockSpec((tm, SSM_CONV_DIM), row),
            pl.BlockSpec((tm, LANES), row),
            pl.BlockSpec((tm, LANES), row),
            pl.BlockSpec((LANES, tm), lambda b, s: (0, b * ns + s)),
        ),
        scratch_shapes=[
            pltpu.VMEM((tm + SUBLANES, CONV_DIM), F32),
            pltpu.VMEM((tm + SUBLANES, SSM_CONV_DIM), F32),
        ],
        compiler_params=_params("arbitrary", "arbitrary"),
        name="even_front",
    )(h, w_in, conv_a, conv_w, conv_b, dt_bias_row, aneg_row, _tril(tm, CHUNK))


def _bcast_heads(arr, n_heads, width):
    per = LANES // width
    length = arr.shape[0]
    lane = lax.broadcasted_iota(jnp.int32, (length, LANES), 1)
    outs = []
    for j in range(n_heads // per):
        v = jnp.broadcast_to(arr[:, j * per:j * per + 1], (length, LANES))
        for r in range(1, per):
            v = jnp.where(lane >= r * width, jnp.broadcast_to(arr[:, j * per + r:j * per + r + 1], (length, LANES)), v)
        outs.append(v)
    return jnp.concatenate(outs, axis=1)


def _ssd_kernel(xbc_ref, dt_ref, acs_ref, acst_ref, z_ref, dskip_ref, nw_ref, y_ref, state):
    L = CHUNK
    P = SSM_HEAD_DIM
    R = SSM_HEADS // SSM_GROUPS
    GW = R * P

    @pl.when(pl.program_id(1) == 0)
    def _():
        state[...] = jnp.zeros(state.shape, F32)

    xs = xbc_ref[:, 0:SSM_D_INNER]
    bm = xbc_ref[:, SSM_D_INNER:SSM_D_INNER + SSM_BC]
    cm = xbc_ref[:, SSM_D_INNER + SSM_BC:SSM_CONV_DIM]
    dt = dt_ref[...]
    acs = acs_ref[...]
    acst = acst_ref[...]
    a_last = acs[L - 1:L, :]

    dtx = _bcast_heads(dt, SSM_HEADS, P)
    decx = _bcast_heads(jnp.exp(a_last - acs), SSM_HEADS, P)
    expx = _bcast_heads(jnp.exp(acs), SSM_HEADS, P)
    xdt = xs * dtx
    xdec = (xdt * decx).astype(MXU_DTYPE)
    xdt_m = xdt.astype(MXU_DTYPE)
    chunk_decay = jnp.exp(jnp.broadcast_to(a_last, (SUBLANES, LANES)))
    cdx = _bcast_heads(chunk_decay, SSM_HEADS, P)[0:1, :]

    bm_t = bm.T.astype(MXU_DTYPE)
    cm_m = cm.astype(MXU_DTYPE)
    bm_m = bm.astype(MXU_DTYPE)
    row = lax.broadcasted_iota(jnp.int32, (L, L), 0)
    col = lax.broadcasted_iota(jnp.int32, (L, L), 1)
    causal = row >= col
    lane_g = lax.broadcasted_iota(jnp.int32, (L, GW), 1)

    ys = []
    for g in range(SSM_GROUPS):
        n0 = g * SSM_STATE
        c_g = cm_m[:, n0:n0 + SSM_STATE]
        cb = lax.dot_general(c_g, bm_m[:, n0:n0 + SSM_STATE], (((1,), (1,)), ((), ())),
                             preferred_element_type=F32)
        ms = []
        for r in range(R):
            hd = g * R + r
            seg = jnp.exp(jnp.where(causal, acs[:, hd:hd + 1] - acst[hd:hd + 1, :], -jnp.inf))
            ms.append((cb * seg).astype(MXU_DTYPE))
        big = jnp.dot(jnp.concatenate(ms, axis=0), xdt_m[:, g * GW:(g + 1) * GW],
                      preferred_element_type=F32)
        y_diag = big[0:L, :]
        for r in range(1, R):
            y_diag = jnp.where(lane_g >= r * P, big[r * L:(r + 1) * L, :], y_diag)
        st = state[g]
        y_off = jnp.dot(c_g, st.astype(MXU_DTYPE), preferred_element_type=F32)
        new = jnp.dot(bm_t[n0:n0 + SSM_STATE, :], xdec[:, g * GW:(g + 1) * GW],
                      preferred_element_type=F32)
        state[g] = st * cdx[:, g * GW:(g + 1) * GW] + new
        ys.append(y_diag + y_off * expx[:, g * GW:(g + 1) * GW])
    y = jnp.concatenate(ys, axis=1) + dskip_ref[...] * xs
    z = z_ref[...]
    u = y * (z * _sigmoid(z))
    y_ref[...] = u * lax.rsqrt(jnp.mean(u * u, axis=-1, keepdims=True) + LN_EPS) * nw_ref[...]


def _ssd(xbc, dt, acs, acst, z, dskip_row, normw_row, bsz, seq):
    tok = bsz * seq
    nc = seq // CHUNK
    row = lambda b, c: (b * nc + c, 0)
    return pl.pallas_call(
        _ssd_kernel,
        out_shape=jax.ShapeDtypeStruct((tok, SSM_D_INNER), F32),
        grid=(bsz, nc),
        in_specs=[
            pl.BlockSpec((CHUNK, SSM_CONV_DIM), row),
            pl.BlockSpec((CHUNK, LANES), row),
            pl.BlockSpec((CHUNK, LANES), row),
            pl.BlockSpec((LANES, CHUNK), lambda b, c: (0, b * nc + c)),
            pl.BlockSpec((CHUNK, SSM_D_INNER), row),
            _full((1, SSM_D_INNER)),
            _full((1, SSM_D_INNER)),
        ],
        out_specs=pl.BlockSpec((CHUNK, SSM_D_INNER), row),
        scratch_shapes=[pltpu.VMEM((SSM_GROUPS, SSM_STATE, SSM_D_INNER // SSM_GROUPS), F32)],
        compiler_params=_params("arbitrary", "arbitrary"),
        name="ssd_scan",
    )(xbc, dt, acs, acst, z, dskip_row, normw_row)


def _outproj_ln_kernel(*refs, widths):
    n = len(widths)
    parts = refs[:n]
    h_ref, w_ref, g_ref, b_ref, o_ref, ot_ref = refs[n:]
    acc = ALPHA * h_ref[...]
    off = 0
    for p, wd in zip(parts, widths):
        acc = acc + jnp.dot(p[...].astype(MXU_DTYPE), w_ref[off:off + wd, :], preferred_element_type=F32)
        off += wd
    out = _layer_norm_rows(acc, g_ref[...], b_ref[...])
    o_ref[...] = out
    _rows_to_tiles(ot_ref, out)


def _outproj_ln(parts, h, w_out, g_row, b_row):
    tok = h.shape[0]
    tm = TM_OUT
    widths = tuple(p.shape[1] for p in parts)
    row = lambda i: (i, 0)
    return pl.pallas_call(
        functools.partial(_outproj_ln_kernel, widths=widths),
        out_shape=(jax.ShapeDtypeStruct((tok, D_MODEL), F32),
                   jax.ShapeDtypeStruct((tok * ROW_TILE[0], LANES), F32)),
        grid=(tok // tm,),
        in_specs=[pl.BlockSpec((tm, wd), row) for wd in widths] + [
            pl.BlockSpec((tm, D_MODEL), row),
            _full((sum(widths), D_MODEL)),
            _full((1, D_MODEL)),
            _full((1, D_MODEL)),
        ],
        out_specs=(pl.BlockSpec((tm, D_MODEL), row), pl.BlockSpec((tm * ROW_TILE[0], LANES), row)),
        compiler_params=_params("arbitrary"),
        name="outproj_ln",
    )(*parts, h, w_out, g_row, b_row)


def _odd_front_kernel(h_ref, w_ref, gb_ref, tril_ref,
                      q_ref, k_ref, v_ref, o_ref, fq_ref, fk_ref, fvt_ref, g_ref, gt_ref, carry):
    tm = h_ref.shape[0]
    proj = jnp.dot(h_ref[...].astype(MXU_DTYPE), w_ref[...], preferred_element_type=F32)
    q_ref[...] = proj[:, 0:512]
    k_ref[...] = proj[:, 512:1024] * (MLSTM_HEAD_DIM ** -0.5)
    v_ref[...] = proj[:, 1024:1536]
    o_ref[...] = _sigmoid(proj[:, 1536:2048])
    fq_ref[...] = proj[:, 2048:2560] * (FOX_HEAD_DIM ** -0.5 * LOG2E)

    @pl.when(pl.program_id(1) == 0)
    def _():
        carry[...] = jnp.zeros(carry.shape, F32)

    raw = proj[:, CD_GATES:CD_GATES + LANES] + gb_ref[...]
    lane = lax.broadcasted_iota(jnp.int32, (tm, LANES), 1)
    g = jnp.where(lane < G_F, raw, -_softplus(-raw))
    prev = carry[0:1, :]
    glob = _cumsum_rows(tril_ref[...], g) + prev
    before = []
    for c in range(tm // CHUNK):
        before.append(jnp.broadcast_to(prev, (CHUNK, LANES)))
        prev = glob[(c + 1) * CHUNK - 1:(c + 1) * CHUNK, :]
    carry[...] = jnp.broadcast_to(prev, carry.shape)
    local = glob - jnp.concatenate(before, axis=0)
    out = jnp.where((lane >= G_FOX) & (lane < G_BCUM), glob, jnp.where(lane >= G_BCUM, local, g))
    g_ref[...] = out
    gt_ref[...] = out.T

    is_bias = (lane >= FOX_HEAD_DIM) & (lane < FOX_HEAD_DIM + 3)
    for hd in range(FOX_HEADS):
        c1, c2, c3 = _split3(out[:, G_FOX + hd:G_FOX + hd + 1] * (-LOG2E))
        bias = jnp.where(lane == FOX_HEAD_DIM, c1, jnp.where(lane == FOX_HEAD_DIM + 1, c2, c3))
        k_h = proj[:, CD_FK + hd * LANES:CD_FK + (hd + 1) * LANES]
        fk_ref[:, hd * LANES:(hd + 1) * LANES] = jnp.where(is_bias, bias, k_h).astype(MXU_DTYPE)
    ones_lane = jnp.where(lax.broadcasted_iota(jnp.int32, (tm, FOX_AUG), 1) % LANES == FOX_HEAD_DIM, 1.0, 0.0)
    fvt_ref[...] = (proj[:, CD_FV:CD_FV + FOX_AUG] + ones_lane).T.astype(MXU_DTYPE)


def _odd_front(h, w_in, gate_bias_row, bsz, seq):
    tok = bsz * seq
    tm = TM_FRONT
    ns = seq // tm
    row = lambda b, s: (b * ns + s, 0)
    col = lambda b, s: (0, b * ns + s)
    wide = jax.ShapeDtypeStruct((tok, 512), F32)
    return pl.pallas_call(
        _odd_front_kernel,
        out_shape=(wide,) * 5 + (jax.ShapeDtypeStruct((tok, FOX_AUG), MXU_DTYPE),
                                 jax.ShapeDtypeStruct((FOX_AUG, tok), MXU_DTYPE),
                                 jax.ShapeDtypeStruct((tok, LANES), F32),
                                 jax.ShapeDtypeStruct((LANES, tok), F32)),
        grid=(bsz, ns),
        in_specs=[
            pl.BlockSpec((tm, D_MODEL), row),
            _full((D_MODEL, CD_PROJ)),
            _full((1, LANES)),
            _full((tm, tm)),
        ],
        out_specs=(pl.BlockSpec((tm, 512), row),) * 5 + (
            pl.BlockSpec((tm, FOX_AUG), row),
            pl.BlockSpec((FOX_AUG, tm), col),
            pl.BlockSpec((tm, LANES), row),
            pl.BlockSpec((LANES, tm), col),
        ),
        scratch_shapes=[pltpu.VMEM((SUBLANES, LANES), F32)],
        compiler_params=_params("arbitrary", "arbitrary"),
        name="odd_front",
    )(h, w_in, gate_bias_row, _tril(tm, tm))


def _mlstm_kernel(q_ref, k_ref, v_ref, o_ref, g_ref, gt_ref, nw_ref, y_ref, c_state, m_state):
    L = CHUNK
    DH = MLSTM_HEAD_DIM

    @pl.when(pl.program_id(1) == 0)
    def _():
        c_state[...] = jnp.zeros(c_state.shape, F32)
        m_state[...] = jnp.zeros(m_state.shape, F32)

    gates = g_ref[...]
    gates_t = gt_ref[...]
    row = lax.broadcasted_iota(jnp.int32, (L, L), 0)
    col = lax.broadcasted_iota(jnp.int32, (L, L), 1)
    causal = row >= col
    ones_col = jnp.where(lax.broadcasted_iota(jnp.int32, (L, DH), 1) == 0, 1.0, 0.0)

    for hd in range(MLSTM_HEADS):
        sl = slice(hd * DH, (hd + 1) * DH)
        q = q_ref[:, sl].astype(MXU_DTYPE)
        k = k_ref[:, sl]
        v_ext = jnp.concatenate([v_ref[:, sl], ones_col], axis=1).astype(MXU_DTYPE)
        b_col = gates[:, G_BCUM + hd:G_BCUM + hd + 1]
        i_col = gates[:, G_I + hd:G_I + hd + 1]
        b_row = gates_t[G_BCUM + hd:G_BCUM + hd + 1, :]
        i_row = gates_t[G_I + hd:G_I + hd + 1, :]
        m_prev = m_state[hd:hd + 1, 0:1]
        c_ext = c_state[hd]

        d_mat = jnp.where(causal, b_col - b_row + i_row, -jnp.inf)
        inter = b_col + m_prev
        m_t = jnp.maximum(jnp.max(d_mat, axis=-1, keepdims=True), inter)
        s_qk = lax.dot_general(q, k.astype(MXU_DTYPE), (((1,), (1,)), ((), ())), preferred_element_type=F32)
        w_qk = s_qk * jnp.exp(d_mat - m_t)
        s_inter = jnp.exp(inter - m_t)
        num_ext = (jnp.dot(w_qk.astype(MXU_DTYPE), v_ext, preferred_element_type=F32)
                   + s_inter * jnp.dot(q, c_ext.astype(MXU_DTYPE), preferred_element_type=F32))
        den = num_ext[:, DH:DH + 1]
        hval = num_ext[:, 0:DH] / jnp.maximum(jnp.abs(den), jnp.exp(-m_t))

        b_last = b_col[L - 1:L, :]
        g_log = b_last - b_col + i_col
        m_new = jnp.maximum(b_last + m_prev, jnp.max(g_log, axis=0, keepdims=True))
        w_k = jnp.exp(g_log - m_new)
        decay = jnp.exp(b_last + m_prev - m_new)
        kw_t = (k * w_k).T.astype(MXU_DTYPE)
        c_state[hd] = decay * c_ext + jnp.dot(kw_t, v_ext, preferred_element_type=F32)
        m_state[hd:hd + 1, :] = jnp.broadcast_to(m_new, (1, LANES))

        mu = jnp.mean(hval, axis=-1, keepdims=True)
        cen = hval - mu
        var = jnp.mean(cen * cen, axis=-1, keepdims=True)
        y_ref[:, sl] = o_ref[:, sl] * (cen * lax.rsqrt(var + LN_EPS) * nw_ref[:, sl])


def _mlstm(q, k, v, o, gates, gates_t, hnorm_row, bsz, seq):
    tok = bsz * seq
    nc = seq // CHUNK
    row = lambda b, c: (b * nc + c, 0)
    wide = pl.BlockSpec((CHUNK, MLSTM_W), row)
    return pl.pallas_call(
        _mlstm_kernel,
        out_shape=jax.ShapeDtypeStruct((tok, MLSTM_W), F32),
        grid=(bsz, nc),
        in_specs=[wide, wide, wide, wide,
                  pl.BlockSpec((CHUNK, LANES), row),
                  pl.BlockSpec((LANES, CHUNK), lambda b, c: (0, b * nc + c)),
                  _full((1, MLSTM_W))],
        out_specs=wide,
        scratch_shapes=[pltpu.VMEM((MLSTM_HEADS, MLSTM_HEAD_DIM, 2 * MLSTM_HEAD_DIM), F32),
                        pltpu.VMEM((SUBLANES, LANES), F32)],
        compiler_params=_params("arbitrary", "arbitrary"),
        name="mlstm_scan",
    )(q, k, v, o, gates, gates_t, hnorm_row)


def _fox_kernel(q_ref, k_ref, vt_ref, gt_ref, y_ref, *scratch):
    acc_refs = scratch[:FOX_HEADS]
    qa_ref = scratch[FOX_HEADS]
    tq = q_ref.shape[0]
    tk = TK_FOX
    dh = FOX_HEAD_DIM
    qi = pl.program_id(1)
    q_t = q_ref[...].T
    bias_rows = jnp.where(lax.broadcasted_iota(jnp.int32, (LANES - dh, tq), 0) < 3, 1.0, 0.0)
    for hd in range(FOX_HEADS):
        qa_ref[hd] = jnp.concatenate([q_t[hd * dh:(hd + 1) * dh, :], bias_rows], axis=0).astype(MXU_DTYPE)
        acc_refs[hd][...] = jnp.zeros(acc_refs[hd].shape, F32)
    cq = gt_ref[G_FOX:G_FOX + FOX_HEADS, :] * LOG2E
    key_pos = lax.broadcasted_iota(jnp.int32, (tk, tq), 0)
    qry_pos = lax.broadcasted_iota(jnp.int32, (tk, tq), 1) + qi * tq
    n_full = qi * (tq // tk)

    def block(j, ms, masked):
        k0 = pl.multiple_of(j * tk, tk)
        out = []
        for hd in range(FOX_HEADS):
            hs = slice(hd * LANES, (hd + 1) * LANES)
            s = jnp.dot(k_ref[pl.ds(k0, tk), hs], qa_ref[hd], preferred_element_type=F32)
            if masked:
                s = jnp.where(key_pos + k0 <= qry_pos, s, -jnp.inf)
            cq_h = cq[hd:hd + 1, :]
            m_new = jnp.maximum(ms[hd], jnp.max(s, axis=0, keepdims=True) + cq_h)
            p = jnp.exp2(s - (m_new - cq_h))
            pv = jnp.dot(vt_ref[hs, pl.ds(k0, tk)], p.astype(MXU_DTYPE), preferred_element_type=F32)
            acc_refs[hd][...] = jnp.exp2(ms[hd] - m_new) * acc_refs[hd][...] + pv[0:FOX_ACC_ROWS, :]
            out.append(m_new)
        return tuple(out)

    ms = tuple(jnp.full((1, tq), -jnp.inf, F32) for _ in range(FOX_HEADS))
    ms = lax.fori_loop(0, n_full, functools.partial(block, masked=False), ms)
    for d in range(tq // tk):
        ms = block(n_full + d, ms, masked=True)
    outs = []
    for hd in range(FOX_HEADS):
        acc = acc_refs[hd][...]
        outs.append(acc[0:dh, :] / acc[dh:dh + 1, :])
    y_ref[...] = jnp.concatenate(outs, axis=0).T


def _fox(fq, fk_aug, fvt_aug, gates_t, bsz, seq):
    tok = bsz * seq
    tq = TQ_FOX
    nq = seq // tq
    return pl.pallas_call(
        _fox_kernel,
        out_shape=jax.ShapeDtypeStruct((tok, FOX_W), F32),
        grid=(bsz, nq),
        in_specs=[
            pl.BlockSpec((tq, FOX_W), lambda b, i: (b * nq + i, 0)),
            pl.BlockSpec((seq, FOX_AUG), lambda b, i: (b, 0)),
            pl.BlockSpec((FOX_AUG, seq), lambda b, i: (0, b)),
            pl.BlockSpec((LANES, tq), lambda b, i: (0, b * nq + i)),
        ],
        out_specs=pl.BlockSpec((tq, FOX_W), lambda b, i: (b * nq + i, 0)),
        scratch_shapes=[pltpu.VMEM((FOX_ACC_ROWS, tq), F32)] * FOX_HEADS
                       + [pltpu.VMEM((FOX_HEADS, LANES, tq), MXU_DTYPE)],
        compiler_params=_params("arbitrary", "arbitrary"),
        name="fox_attention",
    )(fq, fk_aug, fvt_aug, gates_t)


def _router_kernel(h_ref, w_ref, b_ref, stril_ref, id_ref, gate_ref, cnt_ref, carry):
    tm = h_ref.shape[0]

    @pl.when(pl.program_id(0) == 0)
    def _():
        carry[...] = jnp.zeros(carry.shape, F32)

    h = h_ref[...]
    h_hi = h.astype(MXU_DTYPE)
    h_lo = (h - h_hi.astype(F32)).astype(MXU_DTYPE)
    both = jnp.dot(h_hi, w_ref[...], preferred_element_type=F32)
    logits = (both[:, 0:LANES] + both[:, LANES:2 * LANES]
              + jnp.dot(h_lo, w_ref[:, 0:LANES], preferred_element_type=F32) + b_ref[...])
    lane = lax.broadcasted_iota(jnp.int32, (tm, LANES), 1).astype(F32)
    neg = -jnp.inf
    first = lambda hit: jnp.min(jnp.where(hit, lane, float(LANES)), axis=-1, keepdims=True)
    gl = jnp.where(lane < MOE_GROUPS, logits, neg)
    g_max = jnp.max(gl, axis=-1, keepdims=True)
    g_idx = first(gl == g_max)
    p_group = 1.0 / jnp.sum(jnp.exp(gl - g_max), axis=-1, keepdims=True)
    e_lo = MOE_GROUPS + g_idx * EXPERTS_PER_GROUP
    el = jnp.where((lane >= e_lo) & (lane < e_lo + EXPERTS_PER_GROUP), logits, neg)
    v1 = jnp.max(el, axis=-1, keepdims=True)
    i1 = first(el == v1)
    el2 = jnp.where(lane == i1, neg, el)
    v2 = jnp.max(el2, axis=-1, keepdims=True)
    i2 = first(el2 == v2)
    t = jnp.exp(v2 - v1)
    w1 = 1.0 / (1.0 + t)
    gate_ref[...] = jnp.where(lane == 0.0, p_group * w1, jnp.where(lane == 1.0, p_group * (t * w1), 0.0))

    hit1 = lane == i1
    hit2 = lane == i2
    sent = jnp.where(hit1 | hit2, 1.0, 0.0)
    before = jnp.dot(stril_ref[...], sent.astype(jnp.bfloat16), preferred_element_type=F32) + carry[0:1, :]
    r1 = jnp.sum(jnp.where(hit1, before, 0.0), axis=-1, keepdims=True)
    r2 = jnp.sum(jnp.where(hit2, before, 0.0), axis=-1, keepdims=True)
    total = before[tm - 1:tm, :] + sent[tm - 1:tm, :]
    carry[...] = jnp.broadcast_to(total, carry.shape)
    cnt_ref[...] = jnp.broadcast_to(total, cnt_ref.shape).astype(jnp.int32)
    ids = jnp.where(lane == 0.0, i1 - MOE_GROUPS, jnp.where(lane == 1.0, i2 - MOE_GROUPS,
                    jnp.where(lane == 2.0, r1, jnp.where(lane == 3.0, r2, 0.0))))
    id_ref[...] = ids.T[0:SUBLANES, :].astype(jnp.int32)


def _router(h, w_router, b_router):
    tok = h.shape[0]
    tm = TM_ROUTER
    row = lambda i: (i, 0)
    idx = np.arange(tm)
    stril = jnp.asarray((idx[:, None] > idx[None, :]).astype(np.float32), jnp.bfloat16)
    w_hi = w_router.astype(MXU_DTYPE)
    w_hi_lo = (w_hi, (w_router - w_hi.astype(F32)).astype(MXU_DTYPE))
    return pl.pallas_call(
        _router_kernel,
        out_shape=(jax.ShapeDtypeStruct((SUBLANES, tok), jnp.int32), jax.ShapeDtypeStruct((tok, LANES), F32),
                   jax.ShapeDtypeStruct((SUBLANES, LANES), jnp.int32)),
        grid=(tok // tm,),
        in_specs=[pl.BlockSpec((tm, D_MODEL), row), _full((D_MODEL, 2 * LANES)), _full((1, LANES)),
                  _full((tm, tm))],
        out_specs=(pl.BlockSpec((SUBLANES, tm), lambda i: (0, i)), pl.BlockSpec((tm, LANES), row),
                   _full((SUBLANES, LANES))),
        scratch_shapes=[pltpu.VMEM((SUBLANES, LANES), F32)],
        compiler_params=_params("arbitrary"),
        name="moe_router",
    )(h, jnp.concatenate([w_hi, w_hi_lo[1]], axis=1), b_router, stril)


def _pad_pieces(n):
    return tuple(1 << b for b in reversed(range((n - 1).bit_length())))


def _dispatch_kernel(e0_ref, e1_ref, r0_ref, r1_ref, start_ref, cnt_ref, nused_ref, h_ref, xs_hbm, zeros, sem,
                     pad_sem):
    tm = h_ref.shape[0]
    i = pl.program_id(0)
    base = i * tm

    def issue(r, c):
        t = base + r
        pltpu.make_async_copy(h_ref.at[r], xs_hbm.at[start_ref[e0_ref[t]] + r0_ref[t]], sem).start(priority=0)
        pltpu.make_async_copy(h_ref.at[r], xs_hbm.at[start_ref[e1_ref[t]] + r1_ref[t]], sem).start(priority=1)
        return c

    lax.fori_loop(0, tm, issue, 0, unroll=8)

    def pad_copies(e, fn):
        cnt = cnt_ref[e]
        n_pad = (MOE_ROWS - cnt % MOE_ROWS) % MOE_ROWS
        first = start_ref[e] + cnt
        for piece in _pad_pieces(MOE_ROWS):
            @pl.when((n_pad & piece) != 0)
            def _(piece=piece):
                off = first + (n_pad & ~(2 * piece - 1))
                fn(pltpu.make_async_copy(zeros.at[pl.ds(0, piece)], xs_hbm.at[pl.ds(off, piece)], pad_sem))

    def tail_copies(blk, fn):
        for part in range(MOE_ROWS // zeros.shape[0]):
            off = blk * MOE_ROWS + part * zeros.shape[0]
            fn(pltpu.make_async_copy(zeros, xs_hbm.at[pl.ds(off, zeros.shape[0])], pad_sem))

    @pl.when(i == 0)
    def _():
        zeros[...] = jnp.zeros(zeros.shape, F32)
        n_blocks = xs_hbm.shape[0] // MOE_ROWS

        def start(e, c):
            pad_copies(e, lambda cp: cp.start())
            return c

        def wait(e, c):
            pad_copies(e, lambda cp: cp.wait())
            return c

        def tail_start(blk, c):
            tail_copies(blk, lambda cp: cp.start())
            return c

        def tail_wait(blk, c):
            tail_copies(blk, lambda cp: cp.wait())
            return c

        lax.fori_loop(0, N_EXPERTS, start, 0)
        lax.fori_loop(nused_ref[0], n_blocks, tail_start, 0)
        lax.fori_loop(0, N_EXPERTS, wait, 0)
        lax.fori_loop(nused_ref[0], n_blocks, tail_wait, 0)

    for _ in range(TOP_K):
        pltpu.make_async_copy(h_ref, xs_hbm.at[pl.ds(0, tm)], sem).wait()


def _dispatch(h_tiles, e0, e1, r0, r1, pad_start, counts, n_used, n_slots):
    tok = h_tiles.shape[0]
    tm = TM_DISPATCH
    grid_spec = pltpu.PrefetchScalarGridSpec(
        num_scalar_prefetch=7,
        grid=(tok // tm,),
        in_specs=[pl.BlockSpec((tm,) + ROW_TILE, lambda i, *_: (i, 0, 0))],
        out_specs=pl.BlockSpec(memory_space=pl.ANY),
        scratch_shapes=[pltpu.VMEM((MOE_ROWS // 2,) + ROW_TILE, F32),
                        pltpu.SemaphoreType.DMA(()), pltpu.SemaphoreType.DMA(())],
    )
    return pl.pallas_call(
        _dispatch_kernel,
        out_shape=jax.ShapeDtypeStruct((n_slots,) + ROW_TILE, F32),
        grid_spec=grid_spec,
        compiler_params=_params("arbitrary"),
        name="moe_dispatch",
    )(e0, e1, r0, r1, pad_start, counts, n_used, h_tiles)


def _expert_kernel(be_ref, nused_ref, x_ref, wg_ref, wu_ref, wd_ref, out_ref, wg_s, wu_s, wd_s):
    i = pl.program_id(0)

    @pl.when(i < nused_ref[0])
    def _():
        @pl.when((i == 0) | (be_ref[i] != be_ref[jnp.maximum(i - 1, 0)]))
        def _():
            wg_s[...] = wg_ref[0].astype(MXU_DTYPE)
            wu_s[...] = wu_ref[0].astype(MXU_DTYPE)
            wd_s[...] = wd_ref[0].astype(MXU_DTYPE)

        x = _tiles_to_rows(x_ref).astype(MXU_DTYPE)
        gate = jnp.dot(x, wg_s[...], preferred_element_type=F32)
        up = jnp.dot(x, wu_s[...], preferred_element_type=F32)
        hidden = (gate * _sigmoid(gate) * up).astype(MXU_DTYPE)
        _rows_to_tiles(out_ref, jnp.dot(hidden, wd_s[...], preferred_element_type=F32))

    @pl.when(i >= nused_ref[0])
    def _():
        out_ref[...] = jnp.zeros(out_ref.shape, F32)


def _expert_ffn(xs_tiles, block_expert, n_used, w_gate, w_up, w_down, layer):
    n_slots = xs_tiles.shape[0]
    n_blocks = block_expert.shape[0]
    rows = MOE_ROWS * ROW_TILE[0]
    blk = lambda i, be, nu: (jnp.maximum(jnp.minimum(i, nu[0] - 1), 0), 0)
    wsel = lambda i, be, nu: (layer * N_EXPERTS + be[i], 0, 0)
    grid_spec = pltpu.PrefetchScalarGridSpec(
        num_scalar_prefetch=2,
        grid=(n_blocks,),
        in_specs=[
            pl.BlockSpec((rows, LANES), blk),
            pl.BlockSpec((1, D_MODEL, D_EXPERT), wsel),
            pl.BlockSpec((1, D_MODEL, D_EXPERT), wsel),
            pl.BlockSpec((1, D_EXPERT, D_MODEL), wsel),
        ],
        out_specs=pl.BlockSpec((rows, LANES), lambda i, be, nu: (i, 0)),
        scratch_shapes=[pltpu.VMEM((D_MODEL, D_EXPERT), MXU_DTYPE), pltpu.VMEM((D_MODEL, D_EXPERT), MXU_DTYPE),
                        pltpu.VMEM((D_EXPERT, D_MODEL), MXU_DTYPE)],
    )
    out = pl.pallas_call(
        _expert_kernel,
        out_shape=jax.ShapeDtypeStruct((n_slots * ROW_TILE[0], LANES), F32),
        grid_spec=grid_spec,
        compiler_params=_params("arbitrary"),
        name="moe_experts",
    )(block_expert, n_used, xs_tiles.reshape(n_slots * ROW_TILE[0], LANES), w_gate, w_up, w_down)
    return out.reshape((n_slots,) + ROW_TILE)


def _combine_ln_kernel(e0_ref, e1_ref, r0_ref, r1_ref, start_ref, rows_hbm, h_ref, gate_ref, g_ref, b_ref, o_ref,
                       buf, sem):
    tm = h_ref.shape[0]
    i = pl.program_id(0)

    def gather(tile, p):
        base = tile * tm

        def issue(r, c):
            t = base + r
            dst = pl.ds(pl.multiple_of(r * ROW_TILE[0], ROW_TILE[0]), ROW_TILE[0])
            pltpu.make_async_copy(rows_hbm.at[start_ref[e0_ref[t]] + r0_ref[t]], buf.at[p, 0, dst],
                                  sem.at[p]).start(priority=0)
            pltpu.make_async_copy(rows_hbm.at[start_ref[e1_ref[t]] + r1_ref[t]], buf.at[p, 1, dst],
                                  sem.at[p]).start(priority=1)
            return c

        lax.fori_loop(0, tm, issue, 0, unroll=8)

    @pl.when(i == 0)
    def _():
        gather(0, 0)

    for p in range(2):
        @pl.when(i % 2 == p)
        def _(p=p):
            @pl.when(i + 1 < pl.num_programs(0))
            def _():
                gather(i + 1, 1 - p)

            for k in range(TOP_K):
                pltpu.make_async_copy(buf.at[1 - p, k], buf.at[p, k], sem.at[p]).wait()
            gate = gate_ref[...]
            acc = (ALPHA * h_ref[...] + gate[:, 0:1] * _tiles_to_rows(buf.at[p, 0])
                   + gate[:, 1:2] * _tiles_to_rows(buf.at[p, 1]))
            o_ref[...] = _layer_norm_rows(acc, g_ref[...], b_ref[...])


def _combine_ln(e0, e1, r0, r1, pad_start, rows_tiles, h, gates, g_row, b_row):
    tok = h.shape[0]
    tm = TM_COMBINE
    row = lambda i, *_: (i, 0)
    const = lambda i, *_: (0, 0)
    grid_spec = pltpu.PrefetchScalarGridSpec(
        num_scalar_prefetch=5,
        grid=(tok // tm,),
        in_specs=[
            pl.BlockSpec(memory_space=pl.ANY),
            pl.BlockSpec((tm, D_MODEL), row),
            pl.BlockSpec((tm, LANES), row),
            pl.BlockSpec((1, D_MODEL), const),
            pl.BlockSpec((1, D_MODEL), const),
        ],
        out_specs=pl.BlockSpec((tm, D_MODEL), row),
        scratch_shapes=[pltpu.VMEM((2, TOP_K, tm * ROW_TILE[0], LANES), F32), pltpu.SemaphoreType.DMA((2,))],
    )
    return pl.pallas_call(
        _combine_ln_kernel,
        out_shape=jax.ShapeDtypeStruct((tok, D_MODEL), F32),
        grid_spec=grid_spec,
        compiler_params=_params("arbitrary"),
        name="moe_combine_ln",
    )(e0, e1, r0, r1, pad_start, rows_tiles, h, gates, g_row, b_row)


def _slot_layout(counts, n_blocks):
    padded = (counts + MOE_ROWS - 1) // MOE_ROWS * MOE_ROWS
    pad_end = jnp.cumsum(padded)
    n_used = pad_end[-1:] // MOE_ROWS
    blocks = jnp.arange(n_blocks, dtype=jnp.int32)
    first_row = jnp.minimum(blocks, n_used - 1) * MOE_ROWS
    block_expert = jnp.sum(first_row[:, None] >= pad_end[None, :], axis=1)
    return (pad_end - padded).astype(jnp.int32), block_expert.astype(jnp.int32), n_used.astype(jnp.int32)


def _moe_ln(h, h_tiles, w_router, b_router, w_gate, w_up, w_down, layer, g_row, b_row):
    tok = h.shape[0]
    n_blocks = tok * TOP_K // MOE_ROWS + N_EXPERTS
    ids, gates, counts = _router(h, w_router, b_router)
    e0, e1, r0, r1 = (ids[c] for c in range(4))
    counts = counts[0, MOE_GROUPS:MOE_GROUPS + N_EXPERTS]
    pad_start, block_expert, n_used = _slot_layout(counts, n_blocks)
    xs = _dispatch(h_tiles.reshape((tok,) + ROW_TILE), e0, e1, r0, r1, pad_start, counts, n_used,
                   n_blocks * MOE_ROWS)
    rows = _expert_ffn(xs, block_expert, n_used, w_gate, w_up, w_down, layer)
    return _combine_ln(e0, e1, r0, r1, pad_start, rows, h, gates, g_row, b_row)


def _pad_lanes(v):
    return jnp.pad(v, (0, LANES - v.shape[0])).reshape(1, LANES)


def _even_mixer(h, bsz, seq, w_in, conv_a, conv_w, conv_b, dt_bias, a_log, d_skip, norm_w, w_out, g_row, b_row):
    w = jnp.pad(w_in, ((0, 0), (0, AB_PROJ - w_in.shape[1]))).astype(MXU_DTYPE)
    y_a, z, xbc, dt, acs, acst = _even_front(
        h, w, conv_a, conv_w, conv_b.reshape(1, -1), _pad_lanes(dt_bias), _pad_lanes(-jnp.exp(a_log)), bsz, seq)
    dskip_row = jnp.repeat(d_skip, SSM_HEAD_DIM).reshape(1, -1)
    y_b = _ssd(xbc, dt, acs, acst, z, dskip_row, norm_w.reshape(1, -1), bsz, seq)
    return _outproj_ln([y_a, y_b], h, w_out.astype(MXU_DTYPE), g_row, b_row)


def _odd_mixer(h, bsz, seq, w_in, i_bias, f_bias, hnorm_w, fox_f_bias, w_out, g_row, b_row):
    c = np.cumsum((0, MLSTM_W, MLSTM_W, MLSTM_W, MLSTM_HEADS, MLSTM_HEADS, MLSTM_W, FOX_W, FOX_W, FOX_W, FOX_HEADS))
    part = lambda j: w_in[:, c[j]:c[j + 1]]
    q, k, v, i_pre, f_pre, o_pre, fq, fk, fv, ff = (part(j) for j in range(10))
    gate_cols = jnp.concatenate([i_pre, f_pre, ff, f_pre], axis=1)
    gate_cols = jnp.pad(gate_cols, ((0, 0), (0, LANES - gate_cols.shape[1])))
    spread = lambda m: jnp.pad(m.reshape(-1, FOX_HEADS, FOX_HEAD_DIM),
                               ((0, 0), (0, 0), (0, LANES - FOX_HEAD_DIM))).reshape(-1, FOX_AUG)
    w = jnp.concatenate([q, k, v, o_pre, fq, spread(fk), spread(fv), gate_cols], axis=1).astype(MXU_DTYPE)
    gate_bias = _pad_lanes(jnp.concatenate([i_bias, f_bias, fox_f_bias, f_bias]))
    q, k, v, o, fq, fk_aug, fvt_aug, gates, gates_t = _odd_front(h, w, gate_bias, bsz, seq)
    y_c = _mlstm(q, k, v, o, gates, gates_t, hnorm_w.reshape(1, -1), bsz, seq)
    y_d = _fox(fq, fk_aug, fvt_aug, gates_t, bsz, seq)
    return _outproj_ln([y_c, y_d], h, w_out.astype(MXU_DTYPE), g_row, b_row)


def kernel(x, ab_w_in, ab_conv_a, ab_conv_ssm_w, ab_conv_ssm_b, ab_dt_bias, ab_a_log, ab_d_skip, ab_norm_w, ab_w_out, cd_w_in, cd_i_bias, cd_f_bias, cd_hnorm_w, cd_fox_f_bias, cd_w_out, ln1_g, ln1_b, ln2_g, ln2_b, moe_rg_w, moe_rg_b, moe_re_w, moe_re_b, moe_w_gate, moe_w_up, moe_w_down):
    bsz, seq, d = x.shape
    h = x.reshape(bsz * seq, d)
    stack = lambda w: w.reshape((w.shape[0] * w.shape[1],) + w.shape[2:])
    w_gate, w_up, w_down = stack(moe_w_gate), stack(moe_w_up), stack(moe_w_down)
    for layer in range(DEPTH):
        j = layer // 2
        g1, b1 = ln1_g[layer].reshape(1, -1), ln1_b[layer].reshape(1, -1)
        if layer % 2 == 0:
            h, h_tiles = _even_mixer(h, bsz, seq, ab_w_in[j], ab_conv_a[j], ab_conv_ssm_w[j], ab_conv_ssm_b[j],
                                     ab_dt_bias[j], ab_a_log[j], ab_d_skip[j], ab_norm_w[j], ab_w_out[j], g1, b1)
        else:
            h, h_tiles = _odd_mixer(h, bsz, seq, cd_w_in[j], cd_i_bias[j], cd_f_bias[j], cd_hnorm_w[j],
                                    cd_fox_f_bias[j], cd_w_out[j], g1, b1)
        re_w = jnp.transpose(moe_re_w[layer], (1, 0, 2)).reshape(d, N_EXPERTS)
        w_router = jnp.pad(jnp.concatenate([moe_rg_w[layer], re_w], axis=1),
                           ((0, 0), (0, LANES - MOE_GROUPS - N_EXPERTS)))
        b_router = _pad_lanes(jnp.concatenate([moe_rg_b[layer], moe_re_b[layer].reshape(-1)]))
        h = _moe_ln(h, h_tiles, w_router, b_router, w_gate, w_up, w_down, layer,
                    ln2_g[layer].reshape(1, -1), ln2_b[layer].reshape(1, -1))
    return h.reshape(bsz, seq, d)
```

```python
import functools

import numpy as np
import jax
import jax.numpy as jnp
from jax import lax
from jax.experimental import pallas as pl
from jax.experimental.pallas import tpu as pltpu

F32 = jnp.float32
MXU_DTYPE = jnp.bfloat16
HIGHEST = lax.Precision.HIGHEST

D_MODEL = 1024
DEPTH = 4
ALPHA = (2 * DEPTH) ** 0.25
LN_EPS = 1e-5
CONV_DIM = D_MODEL // 2
CONV_WIDTH = 3
SSM_D_INNER = D_MODEL
SSM_HEAD_DIM = 64
SSM_HEADS = SSM_D_INNER // SSM_HEAD_DIM
SSM_GROUPS = 4
SSM_STATE = 64
SSM_CONV = 4
SSM_BC = SSM_GROUPS * SSM_STATE
SSM_CONV_DIM = SSM_D_INNER + 2 * SSM_BC
MLSTM_HEADS = 4
MLSTM_HEAD_DIM = D_MODEL // 8
MLSTM_W = MLSTM_HEADS * MLSTM_HEAD_DIM
FOX_HEADS = 8
FOX_HEAD_DIM = D_MODEL // 16
FOX_W = FOX_HEADS * FOX_HEAD_DIM
MOE_GROUPS = 4
EXPERTS_PER_GROUP = 8
N_EXPERTS = MOE_GROUPS * EXPERTS_PER_GROUP
TOP_K = 2
D_EXPERT = D_MODEL // 2

LANES = 128
SUBLANES = 8
VMEM_LIMIT_BYTES = 56 * 1024 * 1024

CHUNK = 128
SEQ_PAIR = 2
MLSTM_SEQS = 1
TM_FRONT = 512
TM_EVEN_FRONT = 256
TM_OUT = 1024
TM_ROUTER = 512
TQ_FOX = 256
MOE_ROWS = 512
TM_COMBINE = 256
TM_DISPATCH = 512

AB_PROJ = 4224
FOX_AUG = FOX_HEADS * LANES
CD_FK = 5 * 512
CD_FV = CD_FK + FOX_AUG
CD_GATES = CD_FV + FOX_AUG
CD_PROJ = CD_GATES + LANES
LOG2E = 1.4426950408889634
TK_FOX = 128
FOX_ACC_ROWS = FOX_HEAD_DIM + SUBLANES
G_I, G_F, G_FOX, G_BCUM = 0, 4, 8, 16


def _params(*sem):
    return pltpu.CompilerParams(dimension_semantics=sem, vmem_limit_bytes=VMEM_LIMIT_BYTES)


def _softplus(x):
    return jnp.maximum(x, 0.0) + jnp.log(1.0 + jnp.exp(-jnp.abs(x)))


def _sigmoid(x):
    return 1.0 / (1.0 + jnp.exp(-x))


def _layer_norm_rows(v, g, b):
    mu = jnp.mean(v, axis=-1, keepdims=True)
    c = v - mu
    var = jnp.mean(c * c, axis=-1, keepdims=True)
    return c * lax.rsqrt(var + LN_EPS) * g + b


def _tril(n, block):
    i = np.arange(n)
    m = (i[:, None] >= i[None, :]) & (i[:, None] // block == i[None, :] // block)
    return jnp.asarray(m.astype(np.float32), MXU_DTYPE)


def _full(shape):
    return pl.BlockSpec(shape, lambda *_: (0,) * len(shape), pipeline_mode=pl.Buffered(1))


ROW_TILE = (D_MODEL // LANES, LANES)


def _tiles_to_rows(ref):
    n = ref.shape[0] // ROW_TILE[0]
    return jnp.concatenate([ref[pl.ds(s, n, stride=ROW_TILE[0]), :] for s in range(ROW_TILE[0])], axis=1)


def _rows_to_tiles(ref, val):
    n = val.shape[0]
    for s in range(ROW_TILE[0]):
        ref[pl.ds(s, n, stride=ROW_TILE[0]), :] = val[:, s * LANES:(s + 1) * LANES]


def _split3(x):
    narrow = lambda v: v.astype(MXU_DTYPE).astype(F32)
    x1 = narrow(x)
    x2 = narrow(x - x1)
    return x1, x2, narrow(x - x1 - x2)


def _cumsum_rows(tril, x):
    parts = jnp.dot(tril, jnp.concatenate(_split3(x), axis=1).astype(MXU_DTYPE), preferred_element_type=F32)
    return parts[:, 0:LANES] + parts[:, LANES:2 * LANES] + parts[:, 2 * LANES:3 * LANES]


def _even_front_kernel(h_ref, w_ref, ca_ref, cw_ref, cb_ref, dtb_ref, aneg_ref, tril_ref,
                       ya_ref, z_ref, xbc_ref, dt_ref, acs_ref, acst_ref,
                       proj_buf, ua_ext, xbc_ext, *, tiles_per_seq):
    tm = h_ref.shape[0]
    i = pl.program_id(0)
    cur = i % 2

    @pl.when(i == 0)
    def _():
        proj_buf[1] = jnp.zeros(proj_buf.shape[1:], F32)

    @pl.when((i == 0) | ((i - 1) % tiles_per_seq == 0))
    def _():
        ua_ext[0:SUBLANES, :] = jnp.zeros((SUBLANES, CONV_DIM), F32)
        xbc_ext[0:SUBLANES, :] = jnp.zeros((SUBLANES, SSM_CONV_DIM), F32)

    proj_buf[cur] = jnp.dot(h_ref[...].astype(MXU_DTYPE), w_ref[...], preferred_element_type=F32)

    proj = proj_buf.at[1 - cur]
    z0 = 3 * CONV_DIM
    x0 = z0 + SSM_D_INNER
    d0 = x0 + SSM_CONV_DIM
    z_ref[...] = proj[:, z0:x0]

    ua_ext[SUBLANES:SUBLANES + tm, :] = proj[:, CONV_DIM:2 * CONV_DIM] * proj[:, 2 * CONV_DIM:3 * CONV_DIM]
    conv = ca_ref[0:1, :] * ua_ext[pl.ds(SUBLANES - 2, tm), :]
    for k in range(1, CONV_WIDTH):
        conv = conv + ca_ref[k:k + 1, :] * ua_ext[pl.ds(SUBLANES - (CONV_WIDTH - 1) + k, tm), :]
    ya_ref[...] = proj[:, 0:CONV_DIM] * conv
    ua_ext[0:SUBLANES, :] = ua_ext[tm:tm + SUBLANES, :]

    xbc_ext[SUBLANES:SUBLANES + tm, :] = proj[:, x0:d0]
    conv = cb_ref[...] + cw_ref[0:1, :] * xbc_ext[pl.ds(SUBLANES - (SSM_CONV - 1), tm), :]
    for k in range(1, SSM_CONV):
        conv = conv + cw_ref[k:k + 1, :] * xbc_ext[pl.ds(SUBLANES - (SSM_CONV - 1) + k, tm), :]
    xbc_ref[...] = conv * _sigmoid(conv)
    xbc_ext[0:SUBLANES, :] = xbc_ext[tm:tm + SUBLANES, :]

    dt = _softplus(proj[:, d0:d0 + LANES] + dtb_ref[...])
    a = dt * aneg_ref[...]
    acs = _cumsum_rows(tril_ref[...], a)
    dt_ref[...] = dt
    acs_ref[...] = acs
    acst_ref[...] = acs.T


def _even_front(h, w_in, conv_a, conv_w, conv_b, dt_bias_row, aneg_row, bsz, seq):
    tok = bsz * seq
    tm = TM_EVEN_FRONT
    n_tiles = tok // tm
    row = lambda i: (jnp.maximum(i - 1, 0), 0)
    out_shapes = (
        jax.ShapeDtypeStruct((tok, CONV_DIM), F32),
        jax.ShapeDtypeStruct((tok, SSM_D_INNER), F32),
        jax.ShapeDtypeStruct((tok, SSM_CONV_DIM), F32),
        jax.ShapeDtypeStruct((tok, LANES), F32),
        jax.ShapeDtypeStruct((tok, LANES), F32),
        jax.ShapeDtypeStruct((LANES, tok), F32),
    )
    return pl.pallas_call(
        functools.partial(_even_front_kernel, tiles_per_seq=seq // tm),
        out_shape=out_shapes,
        grid=(n_tiles + 1,),
        in_specs=[
            pl.BlockSpec((tm, D_MODEL), lambda i: (jnp.minimum(i, n_tiles - 1), 0)),
            _full((D_MODEL, AB_PROJ)),
            _full((CONV_WIDTH, CONV_DIM)),
            _full((SSM_CONV, SSM_CONV_DIM)),
            _full((1, SSM_CONV_DIM)),
            _full((1, LANES)),
            _full((1, LANES)),
            _full((tm, tm)),
        ],
        out_specs=(
            pl.BlockSpec((tm, CONV_DIM), row),
            pl.BlockSpec((tm, SSM_D_INNER), row),
            pl.BlockSpec((tm, SSM_CONV_DIM), row),
            pl.BlockSpec((tm, LANES), row),
            pl.BlockSpec((tm, LANES), row),
            pl.BlockSpec((LANES, tm), lambda i: (0, jnp.maximum(i - 1, 0))),
        ),
        scratch_shapes=[
            pltpu.VMEM((2, tm, AB_PROJ), F32),
            pltpu.VMEM((tm + SUBLANES, CONV_DIM), F32),
            pltpu.VMEM((tm + SUBLANES, SSM_CONV_DIM), F32),
        ],
        compiler_params=_params("arbitrary"),
        name="even_front",
    )(h, w_in, conv_a, conv_w, conv_b, dt_bias_row, aneg_row, _tril(tm, CHUNK))


def _bcast_heads(arr, n_heads, width):
    per = LANES // width
    length = arr.shape[0]
    lane = lax.broadcasted_iota(jnp.int32, (length, LANES), 1)
    outs = []
    for j in range(n_heads // per):
        v = jnp.broadcast_to(arr[:, j * per:j * per + 1], (length, LANES))
        for r in range(1, per):
            v = jnp.where(lane >= r * width, jnp.broadcast_to(arr[:, j * per + r:j * per + r + 1], (length, LANES)), v)
        outs.append(v)
    return jnp.concatenate(outs, axis=1)


def _expand_heads(arr, e3_ref):
    return jnp.dot(jnp.concatenate(_split3(arr), axis=1).astype(MXU_DTYPE), e3_ref[...],
                   preferred_element_type=F32)


def _ssd_kernel(*refs):
    nb = SEQ_PAIR
    xbc_ref, dt_ref, acs_ref, z_ref = refs[:4]
    acst_refs = refs[4:4 + nb]
    dskip_ref, nw_ref, e3_ref, y_ref, state = refs[4 + nb:]
    L = CHUNK
    P = SSM_HEAD_DIM
    R = SSM_HEADS // SSM_GROUPS
    GW = R * P

    @pl.when(pl.program_id(1) == 0)
    def _():
        state[...] = jnp.zeros(state.shape, F32)

    row = lax.broadcasted_iota(jnp.int32, (L, L), 0)
    col = lax.broadcasted_iota(jnp.int32, (L, L), 1)
    causal = row >= col
    lane_g = lax.broadcasted_iota(jnp.int32, (L, GW), 1)

    per_row = []
    for j in range(nb):
        acs = acs_ref[j]
        per_row += [dt_ref[j], jnp.exp(acs[L - 1:L, :] - acs), jnp.exp(acs)]
    expanded = _expand_heads(jnp.concatenate(per_row, axis=0), e3_ref)

    st_old = [state[c] for c in range(nb * SSM_GROUPS)]
    st_new = []
    for j in range(nb):
        xs = xbc_ref[j, :, 0:SSM_D_INNER]
        bm = xbc_ref[j, :, SSM_D_INNER:SSM_D_INNER + SSM_BC]
        cm = xbc_ref[j, :, SSM_D_INNER + SSM_BC:SSM_CONV_DIM]
        acs = acs_ref[j]
        acst = acst_refs[j][...]
        a_last = acs[L - 1:L, :]
        dtx = expanded[(3 * j) * L:(3 * j + 1) * L, :]
        decx = expanded[(3 * j + 1) * L:(3 * j + 2) * L, :]
        expx = expanded[(3 * j + 2) * L:(3 * j + 3) * L, :]
        xdt = xs * dtx
        xdec = (xdt * decx).astype(MXU_DTYPE)
        xdt_m = xdt.astype(MXU_DTYPE)
        chunk_decay = jnp.exp(jnp.broadcast_to(a_last, (SUBLANES, LANES)))
        cdx = _bcast_heads(chunk_decay, SSM_HEADS, P)[0:1, :]

        bm_t = bm.T.astype(MXU_DTYPE)
        cm_m = cm.astype(MXU_DTYPE)
        bm_m = bm.astype(MXU_DTYPE)

        ys = []
        for g in range(SSM_GROUPS):
            n0 = g * SSM_STATE
            c_g = cm_m[:, n0:n0 + SSM_STATE]
            cb = lax.dot_general(c_g, bm_m[:, n0:n0 + SSM_STATE], (((1,), (1,)), ((), ())),
                                 preferred_element_type=F32)
            ms = []
            for r in range(R):
                hd = g * R + r
                seg = jnp.exp(jnp.where(causal, acs[:, hd:hd + 1] - acst[hd:hd + 1, :], -jnp.inf))
                ms.append((cb * seg).astype(MXU_DTYPE))
            big = jnp.dot(jnp.concatenate(ms, axis=0), xdt_m[:, g * GW:(g + 1) * GW],
                          preferred_element_type=F32)
            y_diag = big[0:L, :]
            for r in range(1, R):
                y_diag = jnp.where(lane_g >= r * P, big[r * L:(r + 1) * L, :], y_diag)
            st = st_old[j * SSM_GROUPS + g]
            y_off = jnp.dot(c_g, st.astype(MXU_DTYPE), preferred_element_type=F32)
            new = jnp.dot(bm_t[n0:n0 + SSM_STATE, :], xdec[:, g * GW:(g + 1) * GW],
                          preferred_element_type=F32)
            st_new.append(st * cdx[:, g * GW:(g + 1) * GW] + new)
            ys.append(y_diag + y_off * expx[:, g * GW:(g + 1) * GW])
        y = jnp.concatenate(ys, axis=1) + dskip_ref[...] * xs
        z = z_ref[j]
        u = y * (z * _sigmoid(z))
        y_ref[j] = u * lax.rsqrt(jnp.mean(u * u, axis=-1, keepdims=True) + LN_EPS) * nw_ref[...]

    for c in range(nb * SSM_GROUPS):
        state[c] = st_new[c]


def _ssd(xbc, dt, acs, acst, z, dskip_row, normw_row, bsz, seq):
    nb = SEQ_PAIR
    nc = seq // CHUNK
    per_seq = lambda a: a.reshape(bsz, seq, a.shape[-1])
    blk = lambda width: pl.BlockSpec((nb, CHUNK, width), lambda g, c: (g, c, 0))
    acst_specs = [pl.BlockSpec((LANES, CHUNK), lambda g, c, j=j: (0, (g * nb + j) * nc + c)) for j in range(nb)]
    expand = np.zeros((LANES, SSM_D_INNER), np.float32)
    for hd in range(SSM_HEADS):
        expand[hd, hd * SSM_HEAD_DIM:(hd + 1) * SSM_HEAD_DIM] = 1.0
    e3 = jnp.asarray(np.concatenate([expand] * 3, axis=0), MXU_DTYPE)
    y = pl.pallas_call(
        _ssd_kernel,
        out_shape=jax.ShapeDtypeStruct((bsz, seq, SSM_D_INNER), F32),
        grid=(bsz // nb, nc),
        in_specs=[blk(SSM_CONV_DIM), blk(LANES), blk(LANES), blk(SSM_D_INNER)] + acst_specs + [
            _full((1, SSM_D_INNER)),
            _full((1, SSM_D_INNER)),
            _full((3 * LANES, SSM_D_INNER)),
        ],
        out_specs=blk(SSM_D_INNER),
        scratch_shapes=[pltpu.VMEM((nb * SSM_GROUPS, SSM_STATE, SSM_D_INNER // SSM_GROUPS), F32)],
        compiler_params=_params("arbitrary", "arbitrary"),
        name="ssd_scan",
    )(per_seq(xbc), per_seq(dt), per_seq(acs), per_seq(z), *([acst] * nb), dskip_row, normw_row, e3)
    return y.reshape(bsz * seq, SSM_D_INNER)


def _outproj_ln_kernel(*refs, widths):
    n = len(widths)
    parts = refs[:n]
    h_ref, w_ref, g_ref, b_ref, o_ref, ot_ref = refs[n:]
    acc = ALPHA * h_ref[...]
    off = 0
    for p, wd in zip(parts, widths):
        acc = acc + jnp.dot(p[...].astype(MXU_DTYPE), w_ref[off:off + wd, :], preferred_element_type=F32)
        off += wd
    out = _layer_norm_rows(acc, g_ref[...], b_ref[...])
    o_ref[...] = out
    _rows_to_tiles(ot_ref, out)


def _outproj_ln(parts, h, w_out, g_row, b_row):
    tok = h.shape[0]
    tm = TM_OUT
    widths = tuple(p.shape[1] for p in parts)
    row = lambda i: (i, 0)
    return pl.pallas_call(
        functools.partial(_outproj_ln_kernel, widths=widths),
        out_shape=(jax.ShapeDtypeStruct((tok, D_MODEL), F32),
                   jax.ShapeDtypeStruct((tok * ROW_TILE[0], LANES), F32)),
        grid=(tok // tm,),
        in_specs=[pl.BlockSpec((tm, wd), row) for wd in widths] + [
            pl.BlockSpec((tm, D_MODEL), row),
            _full((sum(widths), D_MODEL)),
            _full((1, D_MODEL)),
            _full((1, D_MODEL)),
        ],
        out_specs=(pl.BlockSpec((tm, D_MODEL), row), pl.BlockSpec((tm * ROW_TILE[0], LANES), row)),
        compiler_params=_params("arbitrary"),
        name="outproj_ln",
    )(*parts, h, w_out, g_row, b_row)


def _odd_front_kernel(h_ref, w_ref, gb_ref, tril_ref,
                      q_ref, k_ref, v_ref, o_ref, fq_ref, fk_ref, fvt_ref, g_ref, gt_ref, carry):
    tm = h_ref.shape[0]
    proj = jnp.dot(h_ref[...].astype(MXU_DTYPE), w_ref[...], preferred_element_type=F32)
    q_ref[...] = proj[:, 0:512]
    k_ref[...] = proj[:, 512:1024] * (MLSTM_HEAD_DIM ** -0.5)
    v_ref[...] = proj[:, 1024:1536]
    o_ref[...] = _sigmoid(proj[:, 1536:2048])
    fq_ref[...] = proj[:, 2048:2560] * (FOX_HEAD_DIM ** -0.5 * LOG2E)

    @pl.when(pl.program_id(1) == 0)
    def _():
        carry[...] = jnp.zeros(carry.shape, F32)

    raw = proj[:, CD_GATES:CD_GATES + LANES] + gb_ref[...]
    lane = lax.broadcasted_iota(jnp.int32, (tm, LANES), 1)
    g = jnp.where(lane < G_F, raw, -_softplus(-raw))
    prev = carry[0:1, :]
    glob = _cumsum_rows(tril_ref[...], g) + prev
    before = []
    for c in range(tm // CHUNK):
        before.append(jnp.broadcast_to(prev, (CHUNK, LANES)))
        prev = glob[(c + 1) * CHUNK - 1:(c + 1) * CHUNK, :]
    carry[...] = jnp.broadcast_to(prev, carry.shape)
    local = glob - jnp.concatenate(before, axis=0)
    out = jnp.where((lane >= G_FOX) & (lane < G_BCUM), glob, jnp.where(lane >= G_BCUM, local, g))
    g_ref[...] = out
    gt_ref[...] = out.T

    is_bias = (lane >= FOX_HEAD_DIM) & (lane < FOX_HEAD_DIM + 3)
    for hd in range(FOX_HEADS):
        c1, c2, c3 = _split3(out[:, G_FOX + hd:G_FOX + hd + 1] * (-LOG2E))
        bias = jnp.where(lane == FOX_HEAD_DIM, c1, jnp.where(lane == FOX_HEAD_DIM + 1, c2, c3))
        k_h = proj[:, CD_FK + hd * LANES:CD_FK + (hd + 1) * LANES]
        fk_ref[:, hd * LANES:(hd + 1) * LANES] = jnp.where(is_bias, bias, k_h).astype(MXU_DTYPE)
    ones_lane = jnp.where(lax.broadcasted_iota(jnp.int32, (tm, FOX_AUG), 1) % LANES == FOX_HEAD_DIM, 1.0, 0.0)
    fvt_ref[...] = (proj[:, CD_FV:CD_FV + FOX_AUG] + ones_lane).T.astype(MXU_DTYPE)


def _odd_front(h, w_in, gate_bias_row, bsz, seq):
    tok = bsz * seq
    tm = TM_FRONT
    ns = seq // tm
    row = lambda b, s: (b * ns + s, 0)
    col = lambda b, s: (0, b * ns + s)
    wide = jax.ShapeDtypeStruct((tok, 512), F32)
    return pl.pallas_call(
        _odd_front_kernel,
        out_shape=(wide,) * 5 + (jax.ShapeDtypeStruct((tok, FOX_AUG), MXU_DTYPE),
                                 jax.ShapeDtypeStruct((FOX_AUG, tok), MXU_DTYPE),
                                 jax.ShapeDtypeStruct((tok, LANES), F32),
                                 jax.ShapeDtypeStruct((LANES, tok), F32)),
        grid=(bsz, ns),
        in_specs=[
            pl.BlockSpec((tm, D_MODEL), row),
            _full((D_MODEL, CD_PROJ)),
            _full((1, LANES)),
            _full((tm, tm)),
        ],
        out_specs=(pl.BlockSpec((tm, 512), row),) * 5 + (
            pl.BlockSpec((tm, FOX_AUG), row),
            pl.BlockSpec((FOX_AUG, tm), col),
            pl.BlockSpec((tm, LANES), row),
            pl.BlockSpec((LANES, tm), col),
        ),
        scratch_shapes=[pltpu.VMEM((SUBLANES, LANES), F32)],
        compiler_params=_params("arbitrary", "arbitrary"),
        name="odd_front",
    )(h, w_in, gate_bias_row, _tril(tm, tm))


def _mlstm_kernel(*refs):
    nb = MLSTM_SEQS
    q_ref, k_ref, v_ref, o_ref, g_ref = refs[:5]
    gt_refs = refs[5:5 + nb]
    nw_ref, y_ref, c_state, m_state = refs[5 + nb:]
    L = CHUNK
    DH = MLSTM_HEAD_DIM

    @pl.when(pl.program_id(1) == 0)
    def _():
        c_state[...] = jnp.zeros(c_state.shape, F32)
        m_state[...] = jnp.zeros(m_state.shape, F32)

    row = lax.broadcasted_iota(jnp.int32, (L, L), 0)
    col = lax.broadcasted_iota(jnp.int32, (L, L), 1)
    causal = row >= col
    ones_col = jnp.where(lax.broadcasted_iota(jnp.int32, (L, DH), 1) == 0, 1.0, 0.0)

    for j in range(nb):
        gates = g_ref[j]
        gates_t = gt_refs[j][...]
        for hd in range(MLSTM_HEADS):
            st = j * MLSTM_HEADS + hd
            sl = slice(hd * DH, (hd + 1) * DH)
            q = q_ref[j, :, sl].astype(MXU_DTYPE)
            k = k_ref[j, :, sl]
            v_ext = jnp.concatenate([v_ref[j, :, sl], ones_col], axis=1).astype(MXU_DTYPE)
            b_col = gates[:, G_BCUM + hd:G_BCUM + hd + 1]
            i_col = gates[:, G_I + hd:G_I + hd + 1]
            b_row = gates_t[G_BCUM + hd:G_BCUM + hd + 1, :]
            i_row = gates_t[G_I + hd:G_I + hd + 1, :]
            m_prev = m_state[st:st + 1, 0:1]
            c_ext = c_state[st]

            d_mat = jnp.where(causal, b_col - b_row + i_row, -jnp.inf)
            inter = b_col + m_prev
            m_t = jnp.maximum(jnp.max(d_mat, axis=-1, keepdims=True), inter)
            s_qk = lax.dot_general(q, k.astype(MXU_DTYPE), (((1,), (1,)), ((), ())), preferred_element_type=F32)
            w_qk = s_qk * jnp.exp(d_mat - m_t)
            s_inter = jnp.exp(inter - m_t)
            num_ext = (jnp.dot(w_qk.astype(MXU_DTYPE), v_ext, preferred_element_type=F32)
                       + s_inter * jnp.dot(q, c_ext.astype(MXU_DTYPE), preferred_element_type=F32))
            den = num_ext[:, DH:DH + 1]
            hval = num_ext[:, 0:DH] / jnp.maximum(jnp.abs(den), jnp.exp(-m_t))

            b_last = b_col[L - 1:L, :]
            g_log = b_last - b_col + i_col
            m_new = jnp.maximum(b_last + m_prev, jnp.max(g_log, axis=0, keepdims=True))
            w_k = jnp.exp(g_log - m_new)
            decay = jnp.exp(b_last + m_prev - m_new)
            kw_t = (k * w_k).T.astype(MXU_DTYPE)
            c_state[st] = decay * c_ext + jnp.dot(kw_t, v_ext, preferred_element_type=F32)
            m_state[st:st + 1, :] = jnp.broadcast_to(m_new, (1, LANES))

            mu = jnp.mean(hval, axis=-1, keepdims=True)
            cen = hval - mu
            var = jnp.mean(cen * cen, axis=-1, keepdims=True)
            y_ref[j, :, sl] = o_ref[j, :, sl] * (cen * lax.rsqrt(var + LN_EPS) * nw_ref[:, sl])


def _mlstm(q, k, v, o, gates, gates_t, hnorm_row, bsz, seq):
    nb = MLSTM_SEQS
    nc = seq // CHUNK
    per_seq = lambda a: a.reshape(bsz, seq, a.shape[-1])
    blk = lambda width: pl.BlockSpec((nb, CHUNK, width), lambda g, c: (g, c, 0))
    gt_specs = [pl.BlockSpec((LANES, CHUNK), lambda g, c, j=j: (0, (g * nb + j) * nc + c)) for j in range(nb)]
    y = pl.pallas_call(
        _mlstm_kernel,
        out_shape=jax.ShapeDtypeStruct((bsz, seq, MLSTM_W), F32),
        grid=(bsz // nb, nc),
        in_specs=[blk(MLSTM_W)] * 4 + [blk(LANES)] + gt_specs + [_full((1, MLSTM_W))],
        out_specs=blk(MLSTM_W),
        scratch_shapes=[pltpu.VMEM((nb * MLSTM_HEADS, MLSTM_HEAD_DIM, 2 * MLSTM_HEAD_DIM), F32),
                        pltpu.VMEM((nb * MLSTM_HEADS, LANES), F32)],
        compiler_params=_params("arbitrary", "arbitrary"),
        name="mlstm_scan",
    )(per_seq(q), per_seq(k), per_seq(v), per_seq(o), per_seq(gates), *([gates_t] * nb), hnorm_row)
    return y.reshape(bsz * seq, MLSTM_W)


def _fox_kernel(q_ref, k_ref, vt_ref, gt_ref, y_ref, *scratch):
    acc_refs = scratch[:FOX_HEADS]
    qa_ref = scratch[FOX_HEADS]
    tq = q_ref.shape[0]
    tk = TK_FOX
    dh = FOX_HEAD_DIM
    qi = pl.program_id(1)
    q_t = q_ref[...].T
    bias_rows = jnp.where(lax.broadcasted_iota(jnp.int32, (LANES - dh, tq), 0) < 3, 1.0, 0.0)
    for hd in range(FOX_HEADS):
        qa_ref[hd] = jnp.concatenate([q_t[hd * dh:(hd + 1) * dh, :], bias_rows], axis=0).astype(MXU_DTYPE)
        acc_refs[hd][...] = jnp.zeros(acc_refs[hd].shape, F32)
    cq = gt_ref[G_FOX:G_FOX + FOX_HEADS, :] * LOG2E
    key_pos = lax.broadcasted_iota(jnp.int32, (tk, tq), 0)
    qry_pos = lax.broadcasted_iota(jnp.int32, (tk, tq), 1) + qi * tq
    n_full = qi * (tq // tk)

    def block(j, ms, masked):
        k0 = pl.multiple_of(j * tk, tk)
        out = []
        for hd in range(FOX_HEADS):
            hs = slice(hd * LANES, (hd + 1) * LANES)
            s = jnp.dot(k_ref[pl.ds(k0, tk), hs], qa_ref[hd], preferred_element_type=F32)
            if masked:
                s = jnp.where(key_pos + k0 <= qry_pos, s, -jnp.inf)
            cq_h = cq[hd:hd + 1, :]
            m_new = jnp.maximum(ms[hd], jnp.max(s, axis=0, keepdims=True) + cq_h)
            p = jnp.exp2(s - (m_new - cq_h))
            pv = jnp.dot(vt_ref[hs, pl.ds(k0, tk)], p.astype(MXU_DTYPE), preferred_element_type=F32)
            acc_refs[hd][...] = jnp.exp2(ms[hd] - m_new) * acc_refs[hd][...] + pv[0:FOX_ACC_ROWS, :]
            out.append(m_new)
        return tuple(out)

    ms = tuple(jnp.full((1, tq), -jnp.inf, F32) for _ in range(FOX_HEADS))
    ms = lax.fori_loop(0, n_full, functools.partial(block, masked=False), ms)
    for d in range(tq // tk):
        ms = block(n_full + d, ms, masked=True)
    outs = []
    for hd in range(FOX_HEADS):
        acc = acc_refs[hd][...]
        outs.append(acc[0:dh, :] / acc[dh:dh + 1, :])
    y_ref[...] = jnp.concatenate(outs, axis=0).T


def _fox(fq, fk_aug, fvt_aug, gates_t, bsz, seq):
    tok = bsz * seq
    tq = TQ_FOX
    nq = seq // tq
    return pl.pallas_call(
        _fox_kernel,
        out_shape=jax.ShapeDtypeStruct((tok, FOX_W), F32),
        grid=(bsz, nq),
        in_specs=[
            pl.BlockSpec((tq, FOX_W), lambda b, i: (b * nq + i, 0)),
            pl.BlockSpec((seq, FOX_AUG), lambda b, i: (b, 0)),
            pl.BlockSpec((FOX_AUG, seq), lambda b, i: (0, b)),
            pl.BlockSpec((LANES, tq), lambda b, i: (0, b * nq + i)),
        ],
        out_specs=pl.BlockSpec((tq, FOX_W), lambda b, i: (b * nq + i, 0)),
        scratch_shapes=[pltpu.VMEM((FOX_ACC_ROWS, tq), F32)] * FOX_HEADS
                       + [pltpu.VMEM((FOX_HEADS, LANES, tq), MXU_DTYPE)],
        compiler_params=_params("arbitrary", "arbitrary"),
        name="fox_attention",
    )(fq, fk_aug, fvt_aug, gates_t)


def _router_kernel(h_ref, w_ref, b_ref, stril_ref, id_ref, gate_ref, cnt_ref, carry):
    tm = h_ref.shape[0]

    @pl.when(pl.program_id(0) == 0)
    def _():
        carry[...] = jnp.zeros(carry.shape, F32)

    h = h_ref[...]
    h_hi = h.astype(MXU_DTYPE)
    h_lo = (h - h_hi.astype(F32)).astype(MXU_DTYPE)
    both = jnp.dot(h_hi, w_ref[...], preferred_element_type=F32)
    logits = (both[:, 0:LANES] + both[:, LANES:2 * LANES]
              + jnp.dot(h_lo, w_ref[:, 0:LANES], preferred_element_type=F32) + b_ref[...])
    lane = lax.broadcasted_iota(jnp.int32, (tm, LANES), 1).astype(F32)
    neg = -jnp.inf
    first = lambda hit: jnp.min(jnp.where(hit, lane, float(LANES)), axis=-1, keepdims=True)
    gl = jnp.where(lane < MOE_GROUPS, logits, neg)
    g_max = jnp.max(gl, axis=-1, keepdims=True)
    g_idx = first(gl == g_max)
    p_group = 1.0 / jnp.sum(jnp.exp(gl - g_max), axis=-1, keepdims=True)
    e_lo = MOE_GROUPS + g_idx * EXPERTS_PER_GROUP
    el = jnp.where((lane >= e_lo) & (lane < e_lo + EXPERTS_PER_GROUP), logits, neg)
    v1 = jnp.max(el, axis=-1, keepdims=True)
    i1 = first(el == v1)
    el2 = jnp.where(lane == i1, neg, el)
    v2 = jnp.max(el2, axis=-1, keepdims=True)
    i2 = first(el2 == v2)
    t = jnp.exp(v2 - v1)
    w1 = 1.0 / (1.0 + t)
    gate_ref[...] = jnp.where(lane == 0.0, p_group * w1, jnp.where(lane == 1.0, p_group * (t * w1), 0.0))

    hit1 = lane == i1
    hit2 = lane == i2
    sent = jnp.where(hit1 | hit2, 1.0, 0.0)
    before = jnp.dot(stril_ref[...], sent.astype(jnp.bfloat16), preferred_element_type=F32) + carry[0:1, :]
    r1 = jnp.sum(jnp.where(hit1, before, 0.0), axis=-1, keepdims=True)
    r2 = jnp.sum(jnp.where(hit2, before, 0.0), axis=-1, keepdims=True)
    total = before[tm - 1:tm, :] + sent[tm - 1:tm, :]
    carry[...] = jnp.broadcast_to(total, carry.shape)
    cnt_ref[...] = jnp.broadcast_to(total, cnt_ref.shape).astype(jnp.int32)
    ids = jnp.where(lane == 0.0, i1 - MOE_GROUPS, jnp.where(lane == 1.0, i2 - MOE_GROUPS,
                    jnp.where(lane == 2.0, r1, jnp.where(lane == 3.0, r2, 0.0))))
    id_ref[...] = ids.T[0:SUBLANES, :].astype(jnp.int32)


def _router(h, w_router, b_router):
    tok = h.shape[0]
    tm = TM_ROUTER
    row = lambda i: (i, 0)
    idx = np.arange(tm)
    stril = jnp.asarray((idx[:, None] > idx[None, :]).astype(np.float32), jnp.bfloat16)
    w_hi = w_router.astype(MXU_DTYPE)
    w_hi_lo = (w_hi, (w_router - w_hi.astype(F32)).astype(MXU_DTYPE))
    return pl.pallas_call(
        _router_kernel,
        out_shape=(jax.ShapeDtypeStruct((SUBLANES, tok), jnp.int32), jax.ShapeDtypeStruct((tok, LANES), F32),
                   jax.ShapeDtypeStruct((SUBLANES, LANES), jnp.int32)),
        grid=(tok // tm,),
        in_specs=[pl.BlockSpec((tm, D_MODEL), row), _full((D_MODEL, 2 * LANES)), _full((1, LANES)),
                  _full((tm, tm))],
        out_specs=(pl.BlockSpec((SUBLANES, tm), lambda i: (0, i)), pl.BlockSpec((tm, LANES), row),
                   _full((SUBLANES, LANES))),
        scratch_shapes=[pltpu.VMEM((SUBLANES, LANES), F32)],
        compiler_params=_params("arbitrary"),
        name="moe_router",
    )(h, jnp.concatenate([w_hi, w_hi_lo[1]], axis=1), b_router, stril)


def _pad_pieces(n):
    return tuple(1 << b for b in reversed(range((n - 1).bit_length())))


def _dispatch_kernel(e0_ref, e1_ref, r0_ref, r1_ref, start_ref, cnt_ref, nused_ref, h_ref, xs_hbm, zeros, sem,
                     pad_sem):
    tm = h_ref.shape[0]
    i = pl.program_id(0)
    base = i * tm

    def issue(r, c):
        t = base + r
        pltpu.make_async_copy(h_ref.at[r], xs_hbm.at[start_ref[e0_ref[t]] + r0_ref[t]], sem).start(priority=0)
        pltpu.make_async_copy(h_ref.at[r], xs_hbm.at[start_ref[e1_ref[t]] + r1_ref[t]], sem).start(priority=1)
        return c

    lax.fori_loop(0, tm, issue, 0, unroll=8)

    def pad_copies(e, fn):
        cnt = cnt_ref[e]
        n_pad = (MOE_ROWS - cnt % MOE_ROWS) % MOE_ROWS
        first = start_ref[e] + cnt
        for piece in _pad_pieces(MOE_ROWS):
            @pl.when((n_pad & piece) != 0)
            def _(piece=piece):
                off = first + (n_pad & ~(2 * piece - 1))
                fn(pltpu.make_async_copy(zeros.at[pl.ds(0, piece)], xs_hbm.at[pl.ds(off, piece)], pad_sem))

    def tail_copies(blk, fn):
        for part in range(MOE_ROWS // zeros.shape[0]):
            off = blk * MOE_ROWS + part * zeros.shape[0]
            fn(pltpu.make_async_copy(zeros, xs_hbm.at[pl.ds(off, zeros.shape[0])], pad_sem))

    @pl.when(i == 0)
    def _():
        zeros[...] = jnp.zeros(zeros.shape, F32)
        n_blocks = xs_hbm.shape[0] // MOE_ROWS

        def start(e, c):
            pad_copies(e, lambda cp: cp.start())
            return c

        def wait(e, c):
            pad_copies(e, lambda cp: cp.wait())
            return c

        def tail_start(blk, c):
            tail_copies(blk, lambda cp: cp.start())
            return c

        def tail_wait(blk, c):
            tail_copies(blk, lambda cp: cp.wait())
            return c

        lax.fori_loop(0, N_EXPERTS, start, 0)
        lax.fori_loop(nused_ref[0], n_blocks, tail_start, 0)
        lax.fori_loop(0, N_EXPERTS, wait, 0)
        lax.fori_loop(nused_ref[0], n_blocks, tail_wait, 0)

    for _ in range(TOP_K):
        pltpu.make_async_copy(h_ref, xs_hbm.at[pl.ds(0, tm)], sem).wait()


def _dispatch(h_tiles, e0, e1, r0, r1, pad_start, counts, n_used, n_slots):
    tok = h_tiles.shape[0]
    tm = TM_DISPATCH
    grid_spec = pltpu.PrefetchScalarGridSpec(
        num_scalar_prefetch=7,
        grid=(tok // tm,),
        in_specs=[pl.BlockSpec((tm,) + ROW_TILE, lambda i, *_: (i, 0, 0))],
        out_specs=pl.BlockSpec(memory_space=pl.ANY),
        scratch_shapes=[pltpu.VMEM((MOE_ROWS // 2,) + ROW_TILE, F32),
                        pltpu.SemaphoreType.DMA(()), pltpu.SemaphoreType.DMA(())],
    )
    return pl.pallas_call(
        _dispatch_kernel,
        out_shape=jax.ShapeDtypeStruct((n_slots,) + ROW_TILE, F32),
        grid_spec=grid_spec,
        compiler_params=_params("arbitrary"),
        name="moe_dispatch",
    )(e0, e1, r0, r1, pad_start, counts, n_used, h_tiles)


def _expert_kernel(be_ref, nused_ref, x_ref, wg_ref, wu_ref, wd_ref, out_ref, wg_s, wu_s, wd_s):
    i = pl.program_id(0)

    @pl.when(i < nused_ref[0])
    def _():
        @pl.when((i == 0) | (be_ref[i] != be_ref[jnp.maximum(i - 1, 0)]))
        def _():
            wg_s[...] = wg_ref[0].astype(MXU_DTYPE)
            wu_s[...] = wu_ref[0].astype(MXU_DTYPE)
            wd_s[...] = wd_ref[0].astype(MXU_DTYPE)

        x = _tiles_to_rows(x_ref).astype(MXU_DTYPE)
        gate = jnp.dot(x, wg_s[...], preferred_element_type=F32)
        up = jnp.dot(x, wu_s[...], preferred_element_type=F32)
        hidden = (gate * _sigmoid(gate) * up).astype(MXU_DTYPE)
        _rows_to_tiles(out_ref, jnp.dot(hidden, wd_s[...], preferred_element_type=F32))

    @pl.when(i >= nused_ref[0])
    def _():
        out_ref[...] = jnp.zeros(out_ref.shape, F32)


def _expert_ffn(xs_tiles, block_expert, n_used, w_gate, w_up, w_down, layer):
    n_slots = xs_tiles.shape[0]
    n_blocks = block_expert.shape[0]
    rows = MOE_ROWS * ROW_TILE[0]
    blk = lambda i, be, nu: (jnp.maximum(jnp.minimum(i, nu[0] - 1), 0), 0)
    wsel = lambda i, be, nu: (layer * N_EXPERTS + be[i], 0, 0)
    grid_spec = pltpu.PrefetchScalarGridSpec(
        num_scalar_prefetch=2,
        grid=(n_blocks,),
        in_specs=[
            pl.BlockSpec((rows, LANES), blk),
            pl.BlockSpec((1, D_MODEL, D_EXPERT), wsel),
            pl.BlockSpec((1, D_MODEL, D_EXPERT), wsel),
            pl.BlockSpec((1, D_EXPERT, D_MODEL), wsel),
        ],
        out_specs=pl.BlockSpec((rows, LANES), lambda i, be, nu: (i, 0)),
        scratch_shapes=[pltpu.VMEM((D_MODEL, D_EXPERT), MXU_DTYPE), pltpu.VMEM((D_MODEL, D_EXPERT), MXU_DTYPE),
                        pltpu.VMEM((D_EXPERT, D_MODEL), MXU_DTYPE)],
    )
    out = pl.pallas_call(
        _expert_kernel,
        out_shape=jax.ShapeDtypeStruct((n_slots * ROW_TILE[0], LANES), F32),
        grid_spec=grid_spec,
        compiler_params=_params("arbitrary"),
        name="moe_experts",
    )(block_expert, n_used, xs_tiles.reshape(n_slots * ROW_TILE[0], LANES), w_gate, w_up, w_down)
    return out.reshape((n_slots,) + ROW_TILE)


def _combine_ln_kernel(e0_ref, e1_ref, r0_ref, r1_ref, start_ref, rows_hbm, h_ref, gate_ref, g_ref, b_ref, o_ref,
                       buf, sem):
    tm = h_ref.shape[0]
    i = pl.program_id(0)

    def gather(tile, p):
        base = tile * tm

        def issue(r, c):
            t = base + r
            dst = pl.ds(pl.multiple_of(r * ROW_TILE[0], ROW_TILE[0]), ROW_TILE[0])
            pltpu.make_async_copy(rows_hbm.at[start_ref[e0_ref[t]] + r0_ref[t]], buf.at[p, 0, dst],
                                  sem.at[p]).start(priority=0)
            pltpu.make_async_copy(rows_hbm.at[start_ref[e1_ref[t]] + r1_ref[t]], buf.at[p, 1, dst],
                                  sem.at[p]).start(priority=1)
            return c

        lax.fori_loop(0, tm, issue, 0, unroll=8)

    @pl.when(i == 0)
    def _():
        gather(0, 0)

    for p in range(2):
        @pl.when(i % 2 == p)
        def _(p=p):
            @pl.when(i + 1 < pl.num_programs(0))
            def _():
                gather(i + 1, 1 - p)

            for k in range(TOP_K):
                pltpu.make_async_copy(buf.at[1 - p, k], buf.at[p, k], sem.at[p]).wait()
            gate = gate_ref[...]
            acc = (ALPHA * h_ref[...] + gate[:, 0:1] * _tiles_to_rows(buf.at[p, 0])
                   + gate[:, 1:2] * _tiles_to_rows(buf.at[p, 1]))
            o_ref[...] = _layer_norm_rows(acc, g_ref[...], b_ref[...])


def _combine_ln(e0, e1, r0, r1, pad_start, rows_tiles, h, gates, g_row, b_row):
    tok = h.shape[0]
    tm = TM_COMBINE
    row = lambda i, *_: (i, 0)
    const = lambda i, *_: (0, 0)
    grid_spec = pltpu.PrefetchScalarGridSpec(
        num_scalar_prefetch=5,
        grid=(tok // tm,),
        in_specs=[
            pl.BlockSpec(memory_space=pl.ANY),
            pl.BlockSpec((tm, D_MODEL), row),
            pl.BlockSpec((tm, LANES), row),
            pl.BlockSpec((1, D_MODEL), const),
            pl.BlockSpec((1, D_MODEL), const),
        ],
        out_specs=pl.BlockSpec((tm, D_MODEL), row),
        scratch_shapes=[pltpu.VMEM((2, TOP_K, tm * ROW_TILE[0], LANES), F32), pltpu.SemaphoreType.DMA((2,))],
    )
    return pl.pallas_call(
        _combine_ln_kernel,
        out_shape=jax.ShapeDtypeStruct((tok, D_MODEL), F32),
        grid_spec=grid_spec,
        compiler_params=_params("arbitrary"),
        name="moe_combine_ln",
    )(e0, e1, r0, r1, pad_start, rows_tiles, h, gates, g_row, b_row)


def _slot_layout(counts, n_blocks):
    padded = (counts + MOE_ROWS - 1) // MOE_ROWS * MOE_ROWS
    pad_end = jnp.cumsum(padded)
    n_used = pad_end[-1:] // MOE_ROWS
    blocks = jnp.arange(n_blocks, dtype=jnp.int32)
    first_row = jnp.minimum(blocks, n_used - 1) * MOE_ROWS
    block_expert = jnp.sum(first_row[:, None] >= pad_end[None, :], axis=1)
    return (pad_end - padded).astype(jnp.int32), block_expert.astype(jnp.int32), n_used.astype(jnp.int32)


def _moe_ln(h, h_tiles, w_router, b_router, w_gate, w_up, w_down, layer, g_row, b_row):
    tok = h.shape[0]
    n_blocks = tok * TOP_K // MOE_ROWS + N_EXPERTS
    ids, gates, counts = _router(h, w_router, b_router)
    e0, e1, r0, r1 = (ids[c] for c in range(4))
    counts = counts[0, MOE_GROUPS:MOE_GROUPS + N_EXPERTS]
    pad_start, block_expert, n_used = _slot_layout(counts, n_blocks)
    xs = _dispatch(h_tiles.reshape((tok,) + ROW_TILE), e0, e1, r0, r1, pad_start, counts, n_used,
                   n_blocks * MOE_ROWS)
    rows = _expert_ffn(xs, block_expert, n_used, w_gate, w_up, w_down, layer)
    return _combine_ln(e0, e1, r0, r1, pad_start, rows, h, gates, g_row, b_row)


def _pad_lanes(v):
    return jnp.pad(v, (0, LANES - v.shape[0])).reshape(1, LANES)


def _even_mixer(h, bsz, seq, w_in, conv_a, conv_w, conv_b, dt_bias, a_log, d_skip, norm_w, w_out, g_row, b_row):
    w = jnp.pad(w_in, ((0, 0), (0, AB_PROJ - w_in.shape[1]))).astype(MXU_DTYPE)
    y_a, z, xbc, dt, acs, acst = _even_front(
        h, w, conv_a, conv_w, conv_b.reshape(1, -1), _pad_lanes(dt_bias), _pad_lanes(-jnp.exp(a_log)), bsz, seq)
    dskip_row = jnp.repeat(d_skip, SSM_HEAD_DIM).reshape(1, -1)
    y_b = _ssd(xbc, dt, acs, acst, z, dskip_row, norm_w.reshape(1, -1), bsz, seq)
    return _outproj_ln([y_a, y_b], h, w_out.astype(MXU_DTYPE), g_row, b_row)


def _odd_mixer(h, bsz, seq, w_in, i_bias, f_bias, hnorm_w, fox_f_bias, w_out, g_row, b_row):
    c = np.cumsum((0, MLSTM_W, MLSTM_W, MLSTM_W, MLSTM_HEADS, MLSTM_HEADS, MLSTM_W, FOX_W, FOX_W, FOX_W, FOX_HEADS))
    part = lambda j: w_in[:, c[j]:c[j + 1]]
    q, k, v, i_pre, f_pre, o_pre, fq, fk, fv, ff = (part(j) for j in range(10))
    gate_cols = jnp.concatenate([i_pre, f_pre, ff, f_pre], axis=1)
    gate_cols = jnp.pad(gate_cols, ((0, 0), (0, LANES - gate_cols.shape[1])))
    spread = lambda m: jnp.pad(m.reshape(-1, FOX_HEADS, FOX_HEAD_DIM),
                               ((0, 0), (0, 0), (0, LANES - FOX_HEAD_DIM))).reshape(-1, FOX_AUG)
    w = jnp.concatenate([q, k, v, o_pre, fq, spread(fk), spread(fv), gate_cols], axis=1).astype(MXU_DTYPE)
    gate_bias = _pad_lanes(jnp.concatenate([i_bias, f_bias, fox_f_bias, f_bias]))
    q, k, v, o, fq, fk_aug, fvt_aug, gates, gates_t = _odd_front(h, w, gate_bias, bsz, seq)
    y_c = _mlstm(q, k, v, o, gates, gates_t, hnorm_w.reshape(1, -1), bsz, seq)
    y_d = _fox(fq, fk_aug, fvt_aug, gates_t, bsz, seq)
    return _outproj_ln([y_c, y_d], h, w_out.astype(MXU_DTYPE), g_row, b_row)


def kernel(x, ab_w_in, ab_conv_a, ab_conv_ssm_w, ab_conv_ssm_b, ab_dt_bias, ab_a_log, ab_d_skip, ab_norm_w, ab_w_out, cd_w_in, cd_i_bias, cd_f_bias, cd_hnorm_w, cd_fox_f_bias, cd_w_out, ln1_g, ln1_b, ln2_g, ln2_b, moe_rg_w, moe_rg_b, moe_re_w, moe_re_b, moe_w_gate, moe_w_up, moe_w_down):
    bsz, seq, d = x.shape
    h = x.reshape(bsz * seq, d)
    stack = lambda w: w.reshape((w.shape[0] * w.shape[1],) + w.shape[2:])
    w_gate, w_up, w_down = stack(moe_w_gate), stack(moe_w_up), stack(moe_w_down)
    for layer in range(DEPTH):
        j = layer // 2
        g1, b1 = ln1_g[layer].reshape(1, -1), ln1_b[layer].reshape(1, -1)
        if layer % 2 == 0:
            h, h_tiles = _even_mixer(h, bsz, seq, ab_w_in[j], ab_conv_a[j], ab_conv_ssm_w[j], ab_conv_ssm_b[j],
                                     ab_dt_bias[j], ab_a_log[j], ab_d_skip[j], ab_norm_w[j], ab_w_out[j], g1, b1)
        else:
            h, h_tiles = _odd_mixer(h, bsz, seq, cd_w_in[j], cd_i_bias[j], cd_f_bias[j], cd_hnorm_w[j],
                                    cd_fox_f_bias[j], cd_w_out[j], g1, b1)
        re_w = jnp.transpose(moe_re_w[layer], (1, 0, 2)).reshape(d, N_EXPERTS)
        w_router = jnp.pad(jnp.concatenate([moe_rg_w[layer], re_w], axis=1),
                           ((0, 0), (0, LANES - MOE_GROUPS - N_EXPERTS)))
        b_router = _pad_lanes(jnp.concatenate([moe_rg_b[layer], moe_re_b[layer].reshape(-1)]))
        h = _moe_ln(h, h_tiles, w_router, b_router, w_gate, w_up, w_down, layer,
                    ln2_g[layer].reshape(1, -1), ln2_b[layer].reshape(1, -1))
    return h.reshape(bsz, seq, d)
```

```python
import functools

import numpy as np
import jax
import jax.numpy as jnp
from jax import lax
from jax.experimental import pallas as pl
from jax.experimental.pallas import tpu as pltpu

F32 = jnp.float32
MXU_DTYPE = jnp.bfloat16
HIGHEST = lax.Precision.HIGHEST

D_MODEL = 1024
DEPTH = 4
ALPHA = (2 * DEPTH) ** 0.25
LN_EPS = 1e-5
CONV_DIM = D_MODEL // 2
CONV_WIDTH = 3
SSM_D_INNER = D_MODEL
SSM_HEAD_DIM = 64
SSM_HEADS = SSM_D_INNER // SSM_HEAD_DIM
SSM_GROUPS = 4
SSM_STATE = 64
SSM_CONV = 4
SSM_BC = SSM_GROUPS * SSM_STATE
SSM_CONV_DIM = SSM_D_INNER + 2 * SSM_BC
MLSTM_HEADS = 4
MLSTM_HEAD_DIM = D_MODEL // 8
MLSTM_W = MLSTM_HEADS * MLSTM_HEAD_DIM
FOX_HEADS = 8
FOX_HEAD_DIM = D_MODEL // 16
FOX_W = FOX_HEADS * FOX_HEAD_DIM
MOE_GROUPS = 4
EXPERTS_PER_GROUP = 8
N_EXPERTS = MOE_GROUPS * EXPERTS_PER_GROUP
TOP_K = 2
D_EXPERT = D_MODEL // 2

LANES = 128
SUBLANES = 8
VMEM_LIMIT_BYTES = 56 * 1024 * 1024

CHUNK = 128
SEQ_PAIR = 2
MLSTM_SEQS = 1
MLSTM_CHUNK = 256
TM_FRONT = 512
TM_EVEN_FRONT = 256
TM_OUT = 1024
TM_ROUTER = 512
TQ_FOX = 256
MOE_ROWS = 512
TM_COMBINE = 256
TM_DISPATCH = 512

AB_PROJ = 4224
FOX_AUG = FOX_HEADS * LANES
CD_FK = 5 * 512
CD_FV = CD_FK + FOX_AUG
CD_GATES = CD_FV + FOX_W
CD_PROJ = CD_GATES + LANES
LOG2E = 1.4426950408889634
TK_FOX = 128
FOX_ACC_ROWS = FOX_HEAD_DIM + SUBLANES
FOX_VROWS = LANES
G_I, G_F, G_FOX, G_BCUM = 0, 4, 8, 16


def _params(*sem):
    return pltpu.CompilerParams(dimension_semantics=sem, vmem_limit_bytes=VMEM_LIMIT_BYTES)


def _softplus(x):
    return jnp.maximum(x, 0.0) + jnp.log(1.0 + jnp.exp(-jnp.abs(x)))


def _sigmoid(x):
    return 1.0 / (1.0 + jnp.exp(-x))


def _layer_norm_rows(v, g, b):
    mu = jnp.mean(v, axis=-1, keepdims=True)
    c = v - mu
    var = jnp.mean(c * c, axis=-1, keepdims=True)
    return c * lax.rsqrt(var + LN_EPS) * g + b


def _tril(n, block):
    i = np.arange(n)
    m = (i[:, None] >= i[None, :]) & (i[:, None] // block == i[None, :] // block)
    return jnp.asarray(m.astype(np.float32), MXU_DTYPE)


def _full(shape):
    return pl.BlockSpec(shape, lambda *_: (0,) * len(shape), pipeline_mode=pl.Buffered(1))


ROW_TILE = (D_MODEL // LANES, LANES)


def _tiles_to_rows(ref):
    n = ref.shape[0] // ROW_TILE[0]
    return jnp.concatenate([ref[pl.ds(s, n, stride=ROW_TILE[0]), :] for s in range(ROW_TILE[0])], axis=1)


def _rows_to_tiles(ref, val):
    n = val.shape[0]
    for s in range(ROW_TILE[0]):
        ref[pl.ds(s, n, stride=ROW_TILE[0]), :] = val[:, s * LANES:(s + 1) * LANES]


def _split3(x):
    narrow = lambda v: v.astype(MXU_DTYPE).astype(F32)
    x1 = narrow(x)
    x2 = narrow(x - x1)
    return x1, x2, narrow(x - x1 - x2)


def _cumsum_rows(tril, x):
    parts = jnp.dot(tril, jnp.concatenate(_split3(x), axis=1).astype(MXU_DTYPE), preferred_element_type=F32)
    return parts[:, 0:LANES] + parts[:, LANES:2 * LANES] + parts[:, 2 * LANES:3 * LANES]


def _even_front_kernel(h_ref, w_ref, ca_ref, cw_ref, cb_ref, dtb_ref, aneg_ref, tril_ref,
                       ya_ref, z_ref, xbc_ref, dt_ref, acs_ref, acst_ref,
                       proj_buf, ua_ext, xbc_ext, *, tiles_per_seq):
    tm = h_ref.shape[0]
    i = pl.program_id(0)
    cur = i % 2

    @pl.when(i == 0)
    def _():
        proj_buf[1] = jnp.zeros(proj_buf.shape[1:], F32)

    @pl.when((i == 0) | ((i - 1) % tiles_per_seq == 0))
    def _():
        ua_ext[0:SUBLANES, :] = jnp.zeros((SUBLANES, CONV_DIM), F32)
        xbc_ext[0:SUBLANES, :] = jnp.zeros((SUBLANES, SSM_CONV_DIM), F32)

    proj_buf[cur] = jnp.dot(h_ref[...].astype(MXU_DTYPE), w_ref[...], preferred_element_type=F32)

    proj = proj_buf.at[1 - cur]
    z0 = 3 * CONV_DIM
    x0 = z0 + SSM_D_INNER
    d0 = x0 + SSM_CONV_DIM
    z_ref[...] = proj[:, z0:x0]

    ua_ext[SUBLANES:SUBLANES + tm, :] = proj[:, CONV_DIM:2 * CONV_DIM] * proj[:, 2 * CONV_DIM:3 * CONV_DIM]
    conv = ca_ref[0:1, :] * ua_ext[pl.ds(SUBLANES - 2, tm), :]
    for k in range(1, CONV_WIDTH):
        conv = conv + ca_ref[k:k + 1, :] * ua_ext[pl.ds(SUBLANES - (CONV_WIDTH - 1) + k, tm), :]
    ya_ref[...] = proj[:, 0:CONV_DIM] * conv
    ua_ext[0:SUBLANES, :] = ua_ext[tm:tm + SUBLANES, :]

    xbc_ext[SUBLANES:SUBLANES + tm, :] = proj[:, x0:d0]
    conv = cb_ref[...] + cw_ref[0:1, :] * xbc_ext[pl.ds(SUBLANES - (SSM_CONV - 1), tm), :]
    for k in range(1, SSM_CONV):
        conv = conv + cw_ref[k:k + 1, :] * xbc_ext[pl.ds(SUBLANES - (SSM_CONV - 1) + k, tm), :]
    xbc_ref[...] = conv * _sigmoid(conv)
    xbc_ext[0:SUBLANES, :] = xbc_ext[tm:tm + SUBLANES, :]

    dt = _softplus(proj[:, d0:d0 + LANES] + dtb_ref[...])
    a = dt * aneg_ref[...]
    acs = _cumsum_rows(tril_ref[...], a)
    dt_ref[...] = dt
    acs_ref[...] = acs
    acst_ref[...] = acs.T


def _even_front(h, w_in, conv_a, conv_w, conv_b, dt_bias_row, aneg_row, bsz, seq):
    tok = bsz * seq
    tm = TM_EVEN_FRONT
    n_tiles = tok // tm
    row = lambda i: (jnp.maximum(i - 1, 0), 0)
    out_shapes = (
        jax.ShapeDtypeStruct((tok, CONV_DIM), F32),
        jax.ShapeDtypeStruct((tok, SSM_D_INNER), F32),
        jax.ShapeDtypeStruct((tok, SSM_CONV_DIM), F32),
        jax.ShapeDtypeStruct((tok, LANES), F32),
        jax.ShapeDtypeStruct((tok, LANES), F32),
        jax.ShapeDtypeStruct((LANES, tok), F32),
    )
    return pl.pallas_call(
        functools.partial(_even_front_kernel, tiles_per_seq=seq // tm),
        out_shape=out_shapes,
        grid=(n_tiles + 1,),
        in_specs=[
            pl.BlockSpec((tm, D_MODEL), lambda i: (jnp.minimum(i, n_tiles - 1), 0)),
            _full((D_MODEL, AB_PROJ)),
            _full((CONV_WIDTH, CONV_DIM)),
            _full((SSM_CONV, SSM_CONV_DIM)),
            _full((1, SSM_CONV_DIM)),
            _full((1, LANES)),
            _full((1, LANES)),
            _full((tm, tm)),
        ],
        out_specs=(
            pl.BlockSpec((tm, CONV_DIM), row),
            pl.BlockSpec((tm, SSM_D_INNER), row),
            pl.BlockSpec((tm, SSM_CONV_DIM), row),
            pl.BlockSpec((tm, LANES), row),
            pl.BlockSpec((tm, LANES), row),
            pl.BlockSpec((LANES, tm), lambda i: (0, jnp.maximum(i - 1, 0))),
        ),
        scratch_shapes=[
            pltpu.VMEM((2, tm, AB_PROJ), F32),
            pltpu.VMEM((tm + SUBLANES, CONV_DIM), F32),
            pltpu.VMEM((tm + SUBLANES, SSM_CONV_DIM), F32),
        ],
        compiler_params=_params("arbitrary"),
        name="even_front",
    )(h, w_in, conv_a, conv_w, conv_b, dt_bias_row, aneg_row, _tril(tm, CHUNK))


def _bcast_heads(arr, n_heads, width):
    per = LANES // width
    length = arr.shape[0]
    lane = lax.broadcasted_iota(jnp.int32, (length, LANES), 1)
    outs = []
    for j in range(n_heads // per):
        v = jnp.broadcast_to(arr[:, j * per:j * per + 1], (length, LANES))
        for r in range(1, per):
            v = jnp.where(lane >= r * width, jnp.broadcast_to(arr[:, j * per + r:j * per + r + 1], (length, LANES)), v)
        outs.append(v)
    return jnp.concatenate(outs, axis=1)


def _expand_heads(arr, e3_ref):
    return jnp.dot(jnp.concatenate(_split3(arr), axis=1).astype(MXU_DTYPE), e3_ref[...],
                   preferred_element_type=F32)


def _ssd_kernel(*refs):
    nb = SEQ_PAIR
    xbc_ref, dt_ref, acs_ref, z_ref = refs[:4]
    acst_refs = refs[4:4 + nb]
    dskip_ref, nw_ref, e3_ref, y_ref, state = refs[4 + nb:]
    L = CHUNK
    P = SSM_HEAD_DIM
    R = SSM_HEADS // SSM_GROUPS
    GW = R * P

    @pl.when(pl.program_id(1) == 0)
    def _():
        state[...] = jnp.zeros(state.shape, F32)

    row = lax.broadcasted_iota(jnp.int32, (L, L), 0)
    col = lax.broadcasted_iota(jnp.int32, (L, L), 1)
    causal = row >= col
    lane_g = lax.broadcasted_iota(jnp.int32, (L, GW), 1)

    per_row = []
    for j in range(nb):
        acs = acs_ref[j]
        per_row += [dt_ref[j], jnp.exp(acs[L - 1:L, :] - acs), jnp.exp(acs)]
    expanded = _expand_heads(jnp.concatenate(per_row, axis=0), e3_ref)

    st_old = [state[c] for c in range(nb * SSM_GROUPS)]
    st_new = []
    for j in range(nb):
        xs = xbc_ref[j, :, 0:SSM_D_INNER]
        bm = xbc_ref[j, :, SSM_D_INNER:SSM_D_INNER + SSM_BC]
        cm = xbc_ref[j, :, SSM_D_INNER + SSM_BC:SSM_CONV_DIM]
        acs = acs_ref[j]
        acst = acst_refs[j][...]
        a_last = acs[L - 1:L, :]
        dtx = expanded[(3 * j) * L:(3 * j + 1) * L, :]
        decx = expanded[(3 * j + 1) * L:(3 * j + 2) * L, :]
        expx = expanded[(3 * j + 2) * L:(3 * j + 3) * L, :]
        xdt = xs * dtx
        xdec = (xdt * decx).astype(MXU_DTYPE)
        xdt_m = xdt.astype(MXU_DTYPE)
        chunk_decay = jnp.exp(jnp.broadcast_to(a_last, (SUBLANES, LANES)))
        cdx = _bcast_heads(chunk_decay, SSM_HEADS, P)[0:1, :]

        bm_t = bm.T.astype(MXU_DTYPE)
        cm_m = cm.astype(MXU_DTYPE)
        bm_m = bm.astype(MXU_DTYPE)

        ys = []
        for g in range(SSM_GROUPS):
            n0 = g * SSM_STATE
            c_g = cm_m[:, n0:n0 + SSM_STATE]
            cb = lax.dot_general(c_g, bm_m[:, n0:n0 + SSM_STATE], (((1,), (1,)), ((), ())),
                                 preferred_element_type=F32)
            ms = []
            for r in range(R):
                hd = g * R + r
                seg = jnp.exp(jnp.where(causal, acs[:, hd:hd + 1] - acst[hd:hd + 1, :], -jnp.inf))
                ms.append((cb * seg).astype(MXU_DTYPE))
            big = jnp.dot(jnp.concatenate(ms, axis=0), xdt_m[:, g * GW:(g + 1) * GW],
                          preferred_element_type=F32)
            y_diag = big[0:L, :]
            for r in range(1, R):
                y_diag = jnp.where(lane_g >= r * P, big[r * L:(r + 1) * L, :], y_diag)
            st = st_old[j * SSM_GROUPS + g]
            y_off = jnp.dot(c_g, st.astype(MXU_DTYPE), preferred_element_type=F32)
            new = jnp.dot(bm_t[n0:n0 + SSM_STATE, :], xdec[:, g * GW:(g + 1) * GW],
                          preferred_element_type=F32)
            st_new.append(st * cdx[:, g * GW:(g + 1) * GW] + new)
            ys.append(y_diag + y_off * expx[:, g * GW:(g + 1) * GW])
        y = jnp.concatenate(ys, axis=1) + dskip_ref[...] * xs
        z = z_ref[j]
        u = y * (z * _sigmoid(z))
        y_ref[j] = u * lax.rsqrt(jnp.mean(u * u, axis=-1, keepdims=True) + LN_EPS) * nw_ref[...]

    for c in range(nb * SSM_GROUPS):
        state[c] = st_new[c]


def _ssd(xbc, dt, acs, acst, z, dskip_row, normw_row, bsz, seq):
    nb = SEQ_PAIR
    nc = seq // CHUNK
    per_seq = lambda a: a.reshape(bsz, seq, a.shape[-1])
    blk = lambda width: pl.BlockSpec((nb, CHUNK, width), lambda g, c: (g, c, 0))
    acst_specs = [pl.BlockSpec((LANES, CHUNK), lambda g, c, j=j: (0, (g * nb + j) * nc + c)) for j in range(nb)]
    expand = np.zeros((LANES, SSM_D_INNER), np.float32)
    for hd in range(SSM_HEADS):
        expand[hd, hd * SSM_HEAD_DIM:(hd + 1) * SSM_HEAD_DIM] = 1.0
    e3 = jnp.asarray(np.concatenate([expand] * 3, axis=0), MXU_DTYPE)
    y = pl.pallas_call(
        _ssd_kernel,
        out_shape=jax.ShapeDtypeStruct((bsz, seq, SSM_D_INNER), F32),
        grid=(bsz // nb, nc),
        in_specs=[blk(SSM_CONV_DIM), blk(LANES), blk(LANES), blk(SSM_D_INNER)] + acst_specs + [
            _full((1, SSM_D_INNER)),
            _full((1, SSM_D_INNER)),
            _full((3 * LANES, SSM_D_INNER)),
        ],
        out_specs=blk(SSM_D_INNER),
        scratch_shapes=[pltpu.VMEM((nb * SSM_GROUPS, SSM_STATE, SSM_D_INNER // SSM_GROUPS), F32)],
        compiler_params=_params("arbitrary", "arbitrary"),
        name="ssd_scan",
    )(per_seq(xbc), per_seq(dt), per_seq(acs), per_seq(z), *([acst] * nb), dskip_row, normw_row, e3)
    return y.reshape(bsz * seq, SSM_D_INNER)


def _outproj_ln_kernel(*refs, widths):
    n = len(widths)
    parts = refs[:n]
    h_ref, w_ref, g_ref, b_ref, o_ref, ot_ref = refs[n:]
    acc = ALPHA * h_ref[...]
    off = 0
    for p, wd in zip(parts, widths):
        acc = acc + jnp.dot(p[...].astype(MXU_DTYPE), w_ref[off:off + wd, :], preferred_element_type=F32)
        off += wd
    out = _layer_norm_rows(acc, g_ref[...], b_ref[...])
    o_ref[...] = out
    _rows_to_tiles(ot_ref, out)


def _outproj_ln(parts, h, w_out, g_row, b_row):
    tok = h.shape[0]
    tm = TM_OUT
    widths = tuple(p.shape[1] for p in parts)
    row = lambda i: (i, 0)
    return pl.pallas_call(
        functools.partial(_outproj_ln_kernel, widths=widths),
        out_shape=(jax.ShapeDtypeStruct((tok, D_MODEL), F32),
                   jax.ShapeDtypeStruct((tok * ROW_TILE[0], LANES), F32)),
        grid=(tok // tm,),
        in_specs=[pl.BlockSpec((tm, wd), row) for wd in widths] + [
            pl.BlockSpec((tm, D_MODEL), row),
            _full((sum(widths), D_MODEL)),
            _full((1, D_MODEL)),
            _full((1, D_MODEL)),
        ],
        out_specs=(pl.BlockSpec((tm, D_MODEL), row), pl.BlockSpec((tm * ROW_TILE[0], LANES), row)),
        compiler_params=_params("arbitrary"),
        name="outproj_ln",
    )(*parts, h, w_out, g_row, b_row)


def _odd_front_kernel(h_ref, w_ref, gb_ref, tril_ref,
                      q_ref, k_ref, v_ref, o_ref, fq_ref, fk_ref, fvt_ref, g_ref, gt_ref, carry):
    tm = h_ref.shape[0]
    proj = jnp.dot(h_ref[...].astype(MXU_DTYPE), w_ref[...], preferred_element_type=F32)
    q_ref[...] = proj[:, 0:512]
    k_ref[...] = proj[:, 512:1024] * (MLSTM_HEAD_DIM ** -0.5)
    v_ref[...] = proj[:, 1024:1536]
    o_ref[...] = _sigmoid(proj[:, 1536:2048])
    fq_ref[...] = proj[:, 2048:2560] * (FOX_HEAD_DIM ** -0.5 * LOG2E)

    @pl.when(pl.program_id(1) == 0)
    def _():
        carry[...] = jnp.zeros(carry.shape, F32)

    raw = proj[:, CD_GATES:CD_GATES + LANES] + gb_ref[...]
    lane = lax.broadcasted_iota(jnp.int32, (tm, LANES), 1)
    g = jnp.where(lane < G_F, raw, -_softplus(-raw))
    prev = carry[0:1, :]
    glob = _cumsum_rows(tril_ref[...], g) + prev
    before = []
    for c in range(tm // MLSTM_CHUNK):
        before.append(jnp.broadcast_to(prev, (MLSTM_CHUNK, LANES)))
        prev = glob[(c + 1) * MLSTM_CHUNK - 1:(c + 1) * MLSTM_CHUNK, :]
    carry[...] = jnp.broadcast_to(prev, carry.shape)
    local = glob - jnp.concatenate(before, axis=0)
    out = jnp.where((lane >= G_FOX) & (lane < G_BCUM), glob, jnp.where(lane >= G_BCUM, local, g))
    g_ref[...] = out
    gt_ref[...] = out.T

    is_bias = (lane >= FOX_HEAD_DIM) & (lane < FOX_HEAD_DIM + 3)
    for hd in range(FOX_HEADS):
        c1, c2, c3 = _split3(out[:, G_FOX + hd:G_FOX + hd + 1] * (-LOG2E))
        bias = jnp.where(lane == FOX_HEAD_DIM, c1, jnp.where(lane == FOX_HEAD_DIM + 1, c2, c3))
        k_h = proj[:, CD_FK + hd * LANES:CD_FK + (hd + 1) * LANES]
        fk_ref[:, hd * LANES:(hd + 1) * LANES] = jnp.where(is_bias, bias, k_h).astype(MXU_DTYPE)
    v_t = proj[:, CD_FV:CD_FV + FOX_W].T
    extra = jnp.where(lax.broadcasted_iota(jnp.int32, (FOX_VROWS - FOX_HEAD_DIM, tm), 0) == 0, 1.0, 0.0)
    fvt_ref[...] = jnp.concatenate(
        [blk for hd in range(FOX_HEADS) for blk in (v_t[hd * FOX_HEAD_DIM:(hd + 1) * FOX_HEAD_DIM, :], extra)],
        axis=0).astype(MXU_DTYPE)


def _odd_front(h, w_in, gate_bias_row, bsz, seq):
    tok = bsz * seq
    tm = TM_FRONT
    ns = seq // tm
    row = lambda b, s: (b * ns + s, 0)
    col = lambda b, s: (0, b * ns + s)
    wide = jax.ShapeDtypeStruct((tok, 512), F32)
    return pl.pallas_call(
        _odd_front_kernel,
        out_shape=(wide,) * 5 + (jax.ShapeDtypeStruct((tok, FOX_AUG), MXU_DTYPE),
                                 jax.ShapeDtypeStruct((FOX_HEADS * FOX_VROWS, tok), MXU_DTYPE),
                                 jax.ShapeDtypeStruct((tok, LANES), F32),
                                 jax.ShapeDtypeStruct((LANES, tok), F32)),
        grid=(bsz, ns),
        in_specs=[
            pl.BlockSpec((tm, D_MODEL), row),
            _full((D_MODEL, CD_PROJ)),
            _full((1, LANES)),
            _full((tm, tm)),
        ],
        out_specs=(pl.BlockSpec((tm, 512), row),) * 5 + (
            pl.BlockSpec((tm, FOX_AUG), row),
            pl.BlockSpec((FOX_HEADS * FOX_VROWS, tm), col),
            pl.BlockSpec((tm, LANES), row),
            pl.BlockSpec((LANES, tm), col),
        ),
        scratch_shapes=[pltpu.VMEM((SUBLANES, LANES), F32)],
        compiler_params=_params("arbitrary", "arbitrary"),
        name="odd_front",
    )(h, w_in, gate_bias_row, _tril(tm, tm))


def _mlstm_kernel(*refs):
    nb = MLSTM_SEQS
    q_ref, k_ref, v_ref, o_ref, g_ref = refs[:5]
    gt_refs = refs[5:5 + nb]
    nw_ref, y_ref, c_state, m_state = refs[5 + nb:]
    L = MLSTM_CHUNK
    DH = MLSTM_HEAD_DIM

    @pl.when(pl.program_id(1) == 0)
    def _():
        c_state[...] = jnp.zeros(c_state.shape, F32)
        m_state[...] = jnp.zeros(m_state.shape, F32)

    row = lax.broadcasted_iota(jnp.int32, (L, L), 0)
    col = lax.broadcasted_iota(jnp.int32, (L, L), 1)
    causal = row >= col
    ones_col = jnp.where(lax.broadcasted_iota(jnp.int32, (L, DH), 1) == 0, 1.0, 0.0)

    for j in range(nb):
        gates = g_ref[j]
        gates_t = gt_refs[j][...]
        for hd in range(MLSTM_HEADS):
            st = j * MLSTM_HEADS + hd
            sl = slice(hd * DH, (hd + 1) * DH)
            q = q_ref[j, :, sl].astype(MXU_DTYPE)
            k = k_ref[j, :, sl]
            v_ext = jnp.concatenate([v_ref[j, :, sl], ones_col], axis=1).astype(MXU_DTYPE)
            b_col = gates[:, G_BCUM + hd:G_BCUM + hd + 1]
            i_col = gates[:, G_I + hd:G_I + hd + 1]
            b_row = gates_t[G_BCUM + hd:G_BCUM + hd + 1, :]
            i_row = gates_t[G_I + hd:G_I + hd + 1, :]
            m_prev = m_state[st:st + 1, 0:1]
            c_ext = c_state[st]

            d_mat = jnp.where(causal, b_col - b_row + i_row, -jnp.inf)
            inter = b_col + m_prev
            m_t = jnp.maximum(jnp.max(d_mat, axis=-1, keepdims=True), inter)
            s_qk = lax.dot_general(q, k.astype(MXU_DTYPE), (((1,), (1,)), ((), ())), preferred_element_type=F32)
            w_qk = s_qk * jnp.exp(d_mat - m_t)
            s_inter = jnp.exp(inter - m_t)
            num_ext = (jnp.dot(w_qk.astype(MXU_DTYPE), v_ext, preferred_element_type=F32)
                       + s_inter * jnp.dot(q, c_ext.astype(MXU_DTYPE), preferred_element_type=F32))
            den = num_ext[:, DH:DH + 1]
            hval = num_ext[:, 0:DH] / jnp.maximum(jnp.abs(den), jnp.exp(-m_t))

            b_last = b_col[L - 1:L, :]
            g_log = b_last - b_col + i_col
            m_new = jnp.maximum(b_last + m_prev, jnp.max(g_log, axis=0, keepdims=True))
            w_k = jnp.exp(g_log - m_new)
            decay = jnp.exp(b_last + m_prev - m_new)
            kw_t = (k * w_k).T.astype(MXU_DTYPE)
            c_state[st] = decay * c_ext + jnp.dot(kw_t, v_ext, preferred_element_type=F32)
            m_state[st:st + 1, :] = jnp.broadcast_to(m_new, (1, LANES))

            mu = jnp.mean(hval, axis=-1, keepdims=True)
            cen = hval - mu
            var = jnp.mean(cen * cen, axis=-1, keepdims=True)
            y_ref[j, :, sl] = o_ref[j, :, sl] * (cen * lax.rsqrt(var + LN_EPS) * nw_ref[:, sl])


def _mlstm(q, k, v, o, gates, gates_t, hnorm_row, bsz, seq):
    nb = MLSTM_SEQS
    nc = seq // MLSTM_CHUNK
    per_seq = lambda a: a.reshape(bsz, seq, a.shape[-1])
    blk = lambda width: pl.BlockSpec((nb, MLSTM_CHUNK, width), lambda g, c: (g, c, 0))
    gt_specs = [pl.BlockSpec((LANES, MLSTM_CHUNK), lambda g, c, j=j: (0, (g * nb + j) * nc + c))
                for j in range(nb)]
    y = pl.pallas_call(
        _mlstm_kernel,
        out_shape=jax.ShapeDtypeStruct((bsz, seq, MLSTM_W), F32),
        grid=(bsz // nb, nc),
        in_specs=[blk(MLSTM_W)] * 4 + [blk(LANES)] + gt_specs + [_full((1, MLSTM_W))],
        out_specs=blk(MLSTM_W),
        scratch_shapes=[pltpu.VMEM((nb * MLSTM_HEADS, MLSTM_HEAD_DIM, 2 * MLSTM_HEAD_DIM), F32),
                        pltpu.VMEM((nb * MLSTM_HEADS, LANES), F32)],
        compiler_params=_params("arbitrary", "arbitrary"),
        name="mlstm_scan",
    )(per_seq(q), per_seq(k), per_seq(v), per_seq(o), per_seq(gates), *([gates_t] * nb), hnorm_row)
    return y.reshape(bsz * seq, MLSTM_W)


def _fox_kernel(q_ref, k_ref, vt_ref, gt_ref, y_ref, *scratch):
    acc_refs = scratch[:FOX_HEADS]
    qa_ref = scratch[FOX_HEADS]
    tq = q_ref.shape[0]
    tk = TK_FOX
    dh = FOX_HEAD_DIM
    qi = pl.program_id(1)
    q_t = q_ref[...].T
    bias_rows = jnp.where(lax.broadcasted_iota(jnp.int32, (LANES - dh, tq), 0) < 3, 1.0, 0.0)
    for hd in range(FOX_HEADS):
        qa_ref[hd] = jnp.concatenate([q_t[hd * dh:(hd + 1) * dh, :], bias_rows], axis=0).astype(MXU_DTYPE)
        acc_refs[hd][...] = jnp.zeros(acc_refs[hd].shape, F32)
    cq = gt_ref[G_FOX:G_FOX + FOX_HEADS, :] * LOG2E
    key_pos = lax.broadcasted_iota(jnp.int32, (tk, tq), 0)
    qry_pos = lax.broadcasted_iota(jnp.int32, (tk, tq), 1) + qi * tq
    n_full = qi * (tq // tk)

    def block(j, ms, masked):
        k0 = pl.multiple_of(j * tk, tk)
        out = []
        for hd in range(FOX_HEADS):
            hs = slice(hd * LANES, (hd + 1) * LANES)
            s = jnp.dot(k_ref[pl.ds(k0, tk), hs], qa_ref[hd], preferred_element_type=F32)
            if masked:
                s = jnp.where(key_pos + k0 <= qry_pos, s, -jnp.inf)
            cq_h = cq[hd:hd + 1, :]
            m_new = jnp.maximum(ms[hd], jnp.max(s, axis=0, keepdims=True) + cq_h)
            p = jnp.exp2(s - (m_new - cq_h))
            pv = jnp.dot(vt_ref[hd * FOX_VROWS:(hd + 1) * FOX_VROWS, pl.ds(k0, tk)], p.astype(MXU_DTYPE),
                         preferred_element_type=F32)
            acc_refs[hd][...] = jnp.exp2(ms[hd] - m_new) * acc_refs[hd][...] + pv[0:FOX_ACC_ROWS, :]
            out.append(m_new)
        return tuple(out)

    ms = tuple(jnp.full((1, tq), -jnp.inf, F32) for _ in range(FOX_HEADS))
    ms = lax.fori_loop(0, n_full, functools.partial(block, masked=False), ms)
    for d in range(tq // tk):
        ms = block(n_full + d, ms, masked=True)
    outs = []
    for hd in range(FOX_HEADS):
        acc = acc_refs[hd][...]
        outs.append(acc[0:dh, :] / acc[dh:dh + 1, :])
    y_ref[...] = jnp.concatenate(outs, axis=0).T


def _fox(fq, fk_aug, fvt_aug, gates_t, bsz, seq):
    tok = bsz * seq
    tq = TQ_FOX
    nq = seq // tq
    return pl.pallas_call(
        _fox_kernel,
        out_shape=jax.ShapeDtypeStruct((tok, FOX_W), F32),
        grid=(bsz, nq),
        in_specs=[
            pl.BlockSpec((tq, FOX_W), lambda b, i: (b * nq + i, 0)),
            pl.BlockSpec((seq, FOX_AUG), lambda b, i: (b, 0)),
            pl.BlockSpec((FOX_HEADS * FOX_VROWS, seq), lambda b, i: (0, b)),
            pl.BlockSpec((LANES, tq), lambda b, i: (0, b * nq + i)),
        ],
        out_specs=pl.BlockSpec((tq, FOX_W), lambda b, i: (b * nq + i, 0)),
        scratch_shapes=[pltpu.VMEM((FOX_ACC_ROWS, tq), F32)] * FOX_HEADS
                       + [pltpu.VMEM((FOX_HEADS, LANES, tq), MXU_DTYPE)],
        compiler_params=_params("arbitrary", "arbitrary"),
        name="fox_attention",
    )(fq, fk_aug, fvt_aug, gates_t)


def _router_kernel(h_ref, w_ref, b_ref, stril_ref, id_ref, gate_ref, cnt_ref, carry):
    tm = h_ref.shape[0]

    @pl.when(pl.program_id(0) == 0)
    def _():
        carry[...] = jnp.zeros(carry.shape, F32)

    h = h_ref[...]
    h_hi = h.astype(MXU_DTYPE)
    h_lo = (h - h_hi.astype(F32)).astype(MXU_DTYPE)
    both = jnp.dot(h_hi, w_ref[...], preferred_element_type=F32)
    logits = (both[:, 0:LANES] + both[:, LANES:2 * LANES]
              + jnp.dot(h_lo, w_ref[:, 0:LANES], preferred_element_type=F32) + b_ref[...])
    lane = lax.broadcasted_iota(jnp.int32, (tm, LANES), 1).astype(F32)
    neg = -jnp.inf
    first = lambda hit: jnp.min(jnp.where(hit, lane, float(LANES)), axis=-1, keepdims=True)
    gl = jnp.where(lane < MOE_GROUPS, logits, neg)
    g_max = jnp.max(gl, axis=-1, keepdims=True)
    g_idx = first(gl == g_max)
    p_group = 1.0 / jnp.sum(jnp.exp(gl - g_max), axis=-1, keepdims=True)
    e_lo = MOE_GROUPS + g_idx * EXPERTS_PER_GROUP
    el = jnp.where((lane >= e_lo) & (lane < e_lo + EXPERTS_PER_GROUP), logits, neg)
    v1 = jnp.max(el, axis=-1, keepdims=True)
    i1 = first(el == v1)
    el2 = jnp.where(lane == i1, neg, el)
    v2 = jnp.max(el2, axis=-1, keepdims=True)
    i2 = first(el2 == v2)
    t = jnp.exp(v2 - v1)
    w1 = 1.0 / (1.0 + t)
    gate_ref[...] = jnp.where(lane == 0.0, p_group * w1, jnp.where(lane == 1.0, p_group * (t * w1), 0.0))

    hit1 = lane == i1
    hit2 = lane == i2
    sent = jnp.where(hit1 | hit2, 1.0, 0.0)
    before = jnp.dot(stril_ref[...], sent.astype(jnp.bfloat16), preferred_element_type=F32) + carry[0:1, :]
    r1 = jnp.sum(jnp.where(hit1, before, 0.0), axis=-1, keepdims=True)
    r2 = jnp.sum(jnp.where(hit2, before, 0.0), axis=-1, keepdims=True)
    total = before[tm - 1:tm, :] + sent[tm - 1:tm, :]
    carry[...] = jnp.broadcast_to(total, carry.shape)
    cnt_ref[...] = jnp.broadcast_to(total, cnt_ref.shape).astype(jnp.int32)
    ids = jnp.where(lane == 0.0, i1 - MOE_GROUPS, jnp.where(lane == 1.0, i2 - MOE_GROUPS,
                    jnp.where(lane == 2.0, r1, jnp.where(lane == 3.0, r2, 0.0))))
    id_ref[...] = ids.T[0:SUBLANES, :].astype(jnp.int32)


def _router(h, w_router, b_router):
    tok = h.shape[0]
    tm = TM_ROUTER
    row = lambda i: (i, 0)
    idx = np.arange(tm)
    stril = jnp.asarray((idx[:, None] > idx[None, :]).astype(np.float32), jnp.bfloat16)
    w_hi = w_router.astype(MXU_DTYPE)
    w_hi_lo = (w_hi, (w_router - w_hi.astype(F32)).astype(MXU_DTYPE))
    return pl.pallas_call(
        _router_kernel,
        out_shape=(jax.ShapeDtypeStruct((SUBLANES, tok), jnp.int32), jax.ShapeDtypeStruct((tok, LANES), F32),
                   jax.ShapeDtypeStruct((SUBLANES, LANES), jnp.int32)),
        grid=(tok // tm,),
        in_specs=[pl.BlockSpec((tm, D_MODEL), row), _full((D_MODEL, 2 * LANES)), _full((1, LANES)),
                  _full((tm, tm))],
        out_specs=(pl.BlockSpec((SUBLANES, tm), lambda i: (0, i)), pl.BlockSpec((tm, LANES), row),
                   _full((SUBLANES, LANES))),
        scratch_shapes=[pltpu.VMEM((SUBLANES, LANES), F32)],
        compiler_params=_params("arbitrary"),
        name="moe_router",
    )(h, jnp.concatenate([w_hi, w_hi_lo[1]], axis=1), b_router, stril)


def _pad_pieces(n):
    return tuple(1 << b for b in reversed(range((n - 1).bit_length())))


def _dispatch_kernel(e0_ref, e1_ref, r0_ref, r1_ref, start_ref, cnt_ref, nused_ref, h_ref, xs_hbm, zeros, sem,
                     pad_sem):
    tm = h_ref.shape[0]
    i = pl.program_id(0)
    base = i * tm

    def issue(r, c):
        t = base + r
        pltpu.make_async_copy(h_ref.at[r], xs_hbm.at[start_ref[e0_ref[t]] + r0_ref[t]], sem).start(priority=0)
        pltpu.make_async_copy(h_ref.at[r], xs_hbm.at[start_ref[e1_ref[t]] + r1_ref[t]], sem).start(priority=1)
        return c

    lax.fori_loop(0, tm, issue, 0, unroll=8)

    def pad_copies(e, fn):
        cnt = cnt_ref[e]
        n_pad = (MOE_ROWS - cnt % MOE_ROWS) % MOE_ROWS
        first = start_ref[e] + cnt
        for piece in _pad_pieces(MOE_ROWS):
            @pl.when((n_pad & piece) != 0)
            def _(piece=piece):
                off = first + (n_pad & ~(2 * piece - 1))
                fn(pltpu.make_async_copy(zeros.at[pl.ds(0, piece)], xs_hbm.at[pl.ds(off, piece)], pad_sem))

    def tail_copies(blk, fn):
        for part in range(MOE_ROWS // zeros.shape[0]):
            off = blk * MOE_ROWS + part * zeros.shape[0]
            fn(pltpu.make_async_copy(zeros, xs_hbm.at[pl.ds(off, zeros.shape[0])], pad_sem))

    @pl.when(i == 0)
    def _():
        zeros[...] = jnp.zeros(zeros.shape, F32)
        n_blocks = xs_hbm.shape[0] // MOE_ROWS

        def start(e, c):
            pad_copies(e, lambda cp: cp.start())
            return c

        def wait(e, c):
            pad_copies(e, lambda cp: cp.wait())
            return c

        def tail_start(blk, c):
            tail_copies(blk, lambda cp: cp.start())
            return c

        def tail_wait(blk, c):
            tail_copies(blk, lambda cp: cp.wait())
            return c

        lax.fori_loop(0, N_EXPERTS, start, 0)
        lax.fori_loop(nused_ref[0], n_blocks, tail_start, 0)
        lax.fori_loop(0, N_EXPERTS, wait, 0)
        lax.fori_loop(nused_ref[0], n_blocks, tail_wait, 0)

    for _ in range(TOP_K):
        pltpu.make_async_copy(h_ref, xs_hbm.at[pl.ds(0, tm)], sem).wait()


def _dispatch(h_tiles, e0, e1, r0, r1, pad_start, counts, n_used, n_slots):
    tok = h_tiles.shape[0]
    tm = TM_DISPATCH
    grid_spec = pltpu.PrefetchScalarGridSpec(
        num_scalar_prefetch=7,
        grid=(tok // tm,),
        in_specs=[pl.BlockSpec((tm,) + ROW_TILE, lambda i, *_: (i, 0, 0))],
        out_specs=pl.BlockSpec(memory_space=pl.ANY),
        scratch_shapes=[pltpu.VMEM((MOE_ROWS // 2,) + ROW_TILE, F32),
                        pltpu.SemaphoreType.DMA(()), pltpu.SemaphoreType.DMA(())],
    )
    return pl.pallas_call(
        _dispatch_kernel,
        out_shape=jax.ShapeDtypeStruct((n_slots,) + ROW_TILE, F32),
        grid_spec=grid_spec,
        compiler_params=_params("arbitrary"),
        name="moe_dispatch",
    )(e0, e1, r0, r1, pad_start, counts, n_used, h_tiles)


def _expert_kernel(be_ref, nused_ref, x_ref, wg_ref, wu_ref, wd_ref, out_ref, wg_s, wu_s, wd_s):
    i = pl.program_id(0)

    @pl.when(i < nused_ref[0])
    def _():
        @pl.when((i == 0) | (be_ref[i] != be_ref[jnp.maximum(i - 1, 0)]))
        def _():
            wg_s[...] = wg_ref[0].astype(MXU_DTYPE)
            wu_s[...] = wu_ref[0].astype(MXU_DTYPE)
            wd_s[...] = wd_ref[0].astype(MXU_DTYPE)

        x = _tiles_to_rows(x_ref).astype(MXU_DTYPE)
        gate = jnp.dot(x, wg_s[...], preferred_element_type=F32)
        up = jnp.dot(x, wu_s[...], preferred_element_type=F32)
        hidden = (gate * _sigmoid(gate) * up).astype(MXU_DTYPE)
        _rows_to_tiles(out_ref, jnp.dot(hidden, wd_s[...], preferred_element_type=F32))

    @pl.when(i >= nused_ref[0])
    def _():
        out_ref[...] = jnp.zeros(out_ref.shape, F32)


def _expert_ffn(xs_tiles, block_expert, n_used, w_gate, w_up, w_down, layer):
    n_slots = xs_tiles.shape[0]
    n_blocks = block_expert.shape[0]
    rows = MOE_ROWS * ROW_TILE[0]
    blk = lambda i, be, nu: (jnp.maximum(jnp.minimum(i, nu[0] - 1), 0), 0)
    wsel = lambda i, be, nu: (layer * N_EXPERTS + be[i], 0, 0)
    grid_spec = pltpu.PrefetchScalarGridSpec(
        num_scalar_prefetch=2,
        grid=(n_blocks,),
        in_specs=[
            pl.BlockSpec((rows, LANES), blk),
            pl.BlockSpec((1, D_MODEL, D_EXPERT), wsel),
            pl.BlockSpec((1, D_MODEL, D_EXPERT), wsel),
            pl.BlockSpec((1, D_EXPERT, D_MODEL), wsel),
        ],
        out_specs=pl.BlockSpec((rows, LANES), lambda i, be, nu: (i, 0)),
        scratch_shapes=[pltpu.VMEM((D_MODEL, D_EXPERT), MXU_DTYPE), pltpu.VMEM((D_MODEL, D_EXPERT), MXU_DTYPE),
                        pltpu.VMEM((D_EXPERT, D_MODEL), MXU_DTYPE)],
    )
    out = pl.pallas_call(
        _expert_kernel,
        out_shape=jax.ShapeDtypeStruct((n_slots * ROW_TILE[0], LANES), F32),
        grid_spec=grid_spec,
        compiler_params=_params("arbitrary"),
        name="moe_experts",
    )(block_expert, n_used, xs_tiles.reshape(n_slots * ROW_TILE[0], LANES), w_gate, w_up, w_down)
    return out.reshape((n_slots,) + ROW_TILE)


def _combine_ln_kernel(e0_ref, e1_ref, r0_ref, r1_ref, start_ref, rows_hbm, h_ref, gate_ref, g_ref, b_ref, o_ref,
                       buf, sem):
    tm = h_ref.shape[0]
    i = pl.program_id(0)

    def gather(tile, p):
        base = tile * tm

        def issue(r, c):
            t = base + r
            dst = pl.ds(pl.multiple_of(r * ROW_TILE[0], ROW_TILE[0]), ROW_TILE[0])
            pltpu.make_async_copy(rows_hbm.at[start_ref[e0_ref[t]] + r0_ref[t]], buf.at[p, 0, dst],
                                  sem.at[p]).start(priority=0)
            pltpu.make_async_copy(rows_hbm.at[start_ref[e1_ref[t]] + r1_ref[t]], buf.at[p, 1, dst],
                                  sem.at[p]).start(priority=1)
            return c

        lax.fori_loop(0, tm, issue, 0, unroll=8)

    @pl.when(i == 0)
    def _():
        gather(0, 0)

    for p in range(2):
        @pl.when(i % 2 == p)
        def _(p=p):
            @pl.when(i + 1 < pl.num_programs(0))
            def _():
                gather(i + 1, 1 - p)

            for k in range(TOP_K):
                pltpu.make_async_copy(buf.at[1 - p, k], buf.at[p, k], sem.at[p]).wait()
            gate = gate_ref[...]
            acc = (ALPHA * h_ref[...] + gate[:, 0:1] * _tiles_to_rows(buf.at[p, 0])
                   + gate[:, 1:2] * _tiles_to_rows(buf.at[p, 1]))
            o_ref[...] = _layer_norm_rows(acc, g_ref[...], b_ref[...])


def _combine_ln(e0, e1, r0, r1, pad_start, rows_tiles, h, gates, g_row, b_row):
    tok = h.shape[0]
    tm = TM_COMBINE
    row = lambda i, *_: (i, 0)
    const = lambda i, *_: (0, 0)
    grid_spec = pltpu.PrefetchScalarGridSpec(
        num_scalar_prefetch=5,
        grid=(tok // tm,),
        in_specs=[
            pl.BlockSpec(memory_space=pl.ANY),
            pl.BlockSpec((tm, D_MODEL), row),
            pl.BlockSpec((tm, LANES), row),
            pl.BlockSpec((1, D_MODEL), const),
            pl.BlockSpec((1, D_MODEL), const),
        ],
        out_specs=pl.BlockSpec((tm, D_MODEL), row),
        scratch_shapes=[pltpu.VMEM((2, TOP_K, tm * ROW_TILE[0], LANES), F32), pltpu.SemaphoreType.DMA((2,))],
    )
    return pl.pallas_call(
        _combine_ln_kernel,
        out_shape=jax.ShapeDtypeStruct((tok, D_MODEL), F32),
        grid_spec=grid_spec,
        compiler_params=_params("arbitrary"),
        name="moe_combine_ln",
    )(e0, e1, r0, r1, pad_start, rows_tiles, h, gates, g_row, b_row)


def _slot_layout(counts, n_blocks):
    padded = (counts + MOE_ROWS - 1) // MOE_ROWS * MOE_ROWS
    pad_end = jnp.cumsum(padded)
    n_used = pad_end[-1:] // MOE_ROWS
    blocks = jnp.arange(n_blocks, dtype=jnp.int32)
    first_row = jnp.minimum(blocks, n_used - 1) * MOE_ROWS
    block_expert = jnp.sum(first_row[:, None] >= pad_end[None, :], axis=1)
    return (pad_end - padded).astype(jnp.int32), block_expert.astype(jnp.int32), n_used.astype(jnp.int32)


def _moe_ln(h, h_tiles, w_router, b_router, w_gate, w_up, w_down, layer, g_row, b_row):
    tok = h.shape[0]
    n_blocks = tok * TOP_K // MOE_ROWS + N_EXPERTS
    ids, gates, counts = _router(h, w_router, b_router)
    e0, e1, r0, r1 = (ids[c] for c in range(4))
    counts = counts[0, MOE_GROUPS:MOE_GROUPS + N_EXPERTS]
    pad_start, block_expert, n_used = _slot_layout(counts, n_blocks)
    xs = _dispatch(h_tiles.reshape((tok,) + ROW_TILE), e0, e1, r0, r1, pad_start, counts, n_used,
                   n_blocks * MOE_ROWS)
    rows = _expert_ffn(xs, block_expert, n_used, w_gate, w_up, w_down, layer)
    return _combine_ln(e0, e1, r0, r1, pad_start, rows, h, gates, g_row, b_row)


def _pad_lanes(v):
    return jnp.pad(v, (0, LANES - v.shape[0])).reshape(1, LANES)


def _even_mixer(h, bsz, seq, w_in, conv_a, conv_w, conv_b, dt_bias, a_log, d_skip, norm_w, w_out, g_row, b_row):
    w = jnp.pad(w_in, ((0, 0), (0, AB_PROJ - w_in.shape[1]))).astype(MXU_DTYPE)
    y_a, z, xbc, dt, acs, acst = _even_front(
        h, w, conv_a, conv_w, conv_b.reshape(1, -1), _pad_lanes(dt_bias), _pad_lanes(-jnp.exp(a_log)), bsz, seq)
    dskip_row = jnp.repeat(d_skip, SSM_HEAD_DIM).reshape(1, -1)
    y_b = _ssd(xbc, dt, acs, acst, z, dskip_row, norm_w.reshape(1, -1), bsz, seq)
    return _outproj_ln([y_a, y_b], h, w_out.astype(MXU_DTYPE), g_row, b_row)


def _odd_mixer(h, bsz, seq, w_in, i_bias, f_bias, hnorm_w, fox_f_bias, w_out, g_row, b_row):
    c = np.cumsum((0, MLSTM_W, MLSTM_W, MLSTM_W, MLSTM_HEADS, MLSTM_HEADS, MLSTM_W, FOX_W, FOX_W, FOX_W, FOX_HEADS))
    part = lambda j: w_in[:, c[j]:c[j + 1]]
    q, k, v, i_pre, f_pre, o_pre, fq, fk, fv, ff = (part(j) for j in range(10))
    gate_cols = jnp.concatenate([i_pre, f_pre, ff, f_pre], axis=1)
    gate_cols = jnp.pad(gate_cols, ((0, 0), (0, LANES - gate_cols.shape[1])))
    spread = lambda m: jnp.pad(m.reshape(-1, FOX_HEADS, FOX_HEAD_DIM),
                               ((0, 0), (0, 0), (0, LANES - FOX_HEAD_DIM))).reshape(-1, FOX_AUG)
    w = jnp.concatenate([q, k, v, o_pre, fq, spread(fk), fv, gate_cols], axis=1).astype(MXU_DTYPE)
    gate_bias = _pad_lanes(jnp.concatenate([i_bias, f_bias, fox_f_bias, f_bias]))
    q, k, v, o, fq, fk_aug, fvt_aug, gates, gates_t = _odd_front(h, w, gate_bias, bsz, seq)
    y_c = _mlstm(q, k, v, o, gates, gates_t, hnorm_w.reshape(1, -1), bsz, seq)
    y_d = _fox(fq, fk_aug, fvt_aug, gates_t, bsz, seq)
    return _outproj_ln([y_c, y_d], h, w_out.astype(MXU_DTYPE), g_row, b_row)


def kernel(x, ab_w_in, ab_conv_a, ab_conv_ssm_w, ab_conv_ssm_b, ab_dt_bias, ab_a_log, ab_d_skip, ab_norm_w, ab_w_out, cd_w_in, cd_i_bias, cd_f_bias, cd_hnorm_w, cd_fox_f_bias, cd_w_out, ln1_g, ln1_b, ln2_g, ln2_b, moe_rg_w, moe_rg_b, moe_re_w, moe_re_b, moe_w_gate, moe_w_up, moe_w_down):
    bsz, seq, d = x.shape
    h = x.reshape(bsz * seq, d)
    stack = lambda w: w.reshape((w.shape[0] * w.shape[1],) + w.shape[2:])
    w_gate, w_up, w_down = stack(moe_w_gate), stack(moe_w_up), stack(moe_w_down)
    for layer in range(DEPTH):
        j = layer // 2
        g1, b1 = ln1_g[layer].reshape(1, -1), ln1_b[layer].reshape(1, -1)
        if layer % 2 == 0:
            h, h_tiles = _even_mixer(h, bsz, seq, ab_w_in[j], ab_conv_a[j], ab_conv_ssm_w[j], ab_conv_ssm_b[j],
                                     ab_dt_bias[j], ab_a_log[j], ab_d_skip[j], ab_norm_w[j], ab_w_out[j], g1, b1)
        else:
            h, h_tiles = _odd_mixer(h, bsz, seq, cd_w_in[j], cd_i_bias[j], cd_f_bias[j], cd_hnorm_w[j],
                                    cd_fox_f_bias[j], cd_w_out[j], g1, b1)
        re_w = jnp.transpose(moe_re_w[layer], (1, 0, 2)).reshape(d, N_EXPERTS)
        w_router = jnp.pad(jnp.concatenate([moe_rg_w[layer], re_w], axis=1),
                           ((0, 0), (0, LANES - MOE_GROUPS - N_EXPERTS)))
        b_router = _pad_lanes(jnp.concatenate([moe_rg_b[layer], moe_re_b[layer].reshape(-1)]))
        h = _moe_ln(h, h_tiles, w_router, b_router, w_gate, w_up, w_down, layer,
                    ln2_g[layer].reshape(1, -1), ln2_b[layer].reshape(1, -1))
    return h.reshape(bsz, seq, d)
```

```python
import functools

import numpy as np
import jax
import jax.numpy as jnp
from jax import lax
from jax.experimental import pallas as pl
from jax.experimental.pallas import tpu as pltpu

F32 = jnp.float32
MXU_DTYPE = jnp.bfloat16
HIGHEST = lax.Precision.HIGHEST

D_MODEL = 1024
DEPTH = 4
ALPHA = (2 * DEPTH) ** 0.25
LN_EPS = 1e-5
CONV_DIM = D_MODEL // 2
CONV_WIDTH = 3
SSM_D_INNER = D_MODEL
SSM_HEAD_DIM = 64
SSM_HEADS = SSM_D_INNER // SSM_HEAD_DIM
SSM_GROUPS = 4
SSM_STATE = 64
SSM_CONV = 4
SSM_BC = SSM_GROUPS * SSM_STATE
SSM_CONV_DIM = SSM_D_INNER + 2 * SSM_BC
MLSTM_HEADS = 4
MLSTM_HEAD_DIM = D_MODEL // 8
MLSTM_W = MLSTM_HEADS * MLSTM_HEAD_DIM
FOX_HEADS = 8
FOX_HEAD_DIM = D_MODEL // 16
FOX_W = FOX_HEADS * FOX_HEAD_DIM
MOE_GROUPS = 4
EXPERTS_PER_GROUP = 8
N_EXPERTS = MOE_GROUPS * EXPERTS_PER_GROUP
TOP_K = 2
D_EXPERT = D_MODEL // 2

LANES = 128
SUBLANES = 8
VMEM_LIMIT_BYTES = 56 * 1024 * 1024

CHUNK = 128
SEQ_PAIR = 2
MLSTM_SEQS = 1
MLSTM_CHUNK = 512
TM_FRONT = 512
TM_EVEN_FRONT = 256
TM_OUT = 1024
TM_ROUTER = 512
TQ_FOX = 256
MOE_ROWS = 512
TM_COMBINE = 256
TM_DISPATCH = 512

AB_PROJ = 4224
FOX_AUG = FOX_HEADS * LANES
CD_FK = 5 * 512
CD_FV = CD_FK + FOX_AUG
CD_GATES = CD_FV + FOX_W
CD_PROJ = CD_GATES + LANES
LOG2E = 1.4426950408889634
TK_FOX = 128
FOX_ACC_ROWS = FOX_HEAD_DIM + SUBLANES
FOX_VROWS = LANES
G_I, G_F, G_FOX, G_BCUM = 0, 4, 8, 16


def _params(*sem):
    return pltpu.CompilerParams(dimension_semantics=sem, vmem_limit_bytes=VMEM_LIMIT_BYTES)


def _softplus(x):
    return jnp.maximum(x, 0.0) + jnp.log(1.0 + jnp.exp(-jnp.abs(x)))


def _sigmoid(x):
    return 1.0 / (1.0 + jnp.exp(-x))


def _layer_norm_rows(v, g, b):
    mu = jnp.mean(v, axis=-1, keepdims=True)
    c = v - mu
    var = jnp.mean(c * c, axis=-1, keepdims=True)
    return c * lax.rsqrt(var + LN_EPS) * g + b


def _tril(n, block):
    i = np.arange(n)
    m = (i[:, None] >= i[None, :]) & (i[:, None] // block == i[None, :] // block)
    return jnp.asarray(m.astype(np.float32), MXU_DTYPE)


def _full(shape):
    return pl.BlockSpec(shape, lambda *_: (0,) * len(shape), pipeline_mode=pl.Buffered(1))


ROW_TILE = (D_MODEL // LANES, LANES)


def _tiles_to_rows(ref):
    n = ref.shape[0] // ROW_TILE[0]
    return jnp.concatenate([ref[pl.ds(s, n, stride=ROW_TILE[0]), :] for s in range(ROW_TILE[0])], axis=1)


def _rows_to_tiles(ref, val):
    n = val.shape[0]
    for s in range(ROW_TILE[0]):
        ref[pl.ds(s, n, stride=ROW_TILE[0]), :] = val[:, s * LANES:(s + 1) * LANES]


def _split3(x):
    narrow = lambda v: v.astype(MXU_DTYPE).astype(F32)
    x1 = narrow(x)
    x2 = narrow(x - x1)
    return x1, x2, narrow(x - x1 - x2)


def _cumsum_rows(tril, x):
    parts = jnp.dot(tril, jnp.concatenate(_split3(x), axis=1).astype(MXU_DTYPE), preferred_element_type=F32)
    return parts[:, 0:LANES] + parts[:, LANES:2 * LANES] + parts[:, 2 * LANES:3 * LANES]


def _even_front_kernel(h_ref, w_ref, ca_ref, cw_ref, cb_ref, dtb_ref, aneg_ref, tril_ref,
                       ya_ref, z_ref, xbc_ref, dt_ref, acs_ref, acst_ref,
                       proj_buf, ua_ext, xbc_ext, *, tiles_per_seq):
    tm = h_ref.shape[0]
    i = pl.program_id(0)
    cur = i % 2

    @pl.when(i == 0)
    def _():
        proj_buf[1] = jnp.zeros(proj_buf.shape[1:], F32)

    @pl.when((i == 0) | ((i - 1) % tiles_per_seq == 0))
    def _():
        ua_ext[0:SUBLANES, :] = jnp.zeros((SUBLANES, CONV_DIM), F32)
        xbc_ext[0:SUBLANES, :] = jnp.zeros((SUBLANES, SSM_CONV_DIM), F32)

    proj_buf[cur] = jnp.dot(h_ref[...].astype(MXU_DTYPE), w_ref[...], preferred_element_type=F32)

    proj = proj_buf.at[1 - cur]
    z0 = 3 * CONV_DIM
    x0 = z0 + SSM_D_INNER
    d0 = x0 + SSM_CONV_DIM
    z_ref[...] = proj[:, z0:x0]

    ua_ext[SUBLANES:SUBLANES + tm, :] = proj[:, CONV_DIM:2 * CONV_DIM] * proj[:, 2 * CONV_DIM:3 * CONV_DIM]
    conv = ca_ref[0:1, :] * ua_ext[pl.ds(SUBLANES - 2, tm), :]
    for k in range(1, CONV_WIDTH):
        conv = conv + ca_ref[k:k + 1, :] * ua_ext[pl.ds(SUBLANES - (CONV_WIDTH - 1) + k, tm), :]
    ya_ref[...] = (proj[:, 0:CONV_DIM] * conv).astype(ya_ref.dtype)
    ua_ext[0:SUBLANES, :] = ua_ext[tm:tm + SUBLANES, :]

    xbc_ext[SUBLANES:SUBLANES + tm, :] = proj[:, x0:d0]
    conv = cb_ref[...] + cw_ref[0:1, :] * xbc_ext[pl.ds(SUBLANES - (SSM_CONV - 1), tm), :]
    for k in range(1, SSM_CONV):
        conv = conv + cw_ref[k:k + 1, :] * xbc_ext[pl.ds(SUBLANES - (SSM_CONV - 1) + k, tm), :]
    xbc_ref[...] = conv * _sigmoid(conv)
    xbc_ext[0:SUBLANES, :] = xbc_ext[tm:tm + SUBLANES, :]

    dt = _softplus(proj[:, d0:d0 + LANES] + dtb_ref[...])
    a = dt * aneg_ref[...]
    acs = _cumsum_rows(tril_ref[...], a)
    dt_ref[...] = dt
    acs_ref[...] = acs
    acst_ref[...] = acs.T


def _even_front(h, w_in, conv_a, conv_w, conv_b, dt_bias_row, aneg_row, bsz, seq):
    tok = bsz * seq
    tm = TM_EVEN_FRONT
    n_tiles = tok // tm
    row = lambda i: (jnp.maximum(i - 1, 0), 0)
    out_shapes = (
        jax.ShapeDtypeStruct((tok, CONV_DIM), MXU_DTYPE),
        jax.ShapeDtypeStruct((tok, SSM_D_INNER), F32),
        jax.ShapeDtypeStruct((tok, SSM_CONV_DIM), F32),
        jax.ShapeDtypeStruct((tok, LANES), F32),
        jax.ShapeDtypeStruct((tok, LANES), F32),
        jax.ShapeDtypeStruct((LANES, tok), F32),
    )
    return pl.pallas_call(
        functools.partial(_even_front_kernel, tiles_per_seq=seq // tm),
        out_shape=out_shapes,
        grid=(n_tiles + 1,),
        in_specs=[
            pl.BlockSpec((tm, D_MODEL), lambda i: (jnp.minimum(i, n_tiles - 1), 0)),
            _full((D_MODEL, AB_PROJ)),
            _full((CONV_WIDTH, CONV_DIM)),
            _full((SSM_CONV, SSM_CONV_DIM)),
            _full((1, SSM_CONV_DIM)),
            _full((1, LANES)),
            _full((1, LANES)),
            _full((tm, tm)),
        ],
        out_specs=(
            pl.BlockSpec((tm, CONV_DIM), row),
            pl.BlockSpec((tm, SSM_D_INNER), row),
            pl.BlockSpec((tm, SSM_CONV_DIM), row),
            pl.BlockSpec((tm, LANES), row),
            pl.BlockSpec((tm, LANES), row),
            pl.BlockSpec((LANES, tm), lambda i: (0, jnp.maximum(i - 1, 0))),
        ),
        scratch_shapes=[
            pltpu.VMEM((2, tm, AB_PROJ), F32),
            pltpu.VMEM((tm + SUBLANES, CONV_DIM), F32),
            pltpu.VMEM((tm + SUBLANES, SSM_CONV_DIM), F32),
        ],
        compiler_params=_params("arbitrary"),
        name="even_front",
    )(h, w_in, conv_a, conv_w, conv_b, dt_bias_row, aneg_row, _tril(tm, CHUNK))


def _bcast_heads(arr, n_heads, width):
    per = LANES // width
    length = arr.shape[0]
    lane = lax.broadcasted_iota(jnp.int32, (length, LANES), 1)
    outs = []
    for j in range(n_heads // per):
        v = jnp.broadcast_to(arr[:, j * per:j * per + 1], (length, LANES))
        for r in range(1, per):
            v = jnp.where(lane >= r * width, jnp.broadcast_to(arr[:, j * per + r:j * per + r + 1], (length, LANES)), v)
        outs.append(v)
    return jnp.concatenate(outs, axis=1)


def _expand_heads(arr, e3_ref):
    return jnp.dot(jnp.concatenate(_split3(arr), axis=1).astype(MXU_DTYPE), e3_ref[...],
                   preferred_element_type=F32)


def _ssd_kernel(*refs):
    nb = SEQ_PAIR
    xbc_ref, dt_ref, acs_ref, z_ref = refs[:4]
    acst_refs = refs[4:4 + nb]
    dskip_ref, nw_ref, e3_ref, y_ref, state = refs[4 + nb:]
    L = CHUNK
    P = SSM_HEAD_DIM
    R = SSM_HEADS // SSM_GROUPS
    GW = R * P

    @pl.when(pl.program_id(1) == 0)
    def _():
        state[...] = jnp.zeros(state.shape, F32)

    row = lax.broadcasted_iota(jnp.int32, (L, L), 0)
    col = lax.broadcasted_iota(jnp.int32, (L, L), 1)
    causal = row >= col
    lane_g = lax.broadcasted_iota(jnp.int32, (L, GW), 1)

    per_row = []
    for j in range(nb):
        acs = acs_ref[j]
        per_row += [dt_ref[j], jnp.exp(acs[L - 1:L, :] - acs), jnp.exp(acs)]
    expanded = _expand_heads(jnp.concatenate(per_row, axis=0), e3_ref)

    st_old = [state[c] for c in range(nb * SSM_GROUPS)]
    st_new = []
    for j in range(nb):
        xs = xbc_ref[j, :, 0:SSM_D_INNER]
        bm = xbc_ref[j, :, SSM_D_INNER:SSM_D_INNER + SSM_BC]
        cm = xbc_ref[j, :, SSM_D_INNER + SSM_BC:SSM_CONV_DIM]
        acs = acs_ref[j]
        acst = acst_refs[j][...]
        a_last = acs[L - 1:L, :]
        dtx = expanded[(3 * j) * L:(3 * j + 1) * L, :]
        decx = expanded[(3 * j + 1) * L:(3 * j + 2) * L, :]
        expx = expanded[(3 * j + 2) * L:(3 * j + 3) * L, :]
        xdt = xs * dtx
        xdec = (xdt * decx).astype(MXU_DTYPE)
        xdt_m = xdt.astype(MXU_DTYPE)
        chunk_decay = jnp.exp(jnp.broadcast_to(a_last, (SUBLANES, LANES)))
        cdx = _bcast_heads(chunk_decay, SSM_HEADS, P)[0:1, :]

        bm_t = bm.T.astype(MXU_DTYPE)
        cm_m = cm.astype(MXU_DTYPE)
        bm_m = bm.astype(MXU_DTYPE)

        ys = []
        for g in range(SSM_GROUPS):
            n0 = g * SSM_STATE
            c_g = cm_m[:, n0:n0 + SSM_STATE]
            cb = lax.dot_general(c_g, bm_m[:, n0:n0 + SSM_STATE], (((1,), (1,)), ((), ())),
                                 preferred_element_type=F32)
            ms = []
            for r in range(R):
                hd = g * R + r
                seg = jnp.exp(jnp.where(causal, acs[:, hd:hd + 1] - acst[hd:hd + 1, :], -jnp.inf))
                ms.append((cb * seg).astype(MXU_DTYPE))
            big = jnp.dot(jnp.concatenate(ms, axis=0), xdt_m[:, g * GW:(g + 1) * GW],
                          preferred_element_type=F32)
            y_diag = big[0:L, :]
            for r in range(1, R):
                y_diag = jnp.where(lane_g >= r * P, big[r * L:(r + 1) * L, :], y_diag)
            st = st_old[j * SSM_GROUPS + g]
            y_off = jnp.dot(c_g, st.astype(MXU_DTYPE), preferred_element_type=F32)
            new = jnp.dot(bm_t[n0:n0 + SSM_STATE, :], xdec[:, g * GW:(g + 1) * GW],
                          preferred_element_type=F32)
            st_new.append(st * cdx[:, g * GW:(g + 1) * GW] + new)
            ys.append(y_diag + y_off * expx[:, g * GW:(g + 1) * GW])
        y = jnp.concatenate(ys, axis=1) + dskip_ref[...] * xs
        z = z_ref[j]
        u = y * (z * _sigmoid(z))
        y = u * lax.rsqrt(jnp.mean(u * u, axis=-1, keepdims=True) + LN_EPS) * nw_ref[...]
        y_ref[j] = y.astype(y_ref.dtype)

    for c in range(nb * SSM_GROUPS):
        state[c] = st_new[c]


def _ssd(xbc, dt, acs, acst, z, dskip_row, normw_row, bsz, seq):
    nb = SEQ_PAIR
    nc = seq // CHUNK
    per_seq = lambda a: a.reshape(bsz, seq, a.shape[-1])
    blk = lambda width: pl.BlockSpec((nb, CHUNK, width), lambda g, c: (g, c, 0))
    acst_specs = [pl.BlockSpec((LANES, CHUNK), lambda g, c, j=j: (0, (g * nb + j) * nc + c)) for j in range(nb)]
    expand = np.zeros((LANES, SSM_D_INNER), np.float32)
    for hd in range(SSM_HEADS):
        expand[hd, hd * SSM_HEAD_DIM:(hd + 1) * SSM_HEAD_DIM] = 1.0
    e3 = jnp.asarray(np.concatenate([expand] * 3, axis=0), MXU_DTYPE)
    y = pl.pallas_call(
        _ssd_kernel,
        out_shape=jax.ShapeDtypeStruct((bsz, seq, SSM_D_INNER), MXU_DTYPE),
        grid=(bsz // nb, nc),
        in_specs=[blk(SSM_CONV_DIM), blk(LANES), blk(LANES), blk(SSM_D_INNER)] + acst_specs + [
            _full((1, SSM_D_INNER)),
            _full((1, SSM_D_INNER)),
            _full((3 * LANES, SSM_D_INNER)),
        ],
        out_specs=blk(SSM_D_INNER),
        scratch_shapes=[pltpu.VMEM((nb * SSM_GROUPS, SSM_STATE, SSM_D_INNER // SSM_GROUPS), F32)],
        compiler_params=_params("arbitrary", "arbitrary"),
        name="ssd_scan",
    )(per_seq(xbc), per_seq(dt), per_seq(acs), per_seq(z), *([acst] * nb), dskip_row, normw_row, e3)
    return y.reshape(bsz * seq, SSM_D_INNER)


def _outproj_ln_kernel(*refs, widths):
    n = len(widths)
    parts = refs[:n]
    h_ref, w_ref, g_ref, b_ref, o_ref, ot_ref = refs[n:]
    acc = ALPHA * h_ref[...]
    off = 0
    for p, wd in zip(parts, widths):
        acc = acc + jnp.dot(p[...].astype(MXU_DTYPE), w_ref[off:off + wd, :], preferred_element_type=F32)
        off += wd
    out = _layer_norm_rows(acc, g_ref[...], b_ref[...])
    o_ref[...] = out
    _rows_to_tiles(ot_ref, out)


def _outproj_ln(parts, h, w_out, g_row, b_row):
    tok = h.shape[0]
    tm = TM_OUT
    widths = tuple(p.shape[1] for p in parts)
    row = lambda i: (i, 0)
    return pl.pallas_call(
        functools.partial(_outproj_ln_kernel, widths=widths),
        out_shape=(jax.ShapeDtypeStruct((tok, D_MODEL), F32),
                   jax.ShapeDtypeStruct((tok * ROW_TILE[0], LANES), F32)),
        grid=(tok // tm,),
        in_specs=[pl.BlockSpec((tm, wd), row) for wd in widths] + [
            pl.BlockSpec((tm, D_MODEL), row),
            _full((sum(widths), D_MODEL)),
            _full((1, D_MODEL)),
            _full((1, D_MODEL)),
        ],
        out_specs=(pl.BlockSpec((tm, D_MODEL), row), pl.BlockSpec((tm * ROW_TILE[0], LANES), row)),
        compiler_params=_params("arbitrary"),
        name="outproj_ln",
    )(*parts, h, w_out, g_row, b_row)


def _odd_front_kernel(h_ref, w_ref, gb_ref, tril_ref,
                      q_ref, k_ref, v_ref, o_ref, fq_ref, fk_ref, fvt_ref, g_ref, gt_ref, carry):
    tm = h_ref.shape[0]
    proj = jnp.dot(h_ref[...].astype(MXU_DTYPE), w_ref[...], preferred_element_type=F32)
    q_ref[...] = proj[:, 0:512]
    k_ref[...] = proj[:, 512:1024] * (MLSTM_HEAD_DIM ** -0.5)
    v_ref[...] = proj[:, 1024:1536]
    o_ref[...] = _sigmoid(proj[:, 1536:2048])
    fq_ref[...] = proj[:, 2048:2560] * (FOX_HEAD_DIM ** -0.5 * LOG2E)

    @pl.when(pl.program_id(1) == 0)
    def _():
        carry[...] = jnp.zeros(carry.shape, F32)

    raw = proj[:, CD_GATES:CD_GATES + LANES] + gb_ref[...]
    lane = lax.broadcasted_iota(jnp.int32, (tm, LANES), 1)
    g = jnp.where(lane < G_F, raw, -_softplus(-raw))
    prev = carry[0:1, :]
    glob = _cumsum_rows(tril_ref[...], g) + prev
    before = []
    for c in range(tm // MLSTM_CHUNK):
        before.append(jnp.broadcast_to(prev, (MLSTM_CHUNK, LANES)))
        prev = glob[(c + 1) * MLSTM_CHUNK - 1:(c + 1) * MLSTM_CHUNK, :]
    carry[...] = jnp.broadcast_to(prev, carry.shape)
    local = glob - jnp.concatenate(before, axis=0)
    out = jnp.where((lane >= G_FOX) & (lane < G_BCUM), glob, jnp.where(lane >= G_BCUM, local, g))
    g_ref[...] = out
    gt_ref[...] = out.T

    is_bias = (lane >= FOX_HEAD_DIM) & (lane < FOX_HEAD_DIM + 3)
    for hd in range(FOX_HEADS):
        c1, c2, c3 = _split3(out[:, G_FOX + hd:G_FOX + hd + 1] * (-LOG2E))
        bias = jnp.where(lane == FOX_HEAD_DIM, c1, jnp.where(lane == FOX_HEAD_DIM + 1, c2, c3))
        k_h = proj[:, CD_FK + hd * LANES:CD_FK + (hd + 1) * LANES]
        fk_ref[:, hd * LANES:(hd + 1) * LANES] = jnp.where(is_bias, bias, k_h).astype(MXU_DTYPE)
    v_t = proj[:, CD_FV:CD_FV + FOX_W].T
    extra = jnp.where(lax.broadcasted_iota(jnp.int32, (FOX_VROWS - FOX_HEAD_DIM, tm), 0) == 0, 1.0, 0.0)
    fvt_ref[...] = jnp.concatenate(
        [blk for hd in range(FOX_HEADS) for blk in (v_t[hd * FOX_HEAD_DIM:(hd + 1) * FOX_HEAD_DIM, :], extra)],
        axis=0).astype(MXU_DTYPE)


def _odd_front(h, w_in, gate_bias_row, bsz, seq):
    tok = bsz * seq
    tm = TM_FRONT
    ns = seq // tm
    row = lambda b, s: (b * ns + s, 0)
    col = lambda b, s: (0, b * ns + s)
    wide = jax.ShapeDtypeStruct((tok, 512), F32)
    return pl.pallas_call(
        _odd_front_kernel,
        out_shape=(wide,) * 5 + (jax.ShapeDtypeStruct((tok, FOX_AUG), MXU_DTYPE),
                                 jax.ShapeDtypeStruct((FOX_HEADS * FOX_VROWS, tok), MXU_DTYPE),
                                 jax.ShapeDtypeStruct((tok, LANES), F32),
                                 jax.ShapeDtypeStruct((LANES, tok), F32)),
        grid=(bsz, ns),
        in_specs=[
            pl.BlockSpec((tm, D_MODEL), row),
            _full((D_MODEL, CD_PROJ)),
            _full((1, LANES)),
            _full((tm, tm)),
        ],
        out_specs=(pl.BlockSpec((tm, 512), row),) * 5 + (
            pl.BlockSpec((tm, FOX_AUG), row),
            pl.BlockSpec((FOX_HEADS * FOX_VROWS, tm), col),
            pl.BlockSpec((tm, LANES), row),
            pl.BlockSpec((LANES, tm), col),
        ),
        scratch_shapes=[pltpu.VMEM((SUBLANES, LANES), F32)],
        compiler_params=_params("arbitrary", "arbitrary"),
        name="odd_front",
    )(h, w_in, gate_bias_row, _tril(tm, tm))


def _mlstm_kernel(*refs):
    nb = MLSTM_SEQS
    q_ref, k_ref, v_ref, o_ref, g_ref = refs[:5]
    gt_refs = refs[5:5 + nb]
    nw_ref, y_ref, c_state, m_state = refs[5 + nb:]
    L = MLSTM_CHUNK
    DH = MLSTM_HEAD_DIM

    @pl.when(pl.program_id(1) == 0)
    def _():
        c_state[...] = jnp.zeros(c_state.shape, F32)
        m_state[...] = jnp.zeros(m_state.shape, F32)

    row = lax.broadcasted_iota(jnp.int32, (L, L), 0)
    col = lax.broadcasted_iota(jnp.int32, (L, L), 1)
    causal = row >= col
    ones_col = jnp.where(lax.broadcasted_iota(jnp.int32, (L, DH), 1) == 0, 1.0, 0.0)

    for j in range(nb):
        gates = g_ref[j]
        gates_t = gt_refs[j][...]
        for hd in range(MLSTM_HEADS):
            st = j * MLSTM_HEADS + hd
            sl = slice(hd * DH, (hd + 1) * DH)
            q = q_ref[j, :, sl].astype(MXU_DTYPE)
            k = k_ref[j, :, sl]
            v_ext = jnp.concatenate([v_ref[j, :, sl], ones_col], axis=1).astype(MXU_DTYPE)
            b_col = gates[:, G_BCUM + hd:G_BCUM + hd + 1]
            i_col = gates[:, G_I + hd:G_I + hd + 1]
            b_row = gates_t[G_BCUM + hd:G_BCUM + hd + 1, :]
            i_row = gates_t[G_I + hd:G_I + hd + 1, :]
            m_prev = m_state[st:st + 1, 0:1]
            c_ext = c_state[st]

            d_mat = jnp.where(causal, b_col - b_row + i_row, -jnp.inf)
            inter = b_col + m_prev
            m_t = jnp.maximum(jnp.max(d_mat, axis=-1, keepdims=True), inter)
            s_qk = lax.dot_general(q, k.astype(MXU_DTYPE), (((1,), (1,)), ((), ())), preferred_element_type=F32)
            w_qk = s_qk * jnp.exp(d_mat - m_t)
            s_inter = jnp.exp(inter - m_t)
            num_ext = (jnp.dot(w_qk.astype(MXU_DTYPE), v_ext, preferred_element_type=F32)
                       + s_inter * jnp.dot(q, c_ext.astype(MXU_DTYPE), preferred_element_type=F32))
            den = num_ext[:, DH:DH + 1]
            hval = num_ext[:, 0:DH] / jnp.maximum(jnp.abs(den), jnp.exp(-m_t))

            b_last = b_col[L - 1:L, :]
            g_log = b_last - b_col + i_col
            m_new = jnp.maximum(b_last + m_prev, jnp.max(g_log, axis=0, keepdims=True))
            w_k = jnp.exp(g_log - m_new)
            decay = jnp.exp(b_last + m_prev - m_new)
            kw_t = (k * w_k).T.astype(MXU_DTYPE)
            c_state[st] = decay * c_ext + jnp.dot(kw_t, v_ext, preferred_element_type=F32)
            m_state[st:st + 1, :] = jnp.broadcast_to(m_new, (1, LANES))

            mu = jnp.mean(hval, axis=-1, keepdims=True)
            cen = hval - mu
            var = jnp.mean(cen * cen, axis=-1, keepdims=True)
            y = o_ref[j, :, sl] * (cen * lax.rsqrt(var + LN_EPS) * nw_ref[:, sl])
            y_ref[j, :, sl] = y.astype(y_ref.dtype)


def _mlstm(q, k, v, o, gates, gates_t, hnorm_row, bsz, seq):
    nb = MLSTM_SEQS
    nc = seq // MLSTM_CHUNK
    per_seq = lambda a: a.reshape(bsz, seq, a.shape[-1])
    blk = lambda width: pl.BlockSpec((nb, MLSTM_CHUNK, width), lambda g, c: (g, c, 0))
    gt_specs = [pl.BlockSpec((LANES, MLSTM_CHUNK), lambda g, c, j=j: (0, (g * nb + j) * nc + c))
                for j in range(nb)]
    y = pl.pallas_call(
        _mlstm_kernel,
        out_shape=jax.ShapeDtypeStruct((bsz, seq, MLSTM_W), MXU_DTYPE),
        grid=(bsz // nb, nc),
        in_specs=[blk(MLSTM_W)] * 4 + [blk(LANES)] + gt_specs + [_full((1, MLSTM_W))],
        out_specs=blk(MLSTM_W),
        scratch_shapes=[pltpu.VMEM((nb * MLSTM_HEADS, MLSTM_HEAD_DIM, 2 * MLSTM_HEAD_DIM), F32),
                        pltpu.VMEM((nb * MLSTM_HEADS, LANES), F32)],
        compiler_params=_params("arbitrary", "arbitrary"),
        name="mlstm_scan",
    )(per_seq(q), per_seq(k), per_seq(v), per_seq(o), per_seq(gates), *([gates_t] * nb), hnorm_row)
    return y.reshape(bsz * seq, MLSTM_W)


def _fox_kernel(q_ref, k_ref, vt_ref, gt_ref, y_ref, *scratch):
    acc_refs = scratch[:FOX_HEADS]
    qa_ref = scratch[FOX_HEADS]
    tq = q_ref.shape[0]
    tk = TK_FOX
    dh = FOX_HEAD_DIM
    qi = pl.program_id(1)
    q_t = q_ref[...].T
    bias_rows = jnp.where(lax.broadcasted_iota(jnp.int32, (LANES - dh, tq), 0) < 3, 1.0, 0.0)
    for hd in range(FOX_HEADS):
        qa_ref[hd] = jnp.concatenate([q_t[hd * dh:(hd + 1) * dh, :], bias_rows], axis=0).astype(MXU_DTYPE)
        acc_refs[hd][...] = jnp.zeros(acc_refs[hd].shape, F32)
    cq = gt_ref[G_FOX:G_FOX + FOX_HEADS, :] * LOG2E
    key_pos = lax.broadcasted_iota(jnp.int32, (tk, tq), 0)
    qry_pos = lax.broadcasted_iota(jnp.int32, (tk, tq), 1) + qi * tq
    n_full = qi * (tq // tk)

    def block(j, ms, masked):
        k0 = pl.multiple_of(j * tk, tk)
        out = []
        for hd in range(FOX_HEADS):
            hs = slice(hd * LANES, (hd + 1) * LANES)
            s = jnp.dot(k_ref[pl.ds(k0, tk), hs], qa_ref[hd], preferred_element_type=F32)
            if masked:
                s = jnp.where(key_pos + k0 <= qry_pos, s, -jnp.inf)
            cq_h = cq[hd:hd + 1, :]
            m_new = jnp.maximum(ms[hd], jnp.max(s, axis=0, keepdims=True) + cq_h)
            p = jnp.exp2(s - (m_new - cq_h))
            pv = jnp.dot(vt_ref[hd * FOX_VROWS:(hd + 1) * FOX_VROWS, pl.ds(k0, tk)], p.astype(MXU_DTYPE),
                         preferred_element_type=F32)
            acc_refs[hd][...] = jnp.exp2(ms[hd] - m_new) * acc_refs[hd][...] + pv[0:FOX_ACC_ROWS, :]
            out.append(m_new)
        return tuple(out)

    ms = tuple(jnp.full((1, tq), -jnp.inf, F32) for _ in range(FOX_HEADS))
    ms = lax.fori_loop(0, n_full, functools.partial(block, masked=False), ms)
    for d in range(tq // tk):
        ms = block(n_full + d, ms, masked=True)
    outs = []
    for hd in range(FOX_HEADS):
        acc = acc_refs[hd][...]
        outs.append(acc[0:dh, :] / acc[dh:dh + 1, :])
    y_ref[...] = jnp.concatenate(outs, axis=0).T.astype(y_ref.dtype)


def _fox(fq, fk_aug, fvt_aug, gates_t, bsz, seq):
    tok = bsz * seq
    tq = TQ_FOX
    nq = seq // tq
    return pl.pallas_call(
        _fox_kernel,
        out_shape=jax.ShapeDtypeStruct((tok, FOX_W), MXU_DTYPE),
        grid=(bsz, nq),
        in_specs=[
            pl.BlockSpec((tq, FOX_W), lambda b, i: (b * nq + i, 0)),
            pl.BlockSpec((seq, FOX_AUG), lambda b, i: (b, 0)),
            pl.BlockSpec((FOX_HEADS * FOX_VROWS, seq), lambda b, i: (0, b)),
            pl.BlockSpec((LANES, tq), lambda b, i: (0, b * nq + i)),
        ],
        out_specs=pl.BlockSpec((tq, FOX_W), lambda b, i: (b * nq + i, 0)),
        scratch_shapes=[pltpu.VMEM((FOX_ACC_ROWS, tq), F32)] * FOX_HEADS
                       + [pltpu.VMEM((FOX_HEADS, LANES, tq), MXU_DTYPE)],
        compiler_params=_params("arbitrary", "arbitrary"),
        name="fox_attention",
    )(fq, fk_aug, fvt_aug, gates_t)


def _router_kernel(h_ref, w_ref, b_ref, stril_ref, id_ref, gate_ref, cnt_ref, carry):
    tm = h_ref.shape[0]

    @pl.when(pl.program_id(0) == 0)
    def _():
        carry[...] = jnp.zeros(carry.shape, F32)

    h = h_ref[...]
    h_hi = h.astype(MXU_DTYPE)
    h_lo = (h - h_hi.astype(F32)).astype(MXU_DTYPE)
    both = jnp.dot(h_hi, w_ref[...], preferred_element_type=F32)
    logits = (both[:, 0:LANES] + both[:, LANES:2 * LANES]
              + jnp.dot(h_lo, w_ref[:, 0:LANES], preferred_element_type=F32) + b_ref[...])
    lane = lax.broadcasted_iota(jnp.int32, (tm, LANES), 1).astype(F32)
    neg = -jnp.inf
    first = lambda hit: jnp.min(jnp.where(hit, lane, float(LANES)), axis=-1, keepdims=True)
    gl = jnp.where(lane < MOE_GROUPS, logits, neg)
    g_max = jnp.max(gl, axis=-1, keepdims=True)
    g_idx = first(gl == g_max)
    p_group = 1.0 / jnp.sum(jnp.exp(gl - g_max), axis=-1, keepdims=True)
    e_lo = MOE_GROUPS + g_idx * EXPERTS_PER_GROUP
    el = jnp.where((lane >= e_lo) & (lane < e_lo + EXPERTS_PER_GROUP), logits, neg)
    v1 = jnp.max(el, axis=-1, keepdims=True)
    i1 = first(el == v1)
    el2 = jnp.where(lane == i1, neg, el)
    v2 = jnp.max(el2, axis=-1, keepdims=True)
    i2 = first(el2 == v2)
    t = jnp.exp(v2 - v1)
    w1 = 1.0 / (1.0 + t)
    gate_ref[...] = jnp.where(lane == 0.0, p_group * w1, jnp.where(lane == 1.0, p_group * (t * w1), 0.0))

    hit1 = lane == i1
    hit2 = lane == i2
    sent = jnp.where(hit1 | hit2, 1.0, 0.0)
    before = jnp.dot(stril_ref[...], sent.astype(jnp.bfloat16), preferred_element_type=F32) + carry[0:1, :]
    r1 = jnp.sum(jnp.where(hit1, before, 0.0), axis=-1, keepdims=True)
    r2 = jnp.sum(jnp.where(hit2, before, 0.0), axis=-1, keepdims=True)
    total = before[tm - 1:tm, :] + sent[tm - 1:tm, :]
    carry[...] = jnp.broadcast_to(total, carry.shape)
    cnt_ref[...] = jnp.broadcast_to(total, cnt_ref.shape).astype(jnp.int32)
    ids = jnp.where(lane == 0.0, i1 - MOE_GROUPS, jnp.where(lane == 1.0, i2 - MOE_GROUPS,
                    jnp.where(lane == 2.0, r1, jnp.where(lane == 3.0, r2, 0.0))))
    id_ref[...] = ids.T[0:SUBLANES, :].astype(jnp.int32)


def _router(h, w_router, b_router):
    tok = h.shape[0]
    tm = TM_ROUTER
    row = lambda i: (i, 0)
    idx = np.arange(tm)
    stril = jnp.asarray((idx[:, None] > idx[None, :]).astype(np.float32), jnp.bfloat16)
    w_hi = w_router.astype(MXU_DTYPE)
    w_hi_lo = (w_hi, (w_router - w_hi.astype(F32)).astype(MXU_DTYPE))
    return pl.pallas_call(
        _router_kernel,
        out_shape=(jax.ShapeDtypeStruct((SUBLANES, tok), jnp.int32), jax.ShapeDtypeStruct((tok, LANES), F32),
                   jax.ShapeDtypeStruct((SUBLANES, LANES), jnp.int32)),
        grid=(tok // tm,),
        in_specs=[pl.BlockSpec((tm, D_MODEL), row), _full((D_MODEL, 2 * LANES)), _full((1, LANES)),
                  _full((tm, tm))],
        out_specs=(pl.BlockSpec((SUBLANES, tm), lambda i: (0, i)), pl.BlockSpec((tm, LANES), row),
                   _full((SUBLANES, LANES))),
        scratch_shapes=[pltpu.VMEM((SUBLANES, LANES), F32)],
        compiler_params=_params("arbitrary"),
        name="moe_router",
    )(h, jnp.concatenate([w_hi, w_hi_lo[1]], axis=1), b_router, stril)


def _pad_pieces(n):
    return tuple(1 << b for b in reversed(range((n - 1).bit_length())))


def _dispatch_kernel(e0_ref, e1_ref, r0_ref, r1_ref, start_ref, cnt_ref, nused_ref, h_ref, xs_hbm, zeros, sem,
                     pad_sem):
    tm = h_ref.shape[0]
    i = pl.program_id(0)
    base = i * tm

    def issue(r, c):
        t = base + r
        pltpu.make_async_copy(h_ref.at[r], xs_hbm.at[start_ref[e0_ref[t]] + r0_ref[t]], sem).start(priority=0)
        pltpu.make_async_copy(h_ref.at[r], xs_hbm.at[start_ref[e1_ref[t]] + r1_ref[t]], sem).start(priority=1)
        return c

    lax.fori_loop(0, tm, issue, 0, unroll=8)

    def pad_copies(e, fn):
        cnt = cnt_ref[e]
        n_pad = (MOE_ROWS - cnt % MOE_ROWS) % MOE_ROWS
        first = start_ref[e] + cnt
        for piece in _pad_pieces(MOE_ROWS):
            @pl.when((n_pad & piece) != 0)
            def _(piece=piece):
                off = first + (n_pad & ~(2 * piece - 1))
                fn(pltpu.make_async_copy(zeros.at[pl.ds(0, piece)], xs_hbm.at[pl.ds(off, piece)], pad_sem))

    def tail_copies(blk, fn):
        for part in range(MOE_ROWS // zeros.shape[0]):
            off = blk * MOE_ROWS + part * zeros.shape[0]
            fn(pltpu.make_async_copy(zeros, xs_hbm.at[pl.ds(off, zeros.shape[0])], pad_sem))

    @pl.when(i == 0)
    def _():
        zeros[...] = jnp.zeros(zeros.shape, F32)
        n_blocks = xs_hbm.shape[0] // MOE_ROWS

        def start(e, c):
            pad_copies(e, lambda cp: cp.start())
            return c

        def wait(e, c):
            pad_copies(e, lambda cp: cp.wait())
            return c

        def tail_start(blk, c):
            tail_copies(blk, lambda cp: cp.start())
            return c

        def tail_wait(blk, c):
            tail_copies(blk, lambda cp: cp.wait())
            return c

        lax.fori_loop(0, N_EXPERTS, start, 0)
        lax.fori_loop(nused_ref[0], n_blocks, tail_start, 0)
        lax.fori_loop(0, N_EXPERTS, wait, 0)
        lax.fori_loop(nused_ref[0], n_blocks, tail_wait, 0)

    for _ in range(TOP_K):
        pltpu.make_async_copy(h_ref, xs_hbm.at[pl.ds(0, tm)], sem).wait()


def _dispatch(h_tiles, e0, e1, r0, r1, pad_start, counts, n_used, n_slots):
    tok = h_tiles.shape[0]
    tm = TM_DISPATCH
    grid_spec = pltpu.PrefetchScalarGridSpec(
        num_scalar_prefetch=7,
        grid=(tok // tm,),
        in_specs=[pl.BlockSpec((tm,) + ROW_TILE, lambda i, *_: (i, 0, 0))],
        out_specs=pl.BlockSpec(memory_space=pl.ANY),
        scratch_shapes=[pltpu.VMEM((MOE_ROWS // 2,) + ROW_TILE, F32),
                        pltpu.SemaphoreType.DMA(()), pltpu.SemaphoreType.DMA(())],
    )
    return pl.pallas_call(
        _dispatch_kernel,
        out_shape=jax.ShapeDtypeStruct((n_slots,) + ROW_TILE, F32),
        grid_spec=grid_spec,
        compiler_params=_params("arbitrary"),
        name="moe_dispatch",
    )(e0, e1, r0, r1, pad_start, counts, n_used, h_tiles)


def _expert_kernel(be_ref, nused_ref, x_ref, wg_ref, wu_ref, wd_ref, out_ref, wg_s, wu_s, wd_s):
    i = pl.program_id(0)

    @pl.when(i < nused_ref[0])
    def _():
        @pl.when((i == 0) | (be_ref[i] != be_ref[jnp.maximum(i - 1, 0)]))
        def _():
            wg_s[...] = wg_ref[0].astype(MXU_DTYPE)
            wu_s[...] = wu_ref[0].astype(MXU_DTYPE)
            wd_s[...] = wd_ref[0].astype(MXU_DTYPE)

        x = _tiles_to_rows(x_ref).astype(MXU_DTYPE)
        gate = jnp.dot(x, wg_s[...], preferred_element_type=F32)
        up = jnp.dot(x, wu_s[...], preferred_element_type=F32)
        hidden = (gate * _sigmoid(gate) * up).astype(MXU_DTYPE)
        _rows_to_tiles(out_ref, jnp.dot(hidden, wd_s[...], preferred_element_type=F32))

    @pl.when(i >= nused_ref[0])
    def _():
        out_ref[...] = jnp.zeros(out_ref.shape, F32)


def _expert_ffn(xs_tiles, block_expert, n_used, w_gate, w_up, w_down, layer):
    n_slots = xs_tiles.shape[0]
    n_blocks = block_expert.shape[0]
    rows = MOE_ROWS * ROW_TILE[0]
    blk = lambda i, be, nu: (jnp.maximum(jnp.minimum(i, nu[0] - 1), 0), 0)
    wsel = lambda i, be, nu: (layer * N_EXPERTS + be[i], 0, 0)
    grid_spec = pltpu.PrefetchScalarGridSpec(
        num_scalar_prefetch=2,
        grid=(n_blocks,),
        in_specs=[
            pl.BlockSpec((rows, LANES), blk),
            pl.BlockSpec((1, D_MODEL, D_EXPERT), wsel),
            pl.BlockSpec((1, D_MODEL, D_EXPERT), wsel),
            pl.BlockSpec((1, D_EXPERT, D_MODEL), wsel),
        ],
        out_specs=pl.BlockSpec((rows, LANES), lambda i, be, nu: (i, 0)),
        scratch_shapes=[pltpu.VMEM((D_MODEL, D_EXPERT), MXU_DTYPE), pltpu.VMEM((D_MODEL, D_EXPERT), MXU_DTYPE),
                        pltpu.VMEM((D_EXPERT, D_MODEL), MXU_DTYPE)],
    )
    out = pl.pallas_call(
        _expert_kernel,
        out_shape=jax.ShapeDtypeStruct((n_slots * ROW_TILE[0], LANES), F32),
        grid_spec=grid_spec,
        compiler_params=_params("arbitrary"),
        name="moe_experts",
    )(block_expert, n_used, xs_tiles.reshape(n_slots * ROW_TILE[0], LANES), w_gate, w_up, w_down)
    return out.reshape((n_slots,) + ROW_TILE)


def _combine_ln_kernel(e0_ref, e1_ref, r0_ref, r1_ref, start_ref, rows_hbm, h_ref, gate_ref, g_ref, b_ref, o_ref,
                       buf, sem):
    tm = h_ref.shape[0]
    i = pl.program_id(0)

    def gather(tile, p):
        base = tile * tm

        def issue(r, c):
            t = base + r
            dst = pl.ds(pl.multiple_of(r * ROW_TILE[0], ROW_TILE[0]), ROW_TILE[0])
            pltpu.make_async_copy(rows_hbm.at[start_ref[e0_ref[t]] + r0_ref[t]], buf.at[p, 0, dst],
                                  sem.at[p]).start(priority=0)
            pltpu.make_async_copy(rows_hbm.at[start_ref[e1_ref[t]] + r1_ref[t]], buf.at[p, 1, dst],
                                  sem.at[p]).start(priority=1)
            return c

        lax.fori_loop(0, tm, issue, 0, unroll=8)

    @pl.when(i == 0)
    def _():
        gather(0, 0)

    for p in range(2):
        @pl.when(i % 2 == p)
        def _(p=p):
            @pl.when(i + 1 < pl.num_programs(0))
            def _():
                gather(i + 1, 1 - p)

            for k in range(TOP_K):
                pltpu.make_async_copy(buf.at[1 - p, k], buf.at[p, k], sem.at[p]).wait()
            gate = gate_ref[...]
            acc = (ALPHA * h_ref[...] + gate[:, 0:1] * _tiles_to_rows(buf.at[p, 0])
                   + gate[:, 1:2] * _tiles_to_rows(buf.at[p, 1]))
            o_ref[...] = _layer_norm_rows(acc, g_ref[...], b_ref[...])


def _combine_ln(e0, e1, r0, r1, pad_start, rows_tiles, h, gates, g_row, b_row):
    tok = h.shape[0]
    tm = TM_COMBINE
    row = lambda i, *_: (i, 0)
    const = lambda i, *_: (0, 0)
    grid_spec = pltpu.PrefetchScalarGridSpec(
        num_scalar_prefetch=5,
        grid=(tok // tm,),
        in_specs=[
            pl.BlockSpec(memory_space=pl.ANY),
            pl.BlockSpec((tm, D_MODEL), row),
            pl.BlockSpec((tm, LANES), row),
            pl.BlockSpec((1, D_MODEL), const),
            pl.BlockSpec((1, D_MODEL), const),
        ],
        out_specs=pl.BlockSpec((tm, D_MODEL), row),
        scratch_shapes=[pltpu.VMEM((2, TOP_K, tm * ROW_TILE[0], LANES), F32), pltpu.SemaphoreType.DMA((2,))],
    )
    return pl.pallas_call(
        _combine_ln_kernel,
        out_shape=jax.ShapeDtypeStruct((tok, D_MODEL), F32),
        grid_spec=grid_spec,
        compiler_params=_params("arbitrary"),
        name="moe_combine_ln",
    )(e0, e1, r0, r1, pad_start, rows_tiles, h, gates, g_row, b_row)


def _slot_layout(counts, n_blocks):
    padded = (counts + MOE_ROWS - 1) // MOE_ROWS * MOE_ROWS
    pad_end = jnp.cumsum(padded)
    n_used = pad_end[-1:] // MOE_ROWS
    blocks = jnp.arange(n_blocks, dtype=jnp.int32)
    first_row = jnp.minimum(blocks, n_used - 1) * MOE_ROWS
    block_expert = jnp.sum(first_row[:, None] >= pad_end[None, :], axis=1)
    return (pad_end - padded).astype(jnp.int32), block_expert.astype(jnp.int32), n_used.astype(jnp.int32)


def _moe_ln(h, h_tiles, w_router, b_router, w_gate, w_up, w_down, layer, g_row, b_row):
    tok = h.shape[0]
    n_blocks = tok * TOP_K // MOE_ROWS + N_EXPERTS
    ids, gates, counts = _router(h, w_router, b_router)
    e0, e1, r0, r1 = (ids[c] for c in range(4))
    counts = counts[0, MOE_GROUPS:MOE_GROUPS + N_EXPERTS]
    pad_start, block_expert, n_used = _slot_layout(counts, n_blocks)
    xs = _dispatch(h_tiles.reshape((tok,) + ROW_TILE), e0, e1, r0, r1, pad_start, counts, n_used,
                   n_blocks * MOE_ROWS)
    rows = _expert_ffn(xs, block_expert, n_used, w_gate, w_up, w_down, layer)
    return _combine_ln(e0, e1, r0, r1, pad_start, rows, h, gates, g_row, b_row)


def _pad_lanes(v):
    return jnp.pad(v, (0, LANES - v.shape[0])).reshape(1, LANES)


def _even_mixer(h, bsz, seq, w_in, conv_a, conv_w, conv_b, dt_bias, a_log, d_skip, norm_w, w_out, g_row, b_row):
    w = jnp.pad(w_in, ((0, 0), (0, AB_PROJ - w_in.shape[1]))).astype(MXU_DTYPE)
    y_a, z, xbc, dt, acs, acst = _even_front(
        h, w, conv_a, conv_w, conv_b.reshape(1, -1), _pad_lanes(dt_bias), _pad_lanes(-jnp.exp(a_log)), bsz, seq)
    dskip_row = jnp.repeat(d_skip, SSM_HEAD_DIM).reshape(1, -1)
    y_b = _ssd(xbc, dt, acs, acst, z, dskip_row, norm_w.reshape(1, -1), bsz, seq)
    return _outproj_ln([y_a, y_b], h, w_out.astype(MXU_DTYPE), g_row, b_row)


def _odd_mixer(h, bsz, seq, w_in, i_bias, f_bias, hnorm_w, fox_f_bias, w_out, g_row, b_row):
    c = np.cumsum((0, MLSTM_W, MLSTM_W, MLSTM_W, MLSTM_HEADS, MLSTM_HEADS, MLSTM_W, FOX_W, FOX_W, FOX_W, FOX_HEADS))
    part = lambda j: w_in[:, c[j]:c[j + 1]]
    q, k, v, i_pre, f_pre, o_pre, fq, fk, fv, ff = (part(j) for j in range(10))
    gate_cols = jnp.concatenate([i_pre, f_pre, ff, f_pre], axis=1)
    gate_cols = jnp.pad(gate_cols, ((0, 0), (0, LANES - gate_cols.shape[1])))
    spread = lambda m: jnp.pad(m.reshape(-1, FOX_HEADS, FOX_HEAD_DIM),
                               ((0, 0), (0, 0), (0, LANES - FOX_HEAD_DIM))).reshape(-1, FOX_AUG)
    w = jnp.concatenate([q, k, v, o_pre, fq, spread(fk), fv, gate_cols], axis=1).astype(MXU_DTYPE)
    gate_bias = _pad_lanes(jnp.concatenate([i_bias, f_bias, fox_f_bias, f_bias]))
    q, k, v, o, fq, fk_aug, fvt_aug, gates, gates_t = _odd_front(h, w, gate_bias, bsz, seq)
    y_c = _mlstm(q, k, v, o, gates, gates_t, hnorm_w.reshape(1, -1), bsz, seq)
    y_d = _fox(fq, fk_aug, fvt_aug, gates_t, bsz, seq)
    return _outproj_ln([y_c, y_d], h, w_out.astype(MXU_DTYPE), g_row, b_row)


def kernel(x, ab_w_in, ab_conv_a, ab_conv_ssm_w, ab_conv_ssm_b, ab_dt_bias, ab_a_log, ab_d_skip, ab_norm_w, ab_w_out, cd_w_in, cd_i_bias, cd_f_bias, cd_hnorm_w, cd_fox_f_bias, cd_w_out, ln1_g, ln1_b, ln2_g, ln2_b, moe_rg_w, moe_rg_b, moe_re_w, moe_re_b, moe_w_gate, moe_w_up, moe_w_down):
    bsz, seq, d = x.shape
    h = x.reshape(bsz * seq, d)
    stack = lambda w: w.reshape((w.shape[0] * w.shape[1],) + w.shape[2:])
    w_gate, w_up, w_down = stack(moe_w_gate), stack(moe_w_up), stack(moe_w_down)
    for layer in range(DEPTH):
        j = layer // 2
        g1, b1 = ln1_g[layer].reshape(1, -1), ln1_b[layer].reshape(1, -1)
        if layer % 2 == 0:
            h, h_tiles = _even_mixer(h, bsz, seq, ab_w_in[j], ab_conv_a[j], ab_conv_ssm_w[j], ab_conv_ssm_b[j],
                                     ab_dt_bias[j], ab_a_log[j], ab_d_skip[j], ab_norm_w[j], ab_w_out[j], g1, b1)
        else:
            h, h_tiles = _odd_mixer(h, bsz, seq, cd_w_in[j], cd_i_bias[j], cd_f_bias[j], cd_hnorm_w[j],
                                    cd_fox_f_bias[j], cd_w_out[j], g1, b1)
        re_w = jnp.transpose(moe_re_w[layer], (1, 0, 2)).reshape(d, N_EXPERTS)
        w_router = jnp.pad(jnp.concatenate([moe_rg_w[layer], re_w], axis=1),
                           ((0, 0), (0, LANES - MOE_GROUPS - N_EXPERTS)))
        b_router = _pad_lanes(jnp.concatenate([moe_rg_b[layer], moe_re_b[layer].reshape(-1)]))
        h = _moe_ln(h, h_tiles, w_router, b_router, w_gate, w_up, w_down, layer,
                    ln2_g[layer].reshape(1, -1), ln2_b[layer].reshape(1, -1))
    return h.reshape(bsz, seq, d)
```

```python
import functools

import numpy as np
import jax
import jax.numpy as jnp
from jax import lax
from jax.experimental import pallas as pl
from jax.experimental.pallas import tpu as pltpu

F32 = jnp.float32
MXU_DTYPE = jnp.bfloat16
HIGHEST = lax.Precision.HIGHEST

D_MODEL = 1024
DEPTH = 4
ALPHA = (2 * DEPTH) ** 0.25
LN_EPS = 1e-5
CONV_DIM = D_MODEL // 2
CONV_WIDTH = 3
SSM_D_INNER = D_MODEL
SSM_HEAD_DIM = 64
SSM_HEADS = SSM_D_INNER // SSM_HEAD_DIM
SSM_GROUPS = 4
SSM_STATE = 64
SSM_CONV = 4
SSM_BC = SSM_GROUPS * SSM_STATE
SSM_CONV_DIM = SSM_D_INNER + 2 * SSM_BC
MLSTM_HEADS = 4
MLSTM_HEAD_DIM = D_MODEL // 8
MLSTM_W = MLSTM_HEADS * MLSTM_HEAD_DIM
FOX_HEADS = 8
FOX_HEAD_DIM = D_MODEL // 16
FOX_W = FOX_HEADS * FOX_HEAD_DIM
MOE_GROUPS = 4
EXPERTS_PER_GROUP = 8
N_EXPERTS = MOE_GROUPS * EXPERTS_PER_GROUP
TOP_K = 2
D_EXPERT = D_MODEL // 2

LANES = 128
SUBLANES = 8
VMEM_LIMIT_BYTES = 56 * 1024 * 1024

CHUNK = 128
SEQ_PAIR = 2
MLSTM_SEQS = 1
MLSTM_CHUNK = 512
TM_FRONT = 512
TM_EVEN_FRONT = 256
CONV_ROWS, CONV_LANES = 64, 512
TM_OUT = 1024
TM_ROUTER = 512
TQ_FOX = 256
MOE_ROWS = 512
TM_COMBINE = 256
TM_DISPATCH = 2048

AB_PROJ = 4224
FOX_AUG = FOX_HEADS * LANES
CD_FK = 5 * 512
CD_FV = CD_FK + FOX_AUG
CD_GATES = CD_FV + FOX_W
CD_PROJ = CD_GATES + LANES
LOG2E = 1.4426950408889634
TK_FOX = 128
FOX_ACC_ROWS = FOX_HEAD_DIM + SUBLANES
FOX_VROWS = LANES
G_I, G_F, G_FOX, G_BCUM = 0, 4, 8, 16


def _params(*sem):
    return pltpu.CompilerParams(dimension_semantics=sem, vmem_limit_bytes=VMEM_LIMIT_BYTES)


def _softplus(x):
    return jnp.maximum(x, 0.0) + jnp.log(1.0 + jnp.exp(-jnp.abs(x)))


def _sigmoid(x):
    return 1.0 / (1.0 + jnp.exp(-x))


def _layer_norm_rows(v, g, b):
    mu = jnp.mean(v, axis=-1, keepdims=True)
    c = v - mu
    var = jnp.mean(c * c, axis=-1, keepdims=True)
    return c * lax.rsqrt(var + LN_EPS) * g + b


def _tril(n, block):
    i = np.arange(n)
    m = (i[:, None] >= i[None, :]) & (i[:, None] // block == i[None, :] // block)
    return jnp.asarray(m.astype(np.float32), MXU_DTYPE)


def _full(shape):
    return pl.BlockSpec(shape, lambda *_: (0,) * len(shape), pipeline_mode=pl.Buffered(1))


ROW_TILE = (D_MODEL // LANES, LANES)


def _tiles_to_rows(ref):
    n = ref.shape[0] // ROW_TILE[0]
    return jnp.concatenate([ref[pl.ds(s, n, stride=ROW_TILE[0]), :] for s in range(ROW_TILE[0])], axis=1)


def _rows_to_tiles(ref, val):
    n = val.shape[0]
    for s in range(ROW_TILE[0]):
        ref[pl.ds(s, n, stride=ROW_TILE[0]), :] = val[:, s * LANES:(s + 1) * LANES]


def _split3(x):
    narrow = lambda v: v.astype(MXU_DTYPE).astype(F32)
    x1 = narrow(x)
    x2 = narrow(x - x1)
    return x1, x2, narrow(x - x1 - x2)


def _cumsum_rows(tril, x):
    parts = jnp.dot(tril, jnp.concatenate(_split3(x), axis=1).astype(MXU_DTYPE), preferred_element_type=F32)
    return parts[:, 0:LANES] + parts[:, LANES:2 * LANES] + parts[:, 2 * LANES:3 * LANES]


def _even_front_kernel(h_ref, w_ref, ca_ref, cw_ref, cb_ref, dtb_ref, aneg_ref, tril_ref,
                       ya_ref, z_ref, xbc_ref, dt_ref, acs_ref, acst_ref,
                       proj_buf, ua_ext, xbc_ext, *, tiles_per_seq):
    tm = h_ref.shape[0]
    i = pl.program_id(0)
    cur = i % 2

    @pl.when(i == 0)
    def _():
        proj_buf[1] = jnp.zeros(proj_buf.shape[1:], F32)

    @pl.when((i == 0) | ((i - 1) % tiles_per_seq == 0))
    def _():
        ua_ext[0:SUBLANES, :] = jnp.zeros((SUBLANES, CONV_DIM), F32)
        xbc_ext[0:SUBLANES, :] = jnp.zeros((SUBLANES, SSM_CONV_DIM), F32)

    z0 = 3 * CONV_DIM
    x0 = z0 + SSM_D_INNER
    d0 = x0 + SSM_CONV_DIM

    x_in = h_ref[...].astype(MXU_DTYPE)
    proj = proj_buf.at[1 - cur]
    z_ref[...] = proj[:, z0:x0]

    n_blocks = tm // CONV_ROWS
    col_cuts = [AB_PROJ * g // n_blocks // LANES * LANES for g in range(n_blocks)] + [AB_PROJ]
    for r0 in range(0, tm, CONV_ROWS):
        c_lo, c_hi = col_cuts[r0 // CONV_ROWS], col_cuts[r0 // CONV_ROWS + 1]
        proj_buf[cur, :, c_lo:c_hi] = jnp.dot(x_in, w_ref[:, c_lo:c_hi], preferred_element_type=F32)
        rows = slice(r0, r0 + CONV_ROWS)
        ext_rows = slice(SUBLANES + r0, SUBLANES + r0 + CONV_ROWS)
        ua_ext[ext_rows, :] = proj[rows, CONV_DIM:2 * CONV_DIM] * proj[rows, 2 * CONV_DIM:3 * CONV_DIM]
        conv = None
        for k in range(CONV_WIDTH):
            tap = ca_ref[k:k + 1, :] * ua_ext[pl.ds(SUBLANES + r0 - (CONV_WIDTH - 1) + k, CONV_ROWS), :]
            conv = tap if conv is None else conv + tap
        ya_ref[rows, :] = (proj[rows, 0:CONV_DIM] * conv).astype(ya_ref.dtype)
        for c0 in range(0, SSM_CONV_DIM, CONV_LANES):
            cols = slice(c0, c0 + CONV_LANES)
            xbc_ext[ext_rows, cols] = proj[rows, x0 + c0:x0 + c0 + CONV_LANES]
            conv = cb_ref[:, cols]
            for k in range(SSM_CONV):
                conv = conv + cw_ref[k:k + 1, cols] * xbc_ext[pl.ds(SUBLANES + r0 - (SSM_CONV - 1) + k, CONV_ROWS), cols]
            xbc_ref[rows, cols] = conv * _sigmoid(conv)
    ua_ext[0:SUBLANES, :] = ua_ext[tm:tm + SUBLANES, :]
    xbc_ext[0:SUBLANES, :] = xbc_ext[tm:tm + SUBLANES, :]

    dt = _softplus(proj[:, d0:d0 + LANES] + dtb_ref[...])
    a = dt * aneg_ref[...]
    acs = _cumsum_rows(tril_ref[...], a)
    dt_ref[...] = dt
    acs_ref[...] = acs
    acst_ref[...] = acs.T


def _even_front(h, w_in, conv_a, conv_w, conv_b, dt_bias_row, aneg_row, bsz, seq):
    tok = bsz * seq
    tm = TM_EVEN_FRONT
    n_tiles = tok // tm
    row = lambda i: (jnp.maximum(i - 1, 0), 0)
    out_shapes = (
        jax.ShapeDtypeStruct((tok, CONV_DIM), MXU_DTYPE),
        jax.ShapeDtypeStruct((tok, SSM_D_INNER), F32),
        jax.ShapeDtypeStruct((tok, SSM_CONV_DIM), F32),
        jax.ShapeDtypeStruct((tok, LANES), F32),
        jax.ShapeDtypeStruct((tok, LANES), F32),
        jax.ShapeDtypeStruct((LANES, tok), F32),
    )
    return pl.pallas_call(
        functools.partial(_even_front_kernel, tiles_per_seq=seq // tm),
        out_shape=out_shapes,
        grid=(n_tiles + 1,),
        in_specs=[
            pl.BlockSpec((tm, D_MODEL), lambda i: (jnp.minimum(i, n_tiles - 1), 0)),
            _full((D_MODEL, AB_PROJ)),
            _full((CONV_WIDTH, CONV_DIM)),
            _full((SSM_CONV, SSM_CONV_DIM)),
            _full((1, SSM_CONV_DIM)),
            _full((1, LANES)),
            _full((1, LANES)),
            _full((tm, tm)),
        ],
        out_specs=(
            pl.BlockSpec((tm, CONV_DIM), row),
            pl.BlockSpec((tm, SSM_D_INNER), row),
            pl.BlockSpec((tm, SSM_CONV_DIM), row),
            pl.BlockSpec((tm, LANES), row),
            pl.BlockSpec((tm, LANES), row),
            pl.BlockSpec((LANES, tm), lambda i: (0, jnp.maximum(i - 1, 0))),
        ),
        scratch_shapes=[
            pltpu.VMEM((2, tm, AB_PROJ), F32),
            pltpu.VMEM((tm + SUBLANES, CONV_DIM), F32),
            pltpu.VMEM((tm + SUBLANES, SSM_CONV_DIM), F32),
        ],
        compiler_params=_params("arbitrary"),
        name="even_front",
    )(h, w_in, conv_a, conv_w, conv_b, dt_bias_row, aneg_row, _tril(tm, CHUNK))


def _bcast_heads(arr, n_heads, width):
    per = LANES // width
    length = arr.shape[0]
    lane = lax.broadcasted_iota(jnp.int32, (length, LANES), 1)
    outs = []
    for j in range(n_heads // per):
        v = jnp.broadcast_to(arr[:, j * per:j * per + 1], (length, LANES))
        for r in range(1, per):
            v = jnp.where(lane >= r * width, jnp.broadcast_to(arr[:, j * per + r:j * per + r + 1], (length, LANES)), v)
        outs.append(v)
    return jnp.concatenate(outs, axis=1)


def _expand_heads(arr, e3_ref):
    return jnp.dot(jnp.concatenate(_split3(arr), axis=1).astype(MXU_DTYPE), e3_ref[...],
                   preferred_element_type=F32)


def _ssd_kernel(*refs):
    nb = SEQ_PAIR
    xbc_ref, dt_ref, acs_ref, z_ref = refs[:4]
    acst_refs = refs[4:4 + nb]
    dskip_ref, nw_ref, e3_ref, y_ref, state = refs[4 + nb:]
    L = CHUNK
    P = SSM_HEAD_DIM
    R = SSM_HEADS // SSM_GROUPS
    GW = R * P

    @pl.when(pl.program_id(1) == 0)
    def _():
        state[...] = jnp.zeros(state.shape, F32)

    row = lax.broadcasted_iota(jnp.int32, (L, L), 0)
    col = lax.broadcasted_iota(jnp.int32, (L, L), 1)
    causal = row >= col
    lane_g = lax.broadcasted_iota(jnp.int32, (L, GW), 1)

    per_row = []
    for j in range(nb):
        acs = acs_ref[j]
        per_row += [dt_ref[j], jnp.exp(acs[L - 1:L, :] - acs), jnp.exp(acs)]
    expanded = _expand_heads(jnp.concatenate(per_row, axis=0), e3_ref)

    st_old = [state[c] for c in range(nb * SSM_GROUPS)]
    st_new = []
    for j in range(nb):
        xs = xbc_ref[j, :, 0:SSM_D_INNER]
        bm = xbc_ref[j, :, SSM_D_INNER:SSM_D_INNER + SSM_BC]
        cm = xbc_ref[j, :, SSM_D_INNER + SSM_BC:SSM_CONV_DIM]
        acs = acs_ref[j]
        acst = acst_refs[j][...]
        a_last = acs[L - 1:L, :]
        dtx = expanded[(3 * j) * L:(3 * j + 1) * L, :]
        decx = expanded[(3 * j + 1) * L:(3 * j + 2) * L, :]
        expx = expanded[(3 * j + 2) * L:(3 * j + 3) * L, :]
        xdt = xs * dtx
        xdec = (xdt * decx).astype(MXU_DTYPE)
        xdt_m = xdt.astype(MXU_DTYPE)
        chunk_decay = jnp.exp(jnp.broadcast_to(a_last, (SUBLANES, LANES)))
        cdx = _bcast_heads(chunk_decay, SSM_HEADS, P)[0:1, :]

        bm_t = bm.T.astype(MXU_DTYPE)
        cm_m = cm.astype(MXU_DTYPE)
        bm_m = bm.astype(MXU_DTYPE)

        ys = []
        for g in range(SSM_GROUPS):
            n0 = g * SSM_STATE
            c_g = cm_m[:, n0:n0 + SSM_STATE]
            cb = lax.dot_general(c_g, bm_m[:, n0:n0 + SSM_STATE], (((1,), (1,)), ((), ())),
                                 preferred_element_type=F32)
            ms = []
            for r in range(R):
                hd = g * R + r
                seg = jnp.exp(jnp.where(causal, acs[:, hd:hd + 1] - acst[hd:hd + 1, :], -jnp.inf))
                ms.append((cb * seg).astype(MXU_DTYPE))
            big = jnp.dot(jnp.concatenate(ms, axis=0), xdt_m[:, g * GW:(g + 1) * GW],
                          preferred_element_type=F32)
            y_diag = big[0:L, :]
            for r in range(1, R):
                y_diag = jnp.where(lane_g >= r * P, big[r * L:(r + 1) * L, :], y_diag)
            st = st_old[j * SSM_GROUPS + g]
            y_off = jnp.dot(c_g, st.astype(MXU_DTYPE), preferred_element_type=F32)
            new = jnp.dot(bm_t[n0:n0 + SSM_STATE, :], xdec[:, g * GW:(g + 1) * GW],
                          preferred_element_type=F32)
            st_new.append(st * cdx[:, g * GW:(g + 1) * GW] + new)
            ys.append(y_diag + y_off * expx[:, g * GW:(g + 1) * GW])
        y = jnp.concatenate(ys, axis=1) + dskip_ref[...] * xs
        z = z_ref[j]
        u = y * (z * _sigmoid(z))
        y = u * lax.rsqrt(jnp.mean(u * u, axis=-1, keepdims=True) + LN_EPS) * nw_ref[...]
        y_ref[j] = y.astype(y_ref.dtype)

    for c in range(nb * SSM_GROUPS):
        state[c] = st_new[c]


def _ssd(xbc, dt, acs, acst, z, dskip_row, normw_row, bsz, seq):
    nb = SEQ_PAIR
    nc = seq // CHUNK
    per_seq = lambda a: a.reshape(bsz, seq, a.shape[-1])
    blk = lambda width: pl.BlockSpec((nb, CHUNK, width), lambda g, c: (g, c, 0))
    acst_specs = [pl.BlockSpec((LANES, CHUNK), lambda g, c, j=j: (0, (g * nb + j) * nc + c)) for j in range(nb)]
    expand = np.zeros((LANES, SSM_D_INNER), np.float32)
    for hd in range(SSM_HEADS):
        expand[hd, hd * SSM_HEAD_DIM:(hd + 1) * SSM_HEAD_DIM] = 1.0
    e3 = jnp.asarray(np.concatenate([expand] * 3, axis=0), MXU_DTYPE)
    y = pl.pallas_call(
        _ssd_kernel,
        out_shape=jax.ShapeDtypeStruct((bsz, seq, SSM_D_INNER), MXU_DTYPE),
        grid=(bsz // nb, nc),
        in_specs=[blk(SSM_CONV_DIM), blk(LANES), blk(LANES), blk(SSM_D_INNER)] + acst_specs + [
            _full((1, SSM_D_INNER)),
            _full((1, SSM_D_INNER)),
            _full((3 * LANES, SSM_D_INNER)),
        ],
        out_specs=blk(SSM_D_INNER),
        scratch_shapes=[pltpu.VMEM((nb * SSM_GROUPS, SSM_STATE, SSM_D_INNER // SSM_GROUPS), F32)],
        compiler_params=_params("arbitrary", "arbitrary"),
        name="ssd_scan",
    )(per_seq(xbc), per_seq(dt), per_seq(acs), per_seq(z), *([acst] * nb), dskip_row, normw_row, e3)
    return y.reshape(bsz * seq, SSM_D_INNER)


def _outproj_ln_kernel(*refs, widths):
    n = len(widths)
    parts = refs[:n]
    h_ref, w_ref, g_ref, b_ref, o_ref, ot_ref = refs[n:]
    acc = ALPHA * h_ref[...]
    off = 0
    for p, wd in zip(parts, widths):
        acc = acc + jnp.dot(p[...].astype(MXU_DTYPE), w_ref[off:off + wd, :], preferred_element_type=F32)
        off += wd
    out = _layer_norm_rows(acc, g_ref[...], b_ref[...])
    o_ref[...] = out
    _rows_to_tiles(ot_ref, out)


def _outproj_ln(parts, h, w_out, g_row, b_row):
    tok = h.shape[0]
    tm = TM_OUT
    widths = tuple(p.shape[1] for p in parts)
    row = lambda i: (i, 0)
    return pl.pallas_call(
        functools.partial(_outproj_ln_kernel, widths=widths),
        out_shape=(jax.ShapeDtypeStruct((tok, D_MODEL), F32),
                   jax.ShapeDtypeStruct((tok * ROW_TILE[0], LANES), F32)),
        grid=(tok // tm,),
        in_specs=[pl.BlockSpec((tm, wd), row) for wd in widths] + [
            pl.BlockSpec((tm, D_MODEL), row),
            _full((sum(widths), D_MODEL)),
            _full((1, D_MODEL)),
            _full((1, D_MODEL)),
        ],
        out_specs=(pl.BlockSpec((tm, D_MODEL), row), pl.BlockSpec((tm * ROW_TILE[0], LANES), row)),
        compiler_params=_params("arbitrary"),
        name="outproj_ln",
    )(*parts, h, w_out, g_row, b_row)


def _odd_front_kernel(h_ref, w_ref, gb_ref, tril_ref,
                      q_ref, k_ref, v_ref, o_ref, fq_ref, fk_ref, fvt_ref, g_ref, gt_ref, carry):
    tm = h_ref.shape[0]
    x_in = h_ref[...].astype(MXU_DTYPE)
    proj_cols = lambda lo, hi: jnp.dot(x_in, w_ref[:, lo:hi], preferred_element_type=F32)

    @pl.when(pl.program_id(1) == 0)
    def _():
        carry[...] = jnp.zeros(carry.shape, F32)

    raw = proj_cols(CD_GATES, CD_GATES + LANES) + gb_ref[...]
    lane = lax.broadcasted_iota(jnp.int32, (tm, LANES), 1)
    g = jnp.where(lane < G_F, raw, -_softplus(-raw))
    prev = carry[0:1, :]
    glob = _cumsum_rows(tril_ref[...], g) + prev
    before = []
    for c in range(tm // MLSTM_CHUNK):
        before.append(jnp.broadcast_to(prev, (MLSTM_CHUNK, LANES)))
        prev = glob[(c + 1) * MLSTM_CHUNK - 1:(c + 1) * MLSTM_CHUNK, :]
    carry[...] = jnp.broadcast_to(prev, carry.shape)
    local = glob - jnp.concatenate(before, axis=0)
    out = jnp.where((lane >= G_FOX) & (lane < G_BCUM), glob, jnp.where(lane >= G_BCUM, local, g))
    g_ref[...] = out
    gt_ref[...] = out.T

    q_ref[...] = proj_cols(0, 512)
    k_ref[...] = proj_cols(512, 1024) * (MLSTM_HEAD_DIM ** -0.5)
    v_ref[...] = proj_cols(1024, 1536)
    o_ref[...] = _sigmoid(proj_cols(1536, 2048))
    fq_ref[...] = proj_cols(2048, 2560) * (FOX_HEAD_DIM ** -0.5 * LOG2E)

    is_bias = (lane >= FOX_HEAD_DIM) & (lane < FOX_HEAD_DIM + 3)
    fk = proj_cols(CD_FK, CD_FK + FOX_AUG)
    for hd in range(FOX_HEADS):
        c1, c2, c3 = _split3(out[:, G_FOX + hd:G_FOX + hd + 1] * (-LOG2E))
        bias = jnp.where(lane == FOX_HEAD_DIM, c1, jnp.where(lane == FOX_HEAD_DIM + 1, c2, c3))
        k_h = fk[:, hd * LANES:(hd + 1) * LANES]
        fk_ref[:, hd * LANES:(hd + 1) * LANES] = jnp.where(is_bias, bias, k_h).astype(MXU_DTYPE)
    v_t = proj_cols(CD_FV, CD_FV + FOX_W).T
    extra = jnp.where(lax.broadcasted_iota(jnp.int32, (FOX_VROWS - FOX_HEAD_DIM, tm), 0) == 0, 1.0, 0.0)
    fvt_ref[...] = jnp.concatenate(
        [blk for hd in range(FOX_HEADS) for blk in (v_t[hd * FOX_HEAD_DIM:(hd + 1) * FOX_HEAD_DIM, :], extra)],
        axis=0).astype(MXU_DTYPE)


def _odd_front(h, w_in, gate_bias_row, bsz, seq):
    tok = bsz * seq
    tm = TM_FRONT
    ns = seq // tm
    row = lambda b, s: (b * ns + s, 0)
    col = lambda b, s: (0, b * ns + s)
    wide = jax.ShapeDtypeStruct((tok, 512), F32)
    return pl.pallas_call(
        _odd_front_kernel,
        out_shape=(wide,) * 5 + (jax.ShapeDtypeStruct((tok, FOX_AUG), MXU_DTYPE),
                                 jax.ShapeDtypeStruct((FOX_HEADS * FOX_VROWS, tok), MXU_DTYPE),
                                 jax.ShapeDtypeStruct((tok, LANES), F32),
                                 jax.ShapeDtypeStruct((LANES, tok), F32)),
        grid=(bsz, ns),
        in_specs=[
            pl.BlockSpec((tm, D_MODEL), row),
            _full((D_MODEL, CD_PROJ)),
            _full((1, LANES)),
            _full((tm, tm)),
        ],
        out_specs=(pl.BlockSpec((tm, 512), row),) * 5 + (
            pl.BlockSpec((tm, FOX_AUG), row),
            pl.BlockSpec((FOX_HEADS * FOX_VROWS, tm), col),
            pl.BlockSpec((tm, LANES), row),
            pl.BlockSpec((LANES, tm), col),
        ),
        scratch_shapes=[pltpu.VMEM((SUBLANES, LANES), F32)],
        compiler_params=_params("arbitrary", "arbitrary"),
        name="odd_front",
    )(h, w_in, gate_bias_row, _tril(tm, tm))


def _mlstm_kernel(*refs):
    nb = MLSTM_SEQS
    q_ref, k_ref, v_ref, o_ref, g_ref = refs[:5]
    gt_refs = refs[5:5 + nb]
    nw_ref, y_ref, c_state, m_state = refs[5 + nb:]
    L = MLSTM_CHUNK
    DH = MLSTM_HEAD_DIM

    @pl.when(pl.program_id(1) == 0)
    def _():
        c_state[...] = jnp.zeros(c_state.shape, F32)
        m_state[...] = jnp.zeros(m_state.shape, F32)

    row = lax.broadcasted_iota(jnp.int32, (L, L), 0)
    col = lax.broadcasted_iota(jnp.int32, (L, L), 1)
    causal = row >= col
    ones_col = jnp.where(lax.broadcasted_iota(jnp.int32, (L, DH), 1) == 0, 1.0, 0.0)

    for j in range(nb):
        gates = g_ref[j]
        gates_t = gt_refs[j][...]
        for hd in range(MLSTM_HEADS):
            st = j * MLSTM_HEADS + hd
            sl = slice(hd * DH, (hd + 1) * DH)
            q = q_ref[j, :, sl].astype(MXU_DTYPE)
            k = k_ref[j, :, sl]
            v_ext = jnp.concatenate([v_ref[j, :, sl], ones_col], axis=1).astype(MXU_DTYPE)
            b_col = gates[:, G_BCUM + hd:G_BCUM + hd + 1]
            i_col = gates[:, G_I + hd:G_I + hd + 1]
            b_row = gates_t[G_BCUM + hd:G_BCUM + hd + 1, :]
            i_row = gates_t[G_I + hd:G_I + hd + 1, :]
            m_prev = m_state[st:st + 1, 0:1]
            c_ext = c_state[st]

            d_mat = jnp.where(causal, b_col - b_row + i_row, -jnp.inf)
            inter = b_col + m_prev
            m_t = jnp.maximum(jnp.max(d_mat, axis=-1, keepdims=True), inter)
            s_qk = lax.dot_general(q, k.astype(MXU_DTYPE), (((1,), (1,)), ((), ())), preferred_element_type=F32)
            w_qk = s_qk * jnp.exp(d_mat - m_t)
            s_inter = jnp.exp(inter - m_t)
            num_ext = (jnp.dot(w_qk.astype(MXU_DTYPE), v_ext, preferred_element_type=F32)
                       + s_inter * jnp.dot(q, c_ext.astype(MXU_DTYPE), preferred_element_type=F32))
            den = num_ext[:, DH:DH + 1]
            hval = num_ext[:, 0:DH] / jnp.maximum(jnp.abs(den), jnp.exp(-m_t))

            b_last = b_col[L - 1:L, :]
            g_log = b_last - b_col + i_col
            m_new = jnp.maximum(b_last + m_prev, jnp.max(g_log, axis=0, keepdims=True))
            w_k = jnp.exp(g_log - m_new)
            decay = jnp.exp(b_last + m_prev - m_new)
            kw_t = (k * w_k).T.astype(MXU_DTYPE)
            c_state[st] = decay * c_ext + jnp.dot(kw_t, v_ext, preferred_element_type=F32)
            m_state[st:st + 1, :] = jnp.broadcast_to(m_new, (1, LANES))

            mu = jnp.mean(hval, axis=-1, keepdims=True)
            cen = hval - mu
            var = jnp.mean(cen * cen, axis=-1, keepdims=True)
            y = o_ref[j, :, sl] * (cen * lax.rsqrt(var + LN_EPS) * nw_ref[:, sl])
            y_ref[j, :, sl] = y.astype(y_ref.dtype)


def _mlstm(q, k, v, o, gates, gates_t, hnorm_row, bsz, seq):
    nb = MLSTM_SEQS
    nc = seq // MLSTM_CHUNK
    per_seq = lambda a: a.reshape(bsz, seq, a.shape[-1])
    blk = lambda width: pl.BlockSpec((nb, MLSTM_CHUNK, width), lambda g, c: (g, c, 0))
    gt_specs = [pl.BlockSpec((LANES, MLSTM_CHUNK), lambda g, c, j=j: (0, (g * nb + j) * nc + c))
                for j in range(nb)]
    y = pl.pallas_call(
        _mlstm_kernel,
        out_shape=jax.ShapeDtypeStruct((bsz, seq, MLSTM_W), MXU_DTYPE),
        grid=(bsz // nb, nc),
        in_specs=[blk(MLSTM_W)] * 4 + [blk(LANES)] + gt_specs + [_full((1, MLSTM_W))],
        out_specs=blk(MLSTM_W),
        scratch_shapes=[pltpu.VMEM((nb * MLSTM_HEADS, MLSTM_HEAD_DIM, 2 * MLSTM_HEAD_DIM), F32),
                        pltpu.VMEM((nb * MLSTM_HEADS, LANES), F32)],
        compiler_params=_params("arbitrary", "arbitrary"),
        name="mlstm_scan",
    )(per_seq(q), per_seq(k), per_seq(v), per_seq(o), per_seq(gates), *([gates_t] * nb), hnorm_row)
    return y.reshape(bsz * seq, MLSTM_W)


def _fox_kernel(q_ref, k_ref, vt_ref, gt_ref, y_ref, *scratch):
    acc_refs = scratch[:FOX_HEADS]
    qa_ref = scratch[FOX_HEADS]
    tq = q_ref.shape[0]
    tk = TK_FOX
    dh = FOX_HEAD_DIM
    qi = pl.program_id(1)
    q_t = q_ref[...].T
    bias_rows = jnp.where(lax.broadcasted_iota(jnp.int32, (LANES - dh, tq), 0) < 3, 1.0, 0.0)
    for hd in range(FOX_HEADS):
        qa_ref[hd] = jnp.concatenate([q_t[hd * dh:(hd + 1) * dh, :], bias_rows], axis=0).astype(MXU_DTYPE)
        acc_refs[hd][...] = jnp.zeros(acc_refs[hd].shape, F32)
    cq = gt_ref[G_FOX:G_FOX + FOX_HEADS, :] * LOG2E
    key_pos = lax.broadcasted_iota(jnp.int32, (tk, tq), 0)
    qry_pos = lax.broadcasted_iota(jnp.int32, (tk, tq), 1) + qi * tq
    n_full = qi * (tq // tk)

    def block(j, ms, masked):
        k0 = pl.multiple_of(j * tk, tk)
        out = []
        for hd in range(FOX_HEADS):
            hs = slice(hd * LANES, (hd + 1) * LANES)
            s = jnp.dot(k_ref[pl.ds(k0, tk), hs], qa_ref[hd], preferred_element_type=F32)
            if masked:
                s = jnp.where(key_pos + k0 <= qry_pos, s, -jnp.inf)
            cq_h = cq[hd:hd + 1, :]
            m_new = jnp.maximum(ms[hd], jnp.max(s, axis=0, keepdims=True) + cq_h)
            p = jnp.exp2(s - (m_new - cq_h))
            pv = jnp.dot(vt_ref[hd * FOX_VROWS:(hd + 1) * FOX_VROWS, pl.ds(k0, tk)], p.astype(MXU_DTYPE),
                         preferred_element_type=F32)
            acc_refs[hd][...] = jnp.exp2(ms[hd] - m_new) * acc_refs[hd][...] + pv[0:FOX_ACC_ROWS, :]
            out.append(m_new)
        return tuple(out)

    ms = tuple(jnp.full((1, tq), -jnp.inf, F32) for _ in range(FOX_HEADS))
    ms = lax.fori_loop(0, n_full, functools.partial(block, masked=False), ms)
    for d in range(tq // tk):
        ms = block(n_full + d, ms, masked=True)
    outs = []
    for hd in range(FOX_HEADS):
        acc = acc_refs[hd][...]
        outs.append(acc[0:dh, :] / acc[dh:dh + 1, :])
    y_ref[...] = jnp.concatenate(outs, axis=0).T.astype(y_ref.dtype)


def _fox(fq, fk_aug, fvt_aug, gates_t, bsz, seq):
    tok = bsz * seq
    tq = TQ_FOX
    nq = seq // tq
    return pl.pallas_call(
        _fox_kernel,
        out_shape=jax.ShapeDtypeStruct((tok, FOX_W), MXU_DTYPE),
        grid=(bsz, nq),
        in_specs=[
            pl.BlockSpec((tq, FOX_W), lambda b, i: (b * nq + i, 0)),
            pl.BlockSpec((seq, FOX_AUG), lambda b, i: (b, 0)),
            pl.BlockSpec((FOX_HEADS * FOX_VROWS, seq), lambda b, i: (0, b)),
            pl.BlockSpec((LANES, tq), lambda b, i: (0, b * nq + i)),
        ],
        out_specs=pl.BlockSpec((tq, FOX_W), lambda b, i: (b * nq + i, 0)),
        scratch_shapes=[pltpu.VMEM((FOX_ACC_ROWS, tq), F32)] * FOX_HEADS
                       + [pltpu.VMEM((FOX_HEADS, LANES, tq), MXU_DTYPE)],
        compiler_params=_params("arbitrary", "arbitrary"),
        name="fox_attention",
    )(fq, fk_aug, fvt_aug, gates_t)


def _router_kernel(h_ref, w_ref, b_ref, stril_ref, id_ref, gate_ref, cnt_ref, carry):
    tm = h_ref.shape[0]

    @pl.when(pl.program_id(0) == 0)
    def _():
        carry[...] = jnp.zeros(carry.shape, F32)

    h = h_ref[...]
    h_hi = h.astype(MXU_DTYPE)
    h_lo = (h - h_hi.astype(F32)).astype(MXU_DTYPE)
    both = jnp.dot(h_hi, w_ref[...], preferred_element_type=F32)
    logits = (both[:, 0:LANES] + both[:, LANES:2 * LANES]
              + jnp.dot(h_lo, w_ref[:, 0:LANES], preferred_element_type=F32) + b_ref[...])
    lane = lax.broadcasted_iota(jnp.int32, (tm, LANES), 1).astype(F32)
    neg = -jnp.inf
    first = lambda hit: jnp.min(jnp.where(hit, lane, float(LANES)), axis=-1, keepdims=True)
    gl = jnp.where(lane < MOE_GROUPS, logits, neg)
    g_max = jnp.max(gl, axis=-1, keepdims=True)
    g_idx = first(gl == g_max)
    p_group = 1.0 / jnp.sum(jnp.exp(gl - g_max), axis=-1, keepdims=True)
    e_lo = MOE_GROUPS + g_idx * EXPERTS_PER_GROUP
    el = jnp.where((lane >= e_lo) & (lane < e_lo + EXPERTS_PER_GROUP), logits, neg)
    v1 = jnp.max(el, axis=-1, keepdims=True)
    i1 = first(el == v1)
    el2 = jnp.where(lane == i1, neg, el)
    v2 = jnp.max(el2, axis=-1, keepdims=True)
    i2 = first(el2 == v2)
    t = jnp.exp(v2 - v1)
    w1 = 1.0 / (1.0 + t)
    gate_ref[...] = jnp.where(lane == 0.0, p_group * w1, jnp.where(lane == 1.0, p_group * (t * w1), 0.0))

    hit1 = lane == i1
    hit2 = lane == i2
    sent = jnp.where(hit1 | hit2, 1.0, 0.0)
    before = jnp.dot(stril_ref[...], sent.astype(jnp.bfloat16), preferred_element_type=F32) + carry[0:1, :]
    r1 = jnp.sum(jnp.where(hit1, before, 0.0), axis=-1, keepdims=True)
    r2 = jnp.sum(jnp.where(hit2, before, 0.0), axis=-1, keepdims=True)
    total = before[tm - 1:tm, :] + sent[tm - 1:tm, :]
    carry[...] = jnp.broadcast_to(total, carry.shape)
    cnt_ref[...] = jnp.broadcast_to(total, cnt_ref.shape).astype(jnp.int32)
    ids = jnp.where(lane == 0.0, i1 - MOE_GROUPS, jnp.where(lane == 1.0, i2 - MOE_GROUPS,
                    jnp.where(lane == 2.0, r1, jnp.where(lane == 3.0, r2, 0.0))))
    id_ref[...] = ids.T[0:SUBLANES, :].astype(jnp.int32)


def _router(h, w_router, b_router):
    tok = h.shape[0]
    tm = TM_ROUTER
    row = lambda i: (i, 0)
    idx = np.arange(tm)
    stril = jnp.asarray((idx[:, None] > idx[None, :]).astype(np.float32), jnp.bfloat16)
    w_hi = w_router.astype(MXU_DTYPE)
    w_hi_lo = (w_hi, (w_router - w_hi.astype(F32)).astype(MXU_DTYPE))
    return pl.pallas_call(
        _router_kernel,
        out_shape=(jax.ShapeDtypeStruct((SUBLANES, tok), jnp.int32), jax.ShapeDtypeStruct((tok, LANES), F32),
                   jax.ShapeDtypeStruct((SUBLANES, LANES), jnp.int32)),
        grid=(tok // tm,),
        in_specs=[pl.BlockSpec((tm, D_MODEL), row), _full((D_MODEL, 2 * LANES)), _full((1, LANES)),
                  _full((tm, tm))],
        out_specs=(pl.BlockSpec((SUBLANES, tm), lambda i: (0, i)), pl.BlockSpec((tm, LANES), row),
                   _full((SUBLANES, LANES))),
        scratch_shapes=[pltpu.VMEM((SUBLANES, LANES), F32)],
        compiler_params=_params("arbitrary"),
        name="moe_router",
    )(h, jnp.concatenate([w_hi, w_hi_lo[1]], axis=1), b_router, stril)


def _pad_pieces(n):
    return tuple(1 << b for b in reversed(range((n - 1).bit_length())))


def _dispatch_kernel(e0_ref, e1_ref, r0_ref, r1_ref, start_ref, cnt_ref, nused_ref, h_ref, xs_hbm, zeros, sem,
                     pad_sem):
    tm = h_ref.shape[0]
    i = pl.program_id(0)
    base = i * tm

    def issue(r, c):
        t = base + r
        pltpu.make_async_copy(h_ref.at[r], xs_hbm.at[start_ref[e0_ref[t]] + r0_ref[t]], sem).start(priority=0)
        pltpu.make_async_copy(h_ref.at[r], xs_hbm.at[start_ref[e1_ref[t]] + r1_ref[t]], sem).start(priority=1)
        return c

    lax.fori_loop(0, tm, issue, 0, unroll=8)

    def pad_copies(e, fn):
        cnt = cnt_ref[e]
        n_pad = (MOE_ROWS - cnt % MOE_ROWS) % MOE_ROWS
        first = start_ref[e] + cnt
        for piece in _pad_pieces(MOE_ROWS):
            @pl.when((n_pad & piece) != 0)
            def _(piece=piece):
                off = first + (n_pad & ~(2 * piece - 1))
                fn(pltpu.make_async_copy(zeros.at[pl.ds(0, piece)], xs_hbm.at[pl.ds(off, piece)], pad_sem))

    def tail_copies(blk, fn):
        for part in range(MOE_ROWS // zeros.shape[0]):
            off = blk * MOE_ROWS + part * zeros.shape[0]
            fn(pltpu.make_async_copy(zeros, xs_hbm.at[pl.ds(off, zeros.shape[0])], pad_sem))

    @pl.when(i == 0)
    def _():
        zeros[...] = jnp.zeros(zeros.shape, F32)
        n_blocks = xs_hbm.shape[0] // MOE_ROWS

        def start(e, c):
            pad_copies(e, lambda cp: cp.start())
            return c

        def wait(e, c):
            pad_copies(e, lambda cp: cp.wait())
            return c

        def tail_start(blk, c):
            tail_copies(blk, lambda cp: cp.start())
            return c

        def tail_wait(blk, c):
            tail_copies(blk, lambda cp: cp.wait())
            return c

        lax.fori_loop(0, N_EXPERTS, start, 0)
        lax.fori_loop(nused_ref[0], n_blocks, tail_start, 0)
        lax.fori_loop(0, N_EXPERTS, wait, 0)
        lax.fori_loop(nused_ref[0], n_blocks, tail_wait, 0)

    for _ in range(TOP_K):
        pltpu.make_async_copy(h_ref, xs_hbm.at[pl.ds(0, tm)], sem).wait()


def _dispatch(h_tiles, e0, e1, r0, r1, pad_start, counts, n_used, n_slots):
    tok = h_tiles.shape[0]
    tm = TM_DISPATCH
    assert tok % tm == 0, (tok, tm)
    grid_spec = pltpu.PrefetchScalarGridSpec(
        num_scalar_prefetch=7,
        grid=(tok // tm,),
        in_specs=[pl.BlockSpec((tm,) + ROW_TILE, lambda i, *_: (i, 0, 0))],
        out_specs=pl.BlockSpec(memory_space=pl.ANY),
        scratch_shapes=[pltpu.VMEM((MOE_ROWS // 2,) + ROW_TILE, F32),
                        pltpu.SemaphoreType.DMA(()), pltpu.SemaphoreType.DMA(())],
    )
    return pl.pallas_call(
        _dispatch_kernel,
        out_shape=jax.ShapeDtypeStruct((n_slots,) + ROW_TILE, F32),
        grid_spec=grid_spec,
        compiler_params=_params("arbitrary"),
        name="moe_dispatch",
    )(e0, e1, r0, r1, pad_start, counts, n_used, h_tiles)


def _expert_kernel(be_ref, nused_ref, x_ref, wg_ref, wu_ref, wd_ref, out_ref, wg_s, wu_s, wd_s):
    i = pl.program_id(0)

    @pl.when(i < nused_ref[0])
    def _():
        @pl.when((i == 0) | (be_ref[i] != be_ref[jnp.maximum(i - 1, 0)]))
        def _():
            wg_s[...] = wg_ref[0].astype(MXU_DTYPE)
            wu_s[...] = wu_ref[0].astype(MXU_DTYPE)
            wd_s[...] = wd_ref[0].astype(MXU_DTYPE)

        x = _tiles_to_rows(x_ref).astype(MXU_DTYPE)
        gate = jnp.dot(x, wg_s[...], preferred_element_type=F32)
        up = jnp.dot(x, wu_s[...], preferred_element_type=F32)
        hidden = (gate * _sigmoid(gate) * up).astype(MXU_DTYPE)
        _rows_to_tiles(out_ref, jnp.dot(hidden, wd_s[...], preferred_element_type=F32))

    @pl.when(i >= nused_ref[0])
    def _():
        out_ref[...] = jnp.zeros(out_ref.shape, F32)


def _expert_ffn(xs_tiles, block_expert, n_used, w_gate, w_up, w_down, layer):
    n_slots = xs_tiles.shape[0]
    n_blocks = block_expert.shape[0]
    rows = MOE_ROWS * ROW_TILE[0]
    blk = lambda i, be, nu: (jnp.maximum(jnp.minimum(i, nu[0] - 1), 0), 0)
    wsel = lambda i, be, nu: (layer * N_EXPERTS + be[i], 0, 0)
    grid_spec = pltpu.PrefetchScalarGridSpec(
        num_scalar_prefetch=2,
        grid=(n_blocks,),
        in_specs=[
            pl.BlockSpec((rows, LANES), blk),
            pl.BlockSpec((1, D_MODEL, D_EXPERT), wsel),
            pl.BlockSpec((1, D_MODEL, D_EXPERT), wsel),
            pl.BlockSpec((1, D_EXPERT, D_MODEL), wsel),
        ],
        out_specs=pl.BlockSpec((rows, LANES), lambda i, be, nu: (i, 0)),
        scratch_shapes=[pltpu.VMEM((D_MODEL, D_EXPERT), MXU_DTYPE), pltpu.VMEM((D_MODEL, D_EXPERT), MXU_DTYPE),
                        pltpu.VMEM((D_EXPERT, D_MODEL), MXU_DTYPE)],
    )
    out = pl.pallas_call(
        _expert_kernel,
        out_shape=jax.ShapeDtypeStruct((n_slots * ROW_TILE[0], LANES), F32),
        grid_spec=grid_spec,
        compiler_params=_params("arbitrary"),
        name="moe_experts",
    )(block_expert, n_used, xs_tiles.reshape(n_slots * ROW_TILE[0], LANES), w_gate, w_up, w_down)
    return out.reshape((n_slots,) + ROW_TILE)


def _combine_ln_kernel(e0_ref, e1_ref, r0_ref, r1_ref, start_ref, rows_hbm, h_ref, gate_ref, g_ref, b_ref, o_ref,
                       buf, sem):
    tm = h_ref.shape[0]
    i = pl.program_id(0)

    def gather(tile, p):
        base = tile * tm

        def issue(r, c):
            t = base + r
            dst = pl.ds(pl.multiple_of(r * ROW_TILE[0], ROW_TILE[0]), ROW_TILE[0])
            pltpu.make_async_copy(rows_hbm.at[start_ref[e0_ref[t]] + r0_ref[t]], buf.at[p, 0, dst],
                                  sem.at[p]).start(priority=0)
            pltpu.make_async_copy(rows_hbm.at[start_ref[e1_ref[t]] + r1_ref[t]], buf.at[p, 1, dst],
                                  sem.at[p]).start(priority=1)
            return c

        lax.fori_loop(0, tm, issue, 0, unroll=8)

    @pl.when(i == 0)
    def _():
        gather(0, 0)

    for p in range(2):
        @pl.when(i % 2 == p)
        def _(p=p):
            @pl.when(i + 1 < pl.num_programs(0))
            def _():
                gather(i + 1, 1 - p)

            for k in range(TOP_K):
                pltpu.make_async_copy(buf.at[1 - p, k], buf.at[p, k], sem.at[p]).wait()
            gate = gate_ref[...]
            acc = (ALPHA * h_ref[...] + gate[:, 0:1] * _tiles_to_rows(buf.at[p, 0])
                   + gate[:, 1:2] * _tiles_to_rows(buf.at[p, 1]))
            o_ref[...] = _layer_norm_rows(acc, g_ref[...], b_ref[...])


def _combine_ln(e0, e1, r0, r1, pad_start, rows_tiles, h, gates, g_row, b_row):
    tok = h.shape[0]
    tm = TM_COMBINE
    row = lambda i, *_: (i, 0)
    const = lambda i, *_: (0, 0)
    grid_spec = pltpu.PrefetchScalarGridSpec(
        num_scalar_prefetch=5,
        grid=(tok // tm,),
        in_specs=[
            pl.BlockSpec(memory_space=pl.ANY),
            pl.BlockSpec((tm, D_MODEL), row),
            pl.BlockSpec((tm, LANES), row),
            pl.BlockSpec((1, D_MODEL), const),
            pl.BlockSpec((1, D_MODEL), const),
        ],
        out_specs=pl.BlockSpec((tm, D_MODEL), row),
        scratch_shapes=[pltpu.VMEM((2, TOP_K, tm * ROW_TILE[0], LANES), F32), pltpu.SemaphoreType.DMA((2,))],
    )
    return pl.pallas_call(
        _combine_ln_kernel,
        out_shape=jax.ShapeDtypeStruct((tok, D_MODEL), F32),
        grid_spec=grid_spec,
        compiler_params=_params("arbitrary"),
        name="moe_combine_ln",
    )(e0, e1, r0, r1, pad_start, rows_tiles, h, gates, g_row, b_row)


def _slot_layout(counts, n_blocks):
    padded = (counts + MOE_ROWS - 1) // MOE_ROWS * MOE_ROWS
    pad_end = jnp.cumsum(padded)
    n_used = pad_end[-1:] // MOE_ROWS
    blocks = jnp.arange(n_blocks, dtype=jnp.int32)
    first_row = jnp.minimum(blocks, n_used - 1) * MOE_ROWS
    block_expert = jnp.sum(first_row[:, None] >= pad_end[None, :], axis=1)
    return (pad_end - padded).astype(jnp.int32), block_expert.astype(jnp.int32), n_used.astype(jnp.int32)


def _moe_ln(h, h_tiles, w_router, b_router, w_gate, w_up, w_down, layer, g_row, b_row):
    tok = h.shape[0]
    n_blocks = tok * TOP_K // MOE_ROWS + N_EXPERTS
    ids, gates, counts = _router(h, w_router, b_router)
    e0, e1, r0, r1 = (ids[c] for c in range(4))
    counts = counts[0, MOE_GROUPS:MOE_GROUPS + N_EXPERTS]
    pad_start, block_expert, n_used = _slot_layout(counts, n_blocks)
    xs = _dispatch(h_tiles.reshape((tok,) + ROW_TILE), e0, e1, r0, r1, pad_start, counts, n_used,
                   n_blocks * MOE_ROWS)
    rows = _expert_ffn(xs, block_expert, n_used, w_gate, w_up, w_down, layer)
    return _combine_ln(e0, e1, r0, r1, pad_start, rows, h, gates, g_row, b_row)


def _pad_lanes(v):
    return jnp.pad(v, (0, LANES - v.shape[0])).reshape(1, LANES)


def _even_mixer(h, bsz, seq, w_in, conv_a, conv_w, conv_b, dt_bias, a_log, d_skip, norm_w, w_out, g_row, b_row):
    w = jnp.pad(w_in, ((0, 0), (0, AB_PROJ - w_in.shape[1]))).astype(MXU_DTYPE)
    y_a, z, xbc, dt, acs, acst = _even_front(
        h, w, conv_a, conv_w, conv_b.reshape(1, -1), _pad_lanes(dt_bias), _pad_lanes(-jnp.exp(a_log)), bsz, seq)
    dskip_row = jnp.repeat(d_skip, SSM_HEAD_DIM).reshape(1, -1)
    y_b = _ssd(xbc, dt, acs, acst, z, dskip_row, norm_w.reshape(1, -1), bsz, seq)
    return _outproj_ln([y_a, y_b], h, w_out.astype(MXU_DTYPE), g_row, b_row)


def _odd_mixer(h, bsz, seq, w_in, i_bias, f_bias, hnorm_w, fox_f_bias, w_out, g_row, b_row):
    c = np.cumsum((0, MLSTM_W, MLSTM_W, MLSTM_W, MLSTM_HEADS, MLSTM_HEADS, MLSTM_W, FOX_W, FOX_W, FOX_W, FOX_HEADS))
    part = lambda j: w_in[:, c[j]:c[j + 1]]
    q, k, v, i_pre, f_pre, o_pre, fq, fk, fv, ff = (part(j) for j in range(10))
    gate_cols = jnp.concatenate([i_pre, f_pre, ff, f_pre], axis=1)
    gate_cols = jnp.pad(gate_cols, ((0, 0), (0, LANES - gate_cols.shape[1])))
    spread = lambda m: jnp.pad(m.reshape(-1, FOX_HEADS, FOX_HEAD_DIM),
                               ((0, 0), (0, 0), (0, LANES - FOX_HEAD_DIM))).reshape(-1, FOX_AUG)
    w = jnp.concatenate([q, k, v, o_pre, fq, spread(fk), fv, gate_cols], axis=1).astype(MXU_DTYPE)
    gate_bias = _pad_lanes(jnp.concatenate([i_bias, f_bias, fox_f_bias, f_bias]))
    q, k, v, o, fq, fk_aug, fvt_aug, gates, gates_t = _odd_front(h, w, gate_bias, bsz, seq)
    y_c = _mlstm(q, k, v, o, gates, gates_t, hnorm_w.reshape(1, -1), bsz, seq)
    y_d = _fox(fq, fk_aug, fvt_aug, gates_t, bsz, seq)
    return _outproj_ln([y_c, y_d], h, w_out.astype(MXU_DTYPE), g_row, b_row)


def kernel(x, ab_w_in, ab_conv_a, ab_conv_ssm_w, ab_conv_ssm_b, ab_dt_bias, ab_a_log, ab_d_skip, ab_norm_w, ab_w_out, cd_w_in, cd_i_bias, cd_f_bias, cd_hnorm_w, cd_fox_f_bias, cd_w_out, ln1_g, ln1_b, ln2_g, ln2_b, moe_rg_w, moe_rg_b, moe_re_w, moe_re_b, moe_w_gate, moe_w_up, moe_w_down):
    bsz, seq, d = x.shape
    h = x.reshape(bsz * seq, d)
    stack = lambda w: w.reshape((w.shape[0] * w.shape[1],) + w.shape[2:])
    w_gate, w_up, w_down = stack(moe_w_gate), stack(moe_w_up), stack(moe_w_down)
    for layer in range(DEPTH):
        j = layer // 2
        g1, b1 = ln1_g[layer].reshape(1, -1), ln1_b[layer].reshape(1, -1)
        if layer % 2 == 0:
            h, h_tiles = _even_mixer(h, bsz, seq, ab_w_in[j], ab_conv_a[j], ab_conv_ssm_w[j], ab_conv_ssm_b[j],
                                     ab_dt_bias[j], ab_a_log[j], ab_d_skip[j], ab_norm_w[j], ab_w_out[j], g1, b1)
        else:
            h, h_tiles = _odd_mixer(h, bsz, seq, cd_w_in[j], cd_i_bias[j], cd_f_bias[j], cd_hnorm_w[j],
                                    cd_fox_f_bias[j], cd_w_out[j], g1, b1)
        re_w = jnp.transpose(moe_re_w[layer], (1, 0, 2)).reshape(d, N_EXPERTS)
        w_router = jnp.pad(jnp.concatenate([moe_rg_w[layer], re_w], axis=1),
                           ((0, 0), (0, LANES - MOE_GROUPS - N_EXPERTS)))
        b_router = _pad_lanes(jnp.concatenate([moe_rg_b[layer], moe_re_b[layer].reshape(-1)]))
        h = _moe_ln(h, h_tiles, w_router, b_router, w_gate, w_up, w_down, layer,
                    ln2_g[layer].reshape(1, -1), ln2_b[layer].reshape(1, -1))
    return h.reshape(bsz, seq, d)
```

```python
import functools

import numpy as np
import jax
import jax.numpy as jnp
from jax import lax
from jax.experimental import pallas as pl
from jax.experimental.pallas import tpu as pltpu

F32 = jnp.float32
MXU_DTYPE = jnp.bfloat16
HIGHEST = lax.Precision.HIGHEST

D_MODEL = 1024
DEPTH = 4
ALPHA = (2 * DEPTH) ** 0.25
LN_EPS = 1e-5
CONV_DIM = D_MODEL // 2
CONV_WIDTH = 3
SSM_D_INNER = D_MODEL
SSM_HEAD_DIM = 64
SSM_HEADS = SSM_D_INNER // SSM_HEAD_DIM
SSM_GROUPS = 4
SSM_STATE = 64
SSM_CONV = 4
SSM_BC = SSM_GROUPS * SSM_STATE
SSM_CONV_DIM = SSM_D_INNER + 2 * SSM_BC
MLSTM_HEADS = 4
MLSTM_HEAD_DIM = D_MODEL // 8
MLSTM_W = MLSTM_HEADS * MLSTM_HEAD_DIM
FOX_HEADS = 8
FOX_HEAD_DIM = D_MODEL // 16
FOX_W = FOX_HEADS * FOX_HEAD_DIM
MOE_GROUPS = 4
EXPERTS_PER_GROUP = 8
N_EXPERTS = MOE_GROUPS * EXPERTS_PER_GROUP
TOP_K = 2
D_EXPERT = D_MODEL // 2

LANES = 128
SUBLANES = 8
VMEM_LIMIT_BYTES = 56 * 1024 * 1024

CHUNK = 128
SEQ_PAIR = 2
MLSTM_SEQS = 1
MLSTM_CHUNK = 512
TM_FRONT = 512
TM_EVEN_FRONT = 256
CONV_ROWS, CONV_LANES = 64, 512
TM_OUT = 1024
TM_ROUTER = 512
TQ_FOX = 256
MOE_ROWS = 512
FFN_COLS = 256
TM_COMBINE = 256
TM_DISPATCH = 2048

AB_PROJ = 4224
FOX_AUG = FOX_HEADS * LANES
CD_FK = 5 * 512
CD_FV = CD_FK + FOX_AUG
CD_GATES = CD_FV + FOX_W
CD_PROJ = CD_GATES + LANES
LOG2E = 1.4426950408889634
TK_FOX = 128
FOX_ACC_ROWS = FOX_HEAD_DIM + SUBLANES
FOX_VROWS = LANES
G_I, G_F, G_FOX, G_BCUM = 0, 4, 8, 16


def _params(*sem):
    return pltpu.CompilerParams(dimension_semantics=sem, vmem_limit_bytes=VMEM_LIMIT_BYTES)


def _softplus(x):
    return jnp.maximum(x, 0.0) + jnp.log(1.0 + jnp.exp(-jnp.abs(x)))


def _sigmoid(x):
    return 1.0 / (1.0 + jnp.exp(-x))


def _layer_norm_rows(v, g, b):
    mu = jnp.mean(v, axis=-1, keepdims=True)
    c = v - mu
    var = jnp.mean(c * c, axis=-1, keepdims=True)
    return c * lax.rsqrt(var + LN_EPS) * g + b


def _tril(n, block):
    i = np.arange(n)
    m = (i[:, None] >= i[None, :]) & (i[:, None] // block == i[None, :] // block)
    return jnp.asarray(m.astype(np.float32), MXU_DTYPE)


def _full(shape):
    return pl.BlockSpec(shape, lambda *_: (0,) * len(shape), pipeline_mode=pl.Buffered(1))


ROW_TILE = (D_MODEL // LANES, LANES)


def _tiles_to_rows(ref):
    n = ref.shape[0] // ROW_TILE[0]
    return jnp.concatenate([ref[pl.ds(s, n, stride=ROW_TILE[0]), :] for s in range(ROW_TILE[0])], axis=1)


def _rows_to_tiles(ref, val):
    n = val.shape[0]
    for s in range(ROW_TILE[0]):
        ref[pl.ds(s, n, stride=ROW_TILE[0]), :] = val[:, s * LANES:(s + 1) * LANES]


def _split3(x):
    narrow = lambda v: v.astype(MXU_DTYPE).astype(F32)
    x1 = narrow(x)
    x2 = narrow(x - x1)
    return x1, x2, narrow(x - x1 - x2)


def _cumsum_rows(tril, x):
    parts = jnp.dot(tril, jnp.concatenate(_split3(x), axis=1).astype(MXU_DTYPE), preferred_element_type=F32)
    return parts[:, 0:LANES] + parts[:, LANES:2 * LANES] + parts[:, 2 * LANES:3 * LANES]


def _even_front_kernel(h_ref, w_ref, ca_ref, cw_ref, cb_ref, dtb_ref, aneg_ref, tril_ref,
                       ya_ref, z_ref, xbc_ref, dt_ref, acs_ref, acst_ref,
                       proj_buf, ua_ext, xbc_ext, *, tiles_per_seq):
    tm = h_ref.shape[0]
    i = pl.program_id(0)
    cur = i % 2

    @pl.when(i == 0)
    def _():
        proj_buf[1] = jnp.zeros(proj_buf.shape[1:], F32)

    @pl.when((i == 0) | ((i - 1) % tiles_per_seq == 0))
    def _():
        ua_ext[0:SUBLANES, :] = jnp.zeros((SUBLANES, CONV_DIM), F32)
        xbc_ext[0:SUBLANES, :] = jnp.zeros((SUBLANES, SSM_CONV_DIM), F32)

    z0 = 3 * CONV_DIM
    x0 = z0 + SSM_D_INNER
    d0 = x0 + SSM_CONV_DIM

    x_in = h_ref[...].astype(MXU_DTYPE)
    proj = proj_buf.at[1 - cur]
    z_ref[...] = proj[:, z0:x0]

    n_blocks = tm // CONV_ROWS
    col_cuts = [AB_PROJ * g // n_blocks // LANES * LANES for g in range(n_blocks)] + [AB_PROJ]
    for r0 in range(0, tm, CONV_ROWS):
        c_lo, c_hi = col_cuts[r0 // CONV_ROWS], col_cuts[r0 // CONV_ROWS + 1]
        proj_buf[cur, :, c_lo:c_hi] = jnp.dot(x_in, w_ref[:, c_lo:c_hi], preferred_element_type=F32)
        rows = slice(r0, r0 + CONV_ROWS)
        ext_rows = slice(SUBLANES + r0, SUBLANES + r0 + CONV_ROWS)
        ua_ext[ext_rows, :] = proj[rows, CONV_DIM:2 * CONV_DIM] * proj[rows, 2 * CONV_DIM:3 * CONV_DIM]
        conv = None
        for k in range(CONV_WIDTH):
            tap = ca_ref[k:k + 1, :] * ua_ext[pl.ds(SUBLANES + r0 - (CONV_WIDTH - 1) + k, CONV_ROWS), :]
            conv = tap if conv is None else conv + tap
        ya_ref[rows, :] = (proj[rows, 0:CONV_DIM] * conv).astype(ya_ref.dtype)
        for c0 in range(0, SSM_CONV_DIM, CONV_LANES):
            cols = slice(c0, c0 + CONV_LANES)
            xbc_ext[ext_rows, cols] = proj[rows, x0 + c0:x0 + c0 + CONV_LANES]
            conv = cb_ref[:, cols]
            for k in range(SSM_CONV):
                conv = conv + cw_ref[k:k + 1, cols] * xbc_ext[pl.ds(SUBLANES + r0 - (SSM_CONV - 1) + k, CONV_ROWS), cols]
            xbc_ref[rows, cols] = conv * _sigmoid(conv)
    ua_ext[0:SUBLANES, :] = ua_ext[tm:tm + SUBLANES, :]
    xbc_ext[0:SUBLANES, :] = xbc_ext[tm:tm + SUBLANES, :]

    dt = _softplus(proj[:, d0:d0 + LANES] + dtb_ref[...])
    a = dt * aneg_ref[...]
    acs = _cumsum_rows(tril_ref[...], a)
    dt_ref[...] = dt
    acs_ref[...] = acs
    acst_ref[...] = acs.T


def _even_front(h, w_in, conv_a, conv_w, conv_b, dt_bias_row, aneg_row, bsz, seq):
    tok = bsz * seq
    tm = TM_EVEN_FRONT
    n_tiles = tok // tm
    row = lambda i: (jnp.maximum(i - 1, 0), 0)
    out_shapes = (
        jax.ShapeDtypeStruct((tok, CONV_DIM), MXU_DTYPE),
        jax.ShapeDtypeStruct((tok, SSM_D_INNER), F32),
        jax.ShapeDtypeStruct((tok, SSM_CONV_DIM), F32),
        jax.ShapeDtypeStruct((tok, LANES), F32),
        jax.ShapeDtypeStruct((tok, LANES), F32),
        jax.ShapeDtypeStruct((LANES, tok), F32),
    )
    return pl.pallas_call(
        functools.partial(_even_front_kernel, tiles_per_seq=seq // tm),
        out_shape=out_shapes,
        grid=(n_tiles + 1,),
        in_specs=[
            pl.BlockSpec((tm, D_MODEL), lambda i: (jnp.minimum(i, n_tiles - 1), 0)),
            _full((D_MODEL, AB_PROJ)),
            _full((CONV_WIDTH, CONV_DIM)),
            _full((SSM_CONV, SSM_CONV_DIM)),
            _full((1, SSM_CONV_DIM)),
            _full((1, LANES)),
            _full((1, LANES)),
            _full((tm, tm)),
        ],
        out_specs=(
            pl.BlockSpec((tm, CONV_DIM), row),
            pl.BlockSpec((tm, SSM_D_INNER), row),
            pl.BlockSpec((tm, SSM_CONV_DIM), row),
            pl.BlockSpec((tm, LANES), row),
            pl.BlockSpec((tm, LANES), row),
            pl.BlockSpec((LANES, tm), lambda i: (0, jnp.maximum(i - 1, 0))),
        ),
        scratch_shapes=[
            pltpu.VMEM((2, tm, AB_PROJ), F32),
            pltpu.VMEM((tm + SUBLANES, CONV_DIM), F32),
            pltpu.VMEM((tm + SUBLANES, SSM_CONV_DIM), F32),
        ],
        compiler_params=_params("arbitrary"),
        name="even_front",
    )(h, w_in, conv_a, conv_w, conv_b, dt_bias_row, aneg_row, _tril(tm, CHUNK))


def _bcast_heads(arr, n_heads, width):
    per = LANES // width
    length = arr.shape[0]
    lane = lax.broadcasted_iota(jnp.int32, (length, LANES), 1)
    outs = []
    for j in range(n_heads // per):
        v = jnp.broadcast_to(arr[:, j * per:j * per + 1], (length, LANES))
        for r in range(1, per):
            v = jnp.where(lane >= r * width, jnp.broadcast_to(arr[:, j * per + r:j * per + r + 1], (length, LANES)), v)
        outs.append(v)
    return jnp.concatenate(outs, axis=1)


def _expand_heads(arr, e3_ref):
    return jnp.dot(jnp.concatenate(_split3(arr), axis=1).astype(MXU_DTYPE), e3_ref[...],
                   preferred_element_type=F32)


def _ssd_kernel(*refs):
    nb = SEQ_PAIR
    xbc_ref, dt_ref, acs_ref, z_ref = refs[:4]
    acst_refs = refs[4:4 + nb]
    dskip_ref, nw_ref, e3_ref, y_ref, state = refs[4 + nb:]
    L = CHUNK
    P = SSM_HEAD_DIM
    R = SSM_HEADS // SSM_GROUPS
    GW = R * P

    @pl.when(pl.program_id(1) == 0)
    def _():
        state[...] = jnp.zeros(state.shape, F32)

    row = lax.broadcasted_iota(jnp.int32, (L, L), 0)
    col = lax.broadcasted_iota(jnp.int32, (L, L), 1)
    causal = row >= col
    lane_g = lax.broadcasted_iota(jnp.int32, (L, GW), 1)

    per_row = []
    for j in range(nb):
        acs = acs_ref[j]
        per_row += [dt_ref[j], jnp.exp(acs[L - 1:L, :] - acs), jnp.exp(acs)]
    expanded = _expand_heads(jnp.concatenate(per_row, axis=0), e3_ref)

    st_old = [state[c] for c in range(nb * SSM_GROUPS)]
    st_new = []
    for j in range(nb):
        xs = xbc_ref[j, :, 0:SSM_D_INNER]
        bm = xbc_ref[j, :, SSM_D_INNER:SSM_D_INNER + SSM_BC]
        cm = xbc_ref[j, :, SSM_D_INNER + SSM_BC:SSM_CONV_DIM]
        acs = acs_ref[j]
        acst = acst_refs[j][...]
        a_last = acs[L - 1:L, :]
        dtx = expanded[(3 * j) * L:(3 * j + 1) * L, :]
        decx = expanded[(3 * j + 1) * L:(3 * j + 2) * L, :]
        expx = expanded[(3 * j + 2) * L:(3 * j + 3) * L, :]
        xdt = xs * dtx
        xdec = (xdt * decx).astype(MXU_DTYPE)
        xdt_m = xdt.astype(MXU_DTYPE)
        chunk_decay = jnp.exp(jnp.broadcast_to(a_last, (SUBLANES, LANES)))
        cdx = _bcast_heads(chunk_decay, SSM_HEADS, P)[0:1, :]

        bm_t = bm.T.astype(MXU_DTYPE)
        cm_m = cm.astype(MXU_DTYPE)
        bm_m = bm.astype(MXU_DTYPE)

        ys = []
        for g in range(SSM_GROUPS):
            n0 = g * SSM_STATE
            c_g = cm_m[:, n0:n0 + SSM_STATE]
            cb = lax.dot_general(c_g, bm_m[:, n0:n0 + SSM_STATE], (((1,), (1,)), ((), ())),
                                 preferred_element_type=F32)
            ms = []
            for r in range(R):
                hd = g * R + r
                seg = jnp.exp(jnp.where(causal, acs[:, hd:hd + 1] - acst[hd:hd + 1, :], -jnp.inf))
                ms.append((cb * seg).astype(MXU_DTYPE))
            big = jnp.dot(jnp.concatenate(ms, axis=0), xdt_m[:, g * GW:(g + 1) * GW],
                          preferred_element_type=F32)
            y_diag = big[0:L, :]
            for r in range(1, R):
                y_diag = jnp.where(lane_g >= r * P, big[r * L:(r + 1) * L, :], y_diag)
            st = st_old[j * SSM_GROUPS + g]
            y_off = jnp.dot(c_g, st.astype(MXU_DTYPE), preferred_element_type=F32)
            new = jnp.dot(bm_t[n0:n0 + SSM_STATE, :], xdec[:, g * GW:(g + 1) * GW],
                          preferred_element_type=F32)
            st_new.append(st * cdx[:, g * GW:(g + 1) * GW] + new)
            ys.append(y_diag + y_off * expx[:, g * GW:(g + 1) * GW])
        y = jnp.concatenate(ys, axis=1) + dskip_ref[...] * xs
        z = z_ref[j]
        u = y * (z * _sigmoid(z))
        y = u * lax.rsqrt(jnp.mean(u * u, axis=-1, keepdims=True) + LN_EPS) * nw_ref[...]
        y_ref[j] = y.astype(y_ref.dtype)

    for c in range(nb * SSM_GROUPS):
        state[c] = st_new[c]


def _ssd(xbc, dt, acs, acst, z, dskip_row, normw_row, bsz, seq):
    nb = SEQ_PAIR
    nc = seq // CHUNK
    per_seq = lambda a: a.reshape(bsz, seq, a.shape[-1])
    blk = lambda width: pl.BlockSpec((nb, CHUNK, width), lambda g, c: (g, c, 0))
    acst_specs = [pl.BlockSpec((LANES, CHUNK), lambda g, c, j=j: (0, (g * nb + j) * nc + c)) for j in range(nb)]
    expand = np.zeros((LANES, SSM_D_INNER), np.float32)
    for hd in range(SSM_HEADS):
        expand[hd, hd * SSM_HEAD_DIM:(hd + 1) * SSM_HEAD_DIM] = 1.0
    e3 = jnp.asarray(np.concatenate([expand] * 3, axis=0), MXU_DTYPE)
    y = pl.pallas_call(
        _ssd_kernel,
        out_shape=jax.ShapeDtypeStruct((bsz, seq, SSM_D_INNER), MXU_DTYPE),
        grid=(bsz // nb, nc),
        in_specs=[blk(SSM_CONV_DIM), blk(LANES), blk(LANES), blk(SSM_D_INNER)] + acst_specs + [
            _full((1, SSM_D_INNER)),
            _full((1, SSM_D_INNER)),
            _full((3 * LANES, SSM_D_INNER)),
        ],
        out_specs=blk(SSM_D_INNER),
        scratch_shapes=[pltpu.VMEM((nb * SSM_GROUPS, SSM_STATE, SSM_D_INNER // SSM_GROUPS), F32)],
        compiler_params=_params("arbitrary", "arbitrary"),
        name="ssd_scan",
    )(per_seq(xbc), per_seq(dt), per_seq(acs), per_seq(z), *([acst] * nb), dskip_row, normw_row, e3)
    return y.reshape(bsz * seq, SSM_D_INNER)


def _outproj_ln_kernel(*refs, widths):
    n = len(widths)
    parts = refs[:n]
    h_ref, w_ref, g_ref, b_ref, o_ref, ot_ref = refs[n:]
    acc = ALPHA * h_ref[...]
    off = 0
    for p, wd in zip(parts, widths):
        acc = acc + jnp.dot(p[...].astype(MXU_DTYPE), w_ref[off:off + wd, :], preferred_element_type=F32)
        off += wd
    out = _layer_norm_rows(acc, g_ref[...], b_ref[...])
    o_ref[...] = out
    _rows_to_tiles(ot_ref, out)


def _outproj_ln(parts, h, w_out, g_row, b_row):
    tok = h.shape[0]
    tm = TM_OUT
    widths = tuple(p.shape[1] for p in parts)
    row = lambda i: (i, 0)
    return pl.pallas_call(
        functools.partial(_outproj_ln_kernel, widths=widths),
        out_shape=(jax.ShapeDtypeStruct((tok, D_MODEL), F32),
                   jax.ShapeDtypeStruct((tok * ROW_TILE[0], LANES), F32)),
        grid=(tok // tm,),
        in_specs=[pl.BlockSpec((tm, wd), row) for wd in widths] + [
            pl.BlockSpec((tm, D_MODEL), row),
            _full((sum(widths), D_MODEL)),
            _full((1, D_MODEL)),
            _full((1, D_MODEL)),
        ],
        out_specs=(pl.BlockSpec((tm, D_MODEL), row), pl.BlockSpec((tm * ROW_TILE[0], LANES), row)),
        compiler_params=_params("arbitrary"),
        name="outproj_ln",
    )(*parts, h, w_out, g_row, b_row)


def _odd_front_kernel(h_ref, w_ref, gb_ref, tril_ref,
                      q_ref, k_ref, v_ref, o_ref, fq_ref, fk_ref, fvt_ref, g_ref, gt_ref, carry):
    tm = h_ref.shape[0]
    x_in = h_ref[...].astype(MXU_DTYPE)
    proj_cols = lambda lo, hi: jnp.dot(x_in, w_ref[:, lo:hi], preferred_element_type=F32)

    @pl.when(pl.program_id(1) == 0)
    def _():
        carry[...] = jnp.zeros(carry.shape, F32)

    raw = proj_cols(CD_GATES, CD_GATES + LANES) + gb_ref[...]
    lane = lax.broadcasted_iota(jnp.int32, (tm, LANES), 1)
    g = jnp.where(lane < G_F, raw, -_softplus(-raw))
    prev = carry[0:1, :]
    glob = _cumsum_rows(tril_ref[...], g) + prev
    before = []
    for c in range(tm // MLSTM_CHUNK):
        before.append(jnp.broadcast_to(prev, (MLSTM_CHUNK, LANES)))
        prev = glob[(c + 1) * MLSTM_CHUNK - 1:(c + 1) * MLSTM_CHUNK, :]
    carry[...] = jnp.broadcast_to(prev, carry.shape)
    local = glob - jnp.concatenate(before, axis=0)
    out = jnp.where((lane >= G_FOX) & (lane < G_BCUM), glob, jnp.where(lane >= G_BCUM, local, g))
    g_ref[...] = out
    gt_ref[...] = out.T

    q_ref[...] = proj_cols(0, 512)
    k_ref[...] = proj_cols(512, 1024) * (MLSTM_HEAD_DIM ** -0.5)
    v_ref[...] = proj_cols(1024, 1536)
    o_ref[...] = _sigmoid(proj_cols(1536, 2048))
    fq_ref[...] = proj_cols(2048, 2560) * (FOX_HEAD_DIM ** -0.5 * LOG2E)

    is_bias = (lane >= FOX_HEAD_DIM) & (lane < FOX_HEAD_DIM + 3)
    fk = proj_cols(CD_FK, CD_FK + FOX_AUG)
    for hd in range(FOX_HEADS):
        c1, c2, c3 = _split3(out[:, G_FOX + hd:G_FOX + hd + 1] * (-LOG2E))
        bias = jnp.where(lane == FOX_HEAD_DIM, c1, jnp.where(lane == FOX_HEAD_DIM + 1, c2, c3))
        k_h = fk[:, hd * LANES:(hd + 1) * LANES]
        fk_ref[:, hd * LANES:(hd + 1) * LANES] = jnp.where(is_bias, bias, k_h).astype(MXU_DTYPE)
    v_t = proj_cols(CD_FV, CD_FV + FOX_W).T
    extra = jnp.where(lax.broadcasted_iota(jnp.int32, (FOX_VROWS - FOX_HEAD_DIM, tm), 0) == 0, 1.0, 0.0)
    fvt_ref[...] = jnp.concatenate(
        [blk for hd in range(FOX_HEADS) for blk in (v_t[hd * FOX_HEAD_DIM:(hd + 1) * FOX_HEAD_DIM, :], extra)],
        axis=0).astype(MXU_DTYPE)


def _odd_front(h, w_in, gate_bias_row, bsz, seq):
    tok = bsz * seq
    tm = TM_FRONT
    ns = seq // tm
    row = lambda b, s: (b * ns + s, 0)
    col = lambda b, s: (0, b * ns + s)
    wide = jax.ShapeDtypeStruct((tok, 512), F32)
    return pl.pallas_call(
        _odd_front_kernel,
        out_shape=(wide,) * 5 + (jax.ShapeDtypeStruct((tok, FOX_AUG), MXU_DTYPE),
                                 jax.ShapeDtypeStruct((FOX_HEADS * FOX_VROWS, tok), MXU_DTYPE),
                                 jax.ShapeDtypeStruct((tok, LANES), F32),
                                 jax.ShapeDtypeStruct((LANES, tok), F32)),
        grid=(bsz, ns),
        in_specs=[
            pl.BlockSpec((tm, D_MODEL), row),
            _full((D_MODEL, CD_PROJ)),
            _full((1, LANES)),
            _full((tm, tm)),
        ],
        out_specs=(pl.BlockSpec((tm, 512), row),) * 5 + (
            pl.BlockSpec((tm, FOX_AUG), row),
            pl.BlockSpec((FOX_HEADS * FOX_VROWS, tm), col),
            pl.BlockSpec((tm, LANES), row),
            pl.BlockSpec((LANES, tm), col),
        ),
        scratch_shapes=[pltpu.VMEM((SUBLANES, LANES), F32)],
        compiler_params=_params("arbitrary", "arbitrary"),
        name="odd_front",
    )(h, w_in, gate_bias_row, _tril(tm, tm))


def _mlstm_kernel(*refs):
    nb = MLSTM_SEQS
    q_ref, k_ref, v_ref, o_ref, g_ref = refs[:5]
    gt_refs = refs[5:5 + nb]
    nw_ref, y_ref, c_state, m_state = refs[5 + nb:]
    L = MLSTM_CHUNK
    DH = MLSTM_HEAD_DIM

    @pl.when(pl.program_id(1) == 0)
    def _():
        c_state[...] = jnp.zeros(c_state.shape, F32)
        m_state[...] = jnp.zeros(m_state.shape, F32)

    row = lax.broadcasted_iota(jnp.int32, (L, L), 0)
    col = lax.broadcasted_iota(jnp.int32, (L, L), 1)
    causal = row >= col
    ones_col = jnp.where(lax.broadcasted_iota(jnp.int32, (L, DH), 1) == 0, 1.0, 0.0)

    for j in range(nb):
        gates = g_ref[j]
        gates_t = gt_refs[j][...]
        for hd in range(MLSTM_HEADS):
            st = j * MLSTM_HEADS + hd
            sl = slice(hd * DH, (hd + 1) * DH)
            q = q_ref[j, :, sl].astype(MXU_DTYPE)
            k = k_ref[j, :, sl]
            v_ext = jnp.concatenate([v_ref[j, :, sl], ones_col], axis=1).astype(MXU_DTYPE)
            b_col = gates[:, G_BCUM + hd:G_BCUM + hd + 1]
            i_col = gates[:, G_I + hd:G_I + hd + 1]
            b_row = gates_t[G_BCUM + hd:G_BCUM + hd + 1, :]
            i_row = gates_t[G_I + hd:G_I + hd + 1, :]
            m_prev = m_state[st:st + 1, 0:1]
            c_ext = c_state[st]

            d_mat = jnp.where(causal, b_col - b_row + i_row, -jnp.inf)
            inter = b_col + m_prev
            m_t = jnp.maximum(jnp.max(d_mat, axis=-1, keepdims=True), inter)
            s_qk = lax.dot_general(q, k.astype(MXU_DTYPE), (((1,), (1,)), ((), ())), preferred_element_type=F32)
            w_qk = s_qk * jnp.exp(d_mat - m_t)
            s_inter = jnp.exp(inter - m_t)
            num_ext = (jnp.dot(w_qk.astype(MXU_DTYPE), v_ext, preferred_element_type=F32)
                       + s_inter * jnp.dot(q, c_ext.astype(MXU_DTYPE), preferred_element_type=F32))
            den = num_ext[:, DH:DH + 1]
            hval = num_ext[:, 0:DH] / jnp.maximum(jnp.abs(den), jnp.exp(-m_t))

            b_last = b_col[L - 1:L, :]
            g_log = b_last - b_col + i_col
            m_new = jnp.maximum(b_last + m_prev, jnp.max(g_log, axis=0, keepdims=True))
            w_k = jnp.exp(g_log - m_new)
            decay = jnp.exp(b_last + m_prev - m_new)
            kw_t = (k * w_k).T.astype(MXU_DTYPE)
            c_state[st] = decay * c_ext + jnp.dot(kw_t, v_ext, preferred_element_type=F32)
            m_state[st:st + 1, :] = jnp.broadcast_to(m_new, (1, LANES))

            mu = jnp.mean(hval, axis=-1, keepdims=True)
            cen = hval - mu
            var = jnp.mean(cen * cen, axis=-1, keepdims=True)
            y = o_ref[j, :, sl] * (cen * lax.rsqrt(var + LN_EPS) * nw_ref[:, sl])
            y_ref[j, :, sl] = y.astype(y_ref.dtype)


def _mlstm(q, k, v, o, gates, gates_t, hnorm_row, bsz, seq):
    nb = MLSTM_SEQS
    nc = seq // MLSTM_CHUNK
    per_seq = lambda a: a.reshape(bsz, seq, a.shape[-1])
    blk = lambda width: pl.BlockSpec((nb, MLSTM_CHUNK, width), lambda g, c: (g, c, 0))
    gt_specs = [pl.BlockSpec((LANES, MLSTM_CHUNK), lambda g, c, j=j: (0, (g * nb + j) * nc + c))
                for j in range(nb)]
    y = pl.pallas_call(
        _mlstm_kernel,
        out_shape=jax.ShapeDtypeStruct((bsz, seq, MLSTM_W), MXU_DTYPE),
        grid=(bsz // nb, nc),
        in_specs=[blk(MLSTM_W)] * 4 + [blk(LANES)] + gt_specs + [_full((1, MLSTM_W))],
        out_specs=blk(MLSTM_W),
        scratch_shapes=[pltpu.VMEM((nb * MLSTM_HEADS, MLSTM_HEAD_DIM, 2 * MLSTM_HEAD_DIM), F32),
                        pltpu.VMEM((nb * MLSTM_HEADS, LANES), F32)],
        compiler_params=_params("arbitrary", "arbitrary"),
        name="mlstm_scan",
    )(per_seq(q), per_seq(k), per_seq(v), per_seq(o), per_seq(gates), *([gates_t] * nb), hnorm_row)
    return y.reshape(bsz * seq, MLSTM_W)


def _fox_kernel(q_ref, k_ref, vt_ref, gt_ref, y_ref, *scratch):
    acc_refs = scratch[:FOX_HEADS]
    qa_ref = scratch[FOX_HEADS]
    tq = q_ref.shape[0]
    tk = TK_FOX
    dh = FOX_HEAD_DIM
    qi = pl.program_id(1)
    q_t = q_ref[...].T
    bias_rows = jnp.where(lax.broadcasted_iota(jnp.int32, (LANES - dh, tq), 0) < 3, 1.0, 0.0)
    for hd in range(FOX_HEADS):
        qa_ref[hd] = jnp.concatenate([q_t[hd * dh:(hd + 1) * dh, :], bias_rows], axis=0).astype(MXU_DTYPE)
        acc_refs[hd][...] = jnp.zeros(acc_refs[hd].shape, F32)
    cq = gt_ref[G_FOX:G_FOX + FOX_HEADS, :] * LOG2E
    key_pos = lax.broadcasted_iota(jnp.int32, (tk, tq), 0)
    qry_pos = lax.broadcasted_iota(jnp.int32, (tk, tq), 1) + qi * tq
    n_full = qi * (tq // tk)

    def block(j, ms, masked):
        k0 = pl.multiple_of(j * tk, tk)
        out = []
        for hd in range(FOX_HEADS):
            hs = slice(hd * LANES, (hd + 1) * LANES)
            s = jnp.dot(k_ref[pl.ds(k0, tk), hs], qa_ref[hd], preferred_element_type=F32)
            if masked:
                s = jnp.where(key_pos + k0 <= qry_pos, s, -jnp.inf)
            cq_h = cq[hd:hd + 1, :]
            m_new = jnp.maximum(ms[hd], jnp.max(s, axis=0, keepdims=True) + cq_h)
            p = jnp.exp2(s - (m_new - cq_h))
            pv = jnp.dot(vt_ref[hd * FOX_VROWS:(hd + 1) * FOX_VROWS, pl.ds(k0, tk)], p.astype(MXU_DTYPE),
                         preferred_element_type=F32)
            acc_refs[hd][...] = jnp.exp2(ms[hd] - m_new) * acc_refs[hd][...] + pv[0:FOX_ACC_ROWS, :]
            out.append(m_new)
        return tuple(out)

    ms = tuple(jnp.full((1, tq), -jnp.inf, F32) for _ in range(FOX_HEADS))
    ms = lax.fori_loop(0, n_full, functools.partial(block, masked=False), ms)
    for d in range(tq // tk):
        ms = block(n_full + d, ms, masked=True)
    outs = []
    for hd in range(FOX_HEADS):
        acc = acc_refs[hd][...]
        outs.append(acc[0:dh, :] / acc[dh:dh + 1, :])
    y_ref[...] = jnp.concatenate(outs, axis=0).T.astype(y_ref.dtype)


def _fox(fq, fk_aug, fvt_aug, gates_t, bsz, seq):
    tok = bsz * seq
    tq = TQ_FOX
    nq = seq // tq
    return pl.pallas_call(
        _fox_kernel,
        out_shape=jax.ShapeDtypeStruct((tok, FOX_W), MXU_DTYPE),
        grid=(bsz, nq),
        in_specs=[
            pl.BlockSpec((tq, FOX_W), lambda b, i: (b * nq + i, 0)),
            pl.BlockSpec((seq, FOX_AUG), lambda b, i: (b, 0)),
            pl.BlockSpec((FOX_HEADS * FOX_VROWS, seq), lambda b, i: (0, b)),
            pl.BlockSpec((LANES, tq), lambda b, i: (0, b * nq + i)),
        ],
        out_specs=pl.BlockSpec((tq, FOX_W), lambda b, i: (b * nq + i, 0)),
        scratch_shapes=[pltpu.VMEM((FOX_ACC_ROWS, tq), F32)] * FOX_HEADS
                       + [pltpu.VMEM((FOX_HEADS, LANES, tq), MXU_DTYPE)],
        compiler_params=_params("arbitrary", "arbitrary"),
        name="fox_attention",
    )(fq, fk_aug, fvt_aug, gates_t)


def _router_kernel(h_ref, w_ref, b_ref, stril_ref, id_ref, gate_ref, cnt_ref, carry):
    tm = h_ref.shape[0]

    @pl.when(pl.program_id(0) == 0)
    def _():
        carry[...] = jnp.zeros(carry.shape, F32)

    h = h_ref[...]
    h_hi = h.astype(MXU_DTYPE)
    h_lo = (h - h_hi.astype(F32)).astype(MXU_DTYPE)
    both = jnp.dot(h_hi, w_ref[...], preferred_element_type=F32)
    logits = (both[:, 0:LANES] + both[:, LANES:2 * LANES]
              + jnp.dot(h_lo, w_ref[:, 0:LANES], preferred_element_type=F32) + b_ref[...])
    lane = lax.broadcasted_iota(jnp.int32, (tm, LANES), 1).astype(F32)
    neg = -jnp.inf
    first = lambda hit: jnp.min(jnp.where(hit, lane, float(LANES)), axis=-1, keepdims=True)
    gl = jnp.where(lane < MOE_GROUPS, logits, neg)
    g_max = jnp.max(gl, axis=-1, keepdims=True)
    g_idx = first(gl == g_max)
    p_group = 1.0 / jnp.sum(jnp.exp(gl - g_max), axis=-1, keepdims=True)
    e_lo = MOE_GROUPS + g_idx * EXPERTS_PER_GROUP
    el = jnp.where((lane >= e_lo) & (lane < e_lo + EXPERTS_PER_GROUP), logits, neg)
    v1 = jnp.max(el, axis=-1, keepdims=True)
    i1 = first(el == v1)
    el2 = jnp.where(lane == i1, neg, el)
    v2 = jnp.max(el2, axis=-1, keepdims=True)
    i2 = first(el2 == v2)
    t = jnp.exp(v2 - v1)
    w1 = 1.0 / (1.0 + t)
    gate_ref[...] = jnp.where(lane == 0.0, p_group * w1, jnp.where(lane == 1.0, p_group * (t * w1), 0.0))

    hit1 = lane == i1
    hit2 = lane == i2
    sent = jnp.where(hit1 | hit2, 1.0, 0.0)
    before = jnp.dot(stril_ref[...], sent.astype(jnp.bfloat16), preferred_element_type=F32) + carry[0:1, :]
    r1 = jnp.sum(jnp.where(hit1, before, 0.0), axis=-1, keepdims=True)
    r2 = jnp.sum(jnp.where(hit2, before, 0.0), axis=-1, keepdims=True)
    total = before[tm - 1:tm, :] + sent[tm - 1:tm, :]
    carry[...] = jnp.broadcast_to(total, carry.shape)
    cnt_ref[...] = jnp.broadcast_to(total, cnt_ref.shape).astype(jnp.int32)
    ids = jnp.where(lane == 0.0, i1 - MOE_GROUPS, jnp.where(lane == 1.0, i2 - MOE_GROUPS,
                    jnp.where(lane == 2.0, r1, jnp.where(lane == 3.0, r2, 0.0))))
    id_ref[...] = ids.T[0:SUBLANES, :].astype(jnp.int32)


def _router(h, w_router, b_router):
    tok = h.shape[0]
    tm = TM_ROUTER
    row = lambda i: (i, 0)
    idx = np.arange(tm)
    stril = jnp.asarray((idx[:, None] > idx[None, :]).astype(np.float32), jnp.bfloat16)
    w_hi = w_router.astype(MXU_DTYPE)
    w_hi_lo = (w_hi, (w_router - w_hi.astype(F32)).astype(MXU_DTYPE))
    return pl.pallas_call(
        _router_kernel,
        out_shape=(jax.ShapeDtypeStruct((SUBLANES, tok), jnp.int32), jax.ShapeDtypeStruct((tok, LANES), F32),
                   jax.ShapeDtypeStruct((SUBLANES, LANES), jnp.int32)),
        grid=(tok // tm,),
        in_specs=[pl.BlockSpec((tm, D_MODEL), row), _full((D_MODEL, 2 * LANES)), _full((1, LANES)),
                  _full((tm, tm))],
        out_specs=(pl.BlockSpec((SUBLANES, tm), lambda i: (0, i)), pl.BlockSpec((tm, LANES), row),
                   _full((SUBLANES, LANES))),
        scratch_shapes=[pltpu.VMEM((SUBLANES, LANES), F32)],
        compiler_params=_params("arbitrary"),
        name="moe_router",
    )(h, jnp.concatenate([w_hi, w_hi_lo[1]], axis=1), b_router, stril)


def _pad_pieces(n):
    return tuple(1 << b for b in reversed(range((n - 1).bit_length())))


def _dispatch_kernel(e0_ref, e1_ref, r0_ref, r1_ref, start_ref, cnt_ref, nused_ref, h_ref, xs_hbm, zeros, sem,
                     pad_sem):
    tm = h_ref.shape[0]
    i = pl.program_id(0)
    base = i * tm

    def issue(r, c):
        t = base + r
        pltpu.make_async_copy(h_ref.at[r], xs_hbm.at[start_ref[e0_ref[t]] + r0_ref[t]], sem).start(priority=0)
        pltpu.make_async_copy(h_ref.at[r], xs_hbm.at[start_ref[e1_ref[t]] + r1_ref[t]], sem).start(priority=1)
        return c

    lax.fori_loop(0, tm, issue, 0, unroll=8)

    def pad_copies(e, fn):
        cnt = cnt_ref[e]
        n_pad = (MOE_ROWS - cnt % MOE_ROWS) % MOE_ROWS
        first = start_ref[e] + cnt
        for piece in _pad_pieces(MOE_ROWS):
            @pl.when((n_pad & piece) != 0)
            def _(piece=piece):
                off = first + (n_pad & ~(2 * piece - 1))
                fn(pltpu.make_async_copy(zeros.at[pl.ds(0, piece)], xs_hbm.at[pl.ds(off, piece)], pad_sem))

    def tail_copies(blk, fn):
        for part in range(MOE_ROWS // zeros.shape[0]):
            off = blk * MOE_ROWS + part * zeros.shape[0]
            fn(pltpu.make_async_copy(zeros, xs_hbm.at[pl.ds(off, zeros.shape[0])], pad_sem))

    @pl.when(i == 0)
    def _():
        zeros[...] = jnp.zeros(zeros.shape, F32)
        n_blocks = xs_hbm.shape[0] // MOE_ROWS

        def start(e, c):
            pad_copies(e, lambda cp: cp.start())
            return c

        def wait(e, c):
            pad_copies(e, lambda cp: cp.wait())
            return c

        def tail_start(blk, c):
            tail_copies(blk, lambda cp: cp.start())
            return c

        def tail_wait(blk, c):
            tail_copies(blk, lambda cp: cp.wait())
            return c

        lax.fori_loop(0, N_EXPERTS, start, 0)
        lax.fori_loop(nused_ref[0], n_blocks, tail_start, 0)
        lax.fori_loop(0, N_EXPERTS, wait, 0)
        lax.fori_loop(nused_ref[0], n_blocks, tail_wait, 0)

    for _ in range(TOP_K):
        pltpu.make_async_copy(h_ref, xs_hbm.at[pl.ds(0, tm)], sem).wait()


def _dispatch(h_tiles, e0, e1, r0, r1, pad_start, counts, n_used, n_slots):
    tok = h_tiles.shape[0]
    tm = TM_DISPATCH
    assert tok % tm == 0, (tok, tm)
    grid_spec = pltpu.PrefetchScalarGridSpec(
        num_scalar_prefetch=7,
        grid=(tok // tm,),
        in_specs=[pl.BlockSpec((tm,) + ROW_TILE, lambda i, *_: (i, 0, 0))],
        out_specs=pl.BlockSpec(memory_space=pl.ANY),
        scratch_shapes=[pltpu.VMEM((MOE_ROWS // 2,) + ROW_TILE, F32),
                        pltpu.SemaphoreType.DMA(()), pltpu.SemaphoreType.DMA(())],
    )
    return pl.pallas_call(
        _dispatch_kernel,
        out_shape=jax.ShapeDtypeStruct((n_slots,) + ROW_TILE, F32),
        grid_spec=grid_spec,
        compiler_params=_params("arbitrary"),
        name="moe_dispatch",
    )(e0, e1, r0, r1, pad_start, counts, n_used, h_tiles)


def _expert_kernel(be_ref, nused_ref, next_ref, slot_ref, x_ref, wg_hbm, wu_hbm, wd_hbm, out_ref,
                   wg_s, wu_s, wd_s, wg_buf, wu_buf, wd_buf, sem, *, layer):
    i = pl.program_id(0)

    def weight_copies(e, slot):
        w = layer * N_EXPERTS + e
        return (pltpu.make_async_copy(wg_hbm.at[w], wg_buf.at[slot], sem.at[slot]),
                pltpu.make_async_copy(wu_hbm.at[w], wu_buf.at[slot], sem.at[slot]),
                pltpu.make_async_copy(wd_hbm.at[w], wd_buf.at[slot], sem.at[slot]))

    @pl.when(i == 0)
    def _():
        for cp in weight_copies(be_ref[0], slot_ref[be_ref[0]]):
            cp.start()

    @pl.when(i < nused_ref[0])
    def _():
        @pl.when((i == 0) | (be_ref[i] != be_ref[jnp.maximum(i - 1, 0)]))
        def _():
            e = be_ref[i]
            slot = slot_ref[e]
            for cp in weight_copies(e, slot):
                cp.wait()

            @pl.when(next_ref[e] >= 0)
            def _():
                for cp in weight_copies(next_ref[e], 1 - slot):
                    cp.start()

            wg_s[...] = wg_buf[slot].astype(MXU_DTYPE)
            wu_s[...] = wu_buf[slot].astype(MXU_DTYPE)
            wd_s[...] = wd_buf[slot].astype(MXU_DTYPE)

        x = _tiles_to_rows(x_ref).astype(MXU_DTYPE)
        n = x.shape[0]
        hidden = []
        for c in range(0, D_EXPERT, FFN_COLS):
            gate = jnp.dot(x, wg_s[:, c:c + FFN_COLS], preferred_element_type=F32)
            up = jnp.dot(x, wu_s[:, c:c + FFN_COLS], preferred_element_type=F32)
            hidden.append((gate * _sigmoid(gate) * up).astype(MXU_DTYPE))
        hidden = jnp.concatenate(hidden, axis=1)
        for c in range(0, D_MODEL, FFN_COLS):
            out = jnp.dot(hidden, wd_s[:, c:c + FFN_COLS], preferred_element_type=F32)
            for s in range(FFN_COLS // LANES):
                out_ref[pl.ds(c // LANES + s, n, stride=ROW_TILE[0]), :] = out[:, s * LANES:(s + 1) * LANES]

    @pl.when(i >= nused_ref[0])
    def _():
        out_ref[...] = jnp.zeros(out_ref.shape, F32)


def _expert_ffn(xs_tiles, block_expert, n_used, counts, w_gate, w_up, w_down, layer):
    n_slots = xs_tiles.shape[0]
    n_blocks = block_expert.shape[0]
    rows = MOE_ROWS * ROW_TILE[0]
    experts = jnp.arange(N_EXPERTS, dtype=jnp.int32)
    nonempty = counts > 0
    later = (experts[None, :] > experts[:, None]) & nonempty[None, :]
    next_expert = jnp.min(jnp.where(later, experts[None, :], N_EXPERTS), axis=1)
    next_expert = jnp.where(next_expert < N_EXPERTS, next_expert, -1).astype(jnp.int32)
    buf_half = ((jnp.cumsum(nonempty.astype(jnp.int32)) - 1) % 2).astype(jnp.int32)
    blk = lambda i, be, nu, nx, sl: (jnp.maximum(jnp.minimum(i, nu[0] - 1), 0), 0)
    grid_spec = pltpu.PrefetchScalarGridSpec(
        num_scalar_prefetch=4,
        grid=(n_blocks,),
        in_specs=[
            pl.BlockSpec((rows, LANES), blk),
            pl.BlockSpec(memory_space=pl.ANY),
            pl.BlockSpec(memory_space=pl.ANY),
            pl.BlockSpec(memory_space=pl.ANY),
        ],
        out_specs=pl.BlockSpec((rows, LANES), lambda i, be, nu, nx, sl: (i, 0)),
        scratch_shapes=[pltpu.VMEM((D_MODEL, D_EXPERT), MXU_DTYPE), pltpu.VMEM((D_MODEL, D_EXPERT), MXU_DTYPE),
                        pltpu.VMEM((D_EXPERT, D_MODEL), MXU_DTYPE),
                        pltpu.VMEM((2, D_MODEL, D_EXPERT), F32), pltpu.VMEM((2, D_MODEL, D_EXPERT), F32),
                        pltpu.VMEM((2, D_EXPERT, D_MODEL), F32), pltpu.SemaphoreType.DMA((2,))],
    )
    out = pl.pallas_call(
        functools.partial(_expert_kernel, layer=layer),
        out_shape=jax.ShapeDtypeStruct((n_slots * ROW_TILE[0], LANES), F32),
        grid_spec=grid_spec,
        compiler_params=_params("arbitrary"),
        name="moe_experts",
    )(block_expert, n_used, next_expert, buf_half, xs_tiles.reshape(n_slots * ROW_TILE[0], LANES),
      w_gate, w_up, w_down)
    return out.reshape((n_slots,) + ROW_TILE)


def _combine_ln_kernel(e0_ref, e1_ref, r0_ref, r1_ref, start_ref, rows_hbm, h_ref, gate_ref, g_ref, b_ref, o_ref,
                       buf, sem):
    tm = h_ref.shape[0]
    i = pl.program_id(0)

    def gather(tile, p):
        base = tile * tm

        def issue(r, c):
            t = base + r
            dst = pl.ds(pl.multiple_of(r * ROW_TILE[0], ROW_TILE[0]), ROW_TILE[0])
            pltpu.make_async_copy(rows_hbm.at[start_ref[e0_ref[t]] + r0_ref[t]], buf.at[p, 0, dst],
                                  sem.at[p]).start(priority=0)
            pltpu.make_async_copy(rows_hbm.at[start_ref[e1_ref[t]] + r1_ref[t]], buf.at[p, 1, dst],
                                  sem.at[p]).start(priority=1)
            return c

        lax.fori_loop(0, tm, issue, 0, unroll=8)

    @pl.when(i == 0)
    def _():
        gather(0, 0)

    for p in range(2):
        @pl.when(i % 2 == p)
        def _(p=p):
            @pl.when(i + 1 < pl.num_programs(0))
            def _():
                gather(i + 1, 1 - p)

            for k in range(TOP_K):
                pltpu.make_async_copy(buf.at[1 - p, k], buf.at[p, k], sem.at[p]).wait()
            gate = gate_ref[...]
            acc = (ALPHA * h_ref[...] + gate[:, 0:1] * _tiles_to_rows(buf.at[p, 0])
                   + gate[:, 1:2] * _tiles_to_rows(buf.at[p, 1]))
            o_ref[...] = _layer_norm_rows(acc, g_ref[...], b_ref[...])


def _combine_ln(e0, e1, r0, r1, pad_start, rows_tiles, h, gates, g_row, b_row):
    tok = h.shape[0]
    tm = TM_COMBINE
    row = lambda i, *_: (i, 0)
    const = lambda i, *_: (0, 0)
    grid_spec = pltpu.PrefetchScalarGridSpec(
        num_scalar_prefetch=5,
        grid=(tok // tm,),
        in_specs=[
            pl.BlockSpec(memory_space=pl.ANY),
            pl.BlockSpec((tm, D_MODEL), row),
            pl.BlockSpec((tm, LANES), row),
            pl.BlockSpec((1, D_MODEL), const),
            pl.BlockSpec((1, D_MODEL), const),
        ],
        out_specs=pl.BlockSpec((tm, D_MODEL), row),
        scratch_shapes=[pltpu.VMEM((2, TOP_K, tm * ROW_TILE[0], LANES), F32), pltpu.SemaphoreType.DMA((2,))],
    )
    return pl.pallas_call(
        _combine_ln_kernel,
        out_shape=jax.ShapeDtypeStruct((tok, D_MODEL), F32),
        grid_spec=grid_spec,
        compiler_params=_params("arbitrary"),
        name="moe_combine_ln",
    )(e0, e1, r0, r1, pad_start, rows_tiles, h, gates, g_row, b_row)


def _slot_layout(counts, n_blocks):
    padded = (counts + MOE_ROWS - 1) // MOE_ROWS * MOE_ROWS
    pad_end = jnp.cumsum(padded)
    n_used = pad_end[-1:] // MOE_ROWS
    blocks = jnp.arange(n_blocks, dtype=jnp.int32)
    first_row = jnp.minimum(blocks, n_used - 1) * MOE_ROWS
    block_expert = jnp.sum(first_row[:, None] >= pad_end[None, :], axis=1)
    return (pad_end - padded).astype(jnp.int32), block_expert.astype(jnp.int32), n_used.astype(jnp.int32)


def _moe_ln(h, h_tiles, w_router, b_router, w_gate, w_up, w_down, layer, g_row, b_row):
    tok = h.shape[0]
    n_blocks = tok * TOP_K // MOE_ROWS + N_EXPERTS
    ids, gates, counts = _router(h, w_router, b_router)
    e0, e1, r0, r1 = (ids[c] for c in range(4))
    counts = counts[0, MOE_GROUPS:MOE_GROUPS + N_EXPERTS]
    pad_start, block_expert, n_used = _slot_layout(counts, n_blocks)
    xs = _dispatch(h_tiles.reshape((tok,) + ROW_TILE), e0, e1, r0, r1, pad_start, counts, n_used,
                   n_blocks * MOE_ROWS)
    rows = _expert_ffn(xs, block_expert, n_used, counts, w_gate, w_up, w_down, layer)
    return _combine_ln(e0, e1, r0, r1, pad_start, rows, h, gates, g_row, b_row)


def _pad_lanes(v):
    return jnp.pad(v, (0, LANES - v.shape[0])).reshape(1, LANES)


def _even_mixer(h, bsz, seq, w_in, conv_a, conv_w, conv_b, dt_bias, a_log, d_skip, norm_w, w_out, g_row, b_row):
    w = jnp.pad(w_in, ((0, 0), (0, AB_PROJ - w_in.shape[1]))).astype(MXU_DTYPE)
    y_a, z, xbc, dt, acs, acst = _even_front(
        h, w, conv_a, conv_w, conv_b.reshape(1, -1), _pad_lanes(dt_bias), _pad_lanes(-jnp.exp(a_log)), bsz, seq)
    dskip_row = jnp.repeat(d_skip, SSM_HEAD_DIM).reshape(1, -1)
    y_b = _ssd(xbc, dt, acs, acst, z, dskip_row, norm_w.reshape(1, -1), bsz, seq)
    return _outproj_ln([y_a, y_b], h, w_out.astype(MXU_DTYPE), g_row, b_row)


def _odd_mixer(h, bsz, seq, w_in, i_bias, f_bias, hnorm_w, fox_f_bias, w_out, g_row, b_row):
    c = np.cumsum((0, MLSTM_W, MLSTM_W, MLSTM_W, MLSTM_HEADS, MLSTM_HEADS, MLSTM_W, FOX_W, FOX_W, FOX_W, FOX_HEADS))
    part = lambda j: w_in[:, c[j]:c[j + 1]]
    q, k, v, i_pre, f_pre, o_pre, fq, fk, fv, ff = (part(j) for j in range(10))
    gate_cols = jnp.concatenate([i_pre, f_pre, ff, f_pre], axis=1)
    gate_cols = jnp.pad(gate_cols, ((0, 0), (0, LANES - gate_cols.shape[1])))
    spread = lambda m: jnp.pad(m.reshape(-1, FOX_HEADS, FOX_HEAD_DIM),
                               ((0, 0), (0, 0), (0, LANES - FOX_HEAD_DIM))).reshape(-1, FOX_AUG)
    w = jnp.concatenate([q, k, v, o_pre, fq, spread(fk), fv, gate_cols], axis=1).astype(MXU_DTYPE)
    gate_bias = _pad_lanes(jnp.concatenate([i_bias, f_bias, fox_f_bias, f_bias]))
    q, k, v, o, fq, fk_aug, fvt_aug, gates, gates_t = _odd_front(h, w, gate_bias, bsz, seq)
    y_c = _mlstm(q, k, v, o, gates, gates_t, hnorm_w.reshape(1, -1), bsz, seq)
    y_d = _fox(fq, fk_aug, fvt_aug, gates_t, bsz, seq)
    return _outproj_ln([y_c, y_d], h, w_out.astype(MXU_DTYPE), g_row, b_row)


def kernel(x, ab_w_in, ab_conv_a, ab_conv_ssm_w, ab_conv_ssm_b, ab_dt_bias, ab_a_log, ab_d_skip, ab_norm_w, ab_w_out, cd_w_in, cd_i_bias, cd_f_bias, cd_hnorm_w, cd_fox_f_bias, cd_w_out, ln1_g, ln1_b, ln2_g, ln2_b, moe_rg_w, moe_rg_b, moe_re_w, moe_re_b, moe_w_gate, moe_w_up, moe_w_down):
    bsz, seq, d = x.shape
    h = x.reshape(bsz * seq, d)
    stack = lambda w: w.reshape((w.shape[0] * w.shape[1],) + w.shape[2:])
    w_gate, w_up, w_down = stack(moe_w_gate), stack(moe_w_up), stack(moe_w_down)
    for layer in range(DEPTH):
        j = layer // 2
        g1, b1 = ln1_g[layer].reshape(1, -1), ln1_b[layer].reshape(1, -1)
        if layer % 2 == 0:
            h, h_tiles = _even_mixer(h, bsz, seq, ab_w_in[j], ab_conv_a[j], ab_conv_ssm_w[j], ab_conv_ssm_b[j],
                                     ab_dt_bias[j], ab_a_log[j], ab_d_skip[j], ab_norm_w[j], ab_w_out[j], g1, b1)
        else:
            h, h_tiles = _odd_mixer(h, bsz, seq, cd_w_in[j], cd_i_bias[j], cd_f_bias[j], cd_hnorm_w[j],
                                    cd_fox_f_bias[j], cd_w_out[j], g1, b1)
        re_w = jnp.transpose(moe_re_w[layer], (1, 0, 2)).reshape(d, N_EXPERTS)
        w_router = jnp.pad(jnp.concatenate([moe_rg_w[layer], re_w], axis=1),
                           ((0, 0), (0, LANES - MOE_GROUPS - N_EXPERTS)))
        b_router = _pad_lanes(jnp.concatenate([moe_rg_b[layer], moe_re_b[layer].reshape(-1)]))
        h = _moe_ln(h, h_tiles, w_router, b_router, w_gate, w_up, w_down, layer,
                    ln2_g[layer].reshape(1, -1), ln2_b[layer].reshape(1, -1))
    return h.reshape(bsz, seq, d)
```

```python
import functools

import numpy as np
import jax
import jax.numpy as jnp
from jax import lax
from jax.experimental import pallas as pl
from jax.experimental.pallas import tpu as pltpu

F32 = jnp.float32
MXU_DTYPE = jnp.bfloat16
HIGHEST = lax.Precision.HIGHEST

D_MODEL = 1024
DEPTH = 4
ALPHA = (2 * DEPTH) ** 0.25
LN_EPS = 1e-5
CONV_DIM = D_MODEL // 2
CONV_WIDTH = 3
SSM_D_INNER = D_MODEL
SSM_HEAD_DIM = 64
SSM_HEADS = SSM_D_INNER // SSM_HEAD_DIM
SSM_GROUPS = 4
SSM_STATE = 64
SSM_CONV = 4
SSM_BC = SSM_GROUPS * SSM_STATE
SSM_CONV_DIM = SSM_D_INNER + 2 * SSM_BC
MLSTM_HEADS = 4
MLSTM_HEAD_DIM = D_MODEL // 8
MLSTM_W = MLSTM_HEADS * MLSTM_HEAD_DIM
FOX_HEADS = 8
FOX_HEAD_DIM = D_MODEL // 16
FOX_W = FOX_HEADS * FOX_HEAD_DIM
MOE_GROUPS = 4
EXPERTS_PER_GROUP = 8
N_EXPERTS = MOE_GROUPS * EXPERTS_PER_GROUP
TOP_K = 2
D_EXPERT = D_MODEL // 2

LANES = 128
SUBLANES = 8
VMEM_LIMIT_BYTES = 56 * 1024 * 1024

CHUNK = 128
SEQ_PAIR = 4
MLSTM_SEQS = 1
MLSTM_CHUNK = 512
TM_FRONT = 512
TM_EVEN_FRONT = 256
CONV_ROWS, CONV_LANES = 64, 512
TM_OUT = 1024
TM_ROUTER = 512
TQ_FOX = 256
MOE_ROWS = 512
FFN_COLS = 256
TM_COMBINE = 512
TM_DISPATCH = 2048

AB_PROJ = 4224
FOX_AUG = FOX_HEADS * LANES
CD_FK = 5 * 512
CD_FV = CD_FK + FOX_AUG
CD_GATES = CD_FV + FOX_W
CD_PROJ = CD_GATES + LANES
LOG2E = 1.4426950408889634
TK_FOX = 128
FOX_ACC_ROWS = FOX_HEAD_DIM + SUBLANES
FOX_VROWS = LANES
G_I, G_F, G_FOX, G_BCUM = 0, 4, 8, 16


def _params(*sem):
    return pltpu.CompilerParams(dimension_semantics=sem, vmem_limit_bytes=VMEM_LIMIT_BYTES)


def _softplus(x):
    return jnp.maximum(x, 0.0) + jnp.log(1.0 + jnp.exp(-jnp.abs(x)))


def _sigmoid(x):
    return 1.0 / (1.0 + jnp.exp(-x))


def _layer_norm_rows(v, g, b):
    mu = jnp.mean(v, axis=-1, keepdims=True)
    c = v - mu
    var = jnp.mean(c * c, axis=-1, keepdims=True)
    return c * lax.rsqrt(var + LN_EPS) * g + b


def _tril(n, block):
    i = np.arange(n)
    m = (i[:, None] >= i[None, :]) & (i[:, None] // block == i[None, :] // block)
    return jnp.asarray(m.astype(np.float32), MXU_DTYPE)


def _full(shape):
    return pl.BlockSpec(shape, lambda *_: (0,) * len(shape), pipeline_mode=pl.Buffered(1))


ROW_TILE = (D_MODEL // LANES, LANES)


def _tiles_to_rows(ref):
    n = ref.shape[0] // ROW_TILE[0]
    return jnp.concatenate([ref[pl.ds(s, n, stride=ROW_TILE[0]), :] for s in range(ROW_TILE[0])], axis=1)


def _rows_to_tiles(ref, val):
    n = val.shape[0]
    for s in range(ROW_TILE[0]):
        ref[pl.ds(s, n, stride=ROW_TILE[0]), :] = val[:, s * LANES:(s + 1) * LANES]


def _split3(x):
    narrow = lambda v: v.astype(MXU_DTYPE).astype(F32)
    x1 = narrow(x)
    x2 = narrow(x - x1)
    return x1, x2, narrow(x - x1 - x2)


def _cumsum_rows(tril, x):
    parts = jnp.dot(tril, jnp.concatenate(_split3(x), axis=1).astype(MXU_DTYPE), preferred_element_type=F32)
    return parts[:, 0:LANES] + parts[:, LANES:2 * LANES] + parts[:, 2 * LANES:3 * LANES]


def _even_front_kernel(h_ref, w_ref, ca_ref, cw_ref, cb_ref, dtb_ref, aneg_ref, tril_ref,
                       ya_ref, z_ref, xbc_ref, dt_ref, acs_ref, acst_ref,
                       proj_buf, ua_ext, xbc_ext, *, tiles_per_seq):
    tm = h_ref.shape[0]
    i = pl.program_id(0)
    cur = i % 2

    @pl.when(i == 0)
    def _():
        proj_buf[1] = jnp.zeros(proj_buf.shape[1:], F32)

    @pl.when((i == 0) | ((i - 1) % tiles_per_seq == 0))
    def _():
        ua_ext[0:SUBLANES, :] = jnp.zeros((SUBLANES, CONV_DIM), F32)
        xbc_ext[0:SUBLANES, :] = jnp.zeros((SUBLANES, SSM_CONV_DIM), F32)

    z0 = 3 * CONV_DIM
    x0 = z0 + SSM_D_INNER
    d0 = x0 + SSM_CONV_DIM

    x_in = h_ref[...].astype(MXU_DTYPE)
    proj = proj_buf.at[1 - cur]
    z_ref[...] = proj[:, z0:x0]

    n_blocks = tm // CONV_ROWS
    col_cuts = [AB_PROJ * g // n_blocks // LANES * LANES for g in range(n_blocks)] + [AB_PROJ]
    for r0 in range(0, tm, CONV_ROWS):
        c_lo, c_hi = col_cuts[r0 // CONV_ROWS], col_cuts[r0 // CONV_ROWS + 1]
        proj_buf[cur, :, c_lo:c_hi] = jnp.dot(x_in, w_ref[:, c_lo:c_hi], preferred_element_type=F32)
        rows = slice(r0, r0 + CONV_ROWS)
        ext_rows = slice(SUBLANES + r0, SUBLANES + r0 + CONV_ROWS)
        ua_ext[ext_rows, :] = proj[rows, CONV_DIM:2 * CONV_DIM] * proj[rows, 2 * CONV_DIM:3 * CONV_DIM]
        conv = None
        for k in range(CONV_WIDTH):
            tap = ca_ref[k:k + 1, :] * ua_ext[pl.ds(SUBLANES + r0 - (CONV_WIDTH - 1) + k, CONV_ROWS), :]
            conv = tap if conv is None else conv + tap
        ya_ref[rows, :] = (proj[rows, 0:CONV_DIM] * conv).astype(ya_ref.dtype)
        for c0 in range(0, SSM_CONV_DIM, CONV_LANES):
            cols = slice(c0, c0 + CONV_LANES)
            xbc_ext[ext_rows, cols] = proj[rows, x0 + c0:x0 + c0 + CONV_LANES]
            conv = cb_ref[:, cols]
            for k in range(SSM_CONV):
                conv = conv + cw_ref[k:k + 1, cols] * xbc_ext[pl.ds(SUBLANES + r0 - (SSM_CONV - 1) + k, CONV_ROWS), cols]
            xbc_ref[rows, cols] = conv * _sigmoid(conv)
    ua_ext[0:SUBLANES, :] = ua_ext[tm:tm + SUBLANES, :]
    xbc_ext[0:SUBLANES, :] = xbc_ext[tm:tm + SUBLANES, :]

    dt = _softplus(proj[:, d0:d0 + LANES] + dtb_ref[...])
    a = dt * aneg_ref[...]
    acs = _cumsum_rows(tril_ref[...], a)
    dt_ref[...] = dt
    acs_ref[...] = acs
    acst_ref[...] = acs.T


def _even_front(h, w_in, conv_a, conv_w, conv_b, dt_bias_row, aneg_row, bsz, seq):
    tok = bsz * seq
    tm = TM_EVEN_FRONT
    n_tiles = tok // tm
    row = lambda i: (jnp.maximum(i - 1, 0), 0)
    out_shapes = (
        jax.ShapeDtypeStruct((tok, CONV_DIM), MXU_DTYPE),
        jax.ShapeDtypeStruct((tok, SSM_D_INNER), F32),
        jax.ShapeDtypeStruct((tok, SSM_CONV_DIM), F32),
        jax.ShapeDtypeStruct((tok, LANES), F32),
        jax.ShapeDtypeStruct((tok, LANES), F32),
        jax.ShapeDtypeStruct((LANES, tok), F32),
    )
    return pl.pallas_call(
        functools.partial(_even_front_kernel, tiles_per_seq=seq // tm),
        out_shape=out_shapes,
        grid=(n_tiles + 1,),
        in_specs=[
            pl.BlockSpec((tm, D_MODEL), lambda i: (jnp.minimum(i, n_tiles - 1), 0)),
            _full((D_MODEL, AB_PROJ)),
            _full((CONV_WIDTH, CONV_DIM)),
            _full((SSM_CONV, SSM_CONV_DIM)),
            _full((1, SSM_CONV_DIM)),
            _full((1, LANES)),
            _full((1, LANES)),
            _full((tm, tm)),
        ],
        out_specs=(
            pl.BlockSpec((tm, CONV_DIM), row),
            pl.BlockSpec((tm, SSM_D_INNER), row),
            pl.BlockSpec((tm, SSM_CONV_DIM), row),
            pl.BlockSpec((tm, LANES), row),
            pl.BlockSpec((tm, LANES), row),
            pl.BlockSpec((LANES, tm), lambda i: (0, jnp.maximum(i - 1, 0))),
        ),
        scratch_shapes=[
            pltpu.VMEM((2, tm, AB_PROJ), F32),
            pltpu.VMEM((tm + SUBLANES, CONV_DIM), F32),
            pltpu.VMEM((tm + SUBLANES, SSM_CONV_DIM), F32),
        ],
        compiler_params=_params("arbitrary"),
        name="even_front",
    )(h, w_in, conv_a, conv_w, conv_b, dt_bias_row, aneg_row, _tril(tm, CHUNK))


def _bcast_heads(arr, n_heads, width):
    per = LANES // width
    length = arr.shape[0]
    lane = lax.broadcasted_iota(jnp.int32, (length, LANES), 1)
    outs = []
    for j in range(n_heads // per):
        v = jnp.broadcast_to(arr[:, j * per:j * per + 1], (length, LANES))
        for r in range(1, per):
            v = jnp.where(lane >= r * width, jnp.broadcast_to(arr[:, j * per + r:j * per + r + 1], (length, LANES)), v)
        outs.append(v)
    return jnp.concatenate(outs, axis=1)


def _expand_heads(arr, e3_ref):
    hi, lo, _ = _split3(arr)
    return jnp.dot(jnp.concatenate([hi, lo], axis=1).astype(MXU_DTYPE), e3_ref[...],
                   preferred_element_type=F32)


def _ssd_kernel(*refs):
    nb = SEQ_PAIR
    xbc_ref, dt_ref, acs_ref, z_ref = refs[:4]
    acst_refs = refs[4:4 + nb]
    dskip_ref, nw_ref, e3_ref, y_ref, state = refs[4 + nb:]
    L = CHUNK
    P = SSM_HEAD_DIM
    R = SSM_HEADS // SSM_GROUPS
    GW = R * P

    @pl.when(pl.program_id(1) == 0)
    def _():
        state[...] = jnp.zeros(state.shape, F32)

    row = lax.broadcasted_iota(jnp.int32, (L, L), 0)
    col = lax.broadcasted_iota(jnp.int32, (L, L), 1)
    causal = row >= col
    lane_g = lax.broadcasted_iota(jnp.int32, (L, GW), 1)

    per_row = []
    for j in range(nb):
        acs = acs_ref[j]
        per_row += [dt_ref[j], jnp.exp(acs[L - 1:L, :] - acs), jnp.exp(acs)]
    expanded = _expand_heads(jnp.concatenate(per_row, axis=0), e3_ref)

    st_old = [state[c] for c in range(nb * SSM_GROUPS)]
    st_new = []
    for j in range(nb):
        xs = xbc_ref[j, :, 0:SSM_D_INNER]
        bm = xbc_ref[j, :, SSM_D_INNER:SSM_D_INNER + SSM_BC]
        cm = xbc_ref[j, :, SSM_D_INNER + SSM_BC:SSM_CONV_DIM]
        acs = acs_ref[j]
        acst = acst_refs[j][...]
        a_last = acs[L - 1:L, :]
        dtx = expanded[(3 * j) * L:(3 * j + 1) * L, :]
        decx = expanded[(3 * j + 1) * L:(3 * j + 2) * L, :]
        expx = expanded[(3 * j + 2) * L:(3 * j + 3) * L, :]
        xdt = xs * dtx
        xdec = (xdt * decx).astype(MXU_DTYPE)
        xdt_m = xdt.astype(MXU_DTYPE)
        chunk_decay = jnp.exp(jnp.broadcast_to(a_last, (SUBLANES, LANES)))
        cdx = _bcast_heads(chunk_decay, SSM_HEADS, P)[0:1, :]

        bm_t = bm.T.astype(MXU_DTYPE)
        cm_m = cm.astype(MXU_DTYPE)
        bm_m = bm.astype(MXU_DTYPE)

        ys = []
        for g in range(SSM_GROUPS):
            n0 = g * SSM_STATE
            c_g = cm_m[:, n0:n0 + SSM_STATE]
            cb = lax.dot_general(c_g, bm_m[:, n0:n0 + SSM_STATE], (((1,), (1,)), ((), ())),
                                 preferred_element_type=F32)
            ms = []
            for r in range(R):
                hd = g * R + r
                seg = jnp.exp(jnp.where(causal, acs[:, hd:hd + 1] - acst[hd:hd + 1, :], -jnp.inf))
                ms.append((cb * seg).astype(MXU_DTYPE))
            big = jnp.dot(jnp.concatenate(ms, axis=0), xdt_m[:, g * GW:(g + 1) * GW],
                          preferred_element_type=F32)
            y_diag = big[0:L, :]
            for r in range(1, R):
                y_diag = jnp.where(lane_g >= r * P, big[r * L:(r + 1) * L, :], y_diag)
            st = st_old[j * SSM_GROUPS + g]
            y_off = jnp.dot(c_g, st.astype(MXU_DTYPE), preferred_element_type=F32)
            new = jnp.dot(bm_t[n0:n0 + SSM_STATE, :], xdec[:, g * GW:(g + 1) * GW],
                          preferred_element_type=F32)
            st_new.append(st * cdx[:, g * GW:(g + 1) * GW] + new)
            ys.append(y_diag + y_off * expx[:, g * GW:(g + 1) * GW])
        y = jnp.concatenate(ys, axis=1) + dskip_ref[...] * xs
        z = z_ref[j]
        u = y * (z * _sigmoid(z))
        y = u * lax.rsqrt(jnp.mean(u * u, axis=-1, keepdims=True) + LN_EPS) * nw_ref[...]
        y_ref[j] = y.astype(y_ref.dtype)

    for c in range(nb * SSM_GROUPS):
        state[c] = st_new[c]


def _ssd(xbc, dt, acs, acst, z, dskip_row, normw_row, bsz, seq):
    nb = SEQ_PAIR
    nc = seq // CHUNK
    per_seq = lambda a: a.reshape(bsz, seq, a.shape[-1])
    blk = lambda width: pl.BlockSpec((nb, CHUNK, width), lambda g, c: (g, c, 0))
    acst_specs = [pl.BlockSpec((LANES, CHUNK), lambda g, c, j=j: (0, (g * nb + j) * nc + c)) for j in range(nb)]
    expand = np.zeros((LANES, SSM_D_INNER), np.float32)
    for hd in range(SSM_HEADS):
        expand[hd, hd * SSM_HEAD_DIM:(hd + 1) * SSM_HEAD_DIM] = 1.0
    e3 = jnp.asarray(np.concatenate([expand] * 2, axis=0), MXU_DTYPE)
    y = pl.pallas_call(
        _ssd_kernel,
        out_shape=jax.ShapeDtypeStruct((bsz, seq, SSM_D_INNER), MXU_DTYPE),
        grid=(bsz // nb, nc),
        in_specs=[blk(SSM_CONV_DIM), blk(LANES), blk(LANES), blk(SSM_D_INNER)] + acst_specs + [
            _full((1, SSM_D_INNER)),
            _full((1, SSM_D_INNER)),
            _full((2 * LANES, SSM_D_INNER)),
        ],
        out_specs=blk(SSM_D_INNER),
        scratch_shapes=[pltpu.VMEM((nb * SSM_GROUPS, SSM_STATE, SSM_D_INNER // SSM_GROUPS), F32)],
        compiler_params=_params("arbitrary", "arbitrary"),
        name="ssd_scan",
    )(per_seq(xbc), per_seq(dt), per_seq(acs), per_seq(z), *([acst] * nb), dskip_row, normw_row, e3)
    return y.reshape(bsz * seq, SSM_D_INNER)


def _outproj_ln_kernel(*refs, widths):
    n = len(widths)
    parts = refs[:n]
    h_ref, w_ref, g_ref, b_ref, o_ref, ot_ref = refs[n:]
    acc = ALPHA * h_ref[...]
    off = 0
    for p, wd in zip(parts, widths):
        acc = acc + jnp.dot(p[...].astype(MXU_DTYPE), w_ref[off:off + wd, :], preferred_element_type=F32)
        off += wd
    out = _layer_norm_rows(acc, g_ref[...], b_ref[...])
    o_ref[...] = out
    _rows_to_tiles(ot_ref, out)


def _outproj_ln(parts, h, w_out, g_row, b_row):
    tok = h.shape[0]
    tm = TM_OUT
    widths = tuple(p.shape[1] for p in parts)
    row = lambda i: (i, 0)
    return pl.pallas_call(
        functools.partial(_outproj_ln_kernel, widths=widths),
        out_shape=(jax.ShapeDtypeStruct((tok, D_MODEL), F32),
                   jax.ShapeDtypeStruct((tok * ROW_TILE[0], LANES), F32)),
        grid=(tok // tm,),
        in_specs=[pl.BlockSpec((tm, wd), row) for wd in widths] + [
            pl.BlockSpec((tm, D_MODEL), row),
            _full((sum(widths), D_MODEL)),
            _full((1, D_MODEL)),
            _full((1, D_MODEL)),
        ],
        out_specs=(pl.BlockSpec((tm, D_MODEL), row), pl.BlockSpec((tm * ROW_TILE[0], LANES), row)),
        compiler_params=_params("arbitrary"),
        name="outproj_ln",
    )(*parts, h, w_out, g_row, b_row)


def _odd_front_kernel(h_ref, w_ref, gb_ref, tril_ref,
                      q_ref, k_ref, v_ref, o_ref, fq_ref, fk_ref, fvt_ref, g_ref, gt_ref, carry):
    tm = h_ref.shape[0]
    x_in = h_ref[...].astype(MXU_DTYPE)
    proj_cols = lambda lo, hi: jnp.dot(x_in, w_ref[:, lo:hi], preferred_element_type=F32)

    @pl.when(pl.program_id(1) == 0)
    def _():
        carry[...] = jnp.zeros(carry.shape, F32)

    raw = proj_cols(CD_GATES, CD_GATES + LANES) + gb_ref[...]
    lane = lax.broadcasted_iota(jnp.int32, (tm, LANES), 1)
    g = jnp.where(lane < G_F, raw, -_softplus(-raw))
    prev = carry[0:1, :]
    glob = _cumsum_rows(tril_ref[...], g) + prev
    before = []
    for c in range(tm // MLSTM_CHUNK):
        before.append(jnp.broadcast_to(prev, (MLSTM_CHUNK, LANES)))
        prev = glob[(c + 1) * MLSTM_CHUNK - 1:(c + 1) * MLSTM_CHUNK, :]
    carry[...] = jnp.broadcast_to(prev, carry.shape)
    local = glob - jnp.concatenate(before, axis=0)
    out = jnp.where((lane >= G_FOX) & (lane < G_BCUM), glob, jnp.where(lane >= G_BCUM, local, g))
    g_ref[...] = out
    gt_ref[...] = out.T

    q_ref[...] = proj_cols(0, 512)
    k_ref[...] = proj_cols(512, 1024) * (MLSTM_HEAD_DIM ** -0.5)
    v_ref[...] = proj_cols(1024, 1536)
    o_ref[...] = _sigmoid(proj_cols(1536, 2048))
    fq_ref[...] = proj_cols(2048, 2560) * (FOX_HEAD_DIM ** -0.5 * LOG2E)

    is_bias = (lane >= FOX_HEAD_DIM) & (lane < FOX_HEAD_DIM + 3)
    fk = proj_cols(CD_FK, CD_FK + FOX_AUG)
    for hd in range(FOX_HEADS):
        c1, c2, c3 = _split3(out[:, G_FOX + hd:G_FOX + hd + 1] * (-LOG2E))
        bias = jnp.where(lane == FOX_HEAD_DIM, c1, jnp.where(lane == FOX_HEAD_DIM + 1, c2, c3))
        k_h = fk[:, hd * LANES:(hd + 1) * LANES]
        fk_ref[:, hd * LANES:(hd + 1) * LANES] = jnp.where(is_bias, bias, k_h).astype(MXU_DTYPE)
    v_t = proj_cols(CD_FV, CD_FV + FOX_W).T
    extra = jnp.where(lax.broadcasted_iota(jnp.int32, (FOX_VROWS - FOX_HEAD_DIM, tm), 0) == 0, 1.0, 0.0)
    fvt_ref[...] = jnp.concatenate(
        [blk for hd in range(FOX_HEADS) for blk in (v_t[hd * FOX_HEAD_DIM:(hd + 1) * FOX_HEAD_DIM, :], extra)],
        axis=0).astype(MXU_DTYPE)


def _odd_front(h, w_in, gate_bias_row, bsz, seq):
    tok = bsz * seq
    tm = TM_FRONT
    ns = seq // tm
    row = lambda b, s: (b * ns + s, 0)
    col = lambda b, s: (0, b * ns + s)
    wide = jax.ShapeDtypeStruct((tok, 512), F32)
    return pl.pallas_call(
        _odd_front_kernel,
        out_shape=(wide,) * 5 + (jax.ShapeDtypeStruct((tok, FOX_AUG), MXU_DTYPE),
                                 jax.ShapeDtypeStruct((FOX_HEADS * FOX_VROWS, tok), MXU_DTYPE),
                                 jax.ShapeDtypeStruct((tok, LANES), F32),
                                 jax.ShapeDtypeStruct((LANES, tok), F32)),
        grid=(bsz, ns),
        in_specs=[
            pl.BlockSpec((tm, D_MODEL), row),
            _full((D_MODEL, CD_PROJ)),
            _full((1, LANES)),
            _full((tm, tm)),
        ],
        out_specs=(pl.BlockSpec((tm, 512), row),) * 5 + (
            pl.BlockSpec((tm, FOX_AUG), row),
            pl.BlockSpec((FOX_HEADS * FOX_VROWS, tm), col),
            pl.BlockSpec((tm, LANES), row),
            pl.BlockSpec((LANES, tm), col),
        ),
        scratch_shapes=[pltpu.VMEM((SUBLANES, LANES), F32)],
        compiler_params=_params("arbitrary", "arbitrary"),
        name="odd_front",
    )(h, w_in, gate_bias_row, _tril(tm, tm))


def _mlstm_kernel(*refs):
    nb = MLSTM_SEQS
    q_ref, k_ref, v_ref, o_ref, g_ref = refs[:5]
    gt_refs = refs[5:5 + nb]
    nw_ref, y_ref, c_state, m_state = refs[5 + nb:]
    L = MLSTM_CHUNK
    DH = MLSTM_HEAD_DIM

    @pl.when(pl.program_id(1) == 0)
    def _():
        c_state[...] = jnp.zeros(c_state.shape, F32)
        m_state[...] = jnp.zeros(m_state.shape, F32)

    row = lax.broadcasted_iota(jnp.int32, (L, L), 0)
    col = lax.broadcasted_iota(jnp.int32, (L, L), 1)
    causal = row >= col
    ones_col = jnp.where(lax.broadcasted_iota(jnp.int32, (L, DH), 1) == 0, 1.0, 0.0)

    for j in range(nb):
        gates = g_ref[j]
        gates_t = gt_refs[j][...]
        for hd in range(MLSTM_HEADS):
            st = j * MLSTM_HEADS + hd
            sl = slice(hd * DH, (hd + 1) * DH)
            q = q_ref[j, :, sl].astype(MXU_DTYPE)
            k = k_ref[j, :, sl]
            v_ext = jnp.concatenate([v_ref[j, :, sl], ones_col], axis=1).astype(MXU_DTYPE)
            b_col = gates[:, G_BCUM + hd:G_BCUM + hd + 1]
            i_col = gates[:, G_I + hd:G_I + hd + 1]
            b_row = gates_t[G_BCUM + hd:G_BCUM + hd + 1, :]
            i_row = gates_t[G_I + hd:G_I + hd + 1, :]
            m_prev = m_state[st:st + 1, 0:1]
            c_ext = c_state[st]

            d_mat = jnp.where(causal, b_col - b_row + i_row, -jnp.inf)
            inter = b_col + m_prev
            m_t = jnp.maximum(jnp.max(d_mat, axis=-1, keepdims=True), inter)
            s_qk = lax.dot_general(q, k.astype(MXU_DTYPE), (((1,), (1,)), ((), ())), preferred_element_type=F32)
            w_qk = s_qk * jnp.exp(d_mat - m_t)
            s_inter = jnp.exp(inter - m_t)
            num_ext = (jnp.dot(w_qk.astype(MXU_DTYPE), v_ext, preferred_element_type=F32)
                       + s_inter * jnp.dot(q, c_ext.astype(MXU_DTYPE), preferred_element_type=F32))
            den = num_ext[:, DH:DH + 1]
            hval = num_ext[:, 0:DH] / jnp.maximum(jnp.abs(den), jnp.exp(-m_t))

            b_last = b_col[L - 1:L, :]
            g_log = b_last - b_col + i_col
            m_new = jnp.maximum(b_last + m_prev, jnp.max(g_log, axis=0, keepdims=True))
            w_k = jnp.exp(g_log - m_new)
            decay = jnp.exp(b_last + m_prev - m_new)
            kw_t = (k * w_k).T.astype(MXU_DTYPE)
            c_state[st] = decay * c_ext + jnp.dot(kw_t, v_ext, preferred_element_type=F32)
            m_state[st:st + 1, :] = jnp.broadcast_to(m_new, (1, LANES))

            mu = jnp.mean(hval, axis=-1, keepdims=True)
            cen = hval - mu
            var = jnp.mean(cen * cen, axis=-1, keepdims=True)
            y = o_ref[j, :, sl] * (cen * lax.rsqrt(var + LN_EPS) * nw_ref[:, sl])
            y_ref[j, :, sl] = y.astype(y_ref.dtype)


def _mlstm(q, k, v, o, gates, gates_t, hnorm_row, bsz, seq):
    nb = MLSTM_SEQS
    nc = seq // MLSTM_CHUNK
    per_seq = lambda a: a.reshape(bsz, seq, a.shape[-1])
    blk = lambda width: pl.BlockSpec((nb, MLSTM_CHUNK, width), lambda g, c: (g, c, 0))
    gt_specs = [pl.BlockSpec((LANES, MLSTM_CHUNK), lambda g, c, j=j: (0, (g * nb + j) * nc + c))
                for j in range(nb)]
    y = pl.pallas_call(
        _mlstm_kernel,
        out_shape=jax.ShapeDtypeStruct((bsz, seq, MLSTM_W), MXU_DTYPE),
        grid=(bsz // nb, nc),
        in_specs=[blk(MLSTM_W)] * 4 + [blk(LANES)] + gt_specs + [_full((1, MLSTM_W))],
        out_specs=blk(MLSTM_W),
        scratch_shapes=[pltpu.VMEM((nb * MLSTM_HEADS, MLSTM_HEAD_DIM, 2 * MLSTM_HEAD_DIM), F32),
                        pltpu.VMEM((nb * MLSTM_HEADS, LANES), F32)],
        compiler_params=_params("arbitrary", "arbitrary"),
        name="mlstm_scan",
    )(per_seq(q), per_seq(k), per_seq(v), per_seq(o), per_seq(gates), *([gates_t] * nb), hnorm_row)
    return y.reshape(bsz * seq, MLSTM_W)


def _fox_kernel(q_ref, k_ref, vt_ref, gt_ref, y_ref, *scratch):
    acc_refs = scratch[:FOX_HEADS]
    qa_ref = scratch[FOX_HEADS]
    tq = q_ref.shape[0]
    tk = TK_FOX
    dh = FOX_HEAD_DIM
    qi = pl.program_id(1)
    q_t = q_ref[...].T
    bias_rows = jnp.where(lax.broadcasted_iota(jnp.int32, (LANES - dh, tq), 0) < 3, 1.0, 0.0)
    for hd in range(FOX_HEADS):
        qa_ref[hd] = jnp.concatenate([q_t[hd * dh:(hd + 1) * dh, :], bias_rows], axis=0).astype(MXU_DTYPE)
        acc_refs[hd][...] = jnp.zeros(acc_refs[hd].shape, F32)
    cq = gt_ref[G_FOX:G_FOX + FOX_HEADS, :] * LOG2E
    key_pos = lax.broadcasted_iota(jnp.int32, (tk, tq), 0)
    qry_pos = lax.broadcasted_iota(jnp.int32, (tk, tq), 1) + qi * tq
    n_full = qi * (tq // tk)

    def block(j, ms, masked):
        k0 = pl.multiple_of(j * tk, tk)
        out = []
        for hd in range(FOX_HEADS):
            hs = slice(hd * LANES, (hd + 1) * LANES)
            s = jnp.dot(k_ref[pl.ds(k0, tk), hs], qa_ref[hd], preferred_element_type=F32)
            if masked:
                s = jnp.where(key_pos + k0 <= qry_pos, s, -jnp.inf)
            cq_h = cq[hd:hd + 1, :]
            m_new = jnp.maximum(ms[hd], jnp.max(s, axis=0, keepdims=True) + cq_h)
            p = jnp.exp2(s - (m_new - cq_h))
            pv = jnp.dot(vt_ref[hd * FOX_VROWS:(hd + 1) * FOX_VROWS, pl.ds(k0, tk)], p.astype(MXU_DTYPE),
                         preferred_element_type=F32)
            acc_refs[hd][...] = jnp.exp2(ms[hd] - m_new) * acc_refs[hd][...] + pv[0:FOX_ACC_ROWS, :]
            out.append(m_new)
        return tuple(out)

    ms = tuple(jnp.full((1, tq), -jnp.inf, F32) for _ in range(FOX_HEADS))
    ms = lax.fori_loop(0, n_full, functools.partial(block, masked=False), ms)
    for d in range(tq // tk):
        ms = block(n_full + d, ms, masked=True)
    outs = []
    for hd in range(FOX_HEADS):
        acc = acc_refs[hd][...]
        outs.append(acc[0:dh, :] / acc[dh:dh + 1, :])
    y_ref[...] = jnp.concatenate(outs, axis=0).T.astype(y_ref.dtype)


def _fox(fq, fk_aug, fvt_aug, gates_t, bsz, seq):
    tok = bsz * seq
    tq = TQ_FOX
    nq = seq // tq
    return pl.pallas_call(
        _fox_kernel,
        out_shape=jax.ShapeDtypeStruct((tok, FOX_W), MXU_DTYPE),
        grid=(bsz, nq),
        in_specs=[
            pl.BlockSpec((tq, FOX_W), lambda b, i: (b * nq + i, 0)),
            pl.BlockSpec((seq, FOX_AUG), lambda b, i: (b, 0)),
            pl.BlockSpec((FOX_HEADS * FOX_VROWS, seq), lambda b, i: (0, b)),
            pl.BlockSpec((LANES, tq), lambda b, i: (0, b * nq + i)),
        ],
        out_specs=pl.BlockSpec((tq, FOX_W), lambda b, i: (b * nq + i, 0)),
        scratch_shapes=[pltpu.VMEM((FOX_ACC_ROWS, tq), F32)] * FOX_HEADS
                       + [pltpu.VMEM((FOX_HEADS, LANES, tq), MXU_DTYPE)],
        compiler_params=_params("arbitrary", "arbitrary"),
        name="fox_attention",
    )(fq, fk_aug, fvt_aug, gates_t)


def _router_kernel(h_ref, w_ref, b_ref, stril_ref, id_ref, gate_ref, cnt_ref, carry):
    tm = h_ref.shape[0]

    @pl.when(pl.program_id(0) == 0)
    def _():
        carry[...] = jnp.zeros(carry.shape, F32)

    h = h_ref[...]
    h_hi = h.astype(MXU_DTYPE)
    h_lo = (h - h_hi.astype(F32)).astype(MXU_DTYPE)
    both = jnp.dot(h_hi, w_ref[...], preferred_element_type=F32)
    logits = (both[:, 0:LANES] + both[:, LANES:2 * LANES]
              + jnp.dot(h_lo, w_ref[:, 0:LANES], preferred_element_type=F32) + b_ref[...])
    lane = lax.broadcasted_iota(jnp.int32, (tm, LANES), 1).astype(F32)
    neg = -jnp.inf
    first = lambda hit: jnp.min(jnp.where(hit, lane, float(LANES)), axis=-1, keepdims=True)
    gl = jnp.where(lane < MOE_GROUPS, logits, neg)
    g_max = jnp.max(gl, axis=-1, keepdims=True)
    g_idx = first(gl == g_max)
    p_group = 1.0 / jnp.sum(jnp.exp(gl - g_max), axis=-1, keepdims=True)
    e_lo = MOE_GROUPS + g_idx * EXPERTS_PER_GROUP
    el = jnp.where((lane >= e_lo) & (lane < e_lo + EXPERTS_PER_GROUP), logits, neg)
    v1 = jnp.max(el, axis=-1, keepdims=True)
    i1 = first(el == v1)
    el2 = jnp.where(lane == i1, neg, el)
    v2 = jnp.max(el2, axis=-1, keepdims=True)
    i2 = first(el2 == v2)
    t = jnp.exp(v2 - v1)
    w1 = 1.0 / (1.0 + t)
    gate_ref[...] = jnp.where(lane == 0.0, p_group * w1, jnp.where(lane == 1.0, p_group * (t * w1), 0.0))

    hit1 = lane == i1
    hit2 = lane == i2
    sent = jnp.where(hit1 | hit2, 1.0, 0.0)
    before = jnp.dot(stril_ref[...], sent.astype(jnp.bfloat16), preferred_element_type=F32) + carry[0:1, :]
    r1 = jnp.sum(jnp.where(hit1, before, 0.0), axis=-1, keepdims=True)
    r2 = jnp.sum(jnp.where(hit2, before, 0.0), axis=-1, keepdims=True)
    total = before[tm - 1:tm, :] + sent[tm - 1:tm, :]
    carry[...] = jnp.broadcast_to(total, carry.shape)
    cnt_ref[...] = jnp.broadcast_to(total, cnt_ref.shape).astype(jnp.int32)
    ids = jnp.where(lane == 0.0, i1 - MOE_GROUPS, jnp.where(lane == 1.0, i2 - MOE_GROUPS,
                    jnp.where(lane == 2.0, r1, jnp.where(lane == 3.0, r2, 0.0))))
    id_ref[...] = ids.T[0:SUBLANES, :].astype(jnp.int32)


def _router(h, w_router, b_router):
    tok = h.shape[0]
    tm = TM_ROUTER
    row = lambda i: (i, 0)
    idx = np.arange(tm)
    stril = jnp.asarray((idx[:, None] > idx[None, :]).astype(np.float32), jnp.bfloat16)
    w_hi = w_router.astype(MXU_DTYPE)
    w_hi_lo = (w_hi, (w_router - w_hi.astype(F32)).astype(MXU_DTYPE))
    return pl.pallas_call(
        _router_kernel,
        out_shape=(jax.ShapeDtypeStruct((SUBLANES, tok), jnp.int32), jax.ShapeDtypeStruct((tok, LANES), F32),
                   jax.ShapeDtypeStruct((SUBLANES, LANES), jnp.int32)),
        grid=(tok // tm,),
        in_specs=[pl.BlockSpec((tm, D_MODEL), row), _full((D_MODEL, 2 * LANES)), _full((1, LANES)),
                  _full((tm, tm))],
        out_specs=(pl.BlockSpec((SUBLANES, tm), lambda i: (0, i)), pl.BlockSpec((tm, LANES), row),
                   _full((SUBLANES, LANES))),
        scratch_shapes=[pltpu.VMEM((SUBLANES, LANES), F32)],
        compiler_params=_params("arbitrary"),
        name="moe_router",
    )(h, jnp.concatenate([w_hi, w_hi_lo[1]], axis=1), b_router, stril)


def _pad_pieces(n):
    return tuple(1 << b for b in reversed(range((n - 1).bit_length())))


def _dispatch_kernel(e0_ref, e1_ref, r0_ref, r1_ref, start_ref, cnt_ref, nused_ref, h_ref, xs_hbm, zeros, sem,
                     pad_sem):
    tm = h_ref.shape[0]
    i = pl.program_id(0)
    base = i * tm

    def issue(r, c):
        t = base + r
        pltpu.make_async_copy(h_ref.at[r], xs_hbm.at[start_ref[e0_ref[t]] + r0_ref[t]], sem).start(priority=0)
        pltpu.make_async_copy(h_ref.at[r], xs_hbm.at[start_ref[e1_ref[t]] + r1_ref[t]], sem).start(priority=1)
        return c

    lax.fori_loop(0, tm, issue, 0, unroll=8)

    def pad_copies(e, fn):
        cnt = cnt_ref[e]
        n_pad = (MOE_ROWS - cnt % MOE_ROWS) % MOE_ROWS
        first = start_ref[e] + cnt
        for piece in _pad_pieces(MOE_ROWS):
            @pl.when((n_pad & piece) != 0)
            def _(piece=piece):
                off = first + (n_pad & ~(2 * piece - 1))
                fn(pltpu.make_async_copy(zeros.at[pl.ds(0, piece)], xs_hbm.at[pl.ds(off, piece)], pad_sem))

    def tail_copies(blk, fn):
        for part in range(MOE_ROWS // zeros.shape[0]):
            off = blk * MOE_ROWS + part * zeros.shape[0]
            fn(pltpu.make_async_copy(zeros, xs_hbm.at[pl.ds(off, zeros.shape[0])], pad_sem))

    @pl.when(i == 0)
    def _():
        zeros[...] = jnp.zeros(zeros.shape, F32)
        n_blocks = xs_hbm.shape[0] // MOE_ROWS

        def start(e, c):
            pad_copies(e, lambda cp: cp.start())
            return c

        def wait(e, c):
            pad_copies(e, lambda cp: cp.wait())
            return c

        def tail_start(blk, c):
            tail_copies(blk, lambda cp: cp.start())
            return c

        def tail_wait(blk, c):
            tail_copies(blk, lambda cp: cp.wait())
            return c

        lax.fori_loop(0, N_EXPERTS, start, 0)
        lax.fori_loop(nused_ref[0], n_blocks, tail_start, 0)
        lax.fori_loop(0, N_EXPERTS, wait, 0)
        lax.fori_loop(nused_ref[0], n_blocks, tail_wait, 0)

    for _ in range(TOP_K):
        pltpu.make_async_copy(h_ref, xs_hbm.at[pl.ds(0, tm)], sem).wait()


def _dispatch(h_tiles, e0, e1, r0, r1, pad_start, counts, n_used, n_slots):
    tok = h_tiles.shape[0]
    tm = TM_DISPATCH
    assert tok % tm == 0, (tok, tm)
    grid_spec = pltpu.PrefetchScalarGridSpec(
        num_scalar_prefetch=7,
        grid=(tok // tm,),
        in_specs=[pl.BlockSpec((tm,) + ROW_TILE, lambda i, *_: (i, 0, 0))],
        out_specs=pl.BlockSpec(memory_space=pl.ANY),
        scratch_shapes=[pltpu.VMEM((MOE_ROWS // 2,) + ROW_TILE, F32),
                        pltpu.SemaphoreType.DMA(()), pltpu.SemaphoreType.DMA(())],
    )
    return pl.pallas_call(
        _dispatch_kernel,
        out_shape=jax.ShapeDtypeStruct((n_slots,) + ROW_TILE, F32),
        grid_spec=grid_spec,
        compiler_params=_params("arbitrary"),
        name="moe_dispatch",
    )(e0, e1, r0, r1, pad_start, counts, n_used, h_tiles)


def _expert_kernel(be_ref, nused_ref, next_ref, slot_ref, x_ref, wg_hbm, wu_hbm, wd_hbm, out_ref,
                   wg_s, wu_s, wd_s, wg_buf, wu_buf, wd_buf, sem, *, layer):
    i = pl.program_id(0)

    def weight_copies(e, slot):
        w = layer * N_EXPERTS + e
        return (pltpu.make_async_copy(wg_hbm.at[w], wg_buf.at[slot], sem.at[slot]),
                pltpu.make_async_copy(wu_hbm.at[w], wu_buf.at[slot], sem.at[slot]),
                pltpu.make_async_copy(wd_hbm.at[w], wd_buf.at[slot], sem.at[slot]))

    @pl.when(i == 0)
    def _():
        for cp in weight_copies(be_ref[0], slot_ref[be_ref[0]]):
            cp.start()

    @pl.when(i < nused_ref[0])
    def _():
        @pl.when((i == 0) | (be_ref[i] != be_ref[jnp.maximum(i - 1, 0)]))
        def _():
            e = be_ref[i]
            slot = slot_ref[e]
            for cp in weight_copies(e, slot):
                cp.wait()

            @pl.when(next_ref[e] >= 0)
            def _():
                for cp in weight_copies(next_ref[e], 1 - slot):
                    cp.start()

            wg_s[...] = wg_buf[slot].astype(MXU_DTYPE)
            wu_s[...] = wu_buf[slot].astype(MXU_DTYPE)
            wd_s[...] = wd_buf[slot].astype(MXU_DTYPE)

        x = _tiles_to_rows(x_ref).astype(MXU_DTYPE)
        n = x.shape[0]
        hidden = []
        for c in range(0, D_EXPERT, FFN_COLS):
            gate = jnp.dot(x, wg_s[:, c:c + FFN_COLS], preferred_element_type=F32)
            up = jnp.dot(x, wu_s[:, c:c + FFN_COLS], preferred_element_type=F32)
            hidden.append((gate * _sigmoid(gate) * up).astype(MXU_DTYPE))
        hidden = jnp.concatenate(hidden, axis=1)
        for c in range(0, D_MODEL, FFN_COLS):
            out = jnp.dot(hidden, wd_s[:, c:c + FFN_COLS], preferred_element_type=F32)
            for s in range(FFN_COLS // LANES):
                out_ref[pl.ds(c // LANES + s, n, stride=ROW_TILE[0]), :] = out[:, s * LANES:(s + 1) * LANES]

    @pl.when(i >= nused_ref[0])
    def _():
        out_ref[...] = jnp.zeros(out_ref.shape, F32)


def _expert_ffn(xs_tiles, block_expert, n_used, counts, w_gate, w_up, w_down, layer):
    n_slots = xs_tiles.shape[0]
    n_blocks = block_expert.shape[0]
    rows = MOE_ROWS * ROW_TILE[0]
    experts = jnp.arange(N_EXPERTS, dtype=jnp.int32)
    nonempty = counts > 0
    later = (experts[None, :] > experts[:, None]) & nonempty[None, :]
    next_expert = jnp.min(jnp.where(later, experts[None, :], N_EXPERTS), axis=1)
    next_expert = jnp.where(next_expert < N_EXPERTS, next_expert, -1).astype(jnp.int32)
    buf_half = ((jnp.cumsum(nonempty.astype(jnp.int32)) - 1) % 2).astype(jnp.int32)
    blk = lambda i, be, nu, nx, sl: (jnp.maximum(jnp.minimum(i, nu[0] - 1), 0), 0)
    grid_spec = pltpu.PrefetchScalarGridSpec(
        num_scalar_prefetch=4,
        grid=(n_blocks,),
        in_specs=[
            pl.BlockSpec((rows, LANES), blk),
            pl.BlockSpec(memory_space=pl.ANY),
            pl.BlockSpec(memory_space=pl.ANY),
            pl.BlockSpec(memory_space=pl.ANY),
        ],
        out_specs=pl.BlockSpec((rows, LANES), lambda i, be, nu, nx, sl: (i, 0)),
        scratch_shapes=[pltpu.VMEM((D_MODEL, D_EXPERT), MXU_DTYPE), pltpu.VMEM((D_MODEL, D_EXPERT), MXU_DTYPE),
                        pltpu.VMEM((D_EXPERT, D_MODEL), MXU_DTYPE),
                        pltpu.VMEM((2, D_MODEL, D_EXPERT), F32), pltpu.VMEM((2, D_MODEL, D_EXPERT), F32),
                        pltpu.VMEM((2, D_EXPERT, D_MODEL), F32), pltpu.SemaphoreType.DMA((2,))],
    )
    out = pl.pallas_call(
        functools.partial(_expert_kernel, layer=layer),
        out_shape=jax.ShapeDtypeStruct((n_slots * ROW_TILE[0], LANES), F32),
        grid_spec=grid_spec,
        compiler_params=_params("arbitrary"),
        name="moe_experts",
    )(block_expert, n_used, next_expert, buf_half, xs_tiles.reshape(n_slots * ROW_TILE[0], LANES),
      w_gate, w_up, w_down)
    return out.reshape((n_slots,) + ROW_TILE)


def _combine_ln_kernel(e0_ref, e1_ref, r0_ref, r1_ref, start_ref, rows_hbm, h_ref, gate_ref, g_ref, b_ref, o_ref,
                       buf, sem):
    tm = h_ref.shape[0]
    i = pl.program_id(0)

    def gather(tile, p):
        base = tile * tm

        def issue(r, c):
            t = base + r
            dst = pl.ds(pl.multiple_of(r * ROW_TILE[0], ROW_TILE[0]), ROW_TILE[0])
            pltpu.make_async_copy(rows_hbm.at[start_ref[e0_ref[t]] + r0_ref[t]], buf.at[p, 0, dst],
                                  sem.at[p]).start(priority=0)
            pltpu.make_async_copy(rows_hbm.at[start_ref[e1_ref[t]] + r1_ref[t]], buf.at[p, 1, dst],
                                  sem.at[p]).start(priority=1)
            return c

        lax.fori_loop(0, tm, issue, 0, unroll=8)

    @pl.when(i == 0)
    def _():
        gather(0, 0)

    for p in range(2):
        @pl.when(i % 2 == p)
        def _(p=p):
            @pl.when(i + 1 < pl.num_programs(0))
            def _():
                gather(i + 1, 1 - p)

            for k in range(TOP_K):
                pltpu.make_async_copy(buf.at[1 - p, k], buf.at[p, k], sem.at[p]).wait()
            gate = gate_ref[...]
            acc = (ALPHA * h_ref[...] + gate[:, 0:1] * _tiles_to_rows(buf.at[p, 0])
                   + gate[:, 1:2] * _tiles_to_rows(buf.at[p, 1]))
            o_ref[...] = _layer_norm_rows(acc, g_ref[...], b_ref[...])


def _combine_ln(e0, e1, r0, r1, pad_start, rows_tiles, h, gates, g_row, b_row):
    tok = h.shape[0]
    tm = TM_COMBINE
    row = lambda i, *_: (i, 0)
    const = lambda i, *_: (0, 0)
    grid_spec = pltpu.PrefetchScalarGridSpec(
        num_scalar_prefetch=5,
        grid=(tok // tm,),
        in_specs=[
            pl.BlockSpec(memory_space=pl.ANY),
            pl.BlockSpec((tm, D_MODEL), row),
            pl.BlockSpec((tm, LANES), row),
            pl.BlockSpec((1, D_MODEL), const),
            pl.BlockSpec((1, D_MODEL), const),
        ],
        out_specs=pl.BlockSpec((tm, D_MODEL), row),
        scratch_shapes=[pltpu.VMEM((2, TOP_K, tm * ROW_TILE[0], LANES), F32), pltpu.SemaphoreType.DMA((2,))],
    )
    return pl.pallas_call(
        _combine_ln_kernel,
        out_shape=jax.ShapeDtypeStruct((tok, D_MODEL), F32),
        grid_spec=grid_spec,
        compiler_params=_params("arbitrary"),
        name="moe_combine_ln",
    )(e0, e1, r0, r1, pad_start, rows_tiles, h, gates, g_row, b_row)


def _slot_layout(counts, n_blocks):
    padded = (counts + MOE_ROWS - 1) // MOE_ROWS * MOE_ROWS
    pad_end = jnp.cumsum(padded)
    n_used = pad_end[-1:] // MOE_ROWS
    blocks = jnp.arange(n_blocks, dtype=jnp.int32)
    first_row = jnp.minimum(blocks, n_used - 1) * MOE_ROWS
    block_expert = jnp.sum(first_row[:, None] >= pad_end[None, :], axis=1)
    return (pad_end - padded).astype(jnp.int32), block_expert.astype(jnp.int32), n_used.astype(jnp.int32)


def _moe_ln(h, h_tiles, w_router, b_router, w_gate, w_up, w_down, layer, g_row, b_row):
    tok = h.shape[0]
    n_blocks = tok * TOP_K // MOE_ROWS + N_EXPERTS
    ids, gates, counts = _router(h, w_router, b_router)
    e0, e1, r0, r1 = (ids[c] for c in range(4))
    counts = counts[0, MOE_GROUPS:MOE_GROUPS + N_EXPERTS]
    pad_start, block_expert, n_used = _slot_layout(counts, n_blocks)
    xs = _dispatch(h_tiles.reshape((tok,) + ROW_TILE), e0, e1, r0, r1, pad_start, counts, n_used,
                   n_blocks * MOE_ROWS)
    rows = _expert_ffn(xs, block_expert, n_used, counts, w_gate, w_up, w_down, layer)
    return _combine_ln(e0, e1, r0, r1, pad_start, rows, h, gates, g_row, b_row)


def _pad_lanes(v):
    return jnp.pad(v, (0, LANES - v.shape[0])).reshape(1, LANES)


def _even_mixer(h, bsz, seq, w_in, conv_a, conv_w, conv_b, dt_bias, a_log, d_skip, norm_w, w_out, g_row, b_row):
    w = jnp.pad(w_in, ((0, 0), (0, AB_PROJ - w_in.shape[1]))).astype(MXU_DTYPE)
    y_a, z, xbc, dt, acs, acst = _even_front(
        h, w, conv_a, conv_w, conv_b.reshape(1, -1), _pad_lanes(dt_bias), _pad_lanes(-jnp.exp(a_log)), bsz, seq)
    dskip_row = jnp.repeat(d_skip, SSM_HEAD_DIM).reshape(1, -1)
    y_b = _ssd(xbc, dt, acs, acst, z, dskip_row, norm_w.reshape(1, -1), bsz, seq)
    return _outproj_ln([y_a, y_b], h, w_out.astype(MXU_DTYPE), g_row, b_row)


def _odd_mixer(h, bsz, seq, w_in, i_bias, f_bias, hnorm_w, fox_f_bias, w_out, g_row, b_row):
    c = np.cumsum((0, MLSTM_W, MLSTM_W, MLSTM_W, MLSTM_HEADS, MLSTM_HEADS, MLSTM_W, FOX_W, FOX_W, FOX_W, FOX_HEADS))
    part = lambda j: w_in[:, c[j]:c[j + 1]]
    q, k, v, i_pre, f_pre, o_pre, fq, fk, fv, ff = (part(j) for j in range(10))
    gate_cols = jnp.concatenate([i_pre, f_pre, ff, f_pre], axis=1)
    gate_cols = jnp.pad(gate_cols, ((0, 0), (0, LANES - gate_cols.shape[1])))
    spread = lambda m: jnp.pad(m.reshape(-1, FOX_HEADS, FOX_HEAD_DIM),
                               ((0, 0), (0, 0), (0, LANES - FOX_HEAD_DIM))).reshape(-1, FOX_AUG)
    w = jnp.concatenate([q, k, v, o_pre, fq, spread(fk), fv, gate_cols], axis=1).astype(MXU_DTYPE)
    gate_bias = _pad_lanes(jnp.concatenate([i_bias, f_bias, fox_f_bias, f_bias]))
    q, k, v, o, fq, fk_aug, fvt_aug, gates, gates_t = _odd_front(h, w, gate_bias, bsz, seq)
    y_c = _mlstm(q, k, v, o, gates, gates_t, hnorm_w.reshape(1, -1), bsz, seq)
    y_d = _fox(fq, fk_aug, fvt_aug, gates_t, bsz, seq)
    return _outproj_ln([y_c, y_d], h, w_out.astype(MXU_DTYPE), g_row, b_row)


def kernel(x, ab_w_in, ab_conv_a, ab_conv_ssm_w, ab_conv_ssm_b, ab_dt_bias, ab_a_log, ab_d_skip, ab_norm_w, ab_w_out, cd_w_in, cd_i_bias, cd_f_bias, cd_hnorm_w, cd_fox_f_bias, cd_w_out, ln1_g, ln1_b, ln2_g, ln2_b, moe_rg_w, moe_rg_b, moe_re_w, moe_re_b, moe_w_gate, moe_w_up, moe_w_down):
    bsz, seq, d = x.shape
    h = x.reshape(bsz * seq, d)
    stack = lambda w: w.reshape((w.shape[0] * w.shape[1],) + w.shape[2:])
    w_gate, w_up, w_down = stack(moe_w_gate), stack(moe_w_up), stack(moe_w_down)
    for layer in range(DEPTH):
        j = layer // 2
        g1, b1 = ln1_g[layer].reshape(1, -1), ln1_b[layer].reshape(1, -1)
        if layer % 2 == 0:
            h, h_tiles = _even_mixer(h, bsz, seq, ab_w_in[j], ab_conv_a[j], ab_conv_ssm_w[j], ab_conv_ssm_b[j],
                                     ab_dt_bias[j], ab_a_log[j], ab_d_skip[j], ab_norm_w[j], ab_w_out[j], g1, b1)
        else:
            h, h_tiles = _odd_mixer(h, bsz, seq, cd_w_in[j], cd_i_bias[j], cd_f_bias[j], cd_hnorm_w[j],
                                    cd_fox_f_bias[j], cd_w_out[j], g1, b1)
        re_w = jnp.transpose(moe_re_w[layer], (1, 0, 2)).reshape(d, N_EXPERTS)
        w_router = jnp.pad(jnp.concatenate([moe_rg_w[layer], re_w], axis=1),
                           ((0, 0), (0, LANES - MOE_GROUPS - N_EXPERTS)))
        b_router = _pad_lanes(jnp.concatenate([moe_rg_b[layer], moe_re_b[layer].reshape(-1)]))
        h = _moe_ln(h, h_tiles, w_router, b_router, w_gate, w_up, w_down, layer,
                    ln2_g[layer].reshape(1, -1), ln2_b[layer].reshape(1, -1))
    return h.reshape(bsz, seq, d)
```

```python
import functools

import numpy as np
import jax
import jax.numpy as jnp
from jax import lax
from jax.experimental import pallas as pl
from jax.experimental.pallas import tpu as pltpu

F32 = jnp.float32
MXU_DTYPE = jnp.bfloat16
HIGHEST = lax.Precision.HIGHEST

D_MODEL = 1024
DEPTH = 4
ALPHA = (2 * DEPTH) ** 0.25
LN_EPS = 1e-5
CONV_DIM = D_MODEL // 2
CONV_WIDTH = 3
SSM_D_INNER = D_MODEL
SSM_HEAD_DIM = 64
SSM_HEADS = SSM_D_INNER // SSM_HEAD_DIM
SSM_GROUPS = 4
SSM_STATE = 64
SSM_CONV = 4
SSM_BC = SSM_GROUPS * SSM_STATE
SSM_CONV_DIM = SSM_D_INNER + 2 * SSM_BC
MLSTM_HEADS = 4
MLSTM_HEAD_DIM = D_MODEL // 8
MLSTM_W = MLSTM_HEADS * MLSTM_HEAD_DIM
FOX_HEADS = 8
FOX_HEAD_DIM = D_MODEL // 16
FOX_W = FOX_HEADS * FOX_HEAD_DIM
MOE_GROUPS = 4
EXPERTS_PER_GROUP = 8
N_EXPERTS = MOE_GROUPS * EXPERTS_PER_GROUP
TOP_K = 2
D_EXPERT = D_MODEL // 2

LANES = 128
SUBLANES = 8
VMEM_LIMIT_BYTES = 56 * 1024 * 1024

CHUNK = 128
SEQ_PAIR = 4
MLSTM_SEQS = 1
MLSTM_CHUNK = 512
TM_FRONT = 512
TM_EVEN_FRONT = 256
CONV_ROWS, CONV_LANES = 64, 512
TM_OUT = 1024
TM_ROUTER = 512
TQ_FOX = 256
MOE_ROWS = 512
FFN_COLS = 256
TM_COMBINE = 512
TM_DISPATCH = 2048

AB_PROJ = 4224
FOX_AUG = FOX_HEADS * LANES
CD_FK = 5 * 512
CD_FV = CD_FK + FOX_AUG
CD_GATES = CD_FV + FOX_W
CD_PROJ = CD_GATES + LANES
LOG2E = 1.4426950408889634
TK_FOX = 128
FOX_ACC_ROWS = FOX_HEAD_DIM + SUBLANES
FOX_VROWS = LANES
G_I, G_F, G_FOX, G_BCUM = 0, 4, 8, 16


def _params(*sem):
    return pltpu.CompilerParams(dimension_semantics=sem, vmem_limit_bytes=VMEM_LIMIT_BYTES)


def _softplus(x):
    return jnp.maximum(x, 0.0) + jnp.log(1.0 + jnp.exp(-jnp.abs(x)))


def _sigmoid(x):
    return 1.0 / (1.0 + jnp.exp(-x))


def _layer_norm_rows(v, g, b):
    mu = jnp.mean(v, axis=-1, keepdims=True)
    c = v - mu
    var = jnp.mean(c * c, axis=-1, keepdims=True)
    return c * lax.rsqrt(var + LN_EPS) * g + b


def _tril(n, block):
    i = np.arange(n)
    m = (i[:, None] >= i[None, :]) & (i[:, None] // block == i[None, :] // block)
    return jnp.asarray(m.astype(np.float32), MXU_DTYPE)


def _full(shape):
    return pl.BlockSpec(shape, lambda *_: (0,) * len(shape), pipeline_mode=pl.Buffered(1))


ROW_TILE = (D_MODEL // LANES, LANES)


def _tiles_to_rows(ref):
    n = ref.shape[0] // ROW_TILE[0]
    return jnp.concatenate([ref[pl.ds(s, n, stride=ROW_TILE[0]), :] for s in range(ROW_TILE[0])], axis=1)


def _rows_to_tiles(ref, val):
    n = val.shape[0]
    for s in range(ROW_TILE[0]):
        ref[pl.ds(s, n, stride=ROW_TILE[0]), :] = val[:, s * LANES:(s + 1) * LANES]


def _split3(x):
    narrow = lambda v: v.astype(MXU_DTYPE).astype(F32)
    x1 = narrow(x)
    x2 = narrow(x - x1)
    return x1, x2, narrow(x - x1 - x2)


def _cumsum_rows(tril, x):
    parts = jnp.dot(tril, jnp.concatenate(_split3(x), axis=1).astype(MXU_DTYPE), preferred_element_type=F32)
    return parts[:, 0:LANES] + parts[:, LANES:2 * LANES] + parts[:, 2 * LANES:3 * LANES]


def _even_front_kernel(h_ref, w_ref, ca_ref, cw_ref, cb_ref, dtb_ref, aneg_ref, tril_ref,
                       ya_ref, z_ref, xbc_ref, dt_ref, acs_ref, acst_ref,
                       proj_buf, ua_ext, xbc_ext, *, tiles_per_seq):
    tm = h_ref.shape[0]
    i = pl.program_id(0)
    cur = i % 2

    @pl.when(i == 0)
    def _():
        proj_buf[1] = jnp.zeros(proj_buf.shape[1:], F32)

    @pl.when((i == 0) | ((i - 1) % tiles_per_seq == 0))
    def _():
        ua_ext[0:SUBLANES, :] = jnp.zeros((SUBLANES, CONV_DIM), F32)
        xbc_ext[0:SUBLANES, :] = jnp.zeros((SUBLANES, SSM_CONV_DIM), F32)

    z0 = 3 * CONV_DIM
    x0 = z0 + SSM_D_INNER
    d0 = x0 + SSM_CONV_DIM

    x_in = h_ref[...].astype(MXU_DTYPE)
    proj = proj_buf.at[1 - cur]
    z_ref[...] = proj[:, z0:x0]

    n_blocks = tm // CONV_ROWS
    col_cuts = [AB_PROJ * g // n_blocks // LANES * LANES for g in range(n_blocks)] + [AB_PROJ]
    for r0 in range(0, tm, CONV_ROWS):
        c_lo, c_hi = col_cuts[r0 // CONV_ROWS], col_cuts[r0 // CONV_ROWS + 1]
        proj_buf[cur, :, c_lo:c_hi] = jnp.dot(x_in, w_ref[:, c_lo:c_hi], preferred_element_type=F32)
        rows = slice(r0, r0 + CONV_ROWS)
        ext_rows = slice(SUBLANES + r0, SUBLANES + r0 + CONV_ROWS)
        ua_ext[ext_rows, :] = proj[rows, CONV_DIM:2 * CONV_DIM] * proj[rows, 2 * CONV_DIM:3 * CONV_DIM]
        conv = None
        for k in range(CONV_WIDTH):
            tap = ca_ref[k:k + 1, :] * ua_ext[pl.ds(SUBLANES + r0 - (CONV_WIDTH - 1) + k, CONV_ROWS), :]
            conv = tap if conv is None else conv + tap
        ya_ref[rows, :] = (proj[rows, 0:CONV_DIM] * conv).astype(ya_ref.dtype)
        for c0 in range(0, SSM_CONV_DIM, CONV_LANES):
            cols = slice(c0, c0 + CONV_LANES)
            xbc_ext[ext_rows, cols] = proj[rows, x0 + c0:x0 + c0 + CONV_LANES]
            conv = cb_ref[:, cols]
            for k in range(SSM_CONV):
                conv = conv + cw_ref[k:k + 1, cols] * xbc_ext[pl.ds(SUBLANES + r0 - (SSM_CONV - 1) + k, CONV_ROWS), cols]
            xbc_ref[rows, cols] = conv * _sigmoid(conv)
    ua_ext[0:SUBLANES, :] = ua_ext[tm:tm + SUBLANES, :]
    xbc_ext[0:SUBLANES, :] = xbc_ext[tm:tm + SUBLANES, :]

    dt = _softplus(proj[:, d0:d0 + LANES] + dtb_ref[...])
    a = dt * aneg_ref[...]
    acs = _cumsum_rows(tril_ref[...], a)
    dt_ref[...] = dt
    acs_ref[...] = acs
    acst_ref[...] = acs.T


def _even_front(h, w_in, conv_a, conv_w, conv_b, dt_bias_row, aneg_row, bsz, seq):
    tok = bsz * seq
    tm = TM_EVEN_FRONT
    n_tiles = tok // tm
    row = lambda i: (jnp.maximum(i - 1, 0), 0)
    out_shapes = (
        jax.ShapeDtypeStruct((tok, CONV_DIM), MXU_DTYPE),
        jax.ShapeDtypeStruct((tok, SSM_D_INNER), F32),
        jax.ShapeDtypeStruct((tok, SSM_CONV_DIM), F32),
        jax.ShapeDtypeStruct((tok, LANES), F32),
        jax.ShapeDtypeStruct((tok, LANES), F32),
        jax.ShapeDtypeStruct((LANES, tok), F32),
    )
    return pl.pallas_call(
        functools.partial(_even_front_kernel, tiles_per_seq=seq // tm),
        out_shape=out_shapes,
        grid=(n_tiles + 1,),
        in_specs=[
            pl.BlockSpec((tm, D_MODEL), lambda i: (jnp.minimum(i, n_tiles - 1), 0)),
            _full((D_MODEL, AB_PROJ)),
            _full((CONV_WIDTH, CONV_DIM)),
            _full((SSM_CONV, SSM_CONV_DIM)),
            _full((1, SSM_CONV_DIM)),
            _full((1, LANES)),
            _full((1, LANES)),
            _full((tm, tm)),
        ],
        out_specs=(
            pl.BlockSpec((tm, CONV_DIM), row),
            pl.BlockSpec((tm, SSM_D_INNER), row),
            pl.BlockSpec((tm, SSM_CONV_DIM), row),
            pl.BlockSpec((tm, LANES), row),
            pl.BlockSpec((tm, LANES), row),
            pl.BlockSpec((LANES, tm), lambda i: (0, jnp.maximum(i - 1, 0))),
        ),
        scratch_shapes=[
            pltpu.VMEM((2, tm, AB_PROJ), F32),
            pltpu.VMEM((tm + SUBLANES, CONV_DIM), F32),
            pltpu.VMEM((tm + SUBLANES, SSM_CONV_DIM), F32),
        ],
        compiler_params=_params("arbitrary"),
        name="even_front",
    )(h, w_in, conv_a, conv_w, conv_b, dt_bias_row, aneg_row, _tril(tm, CHUNK))


def _bcast_heads(arr, n_heads, width):
    per = LANES // width
    length = arr.shape[0]
    lane = lax.broadcasted_iota(jnp.int32, (length, LANES), 1)
    outs = []
    for j in range(n_heads // per):
        v = jnp.broadcast_to(arr[:, j * per:j * per + 1], (length, LANES))
        for r in range(1, per):
            v = jnp.where(lane >= r * width, jnp.broadcast_to(arr[:, j * per + r:j * per + r + 1], (length, LANES)), v)
        outs.append(v)
    return jnp.concatenate(outs, axis=1)


def _expand_heads(arr, e3_ref):
    hi, lo, _ = _split3(arr)
    return jnp.dot(jnp.concatenate([hi, lo], axis=1).astype(MXU_DTYPE), e3_ref[...],
                   preferred_element_type=F32)


def _ssd_kernel(*refs):
    nb = SEQ_PAIR
    xbc_ref, dt_ref, acs_ref, z_ref = refs[:4]
    acst_refs = refs[4:4 + nb]
    dskip_ref, nw_ref, e3_ref, y_ref, state = refs[4 + nb:]
    L = CHUNK
    P = SSM_HEAD_DIM
    R = SSM_HEADS // SSM_GROUPS
    GW = R * P

    @pl.when(pl.program_id(1) == 0)
    def _():
        state[...] = jnp.zeros(state.shape, F32)

    row = lax.broadcasted_iota(jnp.int32, (L, L), 0)
    col = lax.broadcasted_iota(jnp.int32, (L, L), 1)
    causal = row >= col
    lane_g = lax.broadcasted_iota(jnp.int32, (L, GW), 1)

    per_row = []
    for j in range(nb):
        acs = acs_ref[j]
        per_row += [dt_ref[j], jnp.exp(acs[L - 1:L, :] - acs), jnp.exp(acs)]
    expanded = _expand_heads(jnp.concatenate(per_row, axis=0), e3_ref)

    st_old = [state[c] for c in range(nb * SSM_GROUPS)]
    st_new = []
    for j in range(nb):
        xs = xbc_ref[j, :, 0:SSM_D_INNER]
        bm = xbc_ref[j, :, SSM_D_INNER:SSM_D_INNER + SSM_BC]
        cm = xbc_ref[j, :, SSM_D_INNER + SSM_BC:SSM_CONV_DIM]
        acs = acs_ref[j]
        acst = acst_refs[j][...]
        a_last = acs[L - 1:L, :]
        dtx = expanded[(3 * j) * L:(3 * j + 1) * L, :]
        decx = expanded[(3 * j + 1) * L:(3 * j + 2) * L, :]
        expx = expanded[(3 * j + 2) * L:(3 * j + 3) * L, :]
        xdt = xs * dtx
        xdec = (xdt * decx).astype(MXU_DTYPE)
        xdt_m = xdt.astype(MXU_DTYPE)
        chunk_decay = jnp.exp(jnp.broadcast_to(a_last, (SUBLANES, LANES)))
        cdx = _bcast_heads(chunk_decay, SSM_HEADS, P)[0:1, :]

        bm_t = bm.T.astype(MXU_DTYPE)
        cm_m = cm.astype(MXU_DTYPE)
        bm_m = bm.astype(MXU_DTYPE)

        ys = []
        for g in range(SSM_GROUPS):
            n0 = g * SSM_STATE
            c_g = cm_m[:, n0:n0 + SSM_STATE]
            cb = lax.dot_general(c_g, bm_m[:, n0:n0 + SSM_STATE], (((1,), (1,)), ((), ())),
                                 preferred_element_type=F32)
            ms = []
            for r in range(R):
                hd = g * R + r
                seg = jnp.exp(jnp.where(causal, acs[:, hd:hd + 1] - acst[hd:hd + 1, :], -jnp.inf))
                ms.append((cb * seg).astype(MXU_DTYPE))
            big = jnp.dot(jnp.concatenate(ms, axis=0), xdt_m[:, g * GW:(g + 1) * GW],
                          preferred_element_type=F32)
            y_diag = big[0:L, :]
            for r in range(1, R):
                y_diag = jnp.where(lane_g >= r * P, big[r * L:(r + 1) * L, :], y_diag)
            st = st_old[j * SSM_GROUPS + g]
            y_off = jnp.dot(c_g, st.astype(MXU_DTYPE), preferred_element_type=F32)
            new = jnp.dot(bm_t[n0:n0 + SSM_STATE, :], xdec[:, g * GW:(g + 1) * GW],
                          preferred_element_type=F32)
            st_new.append(st * cdx[:, g * GW:(g + 1) * GW] + new)
            ys.append(y_diag + y_off * expx[:, g * GW:(g + 1) * GW])
        y = jnp.concatenate(ys, axis=1) + dskip_ref[...] * xs
        z = z_ref[j]
        u = y * (z * _sigmoid(z))
        y = u * lax.rsqrt(jnp.mean(u * u, axis=-1, keepdims=True) + LN_EPS) * nw_ref[...]
        y_ref[j] = y.astype(y_ref.dtype)

    for c in range(nb * SSM_GROUPS):
        state[c] = st_new[c]


def _ssd(xbc, dt, acs, acst, z, dskip_row, normw_row, bsz, seq):
    nb = SEQ_PAIR
    nc = seq // CHUNK
    per_seq = lambda a: a.reshape(bsz, seq, a.shape[-1])
    blk = lambda width: pl.BlockSpec((nb, CHUNK, width), lambda g, c: (g, c, 0))
    acst_specs = [pl.BlockSpec((LANES, CHUNK), lambda g, c, j=j: (0, (g * nb + j) * nc + c)) for j in range(nb)]
    expand = np.zeros((LANES, SSM_D_INNER), np.float32)
    for hd in range(SSM_HEADS):
        expand[hd, hd * SSM_HEAD_DIM:(hd + 1) * SSM_HEAD_DIM] = 1.0
    e3 = jnp.asarray(np.concatenate([expand] * 2, axis=0), MXU_DTYPE)
    y = pl.pallas_call(
        _ssd_kernel,
        out_shape=jax.ShapeDtypeStruct((bsz, seq, SSM_D_INNER), MXU_DTYPE),
        grid=(bsz // nb, nc),
        in_specs=[blk(SSM_CONV_DIM), blk(LANES), blk(LANES), blk(SSM_D_INNER)] + acst_specs + [
            _full((1, SSM_D_INNER)),
            _full((1, SSM_D_INNER)),
            _full((2 * LANES, SSM_D_INNER)),
        ],
        out_specs=blk(SSM_D_INNER),
        scratch_shapes=[pltpu.VMEM((nb * SSM_GROUPS, SSM_STATE, SSM_D_INNER // SSM_GROUPS), F32)],
        compiler_params=_params("arbitrary", "arbitrary"),
        name="ssd_scan",
    )(per_seq(xbc), per_seq(dt), per_seq(acs), per_seq(z), *([acst] * nb), dskip_row, normw_row, e3)
    return y.reshape(bsz * seq, SSM_D_INNER)


def _outproj_ln_kernel(*refs, widths):
    n = len(widths)
    parts = refs[:n]
    h_ref, w_ref, g_ref, b_ref, o_ref, ot_ref = refs[n:]
    acc = ALPHA * h_ref[...]
    off = 0
    for p, wd in zip(parts, widths):
        acc = acc + jnp.dot(p[...].astype(MXU_DTYPE), w_ref[off:off + wd, :], preferred_element_type=F32)
        off += wd
    out = _layer_norm_rows(acc, g_ref[...], b_ref[...])
    o_ref[...] = out
    _rows_to_tiles(ot_ref, out)


def _outproj_ln(parts, h, w_out, g_row, b_row):
    tok = h.shape[0]
    tm = TM_OUT
    widths = tuple(p.shape[1] for p in parts)
    row = lambda i: (i, 0)
    return pl.pallas_call(
        functools.partial(_outproj_ln_kernel, widths=widths),
        out_shape=(jax.ShapeDtypeStruct((tok, D_MODEL), F32),
                   jax.ShapeDtypeStruct((tok * ROW_TILE[0], LANES), F32)),
        grid=(tok // tm,),
        in_specs=[pl.BlockSpec((tm, wd), row) for wd in widths] + [
            pl.BlockSpec((tm, D_MODEL), row),
            _full((sum(widths), D_MODEL)),
            _full((1, D_MODEL)),
            _full((1, D_MODEL)),
        ],
        out_specs=(pl.BlockSpec((tm, D_MODEL), row), pl.BlockSpec((tm * ROW_TILE[0], LANES), row)),
        compiler_params=_params("arbitrary"),
        name="outproj_ln",
    )(*parts, h, w_out, g_row, b_row)


def _odd_front_kernel(h_ref, w_ref, gb_ref, tril_ref,
                      q_ref, k_ref, v_ref, o_ref, fq_ref, fk_ref, fvt_ref, g_ref, gt_ref, carry):
    tm = h_ref.shape[0]
    x_in = h_ref[...].astype(MXU_DTYPE)
    proj_cols = lambda lo, hi: jnp.dot(x_in, w_ref[:, lo:hi], preferred_element_type=F32)

    @pl.when(pl.program_id(1) == 0)
    def _():
        carry[...] = jnp.zeros(carry.shape, F32)

    raw = proj_cols(CD_GATES, CD_GATES + LANES) + gb_ref[...]
    lane = lax.broadcasted_iota(jnp.int32, (tm, LANES), 1)
    g = jnp.where(lane < G_F, raw, -_softplus(-raw))
    prev = carry[0:1, :]
    glob = _cumsum_rows(tril_ref[...], g) + prev
    before = []
    for c in range(tm // MLSTM_CHUNK):
        before.append(jnp.broadcast_to(prev, (MLSTM_CHUNK, LANES)))
        prev = glob[(c + 1) * MLSTM_CHUNK - 1:(c + 1) * MLSTM_CHUNK, :]
    carry[...] = jnp.broadcast_to(prev, carry.shape)
    local = glob - jnp.concatenate(before, axis=0)
    out = jnp.where((lane >= G_FOX) & (lane < G_BCUM), glob, jnp.where(lane >= G_BCUM, local, g))
    g_ref[...] = out
    gt_ref[...] = out.T

    q_ref[...] = proj_cols(0, 512)
    k_ref[...] = proj_cols(512, 1024) * (MLSTM_HEAD_DIM ** -0.5)
    v_ref[...] = proj_cols(1024, 1536)
    o_ref[...] = _sigmoid(proj_cols(1536, 2048))
    fq_ref[...] = proj_cols(2048, 2560) * (FOX_HEAD_DIM ** -0.5 * LOG2E)

    is_bias = (lane >= FOX_HEAD_DIM) & (lane < FOX_HEAD_DIM + 3)
    fk = proj_cols(CD_FK, CD_FK + FOX_AUG)
    for hd in range(FOX_HEADS):
        c1, c2, c3 = _split3(out[:, G_FOX + hd:G_FOX + hd + 1] * (-LOG2E))
        bias = jnp.where(lane == FOX_HEAD_DIM, c1, jnp.where(lane == FOX_HEAD_DIM + 1, c2, c3))
        k_h = fk[:, hd * LANES:(hd + 1) * LANES]
        fk_ref[:, hd * LANES:(hd + 1) * LANES] = jnp.where(is_bias, bias, k_h).astype(MXU_DTYPE)
    v_t = proj_cols(CD_FV, CD_FV + FOX_W).T
    extra = jnp.where(lax.broadcasted_iota(jnp.int32, (FOX_VROWS - FOX_HEAD_DIM, tm), 0) == 0, 1.0, 0.0)
    fvt_ref[...] = jnp.concatenate(
        [blk for hd in range(FOX_HEADS) for blk in (v_t[hd * FOX_HEAD_DIM:(hd + 1) * FOX_HEAD_DIM, :], extra)],
        axis=0).astype(MXU_DTYPE)


def _odd_front(h, w_in, gate_bias_row, bsz, seq):
    tok = bsz * seq
    tm = TM_FRONT
    ns = seq // tm
    row = lambda b, s: (b * ns + s, 0)
    col = lambda b, s: (0, b * ns + s)
    wide = jax.ShapeDtypeStruct((tok, 512), F32)
    return pl.pallas_call(
        _odd_front_kernel,
        out_shape=(wide,) * 5 + (jax.ShapeDtypeStruct((tok, FOX_AUG), MXU_DTYPE),
                                 jax.ShapeDtypeStruct((FOX_HEADS * FOX_VROWS, tok), MXU_DTYPE),
                                 jax.ShapeDtypeStruct((tok, LANES), F32),
                                 jax.ShapeDtypeStruct((LANES, tok), F32)),
        grid=(bsz, ns),
        in_specs=[
            pl.BlockSpec((tm, D_MODEL), row),
            _full((D_MODEL, CD_PROJ)),
            _full((1, LANES)),
            _full((tm, tm)),
        ],
        out_specs=(pl.BlockSpec((tm, 512), row),) * 5 + (
            pl.BlockSpec((tm, FOX_AUG), row),
            pl.BlockSpec((FOX_HEADS * FOX_VROWS, tm), col),
            pl.BlockSpec((tm, LANES), row),
            pl.BlockSpec((LANES, tm), col),
        ),
        scratch_shapes=[pltpu.VMEM((SUBLANES, LANES), F32)],
        compiler_params=_params("arbitrary", "arbitrary"),
        name="odd_front",
    )(h, w_in, gate_bias_row, _tril(tm, tm))


def _mlstm_kernel(*refs):
    nb = MLSTM_SEQS
    q_ref, k_ref, v_ref, o_ref, g_ref = refs[:5]
    gt_refs = refs[5:5 + nb]
    nw_ref, y_ref, c_state, m_state = refs[5 + nb:]
    L = MLSTM_CHUNK
    DH = MLSTM_HEAD_DIM

    @pl.when(pl.program_id(1) == 0)
    def _():
        c_state[...] = jnp.zeros(c_state.shape, F32)
        m_state[...] = jnp.zeros(m_state.shape, F32)

    row = lax.broadcasted_iota(jnp.int32, (L, L), 0)
    col = lax.broadcasted_iota(jnp.int32, (L, L), 1)
    causal = row >= col
    ones_col = jnp.where(lax.broadcasted_iota(jnp.int32, (L, DH), 1) == 0, 1.0, 0.0)

    for j in range(nb):
        gates = g_ref[j]
        gates_t = gt_refs[j][...]
        for hd in range(MLSTM_HEADS):
            st = j * MLSTM_HEADS + hd
            sl = slice(hd * DH, (hd + 1) * DH)
            q = q_ref[j, :, sl].astype(MXU_DTYPE)
            k = k_ref[j, :, sl]
            v_ext = jnp.concatenate([v_ref[j, :, sl], ones_col], axis=1).astype(MXU_DTYPE)
            b_col = gates[:, G_BCUM + hd:G_BCUM + hd + 1]
            i_col = gates[:, G_I + hd:G_I + hd + 1]
            b_row = gates_t[G_BCUM + hd:G_BCUM + hd + 1, :]
            i_row = gates_t[G_I + hd:G_I + hd + 1, :]
            m_prev = m_state[st:st + 1, 0:1]
            c_ext = c_state[st]

            d_mat = jnp.where(causal, b_col - b_row + i_row, -jnp.inf)
            inter = b_col + m_prev
            m_t = jnp.maximum(jnp.max(d_mat, axis=-1, keepdims=True), inter)
            s_qk = lax.dot_general(q, k.astype(MXU_DTYPE), (((1,), (1,)), ((), ())), preferred_element_type=F32)
            w_qk = s_qk * jnp.exp(d_mat - m_t)
            s_inter = jnp.exp(inter - m_t)
            num_ext = (jnp.dot(w_qk.astype(MXU_DTYPE), v_ext, preferred_element_type=F32)
                       + s_inter * jnp.dot(q, c_ext.astype(MXU_DTYPE), preferred_element_type=F32))
            den = num_ext[:, DH:DH + 1]
            hval = num_ext[:, 0:DH] / jnp.maximum(jnp.abs(den), jnp.exp(-m_t))

            b_last = b_col[L - 1:L, :]
            g_log = b_last - b_col + i_col
            m_new = jnp.maximum(b_last + m_prev, jnp.max(g_log, axis=0, keepdims=True))
            w_k = jnp.exp(g_log - m_new)
            decay = jnp.exp(b_last + m_prev - m_new)
            kw_t = (k * w_k).T.astype(MXU_DTYPE)
            c_state[st] = decay * c_ext + jnp.dot(kw_t, v_ext, preferred_element_type=F32)
            m_state[st:st + 1, :] = jnp.broadcast_to(m_new, (1, LANES))

            mu = jnp.mean(hval, axis=-1, keepdims=True)
            cen = hval - mu
            var = jnp.mean(cen * cen, axis=-1, keepdims=True)
            y = o_ref[j, :, sl] * (cen * lax.rsqrt(var + LN_EPS) * nw_ref[:, sl])
            y_ref[j, :, sl] = y.astype(y_ref.dtype)


def _mlstm(q, k, v, o, gates, gates_t, hnorm_row, bsz, seq):
    nb = MLSTM_SEQS
    nc = seq // MLSTM_CHUNK
    per_seq = lambda a: a.reshape(bsz, seq, a.shape[-1])
    blk = lambda width: pl.BlockSpec((nb, MLSTM_CHUNK, width), lambda g, c: (g, c, 0))
    gt_specs = [pl.BlockSpec((LANES, MLSTM_CHUNK), lambda g, c, j=j: (0, (g * nb + j) * nc + c))
                for j in range(nb)]
    y = pl.pallas_call(
        _mlstm_kernel,
        out_shape=jax.ShapeDtypeStruct((bsz, seq, MLSTM_W), MXU_DTYPE),
        grid=(bsz // nb, nc),
        in_specs=[blk(MLSTM_W)] * 4 + [blk(LANES)] + gt_specs + [_full((1, MLSTM_W))],
        out_specs=blk(MLSTM_W),
        scratch_shapes=[pltpu.VMEM((nb * MLSTM_HEADS, MLSTM_HEAD_DIM, 2 * MLSTM_HEAD_DIM), F32),
                        pltpu.VMEM((nb * MLSTM_HEADS, LANES), F32)],
        compiler_params=_params("arbitrary", "arbitrary"),
        name="mlstm_scan",
    )(per_seq(q), per_seq(k), per_seq(v), per_seq(o), per_seq(gates), *([gates_t] * nb), hnorm_row)
    return y.reshape(bsz * seq, MLSTM_W)


def _fox_kernel(q_ref, k_ref, vt_ref, gt_ref, y_ref, *scratch):
    acc_refs = scratch[:FOX_HEADS]
    qa_ref = scratch[FOX_HEADS]
    tq = q_ref.shape[0]
    tk = TK_FOX
    dh = FOX_HEAD_DIM
    qi = pl.program_id(1)
    q_t = q_ref[...].T
    bias_rows = jnp.where(lax.broadcasted_iota(jnp.int32, (LANES - dh, tq), 0) < 3, 1.0, 0.0)
    for hd in range(FOX_HEADS):
        qa_ref[hd] = jnp.concatenate([q_t[hd * dh:(hd + 1) * dh, :], bias_rows], axis=0).astype(MXU_DTYPE)
        acc_refs[hd][...] = jnp.zeros(acc_refs[hd].shape, F32)
    cq = gt_ref[G_FOX:G_FOX + FOX_HEADS, :] * LOG2E
    key_pos = lax.broadcasted_iota(jnp.int32, (tk, tq), 0)
    qry_pos = lax.broadcasted_iota(jnp.int32, (tk, tq), 1) + qi * tq
    n_full = qi * (tq // tk)

    def block(j, ms, masked):
        k0 = pl.multiple_of(j * tk, tk)
        out = []
        for hd in range(FOX_HEADS):
            hs = slice(hd * LANES, (hd + 1) * LANES)
            s = jnp.dot(k_ref[pl.ds(k0, tk), hs], qa_ref[hd], preferred_element_type=F32)
            if masked:
                s = jnp.where(key_pos + k0 <= qry_pos, s, -jnp.inf)
            cq_h = cq[hd:hd + 1, :]
            m_new = jnp.maximum(ms[hd], jnp.max(s, axis=0, keepdims=True) + cq_h)
            p = jnp.exp2(s - (m_new - cq_h))
            pv = jnp.dot(vt_ref[hd * FOX_VROWS:(hd + 1) * FOX_VROWS, pl.ds(k0, tk)], p.astype(MXU_DTYPE),
                         preferred_element_type=F32)
            acc_refs[hd][...] = jnp.exp2(ms[hd] - m_new) * acc_refs[hd][...] + pv[0:FOX_ACC_ROWS, :]
            out.append(m_new)
        return tuple(out)

    ms = tuple(jnp.full((1, tq), -jnp.inf, F32) for _ in range(FOX_HEADS))
    ms = lax.fori_loop(0, n_full, functools.partial(block, masked=False), ms)
    for d in range(tq // tk):
        ms = block(n_full + d, ms, masked=True)
    outs = []
    for hd in range(FOX_HEADS):
        acc = acc_refs[hd][...]
        outs.append(acc[0:dh, :] / acc[dh:dh + 1, :])
    y_ref[...] = jnp.concatenate(outs, axis=0).T.astype(y_ref.dtype)


def _fox(fq, fk_aug, fvt_aug, gates_t, bsz, seq):
    tok = bsz * seq
    tq = TQ_FOX
    nq = seq // tq
    return pl.pallas_call(
        _fox_kernel,
        out_shape=jax.ShapeDtypeStruct((tok, FOX_W), MXU_DTYPE),
        grid=(bsz, nq),
        in_specs=[
            pl.BlockSpec((tq, FOX_W), lambda b, i: (b * nq + i, 0)),
            pl.BlockSpec((seq, FOX_AUG), lambda b, i: (b, 0)),
            pl.BlockSpec((FOX_HEADS * FOX_VROWS, seq), lambda b, i: (0, b)),
            pl.BlockSpec((LANES, tq), lambda b, i: (0, b * nq + i)),
        ],
        out_specs=pl.BlockSpec((tq, FOX_W), lambda b, i: (b * nq + i, 0)),
        scratch_shapes=[pltpu.VMEM((FOX_ACC_ROWS, tq), F32)] * FOX_HEADS
                       + [pltpu.VMEM((FOX_HEADS, LANES, tq), MXU_DTYPE)],
        compiler_params=_params("arbitrary", "arbitrary"),
        name="fox_attention",
    )(fq, fk_aug, fvt_aug, gates_t)


def _router_kernel(h_ref, w_ref, b_ref, stril_ref, id_ref, gate_ref, cnt_ref, carry):
    tm = h_ref.shape[0]

    @pl.when(pl.program_id(0) == 0)
    def _():
        carry[...] = jnp.zeros(carry.shape, F32)

    h = h_ref[...]
    h_hi = h.astype(MXU_DTYPE)
    h_lo = (h - h_hi.astype(F32)).astype(MXU_DTYPE)
    both = jnp.dot(h_hi, w_ref[...], preferred_element_type=F32)
    logits = (both[:, 0:LANES] + both[:, LANES:2 * LANES]
              + jnp.dot(h_lo, w_ref[:, 0:LANES], preferred_element_type=F32) + b_ref[...])
    lane = lax.broadcasted_iota(jnp.int32, (tm, LANES), 1).astype(F32)
    neg = -jnp.inf
    first = lambda hit: jnp.min(jnp.where(hit, lane, float(LANES)), axis=-1, keepdims=True)
    gl = jnp.where(lane < MOE_GROUPS, logits, neg)
    g_max = jnp.max(gl, axis=-1, keepdims=True)
    g_idx = first(gl == g_max)
    p_group = 1.0 / jnp.sum(jnp.exp(gl - g_max), axis=-1, keepdims=True)
    e_lo = MOE_GROUPS + g_idx * EXPERTS_PER_GROUP
    el = jnp.where((lane >= e_lo) & (lane < e_lo + EXPERTS_PER_GROUP), logits, neg)
    v1 = jnp.max(el, axis=-1, keepdims=True)
    i1 = first(el == v1)
    el2 = jnp.where(lane == i1, neg, el)
    v2 = jnp.max(el2, axis=-1, keepdims=True)
    i2 = first(el2 == v2)
    t = jnp.exp(v2 - v1)
    w1 = 1.0 / (1.0 + t)
    gate_ref[...] = jnp.where(lane == 0.0, p_group * w1, jnp.where(lane == 1.0, p_group * (t * w1), 0.0))

    hit1 = lane == i1
    hit2 = lane == i2
    sent = jnp.where(hit1 | hit2, 1.0, 0.0)
    before = jnp.dot(stril_ref[...], sent.astype(jnp.bfloat16), preferred_element_type=F32) + carry[0:1, :]
    r1 = jnp.sum(jnp.where(hit1, before, 0.0), axis=-1, keepdims=True)
    r2 = jnp.sum(jnp.where(hit2, before, 0.0), axis=-1, keepdims=True)
    total = before[tm - 1:tm, :] + sent[tm - 1:tm, :]
    carry[...] = jnp.broadcast_to(total, carry.shape)
    cnt_ref[...] = jnp.broadcast_to(total, cnt_ref.shape).astype(jnp.int32)
    ids = jnp.where(lane == 0.0, i1 - MOE_GROUPS, jnp.where(lane == 1.0, i2 - MOE_GROUPS,
                    jnp.where(lane == 2.0, r1, jnp.where(lane == 3.0, r2, 0.0))))
    id_ref[...] = ids.T[0:SUBLANES, :].astype(jnp.int32)


def _router(h, w_router, b_router):
    tok = h.shape[0]
    tm = TM_ROUTER
    row = lambda i: (i, 0)
    idx = np.arange(tm)
    stril = jnp.asarray((idx[:, None] > idx[None, :]).astype(np.float32), jnp.bfloat16)
    w_hi = w_router.astype(MXU_DTYPE)
    w_hi_lo = (w_hi, (w_router - w_hi.astype(F32)).astype(MXU_DTYPE))
    return pl.pallas_call(
        _router_kernel,
        out_shape=(jax.ShapeDtypeStruct((SUBLANES, tok), jnp.int32), jax.ShapeDtypeStruct((tok, LANES), F32),
                   jax.ShapeDtypeStruct((SUBLANES, LANES), jnp.int32)),
        grid=(tok // tm,),
        in_specs=[pl.BlockSpec((tm, D_MODEL), row), _full((D_MODEL, 2 * LANES)), _full((1, LANES)),
                  _full((tm, tm))],
        out_specs=(pl.BlockSpec((SUBLANES, tm), lambda i: (0, i)), pl.BlockSpec((tm, LANES), row),
                   _full((SUBLANES, LANES))),
        scratch_shapes=[pltpu.VMEM((SUBLANES, LANES), F32)],
        compiler_params=_params("arbitrary"),
        name="moe_router",
    )(h, jnp.concatenate([w_hi, w_hi_lo[1]], axis=1), b_router, stril)


def _pad_pieces(n):
    return tuple(1 << b for b in reversed(range((n - 1).bit_length())))


def _dispatch_kernel(d0_ref, d1_ref, start_ref, cnt_ref, nused_ref, h_ref, xs_hbm, zeros, sem, pad_sem):
    tm = h_ref.shape[0]
    i = pl.program_id(0)
    base = i * tm

    def issue(r, c):
        t = base + r
        pltpu.make_async_copy(h_ref.at[r], xs_hbm.at[d0_ref[t]], sem).start(priority=0)
        pltpu.make_async_copy(h_ref.at[r], xs_hbm.at[d1_ref[t]], sem).start(priority=1)
        return c

    lax.fori_loop(0, tm, issue, 0, unroll=8)

    def pad_copies(e, fn):
        cnt = cnt_ref[e]
        n_pad = (MOE_ROWS - cnt % MOE_ROWS) % MOE_ROWS
        first = start_ref[e] + cnt
        for piece in _pad_pieces(MOE_ROWS):
            @pl.when((n_pad & piece) != 0)
            def _(piece=piece):
                off = first + (n_pad & ~(2 * piece - 1))
                fn(pltpu.make_async_copy(zeros.at[pl.ds(0, piece)], xs_hbm.at[pl.ds(off, piece)], pad_sem))

    def tail_copies(blk, fn):
        for part in range(MOE_ROWS // zeros.shape[0]):
            off = blk * MOE_ROWS + part * zeros.shape[0]
            fn(pltpu.make_async_copy(zeros, xs_hbm.at[pl.ds(off, zeros.shape[0])], pad_sem))

    @pl.when(i == 0)
    def _():
        zeros[...] = jnp.zeros(zeros.shape, F32)
        n_blocks = xs_hbm.shape[0] // MOE_ROWS

        def start(e, c):
            pad_copies(e, lambda cp: cp.start())
            return c

        def wait(e, c):
            pad_copies(e, lambda cp: cp.wait())
            return c

        def tail_start(blk, c):
            tail_copies(blk, lambda cp: cp.start())
            return c

        def tail_wait(blk, c):
            tail_copies(blk, lambda cp: cp.wait())
            return c

        lax.fori_loop(0, N_EXPERTS, start, 0)
        lax.fori_loop(nused_ref[0], n_blocks, tail_start, 0)
        lax.fori_loop(0, N_EXPERTS, wait, 0)
        lax.fori_loop(nused_ref[0], n_blocks, tail_wait, 0)

    for _ in range(TOP_K):
        pltpu.make_async_copy(h_ref, xs_hbm.at[pl.ds(0, tm)], sem).wait()


def _dispatch(h_tiles, dest0, dest1, pad_start, counts, n_used, n_slots):
    tok = h_tiles.shape[0]
    tm = TM_DISPATCH
    assert tok % tm == 0, (tok, tm)
    grid_spec = pltpu.PrefetchScalarGridSpec(
        num_scalar_prefetch=5,
        grid=(tok // tm,),
        in_specs=[pl.BlockSpec((tm,) + ROW_TILE, lambda i, *_: (i, 0, 0))],
        out_specs=pl.BlockSpec(memory_space=pl.ANY),
        scratch_shapes=[pltpu.VMEM((MOE_ROWS // 2,) + ROW_TILE, F32),
                        pltpu.SemaphoreType.DMA(()), pltpu.SemaphoreType.DMA(())],
    )
    return pl.pallas_call(
        _dispatch_kernel,
        out_shape=jax.ShapeDtypeStruct((n_slots,) + ROW_TILE, F32),
        grid_spec=grid_spec,
        compiler_params=_params("arbitrary"),
        name="moe_dispatch",
    )(dest0, dest1, pad_start, counts, n_used, h_tiles)


def _expert_kernel(be_ref, nused_ref, next_ref, slot_ref, x_ref, wg_hbm, wu_hbm, wd_hbm, out_ref,
                   wg_s, wu_s, wd_s, wg_buf, wu_buf, wd_buf, sem, *, layer):
    i = pl.program_id(0)

    def weight_copies(e, slot):
        w = layer * N_EXPERTS + e
        return (pltpu.make_async_copy(wg_hbm.at[w], wg_buf.at[slot], sem.at[slot]),
                pltpu.make_async_copy(wu_hbm.at[w], wu_buf.at[slot], sem.at[slot]),
                pltpu.make_async_copy(wd_hbm.at[w], wd_buf.at[slot], sem.at[slot]))

    @pl.when(i == 0)
    def _():
        for cp in weight_copies(be_ref[0], slot_ref[be_ref[0]]):
            cp.start()

    @pl.when(i < nused_ref[0])
    def _():
        @pl.when((i == 0) | (be_ref[i] != be_ref[jnp.maximum(i - 1, 0)]))
        def _():
            e = be_ref[i]
            slot = slot_ref[e]
            for cp in weight_copies(e, slot):
                cp.wait()

            @pl.when(next_ref[e] >= 0)
            def _():
                for cp in weight_copies(next_ref[e], 1 - slot):
                    cp.start()

            wg_s[...] = wg_buf[slot].astype(MXU_DTYPE)
            wu_s[...] = wu_buf[slot].astype(MXU_DTYPE)
            wd_s[...] = wd_buf[slot].astype(MXU_DTYPE)

        x = _tiles_to_rows(x_ref).astype(MXU_DTYPE)
        n = x.shape[0]
        hidden = []
        for c in range(0, D_EXPERT, FFN_COLS):
            gate = jnp.dot(x, wg_s[:, c:c + FFN_COLS], preferred_element_type=F32)
            up = jnp.dot(x, wu_s[:, c:c + FFN_COLS], preferred_element_type=F32)
            hidden.append((gate * _sigmoid(gate) * up).astype(MXU_DTYPE))
        hidden = jnp.concatenate(hidden, axis=1)
        for c in range(0, D_MODEL, FFN_COLS):
            out = jnp.dot(hidden, wd_s[:, c:c + FFN_COLS], preferred_element_type=F32)
            for s in range(FFN_COLS // LANES):
                out_ref[pl.ds(c // LANES + s, n, stride=ROW_TILE[0]), :] = out[:, s * LANES:(s + 1) * LANES]

    @pl.when(i >= nused_ref[0])
    def _():
        out_ref[...] = jnp.zeros(out_ref.shape, F32)


def _expert_ffn(xs_tiles, block_expert, n_used, counts, w_gate, w_up, w_down, layer):
    n_slots = xs_tiles.shape[0]
    n_blocks = block_expert.shape[0]
    rows = MOE_ROWS * ROW_TILE[0]
    experts = jnp.arange(N_EXPERTS, dtype=jnp.int32)
    nonempty = counts > 0
    later = (experts[None, :] > experts[:, None]) & nonempty[None, :]
    next_expert = jnp.min(jnp.where(later, experts[None, :], N_EXPERTS), axis=1)
    next_expert = jnp.where(next_expert < N_EXPERTS, next_expert, -1).astype(jnp.int32)
    buf_half = ((jnp.cumsum(nonempty.astype(jnp.int32)) - 1) % 2).astype(jnp.int32)
    blk = lambda i, be, nu, nx, sl: (jnp.maximum(jnp.minimum(i, nu[0] - 1), 0), 0)
    grid_spec = pltpu.PrefetchScalarGridSpec(
        num_scalar_prefetch=4,
        grid=(n_blocks,),
        in_specs=[
            pl.BlockSpec((rows, LANES), blk),
            pl.BlockSpec(memory_space=pl.ANY),
            pl.BlockSpec(memory_space=pl.ANY),
            pl.BlockSpec(memory_space=pl.ANY),
        ],
        out_specs=pl.BlockSpec((rows, LANES), lambda i, be, nu, nx, sl: (i, 0)),
        scratch_shapes=[pltpu.VMEM((D_MODEL, D_EXPERT), MXU_DTYPE), pltpu.VMEM((D_MODEL, D_EXPERT), MXU_DTYPE),
                        pltpu.VMEM((D_EXPERT, D_MODEL), MXU_DTYPE),
                        pltpu.VMEM((2, D_MODEL, D_EXPERT), F32), pltpu.VMEM((2, D_MODEL, D_EXPERT), F32),
                        pltpu.VMEM((2, D_EXPERT, D_MODEL), F32), pltpu.SemaphoreType.DMA((2,))],
    )
    out = pl.pallas_call(
        functools.partial(_expert_kernel, layer=layer),
        out_shape=jax.ShapeDtypeStruct((n_slots * ROW_TILE[0], LANES), F32),
        grid_spec=grid_spec,
        compiler_params=_params("arbitrary"),
        name="moe_experts",
    )(block_expert, n_used, next_expert, buf_half, xs_tiles.reshape(n_slots * ROW_TILE[0], LANES),
      w_gate, w_up, w_down)
    return out.reshape((n_slots,) + ROW_TILE)


def _combine_ln_kernel(d0_ref, d1_ref, rows_hbm, h_ref, gate_ref, g_ref, b_ref, o_ref, buf, sem):
    tm = h_ref.shape[0]
    i = pl.program_id(0)

    def gather(tile, p):
        base = tile * tm

        def issue(r, c):
            t = base + r
            dst = pl.ds(pl.multiple_of(r * ROW_TILE[0], ROW_TILE[0]), ROW_TILE[0])
            pltpu.make_async_copy(rows_hbm.at[d0_ref[t]], buf.at[p, 0, dst],
                                  sem.at[p]).start(priority=0)
            pltpu.make_async_copy(rows_hbm.at[d1_ref[t]], buf.at[p, 1, dst],
                                  sem.at[p]).start(priority=1)
            return c

        lax.fori_loop(0, tm, issue, 0, unroll=8)

    @pl.when(i == 0)
    def _():
        gather(0, 0)

    for p in range(2):
        @pl.when(i % 2 == p)
        def _(p=p):
            @pl.when(i + 1 < pl.num_programs(0))
            def _():
                gather(i + 1, 1 - p)

            for k in range(TOP_K):
                pltpu.make_async_copy(buf.at[1 - p, k], buf.at[p, k], sem.at[p]).wait()
            gate = gate_ref[...]
            acc = (ALPHA * h_ref[...] + gate[:, 0:1] * _tiles_to_rows(buf.at[p, 0])
                   + gate[:, 1:2] * _tiles_to_rows(buf.at[p, 1]))
            o_ref[...] = _layer_norm_rows(acc, g_ref[...], b_ref[...])


def _combine_ln(dest0, dest1, rows_tiles, h, gates, g_row, b_row):
    tok = h.shape[0]
    tm = TM_COMBINE
    row = lambda i, *_: (i, 0)
    const = lambda i, *_: (0, 0)
    grid_spec = pltpu.PrefetchScalarGridSpec(
        num_scalar_prefetch=2,
        grid=(tok // tm,),
        in_specs=[
            pl.BlockSpec(memory_space=pl.ANY),
            pl.BlockSpec((tm, D_MODEL), row),
            pl.BlockSpec((tm, LANES), row),
            pl.BlockSpec((1, D_MODEL), const),
            pl.BlockSpec((1, D_MODEL), const),
        ],
        out_specs=pl.BlockSpec((tm, D_MODEL), row),
        scratch_shapes=[pltpu.VMEM((2, TOP_K, tm * ROW_TILE[0], LANES), F32), pltpu.SemaphoreType.DMA((2,))],
    )
    return pl.pallas_call(
        _combine_ln_kernel,
        out_shape=jax.ShapeDtypeStruct((tok, D_MODEL), F32),
        grid_spec=grid_spec,
        compiler_params=_params("arbitrary"),
        name="moe_combine_ln",
    )(dest0, dest1, rows_tiles, h, gates, g_row, b_row)


def _slot_layout(counts, n_blocks):
    padded = (counts + MOE_ROWS - 1) // MOE_ROWS * MOE_ROWS
    pad_end = jnp.cumsum(padded)
    n_used = pad_end[-1:] // MOE_ROWS
    blocks = jnp.arange(n_blocks, dtype=jnp.int32)
    first_row = jnp.minimum(blocks, n_used - 1) * MOE_ROWS
    block_expert = jnp.sum(first_row[:, None] >= pad_end[None, :], axis=1)
    return (pad_end - padded).astype(jnp.int32), block_expert.astype(jnp.int32), n_used.astype(jnp.int32)


def _moe_ln(h, h_tiles, w_router, b_router, w_gate, w_up, w_down, layer, g_row, b_row):
    tok = h.shape[0]
    n_blocks = tok * TOP_K // MOE_ROWS + N_EXPERTS
    ids, gates, counts = _router(h, w_router, b_router)
    e0, e1, r0, r1 = (ids[c] for c in range(4))
    counts = counts[0, MOE_GROUPS:MOE_GROUPS + N_EXPERTS]
    pad_start, block_expert, n_used = _slot_layout(counts, n_blocks)
    dest0, dest1 = pad_start[e0] + r0, pad_start[e1] + r1
    xs = _dispatch(h_tiles.reshape((tok,) + ROW_TILE), dest0, dest1, pad_start, counts, n_used,
                   n_blocks * MOE_ROWS)
    rows = _expert_ffn(xs, block_expert, n_used, counts, w_gate, w_up, w_down, layer)
    return _combine_ln(dest0, dest1, rows, h, gates, g_row, b_row)


def _pad_lanes(v):
    return jnp.pad(v, (0, LANES - v.shape[0])).reshape(1, LANES)


def _even_mixer(h, bsz, seq, w_in, conv_a, conv_w, conv_b, dt_bias, a_log, d_skip, norm_w, w_out, g_row, b_row):
    w = jnp.pad(w_in, ((0, 0), (0, AB_PROJ - w_in.shape[1]))).astype(MXU_DTYPE)
    y_a, z, xbc, dt, acs, acst = _even_front(
        h, w, conv_a, conv_w, conv_b.reshape(1, -1), _pad_lanes(dt_bias), _pad_lanes(-jnp.exp(a_log)), bsz, seq)
    dskip_row = jnp.repeat(d_skip, SSM_HEAD_DIM).reshape(1, -1)
    y_b = _ssd(xbc, dt, acs, acst, z, dskip_row, norm_w.reshape(1, -1), bsz, seq)
    return _outproj_ln([y_a, y_b], h, w_out.astype(MXU_DTYPE), g_row, b_row)


def _odd_mixer(h, bsz, seq, w_in, i_bias, f_bias, hnorm_w, fox_f_bias, w_out, g_row, b_row):
    c = np.cumsum((0, MLSTM_W, MLSTM_W, MLSTM_W, MLSTM_HEADS, MLSTM_HEADS, MLSTM_W, FOX_W, FOX_W, FOX_W, FOX_HEADS))
    part = lambda j: w_in[:, c[j]:c[j + 1]]
    q, k, v, i_pre, f_pre, o_pre, fq, fk, fv, ff = (part(j) for j in range(10))
    gate_cols = jnp.concatenate([i_pre, f_pre, ff, f_pre], axis=1)
    gate_cols = jnp.pad(gate_cols, ((0, 0), (0, LANES - gate_cols.shape[1])))
    spread = lambda m: jnp.pad(m.reshape(-1, FOX_HEADS, FOX_HEAD_DIM),
                               ((0, 0), (0, 0), (0, LANES - FOX_HEAD_DIM))).reshape(-1, FOX_AUG)
    w = jnp.concatenate([q, k, v, o_pre, fq, spread(fk), fv, gate_cols], axis=1).astype(MXU_DTYPE)
    gate_bias = _pad_lanes(jnp.concatenate([i_bias, f_bias, fox_f_bias, f_bias]))
    q, k, v, o, fq, fk_aug, fvt_aug, gates, gates_t = _odd_front(h, w, gate_bias, bsz, seq)
    y_c = _mlstm(q, k, v, o, gates, gates_t, hnorm_w.reshape(1, -1), bsz, seq)
    y_d = _fox(fq, fk_aug, fvt_aug, gates_t, bsz, seq)
    return _outproj_ln([y_c, y_d], h, w_out.astype(MXU_DTYPE), g_row, b_row)


def kernel(x, ab_w_in, ab_conv_a, ab_conv_ssm_w, ab_conv_ssm_b, ab_dt_bias, ab_a_log, ab_d_skip, ab_norm_w, ab_w_out, cd_w_in, cd_i_bias, cd_f_bias, cd_hnorm_w, cd_fox_f_bias, cd_w_out, ln1_g, ln1_b, ln2_g, ln2_b, moe_rg_w, moe_rg_b, moe_re_w, moe_re_b, moe_w_gate, moe_w_up, moe_w_down):
    bsz, seq, d = x.shape
    h = x.reshape(bsz * seq, d)
    stack = lambda w: w.reshape((w.shape[0] * w.shape[1],) + w.shape[2:])
    w_gate, w_up, w_down = stack(moe_w_gate), stack(moe_w_up), stack(moe_w_down)
    for layer in range(DEPTH):
        j = layer // 2
        g1, b1 = ln1_g[layer].reshape(1, -1), ln1_b[layer].reshape(1, -1)
        if layer % 2 == 0:
            h, h_tiles = _even_mixer(h, bsz, seq, ab_w_in[j], ab_conv_a[j], ab_conv_ssm_w[j], ab_conv_ssm_b[j],
                                     ab_dt_bias[j], ab_a_log[j], ab_d_skip[j], ab_norm_w[j], ab_w_out[j], g1, b1)
        else:
            h, h_tiles = _odd_mixer(h, bsz, seq, cd_w_in[j], cd_i_bias[j], cd_f_bias[j], cd_hnorm_w[j],
                                    cd_fox_f_bias[j], cd_w_out[j], g1, b1)
        re_w = jnp.transpose(moe_re_w[layer], (1, 0, 2)).reshape(d, N_EXPERTS)
        w_router = jnp.pad(jnp.concatenate([moe_rg_w[layer], re_w], axis=1),
                           ((0, 0), (0, LANES - MOE_GROUPS - N_EXPERTS)))
        b_router = _pad_lanes(jnp.concatenate([moe_rg_b[layer], moe_re_b[layer].reshape(-1)]))
        h = _moe_ln(h, h_tiles, w_router, b_router, w_gate, w_up, w_down, layer,
                    ln2_g[layer].reshape(1, -1), ln2_b[layer].reshape(1, -1))
    return h.reshape(bsz, seq, d)
```

```python
import functools

import numpy as np
import jax
import jax.numpy as jnp
from jax import lax
from jax.experimental import pallas as pl
from jax.experimental.pallas import tpu as pltpu

F32 = jnp.float32
MXU_DTYPE = jnp.bfloat16
HIGHEST = lax.Precision.HIGHEST

D_MODEL = 1024
DEPTH = 4
ALPHA = (2 * DEPTH) ** 0.25
LN_EPS = 1e-5
CONV_DIM = D_MODEL // 2
CONV_WIDTH = 3
SSM_D_INNER = D_MODEL
SSM_HEAD_DIM = 64
SSM_HEADS = SSM_D_INNER // SSM_HEAD_DIM
SSM_GROUPS = 4
SSM_STATE = 64
SSM_CONV = 4
SSM_BC = SSM_GROUPS * SSM_STATE
SSM_CONV_DIM = SSM_D_INNER + 2 * SSM_BC
MLSTM_HEADS = 4
MLSTM_HEAD_DIM = D_MODEL // 8
MLSTM_W = MLSTM_HEADS * MLSTM_HEAD_DIM
FOX_HEADS = 8
FOX_HEAD_DIM = D_MODEL // 16
FOX_W = FOX_HEADS * FOX_HEAD_DIM
MOE_GROUPS = 4
EXPERTS_PER_GROUP = 8
N_EXPERTS = MOE_GROUPS * EXPERTS_PER_GROUP
TOP_K = 2
D_EXPERT = D_MODEL // 2

LANES = 128
SUBLANES = 8
VMEM_LIMIT_BYTES = 56 * 1024 * 1024

CHUNK = 128
SEQ_PAIR = 4
MLSTM_SEQS = 1
MLSTM_CHUNK = 512
TM_FRONT = 512
TM_EVEN_FRONT = 256
CONV_ROWS, CONV_LANES = 64, 512
TM_OUT = 1024
TM_ROUTER = 512
TQ_FOX = 256
MOE_ROWS = 512
FFN_COLS = 256
TM_COMBINE = 512
TM_DISPATCH = 2048

AB_PROJ = 4224
FOX_AUG = FOX_HEADS * LANES
CD_FK = 5 * 512
CD_FV = CD_FK + FOX_AUG
CD_GATES = CD_FV + FOX_W
CD_PROJ = CD_GATES + LANES
LOG2E = 1.4426950408889634
TK_FOX = 128
FOX_ACC_ROWS = FOX_HEAD_DIM + SUBLANES
FOX_VROWS = LANES
G_I, G_F, G_FOX, G_BCUM = 0, 4, 8, 16


def _params(*sem):
    return pltpu.CompilerParams(dimension_semantics=sem, vmem_limit_bytes=VMEM_LIMIT_BYTES)


def _softplus(x):
    return jnp.maximum(x, 0.0) + jnp.log(1.0 + jnp.exp(-jnp.abs(x)))


def _sigmoid(x):
    return 1.0 / (1.0 + jnp.exp(-x))


def _layer_norm_rows(v, g, b):
    mu = jnp.mean(v, axis=-1, keepdims=True)
    c = v - mu
    var = jnp.mean(c * c, axis=-1, keepdims=True)
    return c * lax.rsqrt(var + LN_EPS) * g + b


def _tril(n, block):
    i = np.arange(n)
    m = (i[:, None] >= i[None, :]) & (i[:, None] // block == i[None, :] // block)
    return jnp.asarray(m.astype(np.float32), MXU_DTYPE)


def _full(shape):
    return pl.BlockSpec(shape, lambda *_: (0,) * len(shape), pipeline_mode=pl.Buffered(1))


ROW_TILE = (D_MODEL // LANES, LANES)


def _tiles_to_rows(ref):
    n = ref.shape[0] // ROW_TILE[0]
    return jnp.concatenate([ref[pl.ds(s, n, stride=ROW_TILE[0]), :] for s in range(ROW_TILE[0])], axis=1)


def _rows_to_tiles(ref, val):
    n = val.shape[0]
    for s in range(ROW_TILE[0]):
        ref[pl.ds(s, n, stride=ROW_TILE[0]), :] = val[:, s * LANES:(s + 1) * LANES]


def _split3(x):
    narrow = lambda v: v.astype(MXU_DTYPE).astype(F32)
    x1 = narrow(x)
    x2 = narrow(x - x1)
    return x1, x2, narrow(x - x1 - x2)


def _cumsum_rows(tril, x):
    parts = jnp.dot(tril, jnp.concatenate(_split3(x), axis=1).astype(MXU_DTYPE), preferred_element_type=F32)
    return parts[:, 0:LANES] + parts[:, LANES:2 * LANES] + parts[:, 2 * LANES:3 * LANES]


def _even_front_kernel(h_ref, w_ref, ca_ref, cw_ref, cb_ref, dtb_ref, aneg_ref, tril_ref,
                       ya_ref, z_ref, xbc_ref, dt_ref, acs_ref, acst_ref,
                       proj_buf, ua_ext, xbc_ext, *, tiles_per_seq):
    tm = h_ref.shape[0]
    i = pl.program_id(0)
    cur = i % 2

    @pl.when(i == 0)
    def _():
        proj_buf[1] = jnp.zeros(proj_buf.shape[1:], F32)

    @pl.when((i == 0) | ((i - 1) % tiles_per_seq == 0))
    def _():
        ua_ext[0:SUBLANES, :] = jnp.zeros((SUBLANES, CONV_DIM), F32)
        xbc_ext[0:SUBLANES, :] = jnp.zeros((SUBLANES, SSM_CONV_DIM), F32)

    z0 = 3 * CONV_DIM
    x0 = z0 + SSM_D_INNER
    d0 = x0 + SSM_CONV_DIM

    x_in = h_ref[...].astype(MXU_DTYPE)
    proj = proj_buf.at[1 - cur]
    z_ref[...] = proj[:, z0:x0]

    n_blocks = tm // CONV_ROWS
    col_cuts = [AB_PROJ * g // n_blocks // LANES * LANES for g in range(n_blocks)] + [AB_PROJ]
    for r0 in range(0, tm, CONV_ROWS):
        c_lo, c_hi = col_cuts[r0 // CONV_ROWS], col_cuts[r0 // CONV_ROWS + 1]
        proj_buf[cur, :, c_lo:c_hi] = jnp.dot(x_in, w_ref[:, c_lo:c_hi], preferred_element_type=F32)
        rows = slice(r0, r0 + CONV_ROWS)
        ext_rows = slice(SUBLANES + r0, SUBLANES + r0 + CONV_ROWS)
        ua_ext[ext_rows, :] = proj[rows, CONV_DIM:2 * CONV_DIM] * proj[rows, 2 * CONV_DIM:3 * CONV_DIM]
        conv = None
        for k in range(CONV_WIDTH):
            tap = ca_ref[k:k + 1, :] * ua_ext[pl.ds(SUBLANES + r0 - (CONV_WIDTH - 1) + k, CONV_ROWS), :]
            conv = tap if conv is None else conv + tap
        ya_ref[rows, :] = (proj[rows, 0:CONV_DIM] * conv).astype(ya_ref.dtype)
        for c0 in range(0, SSM_CONV_DIM, CONV_LANES):
            cols = slice(c0, c0 + CONV_LANES)
            xbc_ext[ext_rows, cols] = proj[rows, x0 + c0:x0 + c0 + CONV_LANES]
            conv = cb_ref[:, cols]
            for k in range(SSM_CONV):
                conv = conv + cw_ref[k:k + 1, cols] * xbc_ext[pl.ds(SUBLANES + r0 - (SSM_CONV - 1) + k, CONV_ROWS), cols]
            xbc_ref[rows, cols] = conv * _sigmoid(conv)
    ua_ext[0:SUBLANES, :] = ua_ext[tm:tm + SUBLANES, :]
    xbc_ext[0:SUBLANES, :] = xbc_ext[tm:tm + SUBLANES, :]

    dt = _softplus(proj[:, d0:d0 + LANES] + dtb_ref[...])
    a = dt * aneg_ref[...]
    acs = _cumsum_rows(tril_ref[...], a)
    dt_ref[...] = dt
    acs_ref[...] = acs
    acst_ref[...] = acs.T


def _even_front(h, w_in, conv_a, conv_w, conv_b, dt_bias_row, aneg_row, bsz, seq):
    tok = bsz * seq
    tm = TM_EVEN_FRONT
    n_tiles = tok // tm
    row = lambda i: (jnp.maximum(i - 1, 0), 0)
    out_shapes = (
        jax.ShapeDtypeStruct((tok, CONV_DIM), MXU_DTYPE),
        jax.ShapeDtypeStruct((tok, SSM_D_INNER), F32),
        jax.ShapeDtypeStruct((tok, SSM_CONV_DIM), F32),
        jax.ShapeDtypeStruct((tok, LANES), F32),
        jax.ShapeDtypeStruct((tok, LANES), F32),
        jax.ShapeDtypeStruct((LANES, tok), F32),
    )
    return pl.pallas_call(
        functools.partial(_even_front_kernel, tiles_per_seq=seq // tm),
        out_shape=out_shapes,
        grid=(n_tiles + 1,),
        in_specs=[
            pl.BlockSpec((tm, D_MODEL), lambda i: (jnp.minimum(i, n_tiles - 1), 0)),
            _full((D_MODEL, AB_PROJ)),
            _full((CONV_WIDTH, CONV_DIM)),
            _full((SSM_CONV, SSM_CONV_DIM)),
            _full((1, SSM_CONV_DIM)),
            _full((1, LANES)),
            _full((1, LANES)),
            _full((tm, tm)),
        ],
        out_specs=(
            pl.BlockSpec((tm, CONV_DIM), row),
            pl.BlockSpec((tm, SSM_D_INNER), row),
            pl.BlockSpec((tm, SSM_CONV_DIM), row),
            pl.BlockSpec((tm, LANES), row),
            pl.BlockSpec((tm, LANES), row),
            pl.BlockSpec((LANES, tm), lambda i: (0, jnp.maximum(i - 1, 0))),
        ),
        scratch_shapes=[
            pltpu.VMEM((2, tm, AB_PROJ), F32),
            pltpu.VMEM((tm + SUBLANES, CONV_DIM), F32),
            pltpu.VMEM((tm + SUBLANES, SSM_CONV_DIM), F32),
        ],
        compiler_params=_params("arbitrary"),
        name="even_front",
    )(h, w_in, conv_a, conv_w, conv_b, dt_bias_row, aneg_row, _tril(tm, CHUNK))


def _bcast_heads(arr, n_heads, width):
    per = LANES // width
    length = arr.shape[0]
    lane = lax.broadcasted_iota(jnp.int32, (length, LANES), 1)
    outs = []
    for j in range(n_heads // per):
        v = jnp.broadcast_to(arr[:, j * per:j * per + 1], (length, LANES))
        for r in range(1, per):
            v = jnp.where(lane >= r * width, jnp.broadcast_to(arr[:, j * per + r:j * per + r + 1], (length, LANES)), v)
        outs.append(v)
    return jnp.concatenate(outs, axis=1)


def _expand_heads(arr, e3_ref):
    hi, lo, _ = _split3(arr)
    return jnp.dot(jnp.concatenate([hi, lo], axis=1).astype(MXU_DTYPE), e3_ref[...],
                   preferred_element_type=F32)


def _ssd_kernel(*refs):
    nb = SEQ_PAIR
    xbc_ref, dt_ref, acs_ref, z_ref = refs[:4]
    acst_refs = refs[4:4 + nb]
    dskip_ref, nw_ref, e3_ref, y_ref, state = refs[4 + nb:]
    L = CHUNK
    P = SSM_HEAD_DIM
    R = SSM_HEADS // SSM_GROUPS
    GW = R * P

    @pl.when(pl.program_id(1) == 0)
    def _():
        state[...] = jnp.zeros(state.shape, F32)

    row = lax.broadcasted_iota(jnp.int32, (L, L), 0)
    col = lax.broadcasted_iota(jnp.int32, (L, L), 1)
    causal = row >= col
    lane_g = lax.broadcasted_iota(jnp.int32, (L, GW), 1)

    per_row = []
    for j in range(nb):
        acs = acs_ref[j]
        per_row += [dt_ref[j], jnp.exp(acs[L - 1:L, :] - acs), jnp.exp(acs)]
    expanded = _expand_heads(jnp.concatenate(per_row, axis=0), e3_ref)

    st_old = [state[c] for c in range(nb * SSM_GROUPS)]
    st_new = []
    for j in range(nb):
        xs = xbc_ref[j, :, 0:SSM_D_INNER]
        bm = xbc_ref[j, :, SSM_D_INNER:SSM_D_INNER + SSM_BC]
        cm = xbc_ref[j, :, SSM_D_INNER + SSM_BC:SSM_CONV_DIM]
        acs = acs_ref[j]
        acst = acst_refs[j][...]
        a_last = acs[L - 1:L, :]
        dtx = expanded[(3 * j) * L:(3 * j + 1) * L, :]
        decx = expanded[(3 * j + 1) * L:(3 * j + 2) * L, :]
        expx = expanded[(3 * j + 2) * L:(3 * j + 3) * L, :]
        xdt = xs * dtx
        xdec = (xdt * decx).astype(MXU_DTYPE)
        xdt_m = xdt.astype(MXU_DTYPE)
        chunk_decay = jnp.exp(jnp.broadcast_to(a_last, (SUBLANES, LANES)))
        cdx = _bcast_heads(chunk_decay, SSM_HEADS, P)[0:1, :]

        bm_t = bm.T.astype(MXU_DTYPE)
        cm_m = cm.astype(MXU_DTYPE)
        bm_m = bm.astype(MXU_DTYPE)

        ys = []
        for g in range(SSM_GROUPS):
            n0 = g * SSM_STATE
            c_g = cm_m[:, n0:n0 + SSM_STATE]
            cb = lax.dot_general(c_g, bm_m[:, n0:n0 + SSM_STATE], (((1,), (1,)), ((), ())),
                                 preferred_element_type=F32)
            ms = []
            for r in range(R):
                hd = g * R + r
                seg = jnp.exp(jnp.where(causal, acs[:, hd:hd + 1] - acst[hd:hd + 1, :], -jnp.inf))
                ms.append((cb * seg).astype(MXU_DTYPE))
            big = jnp.dot(jnp.concatenate(ms, axis=0), xdt_m[:, g * GW:(g + 1) * GW],
                          preferred_element_type=F32)
            y_diag = big[0:L, :]
            for r in range(1, R):
                y_diag = jnp.where(lane_g >= r * P, big[r * L:(r + 1) * L, :], y_diag)
            st = st_old[j * SSM_GROUPS + g]
            y_off = jnp.dot(c_g, st.astype(MXU_DTYPE), preferred_element_type=F32)
            new = jnp.dot(bm_t[n0:n0 + SSM_STATE, :], xdec[:, g * GW:(g + 1) * GW],
                          preferred_element_type=F32)
            st_new.append(st * cdx[:, g * GW:(g + 1) * GW] + new)
            ys.append(y_diag + y_off * expx[:, g * GW:(g + 1) * GW])
        y = jnp.concatenate(ys, axis=1) + dskip_ref[...] * xs
        z = z_ref[j]
        u = y * (z * _sigmoid(z))
        y = u * lax.rsqrt(jnp.mean(u * u, axis=-1, keepdims=True) + LN_EPS) * nw_ref[...]
        y_ref[j] = y.astype(y_ref.dtype)

    for c in range(nb * SSM_GROUPS):
        state[c] = st_new[c]


def _ssd(xbc, dt, acs, acst, z, dskip_row, normw_row, bsz, seq):
    nb = SEQ_PAIR
    nc = seq // CHUNK
    per_seq = lambda a: a.reshape(bsz, seq, a.shape[-1])
    blk = lambda width: pl.BlockSpec((nb, CHUNK, width), lambda g, c: (g, c, 0))
    acst_specs = [pl.BlockSpec((LANES, CHUNK), lambda g, c, j=j: (0, (g * nb + j) * nc + c)) for j in range(nb)]
    expand = np.zeros((LANES, SSM_D_INNER), np.float32)
    for hd in range(SSM_HEADS):
        expand[hd, hd * SSM_HEAD_DIM:(hd + 1) * SSM_HEAD_DIM] = 1.0
    e3 = jnp.asarray(np.concatenate([expand] * 2, axis=0), MXU_DTYPE)
    y = pl.pallas_call(
        _ssd_kernel,
        out_shape=jax.ShapeDtypeStruct((bsz, seq, SSM_D_INNER), MXU_DTYPE),
        grid=(bsz // nb, nc),
        in_specs=[blk(SSM_CONV_DIM), blk(LANES), blk(LANES), blk(SSM_D_INNER)] + acst_specs + [
            _full((1, SSM_D_INNER)),
            _full((1, SSM_D_INNER)),
            _full((2 * LANES, SSM_D_INNER)),
        ],
        out_specs=blk(SSM_D_INNER),
        scratch_shapes=[pltpu.VMEM((nb * SSM_GROUPS, SSM_STATE, SSM_D_INNER // SSM_GROUPS), F32)],
        compiler_params=_params("arbitrary", "arbitrary"),
        name="ssd_scan",
    )(per_seq(xbc), per_seq(dt), per_seq(acs), per_seq(z), *([acst] * nb), dskip_row, normw_row, e3)
    return y.reshape(bsz * seq, SSM_D_INNER)


def _outproj_ln_kernel(*refs, widths):
    n = len(widths)
    parts = refs[:n]
    h_ref, w_ref, g_ref, b_ref, o_ref, ot_ref = refs[n:]
    acc = ALPHA * h_ref[...]
    off = 0
    for p, wd in zip(parts, widths):
        acc = acc + jnp.dot(p[...].astype(MXU_DTYPE), w_ref[off:off + wd, :], preferred_element_type=F32)
        off += wd
    out = _layer_norm_rows(acc, g_ref[...], b_ref[...])
    o_ref[...] = out
    _rows_to_tiles(ot_ref, out)


def _outproj_ln(parts, h, w_out, g_row, b_row):
    tok = h.shape[0]
    tm = TM_OUT
    widths = tuple(p.shape[1] for p in parts)
    row = lambda i: (i, 0)
    return pl.pallas_call(
        functools.partial(_outproj_ln_kernel, widths=widths),
        out_shape=(jax.ShapeDtypeStruct((tok, D_MODEL), F32),
                   jax.ShapeDtypeStruct((tok * ROW_TILE[0], LANES), F32)),
        grid=(tok // tm,),
        in_specs=[pl.BlockSpec((tm, wd), row) for wd in widths] + [
            pl.BlockSpec((tm, D_MODEL), row),
            _full((sum(widths), D_MODEL)),
            _full((1, D_MODEL)),
            _full((1, D_MODEL)),
        ],
        out_specs=(pl.BlockSpec((tm, D_MODEL), row), pl.BlockSpec((tm * ROW_TILE[0], LANES), row)),
        compiler_params=_params("arbitrary"),
        name="outproj_ln",
    )(*parts, h, w_out, g_row, b_row)


def _odd_front_kernel(h_ref, w_ref, gb_ref, tril_ref,
                      q_ref, k_ref, v_ref, o_ref, fq_ref, fk_ref, fvt_ref, g_ref, gt_ref, carry):
    tm = h_ref.shape[0]
    x_in = h_ref[...].astype(MXU_DTYPE)
    proj_cols = lambda lo, hi: jnp.dot(x_in, w_ref[:, lo:hi], preferred_element_type=F32)

    @pl.when(pl.program_id(1) == 0)
    def _():
        carry[...] = jnp.zeros(carry.shape, F32)

    raw = proj_cols(CD_GATES, CD_GATES + LANES) + gb_ref[...]
    lane = lax.broadcasted_iota(jnp.int32, (tm, LANES), 1)
    g = jnp.where(lane < G_F, raw, -_softplus(-raw))
    prev = carry[0:1, :]
    glob = _cumsum_rows(tril_ref[...], g) + prev
    before = []
    for c in range(tm // MLSTM_CHUNK):
        before.append(jnp.broadcast_to(prev, (MLSTM_CHUNK, LANES)))
        prev = glob[(c + 1) * MLSTM_CHUNK - 1:(c + 1) * MLSTM_CHUNK, :]
    carry[...] = jnp.broadcast_to(prev, carry.shape)
    local = glob - jnp.concatenate(before, axis=0)
    out = jnp.where((lane >= G_FOX) & (lane < G_BCUM), glob, jnp.where(lane >= G_BCUM, local, g))
    g_ref[...] = out
    gt_ref[...] = out.T

    q_ref[...] = proj_cols(0, 512)
    k_ref[...] = proj_cols(512, 1024) * (MLSTM_HEAD_DIM ** -0.5)
    v_ref[...] = proj_cols(1024, 1536)
    o_ref[...] = _sigmoid(proj_cols(1536, 2048))
    fq_ref[...] = proj_cols(2048, 2560) * (FOX_HEAD_DIM ** -0.5 * LOG2E)

    is_bias = (lane >= FOX_HEAD_DIM) & (lane < FOX_HEAD_DIM + 3)
    fk = proj_cols(CD_FK, CD_FK + FOX_AUG)
    for hd in range(FOX_HEADS):
        c1, c2, c3 = _split3(out[:, G_FOX + hd:G_FOX + hd + 1] * (-LOG2E))
        bias = jnp.where(lane == FOX_HEAD_DIM, c1, jnp.where(lane == FOX_HEAD_DIM + 1, c2, c3))
        k_h = fk[:, hd * LANES:(hd + 1) * LANES]
        fk_ref[:, hd * LANES:(hd + 1) * LANES] = jnp.where(is_bias, bias, k_h).astype(MXU_DTYPE)
    v_t = proj_cols(CD_FV, CD_FV + FOX_W).T
    extra = jnp.where(lax.broadcasted_iota(jnp.int32, (FOX_VROWS - FOX_HEAD_DIM, tm), 0) == 0, 1.0, 0.0)
    fvt_ref[...] = jnp.concatenate(
        [blk for hd in range(FOX_HEADS) for blk in (v_t[hd * FOX_HEAD_DIM:(hd + 1) * FOX_HEAD_DIM, :], extra)],
        axis=0).astype(MXU_DTYPE)


def _odd_front(h, w_in, gate_bias_row, bsz, seq):
    tok = bsz * seq
    tm = TM_FRONT
    ns = seq // tm
    row = lambda b, s: (b * ns + s, 0)
    col = lambda b, s: (0, b * ns + s)
    wide = jax.ShapeDtypeStruct((tok, 512), F32)
    return pl.pallas_call(
        _odd_front_kernel,
        out_shape=(wide,) * 5 + (jax.ShapeDtypeStruct((tok, FOX_AUG), MXU_DTYPE),
                                 jax.ShapeDtypeStruct((FOX_HEADS * FOX_VROWS, tok), MXU_DTYPE),
                                 jax.ShapeDtypeStruct((tok, LANES), F32),
                                 jax.ShapeDtypeStruct((LANES, tok), F32)),
        grid=(bsz, ns),
        in_specs=[
            pl.BlockSpec((tm, D_MODEL), row),
            _full((D_MODEL, CD_PROJ)),
            _full((1, LANES)),
            _full((tm, tm)),
        ],
        out_specs=(pl.BlockSpec((tm, 512), row),) * 5 + (
            pl.BlockSpec((tm, FOX_AUG), row),
            pl.BlockSpec((FOX_HEADS * FOX_VROWS, tm), col),
            pl.BlockSpec((tm, LANES), row),
            pl.BlockSpec((LANES, tm), col),
        ),
        scratch_shapes=[pltpu.VMEM((SUBLANES, LANES), F32)],
        compiler_params=_params("arbitrary", "arbitrary"),
        name="odd_front",
    )(h, w_in, gate_bias_row, _tril(tm, tm))


def _mlstm_kernel(*refs):
    nb = MLSTM_SEQS
    q_ref, k_ref, v_ref, o_ref, g_ref = refs[:5]
    gt_refs = refs[5:5 + nb]
    nw_ref, y_ref, c_state, m_state = refs[5 + nb:]
    L = MLSTM_CHUNK
    DH = MLSTM_HEAD_DIM

    @pl.when(pl.program_id(1) == 0)
    def _():
        c_state[...] = jnp.zeros(c_state.shape, F32)
        m_state[...] = jnp.zeros(m_state.shape, F32)

    row = lax.broadcasted_iota(jnp.int32, (L, L), 0)
    col = lax.broadcasted_iota(jnp.int32, (L, L), 1)
    causal = row >= col
    ones_col = jnp.where(lax.broadcasted_iota(jnp.int32, (L, DH), 1) == 0, 1.0, 0.0)

    for j in range(nb):
        gates = g_ref[j]
        gates_t = gt_refs[j][...]
        for hd in range(MLSTM_HEADS):
            st = j * MLSTM_HEADS + hd
            sl = slice(hd * DH, (hd + 1) * DH)
            q = q_ref[j, :, sl].astype(MXU_DTYPE)
            k = k_ref[j, :, sl]
            v_ext = jnp.concatenate([v_ref[j, :, sl], ones_col], axis=1).astype(MXU_DTYPE)
            b_col = gates[:, G_BCUM + hd:G_BCUM + hd + 1]
            i_col = gates[:, G_I + hd:G_I + hd + 1]
            b_row = gates_t[G_BCUM + hd:G_BCUM + hd + 1, :]
            i_row = gates_t[G_I + hd:G_I + hd + 1, :]
            m_prev = m_state[st:st + 1, 0:1]
            c_ext = c_state[st]

            d_mat = jnp.where(causal, b_col - b_row + i_row, -jnp.inf)
            inter = b_col + m_prev
            m_t = jnp.maximum(jnp.max(d_mat, axis=-1, keepdims=True), inter)
            s_qk = lax.dot_general(q, k.astype(MXU_DTYPE), (((1,), (1,)), ((), ())), preferred_element_type=F32)
            w_qk = s_qk * jnp.exp(d_mat - m_t)
            s_inter = jnp.exp(inter - m_t)
            num_ext = (jnp.dot(w_qk.astype(MXU_DTYPE), v_ext, preferred_element_type=F32)
                       + s_inter * jnp.dot(q, c_ext.astype(MXU_DTYPE), preferred_element_type=F32))
            den = num_ext[:, DH:DH + 1]
            hval = num_ext[:, 0:DH] / jnp.maximum(jnp.abs(den), jnp.exp(-m_t))

            b_last = b_col[L - 1:L, :]
            g_log = b_last - b_col + i_col
            m_new = jnp.maximum(b_last + m_prev, jnp.max(g_log, axis=0, keepdims=True))
            w_k = jnp.exp(g_log - m_new)
            decay = jnp.exp(b_last + m_prev - m_new)
            kw_t = (k * w_k).T.astype(MXU_DTYPE)
            c_state[st] = decay * c_ext + jnp.dot(kw_t, v_ext, preferred_element_type=F32)
            m_state[st:st + 1, :] = jnp.broadcast_to(m_new, (1, LANES))

            mu = jnp.mean(hval, axis=-1, keepdims=True)
            cen = hval - mu
            var = jnp.mean(cen * cen, axis=-1, keepdims=True)
            y = o_ref[j, :, sl] * (cen * lax.rsqrt(var + LN_EPS) * nw_ref[:, sl])
            y_ref[j, :, sl] = y.astype(y_ref.dtype)


def _mlstm(q, k, v, o, gates, gates_t, hnorm_row, bsz, seq):
    nb = MLSTM_SEQS
    nc = seq // MLSTM_CHUNK
    per_seq = lambda a: a.reshape(bsz, seq, a.shape[-1])
    blk = lambda width: pl.BlockSpec((nb, MLSTM_CHUNK, width), lambda g, c: (g, c, 0))
    gt_specs = [pl.BlockSpec((LANES, MLSTM_CHUNK), lambda g, c, j=j: (0, (g * nb + j) * nc + c))
                for j in range(nb)]
    y = pl.pallas_call(
        _mlstm_kernel,
        out_shape=jax.ShapeDtypeStruct((bsz, seq, MLSTM_W), MXU_DTYPE),
        grid=(bsz // nb, nc),
        in_specs=[blk(MLSTM_W)] * 4 + [blk(LANES)] + gt_specs + [_full((1, MLSTM_W))],
        out_specs=blk(MLSTM_W),
        scratch_shapes=[pltpu.VMEM((nb * MLSTM_HEADS, MLSTM_HEAD_DIM, 2 * MLSTM_HEAD_DIM), F32),
                        pltpu.VMEM((nb * MLSTM_HEADS, LANES), F32)],
        compiler_params=_params("arbitrary", "arbitrary"),
        name="mlstm_scan",
    )(per_seq(q), per_seq(k), per_seq(v), per_seq(o), per_seq(gates), *([gates_t] * nb), hnorm_row)
    return y.reshape(bsz * seq, MLSTM_W)


def _fox_kernel(q_ref, k_ref, vt_ref, gt_ref, y_ref, *scratch):
    acc_refs = scratch[:FOX_HEADS]
    qa_ref = scratch[FOX_HEADS]
    tq = q_ref.shape[0]
    tk = TK_FOX
    dh = FOX_HEAD_DIM
    qi = pl.program_id(1)
    q_t = q_ref[...].T
    bias_rows = jnp.where(lax.broadcasted_iota(jnp.int32, (LANES - dh, tq), 0) < 3, 1.0, 0.0)
    for hd in range(FOX_HEADS):
        qa_ref[hd] = jnp.concatenate([q_t[hd * dh:(hd + 1) * dh, :], bias_rows], axis=0).astype(MXU_DTYPE)
        acc_refs[hd][...] = jnp.zeros(acc_refs[hd].shape, F32)
    cq = gt_ref[G_FOX:G_FOX + FOX_HEADS, :] * LOG2E
    key_pos = lax.broadcasted_iota(jnp.int32, (tk, tq), 0)
    qry_pos = lax.broadcasted_iota(jnp.int32, (tk, tq), 1) + qi * tq
    n_full = qi * (tq // tk)

    def block(j, ms, masked):
        k0 = pl.multiple_of(j * tk, tk)
        out = []
        for hd in range(FOX_HEADS):
            hs = slice(hd * LANES, (hd + 1) * LANES)
            s = jnp.dot(k_ref[pl.ds(k0, tk), hs], qa_ref[hd], preferred_element_type=F32)
            if masked:
                s = jnp.where(key_pos + k0 <= qry_pos, s, -jnp.inf)
            cq_h = cq[hd:hd + 1, :]
            m_new = jnp.maximum(ms[hd], jnp.max(s, axis=0, keepdims=True) + cq_h)
            p = jnp.exp2(s - (m_new - cq_h))
            pv = jnp.dot(vt_ref[hd * FOX_VROWS:(hd + 1) * FOX_VROWS, pl.ds(k0, tk)], p.astype(MXU_DTYPE),
                         preferred_element_type=F32)
            acc_refs[hd][...] = jnp.exp2(ms[hd] - m_new) * acc_refs[hd][...] + pv[0:FOX_ACC_ROWS, :]
            out.append(m_new)
        return tuple(out)

    ms = tuple(jnp.full((1, tq), -jnp.inf, F32) for _ in range(FOX_HEADS))
    ms = lax.fori_loop(0, n_full, functools.partial(block, masked=False), ms)
    for d in range(tq // tk):
        ms = block(n_full + d, ms, masked=True)
    outs = []
    for hd in range(FOX_HEADS):
        acc = acc_refs[hd][...]
        outs.append(acc[0:dh, :] / acc[dh:dh + 1, :])
    y_ref[...] = jnp.concatenate(outs, axis=0).T.astype(y_ref.dtype)


def _fox(fq, fk_aug, fvt_aug, gates_t, bsz, seq):
    tok = bsz * seq
    tq = TQ_FOX
    nq = seq // tq
    return pl.pallas_call(
        _fox_kernel,
        out_shape=jax.ShapeDtypeStruct((tok, FOX_W), MXU_DTYPE),
        grid=(bsz, nq),
        in_specs=[
            pl.BlockSpec((tq, FOX_W), lambda b, i: (b * nq + i, 0)),
            pl.BlockSpec((seq, FOX_AUG), lambda b, i: (b, 0)),
            pl.BlockSpec((FOX_HEADS * FOX_VROWS, seq), lambda b, i: (0, b)),
            pl.BlockSpec((LANES, tq), lambda b, i: (0, b * nq + i)),
        ],
        out_specs=pl.BlockSpec((tq, FOX_W), lambda b, i: (b * nq + i, 0)),
        scratch_shapes=[pltpu.VMEM((FOX_ACC_ROWS, tq), F32)] * FOX_HEADS
                       + [pltpu.VMEM((FOX_HEADS, LANES, tq), MXU_DTYPE)],
        compiler_params=_params("arbitrary", "arbitrary"),
        name="fox_attention",
    )(fq, fk_aug, fvt_aug, gates_t)


def _router_kernel(h_ref, w_ref, b_ref, stril_ref, id_ref, gate_ref, cnt_ref, carry):
    tm = h_ref.shape[0]

    @pl.when(pl.program_id(0) == 0)
    def _():
        carry[...] = jnp.zeros(carry.shape, F32)

    h = h_ref[...]
    h_hi = h.astype(MXU_DTYPE)
    h_lo = (h - h_hi.astype(F32)).astype(MXU_DTYPE)
    both = jnp.dot(h_hi, w_ref[...], preferred_element_type=F32)
    logits = (both[:, 0:LANES] + both[:, LANES:2 * LANES]
              + jnp.dot(h_lo, w_ref[:, 0:LANES], preferred_element_type=F32) + b_ref[...])
    lane = lax.broadcasted_iota(jnp.int32, (tm, LANES), 1).astype(F32)
    neg = -jnp.inf
    first = lambda hit: jnp.min(jnp.where(hit, lane, float(LANES)), axis=-1, keepdims=True)
    gl = jnp.where(lane < MOE_GROUPS, logits, neg)
    g_max = jnp.max(gl, axis=-1, keepdims=True)
    g_idx = first(gl == g_max)
    p_group = 1.0 / jnp.sum(jnp.exp(gl - g_max), axis=-1, keepdims=True)
    e_lo = MOE_GROUPS + g_idx * EXPERTS_PER_GROUP
    el = jnp.where((lane >= e_lo) & (lane < e_lo + EXPERTS_PER_GROUP), logits, neg)
    v1 = jnp.max(el, axis=-1, keepdims=True)
    i1 = first(el == v1)
    el2 = jnp.where(lane == i1, neg, el)
    v2 = jnp.max(el2, axis=-1, keepdims=True)
    i2 = first(el2 == v2)
    t = jnp.exp(v2 - v1)
    w1 = 1.0 / (1.0 + t)
    gate_ref[...] = jnp.where(lane == 0.0, p_group * w1, jnp.where(lane == 1.0, p_group * (t * w1), 0.0))

    hit1 = lane == i1
    hit2 = lane == i2
    sent = jnp.where(hit1 | hit2, 1.0, 0.0)
    before = jnp.dot(stril_ref[...], sent.astype(jnp.bfloat16), preferred_element_type=F32) + carry[0:1, :]
    r1 = jnp.sum(jnp.where(hit1, before, 0.0), axis=-1, keepdims=True)
    r2 = jnp.sum(jnp.where(hit2, before, 0.0), axis=-1, keepdims=True)
    total = before[tm - 1:tm, :] + sent[tm - 1:tm, :]
    carry[...] = jnp.broadcast_to(total, carry.shape)
    cnt_ref[...] = jnp.broadcast_to(total, cnt_ref.shape).astype(jnp.int32)
    ids = jnp.where(lane == 0.0, i1 - MOE_GROUPS, jnp.where(lane == 1.0, i2 - MOE_GROUPS,
                    jnp.where(lane == 2.0, r1, jnp.where(lane == 3.0, r2, 0.0))))
    id_ref[...] = ids.T[0:SUBLANES, :].astype(jnp.int32)


def _router(h, w_router, b_router):
    tok = h.shape[0]
    tm = TM_ROUTER
    row = lambda i: (i, 0)
    idx = np.arange(tm)
    stril = jnp.asarray((idx[:, None] > idx[None, :]).astype(np.float32), jnp.bfloat16)
    w_hi = w_router.astype(MXU_DTYPE)
    w_hi_lo = (w_hi, (w_router - w_hi.astype(F32)).astype(MXU_DTYPE))
    return pl.pallas_call(
        _router_kernel,
        out_shape=(jax.ShapeDtypeStruct((SUBLANES, tok), jnp.int32), jax.ShapeDtypeStruct((tok, LANES), F32),
                   jax.ShapeDtypeStruct((SUBLANES, LANES), jnp.int32)),
        grid=(tok // tm,),
        in_specs=[pl.BlockSpec((tm, D_MODEL), row), _full((D_MODEL, 2 * LANES)), _full((1, LANES)),
                  _full((tm, tm))],
        out_specs=(pl.BlockSpec((SUBLANES, tm), lambda i: (0, i)), pl.BlockSpec((tm, LANES), row),
                   _full((SUBLANES, LANES))),
        scratch_shapes=[pltpu.VMEM((SUBLANES, LANES), F32)],
        compiler_params=_params("arbitrary"),
        name="moe_router",
    )(h, jnp.concatenate([w_hi, w_hi_lo[1]], axis=1), b_router, stril)


def _pad_pieces(n):
    return tuple(1 << b for b in reversed(range((n - 1).bit_length())))


def _dispatch_kernel(d0_ref, d1_ref, start_ref, cnt_ref, nused_ref, h_ref, xs_hbm, zeros, sem, pad_sem):
    tm = h_ref.shape[0]
    i = pl.program_id(0)
    base = i * tm

    def issue(r, c):
        t = base + r
        pltpu.make_async_copy(h_ref.at[r], xs_hbm.at[d0_ref[t]], sem).start(priority=0)
        pltpu.make_async_copy(h_ref.at[r], xs_hbm.at[d1_ref[t]], sem).start(priority=1)
        return c

    lax.fori_loop(0, tm, issue, 0, unroll=8)

    def pad_copies(e, fn):
        cnt = cnt_ref[e]
        n_pad = (MOE_ROWS - cnt % MOE_ROWS) % MOE_ROWS
        first = start_ref[e] + cnt
        for piece in _pad_pieces(MOE_ROWS):
            @pl.when((n_pad & piece) != 0)
            def _(piece=piece):
                off = first + (n_pad & ~(2 * piece - 1))
                fn(pltpu.make_async_copy(zeros.at[pl.ds(0, piece)], xs_hbm.at[pl.ds(off, piece)], pad_sem))

    def tail_copies(blk, fn):
        for part in range(MOE_ROWS // zeros.shape[0]):
            off = blk * MOE_ROWS + part * zeros.shape[0]
            fn(pltpu.make_async_copy(zeros, xs_hbm.at[pl.ds(off, zeros.shape[0])], pad_sem))

    @pl.when(i == 0)
    def _():
        zeros[...] = jnp.zeros(zeros.shape, F32)
        n_blocks = xs_hbm.shape[0] // MOE_ROWS

        def start(e, c):
            pad_copies(e, lambda cp: cp.start())
            return c

        def wait(e, c):
            pad_copies(e, lambda cp: cp.wait())
            return c

        def tail_start(blk, c):
            tail_copies(blk, lambda cp: cp.start())
            return c

        def tail_wait(blk, c):
            tail_copies(blk, lambda cp: cp.wait())
            return c

        lax.fori_loop(0, N_EXPERTS, start, 0)
        lax.fori_loop(nused_ref[0], n_blocks, tail_start, 0)
        lax.fori_loop(0, N_EXPERTS, wait, 0)
        lax.fori_loop(nused_ref[0], n_blocks, tail_wait, 0)

    for _ in range(TOP_K):
        pltpu.make_async_copy(h_ref, xs_hbm.at[pl.ds(0, tm)], sem).wait()


def _dispatch(h_tiles, dest0, dest1, pad_start, counts, n_used, n_slots):
    tok = h_tiles.shape[0]
    tm = TM_DISPATCH
    assert tok % tm == 0, (tok, tm)
    grid_spec = pltpu.PrefetchScalarGridSpec(
        num_scalar_prefetch=5,
        grid=(tok // tm,),
        in_specs=[pl.BlockSpec((tm,) + ROW_TILE, lambda i, *_: (i, 0, 0))],
        out_specs=pl.BlockSpec(memory_space=pl.ANY),
        scratch_shapes=[pltpu.VMEM((MOE_ROWS // 2,) + ROW_TILE, F32),
                        pltpu.SemaphoreType.DMA(()), pltpu.SemaphoreType.DMA(())],
    )
    return pl.pallas_call(
        _dispatch_kernel,
        out_shape=jax.ShapeDtypeStruct((n_slots,) + ROW_TILE, F32),
        grid_spec=grid_spec,
        compiler_params=_params("arbitrary"),
        name="moe_dispatch",
    )(dest0, dest1, pad_start, counts, n_used, h_tiles)


def _expert_kernel(be_ref, nused_ref, next_ref, slot_ref, x_ref, wg_hbm, wu_hbm, wd_hbm, out_ref,
                   wg_s, wu_s, wd_s, wg_buf, wu_buf, wd_buf, sem, *, layer):
    i = pl.program_id(0)

    def weight_copies(e, slot):
        w = layer * N_EXPERTS + e
        return (pltpu.make_async_copy(wg_hbm.at[w], wg_buf.at[slot], sem.at[slot]),
                pltpu.make_async_copy(wu_hbm.at[w], wu_buf.at[slot], sem.at[slot]),
                pltpu.make_async_copy(wd_hbm.at[w], wd_buf.at[slot], sem.at[slot]))

    @pl.when(i == 0)
    def _():
        for cp in weight_copies(be_ref[0], slot_ref[be_ref[0]]):
            cp.start()

    @pl.when(i < nused_ref[0])
    def _():
        @pl.when((i == 0) | (be_ref[i] != be_ref[jnp.maximum(i - 1, 0)]))
        def _():
            e = be_ref[i]
            slot = slot_ref[e]
            for cp in weight_copies(e, slot):
                cp.wait()

            @pl.when(next_ref[e] >= 0)
            def _():
                for cp in weight_copies(next_ref[e], 1 - slot):
                    cp.start()

            wg_s[...] = wg_buf[slot].astype(MXU_DTYPE)
            wu_s[...] = wu_buf[slot].astype(MXU_DTYPE)
            wd_s[...] = wd_buf[slot].astype(MXU_DTYPE)

        x = _tiles_to_rows(x_ref).astype(MXU_DTYPE)
        n = x.shape[0]
        hidden = []
        for c in range(0, D_EXPERT, FFN_COLS):
            gate = jnp.dot(x, wg_s[:, c:c + FFN_COLS], preferred_element_type=F32)
            up = jnp.dot(x, wu_s[:, c:c + FFN_COLS], preferred_element_type=F32)
            hidden.append((gate * _sigmoid(gate) * up).astype(MXU_DTYPE))
        hidden = jnp.concatenate(hidden, axis=1)
        for c in range(0, D_MODEL, FFN_COLS):
            out = jnp.dot(hidden, wd_s[:, c:c + FFN_COLS], preferred_element_type=F32)
            for s in range(FFN_COLS // LANES):
                out_ref[pl.ds(c // LANES + s, n, stride=ROW_TILE[0]), :] = out[:, s * LANES:(s + 1) * LANES]

    @pl.when(i >= nused_ref[0])
    def _():
        out_ref[...] = jnp.zeros(out_ref.shape, F32)


def _expert_ffn(xs_tiles, block_expert, n_used, counts, w_gate, w_up, w_down, layer):
    n_slots = xs_tiles.shape[0]
    n_blocks = block_expert.shape[0]
    rows = MOE_ROWS * ROW_TILE[0]
    experts = jnp.arange(N_EXPERTS, dtype=jnp.int32)
    nonempty = counts > 0
    later = (experts[None, :] > experts[:, None]) & nonempty[None, :]
    next_expert = jnp.min(jnp.where(later, experts[None, :], N_EXPERTS), axis=1)
    next_expert = jnp.where(next_expert < N_EXPERTS, next_expert, -1).astype(jnp.int32)
    buf_half = ((jnp.cumsum(nonempty.astype(jnp.int32)) - 1) % 2).astype(jnp.int32)
    blk = lambda i, be, nu, nx, sl: (jnp.maximum(jnp.minimum(i, nu[0] - 1), 0), 0)
    grid_spec = pltpu.PrefetchScalarGridSpec(
        num_scalar_prefetch=4,
        grid=(n_blocks,),
        in_specs=[
            pl.BlockSpec((rows, LANES), blk),
            pl.BlockSpec(memory_space=pl.ANY),
            pl.BlockSpec(memory_space=pl.ANY),
            pl.BlockSpec(memory_space=pl.ANY),
        ],
        out_specs=pl.BlockSpec((rows, LANES), lambda i, be, nu, nx, sl: (i, 0)),
        scratch_shapes=[pltpu.VMEM((D_MODEL, D_EXPERT), MXU_DTYPE), pltpu.VMEM((D_MODEL, D_EXPERT), MXU_DTYPE),
                        pltpu.VMEM((D_EXPERT, D_MODEL), MXU_DTYPE),
                        pltpu.VMEM((2, D_MODEL, D_EXPERT), F32), pltpu.VMEM((2, D_MODEL, D_EXPERT), F32),
                        pltpu.VMEM((2, D_EXPERT, D_MODEL), F32), pltpu.SemaphoreType.DMA((2,))],
    )
    out = pl.pallas_call(
        functools.partial(_expert_kernel, layer=layer),
        out_shape=jax.ShapeDtypeStruct((n_slots * ROW_TILE[0], LANES), F32),
        grid_spec=grid_spec,
        compiler_params=_params("arbitrary"),
        name="moe_experts",
    )(block_expert, n_used, next_expert, buf_half, xs_tiles.reshape(n_slots * ROW_TILE[0], LANES),
      w_gate, w_up, w_down)
    return out.reshape((n_slots,) + ROW_TILE)


def _combine_ln_kernel(d0_ref, d1_ref, rows_hbm, h_ref, gate_ref, g_ref, b_ref, o_ref, buf, sem):
    tm = h_ref.shape[0]
    i = pl.program_id(0)

    def gather(tile, p):
        base = tile * tm

        def issue(r, c):
            t = base + r
            dst = pl.ds(pl.multiple_of(r * ROW_TILE[0], ROW_TILE[0]), ROW_TILE[0])
            pltpu.make_async_copy(rows_hbm.at[d0_ref[t]], buf.at[p, 0, dst],
                                  sem.at[p]).start(priority=0)
            pltpu.make_async_copy(rows_hbm.at[d1_ref[t]], buf.at[p, 1, dst],
                                  sem.at[p]).start(priority=1)
            return c

        lax.fori_loop(0, tm, issue, 0, unroll=8)

    @pl.when(i == 0)
    def _():
        gather(0, 0)

    for p in range(2):
        @pl.when(i % 2 == p)
        def _(p=p):
            @pl.when(i + 1 < pl.num_programs(0))
            def _():
                gather(i + 1, 1 - p)

            for k in range(TOP_K):
                pltpu.make_async_copy(buf.at[1 - p, k], buf.at[p, k], sem.at[p]).wait()
            gate = gate_ref[...]
            acc = (ALPHA * h_ref[...] + gate[:, 0:1] * _tiles_to_rows(buf.at[p, 0])
                   + gate[:, 1:2] * _tiles_to_rows(buf.at[p, 1]))
            o_ref[...] = _layer_norm_rows(acc, g_ref[...], b_ref[...])


def _combine_ln(dest0, dest1, rows_tiles, h, gates, g_row, b_row):
    tok = h.shape[0]
    tm = TM_COMBINE
    row = lambda i, *_: (i, 0)
    const = lambda i, *_: (0, 0)
    grid_spec = pltpu.PrefetchScalarGridSpec(
        num_scalar_prefetch=2,
        grid=(tok // tm,),
        in_specs=[
            pl.BlockSpec(memory_space=pl.ANY),
            pl.BlockSpec((tm, D_MODEL), row),
            pl.BlockSpec((tm, LANES), row),
            pl.BlockSpec((1, D_MODEL), const),
            pl.BlockSpec((1, D_MODEL), const),
        ],
        out_specs=pl.BlockSpec((tm, D_MODEL), row),
        scratch_shapes=[pltpu.VMEM((2, TOP_K, tm * ROW_TILE[0], LANES), F32), pltpu.SemaphoreType.DMA((2,))],
    )
    return pl.pallas_call(
        _combine_ln_kernel,
        out_shape=jax.ShapeDtypeStruct((tok, D_MODEL), F32),
        grid_spec=grid_spec,
        compiler_params=_params("arbitrary"),
        name="moe_combine_ln",
    )(dest0, dest1, rows_tiles, h, gates, g_row, b_row)


def _row_slots_kernel(start_ref, ids_ref, out_ref):
    expert = ids_ref[0:TOP_K, :]
    start = jnp.zeros(expert.shape, jnp.int32)
    for e in range(N_EXPERTS):
        start = jnp.where(expert == e, start_ref[e], start)
    out_ref[...] = jnp.zeros(out_ref.shape, jnp.int32)
    out_ref[0:TOP_K, :] = start + ids_ref[TOP_K:2 * TOP_K, :]


def _row_slots(ids, pad_start):
    grid_spec = pltpu.PrefetchScalarGridSpec(
        num_scalar_prefetch=1,
        grid=(1,),
        in_specs=[pl.BlockSpec(ids.shape, lambda i, s: (0, 0))],
        out_specs=pl.BlockSpec(ids.shape, lambda i, s: (0, 0)),
    )
    return pl.pallas_call(
        _row_slots_kernel,
        out_shape=jax.ShapeDtypeStruct(ids.shape, jnp.int32),
        grid_spec=grid_spec,
        name="moe_row_slots",
    )(pad_start, ids)


def _slot_layout(counts, n_blocks):
    padded = (counts + MOE_ROWS - 1) // MOE_ROWS * MOE_ROWS
    pad_end = jnp.cumsum(padded)
    n_used = pad_end[-1:] // MOE_ROWS
    blocks = jnp.arange(n_blocks, dtype=jnp.int32)
    first_row = jnp.minimum(blocks, n_used - 1) * MOE_ROWS
    block_expert = jnp.sum(first_row[:, None] >= pad_end[None, :], axis=1)
    return (pad_end - padded).astype(jnp.int32), block_expert.astype(jnp.int32), n_used.astype(jnp.int32)


def _moe_ln(h, h_tiles, w_router, b_router, w_gate, w_up, w_down, layer, g_row, b_row):
    tok = h.shape[0]
    n_blocks = tok * TOP_K // MOE_ROWS + N_EXPERTS
    ids, gates, counts = _router(h, w_router, b_router)
    counts = counts[0, MOE_GROUPS:MOE_GROUPS + N_EXPERTS]
    pad_start, block_expert, n_used = _slot_layout(counts, n_blocks)
    slots = _row_slots(ids, pad_start)
    dest0, dest1 = slots[0], slots[1]
    xs = _dispatch(h_tiles.reshape((tok,) + ROW_TILE), dest0, dest1, pad_start, counts, n_used,
                   n_blocks * MOE_ROWS)
    rows = _expert_ffn(xs, block_expert, n_used, counts, w_gate, w_up, w_down, layer)
    return _combine_ln(dest0, dest1, rows, h, gates, g_row, b_row)


def _pad_lanes(v):
    return jnp.pad(v, (0, LANES - v.shape[0])).reshape(1, LANES)


def _even_mixer(h, bsz, seq, w_in, conv_a, conv_w, conv_b, dt_bias, a_log, d_skip, norm_w, w_out, g_row, b_row):
    w = jnp.pad(w_in, ((0, 0), (0, AB_PROJ - w_in.shape[1]))).astype(MXU_DTYPE)
    y_a, z, xbc, dt, acs, acst = _even_front(
        h, w, conv_a, conv_w, conv_b.reshape(1, -1), _pad_lanes(dt_bias), _pad_lanes(-jnp.exp(a_log)), bsz, seq)
    dskip_row = jnp.repeat(d_skip, SSM_HEAD_DIM).reshape(1, -1)
    y_b = _ssd(xbc, dt, acs, acst, z, dskip_row, norm_w.reshape(1, -1), bsz, seq)
    return _outproj_ln([y_a, y_b], h, w_out.astype(MXU_DTYPE), g_row, b_row)


def _odd_mixer(h, bsz, seq, w_in, i_bias, f_bias, hnorm_w, fox_f_bias, w_out, g_row, b_row):
    c = np.cumsum((0, MLSTM_W, MLSTM_W, MLSTM_W, MLSTM_HEADS, MLSTM_HEADS, MLSTM_W, FOX_W, FOX_W, FOX_W, FOX_HEADS))
    part = lambda j: w_in[:, c[j]:c[j + 1]]
    q, k, v, i_pre, f_pre, o_pre, fq, fk, fv, ff = (part(j) for j in range(10))
    gate_cols = jnp.concatenate([i_pre, f_pre, ff, f_pre], axis=1)
    gate_cols = jnp.pad(gate_cols, ((0, 0), (0, LANES - gate_cols.shape[1])))
    spread = lambda m: jnp.pad(m.reshape(-1, FOX_HEADS, FOX_HEAD_DIM),
                               ((0, 0), (0, 0), (0, LANES - FOX_HEAD_DIM))).reshape(-1, FOX_AUG)
    w = jnp.concatenate([q, k, v, o_pre, fq, spread(fk), fv, gate_cols], axis=1).astype(MXU_DTYPE)
    gate_bias = _pad_lanes(jnp.concatenate([i_bias, f_bias, fox_f_bias, f_bias]))
    q, k, v, o, fq, fk_aug, fvt_aug, gates, gates_t = _odd_front(h, w, gate_bias, bsz, seq)
    y_c = _mlstm(q, k, v, o, gates, gates_t, hnorm_w.reshape(1, -1), bsz, seq)
    y_d = _fox(fq, fk_aug, fvt_aug, gates_t, bsz, seq)
    return _outproj_ln([y_c, y_d], h, w_out.astype(MXU_DTYPE), g_row, b_row)


def kernel(x, ab_w_in, ab_conv_a, ab_conv_ssm_w, ab_conv_ssm_b, ab_dt_bias, ab_a_log, ab_d_skip, ab_norm_w, ab_w_out, cd_w_in, cd_i_bias, cd_f_bias, cd_hnorm_w, cd_fox_f_bias, cd_w_out, ln1_g, ln1_b, ln2_g, ln2_b, moe_rg_w, moe_rg_b, moe_re_w, moe_re_b, moe_w_gate, moe_w_up, moe_w_down):
    bsz, seq, d = x.shape
    h = x.reshape(bsz * seq, d)
    stack = lambda w: w.reshape((w.shape[0] * w.shape[1],) + w.shape[2:])
    w_gate, w_up, w_down = stack(moe_w_gate), stack(moe_w_up), stack(moe_w_down)
    for layer in range(DEPTH):
        j = layer // 2
        g1, b1 = ln1_g[layer].reshape(1, -1), ln1_b[layer].reshape(1, -1)
        if layer % 2 == 0:
            h, h_tiles = _even_mixer(h, bsz, seq, ab_w_in[j], ab_conv_a[j], ab_conv_ssm_w[j], ab_conv_ssm_b[j],
                                     ab_dt_bias[j], ab_a_log[j], ab_d_skip[j], ab_norm_w[j], ab_w_out[j], g1, b1)
        else:
            h, h_tiles = _odd_mixer(h, bsz, seq, cd_w_in[j], cd_i_bias[j], cd_f_bias[j], cd_hnorm_w[j],
                                    cd_fox_f_bias[j], cd_w_out[j], g1, b1)
        re_w = jnp.transpose(moe_re_w[layer], (1, 0, 2)).reshape(d, N_EXPERTS)
        w_router = jnp.pad(jnp.concatenate([moe_rg_w[layer], re_w], axis=1),
                           ((0, 0), (0, LANES - MOE_GROUPS - N_EXPERTS)))
        b_router = _pad_lanes(jnp.concatenate([moe_rg_b[layer], moe_re_b[layer].reshape(-1)]))
        h = _moe_ln(h, h_tiles, w_router, b_router, w_gate, w_up, w_down, layer,
                    ln2_g[layer].reshape(1, -1), ln2_b[layer].reshape(1, -1))
    return h.reshape(bsz, seq, d)
```

```python
import functools

import numpy as np
import jax
import jax.numpy as jnp
from jax import lax
from jax.experimental import pallas as pl
from jax.experimental.pallas import tpu as pltpu

F32 = jnp.float32
MXU_DTYPE = jnp.bfloat16
HIGHEST = lax.Precision.HIGHEST

D_MODEL = 1024
DEPTH = 4
ALPHA = (2 * DEPTH) ** 0.25
LN_EPS = 1e-5
CONV_DIM = D_MODEL // 2
CONV_WIDTH = 3
SSM_D_INNER = D_MODEL
SSM_HEAD_DIM = 64
SSM_HEADS = SSM_D_INNER // SSM_HEAD_DIM
SSM_GROUPS = 4
SSM_STATE = 64
SSM_CONV = 4
SSM_BC = SSM_GROUPS * SSM_STATE
SSM_CONV_DIM = SSM_D_INNER + 2 * SSM_BC
MLSTM_HEADS = 4
MLSTM_HEAD_DIM = D_MODEL // 8
MLSTM_W = MLSTM_HEADS * MLSTM_HEAD_DIM
FOX_HEADS = 8
FOX_HEAD_DIM = D_MODEL // 16
FOX_W = FOX_HEADS * FOX_HEAD_DIM
MOE_GROUPS = 4
EXPERTS_PER_GROUP = 8
N_EXPERTS = MOE_GROUPS * EXPERTS_PER_GROUP
TOP_K = 2
D_EXPERT = D_MODEL // 2

LANES = 128
SUBLANES = 8
VMEM_LIMIT_BYTES = 56 * 1024 * 1024

CHUNK = 128
SEQ_PAIR = 4
MLSTM_SEQS = 1
MLSTM_CHUNK = 512
TM_FRONT = 512
TM_EVEN_FRONT = 256
CONV_ROWS, CONV_LANES = 64, 512
TM_OUT = 1024
TM_ROUTER = 512
TQ_FOX = 256
MOE_ROWS = 512
FFN_COLS = 256
TM_COMBINE = 512
TM_DISPATCH = 2048

AB_PROJ = 4224
FOX_AUG = FOX_HEADS * LANES
CD_FK = 5 * 512
CD_FV = CD_FK + FOX_AUG
CD_GATES = CD_FV + FOX_W
CD_PROJ = CD_GATES + LANES
LOG2E = 1.4426950408889634
TK_FOX = 128
FOX_ACC_ROWS = FOX_HEAD_DIM + SUBLANES
FOX_VROWS = LANES
G_I, G_F, G_FOX, G_BCUM = 0, 4, 8, 16


def _params(*sem):
    return pltpu.CompilerParams(dimension_semantics=sem, vmem_limit_bytes=VMEM_LIMIT_BYTES)


def _softplus(x):
    return jnp.maximum(x, 0.0) + jnp.log(1.0 + jnp.exp(-jnp.abs(x)))


def _sigmoid(x):
    return 1.0 / (1.0 + jnp.exp(-x))


def _layer_norm_rows(v, g, b):
    mu = jnp.mean(v, axis=-1, keepdims=True)
    c = v - mu
    var = jnp.mean(c * c, axis=-1, keepdims=True)
    return c * lax.rsqrt(var + LN_EPS) * g + b


def _tril(n, block):
    i = np.arange(n)
    m = (i[:, None] >= i[None, :]) & (i[:, None] // block == i[None, :] // block)
    return jnp.asarray(m.astype(np.float32), MXU_DTYPE)


def _full(shape):
    return pl.BlockSpec(shape, lambda *_: (0,) * len(shape), pipeline_mode=pl.Buffered(1))


ROW_TILE = (D_MODEL // LANES, LANES)


def _tiles_to_rows(ref):
    n = ref.shape[0] // ROW_TILE[0]
    return jnp.concatenate([ref[pl.ds(s, n, stride=ROW_TILE[0]), :] for s in range(ROW_TILE[0])], axis=1)


def _rows_to_tiles(ref, val):
    n = val.shape[0]
    for s in range(ROW_TILE[0]):
        ref[pl.ds(s, n, stride=ROW_TILE[0]), :] = val[:, s * LANES:(s + 1) * LANES]


def _split3(x):
    narrow = lambda v: v.astype(MXU_DTYPE).astype(F32)
    x1 = narrow(x)
    x2 = narrow(x - x1)
    return x1, x2, narrow(x - x1 - x2)


def _cumsum_rows(tril, x):
    parts = jnp.dot(tril, jnp.concatenate(_split3(x), axis=1).astype(MXU_DTYPE), preferred_element_type=F32)
    return parts[:, 0:LANES] + parts[:, LANES:2 * LANES] + parts[:, 2 * LANES:3 * LANES]


def _even_front_kernel(h_ref, w_ref, ca_ref, cw_ref, cb_ref, dtb_ref, aneg_ref, tril_ref,
                       ya_ref, z_ref, xbc_ref, dt_ref, acs_ref, acst_ref,
                       proj_buf, ua_ext, xbc_ext, *, tiles_per_seq):
    tm = h_ref.shape[0]
    i = pl.program_id(0)
    cur = i % 2

    @pl.when(i == 0)
    def _():
        proj_buf[1] = jnp.zeros(proj_buf.shape[1:], F32)

    @pl.when((i == 0) | ((i - 1) % tiles_per_seq == 0))
    def _():
        ua_ext[0:SUBLANES, :] = jnp.zeros((SUBLANES, CONV_DIM), F32)
        xbc_ext[0:SUBLANES, :] = jnp.zeros((SUBLANES, SSM_CONV_DIM), F32)

    z0 = 3 * CONV_DIM
    x0 = z0 + SSM_D_INNER
    d0 = x0 + SSM_CONV_DIM

    x_in = h_ref[...].astype(MXU_DTYPE)
    proj = proj_buf.at[1 - cur]
    z_ref[...] = proj[:, z0:x0]

    n_blocks = tm // CONV_ROWS
    col_cuts = [AB_PROJ * g // n_blocks // LANES * LANES for g in range(n_blocks)] + [AB_PROJ]
    for r0 in range(0, tm, CONV_ROWS):
        c_lo, c_hi = col_cuts[r0 // CONV_ROWS], col_cuts[r0 // CONV_ROWS + 1]
        proj_buf[cur, :, c_lo:c_hi] = jnp.dot(x_in, w_ref[:, c_lo:c_hi], preferred_element_type=F32)
        rows = slice(r0, r0 + CONV_ROWS)
        ext_rows = slice(SUBLANES + r0, SUBLANES + r0 + CONV_ROWS)
        ua_ext[ext_rows, :] = proj[rows, CONV_DIM:2 * CONV_DIM] * proj[rows, 2 * CONV_DIM:3 * CONV_DIM]
        conv = None
        for k in range(CONV_WIDTH):
            tap = ca_ref[k:k + 1, :] * ua_ext[pl.ds(SUBLANES + r0 - (CONV_WIDTH - 1) + k, CONV_ROWS), :]
            conv = tap if conv is None else conv + tap
        ya_ref[rows, :] = (proj[rows, 0:CONV_DIM] * conv).astype(ya_ref.dtype)
        for c0 in range(0, SSM_CONV_DIM, CONV_LANES):
            cols = slice(c0, c0 + CONV_LANES)
            xbc_ext[ext_rows, cols] = proj[rows, x0 + c0:x0 + c0 + CONV_LANES]
            conv = cb_ref[:, cols]
            for k in range(SSM_CONV):
                conv = conv + cw_ref[k:k + 1, cols] * xbc_ext[pl.ds(SUBLANES + r0 - (SSM_CONV - 1) + k, CONV_ROWS), cols]
            xbc_ref[rows, cols] = conv * _sigmoid(conv)
    ua_ext[0:SUBLANES, :] = ua_ext[tm:tm + SUBLANES, :]
    xbc_ext[0:SUBLANES, :] = xbc_ext[tm:tm + SUBLANES, :]

    dt = _softplus(proj[:, d0:d0 + LANES] + dtb_ref[...])
    a = dt * aneg_ref[...]
    acs = _cumsum_rows(tril_ref[...], a)
    dt_ref[...] = dt
    acs_ref[...] = acs
    acst_ref[...] = acs.T


def _even_front(h, w_in, conv_a, conv_w, conv_b, dt_bias_row, aneg_row, bsz, seq):
    tok = bsz * seq
    tm = TM_EVEN_FRONT
    n_tiles = tok // tm
    row = lambda i: (jnp.maximum(i - 1, 0), 0)
    out_shapes = (
        jax.ShapeDtypeStruct((tok, CONV_DIM), MXU_DTYPE),
        jax.ShapeDtypeStruct((tok, SSM_D_INNER), F32),
        jax.ShapeDtypeStruct((tok, SSM_CONV_DIM), F32),
        jax.ShapeDtypeStruct((tok, LANES), F32),
        jax.ShapeDtypeStruct((tok, LANES), F32),
        jax.ShapeDtypeStruct((LANES, tok), F32),
    )
    return pl.pallas_call(
        functools.partial(_even_front_kernel, tiles_per_seq=seq // tm),
        out_shape=out_shapes,
        grid=(n_tiles + 1,),
        in_specs=[
            pl.BlockSpec((tm, D_MODEL), lambda i: (jnp.minimum(i, n_tiles - 1), 0)),
            _full((D_MODEL, AB_PROJ)),
            _full((CONV_WIDTH, CONV_DIM)),
            _full((SSM_CONV, SSM_CONV_DIM)),
            _full((1, SSM_CONV_DIM)),
            _full((1, LANES)),
            _full((1, LANES)),
            _full((tm, tm)),
        ],
        out_specs=(
            pl.BlockSpec((tm, CONV_DIM), row),
            pl.BlockSpec((tm, SSM_D_INNER), row),
            pl.BlockSpec((tm, SSM_CONV_DIM), row),
            pl.BlockSpec((tm, LANES), row),
            pl.BlockSpec((tm, LANES), row),
            pl.BlockSpec((LANES, tm), lambda i: (0, jnp.maximum(i - 1, 0))),
        ),
        scratch_shapes=[
            pltpu.VMEM((2, tm, AB_PROJ), F32),
            pltpu.VMEM((tm + SUBLANES, CONV_DIM), F32),
            pltpu.VMEM((tm + SUBLANES, SSM_CONV_DIM), F32),
        ],
        compiler_params=_params("arbitrary"),
        name="even_front",
    )(h, w_in, conv_a, conv_w, conv_b, dt_bias_row, aneg_row, _tril(tm, CHUNK))


def _bcast_heads(arr, n_heads, width):
    per = LANES // width
    length = arr.shape[0]
    lane = lax.broadcasted_iota(jnp.int32, (length, LANES), 1)
    outs = []
    for j in range(n_heads // per):
        v = jnp.broadcast_to(arr[:, j * per:j * per + 1], (length, LANES))
        for r in range(1, per):
            v = jnp.where(lane >= r * width, jnp.broadcast_to(arr[:, j * per + r:j * per + r + 1], (length, LANES)), v)
        outs.append(v)
    return jnp.concatenate(outs, axis=1)


def _expand_heads(arr, e3_ref):
    hi, lo, _ = _split3(arr)
    return jnp.dot(jnp.concatenate([hi, lo], axis=1).astype(MXU_DTYPE), e3_ref[...],
                   preferred_element_type=F32)


def _ssd_kernel(*refs):
    nb = SEQ_PAIR
    xbc_ref, dt_ref, acs_ref, z_ref = refs[:4]
    acst_refs = refs[4:4 + nb]
    dskip_ref, nw_ref, e3_ref, y_ref, state = refs[4 + nb:]
    L = CHUNK
    P = SSM_HEAD_DIM
    R = SSM_HEADS // SSM_GROUPS
    GW = R * P

    @pl.when(pl.program_id(1) == 0)
    def _():
        state[...] = jnp.zeros(state.shape, F32)

    row = lax.broadcasted_iota(jnp.int32, (L, L), 0)
    col = lax.broadcasted_iota(jnp.int32, (L, L), 1)
    causal = row >= col
    lane_g = lax.broadcasted_iota(jnp.int32, (L, GW), 1)

    per_row = []
    for j in range(nb):
        acs = acs_ref[j]
        per_row += [dt_ref[j], jnp.exp(acs[L - 1:L, :] - acs), jnp.exp(acs)]
    expanded = _expand_heads(jnp.concatenate(per_row, axis=0), e3_ref)

    st_old = [state[c] for c in range(nb * SSM_GROUPS)]
    st_new = []
    for j in range(nb):
        xs = xbc_ref[j, :, 0:SSM_D_INNER]
        bm = xbc_ref[j, :, SSM_D_INNER:SSM_D_INNER + SSM_BC]
        cm = xbc_ref[j, :, SSM_D_INNER + SSM_BC:SSM_CONV_DIM]
        acs = acs_ref[j]
        acst = acst_refs[j][...]
        a_last = acs[L - 1:L, :]
        dtx = expanded[(3 * j) * L:(3 * j + 1) * L, :]
        decx = expanded[(3 * j + 1) * L:(3 * j + 2) * L, :]
        expx = expanded[(3 * j + 2) * L:(3 * j + 3) * L, :]
        xdt = xs * dtx
        xdec = (xdt * decx).astype(MXU_DTYPE)
        xdt_m = xdt.astype(MXU_DTYPE)
        chunk_decay = jnp.exp(jnp.broadcast_to(a_last, (SUBLANES, LANES)))
        cdx = _bcast_heads(chunk_decay, SSM_HEADS, P)[0:1, :]

        bm_t = bm.T.astype(MXU_DTYPE)
        cm_m = cm.astype(MXU_DTYPE)
        bm_m = bm.astype(MXU_DTYPE)

        ys = []
        for g in range(SSM_GROUPS):
            n0 = g * SSM_STATE
            c_g = cm_m[:, n0:n0 + SSM_STATE]
            cb = lax.dot_general(c_g, bm_m[:, n0:n0 + SSM_STATE], (((1,), (1,)), ((), ())),
                                 preferred_element_type=F32)
            ms = []
            for r in range(R):
                hd = g * R + r
                seg = jnp.exp(jnp.where(causal, acs[:, hd:hd + 1] - acst[hd:hd + 1, :], -jnp.inf))
                ms.append((cb * seg).astype(MXU_DTYPE))
            big = jnp.dot(jnp.concatenate(ms, axis=0), xdt_m[:, g * GW:(g + 1) * GW],
                          preferred_element_type=F32)
            y_diag = big[0:L, :]
            for r in range(1, R):
                y_diag = jnp.where(lane_g >= r * P, big[r * L:(r + 1) * L, :], y_diag)
            st = st_old[j * SSM_GROUPS + g]
            y_off = jnp.dot(c_g, st.astype(MXU_DTYPE), preferred_element_type=F32)
            new = jnp.dot(bm_t[n0:n0 + SSM_STATE, :], xdec[:, g * GW:(g + 1) * GW],
                          preferred_element_type=F32)
            st_new.append(st * cdx[:, g * GW:(g + 1) * GW] + new)
            ys.append(y_diag + y_off * expx[:, g * GW:(g + 1) * GW])
        y = jnp.concatenate(ys, axis=1) + dskip_ref[...] * xs
        z = z_ref[j]
        u = y * (z * _sigmoid(z))
        y = u * lax.rsqrt(jnp.mean(u * u, axis=-1, keepdims=True) + LN_EPS) * nw_ref[...]
        y_ref[j] = y.astype(y_ref.dtype)

    for c in range(nb * SSM_GROUPS):
        state[c] = st_new[c]


def _ssd(xbc, dt, acs, acst, z, dskip_row, normw_row, bsz, seq):
    nb = SEQ_PAIR
    nc = seq // CHUNK
    per_seq = lambda a: a.reshape(bsz, seq, a.shape[-1])
    blk = lambda width: pl.BlockSpec((nb, CHUNK, width), lambda g, c: (g, c, 0))
    acst_specs = [pl.BlockSpec((LANES, CHUNK), lambda g, c, j=j: (0, (g * nb + j) * nc + c)) for j in range(nb)]
    expand = np.zeros((LANES, SSM_D_INNER), np.float32)
    for hd in range(SSM_HEADS):
        expand[hd, hd * SSM_HEAD_DIM:(hd + 1) * SSM_HEAD_DIM] = 1.0
    e3 = jnp.asarray(np.concatenate([expand] * 2, axis=0), MXU_DTYPE)
    y = pl.pallas_call(
        _ssd_kernel,
        out_shape=jax.ShapeDtypeStruct((bsz, seq, SSM_D_INNER), MXU_DTYPE),
        grid=(bsz // nb, nc),
        in_specs=[blk(SSM_CONV_DIM), blk(LANES), blk(LANES), blk(SSM_D_INNER)] + acst_specs + [
            _full((1, SSM_D_INNER)),
            _full((1, SSM_D_INNER)),
            _full((2 * LANES, SSM_D_INNER)),
        ],
        out_specs=blk(SSM_D_INNER),
        scratch_shapes=[pltpu.VMEM((nb * SSM_GROUPS, SSM_STATE, SSM_D_INNER // SSM_GROUPS), F32)],
        compiler_params=_params("arbitrary", "arbitrary"),
        name="ssd_scan",
    )(per_seq(xbc), per_seq(dt), per_seq(acs), per_seq(z), *([acst] * nb), dskip_row, normw_row, e3)
    return y.reshape(bsz * seq, SSM_D_INNER)


def _outproj_ln_kernel(*refs, widths):
    n = len(widths)
    parts = refs[:n]
    h_ref, w_ref, g_ref, b_ref, o_ref, ot_ref = refs[n:]
    acc = ALPHA * h_ref[...]
    off = 0
    for p, wd in zip(parts, widths):
        acc = acc + jnp.dot(p[...].astype(MXU_DTYPE), w_ref[off:off + wd, :], preferred_element_type=F32)
        off += wd
    out = _layer_norm_rows(acc, g_ref[...], b_ref[...])
    o_ref[...] = out
    _rows_to_tiles(ot_ref, out)


def _outproj_ln(parts, h, w_out, g_row, b_row):
    tok = h.shape[0]
    tm = TM_OUT
    widths = tuple(p.shape[1] for p in parts)
    row = lambda i: (i, 0)
    return pl.pallas_call(
        functools.partial(_outproj_ln_kernel, widths=widths),
        out_shape=(jax.ShapeDtypeStruct((tok, D_MODEL), F32),
                   jax.ShapeDtypeStruct((tok * ROW_TILE[0], LANES), F32)),
        grid=(tok // tm,),
        in_specs=[pl.BlockSpec((tm, wd), row) for wd in widths] + [
            pl.BlockSpec((tm, D_MODEL), row),
            _full((sum(widths), D_MODEL)),
            _full((1, D_MODEL)),
            _full((1, D_MODEL)),
        ],
        out_specs=(pl.BlockSpec((tm, D_MODEL), row), pl.BlockSpec((tm * ROW_TILE[0], LANES), row)),
        compiler_params=_params("arbitrary"),
        name="outproj_ln",
    )(*parts, h, w_out, g_row, b_row)


def _odd_front_kernel(h_ref, w_ref, gb_ref, tril_ref,
                      q_ref, k_ref, v_ref, o_ref, fq_ref, fk_ref, fvt_ref, g_ref, gt_ref, carry):
    tm = h_ref.shape[0]
    x_in = h_ref[...].astype(MXU_DTYPE)
    proj_cols = lambda lo, hi: jnp.dot(x_in, w_ref[:, lo:hi], preferred_element_type=F32)

    @pl.when(pl.program_id(1) == 0)
    def _():
        carry[...] = jnp.zeros(carry.shape, F32)

    raw = proj_cols(CD_GATES, CD_GATES + LANES) + gb_ref[...]
    lane = lax.broadcasted_iota(jnp.int32, (tm, LANES), 1)
    g = jnp.where(lane < G_F, raw, -_softplus(-raw))
    prev = carry[0:1, :]
    glob = _cumsum_rows(tril_ref[...], g) + prev
    before = []
    for c in range(tm // MLSTM_CHUNK):
        before.append(jnp.broadcast_to(prev, (MLSTM_CHUNK, LANES)))
        prev = glob[(c + 1) * MLSTM_CHUNK - 1:(c + 1) * MLSTM_CHUNK, :]
    carry[...] = jnp.broadcast_to(prev, carry.shape)
    local = glob - jnp.concatenate(before, axis=0)
    out = jnp.where((lane >= G_FOX) & (lane < G_BCUM), glob, jnp.where(lane >= G_BCUM, local, g))
    g_ref[...] = out
    gt_ref[...] = out.T

    q_ref[...] = proj_cols(0, 512)
    k_ref[...] = proj_cols(512, 1024) * (MLSTM_HEAD_DIM ** -0.5)
    v_ref[...] = proj_cols(1024, 1536)
    o_ref[...] = _sigmoid(proj_cols(1536, 2048))
    fq_ref[...] = proj_cols(2048, 2560) * (FOX_HEAD_DIM ** -0.5 * LOG2E)

    is_bias = (lane >= FOX_HEAD_DIM) & (lane < FOX_HEAD_DIM + 3)
    fk = proj_cols(CD_FK, CD_FK + FOX_AUG)
    for hd in range(FOX_HEADS):
        c1, c2, c3 = _split3(out[:, G_FOX + hd:G_FOX + hd + 1] * (-LOG2E))
        bias = jnp.where(lane == FOX_HEAD_DIM, c1, jnp.where(lane == FOX_HEAD_DIM + 1, c2, c3))
        k_h = fk[:, hd * LANES:(hd + 1) * LANES]
        fk_ref[:, hd * LANES:(hd + 1) * LANES] = jnp.where(is_bias, bias, k_h).astype(MXU_DTYPE)
    v_t = proj_cols(CD_FV, CD_FV + FOX_W).T
    extra = jnp.where(lax.broadcasted_iota(jnp.int32, (FOX_VROWS - FOX_HEAD_DIM, tm), 0) == 0, 1.0, 0.0)
    fvt_ref[...] = jnp.concatenate(
        [blk for hd in range(FOX_HEADS) for blk in (v_t[hd * FOX_HEAD_DIM:(hd + 1) * FOX_HEAD_DIM, :], extra)],
        axis=0).astype(MXU_DTYPE)


def _odd_front(h, w_in, gate_bias_row, bsz, seq):
    tok = bsz * seq
    tm = TM_FRONT
    ns = seq // tm
    row = lambda b, s: (b * ns + s, 0)
    col = lambda b, s: (0, b * ns + s)
    wide = jax.ShapeDtypeStruct((tok, 512), F32)
    return pl.pallas_call(
        _odd_front_kernel,
        out_shape=(wide,) * 5 + (jax.ShapeDtypeStruct((tok, FOX_AUG), MXU_DTYPE),
                                 jax.ShapeDtypeStruct((FOX_HEADS * FOX_VROWS, tok), MXU_DTYPE),
                                 jax.ShapeDtypeStruct((tok, LANES), F32),
                                 jax.ShapeDtypeStruct((LANES, tok), F32)),
        grid=(bsz, ns),
        in_specs=[
            pl.BlockSpec((tm, D_MODEL), row),
            _full((D_MODEL, CD_PROJ)),
            _full((1, LANES)),
            _full((tm, tm)),
        ],
        out_specs=(pl.BlockSpec((tm, 512), row),) * 5 + (
            pl.BlockSpec((tm, FOX_AUG), row),
            pl.BlockSpec((FOX_HEADS * FOX_VROWS, tm), col),
            pl.BlockSpec((tm, LANES), row),
            pl.BlockSpec((LANES, tm), col),
        ),
        scratch_shapes=[pltpu.VMEM((SUBLANES, LANES), F32)],
        compiler_params=_params("arbitrary", "arbitrary"),
        name="odd_front",
    )(h, w_in, gate_bias_row, _tril(tm, tm))


def _mlstm_kernel(*refs):
    nb = MLSTM_SEQS
    q_ref, k_ref, v_ref, o_ref, g_ref = refs[:5]
    gt_refs = refs[5:5 + nb]
    nw_ref, y_ref, c_state, m_state = refs[5 + nb:]
    L = MLSTM_CHUNK
    DH = MLSTM_HEAD_DIM

    @pl.when(pl.program_id(1) == 0)
    def _():
        c_state[...] = jnp.zeros(c_state.shape, F32)
        m_state[...] = jnp.zeros(m_state.shape, F32)

    row = lax.broadcasted_iota(jnp.int32, (L, L), 0)
    col = lax.broadcasted_iota(jnp.int32, (L, L), 1)
    causal = row >= col
    ones_col = jnp.where(lax.broadcasted_iota(jnp.int32, (L, DH), 1) == 0, 1.0, 0.0)

    for j in range(nb):
        gates = g_ref[j]
        gates_t = gt_refs[j][...]
        for hd in range(MLSTM_HEADS):
            st = j * MLSTM_HEADS + hd
            sl = slice(hd * DH, (hd + 1) * DH)
            q = q_ref[j, :, sl].astype(MXU_DTYPE)
            k = k_ref[j, :, sl]
            v_ext = jnp.concatenate([v_ref[j, :, sl], ones_col], axis=1).astype(MXU_DTYPE)
            b_col = gates[:, G_BCUM + hd:G_BCUM + hd + 1]
            i_col = gates[:, G_I + hd:G_I + hd + 1]
            b_row = gates_t[G_BCUM + hd:G_BCUM + hd + 1, :]
            i_row = gates_t[G_I + hd:G_I + hd + 1, :]
            m_prev = m_state[st:st + 1, 0:1]
            c_ext = c_state[st]

            d_mat = jnp.where(causal, b_col - b_row + i_row, -jnp.inf)
            inter = b_col + m_prev
            m_t = jnp.maximum(jnp.max(d_mat, axis=-1, keepdims=True), inter)
            s_qk = lax.dot_general(q, k.astype(MXU_DTYPE), (((1,), (1,)), ((), ())), preferred_element_type=F32)
            w_qk = s_qk * jnp.exp(d_mat - m_t)
            s_inter = jnp.exp(inter - m_t)
            num_ext = (jnp.dot(w_qk.astype(MXU_DTYPE), v_ext, preferred_element_type=F32)
                       + s_inter * jnp.dot(q, c_ext.astype(MXU_DTYPE), preferred_element_type=F32))
            den = num_ext[:, DH:DH + 1]
            hval = num_ext[:, 0:DH] / jnp.maximum(jnp.abs(den), jnp.exp(-m_t))

            b_last = b_col[L - 1:L, :]
            g_log = b_last - b_col + i_col
            m_new = jnp.maximum(b_last + m_prev, jnp.max(g_log, axis=0, keepdims=True))
            w_k = jnp.exp(g_log - m_new)
            decay = jnp.exp(b_last + m_prev - m_new)
            kw_t = (k * w_k).T.astype(MXU_DTYPE)
            c_state[st] = decay * c_ext + jnp.dot(kw_t, v_ext, preferred_element_type=F32)
            m_state[st:st + 1, :] = jnp.broadcast_to(m_new, (1, LANES))

            mu = jnp.mean(hval, axis=-1, keepdims=True)
            cen = hval - mu
            var = jnp.mean(cen * cen, axis=-1, keepdims=True)
            y = o_ref[j, :, sl] * (cen * lax.rsqrt(var + LN_EPS) * nw_ref[:, sl])
            y_ref[j, :, sl] = y.astype(y_ref.dtype)


def _mlstm(q, k, v, o, gates, gates_t, hnorm_row, bsz, seq):
    nb = MLSTM_SEQS
    nc = seq // MLSTM_CHUNK
    per_seq = lambda a: a.reshape(bsz, seq, a.shape[-1])
    blk = lambda width: pl.BlockSpec((nb, MLSTM_CHUNK, width), lambda g, c: (g, c, 0))
    gt_specs = [pl.BlockSpec((LANES, MLSTM_CHUNK), lambda g, c, j=j: (0, (g * nb + j) * nc + c))
                for j in range(nb)]
    y = pl.pallas_call(
        _mlstm_kernel,
        out_shape=jax.ShapeDtypeStruct((bsz, seq, MLSTM_W), MXU_DTYPE),
        grid=(bsz // nb, nc),
        in_specs=[blk(MLSTM_W)] * 4 + [blk(LANES)] + gt_specs + [_full((1, MLSTM_W))],
        out_specs=blk(MLSTM_W),
        scratch_shapes=[pltpu.VMEM((nb * MLSTM_HEADS, MLSTM_HEAD_DIM, 2 * MLSTM_HEAD_DIM), F32),
                        pltpu.VMEM((nb * MLSTM_HEADS, LANES), F32)],
        compiler_params=_params("arbitrary", "arbitrary"),
        name="mlstm_scan",
    )(per_seq(q), per_seq(k), per_seq(v), per_seq(o), per_seq(gates), *([gates_t] * nb), hnorm_row)
    return y.reshape(bsz * seq, MLSTM_W)


def _fox_kernel(q_ref, k_ref, vt_ref, gt_ref, y_ref, *scratch):
    acc_refs = scratch[:FOX_HEADS]
    qa_ref = scratch[FOX_HEADS]
    tq = q_ref.shape[0]
    tk = TK_FOX
    dh = FOX_HEAD_DIM
    qi = pl.program_id(1)
    q_t = q_ref[...].T
    bias_rows = jnp.where(lax.broadcasted_iota(jnp.int32, (LANES - dh, tq), 0) < 3, 1.0, 0.0)
    for hd in range(FOX_HEADS):
        qa_ref[hd] = jnp.concatenate([q_t[hd * dh:(hd + 1) * dh, :], bias_rows], axis=0).astype(MXU_DTYPE)
        acc_refs[hd][...] = jnp.zeros(acc_refs[hd].shape, F32)
    cq = gt_ref[G_FOX:G_FOX + FOX_HEADS, :] * LOG2E
    key_pos = lax.broadcasted_iota(jnp.int32, (tk, tq), 0)
    qry_pos = lax.broadcasted_iota(jnp.int32, (tk, tq), 1) + qi * tq
    n_full = qi * (tq // tk)

    def block(j, ms, masked):
        k0 = pl.multiple_of(j * tk, tk)
        out = []
        for hd in range(FOX_HEADS):
            hs = slice(hd * LANES, (hd + 1) * LANES)
            s = jnp.dot(k_ref[pl.ds(k0, tk), hs], qa_ref[hd], preferred_element_type=F32)
            if masked:
                s = jnp.where(key_pos + k0 <= qry_pos, s, -jnp.inf)
            cq_h = cq[hd:hd + 1, :]
            m_new = jnp.maximum(ms[hd], jnp.max(s, axis=0, keepdims=True) + cq_h)
            p = jnp.exp2(s - (m_new - cq_h))
            pv = jnp.dot(vt_ref[hd * FOX_VROWS:(hd + 1) * FOX_VROWS, pl.ds(k0, tk)], p.astype(MXU_DTYPE),
                         preferred_element_type=F32)
            acc_refs[hd][...] = jnp.exp2(ms[hd] - m_new) * acc_refs[hd][...] + pv[0:FOX_ACC_ROWS, :]
            out.append(m_new)
        return tuple(out)

    ms = tuple(jnp.full((1, tq), -jnp.inf, F32) for _ in range(FOX_HEADS))
    per_trip = tq // tk

    def past_blocks(t, ms):
        for d in range(per_trip):
            ms = block(t * per_trip + d, ms, masked=False)
        return ms

    ms = lax.fori_loop(0, qi, past_blocks, ms)
    for d in range(tq // tk):
        ms = block(n_full + d, ms, masked=True)
    outs = []
    for hd in range(FOX_HEADS):
        acc = acc_refs[hd][...]
        outs.append(acc[0:dh, :] / acc[dh:dh + 1, :])
    y_ref[...] = jnp.concatenate(outs, axis=0).T.astype(y_ref.dtype)


def _fox(fq, fk_aug, fvt_aug, gates_t, bsz, seq):
    tok = bsz * seq
    tq = TQ_FOX
    nq = seq // tq
    return pl.pallas_call(
        _fox_kernel,
        out_shape=jax.ShapeDtypeStruct((tok, FOX_W), MXU_DTYPE),
        grid=(bsz, nq),
        in_specs=[
            pl.BlockSpec((tq, FOX_W), lambda b, i: (b * nq + i, 0)),
            pl.BlockSpec((seq, FOX_AUG), lambda b, i: (b, 0)),
            pl.BlockSpec((FOX_HEADS * FOX_VROWS, seq), lambda b, i: (0, b)),
            pl.BlockSpec((LANES, tq), lambda b, i: (0, b * nq + i)),
        ],
        out_specs=pl.BlockSpec((tq, FOX_W), lambda b, i: (b * nq + i, 0)),
        scratch_shapes=[pltpu.VMEM((FOX_ACC_ROWS, tq), F32)] * FOX_HEADS
                       + [pltpu.VMEM((FOX_HEADS, LANES, tq), MXU_DTYPE)],
        compiler_params=_params("arbitrary", "arbitrary"),
        name="fox_attention",
    )(fq, fk_aug, fvt_aug, gates_t)


def _router_kernel(h_ref, w_ref, b_ref, stril_ref, id_ref, gate_ref, cnt_ref, carry):
    tm = h_ref.shape[0]

    @pl.when(pl.program_id(0) == 0)
    def _():
        carry[...] = jnp.zeros(carry.shape, F32)

    h = h_ref[...]
    h_hi = h.astype(MXU_DTYPE)
    h_lo = (h - h_hi.astype(F32)).astype(MXU_DTYPE)
    both = jnp.dot(h_hi, w_ref[...], preferred_element_type=F32)
    logits = (both[:, 0:LANES] + both[:, LANES:2 * LANES]
              + jnp.dot(h_lo, w_ref[:, 0:LANES], preferred_element_type=F32) + b_ref[...])
    lane = lax.broadcasted_iota(jnp.int32, (tm, LANES), 1).astype(F32)
    neg = -jnp.inf
    first = lambda hit: jnp.min(jnp.where(hit, lane, float(LANES)), axis=-1, keepdims=True)
    gl = jnp.where(lane < MOE_GROUPS, logits, neg)
    g_max = jnp.max(gl, axis=-1, keepdims=True)
    g_idx = first(gl == g_max)
    p_group = 1.0 / jnp.sum(jnp.exp(gl - g_max), axis=-1, keepdims=True)
    e_lo = MOE_GROUPS + g_idx * EXPERTS_PER_GROUP
    el = jnp.where((lane >= e_lo) & (lane < e_lo + EXPERTS_PER_GROUP), logits, neg)
    v1 = jnp.max(el, axis=-1, keepdims=True)
    i1 = first(el == v1)
    el2 = jnp.where(lane == i1, neg, el)
    v2 = jnp.max(el2, axis=-1, keepdims=True)
    i2 = first(el2 == v2)
    t = jnp.exp(v2 - v1)
    w1 = 1.0 / (1.0 + t)
    gate_ref[...] = jnp.where(lane == 0.0, p_group * w1, jnp.where(lane == 1.0, p_group * (t * w1), 0.0))

    hit1 = lane == i1
    hit2 = lane == i2
    sent = jnp.where(hit1 | hit2, 1.0, 0.0)
    before = jnp.dot(stril_ref[...], sent.astype(jnp.bfloat16), preferred_element_type=F32) + carry[0:1, :]
    r1 = jnp.sum(jnp.where(hit1, before, 0.0), axis=-1, keepdims=True)
    r2 = jnp.sum(jnp.where(hit2, before, 0.0), axis=-1, keepdims=True)
    total = before[tm - 1:tm, :] + sent[tm - 1:tm, :]
    carry[...] = jnp.broadcast_to(total, carry.shape)
    cnt_ref[...] = jnp.broadcast_to(total, cnt_ref.shape).astype(jnp.int32)
    ids = jnp.where(lane == 0.0, i1 - MOE_GROUPS, jnp.where(lane == 1.0, i2 - MOE_GROUPS,
                    jnp.where(lane == 2.0, r1, jnp.where(lane == 3.0, r2, 0.0))))
    id_ref[...] = ids.T[0:SUBLANES, :].astype(jnp.int32)


def _router(h, w_router, b_router):
    tok = h.shape[0]
    tm = TM_ROUTER
    row = lambda i: (i, 0)
    idx = np.arange(tm)
    stril = jnp.asarray((idx[:, None] > idx[None, :]).astype(np.float32), jnp.bfloat16)
    w_hi = w_router.astype(MXU_DTYPE)
    w_hi_lo = (w_hi, (w_router - w_hi.astype(F32)).astype(MXU_DTYPE))
    return pl.pallas_call(
        _router_kernel,
        out_shape=(jax.ShapeDtypeStruct((SUBLANES, tok), jnp.int32), jax.ShapeDtypeStruct((tok, LANES), F32),
                   jax.ShapeDtypeStruct((SUBLANES, LANES), jnp.int32)),
        grid=(tok // tm,),
        in_specs=[pl.BlockSpec((tm, D_MODEL), row), _full((D_MODEL, 2 * LANES)), _full((1, LANES)),
                  _full((tm, tm))],
        out_specs=(pl.BlockSpec((SUBLANES, tm), lambda i: (0, i)), pl.BlockSpec((tm, LANES), row),
                   _full((SUBLANES, LANES))),
        scratch_shapes=[pltpu.VMEM((SUBLANES, LANES), F32)],
        compiler_params=_params("arbitrary"),
        name="moe_router",
    )(h, jnp.concatenate([w_hi, w_hi_lo[1]], axis=1), b_router, stril)


def _pad_pieces(n):
    return tuple(1 << b for b in reversed(range((n - 1).bit_length())))


def _dispatch_kernel(d0_ref, d1_ref, start_ref, cnt_ref, nused_ref, h_ref, xs_hbm, zeros, sem, pad_sem):
    tm = h_ref.shape[0]
    i = pl.program_id(0)
    base = i * tm

    def issue(r, c):
        t = base + r
        pltpu.make_async_copy(h_ref.at[r], xs_hbm.at[d0_ref[t]], sem).start(priority=0)
        pltpu.make_async_copy(h_ref.at[r], xs_hbm.at[d1_ref[t]], sem).start(priority=1)
        return c

    lax.fori_loop(0, tm, issue, 0, unroll=8)

    def pad_copies(e, fn):
        cnt = cnt_ref[e]
        n_pad = (MOE_ROWS - cnt % MOE_ROWS) % MOE_ROWS
        first = start_ref[e] + cnt
        for piece in _pad_pieces(MOE_ROWS):
            @pl.when((n_pad & piece) != 0)
            def _(piece=piece):
                off = first + (n_pad & ~(2 * piece - 1))
                fn(pltpu.make_async_copy(zeros.at[pl.ds(0, piece)], xs_hbm.at[pl.ds(off, piece)], pad_sem))

    def tail_copies(blk, fn):
        for part in range(MOE_ROWS // zeros.shape[0]):
            off = blk * MOE_ROWS + part * zeros.shape[0]
            fn(pltpu.make_async_copy(zeros, xs_hbm.at[pl.ds(off, zeros.shape[0])], pad_sem))

    @pl.when(i == 0)
    def _():
        zeros[...] = jnp.zeros(zeros.shape, F32)
        n_blocks = xs_hbm.shape[0] // MOE_ROWS

        def start(e, c):
            pad_copies(e, lambda cp: cp.start())
            return c

        def wait(e, c):
            pad_copies(e, lambda cp: cp.wait())
            return c

        def tail_start(blk, c):
            tail_copies(blk, lambda cp: cp.start())
            return c

        def tail_wait(blk, c):
            tail_copies(blk, lambda cp: cp.wait())
            return c

        lax.fori_loop(0, N_EXPERTS, start, 0)
        lax.fori_loop(nused_ref[0], n_blocks, tail_start, 0)
        lax.fori_loop(0, N_EXPERTS, wait, 0)
        lax.fori_loop(nused_ref[0], n_blocks, tail_wait, 0)

    for _ in range(TOP_K):
        pltpu.make_async_copy(h_ref, xs_hbm.at[pl.ds(0, tm)], sem).wait()


def _dispatch(h_tiles, dest0, dest1, pad_start, counts, n_used, n_slots):
    tok = h_tiles.shape[0]
    tm = TM_DISPATCH
    assert tok % tm == 0, (tok, tm)
    grid_spec = pltpu.PrefetchScalarGridSpec(
        num_scalar_prefetch=5,
        grid=(tok // tm,),
        in_specs=[pl.BlockSpec((tm,) + ROW_TILE, lambda i, *_: (i, 0, 0))],
        out_specs=pl.BlockSpec(memory_space=pl.ANY),
        scratch_shapes=[pltpu.VMEM((MOE_ROWS // 2,) + ROW_TILE, F32),
                        pltpu.SemaphoreType.DMA(()), pltpu.SemaphoreType.DMA(())],
    )
    return pl.pallas_call(
        _dispatch_kernel,
        out_shape=jax.ShapeDtypeStruct((n_slots,) + ROW_TILE, F32),
        grid_spec=grid_spec,
        compiler_params=_params("arbitrary"),
        name="moe_dispatch",
    )(dest0, dest1, pad_start, counts, n_used, h_tiles)


def _expert_kernel(be_ref, nused_ref, next_ref, slot_ref, x_ref, wg_hbm, wu_hbm, wd_hbm, out_ref,
                   wg_s, wu_s, wd_s, wg_buf, wu_buf, wd_buf, sem, *, layer):
    i = pl.program_id(0)

    def weight_copies(e, slot):
        w = layer * N_EXPERTS + e
        return (pltpu.make_async_copy(wg_hbm.at[w], wg_buf.at[slot], sem.at[slot]),
                pltpu.make_async_copy(wu_hbm.at[w], wu_buf.at[slot], sem.at[slot]),
                pltpu.make_async_copy(wd_hbm.at[w], wd_buf.at[slot], sem.at[slot]))

    @pl.when(i == 0)
    def _():
        for cp in weight_copies(be_ref[0], slot_ref[be_ref[0]]):
            cp.start()

    @pl.when(i < nused_ref[0])
    def _():
        @pl.when((i == 0) | (be_ref[i] != be_ref[jnp.maximum(i - 1, 0)]))
        def _():
            e = be_ref[i]
            slot = slot_ref[e]
            for cp in weight_copies(e, slot):
                cp.wait()

            @pl.when(next_ref[e] >= 0)
            def _():
                for cp in weight_copies(next_ref[e], 1 - slot):
                    cp.start()

            wg_s[...] = wg_buf[slot].astype(MXU_DTYPE)
            wu_s[...] = wu_buf[slot].astype(MXU_DTYPE)
            wd_s[...] = wd_buf[slot].astype(MXU_DTYPE)

        x = _tiles_to_rows(x_ref).astype(MXU_DTYPE)
        n = x.shape[0]
        hidden = []
        for c in range(0, D_EXPERT, FFN_COLS):
            gate = jnp.dot(x, wg_s[:, c:c + FFN_COLS], preferred_element_type=F32)
            up = jnp.dot(x, wu_s[:, c:c + FFN_COLS], preferred_element_type=F32)
            hidden.append((gate * _sigmoid(gate) * up).astype(MXU_DTYPE))
        hidden = jnp.concatenate(hidden, axis=1)
        for c in range(0, D_MODEL, FFN_COLS):
            out = jnp.dot(hidden, wd_s[:, c:c + FFN_COLS], preferred_element_type=F32)
            for s in range(FFN_COLS // LANES):
                out_ref[pl.ds(c // LANES + s, n, stride=ROW_TILE[0]), :] = out[:, s * LANES:(s + 1) * LANES]

    @pl.when(i >= nused_ref[0])
    def _():
        out_ref[...] = jnp.zeros(out_ref.shape, F32)


def _expert_ffn(xs_tiles, block_expert, n_used, counts, w_gate, w_up, w_down, layer):
    n_slots = xs_tiles.shape[0]
    n_blocks = block_expert.shape[0]
    rows = MOE_ROWS * ROW_TILE[0]
    experts = jnp.arange(N_EXPERTS, dtype=jnp.int32)
    nonempty = counts > 0
    later = (experts[None, :] > experts[:, None]) & nonempty[None, :]
    next_expert = jnp.min(jnp.where(later, experts[None, :], N_EXPERTS), axis=1)
    next_expert = jnp.where(next_expert < N_EXPERTS, next_expert, -1).astype(jnp.int32)
    buf_half = ((jnp.cumsum(nonempty.astype(jnp.int32)) - 1) % 2).astype(jnp.int32)
    blk = lambda i, be, nu, nx, sl: (jnp.maximum(jnp.minimum(i, nu[0] - 1), 0), 0)
    grid_spec = pltpu.PrefetchScalarGridSpec(
        num_scalar_prefetch=4,
        grid=(n_blocks,),
        in_specs=[
            pl.BlockSpec((rows, LANES), blk),
            pl.BlockSpec(memory_space=pl.ANY),
            pl.BlockSpec(memory_space=pl.ANY),
            pl.BlockSpec(memory_space=pl.ANY),
        ],
        out_specs=pl.BlockSpec((rows, LANES), lambda i, be, nu, nx, sl: (i, 0)),
        scratch_shapes=[pltpu.VMEM((D_MODEL, D_EXPERT), MXU_DTYPE), pltpu.VMEM((D_MODEL, D_EXPERT), MXU_DTYPE),
                        pltpu.VMEM((D_EXPERT, D_MODEL), MXU_DTYPE),
                        pltpu.VMEM((2, D_MODEL, D_EXPERT), F32), pltpu.VMEM((2, D_MODEL, D_EXPERT), F32),
                        pltpu.VMEM((2, D_EXPERT, D_MODEL), F32), pltpu.SemaphoreType.DMA((2,))],
    )
    out = pl.pallas_call(
        functools.partial(_expert_kernel, layer=layer),
        out_shape=jax.ShapeDtypeStruct((n_slots * ROW_TILE[0], LANES), F32),
        grid_spec=grid_spec,
        compiler_params=_params("arbitrary"),
        name="moe_experts",
    )(block_expert, n_used, next_expert, buf_half, xs_tiles.reshape(n_slots * ROW_TILE[0], LANES),
      w_gate, w_up, w_down)
    return out.reshape((n_slots,) + ROW_TILE)


def _combine_ln_kernel(d0_ref, d1_ref, rows_hbm, h_ref, gate_ref, g_ref, b_ref, o_ref, buf, sem):
    tm = h_ref.shape[0]
    i = pl.program_id(0)

    def gather(tile, p):
        base = tile * tm

        def issue(r, c):
            t = base + r
            dst = pl.ds(pl.multiple_of(r * ROW_TILE[0], ROW_TILE[0]), ROW_TILE[0])
            pltpu.make_async_copy(rows_hbm.at[d0_ref[t]], buf.at[p, 0, dst],
                                  sem.at[p]).start(priority=0)
            pltpu.make_async_copy(rows_hbm.at[d1_ref[t]], buf.at[p, 1, dst],
                                  sem.at[p]).start(priority=1)
            return c

        lax.fori_loop(0, tm, issue, 0, unroll=8)

    @pl.when(i == 0)
    def _():
        gather(0, 0)

    for p in range(2):
        @pl.when(i % 2 == p)
        def _(p=p):
            @pl.when(i + 1 < pl.num_programs(0))
            def _():
                gather(i + 1, 1 - p)

            for k in range(TOP_K):
                pltpu.make_async_copy(buf.at[1 - p, k], buf.at[p, k], sem.at[p]).wait()
            gate = gate_ref[...]
            acc = (ALPHA * h_ref[...] + gate[:, 0:1] * _tiles_to_rows(buf.at[p, 0])
                   + gate[:, 1:2] * _tiles_to_rows(buf.at[p, 1]))
            o_ref[...] = _layer_norm_rows(acc, g_ref[...], b_ref[...])


def _combine_ln(dest0, dest1, rows_tiles, h, gates, g_row, b_row):
    tok = h.shape[0]
    tm = TM_COMBINE
    row = lambda i, *_: (i, 0)
    const = lambda i, *_: (0, 0)
    grid_spec = pltpu.PrefetchScalarGridSpec(
        num_scalar_prefetch=2,
        grid=(tok // tm,),
        in_specs=[
            pl.BlockSpec(memory_space=pl.ANY),
            pl.BlockSpec((tm, D_MODEL), row),
            pl.BlockSpec((tm, LANES), row),
            pl.BlockSpec((1, D_MODEL), const),
            pl.BlockSpec((1, D_MODEL), const),
        ],
        out_specs=pl.BlockSpec((tm, D_MODEL), row),
        scratch_shapes=[pltpu.VMEM((2, TOP_K, tm * ROW_TILE[0], LANES), F32), pltpu.SemaphoreType.DMA((2,))],
    )
    return pl.pallas_call(
        _combine_ln_kernel,
        out_shape=jax.ShapeDtypeStruct((tok, D_MODEL), F32),
        grid_spec=grid_spec,
        compiler_params=_params("arbitrary"),
        name="moe_combine_ln",
    )(dest0, dest1, rows_tiles, h, gates, g_row, b_row)


def _row_slots_kernel(start_ref, ids_ref, out_ref):
    expert = ids_ref[0:TOP_K, :]
    start = jnp.zeros(expert.shape, jnp.int32)
    for e in range(N_EXPERTS):
        start = jnp.where(expert == e, start_ref[e], start)
    out_ref[...] = jnp.zeros(out_ref.shape, jnp.int32)
    out_ref[0:TOP_K, :] = start + ids_ref[TOP_K:2 * TOP_K, :]


def _row_slots(ids, pad_start):
    grid_spec = pltpu.PrefetchScalarGridSpec(
        num_scalar_prefetch=1,
        grid=(1,),
        in_specs=[pl.BlockSpec(ids.shape, lambda i, s: (0, 0))],
        out_specs=pl.BlockSpec(ids.shape, lambda i, s: (0, 0)),
    )
    return pl.pallas_call(
        _row_slots_kernel,
        out_shape=jax.ShapeDtypeStruct(ids.shape, jnp.int32),
        grid_spec=grid_spec,
        name="moe_row_slots",
    )(pad_start, ids)


def _slot_layout(counts, n_blocks):
    padded = (counts + MOE_ROWS - 1) // MOE_ROWS * MOE_ROWS
    pad_end = jnp.cumsum(padded)
    n_used = pad_end[-1:] // MOE_ROWS
    blocks = jnp.arange(n_blocks, dtype=jnp.int32)
    first_row = jnp.minimum(blocks, n_used - 1) * MOE_ROWS
    block_expert = jnp.sum(first_row[:, None] >= pad_end[None, :], axis=1)
    return (pad_end - padded).astype(jnp.int32), block_expert.astype(jnp.int32), n_used.astype(jnp.int32)


def _moe_ln(h, h_tiles, w_router, b_router, w_gate, w_up, w_down, layer, g_row, b_row):
    tok = h.shape[0]
    n_blocks = tok * TOP_K // MOE_ROWS + N_EXPERTS
    ids, gates, counts = _router(h, w_router, b_router)
    counts = counts[0, MOE_GROUPS:MOE_GROUPS + N_EXPERTS]
    pad_start, block_expert, n_used = _slot_layout(counts, n_blocks)
    slots = _row_slots(ids, pad_start)
    dest0, dest1 = slots[0], slots[1]
    xs = _dispatch(h_tiles.reshape((tok,) + ROW_TILE), dest0, dest1, pad_start, counts, n_used,
                   n_blocks * MOE_ROWS)
    rows = _expert_ffn(xs, block_expert, n_used, counts, w_gate, w_up, w_down, layer)
    return _combine_ln(dest0, dest1, rows, h, gates, g_row, b_row)


def _pad_lanes(v):
    return jnp.pad(v, (0, LANES - v.shape[0])).reshape(1, LANES)


def _even_mixer(h, bsz, seq, w_in, conv_a, conv_w, conv_b, dt_bias, a_log, d_skip, norm_w, w_out, g_row, b_row):
    w = jnp.pad(w_in, ((0, 0), (0, AB_PROJ - w_in.shape[1]))).astype(MXU_DTYPE)
    y_a, z, xbc, dt, acs, acst = _even_front(
        h, w, conv_a, conv_w, conv_b.reshape(1, -1), _pad_lanes(dt_bias), _pad_lanes(-jnp.exp(a_log)), bsz, seq)
    dskip_row = jnp.repeat(d_skip, SSM_HEAD_DIM).reshape(1, -1)
    y_b = _ssd(xbc, dt, acs, acst, z, dskip_row, norm_w.reshape(1, -1), bsz, seq)
    return _outproj_ln([y_a, y_b], h, w_out.astype(MXU_DTYPE), g_row, b_row)


def _odd_mixer(h, bsz, seq, w_in, i_bias, f_bias, hnorm_w, fox_f_bias, w_out, g_row, b_row):
    c = np.cumsum((0, MLSTM_W, MLSTM_W, MLSTM_W, MLSTM_HEADS, MLSTM_HEADS, MLSTM_W, FOX_W, FOX_W, FOX_W, FOX_HEADS))
    part = lambda j: w_in[:, c[j]:c[j + 1]]
    q, k, v, i_pre, f_pre, o_pre, fq, fk, fv, ff = (part(j) for j in range(10))
    gate_cols = jnp.concatenate([i_pre, f_pre, ff, f_pre], axis=1)
    gate_cols = jnp.pad(gate_cols, ((0, 0), (0, LANES - gate_cols.shape[1])))
    spread = lambda m: jnp.pad(m.reshape(-1, FOX_HEADS, FOX_HEAD_DIM),
                               ((0, 0), (0, 0), (0, LANES - FOX_HEAD_DIM))).reshape(-1, FOX_AUG)
    w = jnp.concatenate([q, k, v, o_pre, fq, spread(fk), fv, gate_cols], axis=1).astype(MXU_DTYPE)
    gate_bias = _pad_lanes(jnp.concatenate([i_bias, f_bias, fox_f_bias, f_bias]))
    q, k, v, o, fq, fk_aug, fvt_aug, gates, gates_t = _odd_front(h, w, gate_bias, bsz, seq)
    y_c = _mlstm(q, k, v, o, gates, gates_t, hnorm_w.reshape(1, -1), bsz, seq)
    y_d = _fox(fq, fk_aug, fvt_aug, gates_t, bsz, seq)
    return _outproj_ln([y_c, y_d], h, w_out.astype(MXU_DTYPE), g_row, b_row)


def kernel(x, ab_w_in, ab_conv_a, ab_conv_ssm_w, ab_conv_ssm_b, ab_dt_bias, ab_a_log, ab_d_skip, ab_norm_w, ab_w_out, cd_w_in, cd_i_bias, cd_f_bias, cd_hnorm_w, cd_fox_f_bias, cd_w_out, ln1_g, ln1_b, ln2_g, ln2_b, moe_rg_w, moe_rg_b, moe_re_w, moe_re_b, moe_w_gate, moe_w_up, moe_w_down):
    bsz, seq, d = x.shape
    h = x.reshape(bsz * seq, d)
    stack = lambda w: w.reshape((w.shape[0] * w.shape[1],) + w.shape[2:])
    w_gate, w_up, w_down = stack(moe_w_gate), stack(moe_w_up), stack(moe_w_down)
    for layer in range(DEPTH):
        j = layer // 2
        g1, b1 = ln1_g[layer].reshape(1, -1), ln1_b[layer].reshape(1, -1)
        if layer % 2 == 0:
            h, h_tiles = _even_mixer(h, bsz, seq, ab_w_in[j], ab_conv_a[j], ab_conv_ssm_w[j], ab_conv_ssm_b[j],
                                     ab_dt_bias[j], ab_a_log[j], ab_d_skip[j], ab_norm_w[j], ab_w_out[j], g1, b1)
        else:
            h, h_tiles = _odd_mixer(h, bsz, seq, cd_w_in[j], cd_i_bias[j], cd_f_bias[j], cd_hnorm_w[j],
                                    cd_fox_f_bias[j], cd_w_out[j], g1, b1)
        re_w = jnp.transpose(moe_re_w[layer], (1, 0, 2)).reshape(d, N_EXPERTS)
        w_router = jnp.pad(jnp.concatenate([moe_rg_w[layer], re_w], axis=1),
                           ((0, 0), (0, LANES - MOE_GROUPS - N_EXPERTS)))
        b_router = _pad_lanes(jnp.concatenate([moe_rg_b[layer], moe_re_b[layer].reshape(-1)]))
        h = _moe_ln(h, h_tiles, w_router, b_router, w_gate, w_up, w_down, layer,
                    ln2_g[layer].reshape(1, -1), ln2_b[layer].reshape(1, -1))
    return h.reshape(bsz, seq, d)
```

```python
import functools

import numpy as np
import jax
import jax.numpy as jnp
from jax import lax
from jax.experimental import pallas as pl
from jax.experimental.pallas import tpu as pltpu

F32 = jnp.float32
MXU_DTYPE = jnp.bfloat16

D_MODEL = 1024
DEPTH = 4
ALPHA = (2 * DEPTH) ** 0.25
LN_EPS = 1e-5
CONV_DIM = D_MODEL // 2
CONV_WIDTH = 3
SSM_D_INNER = D_MODEL
SSM_HEAD_DIM = 64
SSM_HEADS = SSM_D_INNER // SSM_HEAD_DIM
SSM_GROUPS = 4
SSM_STATE = 64
SSM_CONV = 4
SSM_BC = SSM_GROUPS * SSM_STATE
SSM_CONV_DIM = SSM_D_INNER + 2 * SSM_BC
MLSTM_HEADS = 4
MLSTM_HEAD_DIM = D_MODEL // 8
MLSTM_W = MLSTM_HEADS * MLSTM_HEAD_DIM
FOX_HEADS = 8
FOX_HEAD_DIM = D_MODEL // 16
FOX_W = FOX_HEADS * FOX_HEAD_DIM
MOE_GROUPS = 4
EXPERTS_PER_GROUP = 8
N_EXPERTS = MOE_GROUPS * EXPERTS_PER_GROUP
TOP_K = 2
D_EXPERT = D_MODEL // 2

LANES = 128
SUBLANES = 8
VMEM_LIMIT_BYTES = 56 * 1024 * 1024

CHUNK = 128
SEQ_PAIR = 4
MLSTM_SEQS = 1
MLSTM_CHUNK = 512
TM_FRONT = 512
TM_EVEN_FRONT = 256
CONV_ROWS, CONV_LANES = 64, 512
TM_OUT = 1024
TM_ROUTER = 512
TQ_FOX = 256
MOE_ROWS = 512
FFN_COLS = 256
TM_COMBINE = 512
TM_DISPATCH = 2048

AB_PROJ = 4224
FOX_AUG = FOX_HEADS * LANES
CD_FK = 5 * 512
CD_FV = CD_FK + FOX_AUG
CD_GATES = CD_FV + FOX_W
CD_PROJ = CD_GATES + LANES
LOG2E = 1.4426950408889634
TK_FOX = 128
FOX_ACC_ROWS = FOX_HEAD_DIM + SUBLANES
FOX_VROWS = LANES
G_I, G_F, G_FOX, G_BCUM = 0, 4, 8, 16


def _params(*sem):
    return pltpu.CompilerParams(dimension_semantics=sem, vmem_limit_bytes=VMEM_LIMIT_BYTES)


def _softplus(x):
    return jnp.maximum(x, 0.0) + jnp.log(1.0 + jnp.exp(-jnp.abs(x)))


def _sigmoid(x):
    return 1.0 / (1.0 + jnp.exp(-x))


def _layer_norm_rows(v, g, b):
    mu = jnp.mean(v, axis=-1, keepdims=True)
    c = v - mu
    var = jnp.mean(c * c, axis=-1, keepdims=True)
    return c * lax.rsqrt(var + LN_EPS) * g + b


def _tril(n, block):
    i = np.arange(n)
    m = (i[:, None] >= i[None, :]) & (i[:, None] // block == i[None, :] // block)
    return jnp.asarray(m.astype(np.float32), MXU_DTYPE)


def _full(shape):
    return pl.BlockSpec(shape, lambda *_: (0,) * len(shape), pipeline_mode=pl.Buffered(1))


ROW_TILE = (D_MODEL // LANES, LANES)


def _tiles_to_rows(ref):
    n = ref.shape[0] // ROW_TILE[0]
    return jnp.concatenate([ref[pl.ds(s, n, stride=ROW_TILE[0]), :] for s in range(ROW_TILE[0])], axis=1)


def _rows_to_tiles(ref, val):
    n = val.shape[0]
    for s in range(ROW_TILE[0]):
        ref[pl.ds(s, n, stride=ROW_TILE[0]), :] = val[:, s * LANES:(s + 1) * LANES]


def _split3(x):
    narrow = lambda v: v.astype(MXU_DTYPE).astype(F32)
    x1 = narrow(x)
    x2 = narrow(x - x1)
    return x1, x2, narrow(x - x1 - x2)


def _cumsum_rows(tril, x):
    parts = jnp.dot(tril, jnp.concatenate(_split3(x), axis=1).astype(MXU_DTYPE), preferred_element_type=F32)
    return parts[:, 0:LANES] + parts[:, LANES:2 * LANES] + parts[:, 2 * LANES:3 * LANES]


def _even_front_kernel(h_ref, w_ref, ca_ref, cw_ref, cb_ref, dtb_ref, aneg_ref, tril_ref,
                       ya_ref, z_ref, xbc_ref, dt_ref, acs_ref, acst_ref,
                       proj_buf, ua_ext, xbc_ext, *, tiles_per_seq):
    tm = h_ref.shape[0]
    i = pl.program_id(0)
    cur = i % 2

    @pl.when(i == 0)
    def _():
        proj_buf[1] = jnp.zeros(proj_buf.shape[1:], F32)

    @pl.when((i == 0) | ((i - 1) % tiles_per_seq == 0))
    def _():
        ua_ext[0:SUBLANES, :] = jnp.zeros((SUBLANES, CONV_DIM), F32)
        xbc_ext[0:SUBLANES, :] = jnp.zeros((SUBLANES, SSM_CONV_DIM), F32)

    z0 = 3 * CONV_DIM
    x0 = z0 + SSM_D_INNER
    d0 = x0 + SSM_CONV_DIM

    x_in = h_ref[...].astype(MXU_DTYPE)
    proj = proj_buf.at[1 - cur]
    z_ref[...] = proj[:, z0:x0]

    n_blocks = tm // CONV_ROWS
    col_cuts = [AB_PROJ * g // n_blocks // LANES * LANES for g in range(n_blocks)] + [AB_PROJ]
    for r0 in range(0, tm, CONV_ROWS):
        c_lo, c_hi = col_cuts[r0 // CONV_ROWS], col_cuts[r0 // CONV_ROWS + 1]
        proj_buf[cur, :, c_lo:c_hi] = jnp.dot(x_in, w_ref[:, c_lo:c_hi], preferred_element_type=F32)
        rows = slice(r0, r0 + CONV_ROWS)
        ext_rows = slice(SUBLANES + r0, SUBLANES + r0 + CONV_ROWS)
        ua_ext[ext_rows, :] = proj[rows, CONV_DIM:2 * CONV_DIM] * proj[rows, 2 * CONV_DIM:3 * CONV_DIM]
        conv = None
        for k in range(CONV_WIDTH):
            tap = ca_ref[k:k + 1, :] * ua_ext[pl.ds(SUBLANES + r0 - (CONV_WIDTH - 1) + k, CONV_ROWS), :]
            conv = tap if conv is None else conv + tap
        ya_ref[rows, :] = (proj[rows, 0:CONV_DIM] * conv).astype(ya_ref.dtype)
        for c0 in range(0, SSM_CONV_DIM, CONV_LANES):
            cols = slice(c0, c0 + CONV_LANES)
            xbc_ext[ext_rows, cols] = proj[rows, x0 + c0:x0 + c0 + CONV_LANES]
            conv = cb_ref[:, cols]
            for k in range(SSM_CONV):
                conv = conv + cw_ref[k:k + 1, cols] * xbc_ext[pl.ds(SUBLANES + r0 - (SSM_CONV - 1) + k, CONV_ROWS), cols]
            xbc_ref[rows, cols] = conv * _sigmoid(conv)
    ua_ext[0:SUBLANES, :] = ua_ext[tm:tm + SUBLANES, :]
    xbc_ext[0:SUBLANES, :] = xbc_ext[tm:tm + SUBLANES, :]

    dt = _softplus(proj[:, d0:d0 + LANES] + dtb_ref[...])
    a = dt * aneg_ref[...]
    acs = _cumsum_rows(tril_ref[...], a)
    dt_ref[...] = dt
    acs_ref[...] = acs
    acst_ref[...] = acs.T


def _even_front(h, w_in, conv_a, conv_w, conv_b, dt_bias_row, aneg_row, bsz, seq):
    tok = bsz * seq
    tm = TM_EVEN_FRONT
    n_tiles = tok // tm
    row = lambda i: (jnp.maximum(i - 1, 0), 0)
    out_shapes = (
        jax.ShapeDtypeStruct((tok, CONV_DIM), MXU_DTYPE),
        jax.ShapeDtypeStruct((tok, SSM_D_INNER), F32),
        jax.ShapeDtypeStruct((tok, SSM_CONV_DIM), F32),
        jax.ShapeDtypeStruct((tok, LANES), F32),
        jax.ShapeDtypeStruct((tok, LANES), F32),
        jax.ShapeDtypeStruct((LANES, tok), F32),
    )
    return pl.pallas_call(
        functools.partial(_even_front_kernel, tiles_per_seq=seq // tm),
        out_shape=out_shapes,
        grid=(n_tiles + 1,),
        in_specs=[
            pl.BlockSpec((tm, D_MODEL), lambda i: (jnp.minimum(i, n_tiles - 1), 0)),
            _full((D_MODEL, AB_PROJ)),
            _full((CONV_WIDTH, CONV_DIM)),
            _full((SSM_CONV, SSM_CONV_DIM)),
            _full((1, SSM_CONV_DIM)),
            _full((1, LANES)),
            _full((1, LANES)),
            _full((tm, tm)),
        ],
        out_specs=(
            pl.BlockSpec((tm, CONV_DIM), row),
            pl.BlockSpec((tm, SSM_D_INNER), row),
            pl.BlockSpec((tm, SSM_CONV_DIM), row),
            pl.BlockSpec((tm, LANES), row),
            pl.BlockSpec((tm, LANES), row),
            pl.BlockSpec((LANES, tm), lambda i: (0, jnp.maximum(i - 1, 0))),
        ),
        scratch_shapes=[
            pltpu.VMEM((2, tm, AB_PROJ), F32),
            pltpu.VMEM((tm + SUBLANES, CONV_DIM), F32),
            pltpu.VMEM((tm + SUBLANES, SSM_CONV_DIM), F32),
        ],
        compiler_params=_params("arbitrary"),
        name="even_front",
    )(h, w_in, conv_a, conv_w, conv_b, dt_bias_row, aneg_row, _tril(tm, CHUNK))


def _bcast_heads(arr, n_heads, width):
    per = LANES // width
    length = arr.shape[0]
    lane = lax.broadcasted_iota(jnp.int32, (length, LANES), 1)
    outs = []
    for j in range(n_heads // per):
        v = jnp.broadcast_to(arr[:, j * per:j * per + 1], (length, LANES))
        for r in range(1, per):
            v = jnp.where(lane >= r * width, jnp.broadcast_to(arr[:, j * per + r:j * per + r + 1], (length, LANES)), v)
        outs.append(v)
    return jnp.concatenate(outs, axis=1)


def _expand_heads(arr, expand_ref):
    hi, lo, _ = _split3(arr)
    return jnp.dot(jnp.concatenate([hi, lo], axis=1).astype(MXU_DTYPE), expand_ref[...],
                   preferred_element_type=F32)


def _ssd_kernel(*refs):
    nb = SEQ_PAIR
    xbc_ref, dt_ref, acs_ref, z_ref = refs[:4]
    acst_refs = refs[4:4 + nb]
    dskip_ref, nw_ref, expand_ref, y_ref, state = refs[4 + nb:]
    L = CHUNK
    P = SSM_HEAD_DIM
    R = SSM_HEADS // SSM_GROUPS
    GW = R * P

    @pl.when(pl.program_id(1) == 0)
    def _():
        state[...] = jnp.zeros(state.shape, F32)

    row = lax.broadcasted_iota(jnp.int32, (L, L), 0)
    col = lax.broadcasted_iota(jnp.int32, (L, L), 1)
    causal = row >= col
    lane_g = lax.broadcasted_iota(jnp.int32, (L, GW), 1)

    per_row = []
    for j in range(nb):
        acs = acs_ref[j]
        per_row += [dt_ref[j], jnp.exp(acs[L - 1:L, :] - acs), jnp.exp(acs)]
    expanded = _expand_heads(jnp.concatenate(per_row, axis=0), expand_ref)

    st_old = [state[c] for c in range(nb * SSM_GROUPS)]
    st_new = []
    for j in range(nb):
        xs = xbc_ref[j, :, 0:SSM_D_INNER]
        bm = xbc_ref[j, :, SSM_D_INNER:SSM_D_INNER + SSM_BC]
        cm = xbc_ref[j, :, SSM_D_INNER + SSM_BC:SSM_CONV_DIM]
        acs = acs_ref[j]
        acst = acst_refs[j][...]
        a_last = acs[L - 1:L, :]
        dtx = expanded[(3 * j) * L:(3 * j + 1) * L, :]
        decx = expanded[(3 * j + 1) * L:(3 * j + 2) * L, :]
        expx = expanded[(3 * j + 2) * L:(3 * j + 3) * L, :]
        xdt = xs * dtx
        xdec = (xdt * decx).astype(MXU_DTYPE)
        xdt_m = xdt.astype(MXU_DTYPE)
        chunk_decay = jnp.exp(jnp.broadcast_to(a_last, (SUBLANES, LANES)))
        cdx = _bcast_heads(chunk_decay, SSM_HEADS, P)[0:1, :]

        bm_t = bm.T.astype(MXU_DTYPE)
        cm_m = cm.astype(MXU_DTYPE)
        bm_m = bm.astype(MXU_DTYPE)

        ys = []
        for g in range(SSM_GROUPS):
            n0 = g * SSM_STATE
            c_g = cm_m[:, n0:n0 + SSM_STATE]
            cb = lax.dot_general(c_g, bm_m[:, n0:n0 + SSM_STATE], (((1,), (1,)), ((), ())),
                                 preferred_element_type=F32)
            ms = []
            for r in range(R):
                hd = g * R + r
                seg = jnp.exp(jnp.where(causal, acs[:, hd:hd + 1] - acst[hd:hd + 1, :], -jnp.inf))
                ms.append((cb * seg).astype(MXU_DTYPE))
            big = jnp.dot(jnp.concatenate(ms, axis=0), xdt_m[:, g * GW:(g + 1) * GW],
                          preferred_element_type=F32)
            y_diag = big[0:L, :]
            for r in range(1, R):
                y_diag = jnp.where(lane_g >= r * P, big[r * L:(r + 1) * L, :], y_diag)
            st = st_old[j * SSM_GROUPS + g]
            y_off = jnp.dot(c_g, st.astype(MXU_DTYPE), preferred_element_type=F32)
            new = jnp.dot(bm_t[n0:n0 + SSM_STATE, :], xdec[:, g * GW:(g + 1) * GW],
                          preferred_element_type=F32)
            st_new.append(st * cdx[:, g * GW:(g + 1) * GW] + new)
            ys.append(y_diag + y_off * expx[:, g * GW:(g + 1) * GW])
        y = jnp.concatenate(ys, axis=1) + dskip_ref[...] * xs
        z = z_ref[j]
        u = y * (z * _sigmoid(z))
        y = u * lax.rsqrt(jnp.mean(u * u, axis=-1, keepdims=True) + LN_EPS) * nw_ref[...]
        y_ref[j] = y.astype(y_ref.dtype)

    for c in range(nb * SSM_GROUPS):
        state[c] = st_new[c]


def _ssd(xbc, dt, acs, acst, z, dskip_row, normw_row, bsz, seq):
    nb = SEQ_PAIR
    nc = seq // CHUNK
    assert bsz % nb == 0 and seq % CHUNK == 0, (bsz, seq)
    per_seq = lambda a: a.reshape(bsz, seq, a.shape[-1])
    blk = lambda width: pl.BlockSpec((nb, CHUNK, width), lambda g, c: (g, c, 0))
    acst_specs = [pl.BlockSpec((LANES, CHUNK), lambda g, c, j=j: (0, (g * nb + j) * nc + c)) for j in range(nb)]
    expand = np.zeros((LANES, SSM_D_INNER), np.float32)
    for hd in range(SSM_HEADS):
        expand[hd, hd * SSM_HEAD_DIM:(hd + 1) * SSM_HEAD_DIM] = 1.0
    expand = jnp.asarray(np.concatenate([expand] * 2, axis=0), MXU_DTYPE)
    y = pl.pallas_call(
        _ssd_kernel,
        out_shape=jax.ShapeDtypeStruct((bsz, seq, SSM_D_INNER), MXU_DTYPE),
        grid=(bsz // nb, nc),
        in_specs=[blk(SSM_CONV_DIM), blk(LANES), blk(LANES), blk(SSM_D_INNER)] + acst_specs + [
            _full((1, SSM_D_INNER)),
            _full((1, SSM_D_INNER)),
            _full((2 * LANES, SSM_D_INNER)),
        ],
        out_specs=blk(SSM_D_INNER),
        scratch_shapes=[pltpu.VMEM((nb * SSM_GROUPS, SSM_STATE, SSM_D_INNER // SSM_GROUPS), F32)],
        compiler_params=_params("arbitrary", "arbitrary"),
        name="ssd_scan",
    )(per_seq(xbc), per_seq(dt), per_seq(acs), per_seq(z), *([acst] * nb), dskip_row, normw_row, expand)
    return y.reshape(bsz * seq, SSM_D_INNER)


def _outproj_ln_kernel(*refs, widths):
    n = len(widths)
    parts = refs[:n]
    h_ref, w_ref, g_ref, b_ref, o_ref, ot_ref = refs[n:]
    acc = ALPHA * h_ref[...]
    off = 0
    for p, wd in zip(parts, widths):
        acc = acc + jnp.dot(p[...].astype(MXU_DTYPE), w_ref[off:off + wd, :], preferred_element_type=F32)
        off += wd
    out = _layer_norm_rows(acc, g_ref[...], b_ref[...])
    o_ref[...] = out
    _rows_to_tiles(ot_ref, out)


def _outproj_ln(parts, h, w_out, g_row, b_row):
    tok = h.shape[0]
    tm = TM_OUT
    widths = tuple(p.shape[1] for p in parts)
    row = lambda i: (i, 0)
    return pl.pallas_call(
        functools.partial(_outproj_ln_kernel, widths=widths),
        out_shape=(jax.ShapeDtypeStruct((tok, D_MODEL), F32),
                   jax.ShapeDtypeStruct((tok * ROW_TILE[0], LANES), F32)),
        grid=(tok // tm,),
        in_specs=[pl.BlockSpec((tm, wd), row) for wd in widths] + [
            pl.BlockSpec((tm, D_MODEL), row),
            _full((sum(widths), D_MODEL)),
            _full((1, D_MODEL)),
            _full((1, D_MODEL)),
        ],
        out_specs=(pl.BlockSpec((tm, D_MODEL), row), pl.BlockSpec((tm * ROW_TILE[0], LANES), row)),
        compiler_params=_params("arbitrary"),
        name="outproj_ln",
    )(*parts, h, w_out, g_row, b_row)


def _odd_front_kernel(h_ref, w_ref, gb_ref, tril_ref,
                      q_ref, k_ref, v_ref, o_ref, fq_ref, fk_ref, fvt_ref, g_ref, gt_ref, carry):
    tm = h_ref.shape[0]
    x_in = h_ref[...].astype(MXU_DTYPE)
    proj_cols = lambda lo, hi: jnp.dot(x_in, w_ref[:, lo:hi], preferred_element_type=F32)

    @pl.when(pl.program_id(1) == 0)
    def _():
        carry[...] = jnp.zeros(carry.shape, F32)

    raw = proj_cols(CD_GATES, CD_GATES + LANES) + gb_ref[...]
    lane = lax.broadcasted_iota(jnp.int32, (tm, LANES), 1)
    g = jnp.where(lane < G_F, raw, -_softplus(-raw))
    prev = carry[0:1, :]
    glob = _cumsum_rows(tril_ref[...], g) + prev
    before = []
    for c in range(tm // MLSTM_CHUNK):
        before.append(jnp.broadcast_to(prev, (MLSTM_CHUNK, LANES)))
        prev = glob[(c + 1) * MLSTM_CHUNK - 1:(c + 1) * MLSTM_CHUNK, :]
    carry[...] = jnp.broadcast_to(prev, carry.shape)
    local = glob - jnp.concatenate(before, axis=0)
    out = jnp.where((lane >= G_FOX) & (lane < G_BCUM), glob, jnp.where(lane >= G_BCUM, local, g))
    g_ref[...] = out
    gt_ref[...] = out.T

    q_ref[...] = proj_cols(0, 512)
    k_ref[...] = proj_cols(512, 1024) * (MLSTM_HEAD_DIM ** -0.5)
    v_ref[...] = proj_cols(1024, 1536)
    o_ref[...] = _sigmoid(proj_cols(1536, 2048))
    fq_ref[...] = proj_cols(2048, 2560) * (FOX_HEAD_DIM ** -0.5 * LOG2E)

    is_bias = (lane >= FOX_HEAD_DIM) & (lane < FOX_HEAD_DIM + 3)
    fk = proj_cols(CD_FK, CD_FK + FOX_AUG)
    for hd in range(FOX_HEADS):
        c1, c2, c3 = _split3(out[:, G_FOX + hd:G_FOX + hd + 1] * (-LOG2E))
        bias = jnp.where(lane == FOX_HEAD_DIM, c1, jnp.where(lane == FOX_HEAD_DIM + 1, c2, c3))
        k_h = fk[:, hd * LANES:(hd + 1) * LANES]
        fk_ref[:, hd * LANES:(hd + 1) * LANES] = jnp.where(is_bias, bias, k_h).astype(MXU_DTYPE)
    v_t = proj_cols(CD_FV, CD_FV + FOX_W).T
    extra = jnp.where(lax.broadcasted_iota(jnp.int32, (FOX_VROWS - FOX_HEAD_DIM, tm), 0) == 0, 1.0, 0.0)
    fvt_ref[...] = jnp.concatenate(
        [blk for hd in range(FOX_HEADS) for blk in (v_t[hd * FOX_HEAD_DIM:(hd + 1) * FOX_HEAD_DIM, :], extra)],
        axis=0).astype(MXU_DTYPE)


def _odd_front(h, w_in, gate_bias_row, bsz, seq):
    tok = bsz * seq
    tm = TM_FRONT
    ns = seq // tm
    row = lambda b, s: (b * ns + s, 0)
    col = lambda b, s: (0, b * ns + s)
    wide = jax.ShapeDtypeStruct((tok, 512), F32)
    return pl.pallas_call(
        _odd_front_kernel,
        out_shape=(wide,) * 5 + (jax.ShapeDtypeStruct((tok, FOX_AUG), MXU_DTYPE),
                                 jax.ShapeDtypeStruct((FOX_HEADS * FOX_VROWS, tok), MXU_DTYPE),
                                 jax.ShapeDtypeStruct((tok, LANES), F32),
                                 jax.ShapeDtypeStruct((LANES, tok), F32)),
        grid=(bsz, ns),
        in_specs=[
            pl.BlockSpec((tm, D_MODEL), row),
            _full((D_MODEL, CD_PROJ)),
            _full((1, LANES)),
            _full((tm, tm)),
        ],
        out_specs=(pl.BlockSpec((tm, 512), row),) * 5 + (
            pl.BlockSpec((tm, FOX_AUG), row),
            pl.BlockSpec((FOX_HEADS * FOX_VROWS, tm), col),
            pl.BlockSpec((tm, LANES), row),
            pl.BlockSpec((LANES, tm), col),
        ),
        scratch_shapes=[pltpu.VMEM((SUBLANES, LANES), F32)],
        compiler_params=_params("arbitrary", "arbitrary"),
        name="odd_front",
    )(h, w_in, gate_bias_row, _tril(tm, tm))


def _mlstm_kernel(*refs):
    nb = MLSTM_SEQS
    q_ref, k_ref, v_ref, o_ref, g_ref = refs[:5]
    gt_refs = refs[5:5 + nb]
    nw_ref, y_ref, c_state, m_state = refs[5 + nb:]
    L = MLSTM_CHUNK
    DH = MLSTM_HEAD_DIM

    @pl.when(pl.program_id(1) == 0)
    def _():
        c_state[...] = jnp.zeros(c_state.shape, F32)
        m_state[...] = jnp.zeros(m_state.shape, F32)

    row = lax.broadcasted_iota(jnp.int32, (L, L), 0)
    col = lax.broadcasted_iota(jnp.int32, (L, L), 1)
    causal = row >= col
    ones_col = jnp.where(lax.broadcasted_iota(jnp.int32, (L, DH), 1) == 0, 1.0, 0.0)

    for j in range(nb):
        gates = g_ref[j]
        gates_t = gt_refs[j][...]
        for hd in range(MLSTM_HEADS):
            st = j * MLSTM_HEADS + hd
            sl = slice(hd * DH, (hd + 1) * DH)
            q = q_ref[j, :, sl].astype(MXU_DTYPE)
            k = k_ref[j, :, sl]
            v_ext = jnp.concatenate([v_ref[j, :, sl], ones_col], axis=1).astype(MXU_DTYPE)
            b_col = gates[:, G_BCUM + hd:G_BCUM + hd + 1]
            i_col = gates[:, G_I + hd:G_I + hd + 1]
            b_row = gates_t[G_BCUM + hd:G_BCUM + hd + 1, :]
            i_row = gates_t[G_I + hd:G_I + hd + 1, :]
            m_prev = m_state[st:st + 1, 0:1]
            c_ext = c_state[st]

            d_mat = jnp.where(causal, b_col - b_row + i_row, -jnp.inf)
            inter = b_col + m_prev
            m_t = jnp.maximum(jnp.max(d_mat, axis=-1, keepdims=True), inter)
            s_qk = lax.dot_general(q, k.astype(MXU_DTYPE), (((1,), (1,)), ((), ())), preferred_element_type=F32)
            w_qk = s_qk * jnp.exp(d_mat - m_t)
            s_inter = jnp.exp(inter - m_t)
            num_ext = (jnp.dot(w_qk.astype(MXU_DTYPE), v_ext, preferred_element_type=F32)
                       + s_inter * jnp.dot(q, c_ext.astype(MXU_DTYPE), preferred_element_type=F32))
            den = num_ext[:, DH:DH + 1]
            hval = num_ext[:, 0:DH] / jnp.maximum(jnp.abs(den), jnp.exp(-m_t))

            b_last = b_col[L - 1:L, :]
            g_log = b_last - b_col + i_col
            m_new = jnp.maximum(b_last + m_prev, jnp.max(g_log, axis=0, keepdims=True))
            w_k = jnp.exp(g_log - m_new)
            decay = jnp.exp(b_last + m_prev - m_new)
            kw_t = (k * w_k).T.astype(MXU_DTYPE)
            c_state[st] = decay * c_ext + jnp.dot(kw_t, v_ext, preferred_element_type=F32)
            m_state[st:st + 1, :] = jnp.broadcast_to(m_new, (1, LANES))

            mu = jnp.mean(hval, axis=-1, keepdims=True)
            cen = hval - mu
            var = jnp.mean(cen * cen, axis=-1, keepdims=True)
            y = o_ref[j, :, sl] * (cen * lax.rsqrt(var + LN_EPS) * nw_ref[:, sl])
            y_ref[j, :, sl] = y.astype(y_ref.dtype)


def _mlstm(q, k, v, o, gates, gates_t, hnorm_row, bsz, seq):
    nb = MLSTM_SEQS
    nc = seq // MLSTM_CHUNK
    assert bsz % nb == 0 and seq % MLSTM_CHUNK == 0, (bsz, seq)
    per_seq = lambda a: a.reshape(bsz, seq, a.shape[-1])
    blk = lambda width: pl.BlockSpec((nb, MLSTM_CHUNK, width), lambda g, c: (g, c, 0))
    gt_specs = [pl.BlockSpec((LANES, MLSTM_CHUNK), lambda g, c, j=j: (0, (g * nb + j) * nc + c))
                for j in range(nb)]
    y = pl.pallas_call(
        _mlstm_kernel,
        out_shape=jax.ShapeDtypeStruct((bsz, seq, MLSTM_W), MXU_DTYPE),
        grid=(bsz // nb, nc),
        in_specs=[blk(MLSTM_W)] * 4 + [blk(LANES)] + gt_specs + [_full((1, MLSTM_W))],
        out_specs=blk(MLSTM_W),
        scratch_shapes=[pltpu.VMEM((nb * MLSTM_HEADS, MLSTM_HEAD_DIM, 2 * MLSTM_HEAD_DIM), F32),
                        pltpu.VMEM((nb * MLSTM_HEADS, LANES), F32)],
        compiler_params=_params("arbitrary", "arbitrary"),
        name="mlstm_scan",
    )(per_seq(q), per_seq(k), per_seq(v), per_seq(o), per_seq(gates), *([gates_t] * nb), hnorm_row)
    return y.reshape(bsz * seq, MLSTM_W)


def _fox_kernel(q_ref, k_ref, vt_ref, gt_ref, y_ref, *scratch):
    acc_refs = scratch[:FOX_HEADS]
    qa_ref = scratch[FOX_HEADS]
    tq = q_ref.shape[0]
    tk = TK_FOX
    dh = FOX_HEAD_DIM
    qi = pl.program_id(1)
    q_t = q_ref[...].T
    bias_rows = jnp.where(lax.broadcasted_iota(jnp.int32, (LANES - dh, tq), 0) < 3, 1.0, 0.0)
    for hd in range(FOX_HEADS):
        qa_ref[hd] = jnp.concatenate([q_t[hd * dh:(hd + 1) * dh, :], bias_rows], axis=0).astype(MXU_DTYPE)
        acc_refs[hd][...] = jnp.zeros(acc_refs[hd].shape, F32)
    cq = gt_ref[G_FOX:G_FOX + FOX_HEADS, :] * LOG2E
    key_pos = lax.broadcasted_iota(jnp.int32, (tk, tq), 0)
    qry_pos = lax.broadcasted_iota(jnp.int32, (tk, tq), 1) + qi * tq
    n_full = qi * (tq // tk)

    def block(j, ms, masked):
        k0 = pl.multiple_of(j * tk, tk)
        out = []
        for hd in range(FOX_HEADS):
            hs = slice(hd * LANES, (hd + 1) * LANES)
            s = jnp.dot(k_ref[pl.ds(k0, tk), hs], qa_ref[hd], preferred_element_type=F32)
            if masked:
                s = jnp.where(key_pos + k0 <= qry_pos, s, -jnp.inf)
            cq_h = cq[hd:hd + 1, :]
            m_new = jnp.maximum(ms[hd], jnp.max(s, axis=0, keepdims=True) + cq_h)
            p = jnp.exp2(s - (m_new - cq_h))
            pv = jnp.dot(vt_ref[hd * FOX_VROWS:(hd + 1) * FOX_VROWS, pl.ds(k0, tk)], p.astype(MXU_DTYPE),
                         preferred_element_type=F32)
            acc_refs[hd][...] = jnp.exp2(ms[hd] - m_new) * acc_refs[hd][...] + pv[0:FOX_ACC_ROWS, :]
            out.append(m_new)
        return tuple(out)

    ms = tuple(jnp.full((1, tq), -jnp.inf, F32) for _ in range(FOX_HEADS))
    per_trip = tq // tk

    def past_blocks(t, ms):
        for d in range(per_trip):
            ms = block(t * per_trip + d, ms, masked=False)
        return ms

    ms = lax.fori_loop(0, qi, past_blocks, ms)
    for d in range(tq // tk):
        ms = block(n_full + d, ms, masked=True)
    outs = []
    for hd in range(FOX_HEADS):
        acc = acc_refs[hd][...]
        outs.append(acc[0:dh, :] / acc[dh:dh + 1, :])
    y_ref[...] = jnp.concatenate(outs, axis=0).T.astype(y_ref.dtype)


def _fox(fq, fk_aug, fvt_aug, gates_t, bsz, seq):
    tok = bsz * seq
    tq = TQ_FOX
    nq = seq // tq
    return pl.pallas_call(
        _fox_kernel,
        out_shape=jax.ShapeDtypeStruct((tok, FOX_W), MXU_DTYPE),
        grid=(bsz, nq),
        in_specs=[
            pl.BlockSpec((tq, FOX_W), lambda b, i: (b * nq + i, 0)),
            pl.BlockSpec((seq, FOX_AUG), lambda b, i: (b, 0)),
            pl.BlockSpec((FOX_HEADS * FOX_VROWS, seq), lambda b, i: (0, b)),
            pl.BlockSpec((LANES, tq), lambda b, i: (0, b * nq + i)),
        ],
        out_specs=pl.BlockSpec((tq, FOX_W), lambda b, i: (b * nq + i, 0)),
        scratch_shapes=[pltpu.VMEM((FOX_ACC_ROWS, tq), F32)] * FOX_HEADS
                       + [pltpu.VMEM((FOX_HEADS, LANES, tq), MXU_DTYPE)],
        compiler_params=_params("arbitrary", "arbitrary"),
        name="fox_attention",
    )(fq, fk_aug, fvt_aug, gates_t)


def _router_kernel(h_ref, w_ref, b_ref, stril_ref, id_ref, gate_ref, cnt_ref, carry):
    tm = h_ref.shape[0]

    @pl.when(pl.program_id(0) == 0)
    def _():
        carry[...] = jnp.zeros(carry.shape, F32)

    h = h_ref[...]
    h_hi = h.astype(MXU_DTYPE)
    h_lo = (h - h_hi.astype(F32)).astype(MXU_DTYPE)
    both = jnp.dot(h_hi, w_ref[...], preferred_element_type=F32)
    logits = (both[:, 0:LANES] + both[:, LANES:2 * LANES]
              + jnp.dot(h_lo, w_ref[:, 0:LANES], preferred_element_type=F32) + b_ref[...])
    lane = lax.broadcasted_iota(jnp.int32, (tm, LANES), 1).astype(F32)
    neg = -jnp.inf
    first = lambda hit: jnp.min(jnp.where(hit, lane, float(LANES)), axis=-1, keepdims=True)
    gl = jnp.where(lane < MOE_GROUPS, logits, neg)
    g_max = jnp.max(gl, axis=-1, keepdims=True)
    g_idx = first(gl == g_max)
    p_group = 1.0 / jnp.sum(jnp.exp(gl - g_max), axis=-1, keepdims=True)
    e_lo = MOE_GROUPS + g_idx * EXPERTS_PER_GROUP
    el = jnp.where((lane >= e_lo) & (lane < e_lo + EXPERTS_PER_GROUP), logits, neg)
    v1 = jnp.max(el, axis=-1, keepdims=True)
    i1 = first(el == v1)
    el2 = jnp.where(lane == i1, neg, el)
    v2 = jnp.max(el2, axis=-1, keepdims=True)
    i2 = first(el2 == v2)
    t = jnp.exp(v2 - v1)
    w1 = 1.0 / (1.0 + t)
    gate_ref[...] = jnp.where(lane == 0.0, p_group * w1, jnp.where(lane == 1.0, p_group * (t * w1), 0.0))

    hit1 = lane == i1
    hit2 = lane == i2
    sent = jnp.where(hit1 | hit2, 1.0, 0.0)
    before = jnp.dot(stril_ref[...], sent.astype(jnp.bfloat16), preferred_element_type=F32) + carry[0:1, :]
    r1 = jnp.sum(jnp.where(hit1, before, 0.0), axis=-1, keepdims=True)
    r2 = jnp.sum(jnp.where(hit2, before, 0.0), axis=-1, keepdims=True)
    total = before[tm - 1:tm, :] + sent[tm - 1:tm, :]
    carry[...] = jnp.broadcast_to(total, carry.shape)
    cnt_ref[...] = jnp.broadcast_to(total, cnt_ref.shape).astype(jnp.int32)
    ids = jnp.where(lane == 0.0, i1 - MOE_GROUPS, jnp.where(lane == 1.0, i2 - MOE_GROUPS,
                    jnp.where(lane == 2.0, r1, jnp.where(lane == 3.0, r2, 0.0))))
    id_ref[...] = ids.T[0:SUBLANES, :].astype(jnp.int32)


def _router(h, w_router, b_router):
    tok = h.shape[0]
    tm = TM_ROUTER
    row = lambda i: (i, 0)
    idx = np.arange(tm)
    stril = jnp.asarray((idx[:, None] > idx[None, :]).astype(np.float32), jnp.bfloat16)
    w_hi = w_router.astype(MXU_DTYPE)
    w_lo = (w_router - w_hi.astype(F32)).astype(MXU_DTYPE)
    return pl.pallas_call(
        _router_kernel,
        out_shape=(jax.ShapeDtypeStruct((SUBLANES, tok), jnp.int32), jax.ShapeDtypeStruct((tok, LANES), F32),
                   jax.ShapeDtypeStruct((SUBLANES, LANES), jnp.int32)),
        grid=(tok // tm,),
        in_specs=[pl.BlockSpec((tm, D_MODEL), row), _full((D_MODEL, 2 * LANES)), _full((1, LANES)),
                  _full((tm, tm))],
        out_specs=(pl.BlockSpec((SUBLANES, tm), lambda i: (0, i)), pl.BlockSpec((tm, LANES), row),
                   _full((SUBLANES, LANES))),
        scratch_shapes=[pltpu.VMEM((SUBLANES, LANES), F32)],
        compiler_params=_params("arbitrary"),
        name="moe_router",
    )(h, jnp.concatenate([w_hi, w_lo], axis=1), b_router, stril)


def _pad_pieces(n):
    return tuple(1 << b for b in reversed(range((n - 1).bit_length())))


def _dispatch_kernel(d0_ref, d1_ref, start_ref, cnt_ref, nused_ref, h_ref, xs_hbm, zeros, sem, pad_sem):
    tm = h_ref.shape[0]
    i = pl.program_id(0)
    base = i * tm

    def issue(r, c):
        t = base + r
        pltpu.make_async_copy(h_ref.at[r], xs_hbm.at[d0_ref[t]], sem).start(priority=0)
        pltpu.make_async_copy(h_ref.at[r], xs_hbm.at[d1_ref[t]], sem).start(priority=1)
        return c

    lax.fori_loop(0, tm, issue, 0, unroll=8)

    def pad_copies(e, fn):
        cnt = cnt_ref[e]
        n_pad = (MOE_ROWS - cnt % MOE_ROWS) % MOE_ROWS
        first = start_ref[e] + cnt
        for piece in _pad_pieces(MOE_ROWS):
            @pl.when((n_pad & piece) != 0)
            def _(piece=piece):
                off = first + (n_pad & ~(2 * piece - 1))
                fn(pltpu.make_async_copy(zeros.at[pl.ds(0, piece)], xs_hbm.at[pl.ds(off, piece)], pad_sem))

    def tail_copies(blk, fn):
        for part in range(MOE_ROWS // zeros.shape[0]):
            off = blk * MOE_ROWS + part * zeros.shape[0]
            fn(pltpu.make_async_copy(zeros, xs_hbm.at[pl.ds(off, zeros.shape[0])], pad_sem))

    @pl.when(i == 0)
    def _():
        zeros[...] = jnp.zeros(zeros.shape, F32)
        n_blocks = xs_hbm.shape[0] // MOE_ROWS

        def start(e, c):
            pad_copies(e, lambda cp: cp.start())
            return c

        def wait(e, c):
            pad_copies(e, lambda cp: cp.wait())
            return c

        def tail_start(blk, c):
            tail_copies(blk, lambda cp: cp.start())
            return c

        def tail_wait(blk, c):
            tail_copies(blk, lambda cp: cp.wait())
            return c

        lax.fori_loop(0, N_EXPERTS, start, 0)
        lax.fori_loop(nused_ref[0], n_blocks, tail_start, 0)
        lax.fori_loop(0, N_EXPERTS, wait, 0)
        lax.fori_loop(nused_ref[0], n_blocks, tail_wait, 0)

    for _ in range(TOP_K):
        pltpu.make_async_copy(h_ref, xs_hbm.at[pl.ds(0, tm)], sem).wait()


def _dispatch(h_tiles, dest0, dest1, pad_start, counts, n_used, n_slots):
    tok = h_tiles.shape[0]
    tm = TM_DISPATCH
    assert tok % tm == 0, (tok, tm)
    grid_spec = pltpu.PrefetchScalarGridSpec(
        num_scalar_prefetch=5,
        grid=(tok // tm,),
        in_specs=[pl.BlockSpec((tm,) + ROW_TILE, lambda i, *_: (i, 0, 0))],
        out_specs=pl.BlockSpec(memory_space=pl.ANY),
        scratch_shapes=[pltpu.VMEM((MOE_ROWS // 2,) + ROW_TILE, F32),
                        pltpu.SemaphoreType.DMA(()), pltpu.SemaphoreType.DMA(())],
    )
    return pl.pallas_call(
        _dispatch_kernel,
        out_shape=jax.ShapeDtypeStruct((n_slots,) + ROW_TILE, F32),
        grid_spec=grid_spec,
        compiler_params=_params("arbitrary"),
        name="moe_dispatch",
    )(dest0, dest1, pad_start, counts, n_used, h_tiles)


def _expert_kernel(be_ref, nused_ref, next_ref, slot_ref, x_ref, wg_hbm, wu_hbm, wd_hbm, out_ref,
                   wg_s, wu_s, wd_s, wg_buf, wu_buf, wd_buf, sem, *, layer):
    i = pl.program_id(0)

    def weight_copies(e, slot):
        w = layer * N_EXPERTS + e
        return (pltpu.make_async_copy(wg_hbm.at[w], wg_buf.at[slot], sem.at[slot]),
                pltpu.make_async_copy(wu_hbm.at[w], wu_buf.at[slot], sem.at[slot]),
                pltpu.make_async_copy(wd_hbm.at[w], wd_buf.at[slot], sem.at[slot]))

    @pl.when(i == 0)
    def _():
        for cp in weight_copies(be_ref[0], slot_ref[be_ref[0]]):
            cp.start()

    @pl.when(i < nused_ref[0])
    def _():
        @pl.when((i == 0) | (be_ref[i] != be_ref[jnp.maximum(i - 1, 0)]))
        def _():
            e = be_ref[i]
            slot = slot_ref[e]
            for cp in weight_copies(e, slot):
                cp.wait()

            @pl.when(next_ref[e] >= 0)
            def _():
                for cp in weight_copies(next_ref[e], 1 - slot):
                    cp.start()

            wg_s[...] = wg_buf[slot].astype(MXU_DTYPE)
            wu_s[...] = wu_buf[slot].astype(MXU_DTYPE)
            wd_s[...] = wd_buf[slot].astype(MXU_DTYPE)

        x = _tiles_to_rows(x_ref).astype(MXU_DTYPE)
        n = x.shape[0]
        hidden = []
        for c in range(0, D_EXPERT, FFN_COLS):
            gate = jnp.dot(x, wg_s[:, c:c + FFN_COLS], preferred_element_type=F32)
            up = jnp.dot(x, wu_s[:, c:c + FFN_COLS], preferred_element_type=F32)
            hidden.append((gate * _sigmoid(gate) * up).astype(MXU_DTYPE))
        hidden = jnp.concatenate(hidden, axis=1)
        for c in range(0, D_MODEL, FFN_COLS):
            out = jnp.dot(hidden, wd_s[:, c:c + FFN_COLS], preferred_element_type=F32)
            for s in range(FFN_COLS // LANES):
                out_ref[pl.ds(c // LANES + s, n, stride=ROW_TILE[0]), :] = out[:, s * LANES:(s + 1) * LANES]

    @pl.when(i >= nused_ref[0])
    def _():
        out_ref[...] = jnp.zeros(out_ref.shape, F32)


def _expert_ffn(xs_tiles, block_expert, n_used, counts, w_gate, w_up, w_down, layer):
    n_slots = xs_tiles.shape[0]
    n_blocks = block_expert.shape[0]
    rows = MOE_ROWS * ROW_TILE[0]
    experts = jnp.arange(N_EXPERTS, dtype=jnp.int32)
    nonempty = counts > 0
    later = (experts[None, :] > experts[:, None]) & nonempty[None, :]
    next_expert = jnp.min(jnp.where(later, experts[None, :], N_EXPERTS), axis=1)
    next_expert = jnp.where(next_expert < N_EXPERTS, next_expert, -1).astype(jnp.int32)
    buf_half = ((jnp.cumsum(nonempty.astype(jnp.int32)) - 1) % 2).astype(jnp.int32)
    blk = lambda i, be, nu, nx, sl: (jnp.maximum(jnp.minimum(i, nu[0] - 1), 0), 0)
    grid_spec = pltpu.PrefetchScalarGridSpec(
        num_scalar_prefetch=4,
        grid=(n_blocks,),
        in_specs=[
            pl.BlockSpec((rows, LANES), blk),
            pl.BlockSpec(memory_space=pl.ANY),
            pl.BlockSpec(memory_space=pl.ANY),
            pl.BlockSpec(memory_space=pl.ANY),
        ],
        out_specs=pl.BlockSpec((rows, LANES), lambda i, be, nu, nx, sl: (i, 0)),
        scratch_shapes=[pltpu.VMEM((D_MODEL, D_EXPERT), MXU_DTYPE), pltpu.VMEM((D_MODEL, D_EXPERT), MXU_DTYPE),
                        pltpu.VMEM((D_EXPERT, D_MODEL), MXU_DTYPE),
                        pltpu.VMEM((2, D_MODEL, D_EXPERT), F32), pltpu.VMEM((2, D_MODEL, D_EXPERT), F32),
                        pltpu.VMEM((2, D_EXPERT, D_MODEL), F32), pltpu.SemaphoreType.DMA((2,))],
    )
    out = pl.pallas_call(
        functools.partial(_expert_kernel, layer=layer),
        out_shape=jax.ShapeDtypeStruct((n_slots * ROW_TILE[0], LANES), F32),
        grid_spec=grid_spec,
        compiler_params=_params("arbitrary"),
        name="moe_experts",
    )(block_expert, n_used, next_expert, buf_half, xs_tiles.reshape(n_slots * ROW_TILE[0], LANES),
      w_gate, w_up, w_down)
    return out.reshape((n_slots,) + ROW_TILE)


def _combine_ln_kernel(d0_ref, d1_ref, rows_hbm, h_ref, gate_ref, g_ref, b_ref, o_ref, buf, sem):
    tm = h_ref.shape[0]
    i = pl.program_id(0)

    def gather(tile, p):
        base = tile * tm

        def issue(r, c):
            t = base + r
            dst = pl.ds(pl.multiple_of(r * ROW_TILE[0], ROW_TILE[0]), ROW_TILE[0])
            pltpu.make_async_copy(rows_hbm.at[d0_ref[t]], buf.at[p, 0, dst],
                                  sem.at[p]).start(priority=0)
            pltpu.make_async_copy(rows_hbm.at[d1_ref[t]], buf.at[p, 1, dst],
                                  sem.at[p]).start(priority=1)
            return c

        lax.fori_loop(0, tm, issue, 0, unroll=8)

    @pl.when(i == 0)
    def _():
        gather(0, 0)

    for p in range(2):
        @pl.when(i % 2 == p)
        def _(p=p):
            @pl.when(i + 1 < pl.num_programs(0))
            def _():
                gather(i + 1, 1 - p)

            for k in range(TOP_K):
                pltpu.make_async_copy(buf.at[1 - p, k], buf.at[p, k], sem.at[p]).wait()
            gate = gate_ref[...]
            acc = (ALPHA * h_ref[...] + gate[:, 0:1] * _tiles_to_rows(buf.at[p, 0])
                   + gate[:, 1:2] * _tiles_to_rows(buf.at[p, 1]))
            o_ref[...] = _layer_norm_rows(acc, g_ref[...], b_ref[...])


def _combine_ln(dest0, dest1, rows_tiles, h, gates, g_row, b_row):
    tok = h.shape[0]
    tm = TM_COMBINE
    row = lambda i, *_: (i, 0)
    const = lambda i, *_: (0, 0)
    grid_spec = pltpu.PrefetchScalarGridSpec(
        num_scalar_prefetch=2,
        grid=(tok // tm,),
        in_specs=[
            pl.BlockSpec(memory_space=pl.ANY),
            pl.BlockSpec((tm, D_MODEL), row),
            pl.BlockSpec((tm, LANES), row),
            pl.BlockSpec((1, D_MODEL), const),
            pl.BlockSpec((1, D_MODEL), const),
        ],
        out_specs=pl.BlockSpec((tm, D_MODEL), row),
        scratch_shapes=[pltpu.VMEM((2, TOP_K, tm * ROW_TILE[0], LANES), F32), pltpu.SemaphoreType.DMA((2,))],
    )
    return pl.pallas_call(
        _combine_ln_kernel,
        out_shape=jax.ShapeDtypeStruct((tok, D_MODEL), F32),
        grid_spec=grid_spec,
        compiler_params=_params("arbitrary"),
        name="moe_combine_ln",
    )(dest0, dest1, rows_tiles, h, gates, g_row, b_row)


def _row_slots_kernel(start_ref, ids_ref, out_ref):
    expert = ids_ref[0:TOP_K, :]
    start = jnp.zeros(expert.shape, jnp.int32)
    for e in range(N_EXPERTS):
        start = jnp.where(expert == e, start_ref[e], start)
    out_ref[...] = jnp.zeros(out_ref.shape, jnp.int32)
    out_ref[0:TOP_K, :] = start + ids_ref[TOP_K:2 * TOP_K, :]


def _row_slots(ids, pad_start):
    grid_spec = pltpu.PrefetchScalarGridSpec(
        num_scalar_prefetch=1,
        grid=(1,),
        in_specs=[pl.BlockSpec(ids.shape, lambda i, s: (0, 0))],
        out_specs=pl.BlockSpec(ids.shape, lambda i, s: (0, 0)),
    )
    return pl.pallas_call(
        _row_slots_kernel,
        out_shape=jax.ShapeDtypeStruct(ids.shape, jnp.int32),
        grid_spec=grid_spec,
        name="moe_row_slots",
    )(pad_start, ids)


def _slot_layout(counts, n_blocks):
    padded = (counts + MOE_ROWS - 1) // MOE_ROWS * MOE_ROWS
    pad_end = jnp.cumsum(padded)
    n_used = pad_end[-1:] // MOE_ROWS
    blocks = jnp.arange(n_blocks, dtype=jnp.int32)
    first_row = jnp.minimum(blocks, n_used - 1) * MOE_ROWS
    block_expert = jnp.sum(first_row[:, None] >= pad_end[None, :], axis=1)
    return (pad_end - padded).astype(jnp.int32), block_expert.astype(jnp.int32), n_used.astype(jnp.int32)


def _moe_ln(h, h_tiles, w_router, b_router, w_gate, w_up, w_down, layer, g_row, b_row):
    tok = h.shape[0]
    n_blocks = tok * TOP_K // MOE_ROWS + N_EXPERTS
    ids, gates, counts = _router(h, w_router, b_router)
    counts = counts[0, MOE_GROUPS:MOE_GROUPS + N_EXPERTS]
    pad_start, block_expert, n_used = _slot_layout(counts, n_blocks)
    slots = _row_slots(ids, pad_start)
    dest0, dest1 = slots[0], slots[1]
    xs = _dispatch(h_tiles.reshape((tok,) + ROW_TILE), dest0, dest1, pad_start, counts, n_used,
                   n_blocks * MOE_ROWS)
    rows = _expert_ffn(xs, block_expert, n_used, counts, w_gate, w_up, w_down, layer)
    return _combine_ln(dest0, dest1, rows, h, gates, g_row, b_row)


def _pad_lanes(v):
    return jnp.pad(v, (0, LANES - v.shape[0])).reshape(1, LANES)


def _even_mixer(h, bsz, seq, w_in, conv_a, conv_w, conv_b, dt_bias, a_log, d_skip, norm_w, w_out, g_row, b_row):
    w = jnp.pad(w_in, ((0, 0), (0, AB_PROJ - w_in.shape[1]))).astype(MXU_DTYPE)
    y_a, z, xbc, dt, acs, acst = _even_front(
        h, w, conv_a, conv_w, conv_b.reshape(1, -1), _pad_lanes(dt_bias), _pad_lanes(-jnp.exp(a_log)), bsz, seq)
    dskip_row = jnp.repeat(d_skip, SSM_HEAD_DIM).reshape(1, -1)
    y_b = _ssd(xbc, dt, acs, acst, z, dskip_row, norm_w.reshape(1, -1), bsz, seq)
    return _outproj_ln([y_a, y_b], h, w_out.astype(MXU_DTYPE), g_row, b_row)


def _odd_mixer(h, bsz, seq, w_in, i_bias, f_bias, hnorm_w, fox_f_bias, w_out, g_row, b_row):
    c = np.cumsum((0, MLSTM_W, MLSTM_W, MLSTM_W, MLSTM_HEADS, MLSTM_HEADS, MLSTM_W, FOX_W, FOX_W, FOX_W, FOX_HEADS))
    part = lambda j: w_in[:, c[j]:c[j + 1]]
    q, k, v, i_pre, f_pre, o_pre, fq, fk, fv, ff = (part(j) for j in range(10))
    gate_cols = jnp.concatenate([i_pre, f_pre, ff, f_pre], axis=1)
    gate_cols = jnp.pad(gate_cols, ((0, 0), (0, LANES - gate_cols.shape[1])))
    spread = lambda m: jnp.pad(m.reshape(-1, FOX_HEADS, FOX_HEAD_DIM),
                               ((0, 0), (0, 0), (0, LANES - FOX_HEAD_DIM))).reshape(-1, FOX_AUG)
    w = jnp.concatenate([q, k, v, o_pre, fq, spread(fk), fv, gate_cols], axis=1).astype(MXU_DTYPE)
    gate_bias = _pad_lanes(jnp.concatenate([i_bias, f_bias, fox_f_bias, f_bias]))
    q, k, v, o, fq, fk_aug, fvt_aug, gates, gates_t = _odd_front(h, w, gate_bias, bsz, seq)
    y_c = _mlstm(q, k, v, o, gates, gates_t, hnorm_w.reshape(1, -1), bsz, seq)
    y_d = _fox(fq, fk_aug, fvt_aug, gates_t, bsz, seq)
    return _outproj_ln([y_c, y_d], h, w_out.astype(MXU_DTYPE), g_row, b_row)


def kernel(x, ab_w_in, ab_conv_a, ab_conv_ssm_w, ab_conv_ssm_b, ab_dt_bias, ab_a_log, ab_d_skip, ab_norm_w, ab_w_out, cd_w_in, cd_i_bias, cd_f_bias, cd_hnorm_w, cd_fox_f_bias, cd_w_out, ln1_g, ln1_b, ln2_g, ln2_b, moe_rg_w, moe_rg_b, moe_re_w, moe_re_b, moe_w_gate, moe_w_up, moe_w_down):
    bsz, seq, d = x.shape
    h = x.reshape(bsz * seq, d)
    stack = lambda w: w.reshape((w.shape[0] * w.shape[1],) + w.shape[2:])
    w_gate, w_up, w_down = stack(moe_w_gate), stack(moe_w_up), stack(moe_w_down)
    for layer in range(DEPTH):
        j = layer // 2
        g1, b1 = ln1_g[layer].reshape(1, -1), ln1_b[layer].reshape(1, -1)
        if layer % 2 == 0:
            h, h_tiles = _even_mixer(h, bsz, seq, ab_w_in[j], ab_conv_a[j], ab_conv_ssm_w[j], ab_conv_ssm_b[j],
                                     ab_dt_bias[j], ab_a_log[j], ab_d_skip[j], ab_norm_w[j], ab_w_out[j], g1, b1)
        else:
            h, h_tiles = _odd_mixer(h, bsz, seq, cd_w_in[j], cd_i_bias[j], cd_f_bias[j], cd_hnorm_w[j],
                                    cd_fox_f_bias[j], cd_w_out[j], g1, b1)
        re_w = jnp.transpose(moe_re_w[layer], (1, 0, 2)).reshape(d, N_EXPERTS)
        w_router = jnp.pad(jnp.concatenate([moe_rg_w[layer], re_w], axis=1),
                           ((0, 0), (0, LANES - MOE_GROUPS - N_EXPERTS)))
        b_router = _pad_lanes(jnp.concatenate([moe_rg_b[layer], moe_re_b[layer].reshape(-1)]))
        h = _moe_ln(h, h_tiles, w_router, b_router, w_gate, w_up, w_down, layer,
                    ln2_g[layer].reshape(1, -1), ln2_b[layer].reshape(1, -1))
    return h.reshape(bsz, seq, d)
```

```python
import functools

import numpy as np
import jax
import jax.numpy as jnp
from jax import lax
from jax.experimental import pallas as pl
from jax.experimental.pallas import tpu as pltpu

F32 = jnp.float32
MXU_DTYPE = jnp.bfloat16

D_MODEL = 1024
DEPTH = 4
ALPHA = (2 * DEPTH) ** 0.25
LN_EPS = 1e-5
CONV_DIM = D_MODEL // 2
CONV_WIDTH = 3
SSM_D_INNER = D_MODEL
SSM_HEAD_DIM = 64
SSM_HEADS = SSM_D_INNER // SSM_HEAD_DIM
SSM_GROUPS = 4
SSM_STATE = 64
SSM_CONV = 4
SSM_BC = SSM_GROUPS * SSM_STATE
SSM_CONV_DIM = SSM_D_INNER + 2 * SSM_BC
MLSTM_HEADS = 4
MLSTM_HEAD_DIM = D_MODEL // 8
MLSTM_W = MLSTM_HEADS * MLSTM_HEAD_DIM
FOX_HEADS = 8
FOX_HEAD_DIM = D_MODEL // 16
FOX_W = FOX_HEADS * FOX_HEAD_DIM
MOE_GROUPS = 4
EXPERTS_PER_GROUP = 8
N_EXPERTS = MOE_GROUPS * EXPERTS_PER_GROUP
TOP_K = 2
D_EXPERT = D_MODEL // 2

LANES = 128
SUBLANES = 8
VMEM_LIMIT_BYTES = 56 * 1024 * 1024

CHUNK = 128
SEQ_PAIR = 4
MLSTM_SEQS = 1
MLSTM_CHUNK = 512
TM_FRONT = 512
TM_EVEN_FRONT = 256
CONV_ROWS, CONV_LANES = 64, 512
TM_OUT = 1024
TM_ROUTER = 512
TQ_FOX = 256
MOE_ROWS = 512
FFN_COLS = 256
TM_COMBINE = 512
TM_DISPATCH = 2048

AB_PROJ = 4224
FOX_AUG = FOX_HEADS * LANES
CD_FK = 5 * 512
CD_FV = CD_FK + FOX_AUG
CD_GATES = CD_FV + FOX_W
CD_PROJ = CD_GATES + LANES
LOG2E = 1.4426950408889634
TK_FOX = 128
FOX_ACC_ROWS = FOX_HEAD_DIM + SUBLANES
FOX_VROWS = LANES
G_I, G_F, G_FOX, G_BCUM = 0, 4, 8, 16


def _params(*sem):
    return pltpu.CompilerParams(dimension_semantics=sem, vmem_limit_bytes=VMEM_LIMIT_BYTES)


def _softplus(x):
    return jnp.maximum(x, 0.0) + jnp.log(1.0 + jnp.exp(-jnp.abs(x)))


def _sigmoid(x):
    return 1.0 / (1.0 + jnp.exp(-x))


def _layer_norm_rows(v, g, b):
    mu = jnp.mean(v, axis=-1, keepdims=True)
    c = v - mu
    var = jnp.mean(c * c, axis=-1, keepdims=True)
    return c * lax.rsqrt(var + LN_EPS) * g + b


def _tril(n, block):
    i = np.arange(n)
    m = (i[:, None] >= i[None, :]) & (i[:, None] // block == i[None, :] // block)
    return jnp.asarray(m.astype(np.float32), MXU_DTYPE)


def _full(shape):
    return pl.BlockSpec(shape, lambda *_: (0,) * len(shape), pipeline_mode=pl.Buffered(1))


ROW_TILE = (D_MODEL // LANES, LANES)


def _tiles_to_rows(ref):
    n = ref.shape[0] // ROW_TILE[0]
    return jnp.concatenate([ref[pl.ds(s, n, stride=ROW_TILE[0]), :] for s in range(ROW_TILE[0])], axis=1)


def _rows_to_tiles(ref, val):
    n = val.shape[0]
    for s in range(ROW_TILE[0]):
        ref[pl.ds(s, n, stride=ROW_TILE[0]), :] = val[:, s * LANES:(s + 1) * LANES]


def _split3(x):
    narrow = lambda v: v.astype(MXU_DTYPE).astype(F32)
    x1 = narrow(x)
    x2 = narrow(x - x1)
    return x1, x2, narrow(x - x1 - x2)


def _cumsum_rows(tril, x):
    parts = jnp.dot(tril, jnp.concatenate(_split3(x), axis=1).astype(MXU_DTYPE), preferred_element_type=F32)
    return parts[:, 0:LANES] + parts[:, LANES:2 * LANES] + parts[:, 2 * LANES:3 * LANES]


def _even_front_kernel(h_ref, w_ref, ca_ref, cw_ref, cb_ref, dtb_ref, aneg_ref, tril_ref,
                       ya_ref, z_ref, xbc_ref, dt_ref, acs_ref, acst_ref,
                       proj_buf, ua_ext, xbc_ext, *, tiles_per_seq):
    tm = h_ref.shape[0]
    i = pl.program_id(0)
    cur = i % 2

    @pl.when(i == 0)
    def _():
        proj_buf[1] = jnp.zeros(proj_buf.shape[1:], F32)

    @pl.when((i == 0) | ((i - 1) % tiles_per_seq == 0))
    def _():
        ua_ext[0:SUBLANES, :] = jnp.zeros((SUBLANES, CONV_DIM), F32)
        xbc_ext[0:SUBLANES, :] = jnp.zeros((SUBLANES, SSM_CONV_DIM), F32)

    z0 = 3 * CONV_DIM
    x0 = z0 + SSM_D_INNER
    d0 = x0 + SSM_CONV_DIM

    def step(p):
        x_in = h_ref[...].astype(MXU_DTYPE)
        proj = proj_buf.at[1 - p]
        z_ref[...] = proj[:, z0:x0]

        n_blocks = tm // CONV_ROWS
        col_cuts = [AB_PROJ * g // n_blocks // LANES * LANES for g in range(n_blocks)] + [AB_PROJ]
        for r0 in range(0, tm, CONV_ROWS):
            c_lo, c_hi = col_cuts[r0 // CONV_ROWS], col_cuts[r0 // CONV_ROWS + 1]
            proj_buf[p, :, c_lo:c_hi] = jnp.dot(x_in, w_ref[:, c_lo:c_hi], preferred_element_type=F32)
            rows = slice(r0, r0 + CONV_ROWS)
            ext_rows = slice(SUBLANES + r0, SUBLANES + r0 + CONV_ROWS)
            ua_ext[ext_rows, :] = proj[rows, CONV_DIM:2 * CONV_DIM] * proj[rows, 2 * CONV_DIM:3 * CONV_DIM]
            conv = None
            for k in range(CONV_WIDTH):
                tap = ca_ref[k:k + 1, :] * ua_ext[pl.ds(SUBLANES + r0 - (CONV_WIDTH - 1) + k, CONV_ROWS), :]
                conv = tap if conv is None else conv + tap
            ya_ref[rows, :] = (proj[rows, 0:CONV_DIM] * conv).astype(ya_ref.dtype)
            for c0 in range(0, SSM_CONV_DIM, CONV_LANES):
                cols = slice(c0, c0 + CONV_LANES)
                xbc_ext[ext_rows, cols] = proj[rows, x0 + c0:x0 + c0 + CONV_LANES]
                conv = cb_ref[:, cols]
                for k in range(SSM_CONV):
                    conv = conv + (cw_ref[k:k + 1, cols]
                                   * xbc_ext[pl.ds(SUBLANES + r0 - (SSM_CONV - 1) + k, CONV_ROWS), cols])
                xbc_ref[rows, cols] = conv * _sigmoid(conv)
        ua_ext[0:SUBLANES, :] = ua_ext[tm:tm + SUBLANES, :]
        xbc_ext[0:SUBLANES, :] = xbc_ext[tm:tm + SUBLANES, :]

        dt = _softplus(proj[:, d0:d0 + LANES] + dtb_ref[...])
        a = dt * aneg_ref[...]
        acs = _cumsum_rows(tril_ref[...], a)
        dt_ref[...] = dt
        acs_ref[...] = acs
        acst_ref[...] = acs.T

    for parity in range(2):
        @pl.when(cur == parity)
        def _(parity=parity):
            step(parity)


def _even_front(h, w_in, conv_a, conv_w, conv_b, dt_bias_row, aneg_row, bsz, seq):
    tok = bsz * seq
    tm = TM_EVEN_FRONT
    n_tiles = tok // tm
    row = lambda i: (jnp.maximum(i - 1, 0), 0)
    out_shapes = (
        jax.ShapeDtypeStruct((tok, CONV_DIM), MXU_DTYPE),
        jax.ShapeDtypeStruct((tok, SSM_D_INNER), F32),
        jax.ShapeDtypeStruct((tok, SSM_CONV_DIM), F32),
        jax.ShapeDtypeStruct((tok, LANES), F32),
        jax.ShapeDtypeStruct((tok, LANES), F32),
        jax.ShapeDtypeStruct((LANES, tok), F32),
    )
    return pl.pallas_call(
        functools.partial(_even_front_kernel, tiles_per_seq=seq // tm),
        out_shape=out_shapes,
        grid=(n_tiles + 1,),
        in_specs=[
            pl.BlockSpec((tm, D_MODEL), lambda i: (jnp.minimum(i, n_tiles - 1), 0)),
            _full((D_MODEL, AB_PROJ)),
            _full((CONV_WIDTH, CONV_DIM)),
            _full((SSM_CONV, SSM_CONV_DIM)),
            _full((1, SSM_CONV_DIM)),
            _full((1, LANES)),
            _full((1, LANES)),
            _full((tm, tm)),
        ],
        out_specs=(
            pl.BlockSpec((tm, CONV_DIM), row),
            pl.BlockSpec((tm, SSM_D_INNER), row),
            pl.BlockSpec((tm, SSM_CONV_DIM), row),
            pl.BlockSpec((tm, LANES), row),
            pl.BlockSpec((tm, LANES), row),
            pl.BlockSpec((LANES, tm), lambda i: (0, jnp.maximum(i - 1, 0))),
        ),
        scratch_shapes=[
            pltpu.VMEM((2, tm, AB_PROJ), F32),
            pltpu.VMEM((tm + SUBLANES, CONV_DIM), F32),
            pltpu.VMEM((tm + SUBLANES, SSM_CONV_DIM), F32),
        ],
        compiler_params=_params("arbitrary"),
        name="even_front",
    )(h, w_in, conv_a, conv_w, conv_b, dt_bias_row, aneg_row, _tril(tm, CHUNK))


def _bcast_heads(arr, n_heads, width):
    per = LANES // width
    length = arr.shape[0]
    lane = lax.broadcasted_iota(jnp.int32, (length, LANES), 1)
    outs = []
    for j in range(n_heads // per):
        v = jnp.broadcast_to(arr[:, j * per:j * per + 1], (length, LANES))
        for r in range(1, per):
            v = jnp.where(lane >= r * width, jnp.broadcast_to(arr[:, j * per + r:j * per + r + 1], (length, LANES)), v)
        outs.append(v)
    return jnp.concatenate(outs, axis=1)


def _expand_heads(arr, expand_ref):
    hi, lo, _ = _split3(arr)
    return jnp.dot(jnp.concatenate([hi, lo], axis=1).astype(MXU_DTYPE), expand_ref[...],
                   preferred_element_type=F32)


def _ssd_kernel(*refs):
    nb = SEQ_PAIR
    xbc_ref, dt_ref, acs_ref, z_ref = refs[:4]
    acst_refs = refs[4:4 + nb]
    dskip_ref, nw_ref, expand_ref, y_ref, state = refs[4 + nb:]
    L = CHUNK
    P = SSM_HEAD_DIM
    R = SSM_HEADS // SSM_GROUPS
    GW = R * P

    @pl.when(pl.program_id(1) == 0)
    def _():
        state[...] = jnp.zeros(state.shape, F32)

    row = lax.broadcasted_iota(jnp.int32, (L, L), 0)
    col = lax.broadcasted_iota(jnp.int32, (L, L), 1)
    causal = row >= col
    lane_g = lax.broadcasted_iota(jnp.int32, (L, GW), 1)

    per_row = []
    for j in range(nb):
        acs = acs_ref[j]
        per_row += [dt_ref[j], jnp.exp(acs[L - 1:L, :] - acs), jnp.exp(acs)]
    expanded = _expand_heads(jnp.concatenate(per_row, axis=0), expand_ref)

    st_old = [state[c] for c in range(nb * SSM_GROUPS)]
    st_new = []
    for j in range(nb):
        xs = xbc_ref[j, :, 0:SSM_D_INNER]
        bm = xbc_ref[j, :, SSM_D_INNER:SSM_D_INNER + SSM_BC]
        cm = xbc_ref[j, :, SSM_D_INNER + SSM_BC:SSM_CONV_DIM]
        acs = acs_ref[j]
        acst = acst_refs[j][...]
        a_last = acs[L - 1:L, :]
        dtx = expanded[(3 * j) * L:(3 * j + 1) * L, :]
        decx = expanded[(3 * j + 1) * L:(3 * j + 2) * L, :]
        expx = expanded[(3 * j + 2) * L:(3 * j + 3) * L, :]
        xdt = xs * dtx
        xdec = (xdt * decx).astype(MXU_DTYPE)
        xdt_m = xdt.astype(MXU_DTYPE)
        chunk_decay = jnp.exp(jnp.broadcast_to(a_last, (SUBLANES, LANES)))
        cdx = _bcast_heads(chunk_decay, SSM_HEADS, P)[0:1, :]

        bm_t = bm.T.astype(MXU_DTYPE)
        cm_m = cm.astype(MXU_DTYPE)
        bm_m = bm.astype(MXU_DTYPE)

        ys = []
        for g in range(SSM_GROUPS):
            n0 = g * SSM_STATE
            c_g = cm_m[:, n0:n0 + SSM_STATE]
            cb = lax.dot_general(c_g, bm_m[:, n0:n0 + SSM_STATE], (((1,), (1,)), ((), ())),
                                 preferred_element_type=F32)
            ms = []
            for r in range(R):
                hd = g * R + r
                seg = jnp.exp(jnp.where(causal, acs[:, hd:hd + 1] - acst[hd:hd + 1, :], -jnp.inf))
                ms.append((cb * seg).astype(MXU_DTYPE))
            big = jnp.dot(jnp.concatenate(ms, axis=0), xdt_m[:, g * GW:(g + 1) * GW],
                          preferred_element_type=F32)
            y_diag = big[0:L, :]
            for r in range(1, R):
                y_diag = jnp.where(lane_g >= r * P, big[r * L:(r + 1) * L, :], y_diag)
            st = st_old[j * SSM_GROUPS + g]
            y_off = jnp.dot(c_g, st.astype(MXU_DTYPE), preferred_element_type=F32)
            new = jnp.dot(bm_t[n0:n0 + SSM_STATE, :], xdec[:, g * GW:(g + 1) * GW],
                          preferred_element_type=F32)
            st_new.append(st * cdx[:, g * GW:(g + 1) * GW] + new)
            ys.append(y_diag + y_off * expx[:, g * GW:(g + 1) * GW])
        y = jnp.concatenate(ys, axis=1) + dskip_ref[...] * xs
        z = z_ref[j]
        u = y * (z * _sigmoid(z))
        y = u * lax.rsqrt(jnp.mean(u * u, axis=-1, keepdims=True) + LN_EPS) * nw_ref[...]
        y_ref[j] = y.astype(y_ref.dtype)

    for c in range(nb * SSM_GROUPS):
        state[c] = st_new[c]


def _ssd(xbc, dt, acs, acst, z, dskip_row, normw_row, bsz, seq):
    nb = SEQ_PAIR
    nc = seq // CHUNK
    assert bsz % nb == 0 and seq % CHUNK == 0, (bsz, seq)
    per_seq = lambda a: a.reshape(bsz, seq, a.shape[-1])
    blk = lambda width: pl.BlockSpec((nb, CHUNK, width), lambda g, c: (g, c, 0))
    acst_specs = [pl.BlockSpec((LANES, CHUNK), lambda g, c, j=j: (0, (g * nb + j) * nc + c)) for j in range(nb)]
    expand = np.zeros((LANES, SSM_D_INNER), np.float32)
    for hd in range(SSM_HEADS):
        expand[hd, hd * SSM_HEAD_DIM:(hd + 1) * SSM_HEAD_DIM] = 1.0
    expand = jnp.asarray(np.concatenate([expand] * 2, axis=0), MXU_DTYPE)
    y = pl.pallas_call(
        _ssd_kernel,
        out_shape=jax.ShapeDtypeStruct((bsz, seq, SSM_D_INNER), MXU_DTYPE),
        grid=(bsz // nb, nc),
        in_specs=[blk(SSM_CONV_DIM), blk(LANES), blk(LANES), blk(SSM_D_INNER)] + acst_specs + [
            _full((1, SSM_D_INNER)),
            _full((1, SSM_D_INNER)),
            _full((2 * LANES, SSM_D_INNER)),
        ],
        out_specs=blk(SSM_D_INNER),
        scratch_shapes=[pltpu.VMEM((nb * SSM_GROUPS, SSM_STATE, SSM_D_INNER // SSM_GROUPS), F32)],
        compiler_params=_params("arbitrary", "arbitrary"),
        name="ssd_scan",
    )(per_seq(xbc), per_seq(dt), per_seq(acs), per_seq(z), *([acst] * nb), dskip_row, normw_row, expand)
    return y.reshape(bsz * seq, SSM_D_INNER)


def _outproj_ln_kernel(*refs, widths):
    n = len(widths)
    parts = refs[:n]
    h_ref, w_ref, g_ref, b_ref, o_ref, ot_ref = refs[n:]
    acc = ALPHA * h_ref[...]
    off = 0
    for p, wd in zip(parts, widths):
        acc = acc + jnp.dot(p[...].astype(MXU_DTYPE), w_ref[off:off + wd, :], preferred_element_type=F32)
        off += wd
    out = _layer_norm_rows(acc, g_ref[...], b_ref[...])
    o_ref[...] = out
    _rows_to_tiles(ot_ref, out)


def _outproj_ln(parts, h, w_out, g_row, b_row):
    tok = h.shape[0]
    tm = TM_OUT
    widths = tuple(p.shape[1] for p in parts)
    row = lambda i: (i, 0)
    return pl.pallas_call(
        functools.partial(_outproj_ln_kernel, widths=widths),
        out_shape=(jax.ShapeDtypeStruct((tok, D_MODEL), F32),
                   jax.ShapeDtypeStruct((tok * ROW_TILE[0], LANES), F32)),
        grid=(tok // tm,),
        in_specs=[pl.BlockSpec((tm, wd), row) for wd in widths] + [
            pl.BlockSpec((tm, D_MODEL), row),
            _full((sum(widths), D_MODEL)),
            _full((1, D_MODEL)),
            _full((1, D_MODEL)),
        ],
        out_specs=(pl.BlockSpec((tm, D_MODEL), row), pl.BlockSpec((tm * ROW_TILE[0], LANES), row)),
        compiler_params=_params("arbitrary"),
        name="outproj_ln",
    )(*parts, h, w_out, g_row, b_row)


def _odd_front_kernel(h_ref, w_ref, gb_ref, tril_ref,
                      q_ref, k_ref, v_ref, o_ref, fq_ref, fk_ref, fvt_ref, g_ref, gt_ref, carry):
    tm = h_ref.shape[0]
    x_in = h_ref[...].astype(MXU_DTYPE)
    proj_cols = lambda lo, hi: jnp.dot(x_in, w_ref[:, lo:hi], preferred_element_type=F32)

    @pl.when(pl.program_id(1) == 0)
    def _():
        carry[...] = jnp.zeros(carry.shape, F32)

    raw = proj_cols(CD_GATES, CD_GATES + LANES) + gb_ref[...]
    lane = lax.broadcasted_iota(jnp.int32, (tm, LANES), 1)
    g = jnp.where(lane < G_F, raw, -_softplus(-raw))
    prev = carry[0:1, :]
    glob = _cumsum_rows(tril_ref[...], g) + prev
    before = []
    for c in range(tm // MLSTM_CHUNK):
        before.append(jnp.broadcast_to(prev, (MLSTM_CHUNK, LANES)))
        prev = glob[(c + 1) * MLSTM_CHUNK - 1:(c + 1) * MLSTM_CHUNK, :]
    carry[...] = jnp.broadcast_to(prev, carry.shape)
    local = glob - jnp.concatenate(before, axis=0)
    out = jnp.where((lane >= G_FOX) & (lane < G_BCUM), glob, jnp.where(lane >= G_BCUM, local, g))
    g_ref[...] = out
    gt_ref[...] = out.T

    q_ref[...] = proj_cols(0, 512)
    k_ref[...] = proj_cols(512, 1024) * (MLSTM_HEAD_DIM ** -0.5)
    v_ref[...] = proj_cols(1024, 1536)
    o_ref[...] = _sigmoid(proj_cols(1536, 2048))
    fq_ref[...] = proj_cols(2048, 2560) * (FOX_HEAD_DIM ** -0.5 * LOG2E)

    is_bias = (lane >= FOX_HEAD_DIM) & (lane < FOX_HEAD_DIM + 3)
    fk = proj_cols(CD_FK, CD_FK + FOX_AUG)
    for hd in range(FOX_HEADS):
        c1, c2, c3 = _split3(out[:, G_FOX + hd:G_FOX + hd + 1] * (-LOG2E))
        bias = jnp.where(lane == FOX_HEAD_DIM, c1, jnp.where(lane == FOX_HEAD_DIM + 1, c2, c3))
        k_h = fk[:, hd * LANES:(hd + 1) * LANES]
        fk_ref[:, hd * LANES:(hd + 1) * LANES] = jnp.where(is_bias, bias, k_h).astype(MXU_DTYPE)
    v_t = proj_cols(CD_FV, CD_FV + FOX_W).T
    extra = jnp.where(lax.broadcasted_iota(jnp.int32, (FOX_VROWS - FOX_HEAD_DIM, tm), 0) == 0, 1.0, 0.0)
    fvt_ref[...] = jnp.concatenate(
        [blk for hd in range(FOX_HEADS) for blk in (v_t[hd * FOX_HEAD_DIM:(hd + 1) * FOX_HEAD_DIM, :], extra)],
        axis=0).astype(MXU_DTYPE)


def _odd_front(h, w_in, gate_bias_row, bsz, seq):
    tok = bsz * seq
    tm = TM_FRONT
    ns = seq // tm
    row = lambda b, s: (b * ns + s, 0)
    col = lambda b, s: (0, b * ns + s)
    wide = jax.ShapeDtypeStruct((tok, 512), F32)
    return pl.pallas_call(
        _odd_front_kernel,
        out_shape=(wide,) * 5 + (jax.ShapeDtypeStruct((tok, FOX_AUG), MXU_DTYPE),
                                 jax.ShapeDtypeStruct((FOX_HEADS * FOX_VROWS, tok), MXU_DTYPE),
                                 jax.ShapeDtypeStruct((tok, LANES), F32),
                                 jax.ShapeDtypeStruct((LANES, tok), F32)),
        grid=(bsz, ns),
        in_specs=[
            pl.BlockSpec((tm, D_MODEL), row),
            _full((D_MODEL, CD_PROJ)),
            _full((1, LANES)),
            _full((tm, tm)),
        ],
        out_specs=(pl.BlockSpec((tm, 512), row),) * 5 + (
            pl.BlockSpec((tm, FOX_AUG), row),
            pl.BlockSpec((FOX_HEADS * FOX_VROWS, tm), col),
            pl.BlockSpec((tm, LANES), row),
            pl.BlockSpec((LANES, tm), col),
        ),
        scratch_shapes=[pltpu.VMEM((SUBLANES, LANES), F32)],
        compiler_params=_params("arbitrary", "arbitrary"),
        name="odd_front",
    )(h, w_in, gate_bias_row, _tril(tm, tm))


def _mlstm_kernel(*refs):
    nb = MLSTM_SEQS
    q_ref, k_ref, v_ref, o_ref, g_ref = refs[:5]
    gt_refs = refs[5:5 + nb]
    nw_ref, y_ref, c_state, m_state = refs[5 + nb:]
    L = MLSTM_CHUNK
    DH = MLSTM_HEAD_DIM

    @pl.when(pl.program_id(1) == 0)
    def _():
        c_state[...] = jnp.zeros(c_state.shape, F32)
        m_state[...] = jnp.zeros(m_state.shape, F32)

    row = lax.broadcasted_iota(jnp.int32, (L, L), 0)
    col = lax.broadcasted_iota(jnp.int32, (L, L), 1)
    causal = row >= col
    ones_col = jnp.where(lax.broadcasted_iota(jnp.int32, (L, DH), 1) == 0, 1.0, 0.0)

    for j in range(nb):
        gates = g_ref[j]
        gates_t = gt_refs[j][...]
        for hd in range(MLSTM_HEADS):
            st = j * MLSTM_HEADS + hd
            sl = slice(hd * DH, (hd + 1) * DH)
            q = q_ref[j, :, sl].astype(MXU_DTYPE)
            k = k_ref[j, :, sl]
            v_ext = jnp.concatenate([v_ref[j, :, sl], ones_col], axis=1).astype(MXU_DTYPE)
            b_col = gates[:, G_BCUM + hd:G_BCUM + hd + 1]
            i_col = gates[:, G_I + hd:G_I + hd + 1]
            b_row = gates_t[G_BCUM + hd:G_BCUM + hd + 1, :]
            i_row = gates_t[G_I + hd:G_I + hd + 1, :]
            m_prev = m_state[st:st + 1, 0:1]
            c_ext = c_state[st]

            d_mat = jnp.where(causal, b_col - b_row + i_row, -jnp.inf)
            inter = b_col + m_prev
            m_t = jnp.maximum(jnp.max(d_mat, axis=-1, keepdims=True), inter)
            s_qk = lax.dot_general(q, k.astype(MXU_DTYPE), (((1,), (1,)), ((), ())), preferred_element_type=F32)
            w_qk = s_qk * jnp.exp(d_mat - m_t)
            s_inter = jnp.exp(inter - m_t)
            num_ext = (jnp.dot(w_qk.astype(MXU_DTYPE), v_ext, preferred_element_type=F32)
                       + s_inter * jnp.dot(q, c_ext.astype(MXU_DTYPE), preferred_element_type=F32))
            den = num_ext[:, DH:DH + 1]
            hval = num_ext[:, 0:DH] / jnp.maximum(jnp.abs(den), jnp.exp(-m_t))

            b_last = b_col[L - 1:L, :]
            g_log = b_last - b_col + i_col
            m_new = jnp.maximum(b_last + m_prev, jnp.max(g_log, axis=0, keepdims=True))
            w_k = jnp.exp(g_log - m_new)
            decay = jnp.exp(b_last + m_prev - m_new)
            kw_t = (k * w_k).T.astype(MXU_DTYPE)
            c_state[st] = decay * c_ext + jnp.dot(kw_t, v_ext, preferred_element_type=F32)
            m_state[st:st + 1, :] = jnp.broadcast_to(m_new, (1, LANES))

            mu = jnp.mean(hval, axis=-1, keepdims=True)
            cen = hval - mu
            var = jnp.mean(cen * cen, axis=-1, keepdims=True)
            y = o_ref[j, :, sl] * (cen * lax.rsqrt(var + LN_EPS) * nw_ref[:, sl])
            y_ref[j, :, sl] = y.astype(y_ref.dtype)


def _mlstm(q, k, v, o, gates, gates_t, hnorm_row, bsz, seq):
    nb = MLSTM_SEQS
    nc = seq // MLSTM_CHUNK
    assert bsz % nb == 0 and seq % MLSTM_CHUNK == 0, (bsz, seq)
    per_seq = lambda a: a.reshape(bsz, seq, a.shape[-1])
    blk = lambda width: pl.BlockSpec((nb, MLSTM_CHUNK, width), lambda g, c: (g, c, 0))
    gt_specs = [pl.BlockSpec((LANES, MLSTM_CHUNK), lambda g, c, j=j: (0, (g * nb + j) * nc + c))
                for j in range(nb)]
    y = pl.pallas_call(
        _mlstm_kernel,
        out_shape=jax.ShapeDtypeStruct((bsz, seq, MLSTM_W), MXU_DTYPE),
        grid=(bsz // nb, nc),
        in_specs=[blk(MLSTM_W)] * 4 + [blk(LANES)] + gt_specs + [_full((1, MLSTM_W))],
        out_specs=blk(MLSTM_W),
        scratch_shapes=[pltpu.VMEM((nb * MLSTM_HEADS, MLSTM_HEAD_DIM, 2 * MLSTM_HEAD_DIM), F32),
                        pltpu.VMEM((nb * MLSTM_HEADS, LANES), F32)],
        compiler_params=_params("arbitrary", "arbitrary"),
        name="mlstm_scan",
    )(per_seq(q), per_seq(k), per_seq(v), per_seq(o), per_seq(gates), *([gates_t] * nb), hnorm_row)
    return y.reshape(bsz * seq, MLSTM_W)


def _fox_kernel(q_ref, k_ref, vt_ref, gt_ref, y_ref, *scratch):
    acc_refs = scratch[:FOX_HEADS]
    qa_ref = scratch[FOX_HEADS]
    tq = q_ref.shape[0]
    tk = TK_FOX
    dh = FOX_HEAD_DIM
    qi = pl.program_id(1)
    q_t = q_ref[...].T
    bias_rows = jnp.where(lax.broadcasted_iota(jnp.int32, (LANES - dh, tq), 0) < 3, 1.0, 0.0)
    for hd in range(FOX_HEADS):
        qa_ref[hd] = jnp.concatenate([q_t[hd * dh:(hd + 1) * dh, :], bias_rows], axis=0).astype(MXU_DTYPE)
        acc_refs[hd][...] = jnp.zeros(acc_refs[hd].shape, F32)
    cq = gt_ref[G_FOX:G_FOX + FOX_HEADS, :] * LOG2E
    key_pos = lax.broadcasted_iota(jnp.int32, (tk, tq), 0)
    qry_pos = lax.broadcasted_iota(jnp.int32, (tk, tq), 1) + qi * tq
    n_full = qi * (tq // tk)

    def block(j, ms, masked):
        k0 = pl.multiple_of(j * tk, tk)
        out = []
        for hd in range(FOX_HEADS):
            hs = slice(hd * LANES, (hd + 1) * LANES)
            s = jnp.dot(k_ref[pl.ds(k0, tk), hs], qa_ref[hd], preferred_element_type=F32)
            if masked:
                s = jnp.where(key_pos + k0 <= qry_pos, s, -jnp.inf)
            cq_h = cq[hd:hd + 1, :]
            m_new = jnp.maximum(ms[hd], jnp.max(s, axis=0, keepdims=True) + cq_h)
            p = jnp.exp2(s - (m_new - cq_h))
            pv = jnp.dot(vt_ref[hd * FOX_VROWS:(hd + 1) * FOX_VROWS, pl.ds(k0, tk)], p.astype(MXU_DTYPE),
                         preferred_element_type=F32)
            acc_refs[hd][...] = jnp.exp2(ms[hd] - m_new) * acc_refs[hd][...] + pv[0:FOX_ACC_ROWS, :]
            out.append(m_new)
        return tuple(out)

    ms = tuple(jnp.full((1, tq), -jnp.inf, F32) for _ in range(FOX_HEADS))
    per_trip = tq // tk

    def past_blocks(t, ms):
        for d in range(per_trip):
            ms = block(t * per_trip + d, ms, masked=False)
        return ms

    ms = lax.fori_loop(0, qi, past_blocks, ms)
    for d in range(tq // tk):
        ms = block(n_full + d, ms, masked=True)
    outs = []
    for hd in range(FOX_HEADS):
        acc = acc_refs[hd][...]
        outs.append(acc[0:dh, :] / acc[dh:dh + 1, :])
    y_ref[...] = jnp.concatenate(outs, axis=0).T.astype(y_ref.dtype)


def _fox(fq, fk_aug, fvt_aug, gates_t, bsz, seq):
    tok = bsz * seq
    tq = TQ_FOX
    nq = seq // tq
    return pl.pallas_call(
        _fox_kernel,
        out_shape=jax.ShapeDtypeStruct((tok, FOX_W), MXU_DTYPE),
        grid=(bsz, nq),
        in_specs=[
            pl.BlockSpec((tq, FOX_W), lambda b, i: (b * nq + i, 0)),
            pl.BlockSpec((seq, FOX_AUG), lambda b, i: (b, 0)),
            pl.BlockSpec((FOX_HEADS * FOX_VROWS, seq), lambda b, i: (0, b)),
            pl.BlockSpec((LANES, tq), lambda b, i: (0, b * nq + i)),
        ],
        out_specs=pl.BlockSpec((tq, FOX_W), lambda b, i: (b * nq + i, 0)),
        scratch_shapes=[pltpu.VMEM((FOX_ACC_ROWS, tq), F32)] * FOX_HEADS
                       + [pltpu.VMEM((FOX_HEADS, LANES, tq), MXU_DTYPE)],
        compiler_params=_params("arbitrary", "arbitrary"),
        name="fox_attention",
    )(fq, fk_aug, fvt_aug, gates_t)


def _router_kernel(h_ref, w_ref, b_ref, stril_ref, id_ref, gate_ref, cnt_ref, carry):
    tm = h_ref.shape[0]

    @pl.when(pl.program_id(0) == 0)
    def _():
        carry[...] = jnp.zeros(carry.shape, F32)

    h = h_ref[...]
    h_hi = h.astype(MXU_DTYPE)
    h_lo = (h - h_hi.astype(F32)).astype(MXU_DTYPE)
    both = jnp.dot(h_hi, w_ref[...], preferred_element_type=F32)
    logits = (both[:, 0:LANES] + both[:, LANES:2 * LANES]
              + jnp.dot(h_lo, w_ref[:, 0:LANES], preferred_element_type=F32) + b_ref[...])
    lane = lax.broadcasted_iota(jnp.int32, (tm, LANES), 1).astype(F32)
    neg = -jnp.inf
    first = lambda hit: jnp.min(jnp.where(hit, lane, float(LANES)), axis=-1, keepdims=True)
    gl = jnp.where(lane < MOE_GROUPS, logits, neg)
    g_max = jnp.max(gl, axis=-1, keepdims=True)
    g_idx = first(gl == g_max)
    p_group = 1.0 / jnp.sum(jnp.exp(gl - g_max), axis=-1, keepdims=True)
    e_lo = MOE_GROUPS + g_idx * EXPERTS_PER_GROUP
    el = jnp.where((lane >= e_lo) & (lane < e_lo + EXPERTS_PER_GROUP), logits, neg)
    v1 = jnp.max(el, axis=-1, keepdims=True)
    i1 = first(el == v1)
    el2 = jnp.where(lane == i1, neg, el)
    v2 = jnp.max(el2, axis=-1, keepdims=True)
    i2 = first(el2 == v2)
    t = jnp.exp(v2 - v1)
    w1 = 1.0 / (1.0 + t)
    gate_ref[...] = jnp.where(lane == 0.0, p_group * w1, jnp.where(lane == 1.0, p_group * (t * w1), 0.0))

    hit1 = lane == i1
    hit2 = lane == i2
    sent = jnp.where(hit1 | hit2, 1.0, 0.0)
    before = jnp.dot(stril_ref[...], sent.astype(jnp.bfloat16), preferred_element_type=F32) + carry[0:1, :]
    r1 = jnp.sum(jnp.where(hit1, before, 0.0), axis=-1, keepdims=True)
    r2 = jnp.sum(jnp.where(hit2, before, 0.0), axis=-1, keepdims=True)
    total = before[tm - 1:tm, :] + sent[tm - 1:tm, :]
    carry[...] = jnp.broadcast_to(total, carry.shape)
    cnt_ref[...] = jnp.broadcast_to(total, cnt_ref.shape).astype(jnp.int32)
    ids = jnp.where(lane == 0.0, i1 - MOE_GROUPS, jnp.where(lane == 1.0, i2 - MOE_GROUPS,
                    jnp.where(lane == 2.0, r1, jnp.where(lane == 3.0, r2, 0.0))))
    id_ref[...] = ids.T[0:SUBLANES, :].astype(jnp.int32)


def _router(h, w_router, b_router):
    tok = h.shape[0]
    tm = TM_ROUTER
    row = lambda i: (i, 0)
    idx = np.arange(tm)
    stril = jnp.asarray((idx[:, None] > idx[None, :]).astype(np.float32), jnp.bfloat16)
    w_hi = w_router.astype(MXU_DTYPE)
    w_lo = (w_router - w_hi.astype(F32)).astype(MXU_DTYPE)
    return pl.pallas_call(
        _router_kernel,
        out_shape=(jax.ShapeDtypeStruct((SUBLANES, tok), jnp.int32), jax.ShapeDtypeStruct((tok, LANES), F32),
                   jax.ShapeDtypeStruct((SUBLANES, LANES), jnp.int32)),
        grid=(tok // tm,),
        in_specs=[pl.BlockSpec((tm, D_MODEL), row), _full((D_MODEL, 2 * LANES)), _full((1, LANES)),
                  _full((tm, tm))],
        out_specs=(pl.BlockSpec((SUBLANES, tm), lambda i: (0, i)), pl.BlockSpec((tm, LANES), row),
                   _full((SUBLANES, LANES))),
        scratch_shapes=[pltpu.VMEM((SUBLANES, LANES), F32)],
        compiler_params=_params("arbitrary"),
        name="moe_router",
    )(h, jnp.concatenate([w_hi, w_lo], axis=1), b_router, stril)


def _pad_pieces(n):
    return tuple(1 << b for b in reversed(range((n - 1).bit_length())))


def _dispatch_kernel(d0_ref, d1_ref, start_ref, cnt_ref, nused_ref, h_ref, xs_hbm, zeros, sem, pad_sem):
    tm = h_ref.shape[0]
    i = pl.program_id(0)
    base = i * tm

    def issue(r, c):
        t = base + r
        pltpu.make_async_copy(h_ref.at[r], xs_hbm.at[d0_ref[t]], sem).start(priority=0)
        pltpu.make_async_copy(h_ref.at[r], xs_hbm.at[d1_ref[t]], sem).start(priority=1)
        return c

    lax.fori_loop(0, tm, issue, 0, unroll=8)

    def pad_copies(e, fn):
        cnt = cnt_ref[e]
        n_pad = (MOE_ROWS - cnt % MOE_ROWS) % MOE_ROWS
        first = start_ref[e] + cnt
        for piece in _pad_pieces(MOE_ROWS):
            @pl.when((n_pad & piece) != 0)
            def _(piece=piece):
                off = first + (n_pad & ~(2 * piece - 1))
                fn(pltpu.make_async_copy(zeros.at[pl.ds(0, piece)], xs_hbm.at[pl.ds(off, piece)], pad_sem))

    def tail_copies(blk, fn):
        for part in range(MOE_ROWS // zeros.shape[0]):
            off = blk * MOE_ROWS + part * zeros.shape[0]
            fn(pltpu.make_async_copy(zeros, xs_hbm.at[pl.ds(off, zeros.shape[0])], pad_sem))

    @pl.when(i == 0)
    def _():
        zeros[...] = jnp.zeros(zeros.shape, F32)
        n_blocks = xs_hbm.shape[0] // MOE_ROWS

        def start(e, c):
            pad_copies(e, lambda cp: cp.start())
            return c

        def wait(e, c):
            pad_copies(e, lambda cp: cp.wait())
            return c

        def tail_start(blk, c):
            tail_copies(blk, lambda cp: cp.start())
            return c

        def tail_wait(blk, c):
            tail_copies(blk, lambda cp: cp.wait())
            return c

        lax.fori_loop(0, N_EXPERTS, start, 0)
        lax.fori_loop(nused_ref[0], n_blocks, tail_start, 0)
        lax.fori_loop(0, N_EXPERTS, wait, 0)
        lax.fori_loop(nused_ref[0], n_blocks, tail_wait, 0)

    for _ in range(TOP_K):
        pltpu.make_async_copy(h_ref, xs_hbm.at[pl.ds(0, tm)], sem).wait()


def _dispatch(h_tiles, dest0, dest1, pad_start, counts, n_used, n_slots):
    tok = h_tiles.shape[0]
    tm = TM_DISPATCH
    assert tok % tm == 0, (tok, tm)
    grid_spec = pltpu.PrefetchScalarGridSpec(
        num_scalar_prefetch=5,
        grid=(tok // tm,),
        in_specs=[pl.BlockSpec((tm,) + ROW_TILE, lambda i, *_: (i, 0, 0))],
        out_specs=pl.BlockSpec(memory_space=pl.ANY),
        scratch_shapes=[pltpu.VMEM((MOE_ROWS // 2,) + ROW_TILE, F32),
                        pltpu.SemaphoreType.DMA(()), pltpu.SemaphoreType.DMA(())],
    )
    return pl.pallas_call(
        _dispatch_kernel,
        out_shape=jax.ShapeDtypeStruct((n_slots,) + ROW_TILE, F32),
        grid_spec=grid_spec,
        compiler_params=_params("arbitrary"),
        name="moe_dispatch",
    )(dest0, dest1, pad_start, counts, n_used, h_tiles)


def _expert_kernel(be_ref, nused_ref, next_ref, slot_ref, x_ref, wg_hbm, wu_hbm, wd_hbm, out_ref,
                   wg_s, wu_s, wd_s, wg_buf, wu_buf, wd_buf, sem, *, layer):
    i = pl.program_id(0)

    def weight_copies(e, slot):
        w = layer * N_EXPERTS + e
        return (pltpu.make_async_copy(wg_hbm.at[w], wg_buf.at[slot], sem.at[slot]),
                pltpu.make_async_copy(wu_hbm.at[w], wu_buf.at[slot], sem.at[slot]),
                pltpu.make_async_copy(wd_hbm.at[w], wd_buf.at[slot], sem.at[slot]))

    @pl.when(i == 0)
    def _():
        for cp in weight_copies(be_ref[0], slot_ref[be_ref[0]]):
            cp.start()

    @pl.when(i < nused_ref[0])
    def _():
        @pl.when((i == 0) | (be_ref[i] != be_ref[jnp.maximum(i - 1, 0)]))
        def _():
            e = be_ref[i]
            slot = slot_ref[e]
            for cp in weight_copies(e, slot):
                cp.wait()

            @pl.when(next_ref[e] >= 0)
            def _():
                for cp in weight_copies(next_ref[e], 1 - slot):
                    cp.start()

            wg_s[...] = wg_buf[slot].astype(MXU_DTYPE)
            wu_s[...] = wu_buf[slot].astype(MXU_DTYPE)
            wd_s[...] = wd_buf[slot].astype(MXU_DTYPE)

        x = _tiles_to_rows(x_ref).astype(MXU_DTYPE)
        n = x.shape[0]
        hidden = []
        for c in range(0, D_EXPERT, FFN_COLS):
            gate = jnp.dot(x, wg_s[:, c:c + FFN_COLS], preferred_element_type=F32)
            up = jnp.dot(x, wu_s[:, c:c + FFN_COLS], preferred_element_type=F32)
            hidden.append((gate * _sigmoid(gate) * up).astype(MXU_DTYPE))
        hidden = jnp.concatenate(hidden, axis=1)
        for c in range(0, D_MODEL, FFN_COLS):
            out = jnp.dot(hidden, wd_s[:, c:c + FFN_COLS], preferred_element_type=F32)
            for s in range(FFN_COLS // LANES):
                out_ref[pl.ds(c // LANES + s, n, stride=ROW_TILE[0]), :] = out[:, s * LANES:(s + 1) * LANES]

    @pl.when(i >= nused_ref[0])
    def _():
        out_ref[...] = jnp.zeros(out_ref.shape, F32)


def _expert_ffn(xs_tiles, block_expert, n_used, counts, w_gate, w_up, w_down, layer):
    n_slots = xs_tiles.shape[0]
    n_blocks = block_expert.shape[0]
    rows = MOE_ROWS * ROW_TILE[0]
    experts = jnp.arange(N_EXPERTS, dtype=jnp.int32)
    nonempty = counts > 0
    later = (experts[None, :] > experts[:, None]) & nonempty[None, :]
    next_expert = jnp.min(jnp.where(later, experts[None, :], N_EXPERTS), axis=1)
    next_expert = jnp.where(next_expert < N_EXPERTS, next_expert, -1).astype(jnp.int32)
    buf_half = ((jnp.cumsum(nonempty.astype(jnp.int32)) - 1) % 2).astype(jnp.int32)
    blk = lambda i, be, nu, nx, sl: (jnp.maximum(jnp.minimum(i, nu[0] - 1), 0), 0)
    grid_spec = pltpu.PrefetchScalarGridSpec(
        num_scalar_prefetch=4,
        grid=(n_blocks,),
        in_specs=[
            pl.BlockSpec((rows, LANES), blk),
            pl.BlockSpec(memory_space=pl.ANY),
            pl.BlockSpec(memory_space=pl.ANY),
            pl.BlockSpec(memory_space=pl.ANY),
        ],
        out_specs=pl.BlockSpec((rows, LANES), lambda i, be, nu, nx, sl: (i, 0)),
        scratch_shapes=[pltpu.VMEM((D_MODEL, D_EXPERT), MXU_DTYPE), pltpu.VMEM((D_MODEL, D_EXPERT), MXU_DTYPE),
                        pltpu.VMEM((D_EXPERT, D_MODEL), MXU_DTYPE),
                        pltpu.VMEM((2, D_MODEL, D_EXPERT), F32), pltpu.VMEM((2, D_MODEL, D_EXPERT), F32),
                        pltpu.VMEM((2, D_EXPERT, D_MODEL), F32), pltpu.SemaphoreType.DMA((2,))],
    )
    out = pl.pallas_call(
        functools.partial(_expert_kernel, layer=layer),
        out_shape=jax.ShapeDtypeStruct((n_slots * ROW_TILE[0], LANES), F32),
        grid_spec=grid_spec,
        compiler_params=_params("arbitrary"),
        name="moe_experts",
    )(block_expert, n_used, next_expert, buf_half, xs_tiles.reshape(n_slots * ROW_TILE[0], LANES),
      w_gate, w_up, w_down)
    return out.reshape((n_slots,) + ROW_TILE)


def _combine_ln_kernel(d0_ref, d1_ref, rows_hbm, h_ref, gate_ref, g_ref, b_ref, o_ref, buf, sem):
    tm = h_ref.shape[0]
    i = pl.program_id(0)

    def gather(tile, p):
        base = tile * tm

        def issue(r, c):
            t = base + r
            dst = pl.ds(pl.multiple_of(r * ROW_TILE[0], ROW_TILE[0]), ROW_TILE[0])
            pltpu.make_async_copy(rows_hbm.at[d0_ref[t]], buf.at[p, 0, dst],
                                  sem.at[p]).start(priority=0)
            pltpu.make_async_copy(rows_hbm.at[d1_ref[t]], buf.at[p, 1, dst],
                                  sem.at[p]).start(priority=1)
            return c

        lax.fori_loop(0, tm, issue, 0, unroll=8)

    @pl.when(i == 0)
    def _():
        gather(0, 0)

    for p in range(2):
        @pl.when(i % 2 == p)
        def _(p=p):
            @pl.when(i + 1 < pl.num_programs(0))
            def _():
                gather(i + 1, 1 - p)

            for k in range(TOP_K):
                pltpu.make_async_copy(buf.at[1 - p, k], buf.at[p, k], sem.at[p]).wait()
            gate = gate_ref[...]
            acc = (ALPHA * h_ref[...] + gate[:, 0:1] * _tiles_to_rows(buf.at[p, 0])
                   + gate[:, 1:2] * _tiles_to_rows(buf.at[p, 1]))
            o_ref[...] = _layer_norm_rows(acc, g_ref[...], b_ref[...])


def _combine_ln(dest0, dest1, rows_tiles, h, gates, g_row, b_row):
    tok = h.shape[0]
    tm = TM_COMBINE
    row = lambda i, *_: (i, 0)
    const = lambda i, *_: (0, 0)
    grid_spec = pltpu.PrefetchScalarGridSpec(
        num_scalar_prefetch=2,
        grid=(tok // tm,),
        in_specs=[
            pl.BlockSpec(memory_space=pl.ANY),
            pl.BlockSpec((tm, D_MODEL), row),
            pl.BlockSpec((tm, LANES), row),
            pl.BlockSpec((1, D_MODEL), const),
            pl.BlockSpec((1, D_MODEL), const),
        ],
        out_specs=pl.BlockSpec((tm, D_MODEL), row),
        scratch_shapes=[pltpu.VMEM((2, TOP_K, tm * ROW_TILE[0], LANES), F32), pltpu.SemaphoreType.DMA((2,))],
    )
    return pl.pallas_call(
        _combine_ln_kernel,
        out_shape=jax.ShapeDtypeStruct((tok, D_MODEL), F32),
        grid_spec=grid_spec,
        compiler_params=_params("arbitrary"),
        name="moe_combine_ln",
    )(dest0, dest1, rows_tiles, h, gates, g_row, b_row)


def _row_slots_kernel(start_ref, ids_ref, out_ref):
    expert = ids_ref[0:TOP_K, :]
    start = jnp.zeros(expert.shape, jnp.int32)
    for e in range(N_EXPERTS):
        start = jnp.where(expert == e, start_ref[e], start)
    out_ref[...] = jnp.zeros(out_ref.shape, jnp.int32)
    out_ref[0:TOP_K, :] = start + ids_ref[TOP_K:2 * TOP_K, :]


def _row_slots(ids, pad_start):
    grid_spec = pltpu.PrefetchScalarGridSpec(
        num_scalar_prefetch=1,
        grid=(1,),
        in_specs=[pl.BlockSpec(ids.shape, lambda i, s: (0, 0))],
        out_specs=pl.BlockSpec(ids.shape, lambda i, s: (0, 0)),
    )
    return pl.pallas_call(
        _row_slots_kernel,
        out_shape=jax.ShapeDtypeStruct(ids.shape, jnp.int32),
        grid_spec=grid_spec,
        name="moe_row_slots",
    )(pad_start, ids)


def _slot_layout(counts, n_blocks):
    padded = (counts + MOE_ROWS - 1) // MOE_ROWS * MOE_ROWS
    pad_end = jnp.cumsum(padded)
    n_used = pad_end[-1:] // MOE_ROWS
    blocks = jnp.arange(n_blocks, dtype=jnp.int32)
    first_row = jnp.minimum(blocks, n_used - 1) * MOE_ROWS
    block_expert = jnp.sum(first_row[:, None] >= pad_end[None, :], axis=1)
    return (pad_end - padded).astype(jnp.int32), block_expert.astype(jnp.int32), n_used.astype(jnp.int32)


def _moe_ln(h, h_tiles, w_router, b_router, w_gate, w_up, w_down, layer, g_row, b_row):
    tok = h.shape[0]
    n_blocks = tok * TOP_K // MOE_ROWS + N_EXPERTS
    ids, gates, counts = _router(h, w_router, b_router)
    counts = counts[0, MOE_GROUPS:MOE_GROUPS + N_EXPERTS]
    pad_start, block_expert, n_used = _slot_layout(counts, n_blocks)
    slots = _row_slots(ids, pad_start)
    dest0, dest1 = slots[0], slots[1]
    xs = _dispatch(h_tiles.reshape((tok,) + ROW_TILE), dest0, dest1, pad_start, counts, n_used,
                   n_blocks * MOE_ROWS)
    rows = _expert_ffn(xs, block_expert, n_used, counts, w_gate, w_up, w_down, layer)
    return _combine_ln(dest0, dest1, rows, h, gates, g_row, b_row)


def _pad_lanes(v):
    return jnp.pad(v, (0, LANES - v.shape[0])).reshape(1, LANES)


def _even_mixer(h, bsz, seq, w_in, conv_a, conv_w, conv_b, dt_bias, a_log, d_skip, norm_w, w_out, g_row, b_row):
    w = jnp.pad(w_in, ((0, 0), (0, AB_PROJ - w_in.shape[1]))).astype(MXU_DTYPE)
    y_a, z, xbc, dt, acs, acst = _even_front(
        h, w, conv_a, conv_w, conv_b.reshape(1, -1), _pad_lanes(dt_bias), _pad_lanes(-jnp.exp(a_log)), bsz, seq)
    dskip_row = jnp.repeat(d_skip, SSM_HEAD_DIM).reshape(1, -1)
    y_b = _ssd(xbc, dt, acs, acst, z, dskip_row, norm_w.reshape(1, -1), bsz, seq)
    return _outproj_ln([y_a, y_b], h, w_out.astype(MXU_DTYPE), g_row, b_row)


def _odd_mixer(h, bsz, seq, w_in, i_bias, f_bias, hnorm_w, fox_f_bias, w_out, g_row, b_row):
    c = np.cumsum((0, MLSTM_W, MLSTM_W, MLSTM_W, MLSTM_HEADS, MLSTM_HEADS, MLSTM_W, FOX_W, FOX_W, FOX_W, FOX_HEADS))
    part = lambda j: w_in[:, c[j]:c[j + 1]]
    q, k, v, i_pre, f_pre, o_pre, fq, fk, fv, ff = (part(j) for j in range(10))
    gate_cols = jnp.concatenate([i_pre, f_pre, ff, f_pre], axis=1)
    gate_cols = jnp.pad(gate_cols, ((0, 0), (0, LANES - gate_cols.shape[1])))
    spread = lambda m: jnp.pad(m.reshape(-1, FOX_HEADS, FOX_HEAD_DIM),
                               ((0, 0), (0, 0), (0, LANES - FOX_HEAD_DIM))).reshape(-1, FOX_AUG)
    w = jnp.concatenate([q, k, v, o_pre, fq, spread(fk), fv, gate_cols], axis=1).astype(MXU_DTYPE)
    gate_bias = _pad_lanes(jnp.concatenate([i_bias, f_bias, fox_f_bias, f_bias]))
    q, k, v, o, fq, fk_aug, fvt_aug, gates, gates_t = _odd_front(h, w, gate_bias, bsz, seq)
    y_c = _mlstm(q, k, v, o, gates, gates_t, hnorm_w.reshape(1, -1), bsz, seq)
    y_d = _fox(fq, fk_aug, fvt_aug, gates_t, bsz, seq)
    return _outproj_ln([y_c, y_d], h, w_out.astype(MXU_DTYPE), g_row, b_row)


def kernel(x, ab_w_in, ab_conv_a, ab_conv_ssm_w, ab_conv_ssm_b, ab_dt_bias, ab_a_log, ab_d_skip, ab_norm_w, ab_w_out, cd_w_in, cd_i_bias, cd_f_bias, cd_hnorm_w, cd_fox_f_bias, cd_w_out, ln1_g, ln1_b, ln2_g, ln2_b, moe_rg_w, moe_rg_b, moe_re_w, moe_re_b, moe_w_gate, moe_w_up, moe_w_down):
    bsz, seq, d = x.shape
    h = x.reshape(bsz * seq, d)
    stack = lambda w: w.reshape((w.shape[0] * w.shape[1],) + w.shape[2:])
    w_gate, w_up, w_down = stack(moe_w_gate), stack(moe_w_up), stack(moe_w_down)
    for layer in range(DEPTH):
        j = layer // 2
        g1, b1 = ln1_g[layer].reshape(1, -1), ln1_b[layer].reshape(1, -1)
        if layer % 2 == 0:
            h, h_tiles = _even_mixer(h, bsz, seq, ab_w_in[j], ab_conv_a[j], ab_conv_ssm_w[j], ab_conv_ssm_b[j],
                                     ab_dt_bias[j], ab_a_log[j], ab_d_skip[j], ab_norm_w[j], ab_w_out[j], g1, b1)
        else:
            h, h_tiles = _odd_mixer(h, bsz, seq, cd_w_in[j], cd_i_bias[j], cd_f_bias[j], cd_hnorm_w[j],
                                    cd_fox_f_bias[j], cd_w_out[j], g1, b1)
        re_w = jnp.transpose(moe_re_w[layer], (1, 0, 2)).reshape(d, N_EXPERTS)
        w_router = jnp.pad(jnp.concatenate([moe_rg_w[layer], re_w], axis=1),
                           ((0, 0), (0, LANES - MOE_GROUPS - N_EXPERTS)))
        b_router = _pad_lanes(jnp.concatenate([moe_rg_b[layer], moe_re_b[layer].reshape(-1)]))
        h = _moe_ln(h, h_tiles, w_router, b_router, w_gate, w_up, w_down, layer,
                    ln2_g[layer].reshape(1, -1), ln2_b[layer].reshape(1, -1))
    return h.reshape(bsz, seq, d)
```

```python
import functools

import numpy as np
import jax
import jax.numpy as jnp
from jax import lax
from jax.experimental import pallas as pl
from jax.experimental.pallas import tpu as pltpu

F32 = jnp.float32
MXU_DTYPE = jnp.bfloat16

D_MODEL = 1024
DEPTH = 4
ALPHA = (2 * DEPTH) ** 0.25
LN_EPS = 1e-5
CONV_DIM = D_MODEL // 2
CONV_WIDTH = 3
SSM_D_INNER = D_MODEL
SSM_HEAD_DIM = 64
SSM_HEADS = SSM_D_INNER // SSM_HEAD_DIM
SSM_GROUPS = 4
SSM_STATE = 64
SSM_CONV = 4
SSM_BC = SSM_GROUPS * SSM_STATE
SSM_CONV_DIM = SSM_D_INNER + 2 * SSM_BC
MLSTM_HEADS = 4
MLSTM_HEAD_DIM = D_MODEL // 8
MLSTM_W = MLSTM_HEADS * MLSTM_HEAD_DIM
FOX_HEADS = 8
FOX_HEAD_DIM = D_MODEL // 16
FOX_W = FOX_HEADS * FOX_HEAD_DIM
MOE_GROUPS = 4
EXPERTS_PER_GROUP = 8
N_EXPERTS = MOE_GROUPS * EXPERTS_PER_GROUP
TOP_K = 2
D_EXPERT = D_MODEL // 2

LANES = 128
SUBLANES = 8
VMEM_LIMIT_BYTES = 56 * 1024 * 1024

CHUNK = 128
SEQ_PAIR = 4
MLSTM_SEQS = 1
MLSTM_CHUNK = 512
TM_FRONT = 512
TM_EVEN_FRONT = 256
CONV_ROWS, CONV_LANES = 32, 512
TM_OUT = 1024
TM_ROUTER = 512
TQ_FOX = 256
MOE_ROWS = 512
FFN_COLS = 256
TM_COMBINE = 512
TM_DISPATCH = 2048

AB_PROJ = 4224
FOX_AUG = FOX_HEADS * LANES
CD_FK = 5 * 512
CD_FV = CD_FK + FOX_AUG
CD_GATES = CD_FV + FOX_W
CD_PROJ = CD_GATES + LANES
LOG2E = 1.4426950408889634
TK_FOX = 128
FOX_ACC_ROWS = FOX_HEAD_DIM + SUBLANES
FOX_VROWS = LANES
G_I, G_F, G_FOX, G_BCUM = 0, 4, 8, 16


def _params(*sem):
    return pltpu.CompilerParams(dimension_semantics=sem, vmem_limit_bytes=VMEM_LIMIT_BYTES)


def _softplus(x):
    return jnp.maximum(x, 0.0) + jnp.log(1.0 + jnp.exp(-jnp.abs(x)))


def _sigmoid(x):
    return 1.0 / (1.0 + jnp.exp(-x))


def _layer_norm_rows(v, g, b):
    mu = jnp.mean(v, axis=-1, keepdims=True)
    c = v - mu
    var = jnp.mean(c * c, axis=-1, keepdims=True)
    return c * lax.rsqrt(var + LN_EPS) * g + b


def _tril(n, block):
    i = np.arange(n)
    m = (i[:, None] >= i[None, :]) & (i[:, None] // block == i[None, :] // block)
    return jnp.asarray(m.astype(np.float32), MXU_DTYPE)


def _full(shape):
    return pl.BlockSpec(shape, lambda *_: (0,) * len(shape), pipeline_mode=pl.Buffered(1))


ROW_TILE = (D_MODEL // LANES, LANES)


def _tiles_to_rows(ref):
    n = ref.shape[0] // ROW_TILE[0]
    return jnp.concatenate([ref[pl.ds(s, n, stride=ROW_TILE[0]), :] for s in range(ROW_TILE[0])], axis=1)


def _rows_to_tiles(ref, val):
    n = val.shape[0]
    for s in range(ROW_TILE[0]):
        ref[pl.ds(s, n, stride=ROW_TILE[0]), :] = val[:, s * LANES:(s + 1) * LANES]


def _split3(x):
    narrow = lambda v: v.astype(MXU_DTYPE).astype(F32)
    x1 = narrow(x)
    x2 = narrow(x - x1)
    return x1, x2, narrow(x - x1 - x2)


def _cumsum_rows(tril, x):
    parts = jnp.dot(tril, jnp.concatenate(_split3(x), axis=1).astype(MXU_DTYPE), preferred_element_type=F32)
    return parts[:, 0:LANES] + parts[:, LANES:2 * LANES] + parts[:, 2 * LANES:3 * LANES]


def _even_front_kernel(h_ref, w_ref, ca_ref, cw_ref, cb_ref, dtb_ref, aneg_ref, tril_ref,
                       ya_ref, z_ref, xbc_ref, dt_ref, acs_ref, acst_ref,
                       proj_buf, ua_ext, xbc_ext, *, tiles_per_seq):
    tm = h_ref.shape[0]
    i = pl.program_id(0)
    cur = i % 2

    @pl.when(i == 0)
    def _():
        proj_buf[1] = jnp.zeros(proj_buf.shape[1:], F32)

    @pl.when((i == 0) | ((i - 1) % tiles_per_seq == 0))
    def _():
        ua_ext[0:SUBLANES, :] = jnp.zeros((SUBLANES, CONV_DIM), F32)
        xbc_ext[0:SUBLANES, :] = jnp.zeros((SUBLANES, SSM_CONV_DIM), F32)

    z0 = 3 * CONV_DIM
    x0 = z0 + SSM_D_INNER
    d0 = x0 + SSM_CONV_DIM

    def step(p):
        x_in = h_ref[...].astype(MXU_DTYPE)
        proj = proj_buf.at[1 - p]
        z_ref[...] = proj[:, z0:x0]

        n_blocks = tm // CONV_ROWS
        col_cuts = [AB_PROJ * g // n_blocks // LANES * LANES for g in range(n_blocks)] + [AB_PROJ]
        for r0 in range(0, tm, CONV_ROWS):
            c_lo, c_hi = col_cuts[r0 // CONV_ROWS], col_cuts[r0 // CONV_ROWS + 1]
            proj_buf[p, :, c_lo:c_hi] = jnp.dot(x_in, w_ref[:, c_lo:c_hi], preferred_element_type=F32)
            rows = slice(r0, r0 + CONV_ROWS)
            ext_rows = slice(SUBLANES + r0, SUBLANES + r0 + CONV_ROWS)
            ua_ext[ext_rows, :] = proj[rows, CONV_DIM:2 * CONV_DIM] * proj[rows, 2 * CONV_DIM:3 * CONV_DIM]
            conv = None
            for k in range(CONV_WIDTH):
                tap = ca_ref[k:k + 1, :] * ua_ext[pl.ds(SUBLANES + r0 - (CONV_WIDTH - 1) + k, CONV_ROWS), :]
                conv = tap if conv is None else conv + tap
            ya_ref[rows, :] = (proj[rows, 0:CONV_DIM] * conv).astype(ya_ref.dtype)
            for c0 in range(0, SSM_CONV_DIM, CONV_LANES):
                cols = slice(c0, c0 + CONV_LANES)
                xbc_ext[ext_rows, cols] = proj[rows, x0 + c0:x0 + c0 + CONV_LANES]
                conv = cb_ref[:, cols]
                for k in range(SSM_CONV):
                    conv = conv + (cw_ref[k:k + 1, cols]
                                   * xbc_ext[pl.ds(SUBLANES + r0 - (SSM_CONV - 1) + k, CONV_ROWS), cols])
                xbc_ref[rows, cols] = conv * _sigmoid(conv)
        ua_ext[0:SUBLANES, :] = ua_ext[tm:tm + SUBLANES, :]
        xbc_ext[0:SUBLANES, :] = xbc_ext[tm:tm + SUBLANES, :]

        dt = _softplus(proj[:, d0:d0 + LANES] + dtb_ref[...])
        a = dt * aneg_ref[...]
        acs = _cumsum_rows(tril_ref[...], a)
        dt_ref[...] = dt
        acs_ref[...] = acs
        acst_ref[...] = acs.T

    for parity in range(2):
        @pl.when(cur == parity)
        def _(parity=parity):
            step(parity)


def _even_front(h, w_in, conv_a, conv_w, conv_b, dt_bias_row, aneg_row, bsz, seq):
    tok = bsz * seq
    tm = TM_EVEN_FRONT
    n_tiles = tok // tm
    row = lambda i: (jnp.maximum(i - 1, 0), 0)
    out_shapes = (
        jax.ShapeDtypeStruct((tok, CONV_DIM), MXU_DTYPE),
        jax.ShapeDtypeStruct((tok, SSM_D_INNER), F32),
        jax.ShapeDtypeStruct((tok, SSM_CONV_DIM), F32),
        jax.ShapeDtypeStruct((tok, LANES), F32),
        jax.ShapeDtypeStruct((tok, LANES), F32),
        jax.ShapeDtypeStruct((LANES, tok), F32),
    )
    return pl.pallas_call(
        functools.partial(_even_front_kernel, tiles_per_seq=seq // tm),
        out_shape=out_shapes,
        grid=(n_tiles + 1,),
        in_specs=[
            pl.BlockSpec((tm, D_MODEL), lambda i: (jnp.minimum(i, n_tiles - 1), 0)),
            _full((D_MODEL, AB_PROJ)),
            _full((CONV_WIDTH, CONV_DIM)),
            _full((SSM_CONV, SSM_CONV_DIM)),
            _full((1, SSM_CONV_DIM)),
            _full((1, LANES)),
            _full((1, LANES)),
            _full((tm, tm)),
        ],
        out_specs=(
            pl.BlockSpec((tm, CONV_DIM), row),
            pl.BlockSpec((tm, SSM_D_INNER), row),
            pl.BlockSpec((tm, SSM_CONV_DIM), row),
            pl.BlockSpec((tm, LANES), row),
            pl.BlockSpec((tm, LANES), row),
            pl.BlockSpec((LANES, tm), lambda i: (0, jnp.maximum(i - 1, 0))),
        ),
        scratch_shapes=[
            pltpu.VMEM((2, tm, AB_PROJ), F32),
            pltpu.VMEM((tm + SUBLANES, CONV_DIM), F32),
            pltpu.VMEM((tm + SUBLANES, SSM_CONV_DIM), F32),
        ],
        compiler_params=_params("arbitrary"),
        name="even_front",
    )(h, w_in, conv_a, conv_w, conv_b, dt_bias_row, aneg_row, _tril(tm, CHUNK))


def _bcast_heads(arr, n_heads, width):
    per = LANES // width
    length = arr.shape[0]
    lane = lax.broadcasted_iota(jnp.int32, (length, LANES), 1)
    outs = []
    for j in range(n_heads // per):
        v = jnp.broadcast_to(arr[:, j * per:j * per + 1], (length, LANES))
        for r in range(1, per):
            v = jnp.where(lane >= r * width, jnp.broadcast_to(arr[:, j * per + r:j * per + r + 1], (length, LANES)), v)
        outs.append(v)
    return jnp.concatenate(outs, axis=1)


def _expand_heads(arr, expand_ref):
    hi, lo, _ = _split3(arr)
    return jnp.dot(jnp.concatenate([hi, lo], axis=1).astype(MXU_DTYPE), expand_ref[...],
                   preferred_element_type=F32)


def _ssd_kernel(*refs):
    nb = SEQ_PAIR
    xbc_ref, dt_ref, acs_ref, z_ref = refs[:4]
    acst_refs = refs[4:4 + nb]
    dskip_ref, nw_ref, expand_ref, y_ref, state = refs[4 + nb:]
    L = CHUNK
    P = SSM_HEAD_DIM
    R = SSM_HEADS // SSM_GROUPS
    GW = R * P

    @pl.when(pl.program_id(1) == 0)
    def _():
        state[...] = jnp.zeros(state.shape, F32)

    row = lax.broadcasted_iota(jnp.int32, (L, L), 0)
    col = lax.broadcasted_iota(jnp.int32, (L, L), 1)
    causal = row >= col
    lane_g = lax.broadcasted_iota(jnp.int32, (L, GW), 1)

    per_row = []
    for j in range(nb):
        acs = acs_ref[j]
        per_row += [dt_ref[j], jnp.exp(acs[L - 1:L, :] - acs), jnp.exp(acs)]
    expanded = _expand_heads(jnp.concatenate(per_row, axis=0), expand_ref)

    st_old = [state[c] for c in range(nb * SSM_GROUPS)]
    st_new = []
    for j in range(nb):
        xs = xbc_ref[j, :, 0:SSM_D_INNER]
        bm = xbc_ref[j, :, SSM_D_INNER:SSM_D_INNER + SSM_BC]
        cm = xbc_ref[j, :, SSM_D_INNER + SSM_BC:SSM_CONV_DIM]
        acs = acs_ref[j]
        acst = acst_refs[j][...]
        a_last = acs[L - 1:L, :]
        dtx = expanded[(3 * j) * L:(3 * j + 1) * L, :]
        decx = expanded[(3 * j + 1) * L:(3 * j + 2) * L, :]
        expx = expanded[(3 * j + 2) * L:(3 * j + 3) * L, :]
        xdt = xs * dtx
        xdec = (xdt * decx).astype(MXU_DTYPE)
        xdt_m = xdt.astype(MXU_DTYPE)
        chunk_decay = jnp.exp(jnp.broadcast_to(a_last, (SUBLANES, LANES)))
        cdx = _bcast_heads(chunk_decay, SSM_HEADS, P)[0:1, :]

        bm_t = bm.T.astype(MXU_DTYPE)
        cm_m = cm.astype(MXU_DTYPE)
        bm_m = bm.astype(MXU_DTYPE)

        ys = []
        for g in range(SSM_GROUPS):
            n0 = g * SSM_STATE
            c_g = cm_m[:, n0:n0 + SSM_STATE]
            cb = lax.dot_general(c_g, bm_m[:, n0:n0 + SSM_STATE], (((1,), (1,)), ((), ())),
                                 preferred_element_type=F32)
            ms = []
            for r in range(R):
                hd = g * R + r
                seg = jnp.exp(jnp.where(causal, acs[:, hd:hd + 1] - acst[hd:hd + 1, :], -jnp.inf))
                ms.append((cb * seg).astype(MXU_DTYPE))
            big = jnp.dot(jnp.concatenate(ms, axis=0), xdt_m[:, g * GW:(g + 1) * GW],
                          preferred_element_type=F32)
            y_diag = big[0:L, :]
            for r in range(1, R):
                y_diag = jnp.where(lane_g >= r * P, big[r * L:(r + 1) * L, :], y_diag)
            st = st_old[j * SSM_GROUPS + g]
            y_off = jnp.dot(c_g, st.astype(MXU_DTYPE), preferred_element_type=F32)
            new = jnp.dot(bm_t[n0:n0 + SSM_STATE, :], xdec[:, g * GW:(g + 1) * GW],
                          preferred_element_type=F32)
            st_new.append(st * cdx[:, g * GW:(g + 1) * GW] + new)
            ys.append(y_diag + y_off * expx[:, g * GW:(g + 1) * GW])
        y = jnp.concatenate(ys, axis=1) + dskip_ref[...] * xs
        z = z_ref[j]
        u = y * (z * _sigmoid(z))
        y = u * lax.rsqrt(jnp.mean(u * u, axis=-1, keepdims=True) + LN_EPS) * nw_ref[...]
        y_ref[j] = y.astype(y_ref.dtype)

    for c in range(nb * SSM_GROUPS):
        state[c] = st_new[c]


def _ssd(xbc, dt, acs, acst, z, dskip_row, normw_row, bsz, seq):
    nb = SEQ_PAIR
    nc = seq // CHUNK
    assert bsz % nb == 0 and seq % CHUNK == 0, (bsz, seq)
    per_seq = lambda a: a.reshape(bsz, seq, a.shape[-1])
    blk = lambda width: pl.BlockSpec((nb, CHUNK, width), lambda g, c: (g, c, 0))
    acst_specs = [pl.BlockSpec((LANES, CHUNK), lambda g, c, j=j: (0, (g * nb + j) * nc + c)) for j in range(nb)]
    expand = np.zeros((LANES, SSM_D_INNER), np.float32)
    for hd in range(SSM_HEADS):
        expand[hd, hd * SSM_HEAD_DIM:(hd + 1) * SSM_HEAD_DIM] = 1.0
    expand = jnp.asarray(np.concatenate([expand] * 2, axis=0), MXU_DTYPE)
    y = pl.pallas_call(
        _ssd_kernel,
        out_shape=jax.ShapeDtypeStruct((bsz, seq, SSM_D_INNER), MXU_DTYPE),
        grid=(bsz // nb, nc),
        in_specs=[blk(SSM_CONV_DIM), blk(LANES), blk(LANES), blk(SSM_D_INNER)] + acst_specs + [
            _full((1, SSM_D_INNER)),
            _full((1, SSM_D_INNER)),
            _full((2 * LANES, SSM_D_INNER)),
        ],
        out_specs=blk(SSM_D_INNER),
        scratch_shapes=[pltpu.VMEM((nb * SSM_GROUPS, SSM_STATE, SSM_D_INNER // SSM_GROUPS), F32)],
        compiler_params=_params("arbitrary", "arbitrary"),
        name="ssd_scan",
    )(per_seq(xbc), per_seq(dt), per_seq(acs), per_seq(z), *([acst] * nb), dskip_row, normw_row, expand)
    return y.reshape(bsz * seq, SSM_D_INNER)


def _outproj_ln_kernel(*refs, widths):
    n = len(widths)
    parts = refs[:n]
    h_ref, w_ref, g_ref, b_ref, o_ref, ot_ref = refs[n:]
    acc = ALPHA * h_ref[...]
    off = 0
    for p, wd in zip(parts, widths):
        acc = acc + jnp.dot(p[...].astype(MXU_DTYPE), w_ref[off:off + wd, :], preferred_element_type=F32)
        off += wd
    out = _layer_norm_rows(acc, g_ref[...], b_ref[...])
    o_ref[...] = out
    _rows_to_tiles(ot_ref, out)


def _outproj_ln(parts, h, w_out, g_row, b_row):
    tok = h.shape[0]
    tm = TM_OUT
    widths = tuple(p.shape[1] for p in parts)
    row = lambda i: (i, 0)
    return pl.pallas_call(
        functools.partial(_outproj_ln_kernel, widths=widths),
        out_shape=(jax.ShapeDtypeStruct((tok, D_MODEL), F32),
                   jax.ShapeDtypeStruct((tok * ROW_TILE[0], LANES), F32)),
        grid=(tok // tm,),
        in_specs=[pl.BlockSpec((tm, wd), row) for wd in widths] + [
            pl.BlockSpec((tm, D_MODEL), row),
            _full((sum(widths), D_MODEL)),
            _full((1, D_MODEL)),
            _full((1, D_MODEL)),
        ],
        out_specs=(pl.BlockSpec((tm, D_MODEL), row), pl.BlockSpec((tm * ROW_TILE[0], LANES), row)),
        compiler_params=_params("arbitrary"),
        name="outproj_ln",
    )(*parts, h, w_out, g_row, b_row)


def _odd_front_kernel(h_ref, w_ref, gb_ref, tril_ref,
                      q_ref, k_ref, v_ref, o_ref, fq_ref, fk_ref, fvt_ref, g_ref, gt_ref, carry):
    tm = h_ref.shape[0]
    x_in = h_ref[...].astype(MXU_DTYPE)
    proj_cols = lambda lo, hi: jnp.dot(x_in, w_ref[:, lo:hi], preferred_element_type=F32)

    @pl.when(pl.program_id(1) == 0)
    def _():
        carry[...] = jnp.zeros(carry.shape, F32)

    raw = proj_cols(CD_GATES, CD_GATES + LANES) + gb_ref[...]
    lane = lax.broadcasted_iota(jnp.int32, (tm, LANES), 1)
    g = jnp.where(lane < G_F, raw, -_softplus(-raw))
    prev = carry[0:1, :]
    glob = _cumsum_rows(tril_ref[...], g) + prev
    before = []
    for c in range(tm // MLSTM_CHUNK):
        before.append(jnp.broadcast_to(prev, (MLSTM_CHUNK, LANES)))
        prev = glob[(c + 1) * MLSTM_CHUNK - 1:(c + 1) * MLSTM_CHUNK, :]
    carry[...] = jnp.broadcast_to(prev, carry.shape)
    local = glob - jnp.concatenate(before, axis=0)
    out = jnp.where((lane >= G_FOX) & (lane < G_BCUM), glob, jnp.where(lane >= G_BCUM, local, g))
    g_ref[...] = out
    gt_ref[...] = out.T

    q_ref[...] = proj_cols(0, 512)
    k_ref[...] = proj_cols(512, 1024) * (MLSTM_HEAD_DIM ** -0.5)
    v_ref[...] = proj_cols(1024, 1536)
    o_ref[...] = _sigmoid(proj_cols(1536, 2048))
    fq_ref[...] = proj_cols(2048, 2560) * (FOX_HEAD_DIM ** -0.5 * LOG2E)

    is_bias = (lane >= FOX_HEAD_DIM) & (lane < FOX_HEAD_DIM + 3)
    fk = proj_cols(CD_FK, CD_FK + FOX_AUG)
    for hd in range(FOX_HEADS):
        c1, c2, c3 = _split3(out[:, G_FOX + hd:G_FOX + hd + 1] * (-LOG2E))
        bias = jnp.where(lane == FOX_HEAD_DIM, c1, jnp.where(lane == FOX_HEAD_DIM + 1, c2, c3))
        k_h = fk[:, hd * LANES:(hd + 1) * LANES]
        fk_ref[:, hd * LANES:(hd + 1) * LANES] = jnp.where(is_bias, bias, k_h).astype(MXU_DTYPE)
    v_t = proj_cols(CD_FV, CD_FV + FOX_W).T
    extra = jnp.where(lax.broadcasted_iota(jnp.int32, (FOX_VROWS - FOX_HEAD_DIM, tm), 0) == 0, 1.0, 0.0)
    fvt_ref[...] = jnp.concatenate(
        [blk for hd in range(FOX_HEADS) for blk in (v_t[hd * FOX_HEAD_DIM:(hd + 1) * FOX_HEAD_DIM, :], extra)],
        axis=0).astype(MXU_DTYPE)


def _odd_front(h, w_in, gate_bias_row, bsz, seq):
    tok = bsz * seq
    tm = TM_FRONT
    ns = seq // tm
    row = lambda b, s: (b * ns + s, 0)
    col = lambda b, s: (0, b * ns + s)
    wide = jax.ShapeDtypeStruct((tok, 512), F32)
    return pl.pallas_call(
        _odd_front_kernel,
        out_shape=(wide,) * 5 + (jax.ShapeDtypeStruct((tok, FOX_AUG), MXU_DTYPE),
                                 jax.ShapeDtypeStruct((FOX_HEADS * FOX_VROWS, tok), MXU_DTYPE),
                                 jax.ShapeDtypeStruct((tok, LANES), F32),
                                 jax.ShapeDtypeStruct((LANES, tok), F32)),
        grid=(bsz, ns),
        in_specs=[
            pl.BlockSpec((tm, D_MODEL), row),
            _full((D_MODEL, CD_PROJ)),
            _full((1, LANES)),
            _full((tm, tm)),
        ],
        out_specs=(pl.BlockSpec((tm, 512), row),) * 5 + (
            pl.BlockSpec((tm, FOX_AUG), row),
            pl.BlockSpec((FOX_HEADS * FOX_VROWS, tm), col),
            pl.BlockSpec((tm, LANES), row),
            pl.BlockSpec((LANES, tm), col),
        ),
        scratch_shapes=[pltpu.VMEM((SUBLANES, LANES), F32)],
        compiler_params=_params("arbitrary", "arbitrary"),
        name="odd_front",
    )(h, w_in, gate_bias_row, _tril(tm, tm))


def _mlstm_kernel(*refs):
    nb = MLSTM_SEQS
    q_ref, k_ref, v_ref, o_ref, g_ref = refs[:5]
    gt_refs = refs[5:5 + nb]
    nw_ref, y_ref, c_state, m_state = refs[5 + nb:]
    L = MLSTM_CHUNK
    DH = MLSTM_HEAD_DIM

    @pl.when(pl.program_id(1) == 0)
    def _():
        c_state[...] = jnp.zeros(c_state.shape, F32)
        m_state[...] = jnp.zeros(m_state.shape, F32)

    row = lax.broadcasted_iota(jnp.int32, (L, L), 0)
    col = lax.broadcasted_iota(jnp.int32, (L, L), 1)
    causal = row >= col
    ones_col = jnp.where(lax.broadcasted_iota(jnp.int32, (L, DH), 1) == 0, 1.0, 0.0)

    for j in range(nb):
        gates = g_ref[j]
        gates_t = gt_refs[j][...]
        for hd in range(MLSTM_HEADS):
            st = j * MLSTM_HEADS + hd
            sl = slice(hd * DH, (hd + 1) * DH)
            q = q_ref[j, :, sl].astype(MXU_DTYPE)
            k = k_ref[j, :, sl]
            v_ext = jnp.concatenate([v_ref[j, :, sl], ones_col], axis=1).astype(MXU_DTYPE)
            b_col = gates[:, G_BCUM + hd:G_BCUM + hd + 1]
            i_col = gates[:, G_I + hd:G_I + hd + 1]
            b_row = gates_t[G_BCUM + hd:G_BCUM + hd + 1, :]
            i_row = gates_t[G_I + hd:G_I + hd + 1, :]
            m_prev = m_state[st:st + 1, 0:1]
            c_ext = c_state[st]

            d_mat = jnp.where(causal, b_col - b_row + i_row, -jnp.inf)
            inter = b_col + m_prev
            m_t = jnp.maximum(jnp.max(d_mat, axis=-1, keepdims=True), inter)
            s_qk = lax.dot_general(q, k.astype(MXU_DTYPE), (((1,), (1,)), ((), ())), preferred_element_type=F32)
            w_qk = s_qk * jnp.exp(d_mat - m_t)
            s_inter = jnp.exp(inter - m_t)
            num_ext = (jnp.dot(w_qk.astype(MXU_DTYPE), v_ext, preferred_element_type=F32)
                       + s_inter * jnp.dot(q, c_ext.astype(MXU_DTYPE), preferred_element_type=F32))
            den = num_ext[:, DH:DH + 1]
            hval = num_ext[:, 0:DH] / jnp.maximum(jnp.abs(den), jnp.exp(-m_t))

            b_last = b_col[L - 1:L, :]
            g_log = b_last - b_col + i_col
            m_new = jnp.maximum(b_last + m_prev, jnp.max(g_log, axis=0, keepdims=True))
            w_k = jnp.exp(g_log - m_new)
            decay = jnp.exp(b_last + m_prev - m_new)
            kw_t = (k * w_k).T.astype(MXU_DTYPE)
            c_state[st] = decay * c_ext + jnp.dot(kw_t, v_ext, preferred_element_type=F32)
            m_state[st:st + 1, :] = jnp.broadcast_to(m_new, (1, LANES))

            mu = jnp.mean(hval, axis=-1, keepdims=True)
            cen = hval - mu
            var = jnp.mean(cen * cen, axis=-1, keepdims=True)
            y = o_ref[j, :, sl] * (cen * lax.rsqrt(var + LN_EPS) * nw_ref[:, sl])
            y_ref[j, :, sl] = y.astype(y_ref.dtype)


def _mlstm(q, k, v, o, gates, gates_t, hnorm_row, bsz, seq):
    nb = MLSTM_SEQS
    nc = seq // MLSTM_CHUNK
    assert bsz % nb == 0 and seq % MLSTM_CHUNK == 0, (bsz, seq)
    per_seq = lambda a: a.reshape(bsz, seq, a.shape[-1])
    blk = lambda width: pl.BlockSpec((nb, MLSTM_CHUNK, width), lambda g, c: (g, c, 0))
    gt_specs = [pl.BlockSpec((LANES, MLSTM_CHUNK), lambda g, c, j=j: (0, (g * nb + j) * nc + c))
                for j in range(nb)]
    y = pl.pallas_call(
        _mlstm_kernel,
        out_shape=jax.ShapeDtypeStruct((bsz, seq, MLSTM_W), MXU_DTYPE),
        grid=(bsz // nb, nc),
        in_specs=[blk(MLSTM_W)] * 4 + [blk(LANES)] + gt_specs + [_full((1, MLSTM_W))],
        out_specs=blk(MLSTM_W),
        scratch_shapes=[pltpu.VMEM((nb * MLSTM_HEADS, MLSTM_HEAD_DIM, 2 * MLSTM_HEAD_DIM), F32),
                        pltpu.VMEM((nb * MLSTM_HEADS, LANES), F32)],
        compiler_params=_params("arbitrary", "arbitrary"),
        name="mlstm_scan",
    )(per_seq(q), per_seq(k), per_seq(v), per_seq(o), per_seq(gates), *([gates_t] * nb), hnorm_row)
    return y.reshape(bsz * seq, MLSTM_W)


def _fox_kernel(q_ref, k_ref, vt_ref, gt_ref, y_ref, *scratch):
    acc_refs = scratch[:FOX_HEADS]
    qa_ref = scratch[FOX_HEADS]
    tq = q_ref.shape[0]
    tk = TK_FOX
    dh = FOX_HEAD_DIM
    qi = pl.program_id(1)
    q_t = q_ref[...].T
    bias_rows = jnp.where(lax.broadcasted_iota(jnp.int32, (LANES - dh, tq), 0) < 3, 1.0, 0.0)
    for hd in range(FOX_HEADS):
        qa_ref[hd] = jnp.concatenate([q_t[hd * dh:(hd + 1) * dh, :], bias_rows], axis=0).astype(MXU_DTYPE)
        acc_refs[hd][...] = jnp.zeros(acc_refs[hd].shape, F32)
    cq = gt_ref[G_FOX:G_FOX + FOX_HEADS, :] * LOG2E
    key_pos = lax.broadcasted_iota(jnp.int32, (tk, tq), 0)
    qry_pos = lax.broadcasted_iota(jnp.int32, (tk, tq), 1) + qi * tq
    n_full = qi * (tq // tk)

    def block(j, ms, masked):
        k0 = pl.multiple_of(j * tk, tk)
        out = []
        for hd in range(FOX_HEADS):
            hs = slice(hd * LANES, (hd + 1) * LANES)
            s = jnp.dot(k_ref[pl.ds(k0, tk), hs], qa_ref[hd], preferred_element_type=F32)
            if masked:
                s = jnp.where(key_pos + k0 <= qry_pos, s, -jnp.inf)
            cq_h = cq[hd:hd + 1, :]
            m_new = jnp.maximum(ms[hd], jnp.max(s, axis=0, keepdims=True) + cq_h)
            p = jnp.exp2(s - (m_new - cq_h))
            pv = jnp.dot(vt_ref[hd * FOX_VROWS:(hd + 1) * FOX_VROWS, pl.ds(k0, tk)], p.astype(MXU_DTYPE),
                         preferred_element_type=F32)
            acc_refs[hd][...] = jnp.exp2(ms[hd] - m_new) * acc_refs[hd][...] + pv[0:FOX_ACC_ROWS, :]
            out.append(m_new)
        return tuple(out)

    ms = tuple(jnp.full((1, tq), -jnp.inf, F32) for _ in range(FOX_HEADS))
    per_trip = tq // tk

    def past_blocks(t, ms):
        for d in range(per_trip):
            ms = block(t * per_trip + d, ms, masked=False)
        return ms

    ms = lax.fori_loop(0, qi, past_blocks, ms)
    for d in range(tq // tk):
        ms = block(n_full + d, ms, masked=True)
    outs = []
    for hd in range(FOX_HEADS):
        acc = acc_refs[hd][...]
        outs.append(acc[0:dh, :] / acc[dh:dh + 1, :])
    y_ref[...] = jnp.concatenate(outs, axis=0).T.astype(y_ref.dtype)


def _fox(fq, fk_aug, fvt_aug, gates_t, bsz, seq):
    tok = bsz * seq
    tq = TQ_FOX
    nq = seq // tq
    return pl.pallas_call(
        _fox_kernel,
        out_shape=jax.ShapeDtypeStruct((tok, FOX_W), MXU_DTYPE),
        grid=(bsz, nq),
        in_specs=[
            pl.BlockSpec((tq, FOX_W), lambda b, i: (b * nq + i, 0)),
            pl.BlockSpec((seq, FOX_AUG), lambda b, i: (b, 0)),
            pl.BlockSpec((FOX_HEADS * FOX_VROWS, seq), lambda b, i: (0, b)),
            pl.BlockSpec((LANES, tq), lambda b, i: (0, b * nq + i)),
        ],
        out_specs=pl.BlockSpec((tq, FOX_W), lambda b, i: (b * nq + i, 0)),
        scratch_shapes=[pltpu.VMEM((FOX_ACC_ROWS, tq), F32)] * FOX_HEADS
                       + [pltpu.VMEM((FOX_HEADS, LANES, tq), MXU_DTYPE)],
        compiler_params=_params("arbitrary", "arbitrary"),
        name="fox_attention",
    )(fq, fk_aug, fvt_aug, gates_t)


def _router_kernel(h_ref, w_ref, b_ref, stril_ref, id_ref, gate_ref, cnt_ref, carry):
    tm = h_ref.shape[0]

    @pl.when(pl.program_id(0) == 0)
    def _():
        carry[...] = jnp.zeros(carry.shape, F32)

    h = h_ref[...]
    h_hi = h.astype(MXU_DTYPE)
    h_lo = (h - h_hi.astype(F32)).astype(MXU_DTYPE)
    both = jnp.dot(h_hi, w_ref[...], preferred_element_type=F32)
    logits = (both[:, 0:LANES] + both[:, LANES:2 * LANES]
              + jnp.dot(h_lo, w_ref[:, 0:LANES], preferred_element_type=F32) + b_ref[...])
    lane = lax.broadcasted_iota(jnp.int32, (tm, LANES), 1).astype(F32)
    neg = -jnp.inf
    first = lambda hit: jnp.min(jnp.where(hit, lane, float(LANES)), axis=-1, keepdims=True)
    gl = jnp.where(lane < MOE_GROUPS, logits, neg)
    g_max = jnp.max(gl, axis=-1, keepdims=True)
    g_idx = first(gl == g_max)
    p_group = 1.0 / jnp.sum(jnp.exp(gl - g_max), axis=-1, keepdims=True)
    e_lo = MOE_GROUPS + g_idx * EXPERTS_PER_GROUP
    el = jnp.where((lane >= e_lo) & (lane < e_lo + EXPERTS_PER_GROUP), logits, neg)
    v1 = jnp.max(el, axis=-1, keepdims=True)
    i1 = first(el == v1)
    el2 = jnp.where(lane == i1, neg, el)
    v2 = jnp.max(el2, axis=-1, keepdims=True)
    i2 = first(el2 == v2)
    t = jnp.exp(v2 - v1)
    w1 = 1.0 / (1.0 + t)
    gate_ref[...] = jnp.where(lane == 0.0, p_group * w1, jnp.where(lane == 1.0, p_group * (t * w1), 0.0))

    hit1 = lane == i1
    hit2 = lane == i2
    sent = jnp.where(hit1 | hit2, 1.0, 0.0)
    before = jnp.dot(stril_ref[...], sent.astype(jnp.bfloat16), preferred_element_type=F32) + carry[0:1, :]
    r1 = jnp.sum(jnp.where(hit1, before, 0.0), axis=-1, keepdims=True)
    r2 = jnp.sum(jnp.where(hit2, before, 0.0), axis=-1, keepdims=True)
    total = before[tm - 1:tm, :] + sent[tm - 1:tm, :]
    carry[...] = jnp.broadcast_to(total, carry.shape)
    cnt_ref[...] = jnp.broadcast_to(total, cnt_ref.shape).astype(jnp.int32)
    ids = jnp.where(lane == 0.0, i1 - MOE_GROUPS, jnp.where(lane == 1.0, i2 - MOE_GROUPS,
                    jnp.where(lane == 2.0, r1, jnp.where(lane == 3.0, r2, 0.0))))
    id_ref[...] = ids.T[0:SUBLANES, :].astype(jnp.int32)


def _router(h, w_router, b_router):
    tok = h.shape[0]
    tm = TM_ROUTER
    row = lambda i: (i, 0)
    idx = np.arange(tm)
    stril = jnp.asarray((idx[:, None] > idx[None, :]).astype(np.float32), jnp.bfloat16)
    w_hi = w_router.astype(MXU_DTYPE)
    w_lo = (w_router - w_hi.astype(F32)).astype(MXU_DTYPE)
    return pl.pallas_call(
        _router_kernel,
        out_shape=(jax.ShapeDtypeStruct((SUBLANES, tok), jnp.int32), jax.ShapeDtypeStruct((tok, LANES), F32),
                   jax.ShapeDtypeStruct((SUBLANES, LANES), jnp.int32)),
        grid=(tok // tm,),
        in_specs=[pl.BlockSpec((tm, D_MODEL), row), _full((D_MODEL, 2 * LANES)), _full((1, LANES)),
                  _full((tm, tm))],
        out_specs=(pl.BlockSpec((SUBLANES, tm), lambda i: (0, i)), pl.BlockSpec((tm, LANES), row),
                   _full((SUBLANES, LANES))),
        scratch_shapes=[pltpu.VMEM((SUBLANES, LANES), F32)],
        compiler_params=_params("arbitrary"),
        name="moe_router",
    )(h, jnp.concatenate([w_hi, w_lo], axis=1), b_router, stril)


def _pad_pieces(n):
    return tuple(1 << b for b in reversed(range((n - 1).bit_length())))


def _dispatch_kernel(d0_ref, d1_ref, start_ref, cnt_ref, nused_ref, h_ref, xs_hbm, zeros, sem, pad_sem):
    tm = h_ref.shape[0]
    i = pl.program_id(0)
    base = i * tm

    def issue(r, c):
        t = base + r
        pltpu.make_async_copy(h_ref.at[r], xs_hbm.at[d0_ref[t]], sem).start(priority=0)
        pltpu.make_async_copy(h_ref.at[r], xs_hbm.at[d1_ref[t]], sem).start(priority=1)
        return c

    lax.fori_loop(0, tm, issue, 0, unroll=8)

    def pad_copies(e, fn):
        cnt = cnt_ref[e]
        n_pad = (MOE_ROWS - cnt % MOE_ROWS) % MOE_ROWS
        first = start_ref[e] + cnt
        for piece in _pad_pieces(MOE_ROWS):
            @pl.when((n_pad & piece) != 0)
            def _(piece=piece):
                off = first + (n_pad & ~(2 * piece - 1))
                fn(pltpu.make_async_copy(zeros.at[pl.ds(0, piece)], xs_hbm.at[pl.ds(off, piece)], pad_sem))

    def tail_copies(blk, fn):
        for part in range(MOE_ROWS // zeros.shape[0]):
            off = blk * MOE_ROWS + part * zeros.shape[0]
            fn(pltpu.make_async_copy(zeros, xs_hbm.at[pl.ds(off, zeros.shape[0])], pad_sem))

    @pl.when(i == 0)
    def _():
        zeros[...] = jnp.zeros(zeros.shape, F32)
        n_blocks = xs_hbm.shape[0] // MOE_ROWS

        def start(e, c):
            pad_copies(e, lambda cp: cp.start())
            return c

        def wait(e, c):
            pad_copies(e, lambda cp: cp.wait())
            return c

        def tail_start(blk, c):
            tail_copies(blk, lambda cp: cp.start())
            return c

        def tail_wait(blk, c):
            tail_copies(blk, lambda cp: cp.wait())
            return c

        lax.fori_loop(0, N_EXPERTS, start, 0)
        lax.fori_loop(nused_ref[0], n_blocks, tail_start, 0)
        lax.fori_loop(0, N_EXPERTS, wait, 0)
        lax.fori_loop(nused_ref[0], n_blocks, tail_wait, 0)

    for _ in range(TOP_K):
        pltpu.make_async_copy(h_ref, xs_hbm.at[pl.ds(0, tm)], sem).wait()


def _dispatch(h_tiles, dest0, dest1, pad_start, counts, n_used, n_slots):
    tok = h_tiles.shape[0]
    tm = TM_DISPATCH
    assert tok % tm == 0, (tok, tm)
    grid_spec = pltpu.PrefetchScalarGridSpec(
        num_scalar_prefetch=5,
        grid=(tok // tm,),
        in_specs=[pl.BlockSpec((tm,) + ROW_TILE, lambda i, *_: (i, 0, 0))],
        out_specs=pl.BlockSpec(memory_space=pl.ANY),
        scratch_shapes=[pltpu.VMEM((MOE_ROWS // 2,) + ROW_TILE, F32),
                        pltpu.SemaphoreType.DMA(()), pltpu.SemaphoreType.DMA(())],
    )
    return pl.pallas_call(
        _dispatch_kernel,
        out_shape=jax.ShapeDtypeStruct((n_slots,) + ROW_TILE, F32),
        grid_spec=grid_spec,
        compiler_params=_params("arbitrary"),
        name="moe_dispatch",
    )(dest0, dest1, pad_start, counts, n_used, h_tiles)


def _expert_kernel(be_ref, nused_ref, next_ref, slot_ref, x_ref, wg_hbm, wu_hbm, wd_hbm, out_ref,
                   wg_s, wu_s, wd_s, wg_buf, wu_buf, wd_buf, sem, *, layer):
    i = pl.program_id(0)

    def weight_copies(e, slot):
        w = layer * N_EXPERTS + e
        return (pltpu.make_async_copy(wg_hbm.at[w], wg_buf.at[slot], sem.at[slot]),
                pltpu.make_async_copy(wu_hbm.at[w], wu_buf.at[slot], sem.at[slot]),
                pltpu.make_async_copy(wd_hbm.at[w], wd_buf.at[slot], sem.at[slot]))

    @pl.when(i == 0)
    def _():
        for cp in weight_copies(be_ref[0], slot_ref[be_ref[0]]):
            cp.start()

    @pl.when(i < nused_ref[0])
    def _():
        @pl.when((i == 0) | (be_ref[i] != be_ref[jnp.maximum(i - 1, 0)]))
        def _():
            e = be_ref[i]
            slot = slot_ref[e]
            for cp in weight_copies(e, slot):
                cp.wait()

            @pl.when(next_ref[e] >= 0)
            def _():
                for cp in weight_copies(next_ref[e], 1 - slot):
                    cp.start()

            wg_s[...] = wg_buf[slot].astype(MXU_DTYPE)
            wu_s[...] = wu_buf[slot].astype(MXU_DTYPE)
            wd_s[...] = wd_buf[slot].astype(MXU_DTYPE)

        x = _tiles_to_rows(x_ref).astype(MXU_DTYPE)
        n = x.shape[0]
        hidden = []
        for c in range(0, D_EXPERT, FFN_COLS):
            gate = jnp.dot(x, wg_s[:, c:c + FFN_COLS], preferred_element_type=F32)
            up = jnp.dot(x, wu_s[:, c:c + FFN_COLS], preferred_element_type=F32)
            hidden.append((gate * _sigmoid(gate) * up).astype(MXU_DTYPE))
        hidden = jnp.concatenate(hidden, axis=1)
        for c in range(0, D_MODEL, FFN_COLS):
            out = jnp.dot(hidden, wd_s[:, c:c + FFN_COLS], preferred_element_type=F32)
            for s in range(FFN_COLS // LANES):
                out_ref[pl.ds(c // LANES + s, n, stride=ROW_TILE[0]), :] = out[:, s * LANES:(s + 1) * LANES]

    @pl.when(i >= nused_ref[0])
    def _():
        out_ref[...] = jnp.zeros(out_ref.shape, F32)


def _expert_ffn(xs_tiles, block_expert, n_used, counts, w_gate, w_up, w_down, layer):
    n_slots = xs_tiles.shape[0]
    n_blocks = block_expert.shape[0]
    rows = MOE_ROWS * ROW_TILE[0]
    experts = jnp.arange(N_EXPERTS, dtype=jnp.int32)
    nonempty = counts > 0
    later = (experts[None, :] > experts[:, None]) & nonempty[None, :]
    next_expert = jnp.min(jnp.where(later, experts[None, :], N_EXPERTS), axis=1)
    next_expert = jnp.where(next_expert < N_EXPERTS, next_expert, -1).astype(jnp.int32)
    buf_half = ((jnp.cumsum(nonempty.astype(jnp.int32)) - 1) % 2).astype(jnp.int32)
    blk = lambda i, be, nu, nx, sl: (jnp.maximum(jnp.minimum(i, nu[0] - 1), 0), 0)
    grid_spec = pltpu.PrefetchScalarGridSpec(
        num_scalar_prefetch=4,
        grid=(n_blocks,),
        in_specs=[
            pl.BlockSpec((rows, LANES), blk),
            pl.BlockSpec(memory_space=pl.ANY),
            pl.BlockSpec(memory_space=pl.ANY),
            pl.BlockSpec(memory_space=pl.ANY),
        ],
        out_specs=pl.BlockSpec((rows, LANES), lambda i, be, nu, nx, sl: (i, 0)),
        scratch_shapes=[pltpu.VMEM((D_MODEL, D_EXPERT), MXU_DTYPE), pltpu.VMEM((D_MODEL, D_EXPERT), MXU_DTYPE),
                        pltpu.VMEM((D_EXPERT, D_MODEL), MXU_DTYPE),
                        pltpu.VMEM((2, D_MODEL, D_EXPERT), F32), pltpu.VMEM((2, D_MODEL, D_EXPERT), F32),
                        pltpu.VMEM((2, D_EXPERT, D_MODEL), F32), pltpu.SemaphoreType.DMA((2,))],
    )
    out = pl.pallas_call(
        functools.partial(_expert_kernel, layer=layer),
        out_shape=jax.ShapeDtypeStruct((n_slots * ROW_TILE[0], LANES), F32),
        grid_spec=grid_spec,
        compiler_params=_params("arbitrary"),
        name="moe_experts",
    )(block_expert, n_used, next_expert, buf_half, xs_tiles.reshape(n_slots * ROW_TILE[0], LANES),
      w_gate, w_up, w_down)
    return out.reshape((n_slots,) + ROW_TILE)


def _combine_ln_kernel(d0_ref, d1_ref, rows_hbm, h_ref, gate_ref, g_ref, b_ref, o_ref, buf, sem):
    tm = h_ref.shape[0]
    i = pl.program_id(0)

    def gather(tile, p):
        base = tile * tm

        def issue(r, c):
            t = base + r
            dst = pl.ds(pl.multiple_of(r * ROW_TILE[0], ROW_TILE[0]), ROW_TILE[0])
            pltpu.make_async_copy(rows_hbm.at[d0_ref[t]], buf.at[p, 0, dst],
                                  sem.at[p]).start(priority=0)
            pltpu.make_async_copy(rows_hbm.at[d1_ref[t]], buf.at[p, 1, dst],
                                  sem.at[p]).start(priority=1)
            return c

        lax.fori_loop(0, tm, issue, 0, unroll=8)

    @pl.when(i == 0)
    def _():
        gather(0, 0)

    for p in range(2):
        @pl.when(i % 2 == p)
        def _(p=p):
            @pl.when(i + 1 < pl.num_programs(0))
            def _():
                gather(i + 1, 1 - p)

            for k in range(TOP_K):
                pltpu.make_async_copy(buf.at[1 - p, k], buf.at[p, k], sem.at[p]).wait()
            gate = gate_ref[...]
            acc = (ALPHA * h_ref[...] + gate[:, 0:1] * _tiles_to_rows(buf.at[p, 0])
                   + gate[:, 1:2] * _tiles_to_rows(buf.at[p, 1]))
            o_ref[...] = _layer_norm_rows(acc, g_ref[...], b_ref[...])


def _combine_ln(dest0, dest1, rows_tiles, h, gates, g_row, b_row):
    tok = h.shape[0]
    tm = TM_COMBINE
    row = lambda i, *_: (i, 0)
    const = lambda i, *_: (0, 0)
    grid_spec = pltpu.PrefetchScalarGridSpec(
        num_scalar_prefetch=2,
        grid=(tok // tm,),
        in_specs=[
            pl.BlockSpec(memory_space=pl.ANY),
            pl.BlockSpec((tm, D_MODEL), row),
            pl.BlockSpec((tm, LANES), row),
            pl.BlockSpec((1, D_MODEL), const),
            pl.BlockSpec((1, D_MODEL), const),
        ],
        out_specs=pl.BlockSpec((tm, D_MODEL), row),
        scratch_shapes=[pltpu.VMEM((2, TOP_K, tm * ROW_TILE[0], LANES), F32), pltpu.SemaphoreType.DMA((2,))],
    )
    return pl.pallas_call(
        _combine_ln_kernel,
        out_shape=jax.ShapeDtypeStruct((tok, D_MODEL), F32),
        grid_spec=grid_spec,
        compiler_params=_params("arbitrary"),
        name="moe_combine_ln",
    )(dest0, dest1, rows_tiles, h, gates, g_row, b_row)


def _row_slots_kernel(start_ref, ids_ref, out_ref):
    expert = ids_ref[0:TOP_K, :]
    start = jnp.zeros(expert.shape, jnp.int32)
    for e in range(N_EXPERTS):
        start = jnp.where(expert == e, start_ref[e], start)
    out_ref[...] = jnp.zeros(out_ref.shape, jnp.int32)
    out_ref[0:TOP_K, :] = start + ids_ref[TOP_K:2 * TOP_K, :]


def _row_slots(ids, pad_start):
    grid_spec = pltpu.PrefetchScalarGridSpec(
        num_scalar_prefetch=1,
        grid=(1,),
        in_specs=[pl.BlockSpec(ids.shape, lambda i, s: (0, 0))],
        out_specs=pl.BlockSpec(ids.shape, lambda i, s: (0, 0)),
    )
    return pl.pallas_call(
        _row_slots_kernel,
        out_shape=jax.ShapeDtypeStruct(ids.shape, jnp.int32),
        grid_spec=grid_spec,
        name="moe_row_slots",
    )(pad_start, ids)


def _slot_layout(counts, n_blocks):
    padded = (counts + MOE_ROWS - 1) // MOE_ROWS * MOE_ROWS
    pad_end = jnp.cumsum(padded)
    n_used = pad_end[-1:] // MOE_ROWS
    blocks = jnp.arange(n_blocks, dtype=jnp.int32)
    first_row = jnp.minimum(blocks, n_used - 1) * MOE_ROWS
    block_expert = jnp.sum(first_row[:, None] >= pad_end[None, :], axis=1)
    return (pad_end - padded).astype(jnp.int32), block_expert.astype(jnp.int32), n_used.astype(jnp.int32)


def _moe_ln(h, h_tiles, w_router, b_router, w_gate, w_up, w_down, layer, g_row, b_row):
    tok = h.shape[0]
    n_blocks = tok * TOP_K // MOE_ROWS + N_EXPERTS
    ids, gates, counts = _router(h, w_router, b_router)
    counts = counts[0, MOE_GROUPS:MOE_GROUPS + N_EXPERTS]
    pad_start, block_expert, n_used = _slot_layout(counts, n_blocks)
    slots = _row_slots(ids, pad_start)
    dest0, dest1 = slots[0], slots[1]
    xs = _dispatch(h_tiles.reshape((tok,) + ROW_TILE), dest0, dest1, pad_start, counts, n_used,
                   n_blocks * MOE_ROWS)
    rows = _expert_ffn(xs, block_expert, n_used, counts, w_gate, w_up, w_down, layer)
    return _combine_ln(dest0, dest1, rows, h, gates, g_row, b_row)


def _pad_lanes(v):
    return jnp.pad(v, (0, LANES - v.shape[0])).reshape(1, LANES)


def _even_mixer(h, bsz, seq, w_in, conv_a, conv_w, conv_b, dt_bias, a_log, d_skip, norm_w, w_out, g_row, b_row):
    w = jnp.pad(w_in, ((0, 0), (0, AB_PROJ - w_in.shape[1]))).astype(MXU_DTYPE)
    y_a, z, xbc, dt, acs, acst = _even_front(
        h, w, conv_a, conv_w, conv_b.reshape(1, -1), _pad_lanes(dt_bias), _pad_lanes(-jnp.exp(a_log)), bsz, seq)
    dskip_row = jnp.repeat(d_skip, SSM_HEAD_DIM).reshape(1, -1)
    y_b = _ssd(xbc, dt, acs, acst, z, dskip_row, norm_w.reshape(1, -1), bsz, seq)
    return _outproj_ln([y_a, y_b], h, w_out.astype(MXU_DTYPE), g_row, b_row)


def _odd_mixer(h, bsz, seq, w_in, i_bias, f_bias, hnorm_w, fox_f_bias, w_out, g_row, b_row):
    c = np.cumsum((0, MLSTM_W, MLSTM_W, MLSTM_W, MLSTM_HEADS, MLSTM_HEADS, MLSTM_W, FOX_W, FOX_W, FOX_W, FOX_HEADS))
    part = lambda j: w_in[:, c[j]:c[j + 1]]
    q, k, v, i_pre, f_pre, o_pre, fq, fk, fv, ff = (part(j) for j in range(10))
    gate_cols = jnp.concatenate([i_pre, f_pre, ff, f_pre], axis=1)
    gate_cols = jnp.pad(gate_cols, ((0, 0), (0, LANES - gate_cols.shape[1])))
    spread = lambda m: jnp.pad(m.reshape(-1, FOX_HEADS, FOX_HEAD_DIM),
                               ((0, 0), (0, 0), (0, LANES - FOX_HEAD_DIM))).reshape(-1, FOX_AUG)
    w = jnp.concatenate([q, k, v, o_pre, fq, spread(fk), fv, gate_cols], axis=1).astype(MXU_DTYPE)
    gate_bias = _pad_lanes(jnp.concatenate([i_bias, f_bias, fox_f_bias, f_bias]))
    q, k, v, o, fq, fk_aug, fvt_aug, gates, gates_t = _odd_front(h, w, gate_bias, bsz, seq)
    y_c = _mlstm(q, k, v, o, gates, gates_t, hnorm_w.reshape(1, -1), bsz, seq)
    y_d = _fox(fq, fk_aug, fvt_aug, gates_t, bsz, seq)
    return _outproj_ln([y_c, y_d], h, w_out.astype(MXU_DTYPE), g_row, b_row)


def kernel(x, ab_w_in, ab_conv_a, ab_conv_ssm_w, ab_conv_ssm_b, ab_dt_bias, ab_a_log, ab_d_skip, ab_norm_w, ab_w_out, cd_w_in, cd_i_bias, cd_f_bias, cd_hnorm_w, cd_fox_f_bias, cd_w_out, ln1_g, ln1_b, ln2_g, ln2_b, moe_rg_w, moe_rg_b, moe_re_w, moe_re_b, moe_w_gate, moe_w_up, moe_w_down):
    bsz, seq, d = x.shape
    h = x.reshape(bsz * seq, d)
    stack = lambda w: w.reshape((w.shape[0] * w.shape[1],) + w.shape[2:])
    w_gate, w_up, w_down = stack(moe_w_gate), stack(moe_w_up), stack(moe_w_down)
    for layer in range(DEPTH):
        j = layer // 2
        g1, b1 = ln1_g[layer].reshape(1, -1), ln1_b[layer].reshape(1, -1)
        if layer % 2 == 0:
            h, h_tiles = _even_mixer(h, bsz, seq, ab_w_in[j], ab_conv_a[j], ab_conv_ssm_w[j], ab_conv_ssm_b[j],
                                     ab_dt_bias[j], ab_a_log[j], ab_d_skip[j], ab_norm_w[j], ab_w_out[j], g1, b1)
        else:
            h, h_tiles = _odd_mixer(h, bsz, seq, cd_w_in[j], cd_i_bias[j], cd_f_bias[j], cd_hnorm_w[j],
                                    cd_fox_f_bias[j], cd_w_out[j], g1, b1)
        re_w = jnp.transpose(moe_re_w[layer], (1, 0, 2)).reshape(d, N_EXPERTS)
        w_router = jnp.pad(jnp.concatenate([moe_rg_w[layer], re_w], axis=1),
                           ((0, 0), (0, LANES - MOE_GROUPS - N_EXPERTS)))
        b_router = _pad_lanes(jnp.concatenate([moe_rg_b[layer], moe_re_b[layer].reshape(-1)]))
        h = _moe_ln(h, h_tiles, w_router, b_router, w_gate, w_up, w_down, layer,
                    ln2_g[layer].reshape(1, -1), ln2_b[layer].reshape(1, -1))
    return h.reshape(bsz, seq, d)
```

```python
import functools

import numpy as np
import jax
import jax.numpy as jnp
from jax import lax
from jax.experimental import pallas as pl
from jax.experimental.pallas import tpu as pltpu

F32 = jnp.float32
MXU_DTYPE = jnp.bfloat16

D_MODEL = 1024
DEPTH = 4
ALPHA = (2 * DEPTH) ** 0.25
LN_EPS = 1e-5
CONV_DIM = D_MODEL // 2
CONV_WIDTH = 3
SSM_D_INNER = D_MODEL
SSM_HEAD_DIM = 64
SSM_HEADS = SSM_D_INNER // SSM_HEAD_DIM
SSM_GROUPS = 4
SSM_STATE = 64
SSM_CONV = 4
SSM_BC = SSM_GROUPS * SSM_STATE
SSM_CONV_DIM = SSM_D_INNER + 2 * SSM_BC
MLSTM_HEADS = 4
MLSTM_HEAD_DIM = D_MODEL // 8
MLSTM_W = MLSTM_HEADS * MLSTM_HEAD_DIM
FOX_HEADS = 8
FOX_HEAD_DIM = D_MODEL // 16
FOX_W = FOX_HEADS * FOX_HEAD_DIM
MOE_GROUPS = 4
EXPERTS_PER_GROUP = 8
N_EXPERTS = MOE_GROUPS * EXPERTS_PER_GROUP
TOP_K = 2
D_EXPERT = D_MODEL // 2

LANES = 128
SUBLANES = 8
VMEM_LIMIT_BYTES = 56 * 1024 * 1024

CHUNK = 128
SEQ_PAIR = 4
MLSTM_SEQS = 1
MLSTM_CHUNK = 512
TM_FRONT = 512
TM_EVEN_FRONT = 256
CONV_ROWS, CONV_LANES = 32, 512
TM_OUT = 1024
TM_ROUTER = 512
TQ_FOX = 256
MOE_ROWS = 512
FFN_COLS = 256
TM_COMBINE = 512
TM_DISPATCH = 2048

AB_PROJ = 4224
FOX_AUG = FOX_HEADS * LANES
CD_FK = 5 * 512
CD_FV = CD_FK + FOX_AUG
CD_GATES = CD_FV + FOX_W
CD_PROJ = CD_GATES + LANES
LOG2E = 1.4426950408889634
TK_FOX = 128
FOX_ACC_ROWS = FOX_HEAD_DIM + SUBLANES
FOX_VROWS = LANES
G_I, G_F, G_FOX, G_BCUM = 0, 4, 8, 16


def _params(*sem):
    return pltpu.CompilerParams(dimension_semantics=sem, vmem_limit_bytes=VMEM_LIMIT_BYTES)


def _softplus(x):
    return jnp.maximum(x, 0.0) + jnp.log(1.0 + jnp.exp(-jnp.abs(x)))


def _sigmoid(x):
    return 1.0 / (1.0 + jnp.exp(-x))


def _layer_norm_rows(v, g, b):
    mu = jnp.mean(v, axis=-1, keepdims=True)
    c = v - mu
    var = jnp.mean(c * c, axis=-1, keepdims=True)
    return c * lax.rsqrt(var + LN_EPS) * g + b


def _tril(n, block):
    i = np.arange(n)
    m = (i[:, None] >= i[None, :]) & (i[:, None] // block == i[None, :] // block)
    return jnp.asarray(m.astype(np.float32), MXU_DTYPE)


def _full(shape):
    return pl.BlockSpec(shape, lambda *_: (0,) * len(shape), pipeline_mode=pl.Buffered(1))


ROW_TILE = (D_MODEL // LANES, LANES)


def _tiles_to_rows(ref):
    n = ref.shape[0] // ROW_TILE[0]
    return jnp.concatenate([ref[pl.ds(s, n, stride=ROW_TILE[0]), :] for s in range(ROW_TILE[0])], axis=1)


def _rows_to_tiles(ref, val):
    n = val.shape[0]
    for s in range(ROW_TILE[0]):
        ref[pl.ds(s, n, stride=ROW_TILE[0]), :] = val[:, s * LANES:(s + 1) * LANES]


def _split3(x):
    narrow = lambda v: v.astype(MXU_DTYPE).astype(F32)
    x1 = narrow(x)
    x2 = narrow(x - x1)
    return x1, x2, narrow(x - x1 - x2)


def _cumsum_rows(tril, x):
    parts = jnp.dot(tril, jnp.concatenate(_split3(x), axis=1).astype(MXU_DTYPE), preferred_element_type=F32)
    return parts[:, 0:LANES] + parts[:, LANES:2 * LANES] + parts[:, 2 * LANES:3 * LANES]


def _even_front_kernel(h_ref, w_ref, ca_ref, cw_ref, cb_ref, dtb_ref, aneg_ref, tril_ref,
                       ya_ref, z_ref, xbc_ref, dt_ref, acs_ref, acst_ref,
                       proj_buf, ua_ext, xbc_ext, *, tiles_per_seq):
    tm = h_ref.shape[0]
    i = pl.program_id(0)
    cur = i % 2

    @pl.when(i == 0)
    def _():
        proj_buf[1] = jnp.zeros(proj_buf.shape[1:], F32)

    @pl.when((i == 0) | ((i - 1) % tiles_per_seq == 0))
    def _():
        ua_ext[0:SUBLANES, :] = jnp.zeros((SUBLANES, CONV_DIM), F32)
        xbc_ext[0:SUBLANES, :] = jnp.zeros((SUBLANES, SSM_CONV_DIM), F32)

    z0 = 3 * CONV_DIM
    x0 = z0 + SSM_D_INNER
    d0 = x0 + SSM_CONV_DIM

    def step(p):
        x_in = h_ref[...].astype(MXU_DTYPE)
        proj = proj_buf.at[1 - p]
        z_ref[...] = proj[:, z0:x0]

        n_blocks = tm // CONV_ROWS
        col_cuts = [AB_PROJ * g // n_blocks // LANES * LANES for g in range(n_blocks)] + [AB_PROJ]
        for r0 in range(0, tm, CONV_ROWS):
            c_lo, c_hi = col_cuts[r0 // CONV_ROWS], col_cuts[r0 // CONV_ROWS + 1]
            proj_buf[p, :, c_lo:c_hi] = jnp.dot(x_in, w_ref[:, c_lo:c_hi], preferred_element_type=F32)
            rows = slice(r0, r0 + CONV_ROWS)
            ext_rows = slice(SUBLANES + r0, SUBLANES + r0 + CONV_ROWS)
            ua_ext[ext_rows, :] = proj[rows, CONV_DIM:2 * CONV_DIM] * proj[rows, 2 * CONV_DIM:3 * CONV_DIM]
            conv = None
            for k in range(CONV_WIDTH):
                tap = ca_ref[k:k + 1, :] * ua_ext[pl.ds(SUBLANES + r0 - (CONV_WIDTH - 1) + k, CONV_ROWS), :]
                conv = tap if conv is None else conv + tap
            ya_ref[rows, :] = (proj[rows, 0:CONV_DIM] * conv).astype(ya_ref.dtype)
            for c0 in range(0, SSM_CONV_DIM, CONV_LANES):
                cols = slice(c0, c0 + CONV_LANES)
                xbc_ext[ext_rows, cols] = proj[rows, x0 + c0:x0 + c0 + CONV_LANES]
                conv = cb_ref[:, cols]
                for k in range(SSM_CONV):
                    conv = conv + (cw_ref[k:k + 1, cols]
                                   * xbc_ext[pl.ds(SUBLANES + r0 - (SSM_CONV - 1) + k, CONV_ROWS), cols])
                xbc_ref[rows, cols] = conv * _sigmoid(conv)
        ua_ext[0:SUBLANES, :] = ua_ext[tm:tm + SUBLANES, :]
        xbc_ext[0:SUBLANES, :] = xbc_ext[tm:tm + SUBLANES, :]

        dt = _softplus(proj[:, d0:d0 + LANES] + dtb_ref[...])
        a = dt * aneg_ref[...]
        acs = _cumsum_rows(tril_ref[...], a)
        dt_ref[...] = dt
        acs_ref[...] = acs
        acst_ref[...] = acs.T

    for parity in range(2):
        @pl.when(cur == parity)
        def _(parity=parity):
            step(parity)


def _even_front(h, w_in, conv_a, conv_w, conv_b, dt_bias_row, aneg_row, bsz, seq):
    tok = bsz * seq
    tm = TM_EVEN_FRONT
    n_tiles = tok // tm
    row = lambda i: (jnp.maximum(i - 1, 0), 0)
    out_shapes = (
        jax.ShapeDtypeStruct((tok, CONV_DIM), MXU_DTYPE),
        jax.ShapeDtypeStruct((tok, SSM_D_INNER), F32),
        jax.ShapeDtypeStruct((tok, SSM_CONV_DIM), F32),
        jax.ShapeDtypeStruct((tok, LANES), F32),
        jax.ShapeDtypeStruct((tok, LANES), F32),
        jax.ShapeDtypeStruct((LANES, tok), F32),
    )
    return pl.pallas_call(
        functools.partial(_even_front_kernel, tiles_per_seq=seq // tm),
        out_shape=out_shapes,
        grid=(n_tiles + 1,),
        in_specs=[
            pl.BlockSpec((tm, D_MODEL), lambda i: (jnp.minimum(i, n_tiles - 1), 0)),
            _full((D_MODEL, AB_PROJ)),
            _full((CONV_WIDTH, CONV_DIM)),
            _full((SSM_CONV, SSM_CONV_DIM)),
            _full((1, SSM_CONV_DIM)),
            _full((1, LANES)),
            _full((1, LANES)),
            _full((tm, tm)),
        ],
        out_specs=(
            pl.BlockSpec((tm, CONV_DIM), row),
            pl.BlockSpec((tm, SSM_D_INNER), row),
            pl.BlockSpec((tm, SSM_CONV_DIM), row),
            pl.BlockSpec((tm, LANES), row),
            pl.BlockSpec((tm, LANES), row),
            pl.BlockSpec((LANES, tm), lambda i: (0, jnp.maximum(i - 1, 0))),
        ),
        scratch_shapes=[
            pltpu.VMEM((2, tm, AB_PROJ), F32),
            pltpu.VMEM((tm + SUBLANES, CONV_DIM), F32),
            pltpu.VMEM((tm + SUBLANES, SSM_CONV_DIM), F32),
        ],
        compiler_params=_params("arbitrary"),
        name="even_front",
    )(h, w_in, conv_a, conv_w, conv_b, dt_bias_row, aneg_row, _tril(tm, CHUNK))


def _bcast_heads(arr, n_heads, width):
    per = LANES // width
    length = arr.shape[0]
    lane = lax.broadcasted_iota(jnp.int32, (length, LANES), 1)
    outs = []
    for j in range(n_heads // per):
        v = jnp.broadcast_to(arr[:, j * per:j * per + 1], (length, LANES))
        for r in range(1, per):
            v = jnp.where(lane >= r * width, jnp.broadcast_to(arr[:, j * per + r:j * per + r + 1], (length, LANES)), v)
        outs.append(v)
    return jnp.concatenate(outs, axis=1)


def _expand_heads(arr, expand_ref):
    hi, lo, _ = _split3(arr)
    return jnp.dot(jnp.concatenate([hi, lo], axis=1).astype(MXU_DTYPE), expand_ref[...],
                   preferred_element_type=F32)


def _ssd_kernel(*refs):
    nb = SEQ_PAIR
    xbc_ref, dt_ref, acs_ref, z_ref = refs[:4]
    acst_refs = refs[4:4 + nb]
    dskip_ref, nw_ref, expand_ref, y_ref, state = refs[4 + nb:]
    L = CHUNK
    P = SSM_HEAD_DIM
    R = SSM_HEADS // SSM_GROUPS
    GW = R * P

    @pl.when(pl.program_id(1) == 0)
    def _():
        state[...] = jnp.zeros(state.shape, F32)

    row = lax.broadcasted_iota(jnp.int32, (L, L), 0)
    col = lax.broadcasted_iota(jnp.int32, (L, L), 1)
    causal = row >= col
    lane_g = lax.broadcasted_iota(jnp.int32, (L, GW), 1)

    per_row = []
    for j in range(nb):
        acs = acs_ref[j]
        per_row += [dt_ref[j], jnp.exp(acs[L - 1:L, :] - acs), jnp.exp(acs)]
    expanded = _expand_heads(jnp.concatenate(per_row, axis=0), expand_ref)

    st_old = [state[c] for c in range(nb * SSM_GROUPS)]
    st_new = []
    for j in range(nb):
        xs = xbc_ref[j, :, 0:SSM_D_INNER]
        bm = xbc_ref[j, :, SSM_D_INNER:SSM_D_INNER + SSM_BC]
        cm = xbc_ref[j, :, SSM_D_INNER + SSM_BC:SSM_CONV_DIM]
        acs = acs_ref[j]
        acst = acst_refs[j][...]
        a_last = acs[L - 1:L, :]
        dtx = expanded[(3 * j) * L:(3 * j + 1) * L, :]
        decx = expanded[(3 * j + 1) * L:(3 * j + 2) * L, :]
        expx = expanded[(3 * j + 2) * L:(3 * j + 3) * L, :]
        xdt = xs * dtx
        xdec = (xdt * decx).astype(MXU_DTYPE)
        xdt_m = xdt.astype(MXU_DTYPE)
        chunk_decay = jnp.exp(jnp.broadcast_to(a_last, (SUBLANES, LANES)))
        cdx = _bcast_heads(chunk_decay, SSM_HEADS, P)[0:1, :]

        bm_t = bm.T.astype(MXU_DTYPE)
        cm_m = cm.astype(MXU_DTYPE)
        bm_m = bm.astype(MXU_DTYPE)

        ys = []
        for g in range(SSM_GROUPS):
            n0 = g * SSM_STATE
            c_g = cm_m[:, n0:n0 + SSM_STATE]
            cb = lax.dot_general(c_g, bm_m[:, n0:n0 + SSM_STATE], (((1,), (1,)), ((), ())),
                                 preferred_element_type=F32)
            ms = []
            for r in range(R):
                hd = g * R + r
                seg = jnp.exp(jnp.where(causal, acs[:, hd:hd + 1] - acst[hd:hd + 1, :], -jnp.inf))
                ms.append((cb * seg).astype(MXU_DTYPE))
            big = jnp.dot(jnp.concatenate(ms, axis=0), xdt_m[:, g * GW:(g + 1) * GW],
                          preferred_element_type=F32)
            y_diag = big[0:L, :]
            for r in range(1, R):
                y_diag = jnp.where(lane_g >= r * P, big[r * L:(r + 1) * L, :], y_diag)
            st = st_old[j * SSM_GROUPS + g]
            y_off = jnp.dot(c_g, st.astype(MXU_DTYPE), preferred_element_type=F32)
            new = jnp.dot(bm_t[n0:n0 + SSM_STATE, :], xdec[:, g * GW:(g + 1) * GW],
                          preferred_element_type=F32)
            st_new.append(st * cdx[:, g * GW:(g + 1) * GW] + new)
            ys.append(y_diag + y_off * expx[:, g * GW:(g + 1) * GW])
        y = jnp.concatenate(ys, axis=1) + dskip_ref[...] * xs
        z = z_ref[j]
        u = y * (z * _sigmoid(z))
        y = u * lax.rsqrt(jnp.mean(u * u, axis=-1, keepdims=True) + LN_EPS) * nw_ref[...]
        y_ref[j] = y.astype(y_ref.dtype)

    for c in range(nb * SSM_GROUPS):
        state[c] = st_new[c]


def _ssd(xbc, dt, acs, acst, z, dskip_row, normw_row, bsz, seq):
    nb = SEQ_PAIR
    nc = seq // CHUNK
    assert bsz % nb == 0 and seq % CHUNK == 0, (bsz, seq)
    per_seq = lambda a: a.reshape(bsz, seq, a.shape[-1])
    blk = lambda width: pl.BlockSpec((nb, CHUNK, width), lambda g, c: (g, c, 0))
    acst_specs = [pl.BlockSpec((LANES, CHUNK), lambda g, c, j=j: (0, (g * nb + j) * nc + c)) for j in range(nb)]
    expand = np.zeros((LANES, SSM_D_INNER), np.float32)
    for hd in range(SSM_HEADS):
        expand[hd, hd * SSM_HEAD_DIM:(hd + 1) * SSM_HEAD_DIM] = 1.0
    expand = jnp.asarray(np.concatenate([expand] * 2, axis=0), MXU_DTYPE)
    y = pl.pallas_call(
        _ssd_kernel,
        out_shape=jax.ShapeDtypeStruct((bsz, seq, SSM_D_INNER), MXU_DTYPE),
        grid=(bsz // nb, nc),
        in_specs=[blk(SSM_CONV_DIM), blk(LANES), blk(LANES), blk(SSM_D_INNER)] + acst_specs + [
            _full((1, SSM_D_INNER)),
            _full((1, SSM_D_INNER)),
            _full((2 * LANES, SSM_D_INNER)),
        ],
        out_specs=blk(SSM_D_INNER),
        scratch_shapes=[pltpu.VMEM((nb * SSM_GROUPS, SSM_STATE, SSM_D_INNER // SSM_GROUPS), F32)],
        compiler_params=_params("arbitrary", "arbitrary"),
        name="ssd_scan",
    )(per_seq(xbc), per_seq(dt), per_seq(acs), per_seq(z), *([acst] * nb), dskip_row, normw_row, expand)
    return y.reshape(bsz * seq, SSM_D_INNER)


def _outproj_ln_kernel(*refs, widths):
    n = len(widths)
    parts = refs[:n]
    h_ref, w_ref, g_ref, b_ref, o_ref, ot_ref = refs[n:]
    acc = ALPHA * h_ref[...]
    off = 0
    for p, wd in zip(parts, widths):
        acc = acc + jnp.dot(p[...].astype(MXU_DTYPE), w_ref[off:off + wd, :], preferred_element_type=F32)
        off += wd
    out = _layer_norm_rows(acc, g_ref[...], b_ref[...])
    o_ref[...] = out
    _rows_to_tiles(ot_ref, out)


def _outproj_ln(parts, h, w_out, g_row, b_row):
    tok = h.shape[0]
    tm = TM_OUT
    widths = tuple(p.shape[1] for p in parts)
    row = lambda i: (i, 0)
    return pl.pallas_call(
        functools.partial(_outproj_ln_kernel, widths=widths),
        out_shape=(jax.ShapeDtypeStruct((tok, D_MODEL), F32),
                   jax.ShapeDtypeStruct((tok * ROW_TILE[0], LANES), F32)),
        grid=(tok // tm,),
        in_specs=[pl.BlockSpec((tm, wd), row) for wd in widths] + [
            pl.BlockSpec((tm, D_MODEL), row),
            _full((sum(widths), D_MODEL)),
            _full((1, D_MODEL)),
            _full((1, D_MODEL)),
        ],
        out_specs=(pl.BlockSpec((tm, D_MODEL), row), pl.BlockSpec((tm * ROW_TILE[0], LANES), row)),
        compiler_params=_params("arbitrary"),
        name="outproj_ln",
    )(*parts, h, w_out, g_row, b_row)


def _odd_front_kernel(h_ref, w_ref, gb_ref, tril_ref,
                      q_ref, k_ref, v_ref, o_ref, fq_ref, fk_ref, fvt_ref, g_ref, gt_ref, carry):
    tm = h_ref.shape[0]
    x_in = h_ref[...].astype(MXU_DTYPE)
    proj_cols = lambda lo, hi: jnp.dot(x_in, w_ref[:, lo:hi], preferred_element_type=F32)

    @pl.when(pl.program_id(1) == 0)
    def _():
        carry[...] = jnp.zeros(carry.shape, F32)

    raw = proj_cols(CD_GATES, CD_GATES + LANES) + gb_ref[...]
    lane = lax.broadcasted_iota(jnp.int32, (tm, LANES), 1)
    g = jnp.where(lane < G_F, raw, -_softplus(-raw))
    prev = carry[0:1, :]
    glob = _cumsum_rows(tril_ref[...], g) + prev
    before = []
    for c in range(tm // MLSTM_CHUNK):
        before.append(jnp.broadcast_to(prev, (MLSTM_CHUNK, LANES)))
        prev = glob[(c + 1) * MLSTM_CHUNK - 1:(c + 1) * MLSTM_CHUNK, :]
    carry[...] = jnp.broadcast_to(prev, carry.shape)
    local = glob - jnp.concatenate(before, axis=0)
    out = jnp.where((lane >= G_FOX) & (lane < G_BCUM), glob, jnp.where(lane >= G_BCUM, local, g))
    g_ref[...] = out
    gt_ref[...] = out.T

    q_ref[...] = proj_cols(0, 512)
    k_ref[...] = proj_cols(512, 1024) * (MLSTM_HEAD_DIM ** -0.5)
    v_ref[...] = proj_cols(1024, 1536)
    o_ref[...] = _sigmoid(proj_cols(1536, 2048))
    fq_ref[...] = proj_cols(2048, 2560) * (FOX_HEAD_DIM ** -0.5 * LOG2E)

    is_bias = (lane >= FOX_HEAD_DIM) & (lane < FOX_HEAD_DIM + 3)
    fk = proj_cols(CD_FK, CD_FK + FOX_AUG)
    for hd in range(FOX_HEADS):
        c1, c2, c3 = _split3(out[:, G_FOX + hd:G_FOX + hd + 1] * (-LOG2E))
        bias = jnp.where(lane == FOX_HEAD_DIM, c1, jnp.where(lane == FOX_HEAD_DIM + 1, c2, c3))
        k_h = fk[:, hd * LANES:(hd + 1) * LANES]
        fk_ref[:, hd * LANES:(hd + 1) * LANES] = jnp.where(is_bias, bias, k_h).astype(MXU_DTYPE)
    v_t = proj_cols(CD_FV, CD_FV + FOX_W).T
    extra = jnp.where(lax.broadcasted_iota(jnp.int32, (FOX_VROWS - FOX_HEAD_DIM, tm), 0) == 0, 1.0, 0.0)
    fvt_ref[...] = jnp.concatenate(
        [blk for hd in range(FOX_HEADS) for blk in (v_t[hd * FOX_HEAD_DIM:(hd + 1) * FOX_HEAD_DIM, :], extra)],
        axis=0).astype(MXU_DTYPE)


def _odd_front(h, w_in, gate_bias_row, bsz, seq):
    tok = bsz * seq
    tm = TM_FRONT
    ns = seq // tm
    row = lambda b, s: (b * ns + s, 0)
    col = lambda b, s: (0, b * ns + s)
    wide = jax.ShapeDtypeStruct((tok, 512), F32)
    return pl.pallas_call(
        _odd_front_kernel,
        out_shape=(wide,) * 5 + (jax.ShapeDtypeStruct((tok, FOX_AUG), MXU_DTYPE),
                                 jax.ShapeDtypeStruct((FOX_HEADS * FOX_VROWS, tok), MXU_DTYPE),
                                 jax.ShapeDtypeStruct((tok, LANES), F32),
                                 jax.ShapeDtypeStruct((LANES, tok), F32)),
        grid=(bsz, ns),
        in_specs=[
            pl.BlockSpec((tm, D_MODEL), row),
            _full((D_MODEL, CD_PROJ)),
            _full((1, LANES)),
            _full((tm, tm)),
        ],
        out_specs=(pl.BlockSpec((tm, 512), row),) * 5 + (
            pl.BlockSpec((tm, FOX_AUG), row),
            pl.BlockSpec((FOX_HEADS * FOX_VROWS, tm), col),
            pl.BlockSpec((tm, LANES), row),
            pl.BlockSpec((LANES, tm), col),
        ),
        scratch_shapes=[pltpu.VMEM((SUBLANES, LANES), F32)],
        compiler_params=_params("arbitrary", "arbitrary"),
        name="odd_front",
    )(h, w_in, gate_bias_row, _tril(tm, tm))


def _mlstm_kernel(*refs):
    nb = MLSTM_SEQS
    q_ref, k_ref, v_ref, o_ref, g_ref = refs[:5]
    gt_refs = refs[5:5 + nb]
    nw_ref, y_ref, c_state, m_state = refs[5 + nb:]
    L = MLSTM_CHUNK
    DH = MLSTM_HEAD_DIM

    @pl.when(pl.program_id(1) == 0)
    def _():
        c_state[...] = jnp.zeros(c_state.shape, F32)
        m_state[...] = jnp.zeros(m_state.shape, F32)

    row = lax.broadcasted_iota(jnp.int32, (L, L), 0)
    col = lax.broadcasted_iota(jnp.int32, (L, L), 1)
    causal = row >= col
    ones_col = jnp.where(lax.broadcasted_iota(jnp.int32, (L, DH), 1) == 0, 1.0, 0.0)

    for j in range(nb):
        gates = g_ref[j]
        gates_t = gt_refs[j][...]
        for hd in range(MLSTM_HEADS):
            st = j * MLSTM_HEADS + hd
            sl = slice(hd * DH, (hd + 1) * DH)
            q = q_ref[j, :, sl].astype(MXU_DTYPE)
            k = k_ref[j, :, sl]
            v_ext = jnp.concatenate([v_ref[j, :, sl], ones_col], axis=1).astype(MXU_DTYPE)
            b_col = gates[:, G_BCUM + hd:G_BCUM + hd + 1]
            i_col = gates[:, G_I + hd:G_I + hd + 1]
            b_row = gates_t[G_BCUM + hd:G_BCUM + hd + 1, :]
            i_row = gates_t[G_I + hd:G_I + hd + 1, :]
            m_prev = m_state[st:st + 1, 0:1]
            c_ext = c_state[st]

            d_mat = jnp.where(causal, b_col - b_row + i_row, -jnp.inf)
            inter = b_col + m_prev
            m_t = jnp.maximum(jnp.max(d_mat, axis=-1, keepdims=True), inter)
            s_qk = lax.dot_general(q, k.astype(MXU_DTYPE), (((1,), (1,)), ((), ())), preferred_element_type=F32)
            w_qk = s_qk * jnp.exp(d_mat - m_t)
            s_inter = jnp.exp(inter - m_t)
            num_ext = (jnp.dot(w_qk.astype(MXU_DTYPE), v_ext, preferred_element_type=F32)
                       + s_inter * jnp.dot(q, c_ext.astype(MXU_DTYPE), preferred_element_type=F32))
            den = num_ext[:, DH:DH + 1]
            hval = num_ext[:, 0:DH] / jnp.maximum(jnp.abs(den), jnp.exp(-m_t))

            b_last = b_col[L - 1:L, :]
            g_log = b_last - b_col + i_col
            m_new = jnp.maximum(b_last + m_prev, jnp.max(g_log, axis=0, keepdims=True))
            w_k = jnp.exp(g_log - m_new)
            decay = jnp.exp(b_last + m_prev - m_new)
            kw_t = (k * w_k).T.astype(MXU_DTYPE)
            c_state[st] = decay * c_ext + jnp.dot(kw_t, v_ext, preferred_element_type=F32)
            m_state[st:st + 1, :] = jnp.broadcast_to(m_new, (1, LANES))

            mu = jnp.mean(hval, axis=-1, keepdims=True)
            cen = hval - mu
            var = jnp.mean(cen * cen, axis=-1, keepdims=True)
            y = o_ref[j, :, sl] * (cen * lax.rsqrt(var + LN_EPS) * nw_ref[:, sl])
            y_ref[j, :, sl] = y.astype(y_ref.dtype)


def _mlstm(q, k, v, o, gates, gates_t, hnorm_row, bsz, seq):
    nb = MLSTM_SEQS
    nc = seq // MLSTM_CHUNK
    assert bsz % nb == 0 and seq % MLSTM_CHUNK == 0, (bsz, seq)
    per_seq = lambda a: a.reshape(bsz, seq, a.shape[-1])
    blk = lambda width: pl.BlockSpec((nb, MLSTM_CHUNK, width), lambda g, c: (g, c, 0))
    gt_specs = [pl.BlockSpec((LANES, MLSTM_CHUNK), lambda g, c, j=j: (0, (g * nb + j) * nc + c))
                for j in range(nb)]
    y = pl.pallas_call(
        _mlstm_kernel,
        out_shape=jax.ShapeDtypeStruct((bsz, seq, MLSTM_W), MXU_DTYPE),
        grid=(bsz // nb, nc),
        in_specs=[blk(MLSTM_W)] * 4 + [blk(LANES)] + gt_specs + [_full((1, MLSTM_W))],
        out_specs=blk(MLSTM_W),
        scratch_shapes=[pltpu.VMEM((nb * MLSTM_HEADS, MLSTM_HEAD_DIM, 2 * MLSTM_HEAD_DIM), F32),
                        pltpu.VMEM((nb * MLSTM_HEADS, LANES), F32)],
        compiler_params=_params("arbitrary", "arbitrary"),
        name="mlstm_scan",
    )(per_seq(q), per_seq(k), per_seq(v), per_seq(o), per_seq(gates), *([gates_t] * nb), hnorm_row)
    return y.reshape(bsz * seq, MLSTM_W)


def _fox_kernel(q_ref, k_ref, vt_ref, gt_ref, y_ref, *scratch):
    acc_refs = scratch[:FOX_HEADS]
    qa_ref = scratch[FOX_HEADS]
    tq = q_ref.shape[0]
    tk = TK_FOX
    dh = FOX_HEAD_DIM
    qi = pl.program_id(1)
    q_t = q_ref[...].T
    bias_rows = jnp.where(lax.broadcasted_iota(jnp.int32, (LANES - dh, tq), 0) < 3, 1.0, 0.0)
    for hd in range(FOX_HEADS):
        qa_ref[hd] = jnp.concatenate([q_t[hd * dh:(hd + 1) * dh, :], bias_rows], axis=0).astype(MXU_DTYPE)
        acc_refs[hd][...] = jnp.zeros(acc_refs[hd].shape, F32)
    cq = gt_ref[G_FOX:G_FOX + FOX_HEADS, :] * LOG2E
    key_pos = lax.broadcasted_iota(jnp.int32, (tk, tq), 0)
    qry_pos = lax.broadcasted_iota(jnp.int32, (tk, tq), 1) + qi * tq
    n_full = qi * (tq // tk)

    def block(j, ms, masked):
        k0 = pl.multiple_of(j * tk, tk)
        out = []
        for hd in range(FOX_HEADS):
            hs = slice(hd * LANES, (hd + 1) * LANES)
            s = jnp.dot(k_ref[pl.ds(k0, tk), hs], qa_ref[hd], preferred_element_type=F32)
            if masked:
                s = jnp.where(key_pos + k0 <= qry_pos, s, -jnp.inf)
            cq_h = cq[hd:hd + 1, :]
            m_new = jnp.maximum(ms[hd], jnp.max(s, axis=0, keepdims=True) + cq_h)
            p = jnp.exp2(s - (m_new - cq_h))
            pv = jnp.dot(vt_ref[hd * FOX_VROWS:(hd + 1) * FOX_VROWS, pl.ds(k0, tk)], p.astype(MXU_DTYPE),
                         preferred_element_type=F32)
            acc_refs[hd][...] = jnp.exp2(ms[hd] - m_new) * acc_refs[hd][...] + pv[0:FOX_ACC_ROWS, :]
            out.append(m_new)
        return tuple(out)

    ms = tuple(jnp.full((1, tq), -jnp.inf, F32) for _ in range(FOX_HEADS))
    per_trip = tq // tk

    def past_blocks(t, ms):
        for d in range(per_trip):
            ms = block(t * per_trip + d, ms, masked=False)
        return ms

    ms = lax.fori_loop(0, qi, past_blocks, ms)
    for d in range(tq // tk):
        ms = block(n_full + d, ms, masked=True)
    outs = []
    for hd in range(FOX_HEADS):
        acc = acc_refs[hd][...]
        outs.append(acc[0:dh, :] / acc[dh:dh + 1, :])
    y_ref[...] = jnp.concatenate(outs, axis=0).T.astype(y_ref.dtype)


def _fox(fq, fk_aug, fvt_aug, gates_t, bsz, seq):
    tok = bsz * seq
    tq = TQ_FOX
    nq = seq // tq
    return pl.pallas_call(
        _fox_kernel,
        out_shape=jax.ShapeDtypeStruct((tok, FOX_W), MXU_DTYPE),
        grid=(bsz, nq),
        in_specs=[
            pl.BlockSpec((tq, FOX_W), lambda b, i: (b * nq + i, 0)),
            pl.BlockSpec((seq, FOX_AUG), lambda b, i: (b, 0)),
            pl.BlockSpec((FOX_HEADS * FOX_VROWS, seq), lambda b, i: (0, b)),
            pl.BlockSpec((LANES, tq), lambda b, i: (0, b * nq + i)),
        ],
        out_specs=pl.BlockSpec((tq, FOX_W), lambda b, i: (b * nq + i, 0)),
        scratch_shapes=[pltpu.VMEM((FOX_ACC_ROWS, tq), F32)] * FOX_HEADS
                       + [pltpu.VMEM((FOX_HEADS, LANES, tq), MXU_DTYPE)],
        compiler_params=_params("arbitrary", "arbitrary"),
        name="fox_attention",
    )(fq, fk_aug, fvt_aug, gates_t)


def _router_kernel(h_ref, w_ref, b_ref, stril_ref, id_ref, gate_ref, cnt_ref, carry):
    tm = h_ref.shape[0]

    @pl.when(pl.program_id(0) == 0)
    def _():
        carry[...] = jnp.zeros(carry.shape, F32)

    h = h_ref[...]
    h_hi = h.astype(MXU_DTYPE)
    h_lo = (h - h_hi.astype(F32)).astype(MXU_DTYPE)
    both = jnp.dot(h_hi, w_ref[...], preferred_element_type=F32)
    logits = (both[:, 0:LANES] + both[:, LANES:2 * LANES]
              + jnp.dot(h_lo, w_ref[:, 0:LANES], preferred_element_type=F32) + b_ref[...])
    lane = lax.broadcasted_iota(jnp.int32, (tm, LANES), 1).astype(F32)
    neg = -jnp.inf
    first = lambda hit: jnp.min(jnp.where(hit, lane, float(LANES)), axis=-1, keepdims=True)
    gl = jnp.where(lane < MOE_GROUPS, logits, neg)
    g_max = jnp.max(gl, axis=-1, keepdims=True)
    g_idx = first(gl == g_max)
    p_group = 1.0 / jnp.sum(jnp.exp(gl - g_max), axis=-1, keepdims=True)
    e_lo = MOE_GROUPS + g_idx * EXPERTS_PER_GROUP
    el = jnp.where((lane >= e_lo) & (lane < e_lo + EXPERTS_PER_GROUP), logits, neg)
    v1 = jnp.max(el, axis=-1, keepdims=True)
    i1 = first(el == v1)
    el2 = jnp.where(lane == i1, neg, el)
    v2 = jnp.max(el2, axis=-1, keepdims=True)
    i2 = first(el2 == v2)
    t = jnp.exp(v2 - v1)
    w1 = 1.0 / (1.0 + t)
    gate_ref[...] = jnp.where(lane == 0.0, p_group * w1, jnp.where(lane == 1.0, p_group * (t * w1), 0.0))

    hit1 = lane == i1
    hit2 = lane == i2
    sent = jnp.where(hit1 | hit2, 1.0, 0.0)
    before = jnp.dot(stril_ref[...], sent.astype(jnp.bfloat16), preferred_element_type=F32) + carry[0:1, :]
    r1 = jnp.sum(jnp.where(hit1, before, 0.0), axis=-1, keepdims=True)
    r2 = jnp.sum(jnp.where(hit2, before, 0.0), axis=-1, keepdims=True)
    total = before[tm - 1:tm, :] + sent[tm - 1:tm, :]
    carry[...] = jnp.broadcast_to(total, carry.shape)
    cnt_ref[...] = jnp.broadcast_to(total, cnt_ref.shape).astype(jnp.int32)
    ids = jnp.where(lane == 0.0, i1 - MOE_GROUPS, jnp.where(lane == 1.0, i2 - MOE_GROUPS,
                    jnp.where(lane == 2.0, r1, jnp.where(lane == 3.0, r2, 0.0))))
    id_ref[...] = ids.T[0:SUBLANES, :].astype(jnp.int32)


def _router(h, w_router, b_router):
    tok = h.shape[0]
    tm = TM_ROUTER
    row = lambda i: (i, 0)
    idx = np.arange(tm)
    stril = jnp.asarray((idx[:, None] > idx[None, :]).astype(np.float32), jnp.bfloat16)
    w_hi = w_router.astype(MXU_DTYPE)
    w_lo = (w_router - w_hi.astype(F32)).astype(MXU_DTYPE)
    return pl.pallas_call(
        _router_kernel,
        out_shape=(jax.ShapeDtypeStruct((SUBLANES, tok), jnp.int32), jax.ShapeDtypeStruct((tok, LANES), F32),
                   jax.ShapeDtypeStruct((SUBLANES, LANES), jnp.int32)),
        grid=(tok // tm,),
        in_specs=[pl.BlockSpec((tm, D_MODEL), row), _full((D_MODEL, 2 * LANES)), _full((1, LANES)),
                  _full((tm, tm))],
        out_specs=(pl.BlockSpec((SUBLANES, tm), lambda i: (0, i)), pl.BlockSpec((tm, LANES), row),
                   _full((SUBLANES, LANES))),
        scratch_shapes=[pltpu.VMEM((SUBLANES, LANES), F32)],
        compiler_params=_params("arbitrary"),
        name="moe_router",
    )(h, jnp.concatenate([w_hi, w_lo], axis=1), b_router, stril)


def _pad_pieces(n):
    return tuple(1 << b for b in reversed(range((n - 1).bit_length())))


def _dispatch_kernel(d0_ref, d1_ref, start_ref, cnt_ref, nused_ref, h_ref, xs_hbm, zeros, sem, pad_sem):
    tm = h_ref.shape[0]
    i = pl.program_id(0)
    base = i * tm

    def issue(r, c):
        t = base + r
        pltpu.make_async_copy(h_ref.at[r], xs_hbm.at[d0_ref[t]], sem).start(priority=0)
        pltpu.make_async_copy(h_ref.at[r], xs_hbm.at[d1_ref[t]], sem).start(priority=1)
        return c

    lax.fori_loop(0, tm, issue, 0, unroll=8)

    def pad_copies(e, fn):
        cnt = cnt_ref[e]
        n_pad = (MOE_ROWS - cnt % MOE_ROWS) % MOE_ROWS
        first = start_ref[e] + cnt
        for piece in _pad_pieces(MOE_ROWS):
            @pl.when((n_pad & piece) != 0)
            def _(piece=piece):
                off = first + (n_pad & ~(2 * piece - 1))
                fn(pltpu.make_async_copy(zeros.at[pl.ds(0, piece)], xs_hbm.at[pl.ds(off, piece)], pad_sem))

    def tail_copies(blk, fn):
        for part in range(MOE_ROWS // zeros.shape[0]):
            off = blk * MOE_ROWS + part * zeros.shape[0]
            fn(pltpu.make_async_copy(zeros, xs_hbm.at[pl.ds(off, zeros.shape[0])], pad_sem))

    @pl.when(i == 0)
    def _():
        zeros[...] = jnp.zeros(zeros.shape, F32)
        n_blocks = xs_hbm.shape[0] // MOE_ROWS

        def start(e, c):
            pad_copies(e, lambda cp: cp.start())
            return c

        def wait(e, c):
            pad_copies(e, lambda cp: cp.wait())
            return c

        def tail_start(blk, c):
            tail_copies(blk, lambda cp: cp.start())
            return c

        def tail_wait(blk, c):
            tail_copies(blk, lambda cp: cp.wait())
            return c

        lax.fori_loop(0, N_EXPERTS, start, 0)
        lax.fori_loop(nused_ref[0], n_blocks, tail_start, 0)
        lax.fori_loop(0, N_EXPERTS, wait, 0)
        lax.fori_loop(nused_ref[0], n_blocks, tail_wait, 0)

    for _ in range(TOP_K):
        pltpu.make_async_copy(h_ref, xs_hbm.at[pl.ds(0, tm)], sem).wait()


def _dispatch(h_tiles, dest0, dest1, pad_start, counts, n_used, n_slots):
    tok = h_tiles.shape[0]
    tm = TM_DISPATCH
    assert tok % tm == 0, (tok, tm)
    grid_spec = pltpu.PrefetchScalarGridSpec(
        num_scalar_prefetch=5,
        grid=(tok // tm,),
        in_specs=[pl.BlockSpec((tm,) + ROW_TILE, lambda i, *_: (i, 0, 0))],
        out_specs=pl.BlockSpec(memory_space=pl.ANY),
        scratch_shapes=[pltpu.VMEM((MOE_ROWS // 2,) + ROW_TILE, F32),
                        pltpu.SemaphoreType.DMA(()), pltpu.SemaphoreType.DMA(())],
    )
    return pl.pallas_call(
        _dispatch_kernel,
        out_shape=jax.ShapeDtypeStruct((n_slots,) + ROW_TILE, F32),
        grid_spec=grid_spec,
        compiler_params=_params("arbitrary"),
        name="moe_dispatch",
    )(dest0, dest1, pad_start, counts, n_used, h_tiles)


def _expert_kernel(be_ref, nused_ref, next_ref, slot_ref, x_ref, wg_hbm, wu_hbm, wd_hbm, out_ref,
                   wg_s, wu_s, wd_s, wg_buf, wu_buf, wd_buf, sem, *, layer):
    i = pl.program_id(0)

    def weight_copies(e, slot):
        w = layer * N_EXPERTS + e
        return (pltpu.make_async_copy(wg_hbm.at[w], wg_buf.at[slot], sem.at[slot]),
                pltpu.make_async_copy(wu_hbm.at[w], wu_buf.at[slot], sem.at[slot]),
                pltpu.make_async_copy(wd_hbm.at[w], wd_buf.at[slot], sem.at[slot]))

    @pl.when(i == 0)
    def _():
        for cp in weight_copies(be_ref[0], slot_ref[be_ref[0]]):
            cp.start()

    @pl.when(i < nused_ref[0])
    def _():
        @pl.when((i == 0) | (be_ref[i] != be_ref[jnp.maximum(i - 1, 0)]))
        def _():
            e = be_ref[i]
            slot = slot_ref[e]
            for cp in weight_copies(e, slot):
                cp.wait()

            @pl.when(next_ref[e] >= 0)
            def _():
                for cp in weight_copies(next_ref[e], 1 - slot):
                    cp.start()

            wg_s[...] = wg_buf[slot].astype(MXU_DTYPE)
            wu_s[...] = wu_buf[slot].astype(MXU_DTYPE)
            wd_s[...] = wd_buf[slot].astype(MXU_DTYPE)

        x = _tiles_to_rows(x_ref).astype(MXU_DTYPE)
        n = x.shape[0]
        hidden = []
        for c in range(0, D_EXPERT, FFN_COLS):
            gate = jnp.dot(x, wg_s[:, c:c + FFN_COLS], preferred_element_type=F32)
            up = jnp.dot(x, wu_s[:, c:c + FFN_COLS], preferred_element_type=F32)
            hidden.append((gate * _sigmoid(gate) * up).astype(MXU_DTYPE))
        hidden = jnp.concatenate(hidden, axis=1)
        for c in range(0, D_MODEL, FFN_COLS):
            out = jnp.dot(hidden, wd_s[:, c:c + FFN_COLS], preferred_element_type=F32)
            for s in range(FFN_COLS // LANES):
                out_ref[pl.ds(c // LANES + s, n, stride=ROW_TILE[0]), :] = out[:, s * LANES:(s + 1) * LANES]

    @pl.when(i >= nused_ref[0])
    def _():
        out_ref[...] = jnp.zeros(out_ref.shape, F32)


def _expert_ffn(xs_tiles, block_expert, n_used, counts, w_gate, w_up, w_down, layer):
    n_slots = xs_tiles.shape[0]
    n_blocks = block_expert.shape[0]
    rows = MOE_ROWS * ROW_TILE[0]
    experts = jnp.arange(N_EXPERTS, dtype=jnp.int32)
    nonempty = counts > 0
    later = (experts[None, :] > experts[:, None]) & nonempty[None, :]
    next_expert = jnp.min(jnp.where(later, experts[None, :], N_EXPERTS), axis=1)
    next_expert = jnp.where(next_expert < N_EXPERTS, next_expert, -1).astype(jnp.int32)
    buf_half = ((jnp.cumsum(nonempty.astype(jnp.int32)) - 1) % 2).astype(jnp.int32)
    blk = lambda i, be, nu, nx, sl: (jnp.maximum(jnp.minimum(i, nu[0] - 1), 0), 0)
    grid_spec = pltpu.PrefetchScalarGridSpec(
        num_scalar_prefetch=4,
        grid=(n_blocks,),
        in_specs=[
            pl.BlockSpec((rows, LANES), blk),
            pl.BlockSpec(memory_space=pl.ANY),
            pl.BlockSpec(memory_space=pl.ANY),
            pl.BlockSpec(memory_space=pl.ANY),
        ],
        out_specs=pl.BlockSpec((rows, LANES), lambda i, be, nu, nx, sl: (i, 0)),
        scratch_shapes=[pltpu.VMEM((D_MODEL, D_EXPERT), MXU_DTYPE), pltpu.VMEM((D_MODEL, D_EXPERT), MXU_DTYPE),
                        pltpu.VMEM((D_EXPERT, D_MODEL), MXU_DTYPE),
                        pltpu.VMEM((2, D_MODEL, D_EXPERT), F32), pltpu.VMEM((2, D_MODEL, D_EXPERT), F32),
                        pltpu.VMEM((2, D_EXPERT, D_MODEL), F32), pltpu.SemaphoreType.DMA((2,))],
    )
    out = pl.pallas_call(
        functools.partial(_expert_kernel, layer=layer),
        out_shape=jax.ShapeDtypeStruct((n_slots * ROW_TILE[0], LANES), F32),
        grid_spec=grid_spec,
        compiler_params=_params("arbitrary"),
        name="moe_experts",
    )(block_expert, n_used, next_expert, buf_half, xs_tiles.reshape(n_slots * ROW_TILE[0], LANES),
      w_gate, w_up, w_down)
    return out.reshape((n_slots,) + ROW_TILE)


def _combine_ln_kernel(d0_ref, d1_ref, rows_hbm, h_ref, gate_ref, g_ref, b_ref, o_ref, buf, sem):
    tm = h_ref.shape[0]
    i = pl.program_id(0)

    def gather(tile, p):
        base = tile * tm

        def issue(r, c):
            t = base + r
            dst = pl.ds(pl.multiple_of(r * ROW_TILE[0], ROW_TILE[0]), ROW_TILE[0])
            pltpu.make_async_copy(rows_hbm.at[d0_ref[t]], buf.at[p, 0, dst],
                                  sem.at[p]).start(priority=0)
            pltpu.make_async_copy(rows_hbm.at[d1_ref[t]], buf.at[p, 1, dst],
                                  sem.at[p]).start(priority=1)
            return c

        lax.fori_loop(0, tm, issue, 0, unroll=8)

    @pl.when(i == 0)
    def _():
        gather(0, 0)

    last = pl.num_programs(0) - 1

    def wait_rows(p):
        for k in range(TOP_K):
            pltpu.make_async_copy(buf.at[1 - p, k], buf.at[p, k], sem.at[p]).wait()

    def gather_inline(tile, p):
        base = tile * tm
        for r in range(tm):
            dst = pl.ds(r * ROW_TILE[0], ROW_TILE[0])
            pltpu.make_async_copy(rows_hbm.at[d0_ref[base + r]], buf.at[p, 0, dst], sem.at[p]).start(priority=0)
            pltpu.make_async_copy(rows_hbm.at[d1_ref[base + r]], buf.at[p, 1, dst], sem.at[p]).start(priority=1)

    for p in range(2):
        @pl.when(i % 2 == p)
        def _(p=p):
            wait_rows(p)
            gather_inline(jnp.minimum(i + 1, last), 1 - p)
            gate = gate_ref[...]
            acc = (ALPHA * h_ref[...] + gate[:, 0:1] * _tiles_to_rows(buf.at[p, 0])
                   + gate[:, 1:2] * _tiles_to_rows(buf.at[p, 1]))
            o_ref[...] = _layer_norm_rows(acc, g_ref[...], b_ref[...])

            @pl.when(i == last)
            def _():
                wait_rows(1 - p)


def _combine_ln(dest0, dest1, rows_tiles, h, gates, g_row, b_row):
    tok = h.shape[0]
    tm = TM_COMBINE
    row = lambda i, *_: (i, 0)
    const = lambda i, *_: (0, 0)
    grid_spec = pltpu.PrefetchScalarGridSpec(
        num_scalar_prefetch=2,
        grid=(tok // tm,),
        in_specs=[
            pl.BlockSpec(memory_space=pl.ANY),
            pl.BlockSpec((tm, D_MODEL), row),
            pl.BlockSpec((tm, LANES), row),
            pl.BlockSpec((1, D_MODEL), const),
            pl.BlockSpec((1, D_MODEL), const),
        ],
        out_specs=pl.BlockSpec((tm, D_MODEL), row),
        scratch_shapes=[pltpu.VMEM((2, TOP_K, tm * ROW_TILE[0], LANES), F32), pltpu.SemaphoreType.DMA((2,))],
    )
    return pl.pallas_call(
        _combine_ln_kernel,
        out_shape=jax.ShapeDtypeStruct((tok, D_MODEL), F32),
        grid_spec=grid_spec,
        compiler_params=_params("arbitrary"),
        name="moe_combine_ln",
    )(dest0, dest1, rows_tiles, h, gates, g_row, b_row)


def _row_slots_kernel(start_ref, ids_ref, out_ref):
    expert = ids_ref[0:TOP_K, :]
    start = jnp.zeros(expert.shape, jnp.int32)
    for e in range(N_EXPERTS):
        start = jnp.where(expert == e, start_ref[e], start)
    out_ref[...] = jnp.zeros(out_ref.shape, jnp.int32)
    out_ref[0:TOP_K, :] = start + ids_ref[TOP_K:2 * TOP_K, :]


def _row_slots(ids, pad_start):
    grid_spec = pltpu.PrefetchScalarGridSpec(
        num_scalar_prefetch=1,
        grid=(1,),
        in_specs=[pl.BlockSpec(ids.shape, lambda i, s: (0, 0))],
        out_specs=pl.BlockSpec(ids.shape, lambda i, s: (0, 0)),
    )
    return pl.pallas_call(
        _row_slots_kernel,
        out_shape=jax.ShapeDtypeStruct(ids.shape, jnp.int32),
        grid_spec=grid_spec,
        name="moe_row_slots",
    )(pad_start, ids)


def _slot_layout(counts, n_blocks):
    padded = (counts + MOE_ROWS - 1) // MOE_ROWS * MOE_ROWS
    pad_end = jnp.cumsum(padded)
    n_used = pad_end[-1:] // MOE_ROWS
    blocks = jnp.arange(n_blocks, dtype=jnp.int32)
    first_row = jnp.minimum(blocks, n_used - 1) * MOE_ROWS
    block_expert = jnp.sum(first_row[:, None] >= pad_end[None, :], axis=1)
    return (pad_end - padded).astype(jnp.int32), block_expert.astype(jnp.int32), n_used.astype(jnp.int32)


def _moe_ln(h, h_tiles, w_router, b_router, w_gate, w_up, w_down, layer, g_row, b_row):
    tok = h.shape[0]
    n_blocks = tok * TOP_K // MOE_ROWS + N_EXPERTS
    ids, gates, counts = _router(h, w_router, b_router)
    counts = counts[0, MOE_GROUPS:MOE_GROUPS + N_EXPERTS]
    pad_start, block_expert, n_used = _slot_layout(counts, n_blocks)
    slots = _row_slots(ids, pad_start)
    dest0, dest1 = slots[0], slots[1]
    xs = _dispatch(h_tiles.reshape((tok,) + ROW_TILE), dest0, dest1, pad_start, counts, n_used,
                   n_blocks * MOE_ROWS)
    rows = _expert_ffn(xs, block_expert, n_used, counts, w_gate, w_up, w_down, layer)
    return _combine_ln(dest0, dest1, rows, h, gates, g_row, b_row)


def _pad_lanes(v):
    return jnp.pad(v, (0, LANES - v.shape[0])).reshape(1, LANES)


def _even_mixer(h, bsz, seq, w_in, conv_a, conv_w, conv_b, dt_bias, a_log, d_skip, norm_w, w_out, g_row, b_row):
    w = jnp.pad(w_in, ((0, 0), (0, AB_PROJ - w_in.shape[1]))).astype(MXU_DTYPE)
    y_a, z, xbc, dt, acs, acst = _even_front(
        h, w, conv_a, conv_w, conv_b.reshape(1, -1), _pad_lanes(dt_bias), _pad_lanes(-jnp.exp(a_log)), bsz, seq)
    dskip_row = jnp.repeat(d_skip, SSM_HEAD_DIM).reshape(1, -1)
    y_b = _ssd(xbc, dt, acs, acst, z, dskip_row, norm_w.reshape(1, -1), bsz, seq)
    return _outproj_ln([y_a, y_b], h, w_out.astype(MXU_DTYPE), g_row, b_row)


def _odd_mixer(h, bsz, seq, w_in, i_bias, f_bias, hnorm_w, fox_f_bias, w_out, g_row, b_row):
    c = np.cumsum((0, MLSTM_W, MLSTM_W, MLSTM_W, MLSTM_HEADS, MLSTM_HEADS, MLSTM_W, FOX_W, FOX_W, FOX_W, FOX_HEADS))
    part = lambda j: w_in[:, c[j]:c[j + 1]]
    q, k, v, i_pre, f_pre, o_pre, fq, fk, fv, ff = (part(j) for j in range(10))
    gate_cols = jnp.concatenate([i_pre, f_pre, ff, f_pre], axis=1)
    gate_cols = jnp.pad(gate_cols, ((0, 0), (0, LANES - gate_cols.shape[1])))
    spread = lambda m: jnp.pad(m.reshape(-1, FOX_HEADS, FOX_HEAD_DIM),
                               ((0, 0), (0, 0), (0, LANES - FOX_HEAD_DIM))).reshape(-1, FOX_AUG)
    w = jnp.concatenate([q, k, v, o_pre, fq, spread(fk), fv, gate_cols], axis=1).astype(MXU_DTYPE)
    gate_bias = _pad_lanes(jnp.concatenate([i_bias, f_bias, fox_f_bias, f_bias]))
    q, k, v, o, fq, fk_aug, fvt_aug, gates, gates_t = _odd_front(h, w, gate_bias, bsz, seq)
    y_c = _mlstm(q, k, v, o, gates, gates_t, hnorm_w.reshape(1, -1), bsz, seq)
    y_d = _fox(fq, fk_aug, fvt_aug, gates_t, bsz, seq)
    return _outproj_ln([y_c, y_d], h, w_out.astype(MXU_DTYPE), g_row, b_row)


def kernel(x, ab_w_in, ab_conv_a, ab_conv_ssm_w, ab_conv_ssm_b, ab_dt_bias, ab_a_log, ab_d_skip, ab_norm_w, ab_w_out, cd_w_in, cd_i_bias, cd_f_bias, cd_hnorm_w, cd_fox_f_bias, cd_w_out, ln1_g, ln1_b, ln2_g, ln2_b, moe_rg_w, moe_rg_b, moe_re_w, moe_re_b, moe_w_gate, moe_w_up, moe_w_down):
    bsz, seq, d = x.shape
    h = x.reshape(bsz * seq, d)
    stack = lambda w: w.reshape((w.shape[0] * w.shape[1],) + w.shape[2:])
    w_gate, w_up, w_down = stack(moe_w_gate), stack(moe_w_up), stack(moe_w_down)
    for layer in range(DEPTH):
        j = layer // 2
        g1, b1 = ln1_g[layer].reshape(1, -1), ln1_b[layer].reshape(1, -1)
        if layer % 2 == 0:
            h, h_tiles = _even_mixer(h, bsz, seq, ab_w_in[j], ab_conv_a[j], ab_conv_ssm_w[j], ab_conv_ssm_b[j],
                                     ab_dt_bias[j], ab_a_log[j], ab_d_skip[j], ab_norm_w[j], ab_w_out[j], g1, b1)
        else:
            h, h_tiles = _odd_mixer(h, bsz, seq, cd_w_in[j], cd_i_bias[j], cd_f_bias[j], cd_hnorm_w[j],
                                    cd_fox_f_bias[j], cd_w_out[j], g1, b1)
        re_w = jnp.transpose(moe_re_w[layer], (1, 0, 2)).reshape(d, N_EXPERTS)
        w_router = jnp.pad(jnp.concatenate([moe_rg_w[layer], re_w], axis=1),
                           ((0, 0), (0, LANES - MOE_GROUPS - N_EXPERTS)))
        b_router = _pad_lanes(jnp.concatenate([moe_rg_b[layer], moe_re_b[layer].reshape(-1)]))
        h = _moe_ln(h, h_tiles, w_router, b_router, w_gate, w_up, w_down, layer,
                    ln2_g[layer].reshape(1, -1), ln2_b[layer].reshape(1, -1))
    return h.reshape(bsz, seq, d)
```
